```python
import jax, jax.numpy as jnp
from jax import lax
import numpy as np

D_MODEL = 1024
BATCH = 4
SEQ = 8192
DEPTH = 1
DEC_BATCH = 16
DEC_SEQ = 32
PAST_LEN = 1024

CHUNK = 64
RET_WIDTH = D_MODEL // 2
POOL_WIDTH = D_MODEL - RET_WIDTH
RET_HEADS = 4
RET_HEAD_DIM = RET_WIDTH // RET_HEADS
POOL_WINDOWS = (2, 4, 8, 16)
POOL_GROUPS = len(POOL_WINDOWS)
POOL_GROUP_WIDTH = POOL_WIDTH // POOL_GROUPS
POOL_HIST = max(POOL_WINDOWS) - 1
IN_PROJ_WIDTH = 4 * RET_WIDTH + POOL_WIDTH
N_EXPERT_GROUPS = 4
EXPERTS_PER_GROUP = 8
N_EXPERTS = N_EXPERT_GROUPS * EXPERTS_PER_GROUP
TOP_K_IN_GROUP = 2
EXPERT_HIDDEN = D_MODEL // 2
ROPE_BASE = 10000.0
EPS = 1e-6

kernel_name = "hymba_retention_pool_hmoe_stream_step"

F32 = jnp.float32


def _rmsnorm(x, g):
    xf = x.astype(F32)
    y = xf * lax.rsqrt(jnp.mean(xf * xf, axis=-1, keepdims=True) + EPS)
    return (y * g.astype(F32)).astype(x.dtype)


def _rope(x, pos):
    half = x.shape[-1] // 2
    inv = ROPE_BASE ** (-jnp.arange(half, dtype=F32) / half)
    ang = pos.astype(F32)[:, None] * inv[None, :]
    cos = jnp.cos(ang)[None, :, None, :]
    sin = jnp.sin(ang)[None, :, None, :]
    xf = x.astype(F32)
    x1, x2 = xf[..., :half], xf[..., half:]
    return jnp.concatenate([x1 * cos - x2 * sin, x2 * cos + x1 * sin], axis=-1).astype(x.dtype)


def _retention(q, k, v, s0):
    B, L, H, _ = q.shape
    dv = v.shape[-1]
    C = min(CHUNK, L)
    n = L // C
    lg = jnp.log1p(-jnp.exp2(-5.0 - jnp.arange(H, dtype=F32)))
    idx = jnp.arange(C, dtype=F32)
    diff = idx[:, None] - idx[None, :]
    d_intra = jnp.where(diff[None] >= 0, jnp.exp(jnp.maximum(diff, 0.0)[None] * lg[:, None, None]), 0.0)
    d_q = jnp.exp((idx + 1.0)[None, :] * lg[:, None])[:, :, None]
    d_k = jnp.exp((C - 1.0 - idx)[None, :] * lg[:, None])[:, :, None]
    d_c = jnp.exp(C * lg)[:, None, None]

    def blocks(t):
        return t.astype(F32).reshape(B, n, C, H, t.shape[-1]).transpose(1, 0, 3, 2, 4)

    def step(S, blk):
        qc, kc, vc = blk
        sc = jnp.einsum('bhqd,bhkd->bhqk', qc, kc) * d_intra
        o = jnp.einsum('bhqk,bhke->bhqe', sc, vc) + jnp.einsum('bhqd,bhde->bhqe', qc * d_q, S)
        S = d_c * S + jnp.einsum('bhkd,bhke->bhde', kc * d_k, vc)
        return S, o

    S, o = lax.scan(step, s0.astype(F32), (blocks(q), blocks(k), blocks(v)))
    o = o.transpose(1, 0, 3, 2, 4).reshape(B, L, H, dv)
    return o, S


def _head_groupnorm(o, g):
    mu = jnp.mean(o, axis=-1, keepdims=True)
    oc = o - mu
    var = jnp.mean(oc * oc, axis=-1, keepdims=True)
    y = oc * lax.rsqrt(var + EPS)
    B, L = o.shape[:2]
    return y.reshape(B, L, -1) * g.astype(F32)


def _pool_mix(u, hist, pos0, w_pool, pool_scale):
    B, L, P = u.shape
    ue = jnp.concatenate([hist.astype(u.dtype), u], axis=1)
    cs = jnp.cumsum(ue.astype(F32), axis=1)
    cs = jnp.concatenate([jnp.zeros((B, 1, P), F32), cs], axis=1)
    end = cs[:, POOL_HIST + 1:POOL_HIST + 1 + L]
    pos = pos0 + jnp.arange(L)
    means = []
    for gi, w in enumerate(POOL_WINDOWS):
        sl = slice(gi * POOL_GROUP_WIDTH, (gi + 1) * POOL_GROUP_WIDTH)
        start = cs[:, POOL_HIST + 1 - w:POOL_HIST + 1 - w + L, sl]
        cnt = jnp.minimum(pos + 1, w).astype(F32)[None, :, None]
        means.append((end[..., sl] - start) / cnt)
    p = (jnp.concatenate(means, axis=-1) - u.astype(F32)).astype(u.dtype)
    p = p.reshape(B, L, POOL_GROUPS, POOL_GROUP_WIDTH)
    z = jnp.einsum('blgc,gcd->blgd', p, w_pool).reshape(B, L, P) * pool_scale
    return z, ue[:, -POOL_HIST:]


def _hier_moe(h, w_rg, w_re, w_g, w_u, w_d):
    B, L, D = h.shape
    t = h.reshape(B * L, D)
    gl = (t @ w_rg).astype(F32)
    gp = jax.nn.softmax(gl, axis=-1)
    gi = jnp.argmax(gl, axis=-1)
    p_sel = jnp.take_along_axis(gp, gi[:, None], axis=-1)
    el = (t @ w_re).astype(F32).reshape(-1, N_EXPERT_GROUPS, EXPERTS_PER_GROUP)
    el_sel = jnp.take_along_axis(el, gi[:, None, None], axis=1)[:, 0]
    tv, ti = lax.top_k(el_sel, TOP_K_IN_GROUP)
    wts = p_sel * jax.nn.softmax(tv, axis=-1)
    ids = gi[:, None] * EXPERTS_PER_GROUP + ti
    gate = jnp.einsum('tk,tke->te', wts, jax.nn.one_hot(ids, N_EXPERTS, dtype=F32)).astype(h.dtype)
    out = jnp.zeros_like(t)
    for e in range(N_EXPERTS):
        he = jax.nn.silu(t @ w_g[e]) * (t @ w_u[e])
        out = out + gate[:, e:e + 1] * (he @ w_d[e])
    return out.reshape(B, L, D)


def _layer(x, ret_state, pool_hist, pos0, norm1_g, w_in, ret_norm_g, w_pool, pool_scale, w_out,
           norm2_g, w_rg, w_re, w_g, w_u, w_d):
    B, L, _ = x.shape
    h = _rmsnorm(x, norm1_g)
    proj = h @ w_in
    q = proj[..., 0:RET_WIDTH].reshape(B, L, RET_HEADS, RET_HEAD_DIM)
    k = proj[..., RET_WIDTH:2 * RET_WIDTH].reshape(B, L, RET_HEADS, RET_HEAD_DIM)
    v = proj[..., 2 * RET_WIDTH:3 * RET_WIDTH].reshape(B, L, RET_HEADS, RET_HEAD_DIM)
    g = proj[..., 3 * RET_WIDTH:4 * RET_WIDTH]
    u = proj[..., 4 * RET_WIDTH:]
    pos = pos0 + jnp.arange(L)
    q = _rope(q, pos)
    k = _rope(k, pos) * (RET_HEAD_DIM ** -0.5)
    o, new_s = _retention(q, k, v, ret_state)
    ret_out = (jax.nn.silu(g.astype(F32)) * _head_groupnorm(o, ret_norm_g)).astype(x.dtype)
    pool_out, new_hist = _pool_mix(u, pool_hist, pos0, w_pool, pool_scale)
    x = x + jnp.concatenate([ret_out, pool_out.astype(x.dtype)], axis=-1) @ w_out
    x = x + _hier_moe(_rmsnorm(x, norm2_g), w_rg, w_re, w_g, w_u, w_d)
    return x, new_s, new_hist


def setup_inputs(seed: int = 0) -> dict:
    key = jax.random.key(seed)
    ks = jax.random.split(key, 20)
    nrm = jax.random.normal
    return {
        "x_prompt": nrm(ks[0], (BATCH, SEQ, D_MODEL), F32),
        "x_sample": nrm(ks[1], (DEC_BATCH, DEC_SEQ, D_MODEL), F32),
        "state_ret": 0.5 * nrm(ks[2], (DEPTH, DEC_BATCH, RET_HEADS, RET_HEAD_DIM, RET_HEAD_DIM), F32),
        "cache_pool": nrm(ks[3], (DEPTH, DEC_BATCH, POOL_HIST, POOL_WIDTH), F32),
        "norm1_g": 1.0 + 0.05 * nrm(ks[4], (DEPTH, D_MODEL), F32),
        "w_in": nrm(ks[5], (DEPTH, D_MODEL, IN_PROJ_WIDTH), F32) * D_MODEL ** -0.5,
        "ret_norm_g": 1.0 + 0.05 * nrm(ks[6], (DEPTH, RET_WIDTH), F32),
        "w_pool": nrm(ks[7], (DEPTH, POOL_GROUPS, POOL_GROUP_WIDTH, POOL_GROUP_WIDTH), F32) * POOL_GROUP_WIDTH ** -0.5,
        "pool_scale": 1.0 + 0.05 * nrm(ks[8], (DEPTH, POOL_WIDTH), F32),
        "w_out": nrm(ks[9], (DEPTH, D_MODEL, D_MODEL), F32) * D_MODEL ** -0.5,
        "norm2_g": 1.0 + 0.05 * nrm(ks[10], (DEPTH, D_MODEL), F32),
        "w_router_group": nrm(ks[11], (DEPTH, D_MODEL, N_EXPERT_GROUPS), F32) * D_MODEL ** -0.5,
        "w_router_expert": nrm(ks[12], (DEPTH, D_MODEL, N_EXPERTS), F32) * D_MODEL ** -0.5,
        "w_exp_gate": nrm(ks[13], (DEPTH, N_EXPERTS, D_MODEL, EXPERT_HIDDEN), F32) * D_MODEL ** -0.5,
        "w_exp_up": nrm(ks[14], (DEPTH, N_EXPERTS, D_MODEL, EXPERT_HIDDEN), F32) * D_MODEL ** -0.5,
        "w_exp_down": nrm(ks[15], (DEPTH, N_EXPERTS, EXPERT_HIDDEN, D_MODEL), F32) * EXPERT_HIDDEN ** -0.5,
        "final_norm_g": 1.0 + 0.05 * nrm(ks[16], (D_MODEL,), F32),
    }


def reference(x_prompt, x_sample, state_ret, cache_pool, norm1_g, w_in, ret_norm_g, w_pool, pool_scale,
              w_out, norm2_g, w_router_group, w_router_expert, w_exp_gate, w_exp_up, w_exp_down,
              final_norm_g):
    bp = x_prompt.shape[0]
    yp, ys = x_prompt, x_sample
    s0p = jnp.zeros((bp, RET_HEADS, RET_HEAD_DIM, RET_HEAD_DIM), F32)
    h0p = jnp.zeros((bp, POOL_HIST, POOL_WIDTH), x_prompt.dtype)
    rp, hp, rs, hs = [], [], [], []
    for l in range(DEPTH):
        params = (norm1_g[l], w_in[l], ret_norm_g[l], w_pool[l], pool_scale[l], w_out[l], norm2_g[l],
                  w_router_group[l], w_router_expert[l], w_exp_gate[l], w_exp_up[l], w_exp_down[l])
        yp, s_p, h_p = _layer(yp, s0p, h0p, 0, *params)
        ys, s_s, h_s = _layer(ys, state_ret[l], cache_pool[l], PAST_LEN, *params)
        rp.append(s_p.astype(x_prompt.dtype))
        hp.append(h_p)
        rs.append(s_s.astype(state_ret.dtype))
        hs.append(h_s.astype(cache_pool.dtype))
    y_prompt = _rmsnorm(yp, final_norm_g)
    y_sample = _rmsnorm(ys, final_norm_g)
    return (y_prompt, y_sample, jnp.stack(rp), jnp.stack(hp), jnp.stack(rs), jnp.stack(hs))
```

```python
import functools

import jax
import jax.numpy as jnp
from jax import lax
from jax.experimental import pallas as pl
from jax.experimental.pallas import tpu as pltpu
from jax.experimental.pallas import tpu_sc as plsc

F32 = jnp.float32
BF16 = jnp.bfloat16
I32 = jnp.int32

EPS = 1e-6
ROPE_BASE = 10000.0
RET_HEADS = 4
POOL_WINDOWS = (2, 4, 8, 16)
POOL_HIST = max(POOL_WINDOWS) - 1
N_EXPERT_GROUPS = 4
EXPERTS_PER_GROUP = 8
N_EXPERTS = N_EXPERT_GROUPS * EXPERTS_PER_GROUP
EXPERT_SHIFT = EXPERTS_PER_GROUP.bit_length() - 1
PAST_LEN = 1024

LANES = 128
HIST_ROWS = 16
MOE_TILE = 256
SC_UNIT = 16
VMEM_LIMIT = 56 * 1024 * 1024


def _rms(x, g):
    return x * lax.rsqrt(jnp.mean(x * x, axis=-1, keepdims=True) + EPS) * g


def _sigmoid(x):
    return 1.0 / (1.0 + jnp.exp(-x))


def _layer_kernel(dc_ref, x_ref, s0_ref, h0_ref, cos_ref, sin_ref, dintra_ref, dq_ref, dk_ref,
                  g1_ref, win_ref, gret_ref, wpool_ref, pscale_ref, wout_ref, g2_ref,
                  wrh_ref, wrl_ref, tri_ref,
                  x1_ref, h2_ref, ri_ref, rw_ref, st_ref, hist_ref, cnt_ref,
                  ue_ref, q_ref, k_ref, v_ref, o_ref, a_ref,
                  *, bb, tl, chunk, pos0):
    b_idx = pl.program_id(0)
    l_idx = pl.program_id(1)
    rows = bb * tl
    d_model = x_ref.shape[-1]
    rw_width = q_ref.shape[-1]
    dh = rw_width // RET_HEADS
    pw = ue_ref.shape[-1]
    gw = pw // len(POOL_WINDOWS)
    n_chunks = tl // chunk

    @pl.when(l_idx == 0)
    def _():
        st_ref[...] = s0_ref[...]
        ue_ref[:, 0:HIST_ROWS, :] = h0_ref[...]

    @pl.when((l_idx == 0) & (b_idx == 0))
    def _():
        cnt_ref[...] = jnp.zeros_like(cnt_ref)

    x = x_ref[...].reshape(rows, d_model)
    h = _rms(x, g1_ref[...])
    proj = jnp.dot(h.astype(BF16), win_ref[...], preferred_element_type=F32)

    cosf = cos_ref[...][None]
    sinf = sin_ref[...][None]
    k_scale = dh ** -0.5
    for hh in range(RET_HEADS):
        qh = proj[:, hh * dh:(hh + 1) * dh]
        kh = proj[:, rw_width + hh * dh:rw_width + (hh + 1) * dh]
        qr = (qh.reshape(bb, tl, dh) * cosf
              + pltpu.roll(qh, dh // 2, 1).reshape(bb, tl, dh) * sinf).reshape(rows, dh)
        kr = (kh.reshape(bb, tl, dh) * cosf
              + pltpu.roll(kh, dh // 2, 1).reshape(bb, tl, dh) * sinf).reshape(rows, dh)
        q_ref[:, hh * dh:(hh + 1) * dh] = qr.astype(BF16)
        k_ref[:, hh * dh:(hh + 1) * dh] = kr * k_scale
    v_ref[...] = proj[:, 2 * rw_width:3 * rw_width].astype(BF16)

    def ret_block(b, c):
        r0 = b * tl + c * chunk
        if not isinstance(r0, int):
            r0 = pl.multiple_of(r0, chunk)
        for hh in range(RET_HEADS):
            cs = slice(hh * dh, (hh + 1) * dh)
            qc = q_ref[pl.ds(r0, chunk), cs]
            kf = k_ref[pl.ds(r0, chunk), cs]
            vc = v_ref[pl.ds(r0, chunk), cs]
            s_old = st_ref[b, hh]
            sc = lax.dot_general(qc, kf.astype(BF16), (((1,), (1,)), ((), ())),
                                 preferred_element_type=F32) * dintra_ref[hh]
            o = (jnp.dot(sc.astype(BF16), vc, preferred_element_type=F32)
                 + dq_ref[hh] * jnp.dot(qc, s_old.astype(BF16), preferred_element_type=F32))
            kd = (kf * dk_ref[hh]).astype(BF16)
            s_new = dc_ref[hh] * s_old + lax.dot_general(
                kd, vc, (((0,), (0,)), ((), ())), preferred_element_type=F32)
            st_ref[b, hh] = s_new
            o_ref[pl.ds(r0, chunk), cs] = o

    if bb * n_chunks <= 4:
        for b in range(bb):
            for c in range(n_chunks):
                ret_block(b, c)
    else:
        def body(i, carry):
            ret_block(i // n_chunks, i % n_chunks)
            return carry
        lax.fori_loop(0, bb * n_chunks, body, 0)

    for hh in range(RET_HEADS):
        cs = slice(hh * dh, (hh + 1) * dh)
        oh = o_ref[:, cs]
        mu = jnp.mean(oh, axis=-1, keepdims=True)
        oc = oh - mu
        var = jnp.mean(oc * oc, axis=-1, keepdims=True)
        y = oc * lax.rsqrt(var + EPS) * gret_ref[:, cs]
        g = proj[:, 3 * rw_width + hh * dh:3 * rw_width + (hh + 1) * dh]
        a_ref[:, cs] = (g * _sigmoid(g) * y).astype(BF16)

    u = proj[:, 4 * rw_width:4 * rw_width + pw]
    ue_ref[:, HIST_ROWS:HIST_ROWS + tl, :] = u.reshape(bb, tl, pw)
    pos = pos0 + l_idx * tl + lax.broadcasted_iota(I32, (1, tl, 1), 1)
    for gi, w in enumerate(POOL_WINDOWS):
        cs = slice(gi * gw, (gi + 1) * gw)
        acc = ue_ref[:, HIST_ROWS:HIST_ROWS + tl, cs]
        for j in range(1, w):
            acc = acc + ue_ref[:, HIST_ROWS - j:HIST_ROWS - j + tl, cs]
        inv_cnt = 1.0 / jnp.minimum(pos + 1, w).astype(F32)
        p = (acc * inv_cnt).reshape(rows, gw) - u[:, cs]
        z = jnp.dot(p.astype(BF16), wpool_ref[gi], preferred_element_type=F32) * pscale_ref[:, cs]
        a_ref[:, rw_width + gi * gw:rw_width + (gi + 1) * gw] = z.astype(BF16)
    tail = ue_ref[:, tl:tl + HIST_ROWS, :]
    ue_ref[:, 0:HIST_ROWS, :] = tail
    hist_ref[...] = tail

    x1 = x + jnp.dot(a_ref[...], wout_ref[...], preferred_element_type=F32)
    x1_ref[...] = x1.reshape(bb, tl, d_model)
    h2 = _rms(x1, g2_ref[...])
    h2_ref[...] = h2.reshape(bb, tl, d_model)

    h2_hi = h2.astype(BF16)
    h2_lo = (h2 - h2_hi.astype(F32)).astype(BF16)
    logits = (jnp.dot(h2_hi, wrh_ref[...], preferred_element_type=F32)
              + jnp.dot(h2_lo, wrh_ref[...], preferred_element_type=F32)
              + jnp.dot(h2_hi, wrl_ref[...], preferred_element_type=F32))
    lane = lax.broadcasted_iota(I32, (rows, LANES), 1)
    lane_f = lane.astype(F32)
    neg = jnp.float32(-jnp.inf)
    big = jnp.float32(1e9)
    is_grp = (lane >= N_EXPERTS) & (lane < N_EXPERTS + N_EXPERT_GROUPS)
    gl = jnp.where(is_grp, logits, neg)
    gmax = jnp.max(gl, axis=-1, keepdims=True)
    gidx = jnp.min(jnp.where(gl == gmax, lane_f - N_EXPERTS, big), axis=-1, keepdims=True)
    p_sel = 1.0 / jnp.sum(jnp.exp(gl - gmax), axis=-1, keepdims=True)
    in_grp = (lane < N_EXPERTS) & (lax.shift_right_logical(lane, EXPERT_SHIFT).astype(F32) == gidx)
    el = jnp.where(in_grp, logits, neg)
    m1 = jnp.max(el, axis=-1, keepdims=True)
    i1 = jnp.min(jnp.where(el == m1, lane_f, big), axis=-1, keepdims=True)
    el2 = jnp.where(lane_f == i1, neg, el)
    m2 = jnp.max(el2, axis=-1, keepdims=True)
    i2 = jnp.min(jnp.where(el2 == m2, lane_f, big), axis=-1, keepdims=True)
    e2 = jnp.exp(m2 - m1)
    w1 = p_sel / (1.0 + e2)
    w2 = p_sel * e2 / (1.0 + e2)

    hit1 = lane_f == i1
    hit2 = lane_f == i2
    onehot = (hit1 | hit2).astype(BF16)
    before = jnp.dot(tri_ref[...], onehot, preferred_element_type=F32) + cnt_ref[...]
    r1 = jnp.sum(jnp.where(hit1, before, 0.0), axis=-1, keepdims=True)
    r2 = jnp.sum(jnp.where(hit2, before, 0.0), axis=-1, keepdims=True)
    cnt_ref[...] = cnt_ref[...] + jnp.sum(onehot.astype(F32), axis=0, keepdims=True)

    ri = jnp.where(lane == 0, i1, jnp.where(lane == 1, i2, jnp.where(lane == 2, r1, jnp.where(lane == 3, r2, 0.0))))
    ri_ref[...] = ri.astype(I32).reshape(bb, tl, LANES)
    rw_ref[...] = jnp.where(lane == 0, w1, jnp.where(lane == 1, w2, 0.0)).reshape(bb, tl, LANES)


def _layer_call(x, s0, h0, pos0, consts, *, bb, tl, chunk):
    bsz, seq, d_model = x.shape
    rows = bb * tl
    rw_width = consts["gret"].shape[-1]
    pw = consts["pscale"].shape[-1]
    dh = rw_width // RET_HEADS

    half = dh // 2
    inv = ROPE_BASE ** (-jnp.arange(half, dtype=F32) / half)
    ang = (pos0 + jnp.arange(seq)).astype(F32)[:, None] * inv[None, :]
    cos, sin = jnp.cos(ang), jnp.sin(ang)
    cosf = jnp.concatenate([cos, cos], axis=-1)
    sinf = jnp.concatenate([-sin, sin], axis=-1)

    lg = jnp.log1p(-jnp.exp2(-5.0 - jnp.arange(RET_HEADS, dtype=F32)))
    idx = jnp.arange(chunk, dtype=F32)
    diff = idx[:, None] - idx[None, :]
    d_intra = jnp.where(diff[None] >= 0, jnp.exp(jnp.maximum(diff, 0.0)[None] * lg[:, None, None]), 0.0)
    d_q = jnp.broadcast_to(jnp.exp((idx + 1.0)[None, :] * lg[:, None])[:, :, None], (RET_HEADS, chunk, dh))
    d_k = jnp.broadcast_to(jnp.exp((chunk - 1.0 - idx)[None, :] * lg[:, None])[:, :, None], (RET_HEADS, chunk, dh))
    d_c = jnp.exp(chunk * lg)
    tri = jnp.tril(jnp.ones((rows, rows), BF16), -1)

    const2 = lambda b, l, *_: (0, 0)
    const3 = lambda b, l, *_: (0, 0, 0)
    grid_spec = pltpu.PrefetchScalarGridSpec(
        num_scalar_prefetch=0,
        grid=(bsz // bb, seq // tl),
        in_specs=[
            pl.BlockSpec(memory_space=pltpu.SMEM),
            pl.BlockSpec((bb, tl, d_model), lambda b, l: (b, l, 0)),
            pl.BlockSpec((bb, RET_HEADS, dh, dh), lambda b, l: (b, 0, 0, 0)),
            pl.BlockSpec((bb, HIST_ROWS, pw), lambda b, l: (b, 0, 0)),
            pl.BlockSpec((tl, dh), lambda b, l: (l, 0)),
            pl.BlockSpec((tl, dh), lambda b, l: (l, 0)),
            pl.BlockSpec((RET_HEADS, chunk, chunk), const3),
            pl.BlockSpec((RET_HEADS, chunk, dh), const3),
            pl.BlockSpec((RET_HEADS, chunk, dh), const3),
            pl.BlockSpec((1, d_model), const2),
            pl.BlockSpec(consts["w_in"].shape, const2),
            pl.BlockSpec((1, rw_width), const2),
            pl.BlockSpec(consts["w_pool"].shape, const3),
            pl.BlockSpec((1, pw), const2),
            pl.BlockSpec(consts["w_out"].shape, const2),
            pl.BlockSpec((1, d_model), const2),
            pl.BlockSpec((d_model, LANES), const2),
            pl.BlockSpec((d_model, LANES), const2),
            pl.BlockSpec((rows, rows), const2),
        ],
        out_specs=[
            pl.BlockSpec((bb, tl, d_model), lambda b, l: (b, l, 0)),
            pl.BlockSpec((bb, tl, d_model), lambda b, l: (b, l, 0)),
            pl.BlockSpec((bb, tl, LANES), lambda b, l: (b, l, 0)),
            pl.BlockSpec((bb, tl, LANES), lambda b, l: (b, l, 0)),
            pl.BlockSpec((bb, RET_HEADS, dh, dh), lambda b, l: (b, 0, 0, 0)),
            pl.BlockSpec((bb, HIST_ROWS, pw), lambda b, l: (b, 0, 0)),
            pl.BlockSpec((1, LANES), const2),
        ],
        scratch_shapes=[
            pltpu.VMEM((bb, HIST_ROWS + tl, pw), F32),
            pltpu.VMEM((rows, rw_width), BF16),
            pltpu.VMEM((rows, rw_width), F32),
            pltpu.VMEM((rows, rw_width), BF16),
            pltpu.VMEM((rows, rw_width), F32),
            pltpu.VMEM((rows, d_model), BF16),
        ],
    )
    out_shape = [
        jax.ShapeDtypeStruct((bsz, seq, d_model), F32),
        jax.ShapeDtypeStruct((bsz, seq, d_model), F32),
        jax.ShapeDtypeStruct((bsz, seq, LANES), I32),
        jax.ShapeDtypeStruct((bsz, seq, LANES), F32),
        jax.ShapeDtypeStruct((bsz, RET_HEADS, dh, dh), F32),
        jax.ShapeDtypeStruct((bsz, HIST_ROWS, pw), F32),
        jax.ShapeDtypeStruct((1, LANES), F32),
    ]
    kern = functools.partial(_layer_kernel, bb=bb, tl=tl, chunk=chunk, pos0=pos0)
    return pl.pallas_call(
        kern, grid_spec=grid_spec, out_shape=out_shape, name=f"layer_pos{pos0}",
        compiler_params=pltpu.CompilerParams(
            dimension_semantics=("arbitrary", "arbitrary"), vmem_limit_bytes=VMEM_LIMIT),
    )(d_c, x, s0, h0, cosf, sinf, d_intra, d_q, d_k,
      consts["g1"], consts["w_in"], consts["gret"], consts["w_pool"], consts["pscale"],
      consts["w_out"], consts["g2"], consts["wr_hi"], consts["wr_lo"], tri)


def _sc_workers():
    info = plsc.get_sparse_core_info()
    return info.num_cores, info.num_subcores


def _sc_dispatch(src_a, src_b, idx0, idx1, n_out_rows):
    nc, ns = _sc_workers()
    nw = nc * ns
    ta, d = src_a.shape
    tb = src_b.shape[0]
    n_units = (ta + tb) // SC_UNIT
    units_a = ta // SC_UNIT
    assert ta % SC_UNIT == 0 and tb % SC_UNIT == 0 and n_units % nw == 0
    upw = n_units // nw
    idx0 = idx0.reshape(n_units, SC_UNIT)
    idx1 = idx1.reshape(n_units, SC_UNIT)
    mesh = plsc.VectorSubcoreMesh(core_axis_name="c", subcore_axis_name="s")

    @functools.partial(
        pl.kernel, mesh=mesh,
        out_type=jax.ShapeDtypeStruct((n_out_rows, d), src_a.dtype),
        scratch_types=[
            pltpu.VMEM((SC_UNIT,), I32),
            pltpu.VMEM((SC_UNIT,), I32),
            pltpu.VMEM((SC_UNIT, d), src_a.dtype),
            pltpu.SemaphoreType.DMA,
            pltpu.SemaphoreType.DMA,
        ],
    )
    def k(a_hbm, b_hbm, i0_hbm, i1_hbm, out_hbm, i0_v, i1_v, rows_v, sem0, sem1):
        wid = lax.axis_index("s") * nc + lax.axis_index("c")

        @pl.loop(0, upw)
        def _(j):
            unit = wid * upw + j
            pltpu.sync_copy(i0_hbm.at[unit], i0_v)
            pltpu.sync_copy(i1_hbm.at[unit], i1_v)

            @pl.when(unit < units_a)
            def _():
                pltpu.sync_copy(a_hbm.at[pl.ds(pl.multiple_of(unit * SC_UNIT, 8), SC_UNIT)], rows_v)

            @pl.when(unit >= units_a)
            def _():
                pltpu.sync_copy(b_hbm.at[pl.ds(pl.multiple_of((unit - units_a) * SC_UNIT, 8), SC_UNIT)], rows_v)

            c0 = pltpu.async_copy(rows_v, out_hbm.at[i0_v], sem0)
            c1 = pltpu.async_copy(rows_v, out_hbm.at[i1_v], sem1)
            c0.wait()
            c1.wait()

    return k(src_a, src_b, idx0, idx1)


def _sc_gather(table, idx):
    nc, ns = _sc_workers()
    nw = nc * ns
    n = idx.shape[0]
    d = table.shape[1]
    n_units = n // SC_UNIT
    assert n % SC_UNIT == 0 and n_units % nw == 0
    upw = n_units // nw
    idx = idx.reshape(n_units, SC_UNIT)
    mesh = plsc.VectorSubcoreMesh(core_axis_name="c", subcore_axis_name="s")

    @functools.partial(
        pl.kernel, mesh=mesh,
        out_type=jax.ShapeDtypeStruct((n, d), table.dtype),
        scratch_types=[
            pltpu.VMEM((SC_UNIT,), I32),
            pltpu.VMEM((SC_UNIT, d), table.dtype),
            pltpu.SemaphoreType.DMA,
        ],
    )
    def k(t_hbm, i_hbm, out_hbm, i_v, rows_v, sem):
        wid = lax.axis_index("s") * nc + lax.axis_index("c")

        @pl.loop(0, upw)
        def _(j):
            unit = wid * upw + j
            pltpu.sync_copy(i_hbm.at[unit], i_v)
            pltpu.async_copy(t_hbm.at[i_v], rows_v, sem).wait()
            pltpu.sync_copy(rows_v, out_hbm.at[pl.ds(pl.multiple_of(unit * SC_UNIT, 8), SC_UNIT)])

    return k(table, idx)


def _moe_kernel(te_ref, tv_ref, xs_ref, wg_ref, wu_ref, wd_ref, ys_ref, wgu_s, wd_s):
    i = pl.program_id(0)
    e = te_ref[i]
    valid = tv_ref[i]
    hidden = wd_s.shape[0]
    new_expert = (i == 0) | (te_ref[jnp.maximum(i - 1, 0)] != e)

    @pl.when(new_expert)
    def _():
        wgu_s[:, 0:hidden] = wg_ref[0].astype(BF16)
        wgu_s[:, hidden:2 * hidden] = wu_ref[0].astype(BF16)
        wd_s[...] = wd_ref[0].astype(BF16)

    @pl.when(valid > 0)
    def _():
        row = lax.broadcasted_iota(I32, xs_ref.shape, 0)
        x = jnp.where(row < valid, xs_ref[...], 0.0).astype(BF16)
        ab = jnp.dot(x, wgu_s[...], preferred_element_type=F32)
        a = ab[:, 0:hidden]
        he = a * _sigmoid(a) * ab[:, hidden:2 * hidden]
        ys_ref[...] = jnp.dot(he.astype(BF16), wd_s[...], preferred_element_type=F32)

    @pl.when(valid <= 0)
    def _():
        ys_ref[...] = jnp.zeros_like(ys_ref)


def _moe_call(tile_expert, tile_valid, xs, w_g, w_u, w_d):
    n_rows, d_model = xs.shape
    hidden = w_g.shape[-1]
    n_tiles = n_rows // MOE_TILE
    grid_spec = pltpu.PrefetchScalarGridSpec(
        num_scalar_prefetch=2,
        grid=(n_tiles,),
        in_specs=[
            pl.BlockSpec((MOE_TILE, d_model), lambda i, te, tv: (i, 0)),
            pl.BlockSpec((1, d_model, hidden), lambda i, te, tv: (te[i], 0, 0)),
            pl.BlockSpec((1, d_model, hidden), lambda i, te, tv: (te[i], 0, 0)),
            pl.BlockSpec((1, hidden, d_model), lambda i, te, tv: (te[i], 0, 0)),
        ],
        out_specs=pl.BlockSpec((MOE_TILE, d_model), lambda i, te, tv: (i, 0)),
        scratch_shapes=[
            pltpu.VMEM((d_model, 2 * hidden), BF16),
            pltpu.VMEM((hidden, d_model), BF16),
        ],
    )
    return pl.pallas_call(
        _moe_kernel, grid_spec=grid_spec,
        out_shape=jax.ShapeDtypeStruct((n_rows, d_model), F32), name="moe_experts",
        compiler_params=pltpu.CompilerParams(
            dimension_semantics=("arbitrary",), vmem_limit_bytes=VMEM_LIMIT),
    )(tile_expert, tile_valid, xs, w_g, w_u, w_d)


def _combine_kernel(x1_ref, y0_ref, y1_ref, rw_ref, gf_ref, out_ref):
    rw = rw_ref[...]
    x2 = x1_ref[...] + rw[:, 0:1] * y0_ref[0] + rw[:, 1:2] * y1_ref[0]
    out_ref[...] = _rms(x2, gf_ref[...])


def _combine_call(x1, yg, rw, gf, row_offset, tr=256):
    t, d_model = x1.shape
    off = row_offset // tr
    assert t % tr == 0 and row_offset % tr == 0
    return pl.pallas_call(
        _combine_kernel,
        grid=(t // tr,),
        in_specs=[
            pl.BlockSpec((tr, d_model), lambda i: (i, 0)),
            pl.BlockSpec((1, tr, d_model), lambda i: (0, off + i, 0)),
            pl.BlockSpec((1, tr, d_model), lambda i: (1, off + i, 0)),
            pl.BlockSpec((tr, LANES), lambda i: (i, 0)),
            pl.BlockSpec((1, d_model), lambda i: (0, 0)),
        ],
        out_specs=pl.BlockSpec((tr, d_model), lambda i: (i, 0)),
        out_shape=jax.ShapeDtypeStruct((t, d_model), F32), name=f"combine_off{row_offset}",
        compiler_params=pltpu.CompilerParams(
            dimension_semantics=("arbitrary",), vmem_limit_bytes=VMEM_LIMIT),
    )(x1, yg, yg, rw, gf)


def _one_layer(xp, xs, s_ret, c_pool, norm1_g, w_in, ret_norm_g, w_pool, pool_scale, w_out, norm2_g,
               w_rg, w_re, w_g, w_u, w_d, final_g, past_len, final):
    bp, seq, d_model = xp.shape
    bs, dseq, _ = xs.shape
    rw_width = ret_norm_g.shape[-1]
    pw = pool_scale.shape[-1]
    dh = rw_width // RET_HEADS
    tp, ts = bp * seq, bs * dseq
    t_all = tp + ts

    w_r = jnp.concatenate(
        [w_re, w_rg, jnp.zeros((d_model, LANES - N_EXPERTS - N_EXPERT_GROUPS), F32)], axis=1)
    wr_hi = w_r.astype(BF16)
    consts = dict(
        g1=norm1_g.reshape(1, d_model), w_in=w_in.astype(BF16), gret=ret_norm_g.reshape(1, rw_width),
        w_pool=w_pool.astype(BF16), pscale=pool_scale.reshape(1, pw), w_out=w_out.astype(BF16),
        g2=norm2_g.reshape(1, d_model), wr_hi=wr_hi, wr_lo=(w_r - wr_hi.astype(F32)).astype(BF16))

    s0p = jnp.zeros((bp, RET_HEADS, dh, dh), F32)
    h0p = jnp.zeros((bp, HIST_ROWS, pw), F32)
    h0s = jnp.pad(c_pool, ((0, 0), (HIST_ROWS - POOL_HIST, 0), (0, 0)))

    x1p, h2p, rip, rwp, st_p, hist_p, cnt_p = _layer_call(
        xp, s0p, h0p, 0, consts, bb=1, tl=512, chunk=256)
    x1s, h2s, ris, rws, st_s, hist_s, cnt_s = _layer_call(
        xs, s_ret, h0s, past_len, consts, bb=bs, tl=dseq, chunk=min(64, dseq))

    rip = rip.reshape(tp, LANES)[:, 0:4]
    ris = ris.reshape(ts, LANES)[:, 0:4]
    cnt_p = cnt_p[0, 0:N_EXPERTS].astype(I32)
    cnt_s = cnt_s[0, 0:N_EXPERTS].astype(I32)
    total = cnt_p + cnt_s
    padded = ((total + MOE_TILE - 1) // MOE_TILE) * MOE_TILE
    ends = jnp.cumsum(padded)
    starts = ends - padded
    ids = jnp.concatenate([rip[:, 0:2], ris[:, 0:2]], axis=0)
    onehot = ids[:, :, None] == jnp.arange(N_EXPERTS, dtype=I32)[None, None, :]
    base = jnp.concatenate([jnp.broadcast_to(starts, (tp, N_EXPERTS)),
                            jnp.broadcast_to(starts + cnt_p, (ts, N_EXPERTS))], axis=0)
    ranks = jnp.concatenate([rip[:, 2:4], ris[:, 2:4]], axis=0)
    pos = ranks + jnp.sum(jnp.where(onehot, base[:, None, :], 0), axis=-1)

    n_rows = ((2 * t_all + N_EXPERTS * (MOE_TILE - 1)) // MOE_TILE) * MOE_TILE
    n_tiles = n_rows // MOE_TILE
    tile_start = jnp.arange(n_tiles, dtype=I32) * MOE_TILE
    te = jnp.minimum(jnp.sum(tile_start[:, None] >= ends[None, :], axis=-1), N_EXPERTS - 1).astype(I32)
    tv = jnp.clip(jnp.take(starts + total, te) - tile_start, 0, MOE_TILE).astype(I32)
    last_used = jnp.max(jnp.where(total > 0, jnp.arange(N_EXPERTS, dtype=I32), 0))
    te = jnp.where(tile_start >= ends[-1], last_used, te)

    xs_sorted = _sc_dispatch(h2p.reshape(tp, d_model), h2s.reshape(ts, d_model),
                             pos[:, 0], pos[:, 1], n_rows)
    ys_sorted = _moe_call(te, tv, xs_sorted, w_g, w_u, w_d)
    yg = _sc_gather(ys_sorted, jnp.concatenate([pos[:, 0], pos[:, 1]], axis=0))
    yg = yg.reshape(2, t_all, d_model)

    gf = final_g.reshape(1, d_model) if final else None
    yp = _combine_call(x1p.reshape(tp, d_model), yg, rwp.reshape(tp, LANES), gf, 0)
    ysm = _combine_call(x1s.reshape(ts, d_model), yg, rws.reshape(ts, LANES), gf, tp)
    return (yp.reshape(bp, seq, d_model), ysm.reshape(bs, dseq, d_model),
            st_p, hist_p[:, HIST_ROWS - POOL_HIST:], st_s, hist_s[:, HIST_ROWS - POOL_HIST:])


def kernel(x_prompt, x_sample, state_ret, cache_pool, norm1_g, w_in, ret_norm_g, w_pool, pool_scale, w_out,
           norm2_g, w_router_group, w_router_expert, w_exp_gate, w_exp_up, w_exp_down, final_norm_g):
    depth = w_in.shape[0]
    assert depth == 1, "the final RMSNorm is fused into the last layer's combine kernel"
    yp, ys, s_p, h_p, s_s, h_s = _one_layer(
        x_prompt, x_sample, state_ret[0], cache_pool[0], norm1_g[0], w_in[0], ret_norm_g[0], w_pool[0],
        pool_scale[0], w_out[0], norm2_g[0], w_router_group[0], w_router_expert[0],
        w_exp_gate[0], w_exp_up[0], w_exp_down[0], final_norm_g, PAST_LEN, True)
    return (yp, ys, s_p[None], h_p[None], s_s[None], h_s[None])
```

```python
import functools

import jax
import jax.numpy as jnp
from jax import lax
from jax.experimental import pallas as pl
from jax.experimental.pallas import tpu as pltpu
from jax.experimental.pallas import tpu_sc as plsc

F32 = jnp.float32
BF16 = jnp.bfloat16
I32 = jnp.int32
U32 = jnp.uint32

EPS = 1e-6
ROPE_BASE = 10000.0
RET_HEADS = 4
POOL_WINDOWS = (2, 4, 8, 16)
POOL_HIST = max(POOL_WINDOWS) - 1
N_EXPERT_GROUPS = 4
EXPERTS_PER_GROUP = 8
N_EXPERTS = N_EXPERT_GROUPS * EXPERTS_PER_GROUP
EXPERT_SHIFT = EXPERTS_PER_GROUP.bit_length() - 1
PAST_LEN = 1024

LANES = 128
HIST_ROWS = 16
MOE_TILE = 256
SC_UNIT = 32
VMEM_LIMIT = 56 * 1024 * 1024


def _rms(x, g):
    return x * lax.rsqrt(jnp.mean(x * x, axis=-1, keepdims=True) + EPS) * g


def _sigmoid(x):
    return 1.0 / (1.0 + jnp.exp(-x))


def _pack_bf16_pair(lo, hi):
    lo_b = lax.bitcast_convert_type(lo.astype(BF16).astype(F32), U32)
    hi_b = lax.bitcast_convert_type(hi.astype(BF16).astype(F32), U32)
    return hi_b | (lo_b >> 16)


def _unpack_bf16_pair(p):
    lo = lax.bitcast_convert_type(p << 16, F32)
    hi = lax.bitcast_convert_type(p & jnp.uint32(0xFFFF0000), F32)
    return lo, hi


def _layer_kernel(dc_ref, x_ref, s0_ref, h0_ref, cos_ref, sin_ref, dintra_ref, dq_ref, dk_ref,
                  g1_ref, win_ref, gret_ref, wpool_ref, pscale_ref, wout_ref, g2_ref,
                  wrh_ref, wrl_ref, tri_ref,
                  x1_ref, h2_ref, ri_ref, rw_ref, st_ref, hist_ref, cnt_ref,
                  ue_ref, q_ref, k_ref, v_ref, o_ref, a_ref,
                  *, bb, tl, chunk, pos0):
    b_idx = pl.program_id(0)
    l_idx = pl.program_id(1)
    rows = bb * tl
    d_model = x_ref.shape[-1]
    rw_width = q_ref.shape[-1]
    dh = rw_width // RET_HEADS
    pw = ue_ref.shape[-1]
    gw = pw // len(POOL_WINDOWS)
    n_chunks = tl // chunk

    @pl.when(l_idx == 0)
    def _():
        st_ref[...] = s0_ref[...]
        ue_ref[:, 0:HIST_ROWS, :] = h0_ref[...]

    @pl.when((l_idx == 0) & (b_idx == 0))
    def _():
        cnt_ref[...] = jnp.zeros_like(cnt_ref)

    x = x_ref[...].reshape(rows, d_model)
    h = _rms(x, g1_ref[...])
    proj = jnp.dot(h.astype(BF16), win_ref[...], preferred_element_type=F32)

    cosf = cos_ref[...][None]
    sinf = sin_ref[...][None]
    k_scale = dh ** -0.5
    for hh in range(RET_HEADS):
        qh = proj[:, hh * dh:(hh + 1) * dh]
        kh = proj[:, rw_width + hh * dh:rw_width + (hh + 1) * dh]
        qr = (qh.reshape(bb, tl, dh) * cosf
              + pltpu.roll(qh, dh // 2, 1).reshape(bb, tl, dh) * sinf).reshape(rows, dh)
        kr = (kh.reshape(bb, tl, dh) * cosf
              + pltpu.roll(kh, dh // 2, 1).reshape(bb, tl, dh) * sinf).reshape(rows, dh)
        q_ref[:, hh * dh:(hh + 1) * dh] = qr.astype(BF16)
        k_ref[:, hh * dh:(hh + 1) * dh] = kr * k_scale
    v_ref[...] = proj[:, 2 * rw_width:3 * rw_width].astype(BF16)

    def ret_block(b, c):
        r0 = b * tl + c * chunk
        if not isinstance(r0, int):
            r0 = pl.multiple_of(r0, chunk)
        for hh in range(RET_HEADS):
            cs = slice(hh * dh, (hh + 1) * dh)
            qc = q_ref[pl.ds(r0, chunk), cs]
            kf = k_ref[pl.ds(r0, chunk), cs]
            vc = v_ref[pl.ds(r0, chunk), cs]
            s_old = st_ref[b, hh]
            sc = lax.dot_general(qc, kf.astype(BF16), (((1,), (1,)), ((), ())),
                                 preferred_element_type=F32) * dintra_ref[hh]
            o = (jnp.dot(sc.astype(BF16), vc, preferred_element_type=F32)
                 + dq_ref[hh] * jnp.dot(qc, s_old.astype(BF16), preferred_element_type=F32))
            kd = (kf * dk_ref[hh]).astype(BF16)
            s_new = dc_ref[hh] * s_old + lax.dot_general(
                kd, vc, (((0,), (0,)), ((), ())), preferred_element_type=F32)
            st_ref[b, hh] = s_new
            o_ref[pl.ds(r0, chunk), cs] = o

    if bb * n_chunks <= 4:
        for b in range(bb):
            for c in range(n_chunks):
                ret_block(b, c)
    else:
        def body(i, carry):
            ret_block(i // n_chunks, i % n_chunks)
            return carry
        lax.fori_loop(0, bb * n_chunks, body, 0)

    for hh in range(RET_HEADS):
        cs = slice(hh * dh, (hh + 1) * dh)
        oh = o_ref[:, cs]
        mu = jnp.mean(oh, axis=-1, keepdims=True)
        oc = oh - mu
        var = jnp.mean(oc * oc, axis=-1, keepdims=True)
        y = oc * lax.rsqrt(var + EPS) * gret_ref[:, cs]
        g = proj[:, 3 * rw_width + hh * dh:3 * rw_width + (hh + 1) * dh]
        a_ref[:, cs] = (g * _sigmoid(g) * y).astype(BF16)

    u = proj[:, 4 * rw_width:4 * rw_width + pw]
    ue_ref[:, HIST_ROWS:HIST_ROWS + tl, :] = u.reshape(bb, tl, pw)
    pos = pos0 + l_idx * tl + lax.broadcasted_iota(I32, (1, tl, 1), 1)
    for gi, w in enumerate(POOL_WINDOWS):
        cs = slice(gi * gw, (gi + 1) * gw)
        acc = ue_ref[:, HIST_ROWS:HIST_ROWS + tl, cs]
        for j in range(1, w):
            acc = acc + ue_ref[:, HIST_ROWS - j:HIST_ROWS - j + tl, cs]
        inv_cnt = 1.0 / jnp.minimum(pos + 1, w).astype(F32)
        p = (acc * inv_cnt).reshape(rows, gw) - u[:, cs]
        z = jnp.dot(p.astype(BF16), wpool_ref[gi], preferred_element_type=F32) * pscale_ref[:, cs]
        a_ref[:, rw_width + gi * gw:rw_width + (gi + 1) * gw] = z.astype(BF16)
    tail = ue_ref[:, tl:tl + HIST_ROWS, :]
    ue_ref[:, 0:HIST_ROWS, :] = tail
    hist_ref[...] = tail

    x1 = x + jnp.dot(a_ref[...], wout_ref[...], preferred_element_type=F32)
    x1_ref[...] = x1.reshape(bb, tl, d_model)
    h2 = _rms(x1, g2_ref[...])
    h2_ref[...] = _pack_bf16_pair(h2[:, 0:d_model // 2], h2[:, d_model // 2:]).reshape(bb, tl, d_model // 2)

    h2_hi = h2.astype(BF16)
    h2_lo = (h2 - h2_hi.astype(F32)).astype(BF16)
    logits = (jnp.dot(h2_hi, wrh_ref[...], preferred_element_type=F32)
              + jnp.dot(h2_lo, wrh_ref[...], preferred_element_type=F32)
              + jnp.dot(h2_hi, wrl_ref[...], preferred_element_type=F32))
    lane = lax.broadcasted_iota(I32, (rows, LANES), 1)
    lane_f = lane.astype(F32)
    neg = jnp.float32(-jnp.inf)
    big = jnp.float32(1e9)
    is_grp = (lane >= N_EXPERTS) & (lane < N_EXPERTS + N_EXPERT_GROUPS)
    gl = jnp.where(is_grp, logits, neg)
    gmax = jnp.max(gl, axis=-1, keepdims=True)
    gidx = jnp.min(jnp.where(gl == gmax, lane_f - N_EXPERTS, big), axis=-1, keepdims=True)
    p_sel = 1.0 / jnp.sum(jnp.exp(gl - gmax), axis=-1, keepdims=True)
    in_grp = (lane < N_EXPERTS) & (lax.shift_right_logical(lane, EXPERT_SHIFT).astype(F32) == gidx)
    el = jnp.where(in_grp, logits, neg)
    m1 = jnp.max(el, axis=-1, keepdims=True)
    i1 = jnp.min(jnp.where(el == m1, lane_f, big), axis=-1, keepdims=True)
    el2 = jnp.where(lane_f == i1, neg, el)
    m2 = jnp.max(el2, axis=-1, keepdims=True)
    i2 = jnp.min(jnp.where(el2 == m2, lane_f, big), axis=-1, keepdims=True)
    e2 = jnp.exp(m2 - m1)
    w1 = p_sel / (1.0 + e2)
    w2 = p_sel * e2 / (1.0 + e2)

    hit1 = lane_f == i1
    hit2 = lane_f == i2
    onehot = (hit1 | hit2).astype(BF16)
    before = jnp.dot(tri_ref[...], onehot, preferred_element_type=F32) + cnt_ref[...]
    r1 = jnp.sum(jnp.where(hit1, before, 0.0), axis=-1, keepdims=True)
    r2 = jnp.sum(jnp.where(hit2, before, 0.0), axis=-1, keepdims=True)
    cnt_ref[...] = cnt_ref[...] + jnp.sum(onehot.astype(F32), axis=0, keepdims=True)

    ri = jnp.where(lane == 0, i1, jnp.where(lane == 1, i2, jnp.where(lane == 2, r1, jnp.where(lane == 3, r2, 0.0))))
    ri_ref[...] = ri.astype(I32).reshape(bb, tl, LANES)
    rw_ref[...] = jnp.where(lane == 0, w1, jnp.where(lane == 1, w2, 0.0)).reshape(bb, tl, LANES)


def _layer_call(x, s0, h0, pos0, consts, *, bb, tl, chunk):
    bsz, seq, d_model = x.shape
    rows = bb * tl
    rw_width = consts["gret"].shape[-1]
    pw = consts["pscale"].shape[-1]
    dh = rw_width // RET_HEADS

    half = dh // 2
    inv = ROPE_BASE ** (-jnp.arange(half, dtype=F32) / half)
    ang = (pos0 + jnp.arange(seq)).astype(F32)[:, None] * inv[None, :]
    cos, sin = jnp.cos(ang), jnp.sin(ang)
    cosf = jnp.concatenate([cos, cos], axis=-1)
    sinf = jnp.concatenate([-sin, sin], axis=-1)

    lg = jnp.log1p(-jnp.exp2(-5.0 - jnp.arange(RET_HEADS, dtype=F32)))
    idx = jnp.arange(chunk, dtype=F32)
    diff = idx[:, None] - idx[None, :]
    d_intra = jnp.where(diff[None] >= 0, jnp.exp(jnp.maximum(diff, 0.0)[None] * lg[:, None, None]), 0.0)
    d_q = jnp.broadcast_to(jnp.exp((idx + 1.0)[None, :] * lg[:, None])[:, :, None], (RET_HEADS, chunk, dh))
    d_k = jnp.broadcast_to(jnp.exp((chunk - 1.0 - idx)[None, :] * lg[:, None])[:, :, None], (RET_HEADS, chunk, dh))
    d_c = jnp.exp(chunk * lg)
    tri = jnp.tril(jnp.ones((rows, rows), BF16), -1)

    const2 = lambda b, l, *_: (0, 0)
    const3 = lambda b, l, *_: (0, 0, 0)
    grid_spec = pltpu.PrefetchScalarGridSpec(
        num_scalar_prefetch=0,
        grid=(bsz // bb, seq // tl),
        in_specs=[
            pl.BlockSpec(memory_space=pltpu.SMEM),
            pl.BlockSpec((bb, tl, d_model), lambda b, l: (b, l, 0)),
            pl.BlockSpec((bb, RET_HEADS, dh, dh), lambda b, l: (b, 0, 0, 0)),
            pl.BlockSpec((bb, HIST_ROWS, pw), lambda b, l: (b, 0, 0)),
            pl.BlockSpec((tl, dh), lambda b, l: (l, 0)),
            pl.BlockSpec((tl, dh), lambda b, l: (l, 0)),
            pl.BlockSpec((RET_HEADS, chunk, chunk), const3),
            pl.BlockSpec((RET_HEADS, chunk, dh), const3),
            pl.BlockSpec((RET_HEADS, chunk, dh), const3),
            pl.BlockSpec((1, d_model), const2),
            pl.BlockSpec(consts["w_in"].shape, const2),
            pl.BlockSpec((1, rw_width), const2),
            pl.BlockSpec(consts["w_pool"].shape, const3),
            pl.BlockSpec((1, pw), const2),
            pl.BlockSpec(consts["w_out"].shape, const2),
            pl.BlockSpec((1, d_model), const2),
            pl.BlockSpec((d_model, LANES), const2),
            pl.BlockSpec((d_model, LANES), const2),
            pl.BlockSpec((rows, rows), const2),
        ],
        out_specs=[
            pl.BlockSpec((bb, tl, d_model), lambda b, l: (b, l, 0)),
            pl.BlockSpec((bb, tl, d_model // 2), lambda b, l: (b, l, 0)),
            pl.BlockSpec((bb, tl, LANES), lambda b, l: (b, l, 0)),
            pl.BlockSpec((bb, tl, LANES), lambda b, l: (b, l, 0)),
            pl.BlockSpec((bb, RET_HEADS, dh, dh), lambda b, l: (b, 0, 0, 0)),
            pl.BlockSpec((bb, HIST_ROWS, pw), lambda b, l: (b, 0, 0)),
            pl.BlockSpec((1, LANES), const2),
        ],
        scratch_shapes=[
            pltpu.VMEM((bb, HIST_ROWS + tl, pw), F32),
            pltpu.VMEM((rows, rw_width), BF16),
            pltpu.VMEM((rows, rw_width), F32),
            pltpu.VMEM((rows, rw_width), BF16),
            pltpu.VMEM((rows, rw_width), F32),
            pltpu.VMEM((rows, d_model), BF16),
        ],
    )
    out_shape = [
        jax.ShapeDtypeStruct((bsz, seq, d_model), F32),
        jax.ShapeDtypeStruct((bsz, seq, d_model // 2), U32),
        jax.ShapeDtypeStruct((bsz, seq, LANES), I32),
        jax.ShapeDtypeStruct((bsz, seq, LANES), F32),
        jax.ShapeDtypeStruct((bsz, RET_HEADS, dh, dh), F32),
        jax.ShapeDtypeStruct((bsz, HIST_ROWS, pw), F32),
        jax.ShapeDtypeStruct((1, LANES), F32),
    ]
    kern = functools.partial(_layer_kernel, bb=bb, tl=tl, chunk=chunk, pos0=pos0)
    return pl.pallas_call(
        kern, grid_spec=grid_spec, out_shape=out_shape, name=f"layer_pos{pos0}",
        compiler_params=pltpu.CompilerParams(
            dimension_semantics=("arbitrary", "arbitrary"), vmem_limit_bytes=VMEM_LIMIT),
    )(d_c, x, s0, h0, cosf, sinf, d_intra, d_q, d_k,
      consts["g1"], consts["w_in"], consts["gret"], consts["w_pool"], consts["pscale"],
      consts["w_out"], consts["g2"], consts["wr_hi"], consts["wr_lo"], tri)


def _sc_partition(n_units):
    info = plsc.get_sparse_core_info()
    nc, nw = info.num_cores, info.num_cores * info.num_subcores
    upw = -(-n_units // nw)
    upw += upw % 2
    return nc, nw, upw


def _units_by_worker(idx, n_units, upw, nw):
    idx = jnp.pad(idx.reshape(n_units, SC_UNIT), ((0, nw * upw - n_units), (0, 0)))
    return idx.reshape(upw, nw, SC_UNIT).transpose(1, 0, 2)


def _sc_dispatch(src_a, src_b, idx0, idx1, n_out_rows):
    ta, d = src_a.shape
    tb = src_b.shape[0]
    assert ta % SC_UNIT == 0 and tb % SC_UNIT == 0
    n_units = (ta + tb) // SC_UNIT
    units_a = ta // SC_UNIT
    nc, nw, upw = _sc_partition(n_units)
    idx0 = _units_by_worker(idx0, n_units, upw, nw)
    idx1 = _units_by_worker(idx1, n_units, upw, nw)
    mesh = plsc.VectorSubcoreMesh(core_axis_name="c", subcore_axis_name="s")
    dma = pltpu.SemaphoreType.DMA

    @functools.partial(
        pl.kernel, mesh=mesh,
        out_type=jax.ShapeDtypeStruct((n_out_rows, d), src_a.dtype),
        scratch_types=[
            pltpu.VMEM((upw, SC_UNIT), I32),
            pltpu.VMEM((upw, SC_UNIT), I32),
            pltpu.VMEM((SC_UNIT, d), src_a.dtype),
            pltpu.VMEM((SC_UNIT, d), src_a.dtype),
            dma, dma, dma, dma, dma, dma,
        ],
    )
    def k(a_hbm, b_hbm, i0_hbm, i1_hbm, out_hbm, i0_v, i1_v, rows0, rows1, l0, l1, p0, p1, q0, q1):
        wid = lax.axis_index("s") * nc + lax.axis_index("c")
        pltpu.sync_copy(i0_hbm.at[wid], i0_v)
        pltpu.sync_copy(i1_hbm.at[wid], i1_v)
        rows, lsem, psem, qsem = (rows0, rows1), (l0, l1), (p0, p1), (q0, q1)

        def live(j):
            return j * nw + wid < n_units

        def load(j, b, op):
            unit = j * nw + wid

            @pl.when(live(j) & (unit < units_a))
            def _():
                op(pltpu.make_async_copy(
                    a_hbm.at[pl.ds(pl.multiple_of(unit * SC_UNIT, 8), SC_UNIT)], rows[b], lsem[b]))

            @pl.when(live(j) & (unit >= units_a))
            def _():
                op(pltpu.make_async_copy(
                    b_hbm.at[pl.ds(pl.multiple_of((unit - units_a) * SC_UNIT, 8), SC_UNIT)], rows[b], lsem[b]))

        def scatter(j, b, op):
            @pl.when(live(j))
            def _():
                op(pltpu.make_async_copy(rows[b], out_hbm.at[i0_v.at[j]], psem[b]))
                op(pltpu.make_async_copy(rows[b], out_hbm.at[i1_v.at[j]], qsem[b]))

        start = lambda c: c.start()
        wait = lambda c: c.wait()
        load(0, 0, start)

        @pl.loop(0, upw, step=2)
        def _(j):
            @pl.when(j > 0)
            def _():
                scatter(j - 1, 1, wait)
            load(j + 1, 1, start)
            load(j, 0, wait)
            scatter(j, 0, start)
            scatter(j, 0, wait)

            @pl.when(j + 2 < upw)
            def _():
                load(j + 2, 0, start)
            load(j + 1, 1, wait)
            scatter(j + 1, 1, start)

        scatter(upw - 1, 1, wait)

    return k(src_a, src_b, idx0, idx1)


def _sc_gather(table, idx):
    n = idx.shape[0]
    d = table.shape[1]
    assert n % SC_UNIT == 0
    n_units = n // SC_UNIT
    nc, nw, upw = _sc_partition(n_units)
    idx = _units_by_worker(idx, n_units, upw, nw)
    mesh = plsc.VectorSubcoreMesh(core_axis_name="c", subcore_axis_name="s")
    dma = pltpu.SemaphoreType.DMA

    @functools.partial(
        pl.kernel, mesh=mesh,
        out_type=jax.ShapeDtypeStruct((n, d), table.dtype),
        scratch_types=[
            pltpu.VMEM((upw, SC_UNIT), I32),
            pltpu.VMEM((SC_UNIT, d), table.dtype),
            pltpu.VMEM((SC_UNIT, d), table.dtype),
            dma, dma, dma, dma,
        ],
    )
    def k(t_hbm, i_hbm, out_hbm, i_v, rows0, rows1, g0, g1, w0, w1):
        wid = lax.axis_index("s") * nc + lax.axis_index("c")
        pltpu.sync_copy(i_hbm.at[wid], i_v)
        rows, gsem, wsem = (rows0, rows1), (g0, g1), (w0, w1)

        def live(j):
            return j * nw + wid < n_units

        def gather(j, b, op):
            @pl.when(live(j))
            def _():
                op(pltpu.make_async_copy(t_hbm.at[i_v.at[j]], rows[b], gsem[b]))

        def write(j, b, op):
            @pl.when(live(j))
            def _():
                op(pltpu.make_async_copy(
                    rows[b], out_hbm.at[pl.ds(pl.multiple_of((j * nw + wid) * SC_UNIT, 8), SC_UNIT)], wsem[b]))

        start = lambda c: c.start()
        wait = lambda c: c.wait()
        gather(0, 0, start)

        @pl.loop(0, upw, step=2)
        def _(j):
            @pl.when(j > 0)
            def _():
                write(j - 1, 1, wait)
            gather(j + 1, 1, start)
            gather(j, 0, wait)
            write(j, 0, start)
            write(j, 0, wait)

            @pl.when(j + 2 < upw)
            def _():
                gather(j + 2, 0, start)
            gather(j + 1, 1, wait)
            write(j + 1, 1, start)

        write(upw - 1, 1, wait)

    return k(table, idx)


def _moe_kernel(te_ref, tv_ref, xs_ref, wg_ref, wu_ref, wd_ref, ys_ref, wgu_s, wd_s):
    i = pl.program_id(0)
    e = te_ref[i]
    valid = tv_ref[i]
    hidden = wd_s.shape[0]
    new_expert = (i == 0) | (te_ref[jnp.maximum(i - 1, 0)] != e)

    @pl.when(new_expert)
    def _():
        wgu_s[:, 0:hidden] = wg_ref[0].astype(BF16)
        wgu_s[:, hidden:2 * hidden] = wu_ref[0].astype(BF16)
        wd_s[...] = wd_ref[0].astype(BF16)

    @pl.when(valid > 0)
    def _():
        half = xs_ref.shape[-1]
        row = lax.broadcasted_iota(I32, xs_ref.shape, 0)
        x_lo, x_hi = _unpack_bf16_pair(jnp.where(row < valid, xs_ref[...], jnp.uint32(0)))
        ab = (jnp.dot(x_lo.astype(BF16), wgu_s[0:half, :], preferred_element_type=F32)
              + jnp.dot(x_hi.astype(BF16), wgu_s[half:2 * half, :], preferred_element_type=F32))
        a = ab[:, 0:hidden]
        he = a * _sigmoid(a) * ab[:, hidden:2 * hidden]
        y = jnp.dot(he.astype(BF16), wd_s[...], preferred_element_type=F32)
        ys_ref[...] = _pack_bf16_pair(y[:, 0:half], y[:, half:2 * half])

    @pl.when(valid <= 0)
    def _():
        ys_ref[...] = jnp.zeros_like(ys_ref)


def _moe_call(tile_expert, tile_valid, xs, w_g, w_u, w_d):
    n_rows, half = xs.shape
    d_model, hidden = w_g.shape[-2:]
    n_tiles = n_rows // MOE_TILE
    grid_spec = pltpu.PrefetchScalarGridSpec(
        num_scalar_prefetch=2,
        grid=(n_tiles,),
        in_specs=[
            pl.BlockSpec((MOE_TILE, half), lambda i, te, tv: (i, 0)),
            pl.BlockSpec((1, d_model, hidden), lambda i, te, tv: (te[i], 0, 0)),
            pl.BlockSpec((1, d_model, hidden), lambda i, te, tv: (te[i], 0, 0)),
            pl.BlockSpec((1, hidden, d_model), lambda i, te, tv: (te[i], 0, 0)),
        ],
        out_specs=pl.BlockSpec((MOE_TILE, half), lambda i, te, tv: (i, 0)),
        scratch_shapes=[
            pltpu.VMEM((d_model, 2 * hidden), BF16),
            pltpu.VMEM((hidden, d_model), BF16),
        ],
    )
    return pl.pallas_call(
        _moe_kernel, grid_spec=grid_spec,
        out_shape=jax.ShapeDtypeStruct((n_rows, half), U32), name="moe_experts",
        compiler_params=pltpu.CompilerParams(
            dimension_semantics=("arbitrary",), vmem_limit_bytes=VMEM_LIMIT),
    )(tile_expert, tile_valid, xs, w_g, w_u, w_d)


def _combine_kernel(x1_ref, y0_ref, y1_ref, rw_ref, gf_ref, out_ref):
    rw = rw_ref[...]
    a_lo, a_hi = _unpack_bf16_pair(y0_ref[0])
    b_lo, b_hi = _unpack_bf16_pair(y1_ref[0])
    w0, w1 = rw[:, 0:1], rw[:, 1:2]
    moe = jnp.concatenate([w0 * a_lo + w1 * b_lo, w0 * a_hi + w1 * b_hi], axis=-1)
    out_ref[...] = _rms(x1_ref[...] + moe, gf_ref[...])


def _combine_call(x1, yg, rw, gf, row_offset, tr=256):
    t, d_model = x1.shape
    half = yg.shape[-1]
    off = row_offset // tr
    assert t % tr == 0 and row_offset % tr == 0
    return pl.pallas_call(
        _combine_kernel,
        grid=(t // tr,),
        in_specs=[
            pl.BlockSpec((tr, d_model), lambda i: (i, 0)),
            pl.BlockSpec((1, tr, half), lambda i: (0, off + i, 0)),
            pl.BlockSpec((1, tr, half), lambda i: (1, off + i, 0)),
            pl.BlockSpec((tr, LANES), lambda i: (i, 0)),
            pl.BlockSpec((1, d_model), lambda i: (0, 0)),
        ],
        out_specs=pl.BlockSpec((tr, d_model), lambda i: (i, 0)),
        out_shape=jax.ShapeDtypeStruct((t, d_model), F32), name=f"combine_off{row_offset}",
        compiler_params=pltpu.CompilerParams(
            dimension_semantics=("arbitrary",), vmem_limit_bytes=VMEM_LIMIT),
    )(x1, yg, yg, rw, gf)


def _one_layer(xp, xs, s_ret, c_pool, norm1_g, w_in, ret_norm_g, w_pool, pool_scale, w_out, norm2_g,
               w_rg, w_re, w_g, w_u, w_d, final_g, past_len, final):
    bp, seq, d_model = xp.shape
    bs, dseq, _ = xs.shape
    rw_width = ret_norm_g.shape[-1]
    pw = pool_scale.shape[-1]
    dh = rw_width // RET_HEADS
    tp, ts = bp * seq, bs * dseq
    t_all = tp + ts

    w_r = jnp.concatenate(
        [w_re, w_rg, jnp.zeros((d_model, LANES - N_EXPERTS - N_EXPERT_GROUPS), F32)], axis=1)
    wr_hi = w_r.astype(BF16)
    consts = dict(
        g1=norm1_g.reshape(1, d_model), w_in=w_in.astype(BF16), gret=ret_norm_g.reshape(1, rw_width),
        w_pool=w_pool.astype(BF16), pscale=pool_scale.reshape(1, pw), w_out=w_out.astype(BF16),
        g2=norm2_g.reshape(1, d_model), wr_hi=wr_hi, wr_lo=(w_r - wr_hi.astype(F32)).astype(BF16))

    s0p = jnp.zeros((bp, RET_HEADS, dh, dh), F32)
    h0p = jnp.zeros((bp, HIST_ROWS, pw), F32)
    h0s = jnp.pad(c_pool, ((0, 0), (HIST_ROWS - POOL_HIST, 0), (0, 0)))

    x1p, h2p, rip, rwp, st_p, hist_p, cnt_p = _layer_call(
        xp, s0p, h0p, 0, consts, bb=1, tl=512, chunk=256)
    x1s, h2s, ris, rws, st_s, hist_s, cnt_s = _layer_call(
        xs, s_ret, h0s, past_len, consts, bb=bs, tl=dseq, chunk=min(64, dseq))

    rip = rip.reshape(tp, LANES)[:, 0:4]
    ris = ris.reshape(ts, LANES)[:, 0:4]
    cnt_p = cnt_p[0, 0:N_EXPERTS].astype(I32)
    cnt_s = cnt_s[0, 0:N_EXPERTS].astype(I32)
    total = cnt_p + cnt_s
    padded = ((total + MOE_TILE - 1) // MOE_TILE) * MOE_TILE
    ends = jnp.cumsum(padded)
    starts = ends - padded
    ids = jnp.concatenate([rip[:, 0:2], ris[:, 0:2]], axis=0)
    onehot = ids[:, :, None] == jnp.arange(N_EXPERTS, dtype=I32)[None, None, :]
    base = jnp.concatenate([jnp.broadcast_to(starts, (tp, N_EXPERTS)),
                            jnp.broadcast_to(starts + cnt_p, (ts, N_EXPERTS))], axis=0)
    ranks = jnp.concatenate([rip[:, 2:4], ris[:, 2:4]], axis=0)
    pos = ranks + jnp.sum(jnp.where(onehot, base[:, None, :], 0), axis=-1)

    n_rows = ((2 * t_all + N_EXPERTS * (MOE_TILE - 1)) // MOE_TILE) * MOE_TILE
    n_tiles = n_rows // MOE_TILE
    tile_start = jnp.arange(n_tiles, dtype=I32) * MOE_TILE
    te = jnp.minimum(jnp.sum(tile_start[:, None] >= ends[None, :], axis=-1), N_EXPERTS - 1).astype(I32)
    tv = jnp.clip(jnp.take(starts + total, te) - tile_start, 0, MOE_TILE).astype(I32)
    last_used = jnp.max(jnp.where(total > 0, jnp.arange(N_EXPERTS, dtype=I32), 0))
    te = jnp.where(tile_start >= ends[-1], last_used, te)

    xs_sorted = _sc_dispatch(h2p.reshape(tp, d_model // 2), h2s.reshape(ts, d_model // 2),
                             pos[:, 0], pos[:, 1], n_rows)
    ys_sorted = _moe_call(te, tv, xs_sorted, w_g, w_u, w_d)
    yg = _sc_gather(ys_sorted, jnp.concatenate([pos[:, 0], pos[:, 1]], axis=0))
    yg = yg.reshape(2, t_all, d_model // 2)

    gf = final_g.reshape(1, d_model) if final else None
    yp = _combine_call(x1p.reshape(tp, d_model), yg, rwp.reshape(tp, LANES), gf, 0)
    ysm = _combine_call(x1s.reshape(ts, d_model), yg, rws.reshape(ts, LANES), gf, tp)
    return (yp.reshape(bp, seq, d_model), ysm.reshape(bs, dseq, d_model),
            st_p, hist_p[:, HIST_ROWS - POOL_HIST:], st_s, hist_s[:, HIST_ROWS - POOL_HIST:])


def kernel(x_prompt, x_sample, state_ret, cache_pool, norm1_g, w_in, ret_norm_g, w_pool, pool_scale, w_out,
           norm2_g, w_router_group, w_router_expert, w_exp_gate, w_exp_up, w_exp_down, final_norm_g):
    depth = w_in.shape[0]
    assert depth == 1, "the final RMSNorm is fused into the last layer's combine kernel"
    yp, ys, s_p, h_p, s_s, h_s = _one_layer(
        x_prompt, x_sample, state_ret[0], cache_pool[0], norm1_g[0], w_in[0], ret_norm_g[0], w_pool[0],
        pool_scale[0], w_out[0], norm2_g[0], w_router_group[0], w_router_expert[0],
        w_exp_gate[0], w_exp_up[0], w_exp_down[0], final_norm_g, PAST_LEN, True)
    return (yp, ys, s_p[None], h_p[None], s_s[None], h_s[None])
```

```python
import functools

import jax
import jax.numpy as jnp
from jax import lax
from jax.experimental import pallas as pl
from jax.experimental.pallas import tpu as pltpu
from jax.experimental.pallas import tpu_sc as plsc

F32 = jnp.float32
BF16 = jnp.bfloat16
I32 = jnp.int32
U32 = jnp.uint32

EPS = 1e-6
ROPE_BASE = 10000.0
RET_HEADS = 4
POOL_WINDOWS = (2, 4, 8, 16)
POOL_HIST = max(POOL_WINDOWS) - 1
N_EXPERT_GROUPS = 4
EXPERTS_PER_GROUP = 8
N_EXPERTS = N_EXPERT_GROUPS * EXPERTS_PER_GROUP
EXPERT_SHIFT = EXPERTS_PER_GROUP.bit_length() - 1
PAST_LEN = 1024

LANES = 128
HIST_ROWS = 16
MOE_TILE = 512
SC_UNIT = 32
VMEM_LIMIT = 56 * 1024 * 1024


def _rms(x, g):
    return x * lax.rsqrt(jnp.mean(x * x, axis=-1, keepdims=True) + EPS) * g


def _sigmoid(x):
    return 1.0 / (1.0 + jnp.exp(-x))


def _pack_bf16_pair(lo, hi):
    lo_b = lax.bitcast_convert_type(lo.astype(BF16).astype(F32), U32)
    hi_b = lax.bitcast_convert_type(hi.astype(BF16).astype(F32), U32)
    return hi_b | (lo_b >> 16)


def _unpack_bf16_pair(p):
    lo = lax.bitcast_convert_type(p << 16, F32)
    hi = lax.bitcast_convert_type(p & jnp.uint32(0xFFFF0000), F32)
    return lo, hi


def _layer_kernel(dc_ref, x_ref, s0_ref, h0_ref, cos_ref, sin_ref, dintra_ref, dq_ref, dk_ref,
                  g1_ref, win_ref, gret_ref, wpool_ref, pscale_ref, wout_ref, g2_ref,
                  wrh_ref, wrl_ref, tri_ref,
                  x1_ref, h2_ref, ri_ref, rw_ref, st_ref, hist_ref, cnt_ref,
                  ue_ref, q_ref, k_ref, v_ref, o_ref, a_ref,
                  *, bb, tl, chunk, pos0):
    b_idx = pl.program_id(0)
    l_idx = pl.program_id(1)
    rows = bb * tl
    d_model = x_ref.shape[-1]
    rw_width = q_ref.shape[-1]
    dh = rw_width // RET_HEADS
    pw = ue_ref.shape[-1]
    gw = pw // len(POOL_WINDOWS)
    n_chunks = tl // chunk

    @pl.when(l_idx == 0)
    def _():
        st_ref[...] = s0_ref[...]
        ue_ref[:, 0:HIST_ROWS, :] = h0_ref[...]

    @pl.when((l_idx == 0) & (b_idx == 0))
    def _():
        cnt_ref[...] = jnp.zeros_like(cnt_ref)

    x = x_ref[...].reshape(rows, d_model)
    h = _rms(x, g1_ref[...])
    proj = jnp.dot(h.astype(BF16), win_ref[...], preferred_element_type=F32)

    cosf = cos_ref[...][None]
    sinf = sin_ref[...][None]
    k_scale = dh ** -0.5
    for hh in range(RET_HEADS):
        qh = proj[:, hh * dh:(hh + 1) * dh]
        kh = proj[:, rw_width + hh * dh:rw_width + (hh + 1) * dh]
        qr = (qh.reshape(bb, tl, dh) * cosf
              + pltpu.roll(qh, dh // 2, 1).reshape(bb, tl, dh) * sinf).reshape(rows, dh)
        kr = (kh.reshape(bb, tl, dh) * cosf
              + pltpu.roll(kh, dh // 2, 1).reshape(bb, tl, dh) * sinf).reshape(rows, dh)
        q_ref[:, hh * dh:(hh + 1) * dh] = qr.astype(BF16)
        k_ref[:, hh * dh:(hh + 1) * dh] = kr * k_scale
    v_ref[...] = proj[:, 2 * rw_width:3 * rw_width].astype(BF16)

    def ret_block(b, c):
        r0 = b * tl + c * chunk
        if not isinstance(r0, int):
            r0 = pl.multiple_of(r0, chunk)
        for hh in range(RET_HEADS):
            cs = slice(hh * dh, (hh + 1) * dh)
            qc = q_ref[pl.ds(r0, chunk), cs]
            kf = k_ref[pl.ds(r0, chunk), cs]
            vc = v_ref[pl.ds(r0, chunk), cs]
            s_old = st_ref[b, hh]
            sc = lax.dot_general(qc, kf.astype(BF16), (((1,), (1,)), ((), ())),
                                 preferred_element_type=F32) * dintra_ref[hh]
            o = (jnp.dot(sc.astype(BF16), vc, preferred_element_type=F32)
                 + dq_ref[hh] * jnp.dot(qc, s_old.astype(BF16), preferred_element_type=F32))
            kd = (kf * dk_ref[hh]).astype(BF16)
            s_new = dc_ref[hh] * s_old + lax.dot_general(
                kd, vc, (((0,), (0,)), ((), ())), preferred_element_type=F32)
            st_ref[b, hh] = s_new
            o_ref[pl.ds(r0, chunk), cs] = o

    if bb * n_chunks <= 4:
        for b in range(bb):
            for c in range(n_chunks):
                ret_block(b, c)
    else:
        def body(i, carry):
            ret_block(i // n_chunks, i % n_chunks)
            return carry
        lax.fori_loop(0, bb * n_chunks, body, 0)

    for hh in range(RET_HEADS):
        cs = slice(hh * dh, (hh + 1) * dh)
        oh = o_ref[:, cs]
        mu = jnp.mean(oh, axis=-1, keepdims=True)
        oc = oh - mu
        var = jnp.mean(oc * oc, axis=-1, keepdims=True)
        y = oc * lax.rsqrt(var + EPS) * gret_ref[:, cs]
        g = proj[:, 3 * rw_width + hh * dh:3 * rw_width + (hh + 1) * dh]
        a_ref[:, cs] = (g * _sigmoid(g) * y).astype(BF16)

    u = proj[:, 4 * rw_width:4 * rw_width + pw]
    ue_ref[:, HIST_ROWS:HIST_ROWS + tl, :] = u.reshape(bb, tl, pw)
    pos = pos0 + l_idx * tl + lax.broadcasted_iota(I32, (1, tl, 1), 1)
    for gi, w in enumerate(POOL_WINDOWS):
        cs = slice(gi * gw, (gi + 1) * gw)
        acc = ue_ref[:, HIST_ROWS:HIST_ROWS + tl, cs]
        for j in range(1, w):
            acc = acc + ue_ref[:, HIST_ROWS - j:HIST_ROWS - j + tl, cs]
        inv_cnt = 1.0 / jnp.minimum(pos + 1, w).astype(F32)
        p = (acc * inv_cnt).reshape(rows, gw) - u[:, cs]
        z = jnp.dot(p.astype(BF16), wpool_ref[gi], preferred_element_type=F32) * pscale_ref[:, cs]
        a_ref[:, rw_width + gi * gw:rw_width + (gi + 1) * gw] = z.astype(BF16)
    tail = ue_ref[:, tl:tl + HIST_ROWS, :]
    ue_ref[:, 0:HIST_ROWS, :] = tail
    hist_ref[...] = tail

    x1 = x + jnp.dot(a_ref[...], wout_ref[...], preferred_element_type=F32)
    x1_ref[...] = x1.reshape(bb, tl, d_model)
    h2 = _rms(x1, g2_ref[...])
    h2_ref[...] = _pack_bf16_pair(h2[:, 0:d_model // 2], h2[:, d_model // 2:]).reshape(bb, tl, d_model // 2)

    h2_hi = h2.astype(BF16)
    h2_lo = (h2 - h2_hi.astype(F32)).astype(BF16)
    logits = (jnp.dot(h2_hi, wrh_ref[...], preferred_element_type=F32)
              + jnp.dot(h2_lo, wrh_ref[...], preferred_element_type=F32)
              + jnp.dot(h2_hi, wrl_ref[...], preferred_element_type=F32))
    lane = lax.broadcasted_iota(I32, (rows, LANES), 1)
    lane_f = lane.astype(F32)
    neg = jnp.float32(-jnp.inf)
    big = jnp.float32(1e9)
    is_grp = (lane >= N_EXPERTS) & (lane < N_EXPERTS + N_EXPERT_GROUPS)
    gl = jnp.where(is_grp, logits, neg)
    gmax = jnp.max(gl, axis=-1, keepdims=True)
    gidx = jnp.min(jnp.where(gl == gmax, lane_f - N_EXPERTS, big), axis=-1, keepdims=True)
    p_sel = 1.0 / jnp.sum(jnp.exp(gl - gmax), axis=-1, keepdims=True)
    in_grp = (lane < N_EXPERTS) & (lax.shift_right_logical(lane, EXPERT_SHIFT).astype(F32) == gidx)
    el = jnp.where(in_grp, logits, neg)
    m1 = jnp.max(el, axis=-1, keepdims=True)
    i1 = jnp.min(jnp.where(el == m1, lane_f, big), axis=-1, keepdims=True)
    el2 = jnp.where(lane_f == i1, neg, el)
    m2 = jnp.max(el2, axis=-1, keepdims=True)
    i2 = jnp.min(jnp.where(el2 == m2, lane_f, big), axis=-1, keepdims=True)
    e2 = jnp.exp(m2 - m1)
    w1 = p_sel / (1.0 + e2)
    w2 = p_sel * e2 / (1.0 + e2)

    hit1 = lane_f == i1
    hit2 = lane_f == i2
    onehot = (hit1 | hit2).astype(BF16)
    before = jnp.dot(tri_ref[...], onehot, preferred_element_type=F32) + cnt_ref[...]
    r1 = jnp.sum(jnp.where(hit1, before, 0.0), axis=-1, keepdims=True)
    r2 = jnp.sum(jnp.where(hit2, before, 0.0), axis=-1, keepdims=True)
    cnt_ref[...] = cnt_ref[...] + jnp.sum(onehot.astype(F32), axis=0, keepdims=True)

    ri = jnp.where(lane == 0, i1, jnp.where(lane == 1, i2, jnp.where(lane == 2, r1, jnp.where(lane == 3, r2, 0.0))))
    ri_ref[...] = ri.astype(I32).reshape(bb, tl, LANES)
    rw_ref[...] = jnp.where(lane == 0, w1, jnp.where(lane == 1, w2, 0.0)).reshape(bb, tl, LANES)


def _layer_call(x, s0, h0, pos0, consts, *, bb, tl, chunk):
    bsz, seq, d_model = x.shape
    rows = bb * tl
    rw_width = consts["gret"].shape[-1]
    pw = consts["pscale"].shape[-1]
    dh = rw_width // RET_HEADS

    half = dh // 2
    inv = ROPE_BASE ** (-jnp.arange(half, dtype=F32) / half)
    ang = (pos0 + jnp.arange(seq)).astype(F32)[:, None] * inv[None, :]
    cos, sin = jnp.cos(ang), jnp.sin(ang)
    cosf = jnp.concatenate([cos, cos], axis=-1)
    sinf = jnp.concatenate([-sin, sin], axis=-1)

    lg = jnp.log1p(-jnp.exp2(-5.0 - jnp.arange(RET_HEADS, dtype=F32)))
    idx = jnp.arange(chunk, dtype=F32)
    diff = idx[:, None] - idx[None, :]
    d_intra = jnp.where(diff[None] >= 0, jnp.exp(jnp.maximum(diff, 0.0)[None] * lg[:, None, None]), 0.0)
    d_q = jnp.broadcast_to(jnp.exp((idx + 1.0)[None, :] * lg[:, None])[:, :, None], (RET_HEADS, chunk, dh))
    d_k = jnp.broadcast_to(jnp.exp((chunk - 1.0 - idx)[None, :] * lg[:, None])[:, :, None], (RET_HEADS, chunk, dh))
    d_c = jnp.exp(chunk * lg)
    tri = jnp.tril(jnp.ones((rows, rows), BF16), -1)

    const2 = lambda b, l, *_: (0, 0)
    const3 = lambda b, l, *_: (0, 0, 0)
    grid_spec = pltpu.PrefetchScalarGridSpec(
        num_scalar_prefetch=0,
        grid=(bsz // bb, seq // tl),
        in_specs=[
            pl.BlockSpec(memory_space=pltpu.SMEM),
            pl.BlockSpec((bb, tl, d_model), lambda b, l: (b, l, 0)),
            pl.BlockSpec((bb, RET_HEADS, dh, dh), lambda b, l: (b, 0, 0, 0)),
            pl.BlockSpec((bb, HIST_ROWS, pw), lambda b, l: (b, 0, 0)),
            pl.BlockSpec((tl, dh), lambda b, l: (l, 0)),
            pl.BlockSpec((tl, dh), lambda b, l: (l, 0)),
            pl.BlockSpec((RET_HEADS, chunk, chunk), const3),
            pl.BlockSpec((RET_HEADS, chunk, dh), const3),
            pl.BlockSpec((RET_HEADS, chunk, dh), const3),
            pl.BlockSpec((1, d_model), const2),
            pl.BlockSpec(consts["w_in"].shape, const2),
            pl.BlockSpec((1, rw_width), const2),
            pl.BlockSpec(consts["w_pool"].shape, const3),
            pl.BlockSpec((1, pw), const2),
            pl.BlockSpec(consts["w_out"].shape, const2),
            pl.BlockSpec((1, d_model), const2),
            pl.BlockSpec((d_model, LANES), const2),
            pl.BlockSpec((d_model, LANES), const2),
            pl.BlockSpec((rows, rows), const2),
        ],
        out_specs=[
            pl.BlockSpec((bb, tl, d_model), lambda b, l: (b, l, 0)),
            pl.BlockSpec((bb, tl, d_model // 2), lambda b, l: (b, l, 0)),
            pl.BlockSpec((bb, tl, LANES), lambda b, l: (b, l, 0)),
            pl.BlockSpec((bb, tl, LANES), lambda b, l: (b, l, 0)),
            pl.BlockSpec((bb, RET_HEADS, dh, dh), lambda b, l: (b, 0, 0, 0)),
            pl.BlockSpec((bb, HIST_ROWS, pw), lambda b, l: (b, 0, 0)),
            pl.BlockSpec((1, LANES), const2),
        ],
        scratch_shapes=[
            pltpu.VMEM((bb, HIST_ROWS + tl, pw), F32),
            pltpu.VMEM((rows, rw_width), BF16),
            pltpu.VMEM((rows, rw_width), F32),
            pltpu.VMEM((rows, rw_width), BF16),
            pltpu.VMEM((rows, rw_width), F32),
            pltpu.VMEM((rows, d_model), BF16),
        ],
    )
    out_shape = [
        jax.ShapeDtypeStruct((bsz, seq, d_model), F32),
        jax.ShapeDtypeStruct((bsz, seq, d_model // 2), U32),
        jax.ShapeDtypeStruct((bsz, seq, LANES), I32),
        jax.ShapeDtypeStruct((bsz, seq, LANES), F32),
        jax.ShapeDtypeStruct((bsz, RET_HEADS, dh, dh), F32),
        jax.ShapeDtypeStruct((bsz, HIST_ROWS, pw), F32),
        jax.ShapeDtypeStruct((1, LANES), F32),
    ]
    kern = functools.partial(_layer_kernel, bb=bb, tl=tl, chunk=chunk, pos0=pos0)
    return pl.pallas_call(
        kern, grid_spec=grid_spec, out_shape=out_shape, name=f"layer_pos{pos0}",
        compiler_params=pltpu.CompilerParams(
            dimension_semantics=("arbitrary", "arbitrary"), vmem_limit_bytes=VMEM_LIMIT),
    )(d_c, x, s0, h0, cosf, sinf, d_intra, d_q, d_k,
      consts["g1"], consts["w_in"], consts["gret"], consts["w_pool"], consts["pscale"],
      consts["w_out"], consts["g2"], consts["wr_hi"], consts["wr_lo"], tri)


def _sc_partition(n_units):
    info = plsc.get_sparse_core_info()
    nc, nw = info.num_cores, info.num_cores * info.num_subcores
    upw = -(-n_units // nw)
    upw += upw % 2
    return nc, nw, upw


def _units_by_worker(idx, n_units, upw, nw):
    idx = jnp.pad(idx.reshape(n_units, SC_UNIT), ((0, nw * upw - n_units), (0, 0)))
    return idx.reshape(upw, nw, SC_UNIT).transpose(1, 0, 2)


def _sc_dispatch(src_a, src_b, idx0, idx1, n_out_rows):
    ta, d = src_a.shape
    tb = src_b.shape[0]
    assert ta % SC_UNIT == 0 and tb % SC_UNIT == 0
    n_units = (ta + tb) // SC_UNIT
    units_a = ta // SC_UNIT
    nc, nw, upw = _sc_partition(n_units)
    idx0 = _units_by_worker(idx0, n_units, upw, nw)
    idx1 = _units_by_worker(idx1, n_units, upw, nw)
    mesh = plsc.VectorSubcoreMesh(core_axis_name="c", subcore_axis_name="s")
    dma = pltpu.SemaphoreType.DMA

    @functools.partial(
        pl.kernel, mesh=mesh,
        out_type=jax.ShapeDtypeStruct((n_out_rows, d), src_a.dtype),
        scratch_types=[
            pltpu.VMEM((upw, SC_UNIT), I32),
            pltpu.VMEM((upw, SC_UNIT), I32),
            pltpu.VMEM((SC_UNIT, d), src_a.dtype),
            pltpu.VMEM((SC_UNIT, d), src_a.dtype),
            dma, dma, dma, dma, dma, dma,
        ],
    )
    def k(a_hbm, b_hbm, i0_hbm, i1_hbm, out_hbm, i0_v, i1_v, rows0, rows1, l0, l1, p0, p1, q0, q1):
        wid = lax.axis_index("s") * nc + lax.axis_index("c")
        pltpu.sync_copy(i0_hbm.at[wid], i0_v)
        pltpu.sync_copy(i1_hbm.at[wid], i1_v)
        rows, lsem, psem, qsem = (rows0, rows1), (l0, l1), (p0, p1), (q0, q1)

        def live(j):
            return j * nw + wid < n_units

        def load(j, b, op):
            unit = j * nw + wid

            @pl.when(live(j) & (unit < units_a))
            def _():
                op(pltpu.make_async_copy(
                    a_hbm.at[pl.ds(pl.multiple_of(unit * SC_UNIT, 8), SC_UNIT)], rows[b], lsem[b]))

            @pl.when(live(j) & (unit >= units_a))
            def _():
                op(pltpu.make_async_copy(
                    b_hbm.at[pl.ds(pl.multiple_of((unit - units_a) * SC_UNIT, 8), SC_UNIT)], rows[b], lsem[b]))

        def scatter(j, b, op):
            @pl.when(live(j))
            def _():
                op(pltpu.make_async_copy(rows[b], out_hbm.at[i0_v.at[j]], psem[b]))
                op(pltpu.make_async_copy(rows[b], out_hbm.at[i1_v.at[j]], qsem[b]))

        start = lambda c: c.start()
        wait = lambda c: c.wait()
        load(0, 0, start)

        @pl.loop(0, upw, step=2)
        def _(j):
            @pl.when(j > 0)
            def _():
                scatter(j - 1, 1, wait)
            load(j + 1, 1, start)
            load(j, 0, wait)
            scatter(j, 0, start)
            scatter(j, 0, wait)

            @pl.when(j + 2 < upw)
            def _():
                load(j + 2, 0, start)
            load(j + 1, 1, wait)
            scatter(j + 1, 1, start)

        scatter(upw - 1, 1, wait)

    return k(src_a, src_b, idx0, idx1)


def _sc_gather(table, idx):
    n = idx.shape[0]
    d = table.shape[1]
    assert n % SC_UNIT == 0
    n_units = n // SC_UNIT
    nc, nw, upw = _sc_partition(n_units)
    idx = _units_by_worker(idx, n_units, upw, nw)
    mesh = plsc.VectorSubcoreMesh(core_axis_name="c", subcore_axis_name="s")
    dma = pltpu.SemaphoreType.DMA

    @functools.partial(
        pl.kernel, mesh=mesh,
        out_type=jax.ShapeDtypeStruct((n, d), table.dtype),
        scratch_types=[
            pltpu.VMEM((upw, SC_UNIT), I32),
            pltpu.VMEM((SC_UNIT, d), table.dtype),
            pltpu.VMEM((SC_UNIT, d), table.dtype),
            dma, dma, dma, dma,
        ],
    )
    def k(t_hbm, i_hbm, out_hbm, i_v, rows0, rows1, g0, g1, w0, w1):
        wid = lax.axis_index("s") * nc + lax.axis_index("c")
        pltpu.sync_copy(i_hbm.at[wid], i_v)
        rows, gsem, wsem = (rows0, rows1), (g0, g1), (w0, w1)

        def live(j):
            return j * nw + wid < n_units

        def gather(j, b, op):
            @pl.when(live(j))
            def _():
                op(pltpu.make_async_copy(t_hbm.at[i_v.at[j]], rows[b], gsem[b]))

        def write(j, b, op):
            @pl.when(live(j))
            def _():
                op(pltpu.make_async_copy(
                    rows[b], out_hbm.at[pl.ds(pl.multiple_of((j * nw + wid) * SC_UNIT, 8), SC_UNIT)], wsem[b]))

        start = lambda c: c.start()
        wait = lambda c: c.wait()
        gather(0, 0, start)

        @pl.loop(0, upw, step=2)
        def _(j):
            @pl.when(j > 0)
            def _():
                write(j - 1, 1, wait)
            gather(j + 1, 1, start)
            gather(j, 0, wait)
            write(j, 0, start)
            write(j, 0, wait)

            @pl.when(j + 2 < upw)
            def _():
                gather(j + 2, 0, start)
            gather(j + 1, 1, wait)
            write(j + 1, 1, start)

        write(upw - 1, 1, wait)

    return k(table, idx)


def _moe_kernel(te_ref, tv_ref, xs_ref, wg_ref, wu_ref, wd_ref, ys_ref, wgu_s, wd_s):
    i = pl.program_id(0)
    e = te_ref[i]
    valid = tv_ref[i]
    hidden = wd_s.shape[0]
    new_expert = (i == 0) | (te_ref[jnp.maximum(i - 1, 0)] != e)

    @pl.when(new_expert)
    def _():
        wgu_s[:, 0:hidden] = wg_ref[0].astype(BF16)
        wgu_s[:, hidden:2 * hidden] = wu_ref[0].astype(BF16)
        wd_s[...] = wd_ref[0].astype(BF16)

    @pl.when(valid > 0)
    def _():
        half = xs_ref.shape[-1]
        row = lax.broadcasted_iota(I32, xs_ref.shape, 0)
        x_lo, x_hi = _unpack_bf16_pair(jnp.where(row < valid, xs_ref[...], jnp.uint32(0)))
        ab = (jnp.dot(x_lo.astype(BF16), wgu_s[0:half, :], preferred_element_type=F32)
              + jnp.dot(x_hi.astype(BF16), wgu_s[half:2 * half, :], preferred_element_type=F32))
        a = ab[:, 0:hidden]
        he = a * _sigmoid(a) * ab[:, hidden:2 * hidden]
        y = jnp.dot(he.astype(BF16), wd_s[...], preferred_element_type=F32)
        ys_ref[...] = _pack_bf16_pair(y[:, 0:half], y[:, half:2 * half])

    @pl.when(valid <= 0)
    def _():
        ys_ref[...] = jnp.zeros_like(ys_ref)


def _moe_call(tile_expert, tile_valid, xs, w_g, w_u, w_d):
    n_rows, half = xs.shape
    d_model, hidden = w_g.shape[-2:]
    n_tiles = n_rows // MOE_TILE
    grid_spec = pltpu.PrefetchScalarGridSpec(
        num_scalar_prefetch=2,
        grid=(n_tiles,),
        in_specs=[
            pl.BlockSpec((MOE_TILE, half), lambda i, te, tv: (i, 0)),
            pl.BlockSpec((1, d_model, hidden), lambda i, te, tv: (te[i], 0, 0)),
            pl.BlockSpec((1, d_model, hidden), lambda i, te, tv: (te[i], 0, 0)),
            pl.BlockSpec((1, hidden, d_model), lambda i, te, tv: (te[i], 0, 0)),
        ],
        out_specs=pl.BlockSpec((MOE_TILE, half), lambda i, te, tv: (i, 0)),
        scratch_shapes=[
            pltpu.VMEM((d_model, 2 * hidden), BF16),
            pltpu.VMEM((hidden, d_model), BF16),
        ],
    )
    return pl.pallas_call(
        _moe_kernel, grid_spec=grid_spec,
        out_shape=jax.ShapeDtypeStruct((n_rows, half), U32), name="moe_experts",
        compiler_params=pltpu.CompilerParams(
            dimension_semantics=("arbitrary",), vmem_limit_bytes=VMEM_LIMIT),
    )(tile_expert, tile_valid, xs, w_g, w_u, w_d)


def _combine_kernel(x1_ref, y0_ref, y1_ref, rw_ref, gf_ref, out_ref):
    rw = rw_ref[...]
    a_lo, a_hi = _unpack_bf16_pair(y0_ref[0])
    b_lo, b_hi = _unpack_bf16_pair(y1_ref[0])
    w0, w1 = rw[:, 0:1], rw[:, 1:2]
    moe = jnp.concatenate([w0 * a_lo + w1 * b_lo, w0 * a_hi + w1 * b_hi], axis=-1)
    out_ref[...] = _rms(x1_ref[...] + moe, gf_ref[...])


def _combine_call(x1, yg, rw, gf, row_offset, tr=512):
    t, d_model = x1.shape
    half = yg.shape[-1]
    off = row_offset // tr
    assert t % tr == 0 and row_offset % tr == 0
    return pl.pallas_call(
        _combine_kernel,
        grid=(t // tr,),
        in_specs=[
            pl.BlockSpec((tr, d_model), lambda i: (i, 0)),
            pl.BlockSpec((1, tr, half), lambda i: (0, off + i, 0)),
            pl.BlockSpec((1, tr, half), lambda i: (1, off + i, 0)),
            pl.BlockSpec((tr, LANES), lambda i: (i, 0)),
            pl.BlockSpec((1, d_model), lambda i: (0, 0)),
        ],
        out_specs=pl.BlockSpec((tr, d_model), lambda i: (i, 0)),
        out_shape=jax.ShapeDtypeStruct((t, d_model), F32), name=f"combine_off{row_offset}",
        compiler_params=pltpu.CompilerParams(
            dimension_semantics=("arbitrary",), vmem_limit_bytes=VMEM_LIMIT),
    )(x1, yg, yg, rw, gf)


def _one_layer(xp, xs, s_ret, c_pool, norm1_g, w_in, ret_norm_g, w_pool, pool_scale, w_out, norm2_g,
               w_rg, w_re, w_g, w_u, w_d, final_g, past_len, final):
    bp, seq, d_model = xp.shape
    bs, dseq, _ = xs.shape
    rw_width = ret_norm_g.shape[-1]
    pw = pool_scale.shape[-1]
    dh = rw_width // RET_HEADS
    tp, ts = bp * seq, bs * dseq
    t_all = tp + ts

    w_r = jnp.concatenate(
        [w_re, w_rg, jnp.zeros((d_model, LANES - N_EXPERTS - N_EXPERT_GROUPS), F32)], axis=1)
    wr_hi = w_r.astype(BF16)
    consts = dict(
        g1=norm1_g.reshape(1, d_model), w_in=w_in.astype(BF16), gret=ret_norm_g.reshape(1, rw_width),
        w_pool=w_pool.astype(BF16), pscale=pool_scale.reshape(1, pw), w_out=w_out.astype(BF16),
        g2=norm2_g.reshape(1, d_model), wr_hi=wr_hi, wr_lo=(w_r - wr_hi.astype(F32)).astype(BF16))

    s0p = jnp.zeros((bp, RET_HEADS, dh, dh), F32)
    h0p = jnp.zeros((bp, HIST_ROWS, pw), F32)
    h0s = jnp.pad(c_pool, ((0, 0), (HIST_ROWS - POOL_HIST, 0), (0, 0)))

    x1p, h2p, rip, rwp, st_p, hist_p, cnt_p = _layer_call(
        xp, s0p, h0p, 0, consts, bb=1, tl=512, chunk=256)
    x1s, h2s, ris, rws, st_s, hist_s, cnt_s = _layer_call(
        xs, s_ret, h0s, past_len, consts, bb=bs, tl=dseq, chunk=min(64, dseq))

    rip = rip.reshape(tp, LANES)[:, 0:4]
    ris = ris.reshape(ts, LANES)[:, 0:4]
    cnt_p = cnt_p[0, 0:N_EXPERTS].astype(I32)
    cnt_s = cnt_s[0, 0:N_EXPERTS].astype(I32)
    total = cnt_p + cnt_s
    padded = ((total + MOE_TILE - 1) // MOE_TILE) * MOE_TILE
    ends = jnp.cumsum(padded)
    starts = ends - padded
    ids = jnp.concatenate([rip[:, 0:2], ris[:, 0:2]], axis=0)
    onehot = ids[:, :, None] == jnp.arange(N_EXPERTS, dtype=I32)[None, None, :]
    base = jnp.concatenate([jnp.broadcast_to(starts, (tp, N_EXPERTS)),
                            jnp.broadcast_to(starts + cnt_p, (ts, N_EXPERTS))], axis=0)
    ranks = jnp.concatenate([rip[:, 2:4], ris[:, 2:4]], axis=0)
    pos = ranks + jnp.sum(jnp.where(onehot, base[:, None, :], 0), axis=-1)

    n_rows = ((2 * t_all + N_EXPERTS * (MOE_TILE - 1)) // MOE_TILE) * MOE_TILE
    n_tiles = n_rows // MOE_TILE
    tile_start = jnp.arange(n_tiles, dtype=I32) * MOE_TILE
    te = jnp.minimum(jnp.sum(tile_start[:, None] >= ends[None, :], axis=-1), N_EXPERTS - 1).astype(I32)
    tv = jnp.clip(jnp.take(starts + total, te) - tile_start, 0, MOE_TILE).astype(I32)
    last_used = jnp.max(jnp.where(total > 0, jnp.arange(N_EXPERTS, dtype=I32), 0))
    te = jnp.where(tile_start >= ends[-1], last_used, te)

    xs_sorted = _sc_dispatch(h2p.reshape(tp, d_model // 2), h2s.reshape(ts, d_model // 2),
                             pos[:, 0], pos[:, 1], n_rows)
    ys_sorted = _moe_call(te, tv, xs_sorted, w_g, w_u, w_d)
    yg = _sc_gather(ys_sorted, jnp.concatenate([pos[:, 0], pos[:, 1]], axis=0))
    yg = yg.reshape(2, t_all, d_model // 2)

    gf = final_g.reshape(1, d_model) if final else None
    yp = _combine_call(x1p.reshape(tp, d_model), yg, rwp.reshape(tp, LANES), gf, 0)
    ysm = _combine_call(x1s.reshape(ts, d_model), yg, rws.reshape(ts, LANES), gf, tp)
    return (yp.reshape(bp, seq, d_model), ysm.reshape(bs, dseq, d_model),
            st_p, hist_p[:, HIST_ROWS - POOL_HIST:], st_s, hist_s[:, HIST_ROWS - POOL_HIST:])


def kernel(x_prompt, x_sample, state_ret, cache_pool, norm1_g, w_in, ret_norm_g, w_pool, pool_scale, w_out,
           norm2_g, w_router_group, w_router_expert, w_exp_gate, w_exp_up, w_exp_down, final_norm_g):
    depth = w_in.shape[0]
    assert depth == 1, "the final RMSNorm is fused into the last layer's combine kernel"
    yp, ys, s_p, h_p, s_s, h_s = _one_layer(
        x_prompt, x_sample, state_ret[0], cache_pool[0], norm1_g[0], w_in[0], ret_norm_g[0], w_pool[0],
        pool_scale[0], w_out[0], norm2_g[0], w_router_group[0], w_router_expert[0],
        w_exp_gate[0], w_exp_up[0], w_exp_down[0], final_norm_g, PAST_LEN, True)
    return (yp, ys, s_p[None], h_p[None], s_s[None], h_s[None])
```

```python
import functools

import jax
import jax.numpy as jnp
from jax import lax
from jax.experimental import pallas as pl
from jax.experimental.pallas import tpu as pltpu
from jax.experimental.pallas import tpu_sc as plsc

F32 = jnp.float32
BF16 = jnp.bfloat16
I32 = jnp.int32
U32 = jnp.uint32

EPS = 1e-6
ROPE_BASE = 10000.0
RET_HEADS = 4
POOL_WINDOWS = (2, 4, 8, 16)
POOL_HIST = max(POOL_WINDOWS) - 1
N_EXPERT_GROUPS = 4
EXPERTS_PER_GROUP = 8
N_EXPERTS = N_EXPERT_GROUPS * EXPERTS_PER_GROUP
EXPERT_SHIFT = EXPERTS_PER_GROUP.bit_length() - 1
PAST_LEN = 1024

LANES = 128
HIST_ROWS = 16
MOE_TILE = 512
SC_UNIT = 32
VMEM_LIMIT = 56 * 1024 * 1024


def _rms(x, g):
    return x * lax.rsqrt(jnp.mean(x * x, axis=-1, keepdims=True) + EPS) * g


def _sigmoid(x):
    return 1.0 / (1.0 + jnp.exp(-x))


def _pack_bf16_pair(lo, hi):
    lo_b = lax.bitcast_convert_type(lo.astype(BF16).astype(F32), U32)
    hi_b = lax.bitcast_convert_type(hi.astype(BF16).astype(F32), U32)
    return hi_b | (lo_b >> 16)


def _unpack_bf16_pair(p):
    lo = lax.bitcast_convert_type(p << 16, F32)
    hi = lax.bitcast_convert_type(p & jnp.uint32(0xFFFF0000), F32)
    return lo, hi


def _layer_kernel(dc_ref, x_ref, s0_ref, h0_ref, cos_ref, sin_ref, dintra_ref, dq_ref, dk_ref,
                  g1_ref, win_ref, gret_ref, wpool_ref, pscale_ref, wout_ref, g2_ref,
                  wr_ref, tri_ref,
                  x1_ref, h2_ref, ri_ref, rw_ref, st_ref, hist_ref, cnt_ref,
                  ue_ref, q_ref, k_ref, v_ref, o_ref, a_ref,
                  *, bb, tl, chunk, pos0):
    b_idx = pl.program_id(0)
    l_idx = pl.program_id(1)
    rows = bb * tl
    d_model = x_ref.shape[-1]
    rw_width = q_ref.shape[-1]
    dh = rw_width // RET_HEADS
    pw = ue_ref.shape[-1]
    gw = pw // len(POOL_WINDOWS)
    n_chunks = tl // chunk

    @pl.when(l_idx == 0)
    def _():
        st_ref[...] = s0_ref[...]
        ue_ref[:, 0:HIST_ROWS, :] = h0_ref[...]

    @pl.when((l_idx == 0) & (b_idx == 0))
    def _():
        cnt_ref[...] = jnp.zeros_like(cnt_ref)

    x = x_ref[...].reshape(rows, d_model)
    h = _rms(x, g1_ref[...])
    proj = jnp.dot(h.astype(BF16), win_ref[...], preferred_element_type=F32)

    cosf = cos_ref[...][None]
    sinf = sin_ref[...][None]
    k_scale = dh ** -0.5
    for hh in range(RET_HEADS):
        qh = proj[:, hh * dh:(hh + 1) * dh]
        kh = proj[:, rw_width + hh * dh:rw_width + (hh + 1) * dh]
        qr = (qh.reshape(bb, tl, dh) * cosf
              + pltpu.roll(qh, dh // 2, 1).reshape(bb, tl, dh) * sinf).reshape(rows, dh)
        kr = (kh.reshape(bb, tl, dh) * cosf
              + pltpu.roll(kh, dh // 2, 1).reshape(bb, tl, dh) * sinf).reshape(rows, dh)
        q_ref[:, hh * dh:(hh + 1) * dh] = qr.astype(BF16)
        k_ref[:, hh * dh:(hh + 1) * dh] = kr * k_scale
    v_ref[...] = proj[:, 2 * rw_width:3 * rw_width].astype(BF16)

    def ret_block(b, c):
        r0 = b * tl + c * chunk
        if not isinstance(r0, int):
            r0 = pl.multiple_of(r0, chunk)
        for hh in range(RET_HEADS):
            cs = slice(hh * dh, (hh + 1) * dh)
            qc = q_ref[pl.ds(r0, chunk), cs]
            kf = k_ref[pl.ds(r0, chunk), cs]
            vc = v_ref[pl.ds(r0, chunk), cs]
            s_old = st_ref[b, hh]
            sc = lax.dot_general(qc, kf.astype(BF16), (((1,), (1,)), ((), ())),
                                 preferred_element_type=F32) * dintra_ref[hh]
            o = (jnp.dot(sc.astype(BF16), vc, preferred_element_type=F32)
                 + dq_ref[hh] * jnp.dot(qc, s_old.astype(BF16), preferred_element_type=F32))
            kd = (kf * dk_ref[hh]).astype(BF16)
            s_new = dc_ref[hh] * s_old + lax.dot_general(
                kd, vc, (((0,), (0,)), ((), ())), preferred_element_type=F32)
            st_ref[b, hh] = s_new
            o_ref[pl.ds(r0, chunk), cs] = o

    if bb * n_chunks <= 4:
        for b in range(bb):
            for c in range(n_chunks):
                ret_block(b, c)
    else:
        def body(i, carry):
            ret_block(i // n_chunks, i % n_chunks)
            return carry
        lax.fori_loop(0, bb * n_chunks, body, 0)

    for hh in range(RET_HEADS):
        cs = slice(hh * dh, (hh + 1) * dh)
        oh = o_ref[:, cs]
        mu = jnp.mean(oh, axis=-1, keepdims=True)
        oc = oh - mu
        var = jnp.mean(oc * oc, axis=-1, keepdims=True)
        y = oc * lax.rsqrt(var + EPS) * gret_ref[:, cs]
        g = proj[:, 3 * rw_width + hh * dh:3 * rw_width + (hh + 1) * dh]
        a_ref[:, cs] = (g * _sigmoid(g) * y).astype(BF16)

    u = proj[:, 4 * rw_width:4 * rw_width + pw]
    ue_ref[:, HIST_ROWS:HIST_ROWS + tl, :] = u.reshape(bb, tl, pw)
    pos = pos0 + l_idx * tl + lax.broadcasted_iota(I32, (1, tl, 1), 1)
    for gi, w in enumerate(POOL_WINDOWS):
        cs = slice(gi * gw, (gi + 1) * gw)
        acc = ue_ref[:, HIST_ROWS:HIST_ROWS + tl, cs]
        for j in range(1, w):
            acc = acc + ue_ref[:, HIST_ROWS - j:HIST_ROWS - j + tl, cs]
        inv_cnt = 1.0 / jnp.minimum(pos + 1, w).astype(F32)
        p = (acc * inv_cnt).reshape(rows, gw) - u[:, cs]
        z = jnp.dot(p.astype(BF16), wpool_ref[gi], preferred_element_type=F32) * pscale_ref[:, cs]
        a_ref[:, rw_width + gi * gw:rw_width + (gi + 1) * gw] = z.astype(BF16)
    tail = ue_ref[:, tl:tl + HIST_ROWS, :]
    ue_ref[:, 0:HIST_ROWS, :] = tail
    hist_ref[...] = tail

    x1 = x + jnp.dot(a_ref[...], wout_ref[...], preferred_element_type=F32)
    x1_ref[...] = x1.reshape(bb, tl, d_model)
    h2 = _rms(x1, g2_ref[...])
    h2_ref[...] = _pack_bf16_pair(h2[:, 0:d_model // 2], h2[:, d_model // 2:]).reshape(bb, tl, d_model // 2)

    h2_hi = h2.astype(BF16)
    h2_lo = (h2 - h2_hi.astype(F32)).astype(BF16)
    two = jnp.dot(h2_hi, wr_ref[...], preferred_element_type=F32)
    logits = (two[:, 0:LANES] + two[:, LANES:2 * LANES]
              + jnp.dot(h2_lo, wr_ref[:, 0:LANES], preferred_element_type=F32))
    lt = logits.T
    neg = jnp.float32(-jnp.inf)
    big = jnp.float32(1e9)
    sub = lax.broadcasted_iota(I32, (EXPERTS_PER_GROUP, rows), 0).astype(F32)
    gl = jnp.where(sub < N_EXPERT_GROUPS, lt[N_EXPERTS:N_EXPERTS + EXPERTS_PER_GROUP], neg)
    gmax = jnp.max(gl, axis=0, keepdims=True)
    gidx = jnp.min(jnp.where(gl == gmax, sub, big), axis=0, keepdims=True)
    p_sel = 1.0 / jnp.sum(jnp.exp(gl - gmax), axis=0, keepdims=True)
    el = lt[0:EXPERTS_PER_GROUP]
    for g in range(1, N_EXPERT_GROUPS):
        el = jnp.where(gidx == g, lt[g * EXPERTS_PER_GROUP:(g + 1) * EXPERTS_PER_GROUP], el)
    m1 = jnp.max(el, axis=0, keepdims=True)
    t1 = jnp.min(jnp.where(el == m1, sub, big), axis=0, keepdims=True)
    el2 = jnp.where(sub == t1, neg, el)
    m2 = jnp.max(el2, axis=0, keepdims=True)
    t2 = jnp.min(jnp.where(el2 == m2, sub, big), axis=0, keepdims=True)
    e2 = jnp.exp(m2 - m1)
    w1 = p_sel / (1.0 + e2)
    w2 = p_sel * e2 / (1.0 + e2)
    i1 = gidx * EXPERTS_PER_GROUP + t1
    i2 = gidx * EXPERTS_PER_GROUP + t2

    eid = lax.broadcasted_iota(I32, (N_EXPERTS, rows), 0).astype(F32)
    hit1 = eid == i1
    hit2 = eid == i2
    onehot = (hit1 | hit2).astype(BF16)
    before = jnp.dot(onehot, tri_ref[...], preferred_element_type=F32) + cnt_ref[...]
    r1 = jnp.sum(jnp.where(hit1, before, 0.0), axis=0, keepdims=True)
    r2 = jnp.sum(jnp.where(hit2, before, 0.0), axis=0, keepdims=True)
    cnt_ref[...] = cnt_ref[...] + jnp.sum(onehot.astype(F32), axis=1, keepdims=True)

    ri = jnp.where(sub == 0, i1, jnp.where(sub == 1, i2, jnp.where(sub == 2, r1, jnp.where(sub == 3, r2, 0.0))))
    ri_ref[...] = ri.astype(I32).reshape(ri_ref.shape)
    rw_ref[...] = jnp.where(sub == 0, w1, jnp.where(sub == 1, w2, 0.0)).reshape(rw_ref.shape)


def _layer_call(x, s0, h0, pos0, consts, *, bb, tl, chunk):
    bsz, seq, d_model = x.shape
    rows = bb * tl
    rw_width = consts["gret"].shape[-1]
    pw = consts["pscale"].shape[-1]
    dh = rw_width // RET_HEADS

    half = dh // 2
    inv = ROPE_BASE ** (-jnp.arange(half, dtype=F32) / half)
    ang = (pos0 + jnp.arange(seq)).astype(F32)[:, None] * inv[None, :]
    cos, sin = jnp.cos(ang), jnp.sin(ang)
    cosf = jnp.concatenate([cos, cos], axis=-1)
    sinf = jnp.concatenate([-sin, sin], axis=-1)

    lg = jnp.log1p(-jnp.exp2(-5.0 - jnp.arange(RET_HEADS, dtype=F32)))
    idx = jnp.arange(chunk, dtype=F32)
    diff = idx[:, None] - idx[None, :]
    d_intra = jnp.where(diff[None] >= 0, jnp.exp(jnp.maximum(diff, 0.0)[None] * lg[:, None, None]), 0.0)
    d_q = jnp.broadcast_to(jnp.exp((idx + 1.0)[None, :] * lg[:, None])[:, :, None], (RET_HEADS, chunk, dh))
    d_k = jnp.broadcast_to(jnp.exp((chunk - 1.0 - idx)[None, :] * lg[:, None])[:, :, None], (RET_HEADS, chunk, dh))
    d_c = jnp.exp(chunk * lg)
    tri = jnp.triu(jnp.ones((rows, rows), BF16), 1)

    const2 = lambda b, l, *_: (0, 0)
    const3 = lambda b, l, *_: (0, 0, 0)
    grid_spec = pltpu.PrefetchScalarGridSpec(
        num_scalar_prefetch=0,
        grid=(bsz // bb, seq // tl),
        in_specs=[
            pl.BlockSpec(memory_space=pltpu.SMEM),
            pl.BlockSpec((bb, tl, d_model), lambda b, l: (b, l, 0)),
            pl.BlockSpec((bb, RET_HEADS, dh, dh), lambda b, l: (b, 0, 0, 0)),
            pl.BlockSpec((bb, HIST_ROWS, pw), lambda b, l: (b, 0, 0)),
            pl.BlockSpec((tl, dh), lambda b, l: (l, 0)),
            pl.BlockSpec((tl, dh), lambda b, l: (l, 0)),
            pl.BlockSpec((RET_HEADS, chunk, chunk), const3),
            pl.BlockSpec((RET_HEADS, chunk, dh), const3),
            pl.BlockSpec((RET_HEADS, chunk, dh), const3),
            pl.BlockSpec((1, d_model), const2),
            pl.BlockSpec(consts["w_in"].shape, const2),
            pl.BlockSpec((1, rw_width), const2),
            pl.BlockSpec(consts["w_pool"].shape, const3),
            pl.BlockSpec((1, pw), const2),
            pl.BlockSpec(consts["w_out"].shape, const2),
            pl.BlockSpec((1, d_model), const2),
            pl.BlockSpec((d_model, 2 * LANES), const2),
            pl.BlockSpec((rows, rows), const2),
        ],
        out_specs=[
            pl.BlockSpec((bb, tl, d_model), lambda b, l: (b, l, 0)),
            pl.BlockSpec((bb, tl, d_model // 2), lambda b, l: (b, l, 0)),
            pl.BlockSpec((1, 1, EXPERTS_PER_GROUP, rows), lambda b, l: (b, l, 0, 0)),
            pl.BlockSpec((1, 1, EXPERTS_PER_GROUP, rows), lambda b, l: (b, l, 0, 0)),
            pl.BlockSpec((bb, RET_HEADS, dh, dh), lambda b, l: (b, 0, 0, 0)),
            pl.BlockSpec((bb, HIST_ROWS, pw), lambda b, l: (b, 0, 0)),
            pl.BlockSpec((N_EXPERTS, rows), const2),
        ],
        scratch_shapes=[
            pltpu.VMEM((bb, HIST_ROWS + tl, pw), F32),
            pltpu.VMEM((rows, rw_width), BF16),
            pltpu.VMEM((rows, rw_width), F32),
            pltpu.VMEM((rows, rw_width), BF16),
            pltpu.VMEM((rows, rw_width), F32),
            pltpu.VMEM((rows, d_model), BF16),
        ],
    )
    out_shape = [
        jax.ShapeDtypeStruct((bsz, seq, d_model), F32),
        jax.ShapeDtypeStruct((bsz, seq, d_model // 2), U32),
        jax.ShapeDtypeStruct((bsz // bb, seq // tl, EXPERTS_PER_GROUP, rows), I32),
        jax.ShapeDtypeStruct((bsz // bb, seq // tl, EXPERTS_PER_GROUP, rows), F32),
        jax.ShapeDtypeStruct((bsz, RET_HEADS, dh, dh), F32),
        jax.ShapeDtypeStruct((bsz, HIST_ROWS, pw), F32),
        jax.ShapeDtypeStruct((N_EXPERTS, rows), F32),
    ]
    kern = functools.partial(_layer_kernel, bb=bb, tl=tl, chunk=chunk, pos0=pos0)
    return pl.pallas_call(
        kern, grid_spec=grid_spec, out_shape=out_shape, name=f"layer_pos{pos0}",
        compiler_params=pltpu.CompilerParams(
            dimension_semantics=("arbitrary", "arbitrary"), vmem_limit_bytes=VMEM_LIMIT),
    )(d_c, x, s0, h0, cosf, sinf, d_intra, d_q, d_k,
      consts["g1"], consts["w_in"], consts["gret"], consts["w_pool"], consts["pscale"],
      consts["w_out"], consts["g2"], consts["wr"], tri)


def _sc_partition(n_units):
    info = plsc.get_sparse_core_info()
    nc, nw = info.num_cores, info.num_cores * info.num_subcores
    upw = -(-n_units // nw)
    upw += upw % 2
    return nc, nw, upw


def _units_by_worker(idx, n_units, upw, nw):
    idx = jnp.pad(idx.reshape(n_units, SC_UNIT), ((0, nw * upw - n_units), (0, 0)))
    return idx.reshape(upw, nw, SC_UNIT).transpose(1, 0, 2)


def _sc_dispatch(src_a, src_b, idx0, idx1, n_out_rows):
    ta, d = src_a.shape
    tb = src_b.shape[0]
    assert ta % SC_UNIT == 0 and tb % SC_UNIT == 0
    n_units = (ta + tb) // SC_UNIT
    units_a = ta // SC_UNIT
    nc, nw, upw = _sc_partition(n_units)
    idx0 = _units_by_worker(idx0, n_units, upw, nw)
    idx1 = _units_by_worker(idx1, n_units, upw, nw)
    mesh = plsc.VectorSubcoreMesh(core_axis_name="c", subcore_axis_name="s")
    dma = pltpu.SemaphoreType.DMA

    @functools.partial(
        pl.kernel, mesh=mesh,
        out_type=jax.ShapeDtypeStruct((n_out_rows, d), src_a.dtype),
        scratch_types=[
            pltpu.VMEM((upw, SC_UNIT), I32),
            pltpu.VMEM((upw, SC_UNIT), I32),
            pltpu.VMEM((SC_UNIT, d), src_a.dtype),
            pltpu.VMEM((SC_UNIT, d), src_a.dtype),
            dma, dma, dma, dma, dma, dma,
        ],
    )
    def k(a_hbm, b_hbm, i0_hbm, i1_hbm, out_hbm, i0_v, i1_v, rows0, rows1, l0, l1, p0, p1, q0, q1):
        wid = lax.axis_index("s") * nc + lax.axis_index("c")
        pltpu.sync_copy(i0_hbm.at[wid], i0_v)
        pltpu.sync_copy(i1_hbm.at[wid], i1_v)
        rows, lsem, psem, qsem = (rows0, rows1), (l0, l1), (p0, p1), (q0, q1)

        def live(j):
            return j * nw + wid < n_units

        def load(j, b, op):
            unit = j * nw + wid

            @pl.when(live(j) & (unit < units_a))
            def _():
                op(pltpu.make_async_copy(
                    a_hbm.at[pl.ds(pl.multiple_of(unit * SC_UNIT, 8), SC_UNIT)], rows[b], lsem[b]))

            @pl.when(live(j) & (unit >= units_a))
            def _():
                op(pltpu.make_async_copy(
                    b_hbm.at[pl.ds(pl.multiple_of((unit - units_a) * SC_UNIT, 8), SC_UNIT)], rows[b], lsem[b]))

        def scatter(j, b, op):
            @pl.when(live(j))
            def _():
                op(pltpu.make_async_copy(rows[b], out_hbm.at[i0_v.at[j]], psem[b]))
                op(pltpu.make_async_copy(rows[b], out_hbm.at[i1_v.at[j]], qsem[b]))

        start = lambda c: c.start()
        wait = lambda c: c.wait()
        load(0, 0, start)

        @pl.loop(0, upw, step=2)
        def _(j):
            @pl.when(j > 0)
            def _():
                scatter(j - 1, 1, wait)
            load(j + 1, 1, start)
            load(j, 0, wait)
            scatter(j, 0, start)
            scatter(j, 0, wait)

            @pl.when(j + 2 < upw)
            def _():
                load(j + 2, 0, start)
            load(j + 1, 1, wait)
            scatter(j + 1, 1, start)

        scatter(upw - 1, 1, wait)

    return k(src_a, src_b, idx0, idx1)


def _sc_gather(table, idx):
    n = idx.shape[0]
    d = table.shape[1]
    assert n % SC_UNIT == 0
    n_units = n // SC_UNIT
    nc, nw, upw = _sc_partition(n_units)
    idx = _units_by_worker(idx, n_units, upw, nw)
    mesh = plsc.VectorSubcoreMesh(core_axis_name="c", subcore_axis_name="s")
    dma = pltpu.SemaphoreType.DMA

    @functools.partial(
        pl.kernel, mesh=mesh,
        out_type=jax.ShapeDtypeStruct((n, d), table.dtype),
        scratch_types=[
            pltpu.VMEM((upw, SC_UNIT), I32),
            pltpu.VMEM((SC_UNIT, d), table.dtype),
            pltpu.VMEM((SC_UNIT, d), table.dtype),
            dma, dma, dma, dma,
        ],
    )
    def k(t_hbm, i_hbm, out_hbm, i_v, rows0, rows1, g0, g1, w0, w1):
        wid = lax.axis_index("s") * nc + lax.axis_index("c")
        pltpu.sync_copy(i_hbm.at[wid], i_v)
        rows, gsem, wsem = (rows0, rows1), (g0, g1), (w0, w1)

        def live(j):
            return j * nw + wid < n_units

        def gather(j, b, op):
            @pl.when(live(j))
            def _():
                op(pltpu.make_async_copy(t_hbm.at[i_v.at[j]], rows[b], gsem[b]))

        def write(j, b, op):
            @pl.when(live(j))
            def _():
                op(pltpu.make_async_copy(
                    rows[b], out_hbm.at[pl.ds(pl.multiple_of((j * nw + wid) * SC_UNIT, 8), SC_UNIT)], wsem[b]))

        start = lambda c: c.start()
        wait = lambda c: c.wait()
        gather(0, 0, start)

        @pl.loop(0, upw, step=2)
        def _(j):
            @pl.when(j > 0)
            def _():
                write(j - 1, 1, wait)
            gather(j + 1, 1, start)
            gather(j, 0, wait)
            write(j, 0, start)
            write(j, 0, wait)

            @pl.when(j + 2 < upw)
            def _():
                gather(j + 2, 0, start)
            gather(j + 1, 1, wait)
            write(j + 1, 1, start)

        write(upw - 1, 1, wait)

    return k(table, idx)


def _moe_kernel(te_ref, tv_ref, xs_ref, wg_ref, wu_ref, wd_ref, ys_ref, wgu_s, wd_s):
    i = pl.program_id(0)
    e = te_ref[i]
    valid = tv_ref[i]
    hidden = wd_s.shape[0]
    new_expert = (i == 0) | (te_ref[jnp.maximum(i - 1, 0)] != e)

    @pl.when(new_expert)
    def _():
        wgu_s[:, 0:hidden] = wg_ref[0].astype(BF16)
        wgu_s[:, hidden:2 * hidden] = wu_ref[0].astype(BF16)
        wd_s[...] = wd_ref[0].astype(BF16)

    @pl.when(valid > 0)
    def _():
        half = xs_ref.shape[-1]
        row = lax.broadcasted_iota(I32, xs_ref.shape, 0)
        x_lo, x_hi = _unpack_bf16_pair(jnp.where(row < valid, xs_ref[...], jnp.uint32(0)))
        ab = (jnp.dot(x_lo.astype(BF16), wgu_s[0:half, :], preferred_element_type=F32)
              + jnp.dot(x_hi.astype(BF16), wgu_s[half:2 * half, :], preferred_element_type=F32))
        a = ab[:, 0:hidden]
        he = a * _sigmoid(a) * ab[:, hidden:2 * hidden]
        y = jnp.dot(he.astype(BF16), wd_s[...], preferred_element_type=F32)
        ys_ref[...] = _pack_bf16_pair(y[:, 0:half], y[:, half:2 * half])

    @pl.when(valid <= 0)
    def _():
        ys_ref[...] = jnp.zeros_like(ys_ref)


def _moe_call(tile_expert, tile_valid, xs, w_g, w_u, w_d):
    n_rows, half = xs.shape
    d_model, hidden = w_g.shape[-2:]
    n_tiles = n_rows // MOE_TILE
    grid_spec = pltpu.PrefetchScalarGridSpec(
        num_scalar_prefetch=2,
        grid=(n_tiles,),
        in_specs=[
            pl.BlockSpec((MOE_TILE, half), lambda i, te, tv: (i, 0)),
            pl.BlockSpec((1, d_model, hidden), lambda i, te, tv: (te[i], 0, 0)),
            pl.BlockSpec((1, d_model, hidden), lambda i, te, tv: (te[i], 0, 0)),
            pl.BlockSpec((1, hidden, d_model), lambda i, te, tv: (te[i], 0, 0)),
        ],
        out_specs=pl.BlockSpec((MOE_TILE, half), lambda i, te, tv: (i, 0)),
        scratch_shapes=[
            pltpu.VMEM((d_model, 2 * hidden), BF16),
            pltpu.VMEM((hidden, d_model), BF16),
        ],
    )
    return pl.pallas_call(
        _moe_kernel, grid_spec=grid_spec,
        out_shape=jax.ShapeDtypeStruct((n_rows, half), U32), name="moe_experts",
        compiler_params=pltpu.CompilerParams(
            dimension_semantics=("arbitrary",), vmem_limit_bytes=VMEM_LIMIT),
    )(tile_expert, tile_valid, xs, w_g, w_u, w_d)


def _combine_kernel(x1_ref, y0_ref, y1_ref, rw_ref, gf_ref, out_ref):
    rw = rw_ref[...]
    a_lo, a_hi = _unpack_bf16_pair(y0_ref[0])
    b_lo, b_hi = _unpack_bf16_pair(y1_ref[0])
    w0, w1 = rw[:, 0:1], rw[:, 1:2]
    moe = jnp.concatenate([w0 * a_lo + w1 * b_lo, w0 * a_hi + w1 * b_hi], axis=-1)
    out_ref[...] = _rms(x1_ref[...] + moe, gf_ref[...])


def _combine_call(x1, yg, rw, gf, row_offset, tr=512):
    t, d_model = x1.shape
    half = yg.shape[-1]
    off = row_offset // tr
    assert t % tr == 0 and row_offset % tr == 0
    return pl.pallas_call(
        _combine_kernel,
        grid=(t // tr,),
        in_specs=[
            pl.BlockSpec((tr, d_model), lambda i: (i, 0)),
            pl.BlockSpec((1, tr, half), lambda i: (0, off + i, 0)),
            pl.BlockSpec((1, tr, half), lambda i: (1, off + i, 0)),
            pl.BlockSpec((tr, rw.shape[1]), lambda i: (i, 0)),
            pl.BlockSpec((1, d_model), lambda i: (0, 0)),
        ],
        out_specs=pl.BlockSpec((tr, d_model), lambda i: (i, 0)),
        out_shape=jax.ShapeDtypeStruct((t, d_model), F32), name=f"combine_off{row_offset}",
        compiler_params=pltpu.CompilerParams(
            dimension_semantics=("arbitrary",), vmem_limit_bytes=VMEM_LIMIT),
    )(x1, yg, yg, rw, gf)


def _one_layer(xp, xs, s_ret, c_pool, norm1_g, w_in, ret_norm_g, w_pool, pool_scale, w_out, norm2_g,
               w_rg, w_re, w_g, w_u, w_d, final_g, past_len, final):
    bp, seq, d_model = xp.shape
    bs, dseq, _ = xs.shape
    rw_width = ret_norm_g.shape[-1]
    pw = pool_scale.shape[-1]
    dh = rw_width // RET_HEADS
    tp, ts = bp * seq, bs * dseq
    t_all = tp + ts

    w_r = jnp.concatenate(
        [w_re, w_rg, jnp.zeros((d_model, LANES - N_EXPERTS - N_EXPERT_GROUPS), F32)], axis=1)
    wr_hi = w_r.astype(BF16)
    wr = jnp.concatenate([wr_hi, (w_r - wr_hi.astype(F32)).astype(BF16)], axis=1)
    consts = dict(
        g1=norm1_g.reshape(1, d_model), w_in=w_in.astype(BF16), gret=ret_norm_g.reshape(1, rw_width),
        w_pool=w_pool.astype(BF16), pscale=pool_scale.reshape(1, pw), w_out=w_out.astype(BF16),
        g2=norm2_g.reshape(1, d_model), wr=wr)

    s0p = jnp.zeros((bp, RET_HEADS, dh, dh), F32)
    h0p = jnp.zeros((bp, HIST_ROWS, pw), F32)
    h0s = jnp.pad(c_pool, ((0, 0), (HIST_ROWS - POOL_HIST, 0), (0, 0)))

    x1p, h2p, rip, rwp, st_p, hist_p, cnt_p = _layer_call(
        xp, s0p, h0p, 0, consts, bb=1, tl=512, chunk=256)
    x1s, h2s, ris, rws, st_s, hist_s, cnt_s = _layer_call(
        xs, s_ret, h0s, past_len, consts, bb=bs, tl=dseq, chunk=min(64, dseq))

    def fields(r, t):
        return jnp.moveaxis(r, 2, 0).reshape(r.shape[2], t)

    rip, ris, rwp, rws = fields(rip, tp), fields(ris, ts), fields(rwp, tp), fields(rws, ts)
    cnt_p = cnt_p[:, 0].astype(I32)
    cnt_s = cnt_s[:, 0].astype(I32)
    total = cnt_p + cnt_s
    padded = ((total + MOE_TILE - 1) // MOE_TILE) * MOE_TILE
    ends = jnp.cumsum(padded)
    starts = ends - padded
    experts = jnp.arange(N_EXPERTS, dtype=I32)[:, None]

    def positions(ids, ranks, base):
        return ranks + jnp.sum(jnp.where(ids[:, None, :] == experts[None], base[None, :, None], 0), axis=1)

    pos = jnp.concatenate([positions(rip[0:2], rip[2:4], starts),
                           positions(ris[0:2], ris[2:4], starts + cnt_p)], axis=1)

    n_rows = ((2 * t_all + N_EXPERTS * (MOE_TILE - 1)) // MOE_TILE) * MOE_TILE
    n_tiles = n_rows // MOE_TILE
    tile_start = jnp.arange(n_tiles, dtype=I32) * MOE_TILE
    te = jnp.minimum(jnp.sum(tile_start[:, None] >= ends[None, :], axis=-1), N_EXPERTS - 1).astype(I32)
    tv = jnp.clip(jnp.take(starts + total, te) - tile_start, 0, MOE_TILE).astype(I32)
    last_used = jnp.max(jnp.where(total > 0, jnp.arange(N_EXPERTS, dtype=I32), 0))
    te = jnp.where(tile_start >= ends[-1], last_used, te)

    xs_sorted = _sc_dispatch(h2p.reshape(tp, d_model // 2), h2s.reshape(ts, d_model // 2),
                             pos[0], pos[1], n_rows)
    ys_sorted = _moe_call(te, tv, xs_sorted, w_g, w_u, w_d)
    yg = _sc_gather(ys_sorted, pos.reshape(2 * t_all))
    yg = yg.reshape(2, t_all, d_model // 2)

    gf = final_g.reshape(1, d_model) if final else None
    yp = _combine_call(x1p.reshape(tp, d_model), yg, rwp[0:2].T, gf, 0)
    ysm = _combine_call(x1s.reshape(ts, d_model), yg, rws[0:2].T, gf, tp)
    return (yp.reshape(bp, seq, d_model), ysm.reshape(bs, dseq, d_model),
            st_p, hist_p[:, HIST_ROWS - POOL_HIST:], st_s, hist_s[:, HIST_ROWS - POOL_HIST:])


def kernel(x_prompt, x_sample, state_ret, cache_pool, norm1_g, w_in, ret_norm_g, w_pool, pool_scale, w_out,
           norm2_g, w_router_group, w_router_expert, w_exp_gate, w_exp_up, w_exp_down, final_norm_g):
    depth = w_in.shape[0]
    assert depth == 1, "the final RMSNorm is fused into the last layer's combine kernel"
    yp, ys, s_p, h_p, s_s, h_s = _one_layer(
        x_prompt, x_sample, state_ret[0], cache_pool[0], norm1_g[0], w_in[0], ret_norm_g[0], w_pool[0],
        pool_scale[0], w_out[0], norm2_g[0], w_router_group[0], w_router_expert[0],
        w_exp_gate[0], w_exp_up[0], w_exp_down[0], final_norm_g, PAST_LEN, True)
    return (yp, ys, s_p[None], h_p[None], s_s[None], h_s[None])
```

```python
import functools

import jax
import jax.numpy as jnp
from jax import lax
from jax.experimental import pallas as pl
from jax.experimental.pallas import tpu as pltpu
from jax.experimental.pallas import tpu_sc as plsc

F32 = jnp.float32
BF16 = jnp.bfloat16
I32 = jnp.int32
U32 = jnp.uint32

EPS = 1e-6
ROPE_BASE = 10000.0
RET_HEADS = 4
POOL_WINDOWS = (2, 4, 8, 16)
POOL_HIST = max(POOL_WINDOWS) - 1
N_EXPERT_GROUPS = 4
EXPERTS_PER_GROUP = 8
N_EXPERTS = N_EXPERT_GROUPS * EXPERTS_PER_GROUP
EXPERT_SHIFT = EXPERTS_PER_GROUP.bit_length() - 1
PAST_LEN = 1024

LANES = 128
HIST_ROWS = 16
MOE_TILE = 256
SC_UNIT = 32
VMEM_LIMIT = 56 * 1024 * 1024


def _rms(x, g):
    return x * lax.rsqrt(jnp.mean(x * x, axis=-1, keepdims=True) + EPS) * g


def _sigmoid(x):
    return 1.0 / (1.0 + jnp.exp(-x))


def _pack_bf16_pair(lo, hi):
    lo_b = lax.bitcast_convert_type(lo.astype(BF16).astype(F32), U32)
    hi_b = lax.bitcast_convert_type(hi.astype(BF16).astype(F32), U32)
    return hi_b | (lo_b >> 16)


def _unpack_bf16_pair(p):
    lo = lax.bitcast_convert_type(p << 16, F32)
    hi = lax.bitcast_convert_type(p & jnp.uint32(0xFFFF0000), F32)
    return lo, hi


def _layer_kernel(dc_ref, x_ref, s0_ref, h0_ref, cos_ref, sin_ref, dintra_ref, dq_ref, dk_ref,
                  g1_ref, win_ref, gret_ref, wpool_ref, pscale_ref, wout_ref, g2_ref,
                  wr_ref, tri_ref,
                  x1_ref, h2_ref, ri_ref, rw_ref, st_ref, hist_ref, cnt_ref,
                  ue_ref, q_ref, k_ref, v_ref, o_ref, a_ref,
                  *, bb, tl, chunk, pos0):
    b_idx = pl.program_id(0)
    l_idx = pl.program_id(1)
    rows = bb * tl
    d_model = x_ref.shape[-1]
    rw_width = q_ref.shape[-1]
    dh = rw_width // RET_HEADS
    pw = ue_ref.shape[-1]
    gw = pw // len(POOL_WINDOWS)
    n_chunks = tl // chunk

    @pl.when(l_idx == 0)
    def _():
        st_ref[...] = s0_ref[...]
        ue_ref[:, 0:HIST_ROWS, :] = h0_ref[...]

    @pl.when((l_idx == 0) & (b_idx == 0))
    def _():
        cnt_ref[...] = jnp.zeros_like(cnt_ref)

    x = x_ref[...].reshape(rows, d_model)
    hb = _rms(x, g1_ref[...]).astype(BF16)

    def project(c0, c1):
        return jnp.dot(hb, win_ref[:, c0:c1], preferred_element_type=F32)

    proj = project(0, 2 * rw_width)

    cosf = cos_ref[...][None]
    sinf = sin_ref[...][None]
    k_scale = dh ** -0.5
    for hh in range(RET_HEADS):
        qh = proj[:, hh * dh:(hh + 1) * dh]
        kh = proj[:, rw_width + hh * dh:rw_width + (hh + 1) * dh]
        qr = (qh.reshape(bb, tl, dh) * cosf
              + pltpu.roll(qh, dh // 2, 1).reshape(bb, tl, dh) * sinf).reshape(rows, dh)
        kr = (kh.reshape(bb, tl, dh) * cosf
              + pltpu.roll(kh, dh // 2, 1).reshape(bb, tl, dh) * sinf).reshape(rows, dh)
        q_ref[:, hh * dh:(hh + 1) * dh] = qr.astype(BF16)
        k_ref[:, hh * dh:(hh + 1) * dh] = kr * k_scale
    v_ref[...] = project(2 * rw_width, 3 * rw_width).astype(BF16)
    gate = project(3 * rw_width, 4 * rw_width)
    u = project(4 * rw_width, 4 * rw_width + pw)

    def ret_block(b, c):
        r0 = b * tl + c * chunk
        if not isinstance(r0, int):
            r0 = pl.multiple_of(r0, chunk)
        for hh in range(RET_HEADS):
            cs = slice(hh * dh, (hh + 1) * dh)
            qc = q_ref[pl.ds(r0, chunk), cs]
            kf = k_ref[pl.ds(r0, chunk), cs]
            vc = v_ref[pl.ds(r0, chunk), cs]
            s_old = st_ref[b, hh]
            sc = lax.dot_general(qc, kf.astype(BF16), (((1,), (1,)), ((), ())),
                                 preferred_element_type=F32) * dintra_ref[hh]
            o = (jnp.dot(sc.astype(BF16), vc, preferred_element_type=F32)
                 + dq_ref[hh] * jnp.dot(qc, s_old.astype(BF16), preferred_element_type=F32))
            kd = (kf * dk_ref[hh]).astype(BF16)
            s_new = dc_ref[hh] * s_old + lax.dot_general(
                kd, vc, (((0,), (0,)), ((), ())), preferred_element_type=F32)
            st_ref[b, hh] = s_new
            o_ref[pl.ds(r0, chunk), cs] = o

    if bb * n_chunks <= 4:
        for b in range(bb):
            for c in range(n_chunks):
                ret_block(b, c)
    else:
        def body(i, carry):
            ret_block(i // n_chunks, i % n_chunks)
            return carry
        lax.fori_loop(0, bb * n_chunks, body, 0)

    for hh in range(RET_HEADS):
        cs = slice(hh * dh, (hh + 1) * dh)
        oh = o_ref[:, cs]
        mu = jnp.mean(oh, axis=-1, keepdims=True)
        oc = oh - mu
        var = jnp.mean(oc * oc, axis=-1, keepdims=True)
        y = oc * lax.rsqrt(var + EPS) * gret_ref[:, cs]
        g = gate[:, cs]
        a_ref[:, cs] = (g * _sigmoid(g) * y).astype(BF16)

    ue_ref[:, HIST_ROWS:HIST_ROWS + tl, :] = u.reshape(bb, tl, pw)
    pos = pos0 + l_idx * tl + lax.broadcasted_iota(I32, (1, tl, 1), 1)
    for gi, w in enumerate(POOL_WINDOWS):
        cs = slice(gi * gw, (gi + 1) * gw)
        acc = ue_ref[:, HIST_ROWS:HIST_ROWS + tl, cs]
        for j in range(1, w):
            acc = acc + ue_ref[:, HIST_ROWS - j:HIST_ROWS - j + tl, cs]
        inv_cnt = 1.0 / jnp.minimum(pos + 1, w).astype(F32)
        p = (acc * inv_cnt).reshape(rows, gw) - u[:, cs]
        z = jnp.dot(p.astype(BF16), wpool_ref[gi], preferred_element_type=F32) * pscale_ref[:, cs]
        a_ref[:, rw_width + gi * gw:rw_width + (gi + 1) * gw] = z.astype(BF16)
    tail = ue_ref[:, tl:tl + HIST_ROWS, :]
    ue_ref[:, 0:HIST_ROWS, :] = tail
    hist_ref[...] = tail

    x1 = x + jnp.dot(a_ref[...], wout_ref[...], preferred_element_type=F32)
    x1_ref[...] = x1.reshape(bb, tl, d_model)
    h2 = _rms(x1, g2_ref[...])
    h2_ref[...] = _pack_bf16_pair(h2[:, 0:d_model // 2], h2[:, d_model // 2:]).reshape(bb, tl, d_model // 2)

    h2_hi = h2.astype(BF16)
    h2_lo = (h2 - h2_hi.astype(F32)).astype(BF16)
    two = jnp.dot(h2_hi, wr_ref[...], preferred_element_type=F32)
    logits = (two[:, 0:LANES] + two[:, LANES:2 * LANES]
              + jnp.dot(h2_lo, wr_ref[:, 0:LANES], preferred_element_type=F32))
    lt = logits.T
    neg = jnp.float32(-jnp.inf)
    big = jnp.float32(1e9)
    sub = lax.broadcasted_iota(I32, (EXPERTS_PER_GROUP, rows), 0).astype(F32)
    gl = jnp.where(sub < N_EXPERT_GROUPS, lt[N_EXPERTS:N_EXPERTS + EXPERTS_PER_GROUP], neg)
    gmax = jnp.max(gl, axis=0, keepdims=True)
    gidx = jnp.min(jnp.where(gl == gmax, sub, big), axis=0, keepdims=True)
    p_sel = 1.0 / jnp.sum(jnp.exp(gl - gmax), axis=0, keepdims=True)
    el = lt[0:EXPERTS_PER_GROUP]
    for g in range(1, N_EXPERT_GROUPS):
        el = jnp.where(gidx == g, lt[g * EXPERTS_PER_GROUP:(g + 1) * EXPERTS_PER_GROUP], el)
    m1 = jnp.max(el, axis=0, keepdims=True)
    t1 = jnp.min(jnp.where(el == m1, sub, big), axis=0, keepdims=True)
    el2 = jnp.where(sub == t1, neg, el)
    m2 = jnp.max(el2, axis=0, keepdims=True)
    t2 = jnp.min(jnp.where(el2 == m2, sub, big), axis=0, keepdims=True)
    e2 = jnp.exp(m2 - m1)
    w1 = p_sel / (1.0 + e2)
    w2 = p_sel * e2 / (1.0 + e2)
    i1 = gidx * EXPERTS_PER_GROUP + t1
    i2 = gidx * EXPERTS_PER_GROUP + t2

    eid = lax.broadcasted_iota(I32, (N_EXPERTS, rows), 0).astype(F32)
    hit1 = eid == i1
    hit2 = eid == i2
    onehot = (hit1 | hit2).astype(BF16)
    before = jnp.dot(onehot, tri_ref[...], preferred_element_type=F32) + cnt_ref[...]
    r1 = jnp.sum(jnp.where(hit1, before, 0.0), axis=0, keepdims=True)
    r2 = jnp.sum(jnp.where(hit2, before, 0.0), axis=0, keepdims=True)
    cnt_ref[...] = cnt_ref[...] + jnp.sum(onehot.astype(F32), axis=1, keepdims=True)

    ri = jnp.where(sub == 0, i1, jnp.where(sub == 1, i2, jnp.where(sub == 2, r1, jnp.where(sub == 3, r2, 0.0))))
    ri_ref[...] = ri.astype(I32).reshape(ri_ref.shape)
    rw_ref[...] = jnp.where(sub == 0, w1, jnp.where(sub == 1, w2, 0.0)).reshape(rw_ref.shape)


def _layer_call(x, s0, h0, pos0, consts, *, bb, tl, chunk):
    bsz, seq, d_model = x.shape
    rows = bb * tl
    rw_width = consts["gret"].shape[-1]
    pw = consts["pscale"].shape[-1]
    dh = rw_width // RET_HEADS

    half = dh // 2
    inv = ROPE_BASE ** (-jnp.arange(half, dtype=F32) / half)
    ang = (pos0 + jnp.arange(seq)).astype(F32)[:, None] * inv[None, :]
    cos, sin = jnp.cos(ang), jnp.sin(ang)
    cosf = jnp.concatenate([cos, cos], axis=-1)
    sinf = jnp.concatenate([-sin, sin], axis=-1)

    lg = jnp.log1p(-jnp.exp2(-5.0 - jnp.arange(RET_HEADS, dtype=F32)))
    idx = jnp.arange(chunk, dtype=F32)
    diff = idx[:, None] - idx[None, :]
    d_intra = jnp.where(diff[None] >= 0, jnp.exp(jnp.maximum(diff, 0.0)[None] * lg[:, None, None]), 0.0)
    d_q = jnp.broadcast_to(jnp.exp((idx + 1.0)[None, :] * lg[:, None])[:, :, None], (RET_HEADS, chunk, dh))
    d_k = jnp.broadcast_to(jnp.exp((chunk - 1.0 - idx)[None, :] * lg[:, None])[:, :, None], (RET_HEADS, chunk, dh))
    d_c = jnp.exp(chunk * lg)
    tri = jnp.triu(jnp.ones((rows, rows), BF16), 1)

    const2 = lambda b, l, *_: (0, 0)
    const3 = lambda b, l, *_: (0, 0, 0)
    grid_spec = pltpu.PrefetchScalarGridSpec(
        num_scalar_prefetch=0,
        grid=(bsz // bb, seq // tl),
        in_specs=[
            pl.BlockSpec(memory_space=pltpu.SMEM),
            pl.BlockSpec((bb, tl, d_model), lambda b, l: (b, l, 0)),
            pl.BlockSpec((bb, RET_HEADS, dh, dh), lambda b, l: (b, 0, 0, 0)),
            pl.BlockSpec((bb, HIST_ROWS, pw), lambda b, l: (b, 0, 0)),
            pl.BlockSpec((tl, dh), lambda b, l: (l, 0)),
            pl.BlockSpec((tl, dh), lambda b, l: (l, 0)),
            pl.BlockSpec((RET_HEADS, chunk, chunk), const3),
            pl.BlockSpec((RET_HEADS, chunk, dh), const3),
            pl.BlockSpec((RET_HEADS, chunk, dh), const3),
            pl.BlockSpec((1, d_model), const2),
            pl.BlockSpec(consts["w_in"].shape, const2),
            pl.BlockSpec((1, rw_width), const2),
            pl.BlockSpec(consts["w_pool"].shape, const3),
            pl.BlockSpec((1, pw), const2),
            pl.BlockSpec(consts["w_out"].shape, const2),
            pl.BlockSpec((1, d_model), const2),
            pl.BlockSpec((d_model, 2 * LANES), const2),
            pl.BlockSpec((rows, rows), const2),
        ],
        out_specs=[
            pl.BlockSpec((bb, tl, d_model), lambda b, l: (b, l, 0)),
            pl.BlockSpec((bb, tl, d_model // 2), lambda b, l: (b, l, 0)),
            pl.BlockSpec((1, 1, EXPERTS_PER_GROUP, rows), lambda b, l: (b, l, 0, 0)),
            pl.BlockSpec((1, 1, EXPERTS_PER_GROUP, rows), lambda b, l: (b, l, 0, 0)),
            pl.BlockSpec((bb, RET_HEADS, dh, dh), lambda b, l: (b, 0, 0, 0)),
            pl.BlockSpec((bb, HIST_ROWS, pw), lambda b, l: (b, 0, 0)),
            pl.BlockSpec((N_EXPERTS, rows), const2),
        ],
        scratch_shapes=[
            pltpu.VMEM((bb, HIST_ROWS + tl, pw), F32),
            pltpu.VMEM((rows, rw_width), BF16),
            pltpu.VMEM((rows, rw_width), F32),
            pltpu.VMEM((rows, rw_width), BF16),
            pltpu.VMEM((rows, rw_width), F32),
            pltpu.VMEM((rows, d_model), BF16),
        ],
    )
    out_shape = [
        jax.ShapeDtypeStruct((bsz, seq, d_model), F32),
        jax.ShapeDtypeStruct((bsz, seq, d_model // 2), U32),
        jax.ShapeDtypeStruct((bsz // bb, seq // tl, EXPERTS_PER_GROUP, rows), I32),
        jax.ShapeDtypeStruct((bsz // bb, seq // tl, EXPERTS_PER_GROUP, rows), F32),
        jax.ShapeDtypeStruct((bsz, RET_HEADS, dh, dh), F32),
        jax.ShapeDtypeStruct((bsz, HIST_ROWS, pw), F32),
        jax.ShapeDtypeStruct((N_EXPERTS, rows), F32),
    ]
    kern = functools.partial(_layer_kernel, bb=bb, tl=tl, chunk=chunk, pos0=pos0)
    return pl.pallas_call(
        kern, grid_spec=grid_spec, out_shape=out_shape, name=f"layer_pos{pos0}",
        compiler_params=pltpu.CompilerParams(
            dimension_semantics=("arbitrary", "arbitrary"), vmem_limit_bytes=VMEM_LIMIT),
    )(d_c, x, s0, h0, cosf, sinf, d_intra, d_q, d_k,
      consts["g1"], consts["w_in"], consts["gret"], consts["w_pool"], consts["pscale"],
      consts["w_out"], consts["g2"], consts["wr"], tri)


def _sc_partition(n_units):
    info = plsc.get_sparse_core_info()
    nc, nw = info.num_cores, info.num_cores * info.num_subcores
    upw = -(-n_units // nw)
    upw += upw % 2
    return nc, nw, upw


def _units_by_worker(idx, n_units, upw, nw):
    idx = jnp.pad(idx.reshape(n_units, SC_UNIT), ((0, nw * upw - n_units), (0, 0)))
    return idx.reshape(upw, nw, SC_UNIT).transpose(1, 0, 2)


def _sc_dispatch(src_a, src_b, idx0, idx1, n_out_rows):
    ta, d = src_a.shape
    tb = src_b.shape[0]
    assert ta % SC_UNIT == 0 and tb % SC_UNIT == 0
    n_units = (ta + tb) // SC_UNIT
    units_a = ta // SC_UNIT
    nc, nw, upw = _sc_partition(n_units)
    idx0 = _units_by_worker(idx0, n_units, upw, nw)
    idx1 = _units_by_worker(idx1, n_units, upw, nw)
    mesh = plsc.VectorSubcoreMesh(core_axis_name="c", subcore_axis_name="s")
    dma = pltpu.SemaphoreType.DMA

    @functools.partial(
        pl.kernel, mesh=mesh,
        out_type=jax.ShapeDtypeStruct((n_out_rows, d), src_a.dtype),
        scratch_types=[
            pltpu.VMEM((upw, SC_UNIT), I32),
            pltpu.VMEM((upw, SC_UNIT), I32),
            pltpu.VMEM((SC_UNIT, d), src_a.dtype),
            pltpu.VMEM((SC_UNIT, d), src_a.dtype),
            dma, dma, dma, dma, dma, dma,
        ],
    )
    def k(a_hbm, b_hbm, i0_hbm, i1_hbm, out_hbm, i0_v, i1_v, rows0, rows1, l0, l1, p0, p1, q0, q1):
        wid = lax.axis_index("s") * nc + lax.axis_index("c")
        pltpu.sync_copy(i0_hbm.at[wid], i0_v)
        pltpu.sync_copy(i1_hbm.at[wid], i1_v)
        rows, lsem, psem, qsem = (rows0, rows1), (l0, l1), (p0, p1), (q0, q1)

        def live(j):
            return j * nw + wid < n_units

        def load(j, b, op):
            unit = j * nw + wid

            @pl.when(live(j) & (unit < units_a))
            def _():
                op(pltpu.make_async_copy(
                    a_hbm.at[pl.ds(pl.multiple_of(unit * SC_UNIT, 8), SC_UNIT)], rows[b], lsem[b]))

            @pl.when(live(j) & (unit >= units_a))
            def _():
                op(pltpu.make_async_copy(
                    b_hbm.at[pl.ds(pl.multiple_of((unit - units_a) * SC_UNIT, 8), SC_UNIT)], rows[b], lsem[b]))

        def scatter(j, b, op):
            @pl.when(live(j))
            def _():
                op(pltpu.make_async_copy(rows[b], out_hbm.at[i0_v.at[j]], psem[b]))
                op(pltpu.make_async_copy(rows[b], out_hbm.at[i1_v.at[j]], qsem[b]))

        start = lambda c: c.start()
        wait = lambda c: c.wait()
        load(0, 0, start)

        @pl.loop(0, upw, step=2)
        def _(j):
            @pl.when(j > 0)
            def _():
                scatter(j - 1, 1, wait)
            load(j + 1, 1, start)
            load(j, 0, wait)
            scatter(j, 0, start)
            scatter(j, 0, wait)

            @pl.when(j + 2 < upw)
            def _():
                load(j + 2, 0, start)
            load(j + 1, 1, wait)
            scatter(j + 1, 1, start)

        scatter(upw - 1, 1, wait)

    return k(src_a, src_b, idx0, idx1)


def _sc_gather(table, idx):
    n = idx.shape[0]
    d = table.shape[1]
    assert n % SC_UNIT == 0
    n_units = n // SC_UNIT
    nc, nw, upw = _sc_partition(n_units)
    idx = _units_by_worker(idx, n_units, upw, nw)
    mesh = plsc.VectorSubcoreMesh(core_axis_name="c", subcore_axis_name="s")
    dma = pltpu.SemaphoreType.DMA

    @functools.partial(
        pl.kernel, mesh=mesh,
        out_type=jax.ShapeDtypeStruct((n, d), table.dtype),
        scratch_types=[
            pltpu.VMEM((upw, SC_UNIT), I32),
            pltpu.VMEM((SC_UNIT, d), table.dtype),
            pltpu.VMEM((SC_UNIT, d), table.dtype),
            dma, dma, dma, dma,
        ],
    )
    def k(t_hbm, i_hbm, out_hbm, i_v, rows0, rows1, g0, g1, w0, w1):
        wid = lax.axis_index("s") * nc + lax.axis_index("c")
        pltpu.sync_copy(i_hbm.at[wid], i_v)
        rows, gsem, wsem = (rows0, rows1), (g0, g1), (w0, w1)

        def live(j):
            return j * nw + wid < n_units

        def gather(j, b, op):
            @pl.when(live(j))
            def _():
                op(pltpu.make_async_copy(t_hbm.at[i_v.at[j]], rows[b], gsem[b]))

        def write(j, b, op):
            @pl.when(live(j))
            def _():
                op(pltpu.make_async_copy(
                    rows[b], out_hbm.at[pl.ds(pl.multiple_of((j * nw + wid) * SC_UNIT, 8), SC_UNIT)], wsem[b]))

        start = lambda c: c.start()
        wait = lambda c: c.wait()
        gather(0, 0, start)

        @pl.loop(0, upw, step=2)
        def _(j):
            @pl.when(j > 0)
            def _():
                write(j - 1, 1, wait)
            gather(j + 1, 1, start)
            gather(j, 0, wait)
            write(j, 0, start)
            write(j, 0, wait)

            @pl.when(j + 2 < upw)
            def _():
                gather(j + 2, 0, start)
            gather(j + 1, 1, wait)
            write(j + 1, 1, start)

        write(upw - 1, 1, wait)

    return k(table, idx)


def _moe_kernel(start_ref, count_ref, xs_hbm, wg_ref, wu_ref, wd_ref, ys_hbm,
                wgu_s, wd_s, xbuf, ybuf, sem_in, sem_out):
    e = pl.program_id(0)
    start = start_ref[e]
    count = count_ref[e]
    n_tiles = (count + MOE_TILE - 1) // MOE_TILE
    hidden = wd_s.shape[0]
    half = xbuf.shape[-1]

    def rows_of(t):
        return pl.ds(pl.multiple_of(start + t * MOE_TILE, MOE_TILE), MOE_TILE)

    def copy_in(t, slot):
        return pltpu.make_async_copy(xs_hbm.at[rows_of(t)], xbuf.at[slot], sem_in.at[slot])

    def copy_out(t, slot):
        return pltpu.make_async_copy(ybuf.at[slot], ys_hbm.at[rows_of(t)], sem_out.at[slot])

    @pl.when(n_tiles > 0)
    def _():
        copy_in(0, 0).start()
        wgu_s[:, 0:hidden] = wg_ref[0].astype(BF16)
        wgu_s[:, hidden:2 * hidden] = wu_ref[0].astype(BF16)
        wd_s[...] = wd_ref[0].astype(BF16)

    def tile(t, carry):
        slot = t % 2

        @pl.when(t + 1 < n_tiles)
        def _():
            copy_in(t + 1, 1 - slot).start()

        copy_in(t, slot).wait()

        @pl.when(t >= 2)
        def _():
            copy_out(t - 2, slot).wait()

        row = lax.broadcasted_iota(I32, (MOE_TILE, half), 0)
        x_lo, x_hi = _unpack_bf16_pair(jnp.where(row < count - t * MOE_TILE, xbuf[slot], jnp.uint32(0)))
        ab = (jnp.dot(x_lo.astype(BF16), wgu_s[0:half, :], preferred_element_type=F32)
              + jnp.dot(x_hi.astype(BF16), wgu_s[half:2 * half, :], preferred_element_type=F32))
        a = ab[:, 0:hidden]
        he = a * _sigmoid(a) * ab[:, hidden:2 * hidden]
        y = jnp.dot(he.astype(BF16), wd_s[...], preferred_element_type=F32)
        ybuf[slot] = _pack_bf16_pair(y[:, 0:half], y[:, half:2 * half])
        copy_out(t, slot).start()
        return carry

    lax.fori_loop(0, n_tiles, tile, 0)

    @pl.when(n_tiles >= 2)
    def _():
        copy_out(n_tiles - 2, n_tiles % 2).wait()

    @pl.when(n_tiles >= 1)
    def _():
        copy_out(n_tiles - 1, (n_tiles - 1) % 2).wait()


def _moe_call(starts, counts, xs, w_g, w_u, w_d):
    n_rows, half = xs.shape
    n_experts, d_model, hidden = w_g.shape
    grid_spec = pltpu.PrefetchScalarGridSpec(
        num_scalar_prefetch=2,
        grid=(n_experts,),
        in_specs=[
            pl.BlockSpec(memory_space=pl.ANY),
            pl.BlockSpec((1, d_model, hidden), lambda e, st, ct: (e, 0, 0)),
            pl.BlockSpec((1, d_model, hidden), lambda e, st, ct: (e, 0, 0)),
            pl.BlockSpec((1, hidden, d_model), lambda e, st, ct: (e, 0, 0)),
        ],
        out_specs=pl.BlockSpec(memory_space=pl.ANY),
        scratch_shapes=[
            pltpu.VMEM((d_model, 2 * hidden), BF16),
            pltpu.VMEM((hidden, d_model), BF16),
            pltpu.VMEM((2, MOE_TILE, half), U32),
            pltpu.VMEM((2, MOE_TILE, half), U32),
            pltpu.SemaphoreType.DMA((2,)),
            pltpu.SemaphoreType.DMA((2,)),
        ],
    )
    return pl.pallas_call(
        _moe_kernel, grid_spec=grid_spec,
        out_shape=jax.ShapeDtypeStruct((n_rows, half), U32), name="moe_experts",
        compiler_params=pltpu.CompilerParams(
            dimension_semantics=("arbitrary",), vmem_limit_bytes=VMEM_LIMIT),
    )(starts, counts, xs, w_g, w_u, w_d)


def _combine_kernel(x1_ref, y0_ref, y1_ref, rw_ref, gf_ref, out_ref):
    rw = rw_ref[...]
    a_lo, a_hi = _unpack_bf16_pair(y0_ref[0])
    b_lo, b_hi = _unpack_bf16_pair(y1_ref[0])
    w0, w1 = rw[:, 0:1], rw[:, 1:2]
    moe = jnp.concatenate([w0 * a_lo + w1 * b_lo, w0 * a_hi + w1 * b_hi], axis=-1)
    out_ref[...] = _rms(x1_ref[...] + moe, gf_ref[...])


def _combine_call(x1, yg, rw, gf, row_offset, tr=512):
    t, d_model = x1.shape
    half = yg.shape[-1]
    off = row_offset // tr
    assert t % tr == 0 and row_offset % tr == 0
    return pl.pallas_call(
        _combine_kernel,
        grid=(t // tr,),
        in_specs=[
            pl.BlockSpec((tr, d_model), lambda i: (i, 0)),
            pl.BlockSpec((1, tr, half), lambda i: (0, off + i, 0)),
            pl.BlockSpec((1, tr, half), lambda i: (1, off + i, 0)),
            pl.BlockSpec((tr, rw.shape[1]), lambda i: (i, 0)),
            pl.BlockSpec((1, d_model), lambda i: (0, 0)),
        ],
        out_specs=pl.BlockSpec((tr, d_model), lambda i: (i, 0)),
        out_shape=jax.ShapeDtypeStruct((t, d_model), F32), name=f"combine_off{row_offset}",
        compiler_params=pltpu.CompilerParams(
            dimension_semantics=("arbitrary",), vmem_limit_bytes=VMEM_LIMIT),
    )(x1, yg, yg, rw, gf)


def _one_layer(xp, xs, s_ret, c_pool, norm1_g, w_in, ret_norm_g, w_pool, pool_scale, w_out, norm2_g,
               w_rg, w_re, w_g, w_u, w_d, final_g, past_len, final):
    bp, seq, d_model = xp.shape
    bs, dseq, _ = xs.shape
    rw_width = ret_norm_g.shape[-1]
    pw = pool_scale.shape[-1]
    dh = rw_width // RET_HEADS
    tp, ts = bp * seq, bs * dseq
    t_all = tp + ts

    w_r = jnp.concatenate(
        [w_re, w_rg, jnp.zeros((d_model, LANES - N_EXPERTS - N_EXPERT_GROUPS), F32)], axis=1)
    wr_hi = w_r.astype(BF16)
    wr = jnp.concatenate([wr_hi, (w_r - wr_hi.astype(F32)).astype(BF16)], axis=1)
    consts = dict(
        g1=norm1_g.reshape(1, d_model), w_in=w_in.astype(BF16), gret=ret_norm_g.reshape(1, rw_width),
        w_pool=w_pool.astype(BF16), pscale=pool_scale.reshape(1, pw), w_out=w_out.astype(BF16),
        g2=norm2_g.reshape(1, d_model), wr=wr)

    s0p = jnp.zeros((bp, RET_HEADS, dh, dh), F32)
    h0p = jnp.zeros((bp, HIST_ROWS, pw), F32)
    h0s = jnp.pad(c_pool, ((0, 0), (HIST_ROWS - POOL_HIST, 0), (0, 0)))

    x1p, h2p, rip, rwp, st_p, hist_p, cnt_p = _layer_call(
        xp, s0p, h0p, 0, consts, bb=1, tl=512, chunk=256)
    x1s, h2s, ris, rws, st_s, hist_s, cnt_s = _layer_call(
        xs, s_ret, h0s, past_len, consts, bb=bs, tl=dseq, chunk=min(64, dseq))

    def fields(r, t):
        return jnp.moveaxis(r, 2, 0).reshape(r.shape[2], t)

    rip, ris, rwp, rws = fields(rip, tp), fields(ris, ts), fields(rwp, tp), fields(rws, ts)
    cnt_p = cnt_p[:, 0].astype(I32)
    cnt_s = cnt_s[:, 0].astype(I32)
    total = cnt_p + cnt_s
    padded = ((total + MOE_TILE - 1) // MOE_TILE) * MOE_TILE
    ends = jnp.cumsum(padded)
    starts = ends - padded
    experts = jnp.arange(N_EXPERTS, dtype=I32)[:, None]

    def positions(ids, ranks, base):
        return ranks + jnp.sum(jnp.where(ids[:, None, :] == experts[None], base[None, :, None], 0), axis=1)

    pos = jnp.concatenate([positions(rip[0:2], rip[2:4], starts),
                           positions(ris[0:2], ris[2:4], starts + cnt_p)], axis=1)

    n_rows = ((2 * t_all + N_EXPERTS * (MOE_TILE - 1)) // MOE_TILE) * MOE_TILE

    xs_sorted = _sc_dispatch(h2p.reshape(tp, d_model // 2), h2s.reshape(ts, d_model // 2),
                             pos[0], pos[1], n_rows)
    ys_sorted = _moe_call(starts.astype(I32), total, xs_sorted, w_g, w_u, w_d)
    yg = _sc_gather(ys_sorted, pos.reshape(2 * t_all))
    yg = yg.reshape(2, t_all, d_model // 2)

    gf = final_g.reshape(1, d_model) if final else None
    yp = _combine_call(x1p.reshape(tp, d_model), yg, rwp[0:2].T, gf, 0)
    ysm = _combine_call(x1s.reshape(ts, d_model), yg, rws[0:2].T, gf, tp)
    return (yp.reshape(bp, seq, d_model), ysm.reshape(bs, dseq, d_model),
            st_p, hist_p[:, HIST_ROWS - POOL_HIST:], st_s, hist_s[:, HIST_ROWS - POOL_HIST:])


def kernel(x_prompt, x_sample, state_ret, cache_pool, norm1_g, w_in, ret_norm_g, w_pool, pool_scale, w_out,
           norm2_g, w_router_group, w_router_expert, w_exp_gate, w_exp_up, w_exp_down, final_norm_g):
    depth = w_in.shape[0]
    assert depth == 1, "the final RMSNorm is fused into the last layer's combine kernel"
    yp, ys, s_p, h_p, s_s, h_s = _one_layer(
        x_prompt, x_sample, state_ret[0], cache_pool[0], norm1_g[0], w_in[0], ret_norm_g[0], w_pool[0],
        pool_scale[0], w_out[0], norm2_g[0], w_router_group[0], w_router_expert[0],
        w_exp_gate[0], w_exp_up[0], w_exp_down[0], final_norm_g, PAST_LEN, True)
    return (yp, ys, s_p[None], h_p[None], s_s[None], h_s[None])
```

```python
import functools

import jax
import jax.numpy as jnp
from jax import lax
from jax.experimental import pallas as pl
from jax.experimental.pallas import tpu as pltpu
from jax.experimental.pallas import tpu_sc as plsc

F32 = jnp.float32
BF16 = jnp.bfloat16
I32 = jnp.int32
U32 = jnp.uint32

EPS = 1e-6
ROPE_BASE = 10000.0
RET_HEADS = 4
POOL_WINDOWS = (2, 4, 8, 16)
POOL_HIST = max(POOL_WINDOWS) - 1
N_EXPERT_GROUPS = 4
EXPERTS_PER_GROUP = 8
N_EXPERTS = N_EXPERT_GROUPS * EXPERTS_PER_GROUP
EXPERT_SHIFT = EXPERTS_PER_GROUP.bit_length() - 1
PAST_LEN = 1024

LANES = 128
HIST_ROWS = 16
MOE_TILE = 256
MOE_BUFFERS = 4
MOE_LOOKAHEAD = MOE_BUFFERS - 1
COMBINE_TILE = 512
SC_UNIT = 32
VMEM_LIMIT = 56 * 1024 * 1024


def _rms(x, g):
    return x * lax.rsqrt(jnp.mean(x * x, axis=-1, keepdims=True) + EPS) * g


def _sigmoid(x):
    return 1.0 / (1.0 + jnp.exp(-x))


def _pack_bf16_pair(lo, hi):
    lo_b = lax.bitcast_convert_type(lo.astype(BF16).astype(F32), U32)
    hi_b = lax.bitcast_convert_type(hi.astype(BF16).astype(F32), U32)
    return hi_b | (lo_b >> 16)


def _unpack_bf16_pair(p):
    lo = lax.bitcast_convert_type(p << 16, F32)
    hi = lax.bitcast_convert_type(p & jnp.uint32(0xFFFF0000), F32)
    return lo, hi


def _layer_kernel(dc_ref, x_ref, s0_ref, h0_ref, rb_ref, rc_ref, rs_ref, rcs_ref, rss_ref,
                  dintra_ref, dq_ref, dk_ref,
                  g1_ref, win_ref, gret_ref, wpool_ref, pscale_ref, wout_ref, g2_ref,
                  wr_ref, tri_ref,
                  x1_ref, h2_ref, ri_ref, rw_ref, st_ref, hist_ref, cnt_ref,
                  ue_ref, q_ref, k_ref, v_ref, o_ref, a_ref,
                  *, bb, tl, chunk, pos0):
    b_idx = pl.program_id(0)
    l_idx = pl.program_id(1)
    rows = bb * tl
    d_model = x_ref.shape[-1]
    rw_width = q_ref.shape[-1]
    dh = rw_width // RET_HEADS
    pw = ue_ref.shape[-1]
    gw = pw // len(POOL_WINDOWS)
    n_chunks = tl // chunk

    @pl.when(l_idx == 0)
    def _():
        st_ref[...] = s0_ref[...]
        ue_ref[:, 0:HIST_ROWS, :] = h0_ref[...]

    @pl.when((l_idx == 0) & (b_idx == 0))
    def _():
        cnt_ref[...] = jnp.zeros_like(cnt_ref)

    x = x_ref[...].reshape(rows, d_model)
    hb = _rms(x, g1_ref[...]).astype(BF16)

    def project(c0, c1):
        return jnp.dot(hb, win_ref[:, c0:c1], preferred_element_type=F32)

    proj = project(0, 2 * rw_width)

    cos_b = rb_ref[0, 0:1, :]
    sin_b = rb_ref[0, 1:2, :]
    cosf = (cos_b * rc_ref[...] - sin_b * rs_ref[...])[None]
    sinf = (sin_b * rcs_ref[...] + cos_b * rss_ref[...])[None]
    k_scale = dh ** -0.5
    for hh in range(RET_HEADS):
        qh = proj[:, hh * dh:(hh + 1) * dh]
        kh = proj[:, rw_width + hh * dh:rw_width + (hh + 1) * dh]
        qr = (qh.reshape(bb, tl, dh) * cosf
              + pltpu.roll(qh, dh // 2, 1).reshape(bb, tl, dh) * sinf).reshape(rows, dh)
        kr = (kh.reshape(bb, tl, dh) * cosf
              + pltpu.roll(kh, dh // 2, 1).reshape(bb, tl, dh) * sinf).reshape(rows, dh)
        q_ref[:, hh * dh:(hh + 1) * dh] = qr.astype(BF16)
        k_ref[:, hh * dh:(hh + 1) * dh] = kr * k_scale
    v_ref[...] = project(2 * rw_width, 3 * rw_width).astype(BF16)
    gate = project(3 * rw_width, 4 * rw_width)
    u = project(4 * rw_width, 4 * rw_width + pw)

    def ret_block(b, c):
        r0 = b * tl + c * chunk
        if not isinstance(r0, int):
            r0 = pl.multiple_of(r0, chunk)
        for hh in range(RET_HEADS):
            cs = slice(hh * dh, (hh + 1) * dh)
            qc = q_ref[pl.ds(r0, chunk), cs]
            kf = k_ref[pl.ds(r0, chunk), cs]
            vc = v_ref[pl.ds(r0, chunk), cs]
            s_old = st_ref[b, hh]
            sc = lax.dot_general(qc, kf.astype(BF16), (((1,), (1,)), ((), ())),
                                 preferred_element_type=F32) * dintra_ref[hh]
            o = (jnp.dot(sc.astype(BF16), vc, preferred_element_type=F32)
                 + dq_ref[hh] * jnp.dot(qc, s_old.astype(BF16), preferred_element_type=F32))
            kd = (kf * dk_ref[hh]).astype(BF16)
            s_new = dc_ref[hh] * s_old + lax.dot_general(
                kd, vc, (((0,), (0,)), ((), ())), preferred_element_type=F32)
            st_ref[b, hh] = s_new
            o_ref[pl.ds(r0, chunk), cs] = o

    if bb * n_chunks <= 4:
        for b in range(bb):
            for c in range(n_chunks):
                ret_block(b, c)
    else:
        def body(i, carry):
            ret_block(i // n_chunks, i % n_chunks)
            return carry
        lax.fori_loop(0, bb * n_chunks, body, 0)

    for hh in range(RET_HEADS):
        cs = slice(hh * dh, (hh + 1) * dh)
        oh = o_ref[:, cs]
        mu = jnp.mean(oh, axis=-1, keepdims=True)
        oc = oh - mu
        var = jnp.mean(oc * oc, axis=-1, keepdims=True)
        y = oc * lax.rsqrt(var + EPS) * gret_ref[:, cs]
        g = gate[:, cs]
        a_ref[:, cs] = (g * _sigmoid(g) * y).astype(BF16)

    ue_ref[:, HIST_ROWS:HIST_ROWS + tl, :] = u.reshape(bb, tl, pw)
    pos = pos0 + l_idx * tl + lax.broadcasted_iota(I32, (1, tl, 1), 1)
    for gi, w in enumerate(POOL_WINDOWS):
        cs = slice(gi * gw, (gi + 1) * gw)
        acc = ue_ref[:, HIST_ROWS:HIST_ROWS + tl, cs]
        for j in range(1, w):
            acc = acc + ue_ref[:, HIST_ROWS - j:HIST_ROWS - j + tl, cs]
        inv_cnt = 1.0 / jnp.minimum(pos + 1, w).astype(F32)
        p = (acc * inv_cnt).reshape(rows, gw) - u[:, cs]
        z = jnp.dot(p.astype(BF16), wpool_ref[gi], preferred_element_type=F32) * pscale_ref[:, cs]
        a_ref[:, rw_width + gi * gw:rw_width + (gi + 1) * gw] = z.astype(BF16)
    tail = ue_ref[:, tl:tl + HIST_ROWS, :]
    ue_ref[:, 0:HIST_ROWS, :] = tail
    hist_ref[...] = tail

    x1 = x + jnp.dot(a_ref[...], wout_ref[...], preferred_element_type=F32)
    x1_ref[...] = x1.reshape(bb, tl, d_model)
    h2 = _rms(x1, g2_ref[...])
    h2_ref[...] = _pack_bf16_pair(h2[:, 0:d_model // 2], h2[:, d_model // 2:]).reshape(bb, tl, d_model // 2)

    h2_hi = h2.astype(BF16)
    h2_lo = (h2 - h2_hi.astype(F32)).astype(BF16)
    two = jnp.dot(h2_hi, wr_ref[...], preferred_element_type=F32)
    logits = (two[:, 0:LANES] + two[:, LANES:2 * LANES]
              + jnp.dot(h2_lo, wr_ref[:, 0:LANES], preferred_element_type=F32))
    lt = logits.T
    neg = jnp.float32(-jnp.inf)
    big = jnp.float32(1e9)
    sub = lax.broadcasted_iota(I32, (EXPERTS_PER_GROUP, rows), 0).astype(F32)
    gl = jnp.where(sub < N_EXPERT_GROUPS, lt[N_EXPERTS:N_EXPERTS + EXPERTS_PER_GROUP], neg)
    gmax = jnp.max(gl, axis=0, keepdims=True)
    gidx = jnp.min(jnp.where(gl == gmax, sub, big), axis=0, keepdims=True)
    p_sel = 1.0 / jnp.sum(jnp.exp(gl - gmax), axis=0, keepdims=True)
    el = lt[0:EXPERTS_PER_GROUP]
    for g in range(1, N_EXPERT_GROUPS):
        el = jnp.where(gidx == g, lt[g * EXPERTS_PER_GROUP:(g + 1) * EXPERTS_PER_GROUP], el)
    m1 = jnp.max(el, axis=0, keepdims=True)
    t1 = jnp.min(jnp.where(el == m1, sub, big), axis=0, keepdims=True)
    el2 = jnp.where(sub == t1, neg, el)
    m2 = jnp.max(el2, axis=0, keepdims=True)
    t2 = jnp.min(jnp.where(el2 == m2, sub, big), axis=0, keepdims=True)
    e2 = jnp.exp(m2 - m1)
    w1 = p_sel / (1.0 + e2)
    w2 = p_sel * e2 / (1.0 + e2)
    i1 = gidx * EXPERTS_PER_GROUP + t1
    i2 = gidx * EXPERTS_PER_GROUP + t2

    eid = lax.broadcasted_iota(I32, (N_EXPERTS, rows), 0).astype(F32)
    hit1 = eid == i1
    hit2 = eid == i2
    onehot = (hit1 | hit2).astype(BF16)
    before = jnp.dot(onehot, tri_ref[...], preferred_element_type=F32) + cnt_ref[...]
    r1 = jnp.sum(jnp.where(hit1, before, 0.0), axis=0, keepdims=True)
    r2 = jnp.sum(jnp.where(hit2, before, 0.0), axis=0, keepdims=True)
    cnt_ref[...] = cnt_ref[...] + jnp.sum(onehot.astype(F32), axis=1, keepdims=True)

    ri = jnp.where(sub == 0, i1, jnp.where(sub == 1, i2, jnp.where(sub == 2, r1, jnp.where(sub == 3, r2, 0.0))))
    ri_ref[...] = ri.astype(I32).reshape(ri_ref.shape)
    rw_ref[...] = jnp.where(sub == 0, w1, jnp.where(sub == 1, w2, 0.0)).reshape(rw_ref.shape)


def _rope_tables(pos0, seq, tl, dh):
    half = dh // 2
    inv = ROPE_BASE ** (-jnp.arange(half, dtype=F32) / half)
    ang_t = jnp.arange(tl, dtype=F32)[:, None] * inv[None, :]
    ang_b = (pos0 + tl * jnp.arange(seq // tl)).astype(F32)[:, None] * inv[None, :]
    dup = lambda a: jnp.concatenate([a, a], axis=-1)
    sgn = lambda a: jnp.concatenate([-a, a], axis=-1)
    base = jnp.stack([dup(jnp.cos(ang_b)), dup(jnp.sin(ang_b))], axis=1)
    base = jnp.pad(base, ((0, 0), (0, 8 - base.shape[1]), (0, 0)))
    cos_t, sin_t = jnp.cos(ang_t), jnp.sin(ang_t)
    return base, dup(cos_t), dup(sin_t), sgn(cos_t), sgn(sin_t)


def _layer_call(x, b0, nb, s0, h0, pos0, consts, *, bb, tl, chunk):
    _, seq, d_model = x.shape
    bsz = nb
    blk0 = b0 // bb
    rows = bb * tl
    rw_width = consts["gret"].shape[-1]
    pw = consts["pscale"].shape[-1]
    dh = rw_width // RET_HEADS

    rope = _rope_tables(pos0, seq, tl, dh)

    lg = jnp.log1p(-jnp.exp2(-5.0 - jnp.arange(RET_HEADS, dtype=F32)))
    idx = jnp.arange(chunk, dtype=F32)
    diff = idx[:, None] - idx[None, :]
    d_intra = jnp.where(diff[None] >= 0, jnp.exp(jnp.maximum(diff, 0.0)[None] * lg[:, None, None]), 0.0)
    d_q = jnp.broadcast_to(jnp.exp((idx + 1.0)[None, :] * lg[:, None])[:, :, None], (RET_HEADS, chunk, dh))
    d_k = jnp.broadcast_to(jnp.exp((chunk - 1.0 - idx)[None, :] * lg[:, None])[:, :, None], (RET_HEADS, chunk, dh))
    d_c = jnp.exp(chunk * lg)
    tri = jnp.triu(jnp.ones((rows, rows), BF16), 1)

    const2 = lambda b, l, *_: (0, 0)
    const3 = lambda b, l, *_: (0, 0, 0)
    grid_spec = pltpu.PrefetchScalarGridSpec(
        num_scalar_prefetch=0,
        grid=(bsz // bb, seq // tl),
        in_specs=[
            pl.BlockSpec(memory_space=pltpu.SMEM),
            pl.BlockSpec((bb, tl, d_model), lambda b, l: (blk0 + b, l, 0)),
            pl.BlockSpec((bb, RET_HEADS, dh, dh), lambda b, l: (b, 0, 0, 0)),
            pl.BlockSpec((bb, HIST_ROWS, pw), lambda b, l: (b, 0, 0)),
            pl.BlockSpec((1, 8, dh), lambda b, l: (l, 0, 0)),
            pl.BlockSpec((tl, dh), const2),
            pl.BlockSpec((tl, dh), const2),
            pl.BlockSpec((tl, dh), const2),
            pl.BlockSpec((tl, dh), const2),
            pl.BlockSpec((RET_HEADS, chunk, chunk), const3),
            pl.BlockSpec((RET_HEADS, chunk, dh), const3),
            pl.BlockSpec((RET_HEADS, chunk, dh), const3),
            pl.BlockSpec((1, d_model), const2),
            pl.BlockSpec(consts["w_in"].shape, const2),
            pl.BlockSpec((1, rw_width), const2),
            pl.BlockSpec(consts["w_pool"].shape, const3),
            pl.BlockSpec((1, pw), const2),
            pl.BlockSpec(consts["w_out"].shape, const2),
            pl.BlockSpec((1, d_model), const2),
            pl.BlockSpec((d_model, 2 * LANES), const2),
            pl.BlockSpec((rows, rows), const2),
        ],
        out_specs=[
            pl.BlockSpec((bb, tl, d_model), lambda b, l: (b, l, 0)),
            pl.BlockSpec((bb, tl, d_model // 2), lambda b, l: (b, l, 0)),
            pl.BlockSpec((1, 1, EXPERTS_PER_GROUP, rows), lambda b, l: (b, l, 0, 0)),
            pl.BlockSpec((1, 1, EXPERTS_PER_GROUP, rows), lambda b, l: (b, l, 0, 0)),
            pl.BlockSpec((bb, RET_HEADS, dh, dh), lambda b, l: (b, 0, 0, 0)),
            pl.BlockSpec((bb, HIST_ROWS, pw), lambda b, l: (b, 0, 0)),
            pl.BlockSpec((N_EXPERTS, rows), const2),
        ],
        scratch_shapes=[
            pltpu.VMEM((bb, HIST_ROWS + tl, pw), F32),
            pltpu.VMEM((rows, rw_width), BF16),
            pltpu.VMEM((rows, rw_width), F32),
            pltpu.VMEM((rows, rw_width), BF16),
            pltpu.VMEM((rows, rw_width), F32),
            pltpu.VMEM((rows, d_model), BF16),
        ],
    )
    out_shape = [
        jax.ShapeDtypeStruct((bsz, seq, d_model), F32),
        jax.ShapeDtypeStruct((bsz, seq, d_model // 2), U32),
        jax.ShapeDtypeStruct((bsz // bb, seq // tl, EXPERTS_PER_GROUP, rows), I32),
        jax.ShapeDtypeStruct((bsz // bb, seq // tl, EXPERTS_PER_GROUP, rows), F32),
        jax.ShapeDtypeStruct((bsz, RET_HEADS, dh, dh), F32),
        jax.ShapeDtypeStruct((bsz, HIST_ROWS, pw), F32),
        jax.ShapeDtypeStruct((N_EXPERTS, rows), F32),
    ]
    kern = functools.partial(_layer_kernel, bb=bb, tl=tl, chunk=chunk, pos0=pos0)
    return pl.pallas_call(
        kern, grid_spec=grid_spec, out_shape=out_shape, name=f"layer_pos{pos0}_b{b0}",
        compiler_params=pltpu.CompilerParams(
            dimension_semantics=("arbitrary", "arbitrary"), vmem_limit_bytes=VMEM_LIMIT),
    )(d_c, x, s0, h0, *rope, d_intra, d_q, d_k,
      consts["g1"], consts["w_in"], consts["gret"], consts["w_pool"], consts["pscale"],
      consts["w_out"], consts["g2"], consts["wr"], tri)


def _sc_partition(n_units):
    info = plsc.get_sparse_core_info()
    nc, nw = info.num_cores, info.num_cores * info.num_subcores
    upw = -(-n_units // nw)
    upw += upw % 2
    return nc, nw, upw


def _units_by_worker(idx, n_units, upw, nw):
    idx = jnp.pad(idx.reshape(n_units, SC_UNIT), ((0, nw * upw - n_units), (0, 0)))
    return idx.reshape(upw, nw, SC_UNIT).transpose(1, 0, 2)


def _sc_dispatch(srcs, idx0, idx1, n_out_rows):
    assert 1 <= len(srcs) <= 2
    d = srcs[0].shape[1]
    dtype = srcs[0].dtype
    assert all(src.shape[0] % SC_UNIT == 0 for src in srcs)
    units_a = srcs[0].shape[0] // SC_UNIT
    n_units = sum(src.shape[0] for src in srcs) // SC_UNIT
    nc, nw, upw = _sc_partition(n_units)
    idx0 = _units_by_worker(idx0, n_units, upw, nw)
    idx1 = _units_by_worker(idx1, n_units, upw, nw)
    mesh = plsc.VectorSubcoreMesh(core_axis_name="c", subcore_axis_name="s")
    dma = pltpu.SemaphoreType.DMA

    @functools.partial(
        pl.kernel, mesh=mesh,
        out_type=jax.ShapeDtypeStruct((n_out_rows, d), dtype),
        scratch_types=[
            pltpu.VMEM((upw, SC_UNIT), I32),
            pltpu.VMEM((upw, SC_UNIT), I32),
            pltpu.VMEM((SC_UNIT, d), dtype),
            pltpu.VMEM((SC_UNIT, d), dtype),
            dma, dma, dma, dma, dma, dma,
        ],
    )
    def k(*refs):
        src_hbm = refs[:len(srcs)]
        i0_hbm, i1_hbm, out_hbm, i0_v, i1_v, rows0, rows1, l0, l1, p0, p1, q0, q1 = refs[len(srcs):]
        wid = lax.axis_index("s") * nc + lax.axis_index("c")
        pltpu.sync_copy(i0_hbm.at[wid], i0_v)
        pltpu.sync_copy(i1_hbm.at[wid], i1_v)
        rows, lsem, psem, qsem = (rows0, rows1), (l0, l1), (p0, p1), (q0, q1)

        def live(j):
            return j * nw + wid < n_units

        def load(j, b, op):
            unit = j * nw + wid

            @pl.when(live(j) & (unit < units_a))
            def _():
                op(pltpu.make_async_copy(
                    src_hbm[0].at[pl.ds(pl.multiple_of(unit * SC_UNIT, 8), SC_UNIT)], rows[b], lsem[b]))

            if len(srcs) == 2:
                @pl.when(live(j) & (unit >= units_a))
                def _():
                    op(pltpu.make_async_copy(
                        src_hbm[1].at[pl.ds(pl.multiple_of((unit - units_a) * SC_UNIT, 8), SC_UNIT)],
                        rows[b], lsem[b]))

        def scatter(j, b, op):
            @pl.when(live(j))
            def _():
                op(pltpu.make_async_copy(rows[b], out_hbm.at[i0_v.at[j]], psem[b]))
                op(pltpu.make_async_copy(rows[b], out_hbm.at[i1_v.at[j]], qsem[b]))

        start = lambda c: c.start()
        wait = lambda c: c.wait()
        load(0, 0, start)

        @pl.loop(0, upw, step=2)
        def _(j):
            @pl.when(j > 0)
            def _():
                scatter(j - 1, 1, wait)
            load(j + 1, 1, start)
            load(j, 0, wait)
            scatter(j, 0, start)
            scatter(j, 0, wait)

            @pl.when(j + 2 < upw)
            def _():
                load(j + 2, 0, start)
            load(j + 1, 1, wait)
            scatter(j + 1, 1, start)

        scatter(upw - 1, 1, wait)

    return k(*srcs, idx0, idx1)


def _sc_gather(table, idx):
    n = idx.shape[0]
    d = table.shape[1]
    assert n % SC_UNIT == 0
    n_units = n // SC_UNIT
    nc, nw, upw = _sc_partition(n_units)
    idx = _units_by_worker(idx, n_units, upw, nw)
    mesh = plsc.VectorSubcoreMesh(core_axis_name="c", subcore_axis_name="s")
    dma = pltpu.SemaphoreType.DMA

    @functools.partial(
        pl.kernel, mesh=mesh,
        out_type=jax.ShapeDtypeStruct((n, d), table.dtype),
        scratch_types=[
            pltpu.VMEM((upw, SC_UNIT), I32),
            pltpu.VMEM((SC_UNIT, d), table.dtype),
            pltpu.VMEM((SC_UNIT, d), table.dtype),
            dma, dma, dma, dma,
        ],
    )
    def k(t_hbm, i_hbm, out_hbm, i_v, rows0, rows1, g0, g1, w0, w1):
        wid = lax.axis_index("s") * nc + lax.axis_index("c")
        pltpu.sync_copy(i_hbm.at[wid], i_v)
        rows, gsem, wsem = (rows0, rows1), (g0, g1), (w0, w1)

        def live(j):
            return j * nw + wid < n_units

        def gather(j, b, op):
            @pl.when(live(j))
            def _():
                op(pltpu.make_async_copy(t_hbm.at[i_v.at[j]], rows[b], gsem[b]))

        def write(j, b, op):
            @pl.when(live(j))
            def _():
                op(pltpu.make_async_copy(
                    rows[b], out_hbm.at[pl.ds(pl.multiple_of((j * nw + wid) * SC_UNIT, 8), SC_UNIT)], wsem[b]))

        start = lambda c: c.start()
        wait = lambda c: c.wait()
        gather(0, 0, start)

        @pl.loop(0, upw, step=2)
        def _(j):
            @pl.when(j > 0)
            def _():
                write(j - 1, 1, wait)
            gather(j + 1, 1, start)
            gather(j, 0, wait)
            write(j, 0, start)
            write(j, 0, wait)

            @pl.when(j + 2 < upw)
            def _():
                gather(j + 2, 0, start)
            gather(j + 1, 1, wait)
            write(j + 1, 1, start)

        write(upw - 1, 1, wait)

    return k(table, idx)


def _moe_kernel(start_ref, count_ref, gtot_ref, xs_hbm, wg_ref, wu_ref, wd_ref, ys_hbm,
                wgu_s, wd_s, xbuf, ybuf, sem_in, sem_out):
    e = pl.program_id(0)
    count = count_ref[e]
    g_first = start_ref[e] // MOE_TILE
    n_tiles = (count + MOE_TILE - 1) // MOE_TILE
    g_total = gtot_ref[0]
    hidden = wd_s.shape[0]
    half = xbuf.shape[-1]

    def rows_of(g):
        return pl.ds(pl.multiple_of(g * MOE_TILE, MOE_TILE), MOE_TILE)

    def copy_in(g):
        slot = g % MOE_BUFFERS
        return pltpu.make_async_copy(xs_hbm.at[rows_of(g)], xbuf.at[slot], sem_in.at[slot])

    def copy_out(g):
        slot = g % MOE_BUFFERS
        return pltpu.make_async_copy(ybuf.at[slot], ys_hbm.at[rows_of(g)], sem_out.at[slot])

    @pl.when(e == 0)
    def _():
        for g in range(MOE_LOOKAHEAD):
            @pl.when(g < g_total)
            def _():
                copy_in(g).start()

    @pl.when(n_tiles > 0)
    def _():
        wgu_s[:, 0:hidden] = wg_ref[0].astype(BF16)
        wgu_s[:, hidden:2 * hidden] = wu_ref[0].astype(BF16)
        wd_s[...] = wd_ref[0].astype(BF16)

    def tile(t, carry):
        g = g_first + t

        @pl.when(g + MOE_LOOKAHEAD < g_total)
        def _():
            copy_in(g + MOE_LOOKAHEAD).start()

        copy_in(g).wait()

        @pl.when(g >= MOE_BUFFERS)
        def _():
            copy_out(g - MOE_BUFFERS).wait()

        slot = g % MOE_BUFFERS
        row = lax.broadcasted_iota(I32, (MOE_TILE, half), 0)
        x_lo, x_hi = _unpack_bf16_pair(jnp.where(row < count - t * MOE_TILE, xbuf[slot], jnp.uint32(0)))
        ab = (jnp.dot(x_lo.astype(BF16), wgu_s[0:half, :], preferred_element_type=F32)
              + jnp.dot(x_hi.astype(BF16), wgu_s[half:2 * half, :], preferred_element_type=F32))
        a = ab[:, 0:hidden]
        he = a * _sigmoid(a) * ab[:, hidden:2 * hidden]
        y = jnp.dot(he.astype(BF16), wd_s[...], preferred_element_type=F32)
        ybuf[slot] = _pack_bf16_pair(y[:, 0:half], y[:, half:2 * half])
        copy_out(g).start()
        return carry

    lax.fori_loop(0, n_tiles, tile, 0)

    @pl.when(e == pl.num_programs(0) - 1)
    def _():
        for k in range(1, MOE_BUFFERS + 1):
            @pl.when(g_total >= k)
            def _():
                copy_out(g_total - k).wait()


def _moe_call(starts, counts, g_total, xs, w_g, w_u, w_d):
    n_rows, half = xs.shape
    n_experts, d_model, hidden = w_g.shape
    grid_spec = pltpu.PrefetchScalarGridSpec(
        num_scalar_prefetch=3,
        grid=(n_experts,),
        in_specs=[
            pl.BlockSpec(memory_space=pl.ANY),
            pl.BlockSpec((1, d_model, hidden), lambda e, st, ct, gt: (e, 0, 0)),
            pl.BlockSpec((1, d_model, hidden), lambda e, st, ct, gt: (e, 0, 0)),
            pl.BlockSpec((1, hidden, d_model), lambda e, st, ct, gt: (e, 0, 0)),
        ],
        out_specs=pl.BlockSpec(memory_space=pl.ANY),
        scratch_shapes=[
            pltpu.VMEM((d_model, 2 * hidden), BF16),
            pltpu.VMEM((hidden, d_model), BF16),
            pltpu.VMEM((MOE_BUFFERS, MOE_TILE, half), U32),
            pltpu.VMEM((MOE_BUFFERS, MOE_TILE, half), U32),
            pltpu.SemaphoreType.DMA((MOE_BUFFERS,)),
            pltpu.SemaphoreType.DMA((MOE_BUFFERS,)),
        ],
    )
    return pl.pallas_call(
        _moe_kernel, grid_spec=grid_spec,
        out_shape=jax.ShapeDtypeStruct((n_rows, half), U32), name="moe_experts",
        compiler_params=pltpu.CompilerParams(
            dimension_semantics=("arbitrary",), vmem_limit_bytes=VMEM_LIMIT),
    )(starts, counts, g_total, xs, w_g, w_u, w_d)


def _combine_kernel(x1_ref, y0_ref, y1_ref, rw_ref, gf_ref, *rest):
    out_ref = rest[-1]
    tr = x1_ref.shape[0]
    w_rows = jnp.concatenate([rw_ref[0], jnp.zeros((LANES - rw_ref.shape[1], tr), F32)], axis=0)
    w_cols = w_rows.T
    w0, w1 = w_cols[:, 0:1], w_cols[:, 1:2]
    a_lo, a_hi = _unpack_bf16_pair(y0_ref[0])
    b_lo, b_hi = _unpack_bf16_pair(y1_ref[0])
    moe = jnp.concatenate([w0 * a_lo + w1 * b_lo, w0 * a_hi + w1 * b_hi], axis=-1)
    out_ref[...] = _rms(x1_ref[...] + moe, gf_ref[...])


def _combine_call(x1, yg, yg_row0, rw, gf, out_rows, out_row0, prev_out=None):
    t, d_model = x1.shape
    tr = COMBINE_TILE
    half = yg.shape[-1]
    assert t % tr == 0 and yg_row0 % tr == 0 and out_row0 % tr == 0 and rw.shape == (t // tr, EXPERTS_PER_GROUP, tr)
    yoff, ooff = yg_row0 // tr, out_row0 // tr
    in_specs = [
        pl.BlockSpec((tr, d_model), lambda i: (i, 0)),
        pl.BlockSpec((1, tr, half), lambda i: (0, yoff + i, 0)),
        pl.BlockSpec((1, tr, half), lambda i: (1, yoff + i, 0)),
        pl.BlockSpec((1, EXPERTS_PER_GROUP, tr), lambda i: (i, 0, 0)),
        pl.BlockSpec((1, d_model), lambda i: (0, 0)),
    ]
    args = [x1, yg, yg, rw, gf]
    aliases = {}
    if prev_out is not None:
        in_specs.append(pl.BlockSpec(memory_space=pl.ANY))
        args.append(prev_out)
        aliases = {len(args) - 1: 0}
    return pl.pallas_call(
        _combine_kernel,
        grid=(t // tr,),
        in_specs=in_specs,
        out_specs=pl.BlockSpec((tr, d_model), lambda i: (ooff + i, 0)),
        out_shape=jax.ShapeDtypeStruct((out_rows, d_model), F32), name=f"combine_row{out_row0}_of{out_rows}",
        input_output_aliases=aliases,
        compiler_params=pltpu.CompilerParams(
            dimension_semantics=("arbitrary",), vmem_limit_bytes=VMEM_LIMIT),
    )(*args)


def _route_and_run_experts(streams, w_g, w_u, w_d):
    half = streams[0][0].shape[-1]
    tokens = [h2.shape[0] for h2, _, _ in streams]
    counts = [cnt[:, 0].astype(I32) for _, _, cnt in streams]
    total = sum(counts)
    padded = ((total + MOE_TILE - 1) // MOE_TILE) * MOE_TILE
    ends = jnp.cumsum(padded)
    starts = ends - padded
    experts = jnp.arange(N_EXPERTS, dtype=I32)[None, :, None]
    pos, base = [], starts
    for (_, ri, _), t, cnt in zip(streams, tokens, counts):
        ri = jnp.moveaxis(ri, 2, 0).reshape(ri.shape[2], t)
        first_row = jnp.sum(jnp.where(ri[0:2, None, :] == experts, base[None, :, None], 0), axis=1)
        pos.append(ri[2:4] + first_row)
        base = base + cnt
    pos = jnp.concatenate(pos, axis=1)
    t_all = sum(tokens)
    n_rows = ((2 * t_all + N_EXPERTS * (MOE_TILE - 1)) // MOE_TILE) * MOE_TILE
    xs_sorted = _sc_dispatch([h2 for h2, _, _ in streams], pos[0], pos[1], n_rows)
    ys_sorted = _moe_call(starts.astype(I32), total, (ends[-1:] // MOE_TILE).astype(I32), xs_sorted, w_g, w_u, w_d)
    return _sc_gather(ys_sorted, pos.reshape(2 * t_all)).reshape(2, t_all, half)


def _one_layer(xp, xs, s_ret, c_pool, norm1_g, w_in, ret_norm_g, w_pool, pool_scale, w_out, norm2_g,
               w_rg, w_re, w_g, w_u, w_d, final_g, past_len):
    bp, seq, d_model = xp.shape
    bs, dseq, _ = xs.shape
    rw_width = ret_norm_g.shape[-1]
    pw = pool_scale.shape[-1]
    dh = rw_width // RET_HEADS
    half = d_model // 2

    w_r = jnp.concatenate(
        [w_re, w_rg, jnp.zeros((d_model, LANES - N_EXPERTS - N_EXPERT_GROUPS), F32)], axis=1)
    wr_hi = w_r.astype(BF16)
    wr = jnp.concatenate([wr_hi, (w_r - wr_hi.astype(F32)).astype(BF16)], axis=1)
    consts = dict(
        g1=norm1_g.reshape(1, d_model), w_in=w_in.astype(BF16), gret=ret_norm_g.reshape(1, rw_width),
        w_pool=w_pool.astype(BF16), pscale=pool_scale.reshape(1, pw), w_out=w_out.astype(BF16),
        g2=norm2_g.reshape(1, d_model), wr=wr)

    gf = final_g.reshape(1, d_model)

    bh = bp // 2
    th = bh * seq
    ts = bs * dseq
    s0p = jnp.zeros((bh, RET_HEADS, dh, dh), F32)
    h0p = jnp.zeros((bh, HIST_ROWS, pw), F32)
    h0s = jnp.pad(c_pool, ((0, 0), (HIST_ROWS - POOL_HIST, 0), (0, 0)))
    prompt_tile = dict(bb=1, tl=COMBINE_TILE, chunk=256)

    def stream(layer_out, t):
        x1, h2, ri, rw, st, hist, cnt = layer_out
        return dict(x1=x1.reshape(t, d_model), route=(h2.reshape(t, half), ri, cnt),
                    rw=rw.reshape(-1, EXPERTS_PER_GROUP, COMBINE_TILE), st=st, hist=hist)

    pa = stream(_layer_call(xp, 0, bh, s0p, h0p, 0, consts, **prompt_tile), th)
    yg_a = _route_and_run_experts([pa["route"]], w_g, w_u, w_d)
    pb = stream(_layer_call(xp, bh, bp - bh, s0p, h0p, 0, consts, **prompt_tile), th)
    sm = stream(_layer_call(xs, 0, bs, s_ret, h0s, past_len, consts, bb=bs, tl=dseq, chunk=min(64, dseq)), ts)
    yg_b = _route_and_run_experts([pb["route"], sm["route"]], w_g, w_u, w_d)

    yp = _combine_call(pa["x1"], yg_a, 0, pa["rw"], gf, bp * seq, 0)
    yp = _combine_call(pb["x1"], yg_b, 0, pb["rw"], gf, bp * seq, th, prev_out=yp)
    ysm = _combine_call(sm["x1"], yg_b, th, sm["rw"], gf, ts, 0)
    st_p = jnp.concatenate([pa["st"], pb["st"]], axis=0)
    hist_p = jnp.concatenate([pa["hist"], pb["hist"]], axis=0)
    return (yp.reshape(bp, seq, d_model), ysm.reshape(bs, dseq, d_model),
            st_p, hist_p[:, HIST_ROWS - POOL_HIST:], sm["st"], sm["hist"][:, HIST_ROWS - POOL_HIST:])


def kernel(x_prompt, x_sample, state_ret, cache_pool, norm1_g, w_in, ret_norm_g, w_pool, pool_scale, w_out,
           norm2_g, w_router_group, w_router_expert, w_exp_gate, w_exp_up, w_exp_down, final_norm_g):
    depth = w_in.shape[0]
    assert depth == 1, "the final RMSNorm is fused into the layer's combine kernel"
    assert x_prompt.shape[0] % 2 == 0 and x_prompt.shape[1] % COMBINE_TILE == 0
    assert x_sample.shape[0] * x_sample.shape[1] == COMBINE_TILE
    yp, ys, s_p, h_p, s_s, h_s = _one_layer(
        x_prompt, x_sample, state_ret[0], cache_pool[0], norm1_g[0], w_in[0], ret_norm_g[0], w_pool[0],
        pool_scale[0], w_out[0], norm2_g[0], w_router_group[0], w_router_expert[0],
        w_exp_gate[0], w_exp_up[0], w_exp_down[0], final_norm_g, PAST_LEN)
    return (yp, ys, s_p[None], h_p[None], s_s[None], h_s[None])
```

```python
import functools

import jax
import jax.numpy as jnp
from jax import lax
from jax.experimental import pallas as pl
from jax.experimental.pallas import tpu as pltpu
from jax.experimental.pallas import tpu_sc as plsc

F32 = jnp.float32
BF16 = jnp.bfloat16
I32 = jnp.int32
U32 = jnp.uint32

EPS = 1e-6
ROPE_BASE = 10000.0
RET_HEADS = 4
POOL_WINDOWS = (2, 4, 8, 16)
POOL_HIST = max(POOL_WINDOWS) - 1
N_EXPERT_GROUPS = 4
EXPERTS_PER_GROUP = 8
N_EXPERTS = N_EXPERT_GROUPS * EXPERTS_PER_GROUP
EXPERT_SHIFT = EXPERTS_PER_GROUP.bit_length() - 1
PAST_LEN = 1024

LANES = 128
HIST_ROWS = 16
MOE_TILE = 256
MOE_BUFFERS = 4
MOE_LOOKAHEAD = MOE_BUFFERS - 1
PROMPT_TILE = 1024
SC_UNIT = 32
VMEM_LIMIT = 56 * 1024 * 1024


def _rms(x, g):
    return x * lax.rsqrt(jnp.mean(x * x, axis=-1, keepdims=True) + EPS) * g


def _sigmoid(x):
    return 1.0 / (1.0 + jnp.exp(-x))


def _pack_bf16_pair(lo, hi):
    lo_b = lax.bitcast_convert_type(lo.astype(BF16).astype(F32), U32)
    hi_b = lax.bitcast_convert_type(hi.astype(BF16).astype(F32), U32)
    return hi_b | (lo_b >> 16)


def _unpack_bf16_pair(p):
    lo = lax.bitcast_convert_type(p << 16, F32)
    hi = lax.bitcast_convert_type(p & jnp.uint32(0xFFFF0000), F32)
    return lo, hi


def _layer_kernel(dc_ref, x_ref, s0_ref, h0_ref, rb_ref, rc_ref, rs_ref, rcs_ref, rss_ref,
                  dintra_ref, dq_ref, dk_ref,
                  g1_ref, win_ref, gret_ref, wpool_ref, pscale_ref, wout_ref, g2_ref,
                  wr_ref, tri_ref,
                  x1_ref, h2_ref, ri_ref, rw_ref, st_ref, hist_ref, cnt_ref,
                  ue_ref, q_ref, k_ref, v_ref, o_ref, a_ref,
                  *, bb, tl, chunk, pos0):
    b_idx = pl.program_id(0)
    l_idx = pl.program_id(1)
    rows = bb * tl
    d_model = x_ref.shape[-1]
    rw_width = q_ref.shape[-1]
    dh = rw_width // RET_HEADS
    pw = ue_ref.shape[-1]
    gw = pw // len(POOL_WINDOWS)
    n_chunks = tl // chunk

    @pl.when(l_idx == 0)
    def _():
        st_ref[...] = s0_ref[...]
        ue_ref[:, 0:HIST_ROWS, :] = h0_ref[...]

    @pl.when((l_idx == 0) & (b_idx == 0))
    def _():
        cnt_ref[...] = jnp.zeros_like(cnt_ref)

    x = x_ref[...].reshape(rows, d_model)
    hb = _rms(x, g1_ref[...]).astype(BF16)

    def project(c0, c1):
        return jnp.dot(hb, win_ref[:, c0:c1], preferred_element_type=F32)

    proj = project(0, 2 * rw_width)

    cos_b = rb_ref[0, 0:1, :]
    sin_b = rb_ref[0, 1:2, :]
    cosf = (cos_b * rc_ref[...] - sin_b * rs_ref[...])[None]
    sinf = (sin_b * rcs_ref[...] + cos_b * rss_ref[...])[None]
    k_scale = dh ** -0.5
    for hh in range(RET_HEADS):
        qh = proj[:, hh * dh:(hh + 1) * dh]
        kh = proj[:, rw_width + hh * dh:rw_width + (hh + 1) * dh]
        qr = (qh.reshape(bb, tl, dh) * cosf
              + pltpu.roll(qh, dh // 2, 1).reshape(bb, tl, dh) * sinf).reshape(rows, dh)
        kr = (kh.reshape(bb, tl, dh) * cosf
              + pltpu.roll(kh, dh // 2, 1).reshape(bb, tl, dh) * sinf).reshape(rows, dh)
        q_ref[:, hh * dh:(hh + 1) * dh] = qr.astype(BF16)
        k_ref[:, hh * dh:(hh + 1) * dh] = kr * k_scale
    v_ref[...] = project(2 * rw_width, 3 * rw_width).astype(BF16)
    gate = project(3 * rw_width, 4 * rw_width)
    u = project(4 * rw_width, 4 * rw_width + pw)

    def ret_block(b, c):
        r0 = b * tl + c * chunk
        if not isinstance(r0, int):
            r0 = pl.multiple_of(r0, chunk)
        for hh in range(RET_HEADS):
            cs = slice(hh * dh, (hh + 1) * dh)
            qc = q_ref[pl.ds(r0, chunk), cs]
            kf = k_ref[pl.ds(r0, chunk), cs]
            vc = v_ref[pl.ds(r0, chunk), cs]
            s_old = st_ref[b, hh]
            sc = lax.dot_general(qc, kf.astype(BF16), (((1,), (1,)), ((), ())),
                                 preferred_element_type=F32) * dintra_ref[hh]
            o = (jnp.dot(sc.astype(BF16), vc, preferred_element_type=F32)
                 + dq_ref[hh] * jnp.dot(qc, s_old.astype(BF16), preferred_element_type=F32))
            kd = (kf * dk_ref[hh]).astype(BF16)
            s_new = dc_ref[hh] * s_old + lax.dot_general(
                kd, vc, (((0,), (0,)), ((), ())), preferred_element_type=F32)
            st_ref[b, hh] = s_new
            o_ref[pl.ds(r0, chunk), cs] = o

    if bb * n_chunks <= 4:
        for b in range(bb):
            for c in range(n_chunks):
                ret_block(b, c)
    else:
        def body(i, carry):
            ret_block(i // n_chunks, i % n_chunks)
            return carry
        lax.fori_loop(0, bb * n_chunks, body, 0)

    for hh in range(RET_HEADS):
        cs = slice(hh * dh, (hh + 1) * dh)
        oh = o_ref[:, cs]
        mu = jnp.mean(oh, axis=-1, keepdims=True)
        oc = oh - mu
        var = jnp.mean(oc * oc, axis=-1, keepdims=True)
        y = oc * lax.rsqrt(var + EPS) * gret_ref[:, cs]
        g = gate[:, cs]
        a_ref[:, cs] = (g * _sigmoid(g) * y).astype(BF16)

    ue_ref[:, HIST_ROWS:HIST_ROWS + tl, :] = u.reshape(bb, tl, pw)
    pos = pos0 + l_idx * tl + lax.broadcasted_iota(I32, (1, tl, 1), 1)
    for gi, w in enumerate(POOL_WINDOWS):
        cs = slice(gi * gw, (gi + 1) * gw)
        acc = ue_ref[:, HIST_ROWS:HIST_ROWS + tl, cs]
        for j in range(1, w):
            acc = acc + ue_ref[:, HIST_ROWS - j:HIST_ROWS - j + tl, cs]
        inv_cnt = 1.0 / jnp.minimum(pos + 1, w).astype(F32)
        p = (acc * inv_cnt).reshape(rows, gw) - u[:, cs]
        z = jnp.dot(p.astype(BF16), wpool_ref[gi], preferred_element_type=F32) * pscale_ref[:, cs]
        a_ref[:, rw_width + gi * gw:rw_width + (gi + 1) * gw] = z.astype(BF16)
    tail = ue_ref[:, tl:tl + HIST_ROWS, :]
    ue_ref[:, 0:HIST_ROWS, :] = tail
    hist_ref[...] = tail

    x1 = x + jnp.dot(a_ref[...], wout_ref[...], preferred_element_type=F32)
    x1_ref[...] = x1.reshape(bb, tl, d_model)
    h2 = _rms(x1, g2_ref[...])
    h2_ref[...] = _pack_bf16_pair(h2[:, 0:d_model // 2], h2[:, d_model // 2:]).reshape(bb, tl, d_model // 2)

    h2_hi = h2.astype(BF16)
    h2_lo = (h2 - h2_hi.astype(F32)).astype(BF16)
    two = jnp.dot(h2_hi, wr_ref[...], preferred_element_type=F32)
    logits = (two[:, 0:LANES] + two[:, LANES:2 * LANES]
              + jnp.dot(h2_lo, wr_ref[:, 0:LANES], preferred_element_type=F32))
    lt = logits.T
    neg = jnp.float32(-jnp.inf)
    big = jnp.float32(1e9)
    sub = lax.broadcasted_iota(I32, (EXPERTS_PER_GROUP, rows), 0).astype(F32)
    gl = jnp.where(sub < N_EXPERT_GROUPS, lt[N_EXPERTS:N_EXPERTS + EXPERTS_PER_GROUP], neg)
    gmax = jnp.max(gl, axis=0, keepdims=True)
    gidx = jnp.min(jnp.where(gl == gmax, sub, big), axis=0, keepdims=True)
    p_sel = 1.0 / jnp.sum(jnp.exp(gl - gmax), axis=0, keepdims=True)
    el = lt[0:EXPERTS_PER_GROUP]
    for g in range(1, N_EXPERT_GROUPS):
        el = jnp.where(gidx == g, lt[g * EXPERTS_PER_GROUP:(g + 1) * EXPERTS_PER_GROUP], el)
    m1 = jnp.max(el, axis=0, keepdims=True)
    t1 = jnp.min(jnp.where(el == m1, sub, big), axis=0, keepdims=True)
    el2 = jnp.where(sub == t1, neg, el)
    m2 = jnp.max(el2, axis=0, keepdims=True)
    t2 = jnp.min(jnp.where(el2 == m2, sub, big), axis=0, keepdims=True)
    e2 = jnp.exp(m2 - m1)
    w1 = p_sel / (1.0 + e2)
    w2 = p_sel * e2 / (1.0 + e2)
    i1 = gidx * EXPERTS_PER_GROUP + t1
    i2 = gidx * EXPERTS_PER_GROUP + t2

    eid = lax.broadcasted_iota(I32, (N_EXPERTS, rows), 0).astype(F32)
    hit1 = eid == i1
    hit2 = eid == i2
    onehot = (hit1 | hit2).astype(BF16)
    before = jnp.dot(onehot, tri_ref[...], preferred_element_type=F32) + cnt_ref[...]
    r1 = jnp.sum(jnp.where(hit1, before, 0.0), axis=0, keepdims=True)
    r2 = jnp.sum(jnp.where(hit2, before, 0.0), axis=0, keepdims=True)
    cnt_ref[...] = cnt_ref[...] + jnp.sum(onehot.astype(F32), axis=1, keepdims=True)

    ri = jnp.where(sub == 0, i1, jnp.where(sub == 1, i2, jnp.where(sub == 2, r1, jnp.where(sub == 3, r2, 0.0))))
    ri_ref[...] = ri.astype(I32).reshape(ri_ref.shape)
    rw_ref[...] = jnp.where(sub == 0, w1, jnp.where(sub == 1, w2, 0.0)).reshape(rw_ref.shape)


def _rope_tables(pos0, seq, tl, dh):
    half = dh // 2
    inv = ROPE_BASE ** (-jnp.arange(half, dtype=F32) / half)
    ang_t = jnp.arange(tl, dtype=F32)[:, None] * inv[None, :]
    ang_b = (pos0 + tl * jnp.arange(seq // tl)).astype(F32)[:, None] * inv[None, :]
    dup = lambda a: jnp.concatenate([a, a], axis=-1)
    sgn = lambda a: jnp.concatenate([-a, a], axis=-1)
    base = jnp.stack([dup(jnp.cos(ang_b)), dup(jnp.sin(ang_b))], axis=1)
    base = jnp.pad(base, ((0, 0), (0, 8 - base.shape[1]), (0, 0)))
    cos_t, sin_t = jnp.cos(ang_t), jnp.sin(ang_t)
    return base, dup(cos_t), dup(sin_t), sgn(cos_t), sgn(sin_t)


def _layer_call(x, b0, nb, s0, h0, pos0, consts, *, bb, tl, chunk):
    _, seq, d_model = x.shape
    bsz = nb
    blk0 = b0 // bb
    rows = bb * tl
    rw_width = consts["gret"].shape[-1]
    pw = consts["pscale"].shape[-1]
    dh = rw_width // RET_HEADS

    rope = _rope_tables(pos0, seq, tl, dh)

    lg = jnp.log1p(-jnp.exp2(-5.0 - jnp.arange(RET_HEADS, dtype=F32)))
    idx = jnp.arange(chunk, dtype=F32)
    diff = idx[:, None] - idx[None, :]
    d_intra = jnp.where(diff[None] >= 0, jnp.exp(jnp.maximum(diff, 0.0)[None] * lg[:, None, None]), 0.0)
    d_q = jnp.broadcast_to(jnp.exp((idx + 1.0)[None, :] * lg[:, None])[:, :, None], (RET_HEADS, chunk, dh))
    d_k = jnp.broadcast_to(jnp.exp((chunk - 1.0 - idx)[None, :] * lg[:, None])[:, :, None], (RET_HEADS, chunk, dh))
    d_c = jnp.exp(chunk * lg)
    tri = jnp.triu(jnp.ones((rows, rows), BF16), 1)

    const2 = lambda b, l, *_: (0, 0)
    const3 = lambda b, l, *_: (0, 0, 0)
    grid_spec = pltpu.PrefetchScalarGridSpec(
        num_scalar_prefetch=0,
        grid=(bsz // bb, seq // tl),
        in_specs=[
            pl.BlockSpec(memory_space=pltpu.SMEM),
            pl.BlockSpec((bb, tl, d_model), lambda b, l: (blk0 + b, l, 0)),
            pl.BlockSpec((bb, RET_HEADS, dh, dh), lambda b, l: (b, 0, 0, 0)),
            pl.BlockSpec((bb, HIST_ROWS, pw), lambda b, l: (b, 0, 0)),
            pl.BlockSpec((1, 8, dh), lambda b, l: (l, 0, 0)),
            pl.BlockSpec((tl, dh), const2),
            pl.BlockSpec((tl, dh), const2),
            pl.BlockSpec((tl, dh), const2),
            pl.BlockSpec((tl, dh), const2),
            pl.BlockSpec((RET_HEADS, chunk, chunk), const3),
            pl.BlockSpec((RET_HEADS, chunk, dh), const3),
            pl.BlockSpec((RET_HEADS, chunk, dh), const3),
            pl.BlockSpec((1, d_model), const2),
            pl.BlockSpec(consts["w_in"].shape, const2),
            pl.BlockSpec((1, rw_width), const2),
            pl.BlockSpec(consts["w_pool"].shape, const3),
            pl.BlockSpec((1, pw), const2),
            pl.BlockSpec(consts["w_out"].shape, const2),
            pl.BlockSpec((1, d_model), const2),
            pl.BlockSpec((d_model, 2 * LANES), const2),
            pl.BlockSpec((rows, rows), const2),
        ],
        out_specs=[
            pl.BlockSpec((bb, tl, d_model), lambda b, l: (b, l, 0)),
            pl.BlockSpec((bb, tl, d_model // 2), lambda b, l: (b, l, 0)),
            pl.BlockSpec((1, 1, EXPERTS_PER_GROUP, rows), lambda b, l: (b, l, 0, 0)),
            pl.BlockSpec((1, 1, EXPERTS_PER_GROUP, rows), lambda b, l: (b, l, 0, 0)),
            pl.BlockSpec((bb, RET_HEADS, dh, dh), lambda b, l: (b, 0, 0, 0)),
            pl.BlockSpec((bb, HIST_ROWS, pw), lambda b, l: (b, 0, 0)),
            pl.BlockSpec((N_EXPERTS, rows), const2),
        ],
        scratch_shapes=[
            pltpu.VMEM((bb, HIST_ROWS + tl, pw), F32),
            pltpu.VMEM((rows, rw_width), BF16),
            pltpu.VMEM((rows, rw_width), F32),
            pltpu.VMEM((rows, rw_width), BF16),
            pltpu.VMEM((rows, rw_width), F32),
            pltpu.VMEM((rows, d_model), BF16),
        ],
    )
    out_shape = [
        jax.ShapeDtypeStruct((bsz, seq, d_model), F32),
        jax.ShapeDtypeStruct((bsz, seq, d_model // 2), U32),
        jax.ShapeDtypeStruct((bsz // bb, seq // tl, EXPERTS_PER_GROUP, rows), I32),
        jax.ShapeDtypeStruct((bsz // bb, seq // tl, EXPERTS_PER_GROUP, rows), F32),
        jax.ShapeDtypeStruct((bsz, RET_HEADS, dh, dh), F32),
        jax.ShapeDtypeStruct((bsz, HIST_ROWS, pw), F32),
        jax.ShapeDtypeStruct((N_EXPERTS, rows), F32),
    ]
    kern = functools.partial(_layer_kernel, bb=bb, tl=tl, chunk=chunk, pos0=pos0)
    return pl.pallas_call(
        kern, grid_spec=grid_spec, out_shape=out_shape, name=f"layer_pos{pos0}_b{b0}",
        compiler_params=pltpu.CompilerParams(
            dimension_semantics=("arbitrary", "arbitrary"), vmem_limit_bytes=VMEM_LIMIT),
    )(d_c, x, s0, h0, *rope, d_intra, d_q, d_k,
      consts["g1"], consts["w_in"], consts["gret"], consts["w_pool"], consts["pscale"],
      consts["w_out"], consts["g2"], consts["wr"], tri)


def _sc_partition(n_units):
    info = plsc.get_sparse_core_info()
    nc, nw = info.num_cores, info.num_cores * info.num_subcores
    upw = -(-n_units // nw)
    upw += upw % 2
    return nc, nw, upw


def _units_by_worker(idx, n_units, upw, nw):
    idx = jnp.pad(idx.reshape(n_units, SC_UNIT), ((0, nw * upw - n_units), (0, 0)))
    return idx.reshape(upw, nw, SC_UNIT).transpose(1, 0, 2)


def _sc_dispatch(srcs, idx0, idx1, n_out_rows):
    assert 1 <= len(srcs) <= 2
    d = srcs[0].shape[1]
    dtype = srcs[0].dtype
    assert all(src.shape[0] % SC_UNIT == 0 for src in srcs)
    units_a = srcs[0].shape[0] // SC_UNIT
    n_units = sum(src.shape[0] for src in srcs) // SC_UNIT
    nc, nw, upw = _sc_partition(n_units)
    idx0 = _units_by_worker(idx0, n_units, upw, nw)
    idx1 = _units_by_worker(idx1, n_units, upw, nw)
    mesh = plsc.VectorSubcoreMesh(core_axis_name="c", subcore_axis_name="s")
    dma = pltpu.SemaphoreType.DMA

    @functools.partial(
        pl.kernel, mesh=mesh,
        out_type=jax.ShapeDtypeStruct((n_out_rows, d), dtype),
        scratch_types=[
            pltpu.VMEM((upw, SC_UNIT), I32),
            pltpu.VMEM((upw, SC_UNIT), I32),
            pltpu.VMEM((SC_UNIT, d), dtype),
            pltpu.VMEM((SC_UNIT, d), dtype),
            dma, dma, dma, dma, dma, dma,
        ],
    )
    def k(*refs):
        src_hbm = refs[:len(srcs)]
        i0_hbm, i1_hbm, out_hbm, i0_v, i1_v, rows0, rows1, l0, l1, p0, p1, q0, q1 = refs[len(srcs):]
        wid = lax.axis_index("s") * nc + lax.axis_index("c")
        pltpu.sync_copy(i0_hbm.at[wid], i0_v)
        pltpu.sync_copy(i1_hbm.at[wid], i1_v)
        rows, lsem, psem, qsem = (rows0, rows1), (l0, l1), (p0, p1), (q0, q1)

        def live(j):
            return j * nw + wid < n_units

        def load(j, b, op):
            unit = j * nw + wid

            @pl.when(live(j) & (unit < units_a))
            def _():
                op(pltpu.make_async_copy(
                    src_hbm[0].at[pl.ds(pl.multiple_of(unit * SC_UNIT, 8), SC_UNIT)], rows[b], lsem[b]))

            if len(srcs) == 2:
                @pl.when(live(j) & (unit >= units_a))
                def _():
                    op(pltpu.make_async_copy(
                        src_hbm[1].at[pl.ds(pl.multiple_of((unit - units_a) * SC_UNIT, 8), SC_UNIT)],
                        rows[b], lsem[b]))

        def scatter(j, b, op):
            @pl.when(live(j))
            def _():
                op(pltpu.make_async_copy(rows[b], out_hbm.at[i0_v.at[j]], psem[b]))
                op(pltpu.make_async_copy(rows[b], out_hbm.at[i1_v.at[j]], qsem[b]))

        start = lambda c: c.start()
        wait = lambda c: c.wait()
        load(0, 0, start)

        @pl.loop(0, upw, step=2)
        def _(j):
            @pl.when(j > 0)
            def _():
                scatter(j - 1, 1, wait)
            load(j + 1, 1, start)
            load(j, 0, wait)
            scatter(j, 0, start)
            scatter(j, 0, wait)

            @pl.when(j + 2 < upw)
            def _():
                load(j + 2, 0, start)
            load(j + 1, 1, wait)
            scatter(j + 1, 1, start)

        scatter(upw - 1, 1, wait)

    return k(*srcs, idx0, idx1)


def _sc_gather(table, idx):
    n = idx.shape[0]
    d = table.shape[1]
    assert n % SC_UNIT == 0
    n_units = n // SC_UNIT
    nc, nw, upw = _sc_partition(n_units)
    idx = _units_by_worker(idx, n_units, upw, nw)
    mesh = plsc.VectorSubcoreMesh(core_axis_name="c", subcore_axis_name="s")
    dma = pltpu.SemaphoreType.DMA

    @functools.partial(
        pl.kernel, mesh=mesh,
        out_type=jax.ShapeDtypeStruct((n, d), table.dtype),
        scratch_types=[
            pltpu.VMEM((upw, SC_UNIT), I32),
            pltpu.VMEM((SC_UNIT, d), table.dtype),
            pltpu.VMEM((SC_UNIT, d), table.dtype),
            dma, dma, dma, dma,
        ],
    )
    def k(t_hbm, i_hbm, out_hbm, i_v, rows0, rows1, g0, g1, w0, w1):
        wid = lax.axis_index("s") * nc + lax.axis_index("c")
        pltpu.sync_copy(i_hbm.at[wid], i_v)
        rows, gsem, wsem = (rows0, rows1), (g0, g1), (w0, w1)

        def live(j):
            return j * nw + wid < n_units

        def gather(j, b, op):
            @pl.when(live(j))
            def _():
                op(pltpu.make_async_copy(t_hbm.at[i_v.at[j]], rows[b], gsem[b]))

        def write(j, b, op):
            @pl.when(live(j))
            def _():
                op(pltpu.make_async_copy(
                    rows[b], out_hbm.at[pl.ds(pl.multiple_of((j * nw + wid) * SC_UNIT, 8), SC_UNIT)], wsem[b]))

        start = lambda c: c.start()
        wait = lambda c: c.wait()
        gather(0, 0, start)

        @pl.loop(0, upw, step=2)
        def _(j):
            @pl.when(j > 0)
            def _():
                write(j - 1, 1, wait)
            gather(j + 1, 1, start)
            gather(j, 0, wait)
            write(j, 0, start)
            write(j, 0, wait)

            @pl.when(j + 2 < upw)
            def _():
                gather(j + 2, 0, start)
            gather(j + 1, 1, wait)
            write(j + 1, 1, start)

        write(upw - 1, 1, wait)

    return k(table, idx)


def _moe_kernel(start_ref, count_ref, gtot_ref, xs_hbm, wg_ref, wu_ref, wd_ref, ys_hbm,
                wgu_s, wd_s, xbuf, ybuf, sem_in, sem_out):
    e = pl.program_id(0)
    count = count_ref[e]
    g_first = start_ref[e] // MOE_TILE
    n_tiles = (count + MOE_TILE - 1) // MOE_TILE
    g_total = gtot_ref[0]
    hidden = wd_s.shape[0]
    half = xbuf.shape[-1]

    def rows_of(g):
        return pl.ds(pl.multiple_of(g * MOE_TILE, MOE_TILE), MOE_TILE)

    def copy_in(g):
        slot = g % MOE_BUFFERS
        return pltpu.make_async_copy(xs_hbm.at[rows_of(g)], xbuf.at[slot], sem_in.at[slot])

    def copy_out(g):
        slot = g % MOE_BUFFERS
        return pltpu.make_async_copy(ybuf.at[slot], ys_hbm.at[rows_of(g)], sem_out.at[slot])

    @pl.when(e == 0)
    def _():
        for g in range(MOE_LOOKAHEAD):
            @pl.when(g < g_total)
            def _():
                copy_in(g).start()

    @pl.when(n_tiles > 0)
    def _():
        wgu_s[:, 0:hidden] = wg_ref[0].astype(BF16)
        wgu_s[:, hidden:2 * hidden] = wu_ref[0].astype(BF16)
        wd_s[...] = wd_ref[0].astype(BF16)

    def tile(t, carry):
        g = g_first + t

        @pl.when(g + MOE_LOOKAHEAD < g_total)
        def _():
            copy_in(g + MOE_LOOKAHEAD).start()

        copy_in(g).wait()

        @pl.when(g >= MOE_BUFFERS)
        def _():
            copy_out(g - MOE_BUFFERS).wait()

        slot = g % MOE_BUFFERS
        row = lax.broadcasted_iota(I32, (MOE_TILE, half), 0)
        x_lo, x_hi = _unpack_bf16_pair(jnp.where(row < count - t * MOE_TILE, xbuf[slot], jnp.uint32(0)))
        ab = (jnp.dot(x_lo.astype(BF16), wgu_s[0:half, :], preferred_element_type=F32)
              + jnp.dot(x_hi.astype(BF16), wgu_s[half:2 * half, :], preferred_element_type=F32))
        a = ab[:, 0:hidden]
        he = a * _sigmoid(a) * ab[:, hidden:2 * hidden]
        y = jnp.dot(he.astype(BF16), wd_s[...], preferred_element_type=F32)
        ybuf[slot] = _pack_bf16_pair(y[:, 0:half], y[:, half:2 * half])
        copy_out(g).start()
        return carry

    lax.fori_loop(0, n_tiles, tile, 0)

    @pl.when(e == pl.num_programs(0) - 1)
    def _():
        for k in range(1, MOE_BUFFERS + 1):
            @pl.when(g_total >= k)
            def _():
                copy_out(g_total - k).wait()


def _moe_call(starts, counts, g_total, xs, w_g, w_u, w_d):
    n_rows, half = xs.shape
    n_experts, d_model, hidden = w_g.shape
    grid_spec = pltpu.PrefetchScalarGridSpec(
        num_scalar_prefetch=3,
        grid=(n_experts,),
        in_specs=[
            pl.BlockSpec(memory_space=pl.ANY),
            pl.BlockSpec((1, d_model, hidden), lambda e, st, ct, gt: (e, 0, 0)),
            pl.BlockSpec((1, d_model, hidden), lambda e, st, ct, gt: (e, 0, 0)),
            pl.BlockSpec((1, hidden, d_model), lambda e, st, ct, gt: (e, 0, 0)),
        ],
        out_specs=pl.BlockSpec(memory_space=pl.ANY),
        scratch_shapes=[
            pltpu.VMEM((d_model, 2 * hidden), BF16),
            pltpu.VMEM((hidden, d_model), BF16),
            pltpu.VMEM((MOE_BUFFERS, MOE_TILE, half), U32),
            pltpu.VMEM((MOE_BUFFERS, MOE_TILE, half), U32),
            pltpu.SemaphoreType.DMA((MOE_BUFFERS,)),
            pltpu.SemaphoreType.DMA((MOE_BUFFERS,)),
        ],
    )
    return pl.pallas_call(
        _moe_kernel, grid_spec=grid_spec,
        out_shape=jax.ShapeDtypeStruct((n_rows, half), U32), name="moe_experts",
        compiler_params=pltpu.CompilerParams(
            dimension_semantics=("arbitrary",), vmem_limit_bytes=VMEM_LIMIT),
    )(starts, counts, g_total, xs, w_g, w_u, w_d)


def _combine_kernel(x1_ref, y0_ref, y1_ref, rw_ref, gf_ref, *rest):
    out_ref = rest[-1]
    tr = x1_ref.shape[0]
    w_rows = jnp.concatenate([rw_ref[0], jnp.zeros((LANES - rw_ref.shape[1], tr), F32)], axis=0)
    w_cols = w_rows.T
    w0, w1 = w_cols[:, 0:1], w_cols[:, 1:2]
    a_lo, a_hi = _unpack_bf16_pair(y0_ref[0])
    b_lo, b_hi = _unpack_bf16_pair(y1_ref[0])
    moe = jnp.concatenate([w0 * a_lo + w1 * b_lo, w0 * a_hi + w1 * b_hi], axis=-1)
    out_ref[...] = _rms(x1_ref[...] + moe, gf_ref[...])


def _combine_call(x1, yg, yg_row0, rw, gf, out_rows, out_row0, prev_out=None):
    t, d_model = x1.shape
    tr = rw.shape[-1]
    half = yg.shape[-1]
    assert t % tr == 0 and yg_row0 % tr == 0 and out_row0 % tr == 0 and rw.shape == (t // tr, EXPERTS_PER_GROUP, tr)
    yoff, ooff = yg_row0 // tr, out_row0 // tr
    in_specs = [
        pl.BlockSpec((tr, d_model), lambda i: (i, 0)),
        pl.BlockSpec((1, tr, half), lambda i: (0, yoff + i, 0)),
        pl.BlockSpec((1, tr, half), lambda i: (1, yoff + i, 0)),
        pl.BlockSpec((1, EXPERTS_PER_GROUP, tr), lambda i: (i, 0, 0)),
        pl.BlockSpec((1, d_model), lambda i: (0, 0)),
    ]
    args = [x1, yg, yg, rw, gf]
    aliases = {}
    if prev_out is not None:
        in_specs.append(pl.BlockSpec(memory_space=pl.ANY))
        args.append(prev_out)
        aliases = {len(args) - 1: 0}
    return pl.pallas_call(
        _combine_kernel,
        grid=(t // tr,),
        in_specs=in_specs,
        out_specs=pl.BlockSpec((tr, d_model), lambda i: (ooff + i, 0)),
        out_shape=jax.ShapeDtypeStruct((out_rows, d_model), F32), name=f"combine_row{out_row0}_of{out_rows}",
        input_output_aliases=aliases,
        compiler_params=pltpu.CompilerParams(
            dimension_semantics=("arbitrary",), vmem_limit_bytes=VMEM_LIMIT),
    )(*args)


def _route_and_run_experts(streams, w_g, w_u, w_d):
    half = streams[0][0].shape[-1]
    tokens = [h2.shape[0] for h2, _, _ in streams]
    counts = [cnt[:, 0].astype(I32) for _, _, cnt in streams]
    total = sum(counts)
    padded = ((total + MOE_TILE - 1) // MOE_TILE) * MOE_TILE
    ends = jnp.cumsum(padded)
    starts = ends - padded
    experts = jnp.arange(N_EXPERTS, dtype=I32)[None, :, None]
    pos, base = [], starts
    for (_, ri, _), t, cnt in zip(streams, tokens, counts):
        ri = jnp.moveaxis(ri, 2, 0).reshape(ri.shape[2], t)
        first_row = jnp.sum(jnp.where(ri[0:2, None, :] == experts, base[None, :, None], 0), axis=1)
        pos.append(ri[2:4] + first_row)
        base = base + cnt
    pos = jnp.concatenate(pos, axis=1)
    t_all = sum(tokens)
    n_rows = ((2 * t_all + N_EXPERTS * (MOE_TILE - 1)) // MOE_TILE) * MOE_TILE
    xs_sorted = _sc_dispatch([h2 for h2, _, _ in streams], pos[0], pos[1], n_rows)
    ys_sorted = _moe_call(starts.astype(I32), total, (ends[-1:] // MOE_TILE).astype(I32), xs_sorted, w_g, w_u, w_d)
    return _sc_gather(ys_sorted, pos.reshape(2 * t_all)).reshape(2, t_all, half)


def _one_layer(xp, xs, s_ret, c_pool, norm1_g, w_in, ret_norm_g, w_pool, pool_scale, w_out, norm2_g,
               w_rg, w_re, w_g, w_u, w_d, final_g, past_len):
    bp, seq, d_model = xp.shape
    bs, dseq, _ = xs.shape
    rw_width = ret_norm_g.shape[-1]
    pw = pool_scale.shape[-1]
    dh = rw_width // RET_HEADS
    half = d_model // 2

    w_r = jnp.concatenate(
        [w_re, w_rg, jnp.zeros((d_model, LANES - N_EXPERTS - N_EXPERT_GROUPS), F32)], axis=1)
    wr_hi = w_r.astype(BF16)
    wr = jnp.concatenate([wr_hi, (w_r - wr_hi.astype(F32)).astype(BF16)], axis=1)
    consts = dict(
        g1=norm1_g.reshape(1, d_model), w_in=w_in.astype(BF16), gret=ret_norm_g.reshape(1, rw_width),
        w_pool=w_pool.astype(BF16), pscale=pool_scale.reshape(1, pw), w_out=w_out.astype(BF16),
        g2=norm2_g.reshape(1, d_model), wr=wr)

    gf = final_g.reshape(1, d_model)

    bh = bp // 2
    th = bh * seq
    ts = bs * dseq
    s0p = jnp.zeros((bh, RET_HEADS, dh, dh), F32)
    h0p = jnp.zeros((bh, HIST_ROWS, pw), F32)
    h0s = jnp.pad(c_pool, ((0, 0), (HIST_ROWS - POOL_HIST, 0), (0, 0)))
    prompt_tile = dict(bb=1, tl=PROMPT_TILE, chunk=256)

    def stream(layer_out, t):
        x1, h2, ri, rw, st, hist, cnt = layer_out
        return dict(x1=x1.reshape(t, d_model), route=(h2.reshape(t, half), ri, cnt),
                    rw=rw.reshape(-1, EXPERTS_PER_GROUP, rw.shape[-1]), st=st, hist=hist)

    pa = stream(_layer_call(xp, 0, bh, s0p, h0p, 0, consts, **prompt_tile), th)
    yg_a = _route_and_run_experts([pa["route"]], w_g, w_u, w_d)
    pb = stream(_layer_call(xp, bh, bp - bh, s0p, h0p, 0, consts, **prompt_tile), th)
    sm = stream(_layer_call(xs, 0, bs, s_ret, h0s, past_len, consts, bb=bs, tl=dseq, chunk=min(64, dseq)), ts)
    yg_b = _route_and_run_experts([pb["route"], sm["route"]], w_g, w_u, w_d)

    yp = _combine_call(pa["x1"], yg_a, 0, pa["rw"], gf, bp * seq, 0)
    yp = _combine_call(pb["x1"], yg_b, 0, pb["rw"], gf, bp * seq, th, prev_out=yp)
    ysm = _combine_call(sm["x1"], yg_b, th, sm["rw"], gf, ts, 0)
    st_p = jnp.concatenate([pa["st"], pb["st"]], axis=0)
    hist_p = jnp.concatenate([pa["hist"], pb["hist"]], axis=0)
    return (yp.reshape(bp, seq, d_model), ysm.reshape(bs, dseq, d_model),
            st_p, hist_p[:, HIST_ROWS - POOL_HIST:], sm["st"], sm["hist"][:, HIST_ROWS - POOL_HIST:])


def kernel(x_prompt, x_sample, state_ret, cache_pool, norm1_g, w_in, ret_norm_g, w_pool, pool_scale, w_out,
           norm2_g, w_router_group, w_router_expert, w_exp_gate, w_exp_up, w_exp_down, final_norm_g):
    depth = w_in.shape[0]
    assert depth == 1, "the final RMSNorm is fused into the layer's combine kernel"
    assert x_prompt.shape[0] % 2 == 0 and x_prompt.shape[1] % PROMPT_TILE == 0
    yp, ys, s_p, h_p, s_s, h_s = _one_layer(
        x_prompt, x_sample, state_ret[0], cache_pool[0], norm1_g[0], w_in[0], ret_norm_g[0], w_pool[0],
        pool_scale[0], w_out[0], norm2_g[0], w_router_group[0], w_router_expert[0],
        w_exp_gate[0], w_exp_up[0], w_exp_down[0], final_norm_g, PAST_LEN)
    return (yp, ys, s_p[None], h_p[None], s_s[None], h_s[None])
```

```python
import functools

import jax
import jax.numpy as jnp
from jax import lax
from jax.experimental import pallas as pl
from jax.experimental.pallas import tpu as pltpu
from jax.experimental.pallas import tpu_sc as plsc

F32 = jnp.float32
BF16 = jnp.bfloat16
I32 = jnp.int32
U32 = jnp.uint32

EPS = 1e-6
ROPE_BASE = 10000.0
RET_HEADS = 4
POOL_WINDOWS = (2, 4, 8, 16)
POOL_HIST = max(POOL_WINDOWS) - 1
N_EXPERT_GROUPS = 4
EXPERTS_PER_GROUP = 8
N_EXPERTS = N_EXPERT_GROUPS * EXPERTS_PER_GROUP
EXPERT_SHIFT = EXPERTS_PER_GROUP.bit_length() - 1
PAST_LEN = 1024

LANES = 128
HIST_ROWS = 16
MOE_TILE = 256
MOE_BUFFERS = 4
MOE_LOOKAHEAD = MOE_BUFFERS - 1
PROMPT_TILE = 1024
COMBINE_CHUNKS = (1, 1, 2)
SC_UNIT = 32
VMEM_LIMIT = 56 * 1024 * 1024


def _rms(x, g):
    return x * lax.rsqrt(jnp.mean(x * x, axis=-1, keepdims=True) + EPS) * g


def _sigmoid(x):
    return 1.0 / (1.0 + jnp.exp(-x))


def _pack_bf16_pair(lo, hi):
    lo_b = lax.bitcast_convert_type(lo.astype(BF16).astype(F32), U32)
    hi_b = lax.bitcast_convert_type(hi.astype(BF16).astype(F32), U32)
    return hi_b | (lo_b >> 16)


def _unpack_bf16_pair(p):
    lo = lax.bitcast_convert_type(p << 16, F32)
    hi = lax.bitcast_convert_type(p & jnp.uint32(0xFFFF0000), F32)
    return lo, hi


def _layer_kernel(dc_ref, x_ref, s0_ref, h0_ref, rb_ref, rc_ref, rs_ref, rcs_ref, rss_ref,
                  dintra_ref, dq_ref, dk_ref,
                  g1_ref, win_ref, gret_ref, wpool_ref, pscale_ref, wout_ref, g2_ref,
                  wr_ref, tri_ref,
                  x1_ref, h2_ref, ri_ref, rw_ref, st_ref, hist_ref, cnt_ref,
                  ue_ref, q_ref, k_ref, v_ref, o_ref, a_ref,
                  *, bb, tl, chunk, pos0):
    b_idx = pl.program_id(0)
    l_idx = pl.program_id(1)
    rows = bb * tl
    d_model = x_ref.shape[-1]
    rw_width = q_ref.shape[-1]
    dh = rw_width // RET_HEADS
    pw = ue_ref.shape[-1]
    gw = pw // len(POOL_WINDOWS)
    n_chunks = tl // chunk

    @pl.when(l_idx == 0)
    def _():
        st_ref[...] = s0_ref[...]
        ue_ref[:, 0:HIST_ROWS, :] = h0_ref[...]

    @pl.when((l_idx == 0) & (b_idx == 0))
    def _():
        cnt_ref[...] = jnp.zeros_like(cnt_ref)

    x = x_ref[...].reshape(rows, d_model)
    hb = _rms(x, g1_ref[...]).astype(BF16)

    def project(c0, c1):
        return jnp.dot(hb, win_ref[:, c0:c1], preferred_element_type=F32)

    proj = project(0, 2 * rw_width)

    cos_b = rb_ref[0, 0:1, :]
    sin_b = rb_ref[0, 1:2, :]
    cosf = (cos_b * rc_ref[...] - sin_b * rs_ref[...])[None]
    sinf = (sin_b * rcs_ref[...] + cos_b * rss_ref[...])[None]
    k_scale = dh ** -0.5
    for hh in range(RET_HEADS):
        qh = proj[:, hh * dh:(hh + 1) * dh]
        kh = proj[:, rw_width + hh * dh:rw_width + (hh + 1) * dh]
        qr = (qh.reshape(bb, tl, dh) * cosf
              + pltpu.roll(qh, dh // 2, 1).reshape(bb, tl, dh) * sinf).reshape(rows, dh)
        kr = (kh.reshape(bb, tl, dh) * cosf
              + pltpu.roll(kh, dh // 2, 1).reshape(bb, tl, dh) * sinf).reshape(rows, dh)
        q_ref[:, hh * dh:(hh + 1) * dh] = qr.astype(BF16)
        k_ref[:, hh * dh:(hh + 1) * dh] = kr * k_scale
    v_ref[...] = project(2 * rw_width, 3 * rw_width).astype(BF16)
    gate = project(3 * rw_width, 4 * rw_width)
    u = project(4 * rw_width, 4 * rw_width + pw)

    def ret_block(b, c):
        r0 = b * tl + c * chunk
        if not isinstance(r0, int):
            r0 = pl.multiple_of(r0, chunk)
        for hh in range(RET_HEADS):
            cs = slice(hh * dh, (hh + 1) * dh)
            qc = q_ref[pl.ds(r0, chunk), cs]
            kf = k_ref[pl.ds(r0, chunk), cs]
            vc = v_ref[pl.ds(r0, chunk), cs]
            s_old = st_ref[b, hh]
            sc = lax.dot_general(qc, kf.astype(BF16), (((1,), (1,)), ((), ())),
                                 preferred_element_type=F32) * dintra_ref[hh]
            o = (jnp.dot(sc.astype(BF16), vc, preferred_element_type=F32)
                 + dq_ref[hh] * jnp.dot(qc, s_old.astype(BF16), preferred_element_type=F32))
            kd = (kf * dk_ref[hh]).astype(BF16)
            s_new = dc_ref[hh] * s_old + lax.dot_general(
                kd, vc, (((0,), (0,)), ((), ())), preferred_element_type=F32)
            st_ref[b, hh] = s_new
            o_ref[pl.ds(r0, chunk), cs] = o

    if bb * n_chunks <= 4:
        for b in range(bb):
            for c in range(n_chunks):
                ret_block(b, c)
    else:
        def body(i, carry):
            ret_block(i // n_chunks, i % n_chunks)
            return carry
        lax.fori_loop(0, bb * n_chunks, body, 0)

    for hh in range(RET_HEADS):
        cs = slice(hh * dh, (hh + 1) * dh)
        oh = o_ref[:, cs]
        mu = jnp.mean(oh, axis=-1, keepdims=True)
        oc = oh - mu
        var = jnp.mean(oc * oc, axis=-1, keepdims=True)
        y = oc * lax.rsqrt(var + EPS) * gret_ref[:, cs]
        g = gate[:, cs]
        a_ref[:, cs] = (g * _sigmoid(g) * y).astype(BF16)

    ue_ref[:, HIST_ROWS:HIST_ROWS + tl, :] = u.reshape(bb, tl, pw)
    pos = pos0 + l_idx * tl + lax.broadcasted_iota(I32, (1, tl, 1), 1)
    for gi, w in enumerate(POOL_WINDOWS):
        cs = slice(gi * gw, (gi + 1) * gw)
        acc = ue_ref[:, HIST_ROWS:HIST_ROWS + tl, cs]
        for j in range(1, w):
            acc = acc + ue_ref[:, HIST_ROWS - j:HIST_ROWS - j + tl, cs]
        inv_cnt = 1.0 / jnp.minimum(pos + 1, w).astype(F32)
        p = (acc * inv_cnt).reshape(rows, gw) - u[:, cs]
        z = jnp.dot(p.astype(BF16), wpool_ref[gi], preferred_element_type=F32) * pscale_ref[:, cs]
        a_ref[:, rw_width + gi * gw:rw_width + (gi + 1) * gw] = z.astype(BF16)
    tail = ue_ref[:, tl:tl + HIST_ROWS, :]
    ue_ref[:, 0:HIST_ROWS, :] = tail
    hist_ref[...] = tail

    x1 = x + jnp.dot(a_ref[...], wout_ref[...], preferred_element_type=F32)
    x1_ref[...] = x1.reshape(bb, tl, d_model)
    h2 = _rms(x1, g2_ref[...])
    h2_ref[...] = _pack_bf16_pair(h2[:, 0:d_model // 2], h2[:, d_model // 2:]).reshape(bb, tl, d_model // 2)

    h2_hi = h2.astype(BF16)
    h2_lo = (h2 - h2_hi.astype(F32)).astype(BF16)
    two = jnp.dot(h2_hi, wr_ref[...], preferred_element_type=F32)
    logits = (two[:, 0:LANES] + two[:, LANES:2 * LANES]
              + jnp.dot(h2_lo, wr_ref[:, 0:LANES], preferred_element_type=F32))
    lt = logits.T
    neg = jnp.float32(-jnp.inf)
    big = jnp.float32(1e9)
    sub = lax.broadcasted_iota(I32, (EXPERTS_PER_GROUP, rows), 0).astype(F32)
    gl = jnp.where(sub < N_EXPERT_GROUPS, lt[N_EXPERTS:N_EXPERTS + EXPERTS_PER_GROUP], neg)
    gmax = jnp.max(gl, axis=0, keepdims=True)
    gidx = jnp.min(jnp.where(gl == gmax, sub, big), axis=0, keepdims=True)
    p_sel = 1.0 / jnp.sum(jnp.exp(gl - gmax), axis=0, keepdims=True)
    el = lt[0:EXPERTS_PER_GROUP]
    for g in range(1, N_EXPERT_GROUPS):
        el = jnp.where(gidx == g, lt[g * EXPERTS_PER_GROUP:(g + 1) * EXPERTS_PER_GROUP], el)
    m1 = jnp.max(el, axis=0, keepdims=True)
    t1 = jnp.min(jnp.where(el == m1, sub, big), axis=0, keepdims=True)
    el2 = jnp.where(sub == t1, neg, el)
    m2 = jnp.max(el2, axis=0, keepdims=True)
    t2 = jnp.min(jnp.where(el2 == m2, sub, big), axis=0, keepdims=True)
    e2 = jnp.exp(m2 - m1)
    w1 = p_sel / (1.0 + e2)
    w2 = p_sel * e2 / (1.0 + e2)
    i1 = gidx * EXPERTS_PER_GROUP + t1
    i2 = gidx * EXPERTS_PER_GROUP + t2

    eid = lax.broadcasted_iota(I32, (N_EXPERTS, rows), 0).astype(F32)
    hit1 = eid == i1
    hit2 = eid == i2
    onehot = (hit1 | hit2).astype(BF16)
    before = jnp.dot(onehot, tri_ref[...], preferred_element_type=F32) + cnt_ref[...]
    r1 = jnp.sum(jnp.where(hit1, before, 0.0), axis=0, keepdims=True)
    r2 = jnp.sum(jnp.where(hit2, before, 0.0), axis=0, keepdims=True)
    cnt_ref[...] = cnt_ref[...] + jnp.sum(onehot.astype(F32), axis=1, keepdims=True)

    ri = jnp.where(sub == 0, i1, jnp.where(sub == 1, i2, jnp.where(sub == 2, r1, jnp.where(sub == 3, r2, 0.0))))
    ri_ref[...] = ri.astype(I32).reshape(ri_ref.shape)
    rw_ref[...] = jnp.where(sub == 0, w1, jnp.where(sub == 1, w2, 0.0)).reshape(rw_ref.shape)


def _rope_tables(pos0, seq, tl, dh):
    half = dh // 2
    inv = ROPE_BASE ** (-jnp.arange(half, dtype=F32) / half)
    ang_t = jnp.arange(tl, dtype=F32)[:, None] * inv[None, :]
    ang_b = (pos0 + tl * jnp.arange(seq // tl)).astype(F32)[:, None] * inv[None, :]
    dup = lambda a: jnp.concatenate([a, a], axis=-1)
    sgn = lambda a: jnp.concatenate([-a, a], axis=-1)
    base = jnp.stack([dup(jnp.cos(ang_b)), dup(jnp.sin(ang_b))], axis=1)
    base = jnp.pad(base, ((0, 0), (0, 8 - base.shape[1]), (0, 0)))
    cos_t, sin_t = jnp.cos(ang_t), jnp.sin(ang_t)
    return base, dup(cos_t), dup(sin_t), sgn(cos_t), sgn(sin_t)


def _layer_call(x, b0, nb, s0, h0, pos0, consts, *, bb, tl, chunk):
    _, seq, d_model = x.shape
    bsz = nb
    blk0 = b0 // bb
    rows = bb * tl
    rw_width = consts["gret"].shape[-1]
    pw = consts["pscale"].shape[-1]
    dh = rw_width // RET_HEADS

    rope = _rope_tables(pos0, seq, tl, dh)

    lg = jnp.log1p(-jnp.exp2(-5.0 - jnp.arange(RET_HEADS, dtype=F32)))
    idx = jnp.arange(chunk, dtype=F32)
    diff = idx[:, None] - idx[None, :]
    d_intra = jnp.where(diff[None] >= 0, jnp.exp(jnp.maximum(diff, 0.0)[None] * lg[:, None, None]), 0.0)
    d_q = jnp.broadcast_to(jnp.exp((idx + 1.0)[None, :] * lg[:, None])[:, :, None], (RET_HEADS, chunk, dh))
    d_k = jnp.broadcast_to(jnp.exp((chunk - 1.0 - idx)[None, :] * lg[:, None])[:, :, None], (RET_HEADS, chunk, dh))
    d_c = jnp.exp(chunk * lg)
    tri = jnp.triu(jnp.ones((rows, rows), BF16), 1)

    const2 = lambda b, l, *_: (0, 0)
    const3 = lambda b, l, *_: (0, 0, 0)
    grid_spec = pltpu.PrefetchScalarGridSpec(
        num_scalar_prefetch=0,
        grid=(bsz // bb, seq // tl),
        in_specs=[
            pl.BlockSpec(memory_space=pltpu.SMEM),
            pl.BlockSpec((bb, tl, d_model), lambda b, l: (blk0 + b, l, 0)),
            pl.BlockSpec((bb, RET_HEADS, dh, dh), lambda b, l: (b, 0, 0, 0)),
            pl.BlockSpec((bb, HIST_ROWS, pw), lambda b, l: (b, 0, 0)),
            pl.BlockSpec((1, 8, dh), lambda b, l: (l, 0, 0)),
            pl.BlockSpec((tl, dh), const2),
            pl.BlockSpec((tl, dh), const2),
            pl.BlockSpec((tl, dh), const2),
            pl.BlockSpec((tl, dh), const2),
            pl.BlockSpec((RET_HEADS, chunk, chunk), const3),
            pl.BlockSpec((RET_HEADS, chunk, dh), const3),
            pl.BlockSpec((RET_HEADS, chunk, dh), const3),
            pl.BlockSpec((1, d_model), const2),
            pl.BlockSpec(consts["w_in"].shape, const2),
            pl.BlockSpec((1, rw_width), const2),
            pl.BlockSpec(consts["w_pool"].shape, const3),
            pl.BlockSpec((1, pw), const2),
            pl.BlockSpec(consts["w_out"].shape, const2),
            pl.BlockSpec((1, d_model), const2),
            pl.BlockSpec((d_model, 2 * LANES), const2),
            pl.BlockSpec((rows, rows), const2),
        ],
        out_specs=[
            pl.BlockSpec((bb, tl, d_model), lambda b, l: (b, l, 0)),
            pl.BlockSpec((bb, tl, d_model // 2), lambda b, l: (b, l, 0)),
            pl.BlockSpec((1, 1, EXPERTS_PER_GROUP, rows), lambda b, l: (b, l, 0, 0)),
            pl.BlockSpec((1, 1, EXPERTS_PER_GROUP, rows), lambda b, l: (b, l, 0, 0)),
            pl.BlockSpec((bb, RET_HEADS, dh, dh), lambda b, l: (b, 0, 0, 0)),
            pl.BlockSpec((bb, HIST_ROWS, pw), lambda b, l: (b, 0, 0)),
            pl.BlockSpec((N_EXPERTS, rows), const2),
        ],
        scratch_shapes=[
            pltpu.VMEM((bb, HIST_ROWS + tl, pw), F32),
            pltpu.VMEM((rows, rw_width), BF16),
            pltpu.VMEM((rows, rw_width), F32),
            pltpu.VMEM((rows, rw_width), BF16),
            pltpu.VMEM((rows, rw_width), F32),
            pltpu.VMEM((rows, d_model), BF16),
        ],
    )
    out_shape = [
        jax.ShapeDtypeStruct((bsz, seq, d_model), F32),
        jax.ShapeDtypeStruct((bsz, seq, d_model // 2), U32),
        jax.ShapeDtypeStruct((bsz // bb, seq // tl, EXPERTS_PER_GROUP, rows), I32),
        jax.ShapeDtypeStruct((bsz // bb, seq // tl, EXPERTS_PER_GROUP, rows), F32),
        jax.ShapeDtypeStruct((bsz, RET_HEADS, dh, dh), F32),
        jax.ShapeDtypeStruct((bsz, HIST_ROWS, pw), F32),
        jax.ShapeDtypeStruct((N_EXPERTS, rows), F32),
    ]
    kern = functools.partial(_layer_kernel, bb=bb, tl=tl, chunk=chunk, pos0=pos0)
    return pl.pallas_call(
        kern, grid_spec=grid_spec, out_shape=out_shape, name=f"layer_pos{pos0}_b{b0}",
        compiler_params=pltpu.CompilerParams(
            dimension_semantics=("arbitrary", "arbitrary"), vmem_limit_bytes=VMEM_LIMIT),
    )(d_c, x, s0, h0, *rope, d_intra, d_q, d_k,
      consts["g1"], consts["w_in"], consts["gret"], consts["w_pool"], consts["pscale"],
      consts["w_out"], consts["g2"], consts["wr"], tri)


def _sc_partition(n_units):
    info = plsc.get_sparse_core_info()
    nc, nw = info.num_cores, info.num_cores * info.num_subcores
    upw = -(-n_units // nw)
    upw += upw % 2
    return nc, nw, upw


def _units_by_worker(idx, n_units, upw, nw):
    idx = jnp.pad(idx.reshape(n_units, SC_UNIT), ((0, nw * upw - n_units), (0, 0)))
    return idx.reshape(upw, nw, SC_UNIT).transpose(1, 0, 2)


def _sc_dispatch(srcs, idx0, idx1, n_out_rows):
    assert 1 <= len(srcs) <= 2
    d = srcs[0].shape[1]
    dtype = srcs[0].dtype
    assert all(src.shape[0] % SC_UNIT == 0 for src in srcs)
    units_a = srcs[0].shape[0] // SC_UNIT
    n_units = sum(src.shape[0] for src in srcs) // SC_UNIT
    nc, nw, upw = _sc_partition(n_units)
    idx0 = _units_by_worker(idx0, n_units, upw, nw)
    idx1 = _units_by_worker(idx1, n_units, upw, nw)
    mesh = plsc.VectorSubcoreMesh(core_axis_name="c", subcore_axis_name="s")
    dma = pltpu.SemaphoreType.DMA

    @functools.partial(
        pl.kernel, mesh=mesh,
        out_type=jax.ShapeDtypeStruct((n_out_rows, d), dtype),
        scratch_types=[
            pltpu.VMEM((upw, SC_UNIT), I32),
            pltpu.VMEM((upw, SC_UNIT), I32),
            pltpu.VMEM((SC_UNIT, d), dtype),
            pltpu.VMEM((SC_UNIT, d), dtype),
            dma, dma, dma, dma, dma, dma,
        ],
    )
    def k(*refs):
        src_hbm = refs[:len(srcs)]
        i0_hbm, i1_hbm, out_hbm, i0_v, i1_v, rows0, rows1, l0, l1, p0, p1, q0, q1 = refs[len(srcs):]
        wid = lax.axis_index("s") * nc + lax.axis_index("c")
        pltpu.sync_copy(i0_hbm.at[wid], i0_v)
        pltpu.sync_copy(i1_hbm.at[wid], i1_v)
        rows, lsem, psem, qsem = (rows0, rows1), (l0, l1), (p0, p1), (q0, q1)

        def live(j):
            return j * nw + wid < n_units

        def load(j, b, op):
            unit = j * nw + wid

            @pl.when(live(j) & (unit < units_a))
            def _():
                op(pltpu.make_async_copy(
                    src_hbm[0].at[pl.ds(pl.multiple_of(unit * SC_UNIT, 8), SC_UNIT)], rows[b], lsem[b]))

            if len(srcs) == 2:
                @pl.when(live(j) & (unit >= units_a))
                def _():
                    op(pltpu.make_async_copy(
                        src_hbm[1].at[pl.ds(pl.multiple_of((unit - units_a) * SC_UNIT, 8), SC_UNIT)],
                        rows[b], lsem[b]))

        def scatter(j, b, op):
            @pl.when(live(j))
            def _():
                op(pltpu.make_async_copy(rows[b], out_hbm.at[i0_v.at[j]], psem[b]))
                op(pltpu.make_async_copy(rows[b], out_hbm.at[i1_v.at[j]], qsem[b]))

        start = lambda c: c.start()
        wait = lambda c: c.wait()
        load(0, 0, start)

        @pl.loop(0, upw, step=2)
        def _(j):
            @pl.when(j > 0)
            def _():
                scatter(j - 1, 1, wait)
            load(j + 1, 1, start)
            load(j, 0, wait)
            scatter(j, 0, start)
            scatter(j, 0, wait)

            @pl.when(j + 2 < upw)
            def _():
                load(j + 2, 0, start)
            load(j + 1, 1, wait)
            scatter(j + 1, 1, start)

        scatter(upw - 1, 1, wait)

    return k(*srcs, idx0, idx1)


def _sc_gather(table, idx):
    n = idx.shape[0]
    d = table.shape[1]
    assert n % SC_UNIT == 0
    n_units = n // SC_UNIT
    nc, nw, upw = _sc_partition(n_units)
    idx = _units_by_worker(idx, n_units, upw, nw)
    mesh = plsc.VectorSubcoreMesh(core_axis_name="c", subcore_axis_name="s")
    dma = pltpu.SemaphoreType.DMA

    @functools.partial(
        pl.kernel, mesh=mesh,
        out_type=jax.ShapeDtypeStruct((n, d), table.dtype),
        scratch_types=[
            pltpu.VMEM((upw, SC_UNIT), I32),
            pltpu.VMEM((SC_UNIT, d), table.dtype),
            pltpu.VMEM((SC_UNIT, d), table.dtype),
            dma, dma, dma, dma,
        ],
    )
    def k(t_hbm, i_hbm, out_hbm, i_v, rows0, rows1, g0, g1, w0, w1):
        wid = lax.axis_index("s") * nc + lax.axis_index("c")
        pltpu.sync_copy(i_hbm.at[wid], i_v)
        rows, gsem, wsem = (rows0, rows1), (g0, g1), (w0, w1)

        def live(j):
            return j * nw + wid < n_units

        def gather(j, b, op):
            @pl.when(live(j))
            def _():
                op(pltpu.make_async_copy(t_hbm.at[i_v.at[j]], rows[b], gsem[b]))

        def write(j, b, op):
            @pl.when(live(j))
            def _():
                op(pltpu.make_async_copy(
                    rows[b], out_hbm.at[pl.ds(pl.multiple_of((j * nw + wid) * SC_UNIT, 8), SC_UNIT)], wsem[b]))

        start = lambda c: c.start()
        wait = lambda c: c.wait()
        gather(0, 0, start)

        @pl.loop(0, upw, step=2)
        def _(j):
            @pl.when(j > 0)
            def _():
                write(j - 1, 1, wait)
            gather(j + 1, 1, start)
            gather(j, 0, wait)
            write(j, 0, start)
            write(j, 0, wait)

            @pl.when(j + 2 < upw)
            def _():
                gather(j + 2, 0, start)
            gather(j + 1, 1, wait)
            write(j + 1, 1, start)

        write(upw - 1, 1, wait)

    return k(table, idx)


def _moe_kernel(start_ref, count_ref, gtot_ref, xs_hbm, wg_ref, wu_ref, wd_ref, ys_hbm,
                wgu_s, wd_s, xbuf, ybuf, sem_in, sem_out):
    e = pl.program_id(0)
    count = count_ref[e]
    g_first = start_ref[e] // MOE_TILE
    n_tiles = (count + MOE_TILE - 1) // MOE_TILE
    g_total = gtot_ref[0]
    hidden = wd_s.shape[0]
    half = xbuf.shape[-1]

    def rows_of(g):
        return pl.ds(pl.multiple_of(g * MOE_TILE, MOE_TILE), MOE_TILE)

    def copy_in(g):
        slot = g % MOE_BUFFERS
        return pltpu.make_async_copy(xs_hbm.at[rows_of(g)], xbuf.at[slot], sem_in.at[slot])

    def copy_out(g):
        slot = g % MOE_BUFFERS
        return pltpu.make_async_copy(ybuf.at[slot], ys_hbm.at[rows_of(g)], sem_out.at[slot])

    @pl.when(e == 0)
    def _():
        for g in range(MOE_LOOKAHEAD):
            @pl.when(g < g_total)
            def _():
                copy_in(g).start()

    @pl.when(n_tiles > 0)
    def _():
        wgu_s[:, 0:hidden] = wg_ref[0].astype(BF16)
        wgu_s[:, hidden:2 * hidden] = wu_ref[0].astype(BF16)
        wd_s[...] = wd_ref[0].astype(BF16)

    def tile(t, carry):
        g = g_first + t

        @pl.when(g + MOE_LOOKAHEAD < g_total)
        def _():
            copy_in(g + MOE_LOOKAHEAD).start()

        copy_in(g).wait()

        @pl.when(g >= MOE_BUFFERS)
        def _():
            copy_out(g - MOE_BUFFERS).wait()

        slot = g % MOE_BUFFERS
        row = lax.broadcasted_iota(I32, (MOE_TILE, half), 0)
        x_lo, x_hi = _unpack_bf16_pair(jnp.where(row < count - t * MOE_TILE, xbuf[slot], jnp.uint32(0)))
        ab = (jnp.dot(x_lo.astype(BF16), wgu_s[0:half, :], preferred_element_type=F32)
              + jnp.dot(x_hi.astype(BF16), wgu_s[half:2 * half, :], preferred_element_type=F32))
        a = ab[:, 0:hidden]
        he = a * _sigmoid(a) * ab[:, hidden:2 * hidden]
        y = jnp.dot(he.astype(BF16), wd_s[...], preferred_element_type=F32)
        ybuf[slot] = _pack_bf16_pair(y[:, 0:half], y[:, half:2 * half])
        copy_out(g).start()
        return carry

    lax.fori_loop(0, n_tiles, tile, 0)

    @pl.when(e == pl.num_programs(0) - 1)
    def _():
        for k in range(1, MOE_BUFFERS + 1):
            @pl.when(g_total >= k)
            def _():
                copy_out(g_total - k).wait()


def _moe_call(starts, counts, g_total, xs, w_g, w_u, w_d):
    n_rows, half = xs.shape
    n_experts, d_model, hidden = w_g.shape
    grid_spec = pltpu.PrefetchScalarGridSpec(
        num_scalar_prefetch=3,
        grid=(n_experts,),
        in_specs=[
            pl.BlockSpec(memory_space=pl.ANY),
            pl.BlockSpec((1, d_model, hidden), lambda e, st, ct, gt: (e, 0, 0)),
            pl.BlockSpec((1, d_model, hidden), lambda e, st, ct, gt: (e, 0, 0)),
            pl.BlockSpec((1, hidden, d_model), lambda e, st, ct, gt: (e, 0, 0)),
        ],
        out_specs=pl.BlockSpec(memory_space=pl.ANY),
        scratch_shapes=[
            pltpu.VMEM((d_model, 2 * hidden), BF16),
            pltpu.VMEM((hidden, d_model), BF16),
            pltpu.VMEM((MOE_BUFFERS, MOE_TILE, half), U32),
            pltpu.VMEM((MOE_BUFFERS, MOE_TILE, half), U32),
            pltpu.SemaphoreType.DMA((MOE_BUFFERS,)),
            pltpu.SemaphoreType.DMA((MOE_BUFFERS,)),
        ],
    )
    return pl.pallas_call(
        _moe_kernel, grid_spec=grid_spec,
        out_shape=jax.ShapeDtypeStruct((n_rows, half), U32), name="moe_experts",
        compiler_params=pltpu.CompilerParams(
            dimension_semantics=("arbitrary",), vmem_limit_bytes=VMEM_LIMIT),
    )(starts, counts, g_total, xs, w_g, w_u, w_d)


def _combine_kernel(x1_ref, y0_ref, y1_ref, rw_ref, gf_ref, *rest):
    out_ref = rest[-1]
    tr = x1_ref.shape[0]
    w_rows = jnp.concatenate([rw_ref[0], jnp.zeros((LANES - rw_ref.shape[1], tr), F32)], axis=0)
    w_cols = w_rows.T
    w0, w1 = w_cols[:, 0:1], w_cols[:, 1:2]
    a_lo, a_hi = _unpack_bf16_pair(y0_ref[0])
    b_lo, b_hi = _unpack_bf16_pair(y1_ref[0])
    moe = jnp.concatenate([w0 * a_lo + w1 * b_lo, w0 * a_hi + w1 * b_hi], axis=-1)
    out_ref[...] = _rms(x1_ref[...] + moe, gf_ref[...])


def _combine_call(x1, rw, row0, n, yg, gf, prev_out=None):
    t, d_model = x1.shape
    tr = rw.shape[-1]
    half = yg.shape[-1]
    assert t % tr == 0 and row0 % tr == 0 and n % tr == 0 and rw.shape == (t // tr, EXPERTS_PER_GROUP, tr)
    assert yg.shape == (2, n, half)
    off = row0 // tr
    in_specs = [
        pl.BlockSpec((tr, d_model), lambda i: (off + i, 0)),
        pl.BlockSpec((1, tr, half), lambda i: (0, i, 0)),
        pl.BlockSpec((1, tr, half), lambda i: (1, i, 0)),
        pl.BlockSpec((1, EXPERTS_PER_GROUP, tr), lambda i: (off + i, 0, 0)),
        pl.BlockSpec((1, d_model), lambda i: (0, 0)),
    ]
    args = [x1, yg, yg, rw, gf]
    aliases = {}
    if prev_out is not None:
        in_specs.append(pl.BlockSpec(memory_space=pl.ANY))
        args.append(prev_out)
        aliases = {len(args) - 1: 0}
    return pl.pallas_call(
        _combine_kernel,
        grid=(n // tr,),
        in_specs=in_specs,
        out_specs=pl.BlockSpec((tr, d_model), lambda i: (off + i, 0)),
        out_shape=jax.ShapeDtypeStruct((t, d_model), F32), name=f"combine_row{row0}_of{t}",
        input_output_aliases=aliases,
        compiler_params=pltpu.CompilerParams(
            dimension_semantics=("arbitrary",), vmem_limit_bytes=VMEM_LIMIT),
    )(*args)


def _route_and_run_experts(streams, w_g, w_u, w_d):
    half = streams[0][0].shape[-1]
    tokens = [h2.shape[0] for h2, _, _ in streams]
    counts = [cnt[:, 0].astype(I32) for _, _, cnt in streams]
    total = sum(counts)
    padded = ((total + MOE_TILE - 1) // MOE_TILE) * MOE_TILE
    ends = jnp.cumsum(padded)
    starts = ends - padded
    experts = jnp.arange(N_EXPERTS, dtype=I32)[None, :, None]
    pos, base = [], starts
    for (_, ri, _), t, cnt in zip(streams, tokens, counts):
        ri = jnp.moveaxis(ri, 2, 0).reshape(ri.shape[2], t)
        first_row = jnp.sum(jnp.where(ri[0:2, None, :] == experts, base[None, :, None], 0), axis=1)
        pos.append(ri[2:4] + first_row)
        base = base + cnt
    pos = jnp.concatenate(pos, axis=1)
    t_all = sum(tokens)
    n_rows = ((2 * t_all + N_EXPERTS * (MOE_TILE - 1)) // MOE_TILE) * MOE_TILE
    xs_sorted = _sc_dispatch([h2 for h2, _, _ in streams], pos[0], pos[1], n_rows)
    ys_sorted = _moe_call(starts.astype(I32), total, (ends[-1:] // MOE_TILE).astype(I32), xs_sorted, w_g, w_u, w_d)
    return ys_sorted, pos


def _gather_tokens(ys_sorted, pos, t0, n):
    return _sc_gather(ys_sorted, pos[:, t0:t0 + n].reshape(2 * n)).reshape(2, n, ys_sorted.shape[-1])


def _one_layer(xp, xs, s_ret, c_pool, norm1_g, w_in, ret_norm_g, w_pool, pool_scale, w_out, norm2_g,
               w_rg, w_re, w_g, w_u, w_d, final_g, past_len):
    bp, seq, d_model = xp.shape
    bs, dseq, _ = xs.shape
    rw_width = ret_norm_g.shape[-1]
    pw = pool_scale.shape[-1]
    dh = rw_width // RET_HEADS
    half = d_model // 2

    w_r = jnp.concatenate(
        [w_re, w_rg, jnp.zeros((d_model, LANES - N_EXPERTS - N_EXPERT_GROUPS), F32)], axis=1)
    wr_hi = w_r.astype(BF16)
    wr = jnp.concatenate([wr_hi, (w_r - wr_hi.astype(F32)).astype(BF16)], axis=1)
    consts = dict(
        g1=norm1_g.reshape(1, d_model), w_in=w_in.astype(BF16), gret=ret_norm_g.reshape(1, rw_width),
        w_pool=w_pool.astype(BF16), pscale=pool_scale.reshape(1, pw), w_out=w_out.astype(BF16),
        g2=norm2_g.reshape(1, d_model), wr=wr)

    gf = final_g.reshape(1, d_model)

    ts = bs * dseq
    tp = bp * seq
    s0p = jnp.zeros((bp, RET_HEADS, dh, dh), F32)
    h0p = jnp.zeros((bp, HIST_ROWS, pw), F32)
    h0s = jnp.pad(c_pool, ((0, 0), (HIST_ROWS - POOL_HIST, 0), (0, 0)))

    def stream(layer_out, t):
        x1, h2, ri, rw, st, hist, cnt = layer_out
        return dict(x1=x1.reshape(t, d_model), route=(h2.reshape(t, half), ri, cnt),
                    rw=rw.reshape(-1, EXPERTS_PER_GROUP, rw.shape[-1]), st=st, hist=hist)

    pr = stream(_layer_call(xp, 0, bp, s0p, h0p, 0, consts, bb=1, tl=PROMPT_TILE, chunk=256), tp)
    sm = stream(_layer_call(xs, 0, bs, s_ret, h0s, past_len, consts, bb=bs, tl=dseq, chunk=min(64, dseq)), ts)
    ys_sorted, pos = _route_and_run_experts([pr["route"], sm["route"]], w_g, w_u, w_d)

    yp = None
    row0 = 0
    for n in COMBINE_CHUNKS:
        n = min(n * seq, tp - row0)
        yp = _combine_call(pr["x1"], pr["rw"], row0, n, _gather_tokens(ys_sorted, pos, row0, n), gf, prev_out=yp)
        row0 += n
    assert row0 == tp
    ysm = _combine_call(sm["x1"], sm["rw"], 0, ts, _gather_tokens(ys_sorted, pos, tp, ts), gf)
    return (yp.reshape(bp, seq, d_model), ysm.reshape(bs, dseq, d_model),
            pr["st"], pr["hist"][:, HIST_ROWS - POOL_HIST:], sm["st"], sm["hist"][:, HIST_ROWS - POOL_HIST:])


def kernel(x_prompt, x_sample, state_ret, cache_pool, norm1_g, w_in, ret_norm_g, w_pool, pool_scale, w_out,
           norm2_g, w_router_group, w_router_expert, w_exp_gate, w_exp_up, w_exp_down, final_norm_g):
    depth = w_in.shape[0]
    assert depth == 1, "the final RMSNorm is fused into the layer's combine kernel"
    assert x_prompt.shape[1] % PROMPT_TILE == 0
    yp, ys, s_p, h_p, s_s, h_s = _one_layer(
        x_prompt, x_sample, state_ret[0], cache_pool[0], norm1_g[0], w_in[0], ret_norm_g[0], w_pool[0],
        pool_scale[0], w_out[0], norm2_g[0], w_router_group[0], w_router_expert[0],
        w_exp_gate[0], w_exp_up[0], w_exp_down[0], final_norm_g, PAST_LEN)
    return (yp, ys, s_p[None], h_p[None], s_s[None], h_s[None])
```

```python
import functools

import jax
import jax.numpy as jnp
from jax import lax
from jax.experimental import pallas as pl
from jax.experimental.pallas import tpu as pltpu
from jax.experimental.pallas import tpu_sc as plsc

F32 = jnp.float32
BF16 = jnp.bfloat16
I32 = jnp.int32
U32 = jnp.uint32

EPS = 1e-6
ROPE_BASE = 10000.0
RET_HEADS = 4
POOL_WINDOWS = (2, 4, 8, 16)
POOL_HIST = max(POOL_WINDOWS) - 1
N_EXPERT_GROUPS = 4
EXPERTS_PER_GROUP = 8
N_EXPERTS = N_EXPERT_GROUPS * EXPERTS_PER_GROUP
EXPERT_SHIFT = EXPERTS_PER_GROUP.bit_length() - 1
PAST_LEN = 1024

LANES = 128
HIST_ROWS = 16
MOE_TILE = 256
MOE_BUFFERS = 4
MOE_LOOKAHEAD = MOE_BUFFERS - 1
PROMPT_TILE = 1024
COMBINE_CHUNKS = (1, 2)
SC_UNIT = 32
VMEM_LIMIT = 56 * 1024 * 1024


def _rms(x, g):
    return x * lax.rsqrt(jnp.mean(x * x, axis=-1, keepdims=True) + EPS) * g


def _sigmoid(x):
    return 1.0 / (1.0 + jnp.exp(-x))


def _pack_bf16_pair(lo, hi):
    lo_b = lax.bitcast_convert_type(lo.astype(BF16).astype(F32), U32)
    hi_b = lax.bitcast_convert_type(hi.astype(BF16).astype(F32), U32)
    return hi_b | (lo_b >> 16)


def _unpack_bf16_pair(p):
    lo = lax.bitcast_convert_type(p << 16, F32)
    hi = lax.bitcast_convert_type(p & jnp.uint32(0xFFFF0000), F32)
    return lo, hi


def _layer_kernel(dc_ref, x_ref, s0_ref, h0_ref, rb_ref, rc_ref, rs_ref, rcs_ref, rss_ref,
                  dintra_ref, dq_ref, dk_ref,
                  g1_ref, win_ref, gret_ref, wpool_ref, pscale_ref, wout_ref, g2_ref,
                  wr_ref, tri_ref,
                  x1_ref, h2_ref, ri_ref, rw_ref, st_ref, hist_ref, cnt_ref,
                  ue_ref, q_ref, k_ref, v_ref, o_ref, a_ref,
                  *, bb, tl, chunk, pos0):
    b_idx = pl.program_id(0)
    l_idx = pl.program_id(1)
    rows = bb * tl
    d_model = x_ref.shape[-1]
    rw_width = q_ref.shape[-1]
    dh = rw_width // RET_HEADS
    pw = ue_ref.shape[-1]
    gw = pw // len(POOL_WINDOWS)
    n_chunks = tl // chunk

    @pl.when(l_idx == 0)
    def _():
        st_ref[...] = s0_ref[...]
        ue_ref[:, 0:HIST_ROWS, :] = h0_ref[...]

    @pl.when((l_idx == 0) & (b_idx == 0))
    def _():
        cnt_ref[...] = jnp.zeros_like(cnt_ref)

    x = x_ref[...].reshape(rows, d_model)
    hb = _rms(x, g1_ref[...]).astype(BF16)

    def project(c0, c1):
        return jnp.dot(hb, win_ref[:, c0:c1], preferred_element_type=F32)

    proj = project(0, 2 * rw_width)

    cos_b = rb_ref[0, 0:1, :]
    sin_b = rb_ref[0, 1:2, :]
    cosf = (cos_b * rc_ref[...] - sin_b * rs_ref[...])[None]
    sinf = (sin_b * rcs_ref[...] + cos_b * rss_ref[...])[None]
    k_scale = dh ** -0.5
    for hh in range(RET_HEADS):
        qh = proj[:, hh * dh:(hh + 1) * dh]
        kh = proj[:, rw_width + hh * dh:rw_width + (hh + 1) * dh]
        qr = (qh.reshape(bb, tl, dh) * cosf
              + pltpu.roll(qh, dh // 2, 1).reshape(bb, tl, dh) * sinf).reshape(rows, dh)
        kr = (kh.reshape(bb, tl, dh) * cosf
              + pltpu.roll(kh, dh // 2, 1).reshape(bb, tl, dh) * sinf).reshape(rows, dh)
        q_ref[:, hh * dh:(hh + 1) * dh] = qr.astype(BF16)
        k_ref[:, hh * dh:(hh + 1) * dh] = kr * k_scale
    v_ref[...] = project(2 * rw_width, 3 * rw_width).astype(BF16)
    gate = project(3 * rw_width, 4 * rw_width)
    u = project(4 * rw_width, 4 * rw_width + pw)

    def ret_block(b, c):
        r0 = b * tl + c * chunk
        if not isinstance(r0, int):
            r0 = pl.multiple_of(r0, chunk)
        for hh in range(RET_HEADS):
            cs = slice(hh * dh, (hh + 1) * dh)
            qc = q_ref[pl.ds(r0, chunk), cs]
            kf = k_ref[pl.ds(r0, chunk), cs]
            vc = v_ref[pl.ds(r0, chunk), cs]
            s_old = st_ref[b, hh]
            sc = lax.dot_general(qc, kf.astype(BF16), (((1,), (1,)), ((), ())),
                                 preferred_element_type=F32) * dintra_ref[hh]
            o = (jnp.dot(sc.astype(BF16), vc, preferred_element_type=F32)
                 + dq_ref[hh] * jnp.dot(qc, s_old.astype(BF16), preferred_element_type=F32))
            kd = (kf * dk_ref[hh]).astype(BF16)
            s_new = dc_ref[hh] * s_old + lax.dot_general(
                kd, vc, (((0,), (0,)), ((), ())), preferred_element_type=F32)
            st_ref[b, hh] = s_new
            o_ref[pl.ds(r0, chunk), cs] = o

    if bb * n_chunks <= 4:
        for b in range(bb):
            for c in range(n_chunks):
                ret_block(b, c)
    else:
        def body(i, carry):
            ret_block(i // n_chunks, i % n_chunks)
            return carry
        lax.fori_loop(0, bb * n_chunks, body, 0)

    for hh in range(RET_HEADS):
        cs = slice(hh * dh, (hh + 1) * dh)
        oh = o_ref[:, cs]
        mu = jnp.mean(oh, axis=-1, keepdims=True)
        oc = oh - mu
        var = jnp.mean(oc * oc, axis=-1, keepdims=True)
        y = oc * lax.rsqrt(var + EPS) * gret_ref[:, cs]
        g = gate[:, cs]
        a_ref[:, cs] = (g * _sigmoid(g) * y).astype(BF16)

    ue_ref[:, HIST_ROWS:HIST_ROWS + tl, :] = u.reshape(bb, tl, pw)
    pos = pos0 + l_idx * tl + lax.broadcasted_iota(I32, (1, tl, 1), 1)
    for gi, w in enumerate(POOL_WINDOWS):
        cs = slice(gi * gw, (gi + 1) * gw)
        acc = ue_ref[:, HIST_ROWS:HIST_ROWS + tl, cs]
        for j in range(1, w):
            acc = acc + ue_ref[:, HIST_ROWS - j:HIST_ROWS - j + tl, cs]
        inv_cnt = 1.0 / jnp.minimum(pos + 1, w).astype(F32)
        p = (acc * inv_cnt).reshape(rows, gw) - u[:, cs]
        z = jnp.dot(p.astype(BF16), wpool_ref[gi], preferred_element_type=F32) * pscale_ref[:, cs]
        a_ref[:, rw_width + gi * gw:rw_width + (gi + 1) * gw] = z.astype(BF16)
    tail = ue_ref[:, tl:tl + HIST_ROWS, :]
    ue_ref[:, 0:HIST_ROWS, :] = tail
    hist_ref[...] = tail

    x1 = x + jnp.dot(a_ref[...], wout_ref[...], preferred_element_type=F32)
    x1_ref[...] = x1.reshape(bb, tl, d_model)
    h2 = _rms(x1, g2_ref[...])
    h2_ref[...] = _pack_bf16_pair(h2[:, 0:d_model // 2], h2[:, d_model // 2:]).reshape(bb, tl, d_model // 2)

    h2_hi = h2.astype(BF16)
    h2_lo = (h2 - h2_hi.astype(F32)).astype(BF16)
    two = jnp.dot(h2_hi, wr_ref[...], preferred_element_type=F32)
    logits = (two[:, 0:LANES] + two[:, LANES:2 * LANES]
              + jnp.dot(h2_lo, wr_ref[:, 0:LANES], preferred_element_type=F32))
    lt = logits.T
    neg = jnp.float32(-jnp.inf)
    big = jnp.float32(1e9)
    sub = lax.broadcasted_iota(I32, (EXPERTS_PER_GROUP, rows), 0).astype(F32)
    gl = jnp.where(sub < N_EXPERT_GROUPS, lt[N_EXPERTS:N_EXPERTS + EXPERTS_PER_GROUP], neg)
    gmax = jnp.max(gl, axis=0, keepdims=True)
    gidx = jnp.min(jnp.where(gl == gmax, sub, big), axis=0, keepdims=True)
    p_sel = 1.0 / jnp.sum(jnp.exp(gl - gmax), axis=0, keepdims=True)
    el = lt[0:EXPERTS_PER_GROUP]
    for g in range(1, N_EXPERT_GROUPS):
        el = jnp.where(gidx == g, lt[g * EXPERTS_PER_GROUP:(g + 1) * EXPERTS_PER_GROUP], el)
    m1 = jnp.max(el, axis=0, keepdims=True)
    t1 = jnp.min(jnp.where(el == m1, sub, big), axis=0, keepdims=True)
    el2 = jnp.where(sub == t1, neg, el)
    m2 = jnp.max(el2, axis=0, keepdims=True)
    t2 = jnp.min(jnp.where(el2 == m2, sub, big), axis=0, keepdims=True)
    e2 = jnp.exp(m2 - m1)
    w1 = p_sel / (1.0 + e2)
    w2 = p_sel * e2 / (1.0 + e2)
    i1 = gidx * EXPERTS_PER_GROUP + t1
    i2 = gidx * EXPERTS_PER_GROUP + t2

    eid = lax.broadcasted_iota(I32, (N_EXPERTS, rows), 0).astype(F32)
    hit1 = eid == i1
    hit2 = eid == i2
    onehot = (hit1 | hit2).astype(BF16)
    before = jnp.dot(onehot, tri_ref[...], preferred_element_type=F32) + cnt_ref[...]
    r1 = jnp.sum(jnp.where(hit1, before, 0.0), axis=0, keepdims=True)
    r2 = jnp.sum(jnp.where(hit2, before, 0.0), axis=0, keepdims=True)
    cnt_ref[...] = cnt_ref[...] + jnp.sum(onehot.astype(F32), axis=1, keepdims=True)

    ri = jnp.where(sub == 0, i1, jnp.where(sub == 1, i2, jnp.where(sub == 2, r1, jnp.where(sub == 3, r2, 0.0))))
    ri_ref[...] = ri.astype(I32).reshape(ri_ref.shape)
    rw_ref[...] = jnp.where(sub == 0, w1, jnp.where(sub == 1, w2, 0.0)).reshape(rw_ref.shape)


def _rope_tables(pos0, seq, tl, dh):
    half = dh // 2
    inv = ROPE_BASE ** (-jnp.arange(half, dtype=F32) / half)
    ang_t = jnp.arange(tl, dtype=F32)[:, None] * inv[None, :]
    ang_b = (pos0 + tl * jnp.arange(seq // tl)).astype(F32)[:, None] * inv[None, :]
    dup = lambda a: jnp.concatenate([a, a], axis=-1)
    sgn = lambda a: jnp.concatenate([-a, a], axis=-1)
    base = jnp.stack([dup(jnp.cos(ang_b)), dup(jnp.sin(ang_b))], axis=1)
    base = jnp.pad(base, ((0, 0), (0, 8 - base.shape[1]), (0, 0)))
    cos_t, sin_t = jnp.cos(ang_t), jnp.sin(ang_t)
    return base, dup(cos_t), dup(sin_t), sgn(cos_t), sgn(sin_t)


def _layer_call(x, b0, nb, s0, h0, pos0, consts, *, bb, tl, chunk):
    _, seq, d_model = x.shape
    bsz = nb
    blk0 = b0 // bb
    rows = bb * tl
    rw_width = consts["gret"].shape[-1]
    pw = consts["pscale"].shape[-1]
    dh = rw_width // RET_HEADS

    rope = _rope_tables(pos0, seq, tl, dh)

    lg = jnp.log1p(-jnp.exp2(-5.0 - jnp.arange(RET_HEADS, dtype=F32)))
    idx = jnp.arange(chunk, dtype=F32)
    diff = idx[:, None] - idx[None, :]
    d_intra = jnp.where(diff[None] >= 0, jnp.exp(jnp.maximum(diff, 0.0)[None] * lg[:, None, None]), 0.0)
    d_q = jnp.broadcast_to(jnp.exp((idx + 1.0)[None, :] * lg[:, None])[:, :, None], (RET_HEADS, chunk, dh))
    d_k = jnp.broadcast_to(jnp.exp((chunk - 1.0 - idx)[None, :] * lg[:, None])[:, :, None], (RET_HEADS, chunk, dh))
    d_c = jnp.exp(chunk * lg)
    tri = jnp.triu(jnp.ones((rows, rows), BF16), 1)

    const2 = lambda b, l, *_: (0, 0)
    const3 = lambda b, l, *_: (0, 0, 0)
    grid_spec = pltpu.PrefetchScalarGridSpec(
        num_scalar_prefetch=0,
        grid=(bsz // bb, seq // tl),
        in_specs=[
            pl.BlockSpec(memory_space=pltpu.SMEM),
            pl.BlockSpec((bb, tl, d_model), lambda b, l: (blk0 + b, l, 0)),
            pl.BlockSpec((bb, RET_HEADS, dh, dh), lambda b, l: (b, 0, 0, 0)),
            pl.BlockSpec((bb, HIST_ROWS, pw), lambda b, l: (b, 0, 0)),
            pl.BlockSpec((1, 8, dh), lambda b, l: (l, 0, 0)),
            pl.BlockSpec((tl, dh), const2),
            pl.BlockSpec((tl, dh), const2),
            pl.BlockSpec((tl, dh), const2),
            pl.BlockSpec((tl, dh), const2),
            pl.BlockSpec((RET_HEADS, chunk, chunk), const3),
            pl.BlockSpec((RET_HEADS, chunk, dh), const3),
            pl.BlockSpec((RET_HEADS, chunk, dh), const3),
            pl.BlockSpec((1, d_model), const2),
            pl.BlockSpec(consts["w_in"].shape, const2),
            pl.BlockSpec((1, rw_width), const2),
            pl.BlockSpec(consts["w_pool"].shape, const3),
            pl.BlockSpec((1, pw), const2),
            pl.BlockSpec(consts["w_out"].shape, const2),
            pl.BlockSpec((1, d_model), const2),
            pl.BlockSpec((d_model, 2 * LANES), const2),
            pl.BlockSpec((rows, rows), const2),
        ],
        out_specs=[
            pl.BlockSpec((bb, tl, d_model), lambda b, l: (b, l, 0)),
            pl.BlockSpec((bb, tl, d_model // 2), lambda b, l: (b, l, 0)),
            pl.BlockSpec((1, 1, EXPERTS_PER_GROUP, rows), lambda b, l: (b, l, 0, 0)),
            pl.BlockSpec((1, 1, EXPERTS_PER_GROUP, rows), lambda b, l: (b, l, 0, 0)),
            pl.BlockSpec((bb, RET_HEADS, dh, dh), lambda b, l: (b, 0, 0, 0)),
            pl.BlockSpec((bb, HIST_ROWS, pw), lambda b, l: (b, 0, 0)),
            pl.BlockSpec((N_EXPERTS, rows), const2),
        ],
        scratch_shapes=[
            pltpu.VMEM((bb, HIST_ROWS + tl, pw), F32),
            pltpu.VMEM((rows, rw_width), BF16),
            pltpu.VMEM((rows, rw_width), F32),
            pltpu.VMEM((rows, rw_width), BF16),
            pltpu.VMEM((rows, rw_width), F32),
            pltpu.VMEM((rows, d_model), BF16),
        ],
    )
    out_shape = [
        jax.ShapeDtypeStruct((bsz, seq, d_model), F32),
        jax.ShapeDtypeStruct((bsz, seq, d_model // 2), U32),
        jax.ShapeDtypeStruct((bsz // bb, seq // tl, EXPERTS_PER_GROUP, rows), I32),
        jax.ShapeDtypeStruct((bsz // bb, seq // tl, EXPERTS_PER_GROUP, rows), F32),
        jax.ShapeDtypeStruct((bsz, RET_HEADS, dh, dh), F32),
        jax.ShapeDtypeStruct((bsz, HIST_ROWS, pw), F32),
        jax.ShapeDtypeStruct((N_EXPERTS, rows), F32),
    ]
    kern = functools.partial(_layer_kernel, bb=bb, tl=tl, chunk=chunk, pos0=pos0)
    return pl.pallas_call(
        kern, grid_spec=grid_spec, out_shape=out_shape, name=f"layer_pos{pos0}_b{b0}",
        compiler_params=pltpu.CompilerParams(
            dimension_semantics=("arbitrary", "arbitrary"), vmem_limit_bytes=VMEM_LIMIT),
    )(d_c, x, s0, h0, *rope, d_intra, d_q, d_k,
      consts["g1"], consts["w_in"], consts["gret"], consts["w_pool"], consts["pscale"],
      consts["w_out"], consts["g2"], consts["wr"], tri)


def _sc_partition(n_units):
    info = plsc.get_sparse_core_info()
    nc, nw = info.num_cores, info.num_cores * info.num_subcores
    upw = -(-n_units // nw)
    upw += upw % 2
    return nc, nw, upw


def _units_by_worker(idx, n_units, upw, nw):
    idx = jnp.pad(idx.reshape(n_units, SC_UNIT), ((0, nw * upw - n_units), (0, 0)))
    return idx.reshape(upw, nw, SC_UNIT).transpose(1, 0, 2)


def _sc_dispatch(srcs, idx0, idx1, n_out_rows):
    assert 1 <= len(srcs) <= 2
    d = srcs[0].shape[1]
    dtype = srcs[0].dtype
    assert all(src.shape[0] % SC_UNIT == 0 for src in srcs)
    units_a = srcs[0].shape[0] // SC_UNIT
    n_units = sum(src.shape[0] for src in srcs) // SC_UNIT
    nc, nw, upw = _sc_partition(n_units)
    idx0 = _units_by_worker(idx0, n_units, upw, nw)
    idx1 = _units_by_worker(idx1, n_units, upw, nw)
    mesh = plsc.VectorSubcoreMesh(core_axis_name="c", subcore_axis_name="s")
    dma = pltpu.SemaphoreType.DMA

    @functools.partial(
        pl.kernel, mesh=mesh,
        out_type=jax.ShapeDtypeStruct((n_out_rows, d), dtype),
        scratch_types=[
            pltpu.VMEM((upw, SC_UNIT), I32),
            pltpu.VMEM((upw, SC_UNIT), I32),
            pltpu.VMEM((SC_UNIT, d), dtype),
            pltpu.VMEM((SC_UNIT, d), dtype),
            dma, dma, dma, dma, dma, dma,
        ],
    )
    def k(*refs):
        src_hbm = refs[:len(srcs)]
        i0_hbm, i1_hbm, out_hbm, i0_v, i1_v, rows0, rows1, l0, l1, p0, p1, q0, q1 = refs[len(srcs):]
        wid = lax.axis_index("s") * nc + lax.axis_index("c")
        pltpu.sync_copy(i0_hbm.at[wid], i0_v)
        pltpu.sync_copy(i1_hbm.at[wid], i1_v)
        rows, lsem, psem, qsem = (rows0, rows1), (l0, l1), (p0, p1), (q0, q1)

        def live(j):
            return j * nw + wid < n_units

        def load(j, b, op):
            unit = j * nw + wid

            @pl.when(live(j) & (unit < units_a))
            def _():
                op(pltpu.make_async_copy(
                    src_hbm[0].at[pl.ds(pl.multiple_of(unit * SC_UNIT, 8), SC_UNIT)], rows[b], lsem[b]))

            if len(srcs) == 2:
                @pl.when(live(j) & (unit >= units_a))
                def _():
                    op(pltpu.make_async_copy(
                        src_hbm[1].at[pl.ds(pl.multiple_of((unit - units_a) * SC_UNIT, 8), SC_UNIT)],
                        rows[b], lsem[b]))

        def scatter(j, b, op):
            @pl.when(live(j))
            def _():
                op(pltpu.make_async_copy(rows[b], out_hbm.at[i0_v.at[j]], psem[b]))
                op(pltpu.make_async_copy(rows[b], out_hbm.at[i1_v.at[j]], qsem[b]))

        start = lambda c: c.start()
        wait = lambda c: c.wait()
        load(0, 0, start)

        @pl.loop(0, upw, step=2)
        def _(j):
            @pl.when(j > 0)
            def _():
                scatter(j - 1, 1, wait)
            load(j + 1, 1, start)
            load(j, 0, wait)
            scatter(j, 0, start)
            scatter(j, 0, wait)

            @pl.when(j + 2 < upw)
            def _():
                load(j + 2, 0, start)
            load(j + 1, 1, wait)
            scatter(j + 1, 1, start)

        scatter(upw - 1, 1, wait)

    return k(*srcs, idx0, idx1)


def _sc_gather(table, idx):
    n = idx.shape[0]
    d = table.shape[1]
    assert n % SC_UNIT == 0
    n_units = n // SC_UNIT
    nc, nw, upw = _sc_partition(n_units)
    idx = _units_by_worker(idx, n_units, upw, nw)
    mesh = plsc.VectorSubcoreMesh(core_axis_name="c", subcore_axis_name="s")
    dma = pltpu.SemaphoreType.DMA

    @functools.partial(
        pl.kernel, mesh=mesh,
        out_type=jax.ShapeDtypeStruct((n, d), table.dtype),
        scratch_types=[
            pltpu.VMEM((upw, SC_UNIT), I32),
            pltpu.VMEM((SC_UNIT, d), table.dtype),
            pltpu.VMEM((SC_UNIT, d), table.dtype),
            dma, dma, dma, dma,
        ],
    )
    def k(t_hbm, i_hbm, out_hbm, i_v, rows0, rows1, g0, g1, w0, w1):
        wid = lax.axis_index("s") * nc + lax.axis_index("c")
        pltpu.sync_copy(i_hbm.at[wid], i_v)
        rows, gsem, wsem = (rows0, rows1), (g0, g1), (w0, w1)

        def live(j):
            return j * nw + wid < n_units

        def gather(j, b, op):
            @pl.when(live(j))
            def _():
                op(pltpu.make_async_copy(t_hbm.at[i_v.at[j]], rows[b], gsem[b]))

        def write(j, b, op):
            @pl.when(live(j))
            def _():
                op(pltpu.make_async_copy(
                    rows[b], out_hbm.at[pl.ds(pl.multiple_of((j * nw + wid) * SC_UNIT, 8), SC_UNIT)], wsem[b]))

        start = lambda c: c.start()
        wait = lambda c: c.wait()
        gather(0, 0, start)

        @pl.loop(0, upw, step=2)
        def _(j):
            @pl.when(j > 0)
            def _():
                write(j - 1, 1, wait)
            gather(j + 1, 1, start)
            gather(j, 0, wait)
            write(j, 0, start)
            write(j, 0, wait)

            @pl.when(j + 2 < upw)
            def _():
                gather(j + 2, 0, start)
            gather(j + 1, 1, wait)
            write(j + 1, 1, start)

        write(upw - 1, 1, wait)

    return k(table, idx)


def _moe_kernel(kfirst_ref, ktot_ref, tbuf_ref, trow_ref, tvalid_ref, xs0_hbm, xs1_hbm, wg_ref, wu_ref, wd_ref,
                ys0_hbm, ys1_hbm, wgu_s, wd_s, xbuf, ybuf, sem_in, sem_out):
    e = pl.program_id(0)
    k_first = kfirst_ref[e]
    n_tiles = kfirst_ref[e + 1] - k_first
    k_total = ktot_ref[0]
    hidden = wd_s.shape[0]
    half = xbuf.shape[-1]

    def rows_of(k):
        return pl.ds(pl.multiple_of(trow_ref[k] * MOE_TILE, MOE_TILE), MOE_TILE)

    def start_in(k):
        slot = k % MOE_BUFFERS
        for buf, xs_hbm in enumerate((xs0_hbm, xs1_hbm)):
            @pl.when(tbuf_ref[k] == buf)
            def _():
                pltpu.make_async_copy(xs_hbm.at[rows_of(k)], xbuf.at[slot], sem_in.at[slot]).start()

    def wait_in(k):
        slot = k % MOE_BUFFERS
        pltpu.make_async_copy(xs0_hbm.at[pl.ds(0, MOE_TILE)], xbuf.at[slot], sem_in.at[slot]).wait()

    def start_out(k):
        slot = k % MOE_BUFFERS
        for buf, ys_hbm in enumerate((ys0_hbm, ys1_hbm)):
            @pl.when(tbuf_ref[k] == buf)
            def _():
                pltpu.make_async_copy(ybuf.at[slot], ys_hbm.at[rows_of(k)], sem_out.at[slot]).start()

    def wait_out(k):
        slot = k % MOE_BUFFERS
        pltpu.make_async_copy(ybuf.at[slot], ys0_hbm.at[pl.ds(0, MOE_TILE)], sem_out.at[slot]).wait()

    @pl.when(e == 0)
    def _():
        for k in range(MOE_LOOKAHEAD):
            @pl.when(k < k_total)
            def _():
                start_in(k)

    @pl.when(n_tiles > 0)
    def _():
        wgu_s[:, 0:hidden] = wg_ref[0].astype(BF16)
        wgu_s[:, hidden:2 * hidden] = wu_ref[0].astype(BF16)
        wd_s[...] = wd_ref[0].astype(BF16)

    def tile(t, carry):
        k = k_first + t

        @pl.when(k + MOE_LOOKAHEAD < k_total)
        def _():
            start_in(k + MOE_LOOKAHEAD)

        wait_in(k)

        @pl.when(k >= MOE_BUFFERS)
        def _():
            wait_out(k - MOE_BUFFERS)

        slot = k % MOE_BUFFERS
        row = lax.broadcasted_iota(I32, (MOE_TILE, half), 0)
        x_lo, x_hi = _unpack_bf16_pair(jnp.where(row < tvalid_ref[k], xbuf[slot], jnp.uint32(0)))
        ab = (jnp.dot(x_lo.astype(BF16), wgu_s[0:half, :], preferred_element_type=F32)
              + jnp.dot(x_hi.astype(BF16), wgu_s[half:2 * half, :], preferred_element_type=F32))
        a = ab[:, 0:hidden]
        he = a * _sigmoid(a) * ab[:, hidden:2 * hidden]
        y = jnp.dot(he.astype(BF16), wd_s[...], preferred_element_type=F32)
        ybuf[slot] = _pack_bf16_pair(y[:, 0:half], y[:, half:2 * half])
        start_out(k)
        return carry

    lax.fori_loop(0, n_tiles, tile, 0)

    @pl.when(e == pl.num_programs(0) - 1)
    def _():
        for j in range(1, MOE_BUFFERS + 1):
            @pl.when(k_total >= j)
            def _():
                wait_out(k_total - j)


def _moe_call(groups, w_g, w_u, w_d):
    (xs0, starts0, cnt0), (xs1, starts1, cnt1) = groups
    half = xs0.shape[1]
    n_experts, d_model, hidden = w_g.shape
    tiles_of = lambda cnt: (cnt + MOE_TILE - 1) // MOE_TILE
    n0, n1 = tiles_of(cnt0), tiles_of(cnt1)
    k_end = jnp.cumsum(n0 + n1)
    k_first = jnp.concatenate([jnp.zeros((1,), I32), k_end]).astype(I32)
    k_max = (xs0.shape[0] + xs1.shape[0]) // MOE_TILE
    k = jnp.arange(k_max, dtype=I32)
    e_of_k = jnp.minimum(jnp.sum(k[:, None] >= k_end[None, :], axis=1), n_experts - 1)
    onehot = e_of_k[:, None] == jnp.arange(n_experts, dtype=I32)[None, :]
    pick = lambda v: jnp.sum(jnp.where(onehot, v[None, :], 0), axis=1)
    j0 = k - pick(k_first[:-1])
    j1 = j0 - pick(n0)
    in0 = j1 < 0
    t_buf = jnp.where(in0, 0, 1).astype(I32)
    t_row = jnp.where(in0, jnp.minimum(pick(starts0 // MOE_TILE) + j0, xs0.shape[0] // MOE_TILE - 1),
                      jnp.clip(pick(starts1 // MOE_TILE) + j1, 0, xs1.shape[0] // MOE_TILE - 1)).astype(I32)
    t_valid = jnp.where(in0, pick(cnt0) - j0 * MOE_TILE, pick(cnt1) - j1 * MOE_TILE).astype(I32)

    wspec = lambda shape: pl.BlockSpec(shape, lambda e, *_: (e, 0, 0))
    grid_spec = pltpu.PrefetchScalarGridSpec(
        num_scalar_prefetch=5,
        grid=(n_experts,),
        in_specs=[
            pl.BlockSpec(memory_space=pl.ANY),
            pl.BlockSpec(memory_space=pl.ANY),
            wspec((1, d_model, hidden)),
            wspec((1, d_model, hidden)),
            wspec((1, hidden, d_model)),
        ],
        out_specs=[pl.BlockSpec(memory_space=pl.ANY), pl.BlockSpec(memory_space=pl.ANY)],
        scratch_shapes=[
            pltpu.VMEM((d_model, 2 * hidden), BF16),
            pltpu.VMEM((hidden, d_model), BF16),
            pltpu.VMEM((MOE_BUFFERS, MOE_TILE, half), U32),
            pltpu.VMEM((MOE_BUFFERS, MOE_TILE, half), U32),
            pltpu.SemaphoreType.DMA((MOE_BUFFERS,)),
            pltpu.SemaphoreType.DMA((MOE_BUFFERS,)),
        ],
    )
    return pl.pallas_call(
        _moe_kernel, grid_spec=grid_spec,
        out_shape=[jax.ShapeDtypeStruct(xs0.shape, U32), jax.ShapeDtypeStruct(xs1.shape, U32)], name="moe_experts",
        compiler_params=pltpu.CompilerParams(
            dimension_semantics=("arbitrary",), vmem_limit_bytes=VMEM_LIMIT),
    )(k_first, k_end[-1:].astype(I32), t_buf, t_row, t_valid, xs0, xs1, w_g, w_u, w_d)


def _combine_kernel(x1_ref, y0_ref, y1_ref, rw_ref, gf_ref, *rest):
    out_ref = rest[-1]
    tr = x1_ref.shape[0]
    w_rows = jnp.concatenate([rw_ref[0], jnp.zeros((LANES - rw_ref.shape[1], tr), F32)], axis=0)
    w_cols = w_rows.T
    w0, w1 = w_cols[:, 0:1], w_cols[:, 1:2]
    a_lo, a_hi = _unpack_bf16_pair(y0_ref[0])
    b_lo, b_hi = _unpack_bf16_pair(y1_ref[0])
    moe = jnp.concatenate([w0 * a_lo + w1 * b_lo, w0 * a_hi + w1 * b_hi], axis=-1)
    out_ref[...] = _rms(x1_ref[...] + moe, gf_ref[...])


def _combine_call(x1, rw, row0, n, yg, gf, out_rows, out_row0, prev_out=None):
    t, d_model = x1.shape
    tr = rw.shape[-1]
    half = yg.shape[-1]
    assert t % tr == 0 and row0 % tr == 0 and n % tr == 0 and rw.shape == (t // tr, EXPERTS_PER_GROUP, tr)
    assert yg.shape == (2, n, half) and out_row0 % tr == 0
    off = row0 // tr
    ooff = out_row0 // tr
    in_specs = [
        pl.BlockSpec((tr, d_model), lambda i: (off + i, 0)),
        pl.BlockSpec((1, tr, half), lambda i: (0, i, 0)),
        pl.BlockSpec((1, tr, half), lambda i: (1, i, 0)),
        pl.BlockSpec((1, EXPERTS_PER_GROUP, tr), lambda i: (off + i, 0, 0)),
        pl.BlockSpec((1, d_model), lambda i: (0, 0)),
    ]
    args = [x1, yg, yg, rw, gf]
    aliases = {}
    if prev_out is not None:
        in_specs.append(pl.BlockSpec(memory_space=pl.ANY))
        args.append(prev_out)
        aliases = {len(args) - 1: 0}
    return pl.pallas_call(
        _combine_kernel,
        grid=(n // tr,),
        in_specs=in_specs,
        out_specs=pl.BlockSpec((tr, d_model), lambda i: (ooff + i, 0)),
        out_shape=jax.ShapeDtypeStruct((out_rows, d_model), F32), name=f"combine_row{out_row0}_of{out_rows}",
        input_output_aliases=aliases,
        compiler_params=pltpu.CompilerParams(
            dimension_semantics=("arbitrary",), vmem_limit_bytes=VMEM_LIMIT),
    )(*args)


def _route(streams):
    tokens = [h2.shape[0] for h2, _, _ in streams]
    counts = [cnt[:, 0].astype(I32) for _, _, cnt in streams]
    total = sum(counts)
    padded = ((total + MOE_TILE - 1) // MOE_TILE) * MOE_TILE
    starts = (jnp.cumsum(padded) - padded).astype(I32)
    experts = jnp.arange(N_EXPERTS, dtype=I32)[None, :, None]
    pos, base = [], starts
    for (_, ri, _), t, cnt in zip(streams, tokens, counts):
        ri = jnp.moveaxis(ri, 2, 0).reshape(ri.shape[2], t)
        first_row = jnp.sum(jnp.where(ri[0:2, None, :] == experts, base[None, :, None], 0), axis=1)
        pos.append(ri[2:4] + first_row)
        base = base + cnt
    pos = jnp.concatenate(pos, axis=1)
    n_rows = ((2 * sum(tokens) + N_EXPERTS * (MOE_TILE - 1)) // MOE_TILE) * MOE_TILE
    xs_sorted = _sc_dispatch([h2 for h2, _, _ in streams], pos[0], pos[1], n_rows)
    return (xs_sorted, starts, total), pos


def _gather_tokens(ys_sorted, pos, t0, n):
    return _sc_gather(ys_sorted, pos[:, t0:t0 + n].reshape(2 * n)).reshape(2, n, ys_sorted.shape[-1])


def _one_layer(xp, xs, s_ret, c_pool, norm1_g, w_in, ret_norm_g, w_pool, pool_scale, w_out, norm2_g,
               w_rg, w_re, w_g, w_u, w_d, final_g, past_len):
    bp, seq, d_model = xp.shape
    bs, dseq, _ = xs.shape
    rw_width = ret_norm_g.shape[-1]
    pw = pool_scale.shape[-1]
    dh = rw_width // RET_HEADS
    half = d_model // 2

    w_r = jnp.concatenate(
        [w_re, w_rg, jnp.zeros((d_model, LANES - N_EXPERTS - N_EXPERT_GROUPS), F32)], axis=1)
    wr_hi = w_r.astype(BF16)
    wr = jnp.concatenate([wr_hi, (w_r - wr_hi.astype(F32)).astype(BF16)], axis=1)
    consts = dict(
        g1=norm1_g.reshape(1, d_model), w_in=w_in.astype(BF16), gret=ret_norm_g.reshape(1, rw_width),
        w_pool=w_pool.astype(BF16), pscale=pool_scale.reshape(1, pw), w_out=w_out.astype(BF16),
        g2=norm2_g.reshape(1, d_model), wr=wr)

    gf = final_g.reshape(1, d_model)

    ts = bs * dseq
    b_lead = bp - 1
    t_lead, t_rest = b_lead * seq, (bp - b_lead) * seq
    zeros = lambda nb: (jnp.zeros((nb, RET_HEADS, dh, dh), F32), jnp.zeros((nb, HIST_ROWS, pw), F32))
    h0s = jnp.pad(c_pool, ((0, 0), (HIST_ROWS - POOL_HIST, 0), (0, 0)))
    prompt_tile = dict(bb=1, tl=PROMPT_TILE, chunk=256)

    def stream(layer_out, t):
        x1, h2, ri, rw, st, hist, cnt = layer_out
        return dict(x1=x1.reshape(t, d_model), route=(h2.reshape(t, half), ri, cnt),
                    rw=rw.reshape(-1, EXPERTS_PER_GROUP, rw.shape[-1]), st=st, hist=hist)

    pa = stream(_layer_call(xp, 0, b_lead, *zeros(b_lead), 0, consts, **prompt_tile), t_lead)
    group0, pos0 = _route([pa["route"]])
    pb = stream(_layer_call(xp, b_lead, bp - b_lead, *zeros(bp - b_lead), 0, consts, **prompt_tile), t_rest)
    sm = stream(_layer_call(xs, 0, bs, s_ret, h0s, past_len, consts, bb=bs, tl=dseq, chunk=min(64, dseq)), ts)
    group1, pos1 = _route([pb["route"], sm["route"]])
    ys0, ys1 = _moe_call([group0, group1], w_g, w_u, w_d)

    tp = bp * seq
    yp = _combine_call(pb["x1"], pb["rw"], 0, t_rest, _gather_tokens(ys1, pos1, 0, t_rest), gf, tp, t_lead)
    ysm = _combine_call(sm["x1"], sm["rw"], 0, ts, _gather_tokens(ys1, pos1, t_rest, ts), gf, ts, 0)
    row0 = 0
    for nb in COMBINE_CHUNKS:
        n = min(nb * seq, t_lead - row0)
        if n > 0:
            yp = _combine_call(pa["x1"], pa["rw"], row0, n, _gather_tokens(ys0, pos0, row0, n), gf, tp, row0,
                               prev_out=yp)
            row0 += n
    assert row0 == t_lead
    st_p = jnp.concatenate([pa["st"], pb["st"]], axis=0)
    hist_p = jnp.concatenate([pa["hist"], pb["hist"]], axis=0)
    return (yp.reshape(bp, seq, d_model), ysm.reshape(bs, dseq, d_model),
            st_p, hist_p[:, HIST_ROWS - POOL_HIST:], sm["st"], sm["hist"][:, HIST_ROWS - POOL_HIST:])


def kernel(x_prompt, x_sample, state_ret, cache_pool, norm1_g, w_in, ret_norm_g, w_pool, pool_scale, w_out,
           norm2_g, w_router_group, w_router_expert, w_exp_gate, w_exp_up, w_exp_down, final_norm_g):
    depth = w_in.shape[0]
    assert depth == 1, "the final RMSNorm is fused into the layer's combine kernel"
    assert x_prompt.shape[0] >= 2 and x_prompt.shape[1] % PROMPT_TILE == 0
    yp, ys, s_p, h_p, s_s, h_s = _one_layer(
        x_prompt, x_sample, state_ret[0], cache_pool[0], norm1_g[0], w_in[0], ret_norm_g[0], w_pool[0],
        pool_scale[0], w_out[0], norm2_g[0], w_router_group[0], w_router_expert[0],
        w_exp_gate[0], w_exp_up[0], w_exp_down[0], final_norm_g, PAST_LEN)
    return (yp, ys, s_p[None], h_p[None], s_s[None], h_s[None])
```

```python
import functools

import jax
import jax.numpy as jnp
from jax import lax
from jax.experimental import pallas as pl
from jax.experimental.pallas import tpu as pltpu
from jax.experimental.pallas import tpu_sc as plsc

F32 = jnp.float32
BF16 = jnp.bfloat16
I32 = jnp.int32
U32 = jnp.uint32

EPS = 1e-6
ROPE_BASE = 10000.0
RET_HEADS = 4
POOL_WINDOWS = (2, 4, 8, 16)
POOL_HIST = max(POOL_WINDOWS) - 1
N_EXPERT_GROUPS = 4
EXPERTS_PER_GROUP = 8
N_EXPERTS = N_EXPERT_GROUPS * EXPERTS_PER_GROUP
EXPERT_SHIFT = EXPERTS_PER_GROUP.bit_length() - 1
PAST_LEN = 1024

LANES = 128
HIST_ROWS = 16
MOE_TILE = 256
MOE_BUFFERS = 4
MOE_LOOKAHEAD = MOE_BUFFERS - 1
PROMPT_TILE = 1024
COMBINE_CHUNKS = (1, 2)
SC_UNIT = 32
VMEM_LIMIT = 56 * 1024 * 1024


def _rms(x, g):
    return x * lax.rsqrt(jnp.mean(x * x, axis=-1, keepdims=True) + EPS) * g


def _sigmoid(x):
    return 1.0 / (1.0 + jnp.exp(-x))


def _pack_bf16_pair(lo, hi):
    lo_b = lax.bitcast_convert_type(lo.astype(BF16).astype(F32), U32)
    hi_b = lax.bitcast_convert_type(hi.astype(BF16).astype(F32), U32)
    return hi_b | (lo_b >> 16)


def _unpack_bf16_pair(p):
    lo = lax.bitcast_convert_type(p << 16, F32)
    hi = lax.bitcast_convert_type(p & jnp.uint32(0xFFFF0000), F32)
    return lo, hi


def _layer_kernel(dc_ref, x_ref, s0_ref, h0_ref, rb_ref, rc_ref, rs_ref, rcs_ref, rss_ref,
                  dintra_ref, dq_ref, dk_ref,
                  g1_ref, win_ref, gret_ref, wpool_ref, pscale_ref, wout_ref, g2_ref,
                  wr_ref, tri_ref, after_ref,
                  x1_ref, h2_ref, ri_ref, rw_ref, st_ref, hist_ref, cnt_ref,
                  ue_ref, q_ref, k_ref, v_ref, o_ref, a_ref,
                  *, bb, tl, chunk, pos0):
    b_idx = pl.program_id(0)
    l_idx = pl.program_id(1)
    rows = bb * tl
    d_model = x_ref.shape[-1]
    rw_width = q_ref.shape[-1]
    dh = rw_width // RET_HEADS
    pw = ue_ref.shape[-1]
    gw = pw // len(POOL_WINDOWS)
    n_chunks = tl // chunk

    @pl.when(l_idx == 0)
    def _():
        st_ref[...] = s0_ref[...]
        ue_ref[:, 0:HIST_ROWS, :] = h0_ref[...]

    @pl.when((l_idx == 0) & (b_idx == 0))
    def _():
        cnt_ref[...] = jnp.zeros_like(cnt_ref)

    x = x_ref[...].reshape(rows, d_model)
    hb = _rms(x, g1_ref[...]).astype(BF16)

    def project(c0, c1):
        return jnp.dot(hb, win_ref[:, c0:c1], preferred_element_type=F32)

    proj = project(0, 2 * rw_width)

    cos_b = rb_ref[0, 0:1, :]
    sin_b = rb_ref[0, 1:2, :]
    cosf = (cos_b * rc_ref[...] - sin_b * rs_ref[...])[None]
    sinf = (sin_b * rcs_ref[...] + cos_b * rss_ref[...])[None]
    k_scale = dh ** -0.5
    for hh in range(RET_HEADS):
        qh = proj[:, hh * dh:(hh + 1) * dh]
        kh = proj[:, rw_width + hh * dh:rw_width + (hh + 1) * dh]
        qr = (qh.reshape(bb, tl, dh) * cosf
              + pltpu.roll(qh, dh // 2, 1).reshape(bb, tl, dh) * sinf).reshape(rows, dh)
        kr = (kh.reshape(bb, tl, dh) * cosf
              + pltpu.roll(kh, dh // 2, 1).reshape(bb, tl, dh) * sinf).reshape(rows, dh)
        q_ref[:, hh * dh:(hh + 1) * dh] = qr.astype(BF16)
        k_ref[:, hh * dh:(hh + 1) * dh] = kr * k_scale
    v_ref[...] = project(2 * rw_width, 3 * rw_width).astype(BF16)
    gate = project(3 * rw_width, 4 * rw_width)
    u = project(4 * rw_width, 4 * rw_width + pw)

    def ret_block(b, c):
        r0 = b * tl + c * chunk
        if not isinstance(r0, int):
            r0 = pl.multiple_of(r0, chunk)
        for hh in range(RET_HEADS):
            cs = slice(hh * dh, (hh + 1) * dh)
            qc = q_ref[pl.ds(r0, chunk), cs]
            kf = k_ref[pl.ds(r0, chunk), cs]
            vc = v_ref[pl.ds(r0, chunk), cs]
            s_old = st_ref[b, hh]
            sc = lax.dot_general(qc, kf.astype(BF16), (((1,), (1,)), ((), ())),
                                 preferred_element_type=F32) * dintra_ref[hh]
            o = (jnp.dot(sc.astype(BF16), vc, preferred_element_type=F32)
                 + dq_ref[hh] * jnp.dot(qc, s_old.astype(BF16), preferred_element_type=F32))
            kd = (kf * dk_ref[hh]).astype(BF16)
            s_new = dc_ref[hh] * s_old + lax.dot_general(
                kd, vc, (((0,), (0,)), ((), ())), preferred_element_type=F32)
            st_ref[b, hh] = s_new
            o_ref[pl.ds(r0, chunk), cs] = o

    if bb * n_chunks <= 4:
        for b in range(bb):
            for c in range(n_chunks):
                ret_block(b, c)
    else:
        def body(i, carry):
            ret_block(i // n_chunks, i % n_chunks)
            return carry
        lax.fori_loop(0, bb * n_chunks, body, 0)

    for hh in range(RET_HEADS):
        cs = slice(hh * dh, (hh + 1) * dh)
        oh = o_ref[:, cs]
        mu = jnp.mean(oh, axis=-1, keepdims=True)
        oc = oh - mu
        var = jnp.mean(oc * oc, axis=-1, keepdims=True)
        y = oc * lax.rsqrt(var + EPS) * gret_ref[:, cs]
        g = gate[:, cs]
        a_ref[:, cs] = (g * _sigmoid(g) * y).astype(BF16)

    ue_ref[:, HIST_ROWS:HIST_ROWS + tl, :] = u.reshape(bb, tl, pw)
    pos = pos0 + l_idx * tl + lax.broadcasted_iota(I32, (1, tl, 1), 1)
    for gi, w in enumerate(POOL_WINDOWS):
        cs = slice(gi * gw, (gi + 1) * gw)
        acc = ue_ref[:, HIST_ROWS:HIST_ROWS + tl, cs]
        for j in range(1, w):
            acc = acc + ue_ref[:, HIST_ROWS - j:HIST_ROWS - j + tl, cs]
        inv_cnt = 1.0 / jnp.minimum(pos + 1, w).astype(F32)
        p = (acc * inv_cnt).reshape(rows, gw) - u[:, cs]
        z = jnp.dot(p.astype(BF16), wpool_ref[gi], preferred_element_type=F32) * pscale_ref[:, cs]
        a_ref[:, rw_width + gi * gw:rw_width + (gi + 1) * gw] = z.astype(BF16)
    tail = ue_ref[:, tl:tl + HIST_ROWS, :]
    ue_ref[:, 0:HIST_ROWS, :] = tail
    hist_ref[...] = tail

    x1 = x + jnp.dot(a_ref[...], wout_ref[...], preferred_element_type=F32)
    x1_ref[...] = x1.reshape(bb, tl, d_model)
    h2 = _rms(x1, g2_ref[...])
    h2_ref[...] = _pack_bf16_pair(h2[:, 0:d_model // 2], h2[:, d_model // 2:]).reshape(bb, tl, d_model // 2)

    h2_hi = h2.astype(BF16)
    h2_lo = (h2 - h2_hi.astype(F32)).astype(BF16)
    two = jnp.dot(h2_hi, wr_ref[...], preferred_element_type=F32)
    logits = (two[:, 0:LANES] + two[:, LANES:2 * LANES]
              + jnp.dot(h2_lo, wr_ref[:, 0:LANES], preferred_element_type=F32))
    lt = logits.T
    neg = jnp.float32(-jnp.inf)
    big = jnp.float32(1e9)
    sub = lax.broadcasted_iota(I32, (EXPERTS_PER_GROUP, rows), 0).astype(F32)
    gl = jnp.where(sub < N_EXPERT_GROUPS, lt[N_EXPERTS:N_EXPERTS + EXPERTS_PER_GROUP], neg)
    gmax = jnp.max(gl, axis=0, keepdims=True)
    gidx = jnp.min(jnp.where(gl == gmax, sub, big), axis=0, keepdims=True)
    p_sel = 1.0 / jnp.sum(jnp.exp(gl - gmax), axis=0, keepdims=True)
    el = lt[0:EXPERTS_PER_GROUP]
    for g in range(1, N_EXPERT_GROUPS):
        el = jnp.where(gidx == g, lt[g * EXPERTS_PER_GROUP:(g + 1) * EXPERTS_PER_GROUP], el)
    m1 = jnp.max(el, axis=0, keepdims=True)
    t1 = jnp.min(jnp.where(el == m1, sub, big), axis=0, keepdims=True)
    el2 = jnp.where(sub == t1, neg, el)
    m2 = jnp.max(el2, axis=0, keepdims=True)
    t2 = jnp.min(jnp.where(el2 == m2, sub, big), axis=0, keepdims=True)
    e2 = jnp.exp(m2 - m1)
    w1 = p_sel / (1.0 + e2)
    w2 = p_sel * e2 / (1.0 + e2)
    i1 = gidx * EXPERTS_PER_GROUP + t1
    i2 = gidx * EXPERTS_PER_GROUP + t2

    eid = lax.broadcasted_iota(I32, (N_EXPERTS, rows), 0).astype(F32)
    hit1 = eid == i1
    hit2 = eid == i2
    onehot = (hit1 | hit2).astype(BF16)
    before = jnp.dot(onehot, tri_ref[...], preferred_element_type=F32) + cnt_ref[...]
    r1 = jnp.sum(jnp.where(hit1, before, 0.0), axis=0, keepdims=True)
    r2 = jnp.sum(jnp.where(hit2, before, 0.0), axis=0, keepdims=True)
    cnt_ref[...] = cnt_ref[...] + jnp.sum(onehot.astype(F32), axis=1, keepdims=True)

    ri = jnp.where(sub == 0, i1, jnp.where(sub == 1, i2, jnp.where(sub == 2, r1, jnp.where(sub == 3, r2, 0.0))))
    ri_ref[...] = ri.astype(I32).reshape(ri_ref.shape)
    rw_ref[...] = jnp.where(sub == 0, w1, jnp.where(sub == 1, w2, 0.0)).reshape(rw_ref.shape)


def _rope_tables(pos0, seq, tl, dh):
    half = dh // 2
    inv = ROPE_BASE ** (-jnp.arange(half, dtype=F32) / half)
    ang_t = jnp.arange(tl, dtype=F32)[:, None] * inv[None, :]
    ang_b = (pos0 + tl * jnp.arange(seq // tl)).astype(F32)[:, None] * inv[None, :]
    dup = lambda a: jnp.concatenate([a, a], axis=-1)
    sgn = lambda a: jnp.concatenate([-a, a], axis=-1)
    base = jnp.stack([dup(jnp.cos(ang_b)), dup(jnp.sin(ang_b))], axis=1)
    base = jnp.pad(base, ((0, 0), (0, 8 - base.shape[1]), (0, 0)))
    cos_t, sin_t = jnp.cos(ang_t), jnp.sin(ang_t)
    return base, dup(cos_t), dup(sin_t), sgn(cos_t), sgn(sin_t)


def _layer_call(x, b0, nb, s0, h0, pos0, consts, after, *, bb, tl, chunk):
    _, seq, d_model = x.shape
    bsz = nb
    blk0 = b0 // bb
    rows = bb * tl
    rw_width = consts["gret"].shape[-1]
    pw = consts["pscale"].shape[-1]
    dh = rw_width // RET_HEADS

    rope = _rope_tables(pos0, seq, tl, dh)

    lg = jnp.log1p(-jnp.exp2(-5.0 - jnp.arange(RET_HEADS, dtype=F32)))
    idx = jnp.arange(chunk, dtype=F32)
    diff = idx[:, None] - idx[None, :]
    d_intra = jnp.where(diff[None] >= 0, jnp.exp(jnp.maximum(diff, 0.0)[None] * lg[:, None, None]), 0.0)
    d_q = jnp.broadcast_to(jnp.exp((idx + 1.0)[None, :] * lg[:, None])[:, :, None], (RET_HEADS, chunk, dh))
    d_k = jnp.broadcast_to(jnp.exp((chunk - 1.0 - idx)[None, :] * lg[:, None])[:, :, None], (RET_HEADS, chunk, dh))
    d_c = jnp.exp(chunk * lg)
    tri = jnp.triu(jnp.ones((rows, rows), BF16), 1)

    const2 = lambda b, l, *_: (0, 0)
    const3 = lambda b, l, *_: (0, 0, 0)
    grid_spec = pltpu.PrefetchScalarGridSpec(
        num_scalar_prefetch=0,
        grid=(bsz // bb, seq // tl),
        in_specs=[
            pl.BlockSpec(memory_space=pltpu.SMEM),
            pl.BlockSpec((bb, tl, d_model), lambda b, l: (blk0 + b, l, 0)),
            pl.BlockSpec((bb, RET_HEADS, dh, dh), lambda b, l: (b, 0, 0, 0)),
            pl.BlockSpec((bb, HIST_ROWS, pw), lambda b, l: (b, 0, 0)),
            pl.BlockSpec((1, 8, dh), lambda b, l: (l, 0, 0)),
            pl.BlockSpec((tl, dh), const2),
            pl.BlockSpec((tl, dh), const2),
            pl.BlockSpec((tl, dh), const2),
            pl.BlockSpec((tl, dh), const2),
            pl.BlockSpec((RET_HEADS, chunk, chunk), const3),
            pl.BlockSpec((RET_HEADS, chunk, dh), const3),
            pl.BlockSpec((RET_HEADS, chunk, dh), const3),
            pl.BlockSpec((1, d_model), const2),
            pl.BlockSpec(consts["w_in"].shape, const2),
            pl.BlockSpec((1, rw_width), const2),
            pl.BlockSpec(consts["w_pool"].shape, const3),
            pl.BlockSpec((1, pw), const2),
            pl.BlockSpec(consts["w_out"].shape, const2),
            pl.BlockSpec((1, d_model), const2),
            pl.BlockSpec((d_model, 2 * LANES), const2),
            pl.BlockSpec((rows, rows), const2),
            pl.BlockSpec(memory_space=pl.ANY),
        ],
        out_specs=[
            pl.BlockSpec((bb, tl, d_model), lambda b, l: (b, l, 0)),
            pl.BlockSpec((bb, tl, d_model // 2), lambda b, l: (b, l, 0)),
            pl.BlockSpec((1, 1, EXPERTS_PER_GROUP, rows), lambda b, l: (b, l, 0, 0)),
            pl.BlockSpec((1, 1, EXPERTS_PER_GROUP, rows), lambda b, l: (b, l, 0, 0)),
            pl.BlockSpec((bb, RET_HEADS, dh, dh), lambda b, l: (b, 0, 0, 0)),
            pl.BlockSpec((bb, HIST_ROWS, pw), lambda b, l: (b, 0, 0)),
            pl.BlockSpec((N_EXPERTS, rows), const2),
        ],
        scratch_shapes=[
            pltpu.VMEM((bb, HIST_ROWS + tl, pw), F32),
            pltpu.VMEM((rows, rw_width), BF16),
            pltpu.VMEM((rows, rw_width), F32),
            pltpu.VMEM((rows, rw_width), BF16),
            pltpu.VMEM((rows, rw_width), F32),
            pltpu.VMEM((rows, d_model), BF16),
        ],
    )
    out_shape = [
        jax.ShapeDtypeStruct((bsz, seq, d_model), F32),
        jax.ShapeDtypeStruct((bsz, seq, d_model // 2), U32),
        jax.ShapeDtypeStruct((bsz // bb, seq // tl, EXPERTS_PER_GROUP, rows), I32),
        jax.ShapeDtypeStruct((bsz // bb, seq // tl, EXPERTS_PER_GROUP, rows), F32),
        jax.ShapeDtypeStruct((bsz, RET_HEADS, dh, dh), F32),
        jax.ShapeDtypeStruct((bsz, HIST_ROWS, pw), F32),
        jax.ShapeDtypeStruct((N_EXPERTS, rows), F32),
    ]
    kern = functools.partial(_layer_kernel, bb=bb, tl=tl, chunk=chunk, pos0=pos0)
    return pl.pallas_call(
        kern, grid_spec=grid_spec, out_shape=out_shape, name=f"layer_pos{pos0}_b{b0}",
        compiler_params=pltpu.CompilerParams(
            dimension_semantics=("arbitrary", "arbitrary"), vmem_limit_bytes=VMEM_LIMIT),
    )(d_c, x, s0, h0, *rope, d_intra, d_q, d_k,
      consts["g1"], consts["w_in"], consts["gret"], consts["w_pool"], consts["pscale"],
      consts["w_out"], consts["g2"], consts["wr"], tri, after)


def _sc_partition(n_units):
    info = plsc.get_sparse_core_info()
    nc, nw = info.num_cores, info.num_cores * info.num_subcores
    upw = -(-n_units // nw)
    upw += upw % 2
    return nc, nw, upw


def _units_by_worker(idx, n_units, upw, nw):
    idx = jnp.pad(idx.reshape(n_units, SC_UNIT), ((0, nw * upw - n_units), (0, 0)))
    return idx.reshape(upw, nw, SC_UNIT).transpose(1, 0, 2)


def _sc_dispatch(srcs, idx0, idx1, n_out_rows):
    assert 1 <= len(srcs) <= 2
    d = srcs[0].shape[1]
    dtype = srcs[0].dtype
    assert all(src.shape[0] % SC_UNIT == 0 for src in srcs)
    units_a = srcs[0].shape[0] // SC_UNIT
    n_units = sum(src.shape[0] for src in srcs) // SC_UNIT
    nc, nw, upw = _sc_partition(n_units)
    idx0 = _units_by_worker(idx0, n_units, upw, nw)
    idx1 = _units_by_worker(idx1, n_units, upw, nw)
    mesh = plsc.VectorSubcoreMesh(core_axis_name="c", subcore_axis_name="s")
    dma = pltpu.SemaphoreType.DMA

    @functools.partial(
        pl.kernel, mesh=mesh,
        out_type=jax.ShapeDtypeStruct((n_out_rows, d), dtype),
        scratch_types=[
            pltpu.VMEM((upw, SC_UNIT), I32),
            pltpu.VMEM((upw, SC_UNIT), I32),
            pltpu.VMEM((SC_UNIT, d), dtype),
            pltpu.VMEM((SC_UNIT, d), dtype),
            dma, dma, dma, dma, dma, dma,
        ],
    )
    def k(*refs):
        src_hbm = refs[:len(srcs)]
        i0_hbm, i1_hbm, out_hbm, i0_v, i1_v, rows0, rows1, l0, l1, p0, p1, q0, q1 = refs[len(srcs):]
        wid = lax.axis_index("s") * nc + lax.axis_index("c")
        pltpu.sync_copy(i0_hbm.at[wid], i0_v)
        pltpu.sync_copy(i1_hbm.at[wid], i1_v)
        rows, lsem, psem, qsem = (rows0, rows1), (l0, l1), (p0, p1), (q0, q1)

        def live(j):
            return j * nw + wid < n_units

        def load(j, b, op):
            unit = j * nw + wid

            @pl.when(live(j) & (unit < units_a))
            def _():
                op(pltpu.make_async_copy(
                    src_hbm[0].at[pl.ds(pl.multiple_of(unit * SC_UNIT, 8), SC_UNIT)], rows[b], lsem[b]))

            if len(srcs) == 2:
                @pl.when(live(j) & (unit >= units_a))
                def _():
                    op(pltpu.make_async_copy(
                        src_hbm[1].at[pl.ds(pl.multiple_of((unit - units_a) * SC_UNIT, 8), SC_UNIT)],
                        rows[b], lsem[b]))

        def scatter(j, b, op):
            @pl.when(live(j))
            def _():
                op(pltpu.make_async_copy(rows[b], out_hbm.at[i0_v.at[j]], psem[b]))
                op(pltpu.make_async_copy(rows[b], out_hbm.at[i1_v.at[j]], qsem[b]))

        start = lambda c: c.start()
        wait = lambda c: c.wait()
        load(0, 0, start)

        @pl.loop(0, upw, step=2)
        def _(j):
            @pl.when(j > 0)
            def _():
                scatter(j - 1, 1, wait)
            load(j + 1, 1, start)
            load(j, 0, wait)
            scatter(j, 0, start)
            scatter(j, 0, wait)

            @pl.when(j + 2 < upw)
            def _():
                load(j + 2, 0, start)
            load(j + 1, 1, wait)
            scatter(j + 1, 1, start)

        scatter(upw - 1, 1, wait)

    return k(*srcs, idx0, idx1)


def _sc_gather(table, idx):
    n = idx.shape[0]
    d = table.shape[1]
    assert n % SC_UNIT == 0
    n_units = n // SC_UNIT
    nc, nw, upw = _sc_partition(n_units)
    idx = _units_by_worker(idx, n_units, upw, nw)
    mesh = plsc.VectorSubcoreMesh(core_axis_name="c", subcore_axis_name="s")
    dma = pltpu.SemaphoreType.DMA

    @functools.partial(
        pl.kernel, mesh=mesh,
        out_type=jax.ShapeDtypeStruct((n, d), table.dtype),
        scratch_types=[
            pltpu.VMEM((upw, SC_UNIT), I32),
            pltpu.VMEM((SC_UNIT, d), table.dtype),
            pltpu.VMEM((SC_UNIT, d), table.dtype),
            dma, dma, dma, dma,
        ],
    )
    def k(t_hbm, i_hbm, out_hbm, i_v, rows0, rows1, g0, g1, w0, w1):
        wid = lax.axis_index("s") * nc + lax.axis_index("c")
        pltpu.sync_copy(i_hbm.at[wid], i_v)
        rows, gsem, wsem = (rows0, rows1), (g0, g1), (w0, w1)

        def live(j):
            return j * nw + wid < n_units

        def gather(j, b, op):
            @pl.when(live(j))
            def _():
                op(pltpu.make_async_copy(t_hbm.at[i_v.at[j]], rows[b], gsem[b]))

        def write(j, b, op):
            @pl.when(live(j))
            def _():
                op(pltpu.make_async_copy(
                    rows[b], out_hbm.at[pl.ds(pl.multiple_of((j * nw + wid) * SC_UNIT, 8), SC_UNIT)], wsem[b]))

        start = lambda c: c.start()
        wait = lambda c: c.wait()
        gather(0, 0, start)

        @pl.loop(0, upw, step=2)
        def _(j):
            @pl.when(j > 0)
            def _():
                write(j - 1, 1, wait)
            gather(j + 1, 1, start)
            gather(j, 0, wait)
            write(j, 0, start)
            write(j, 0, wait)

            @pl.when(j + 2 < upw)
            def _():
                gather(j + 2, 0, start)
            gather(j + 1, 1, wait)
            write(j + 1, 1, start)

        write(upw - 1, 1, wait)

    return k(table, idx)


def _moe_kernel(kfirst_ref, ktot_ref, tbuf_ref, trow_ref, tvalid_ref, xs0_hbm, xs1_hbm, wg_ref, wu_ref, wd_ref,
                ys0_hbm, ys1_hbm, wgu_s, wd_s, xbuf, ybuf, sem_in, sem_out):
    e = pl.program_id(0)
    k_first = kfirst_ref[e]
    n_tiles = kfirst_ref[e + 1] - k_first
    k_total = ktot_ref[0]
    hidden = wd_s.shape[0]
    half = xbuf.shape[-1]

    def rows_of(k):
        return pl.ds(pl.multiple_of(trow_ref[k] * MOE_TILE, MOE_TILE), MOE_TILE)

    def start_in(k):
        slot = k % MOE_BUFFERS
        for buf, xs_hbm in enumerate((xs0_hbm, xs1_hbm)):
            @pl.when(tbuf_ref[k] == buf)
            def _():
                pltpu.make_async_copy(xs_hbm.at[rows_of(k)], xbuf.at[slot], sem_in.at[slot]).start()

    def wait_in(k):
        slot = k % MOE_BUFFERS
        pltpu.make_async_copy(xs0_hbm.at[pl.ds(0, MOE_TILE)], xbuf.at[slot], sem_in.at[slot]).wait()

    def start_out(k):
        slot = k % MOE_BUFFERS
        for buf, ys_hbm in enumerate((ys0_hbm, ys1_hbm)):
            @pl.when(tbuf_ref[k] == buf)
            def _():
                pltpu.make_async_copy(ybuf.at[slot], ys_hbm.at[rows_of(k)], sem_out.at[slot]).start()

    def wait_out(k):
        slot = k % MOE_BUFFERS
        pltpu.make_async_copy(ybuf.at[slot], ys0_hbm.at[pl.ds(0, MOE_TILE)], sem_out.at[slot]).wait()

    @pl.when(e == 0)
    def _():
        for k in range(MOE_LOOKAHEAD):
            @pl.when(k < k_total)
            def _():
                start_in(k)

    @pl.when(n_tiles > 0)
    def _():
        wgu_s[:, 0:hidden] = wg_ref[0].astype(BF16)
        wgu_s[:, hidden:2 * hidden] = wu_ref[0].astype(BF16)
        wd_s[...] = wd_ref[0].astype(BF16)

    def tile(t, carry):
        k = k_first + t

        @pl.when(k + MOE_LOOKAHEAD < k_total)
        def _():
            start_in(k + MOE_LOOKAHEAD)

        wait_in(k)

        @pl.when(k >= MOE_BUFFERS)
        def _():
            wait_out(k - MOE_BUFFERS)

        slot = k % MOE_BUFFERS
        row = lax.broadcasted_iota(I32, (MOE_TILE, half), 0)
        x_lo, x_hi = _unpack_bf16_pair(jnp.where(row < tvalid_ref[k], xbuf[slot], jnp.uint32(0)))
        ab = (jnp.dot(x_lo.astype(BF16), wgu_s[0:half, :], preferred_element_type=F32)
              + jnp.dot(x_hi.astype(BF16), wgu_s[half:2 * half, :], preferred_element_type=F32))
        a = ab[:, 0:hidden]
        he = a * _sigmoid(a) * ab[:, hidden:2 * hidden]
        y = jnp.dot(he.astype(BF16), wd_s[...], preferred_element_type=F32)
        ybuf[slot] = _pack_bf16_pair(y[:, 0:half], y[:, half:2 * half])
        start_out(k)
        return carry

    lax.fori_loop(0, n_tiles, tile, 0)

    @pl.when(e == pl.num_programs(0) - 1)
    def _():
        for j in range(1, MOE_BUFFERS + 1):
            @pl.when(k_total >= j)
            def _():
                wait_out(k_total - j)


def _moe_call(groups, w_g, w_u, w_d):
    (xs0, starts0, cnt0), (xs1, starts1, cnt1) = groups
    half = xs0.shape[1]
    n_experts, d_model, hidden = w_g.shape
    tiles_of = lambda cnt: (cnt + MOE_TILE - 1) // MOE_TILE
    n0, n1 = tiles_of(cnt0), tiles_of(cnt1)
    k_end = jnp.cumsum(n0 + n1)
    k_first = jnp.concatenate([jnp.zeros((1,), I32), k_end]).astype(I32)
    k_max = (xs0.shape[0] + xs1.shape[0]) // MOE_TILE
    k = jnp.arange(k_max, dtype=I32)
    e_of_k = jnp.minimum(jnp.sum(k[:, None] >= k_end[None, :], axis=1), n_experts - 1)
    onehot = e_of_k[:, None] == jnp.arange(n_experts, dtype=I32)[None, :]
    pick = lambda v: jnp.sum(jnp.where(onehot, v[None, :], 0), axis=1)
    j0 = k - pick(k_first[:-1])
    j1 = j0 - pick(n0)
    in0 = j1 < 0
    t_buf = jnp.where(in0, 0, 1).astype(I32)
    t_row = jnp.where(in0, jnp.minimum(pick(starts0 // MOE_TILE) + j0, xs0.shape[0] // MOE_TILE - 1),
                      jnp.clip(pick(starts1 // MOE_TILE) + j1, 0, xs1.shape[0] // MOE_TILE - 1)).astype(I32)
    t_valid = jnp.where(in0, pick(cnt0) - j0 * MOE_TILE, pick(cnt1) - j1 * MOE_TILE).astype(I32)

    wspec = lambda shape: pl.BlockSpec(shape, lambda e, *_: (e, 0, 0))
    grid_spec = pltpu.PrefetchScalarGridSpec(
        num_scalar_prefetch=5,
        grid=(n_experts,),
        in_specs=[
            pl.BlockSpec(memory_space=pl.ANY),
            pl.BlockSpec(memory_space=pl.ANY),
            wspec((1, d_model, hidden)),
            wspec((1, d_model, hidden)),
            wspec((1, hidden, d_model)),
        ],
        out_specs=[pl.BlockSpec(memory_space=pl.ANY), pl.BlockSpec(memory_space=pl.ANY)],
        scratch_shapes=[
            pltpu.VMEM((d_model, 2 * hidden), BF16),
            pltpu.VMEM((hidden, d_model), BF16),
            pltpu.VMEM((MOE_BUFFERS, MOE_TILE, half), U32),
            pltpu.VMEM((MOE_BUFFERS, MOE_TILE, half), U32),
            pltpu.SemaphoreType.DMA((MOE_BUFFERS,)),
            pltpu.SemaphoreType.DMA((MOE_BUFFERS,)),
        ],
    )
    return pl.pallas_call(
        _moe_kernel, grid_spec=grid_spec,
        out_shape=[jax.ShapeDtypeStruct(xs0.shape, U32), jax.ShapeDtypeStruct(xs1.shape, U32)], name="moe_experts",
        compiler_params=pltpu.CompilerParams(
            dimension_semantics=("arbitrary",), vmem_limit_bytes=VMEM_LIMIT),
    )(k_first, k_end[-1:].astype(I32), t_buf, t_row, t_valid, xs0, xs1, w_g, w_u, w_d)


def _combine_kernel(x1_ref, y0_ref, y1_ref, rw_ref, gf_ref, *rest):
    out_ref = rest[-1]
    tr = x1_ref.shape[0]
    w_rows = jnp.concatenate([rw_ref[0], jnp.zeros((LANES - rw_ref.shape[1], tr), F32)], axis=0)
    w_cols = w_rows.T
    w0, w1 = w_cols[:, 0:1], w_cols[:, 1:2]
    a_lo, a_hi = _unpack_bf16_pair(y0_ref[0])
    b_lo, b_hi = _unpack_bf16_pair(y1_ref[0])
    moe = jnp.concatenate([w0 * a_lo + w1 * b_lo, w0 * a_hi + w1 * b_hi], axis=-1)
    out_ref[...] = _rms(x1_ref[...] + moe, gf_ref[...])


def _combine_call(x1, rw, row0, n, yg, gf, out_rows, out_row0, prev_out=None):
    t, d_model = x1.shape
    tr = rw.shape[-1]
    half = yg.shape[-1]
    assert t % tr == 0 and row0 % tr == 0 and n % tr == 0 and rw.shape == (t // tr, EXPERTS_PER_GROUP, tr)
    assert yg.shape == (2, n, half) and out_row0 % tr == 0
    off = row0 // tr
    ooff = out_row0 // tr
    in_specs = [
        pl.BlockSpec((tr, d_model), lambda i: (off + i, 0)),
        pl.BlockSpec((1, tr, half), lambda i: (0, i, 0)),
        pl.BlockSpec((1, tr, half), lambda i: (1, i, 0)),
        pl.BlockSpec((1, EXPERTS_PER_GROUP, tr), lambda i: (off + i, 0, 0)),
        pl.BlockSpec((1, d_model), lambda i: (0, 0)),
    ]
    args = [x1, yg, yg, rw, gf]
    aliases = {}
    if prev_out is not None:
        in_specs.append(pl.BlockSpec(memory_space=pl.ANY))
        args.append(prev_out)
        aliases = {len(args) - 1: 0}
    return pl.pallas_call(
        _combine_kernel,
        grid=(n // tr,),
        in_specs=in_specs,
        out_specs=pl.BlockSpec((tr, d_model), lambda i: (ooff + i, 0)),
        out_shape=jax.ShapeDtypeStruct((out_rows, d_model), F32), name=f"combine_row{out_row0}_of{out_rows}",
        input_output_aliases=aliases,
        compiler_params=pltpu.CompilerParams(
            dimension_semantics=("arbitrary",), vmem_limit_bytes=VMEM_LIMIT),
    )(*args)


def _route(streams):
    tokens = [h2.shape[0] for h2, _, _ in streams]
    counts = [cnt[:, 0].astype(I32) for _, _, cnt in streams]
    total = sum(counts)
    padded = ((total + MOE_TILE - 1) // MOE_TILE) * MOE_TILE
    starts = (jnp.cumsum(padded) - padded).astype(I32)
    experts = jnp.arange(N_EXPERTS, dtype=I32)[None, :, None]
    pos, base = [], starts
    for (_, ri, _), t, cnt in zip(streams, tokens, counts):
        ri = jnp.moveaxis(ri, 2, 0).reshape(ri.shape[2], t)
        first_row = jnp.sum(jnp.where(ri[0:2, None, :] == experts, base[None, :, None], 0), axis=1)
        pos.append(ri[2:4] + first_row)
        base = base + cnt
    pos = jnp.concatenate(pos, axis=1)
    n_rows = ((2 * sum(tokens) + N_EXPERTS * (MOE_TILE - 1)) // MOE_TILE) * MOE_TILE
    xs_sorted = _sc_dispatch([h2 for h2, _, _ in streams], pos[0], pos[1], n_rows)
    return (xs_sorted, starts, total), pos


def _gather_tokens(ys_sorted, pos, t0, n):
    return _sc_gather(ys_sorted, pos[:, t0:t0 + n].reshape(2 * n)).reshape(2, n, ys_sorted.shape[-1])


def _one_layer(xp, xs, s_ret, c_pool, norm1_g, w_in, ret_norm_g, w_pool, pool_scale, w_out, norm2_g,
               w_rg, w_re, w_g, w_u, w_d, final_g, past_len):
    bp, seq, d_model = xp.shape
    bs, dseq, _ = xs.shape
    rw_width = ret_norm_g.shape[-1]
    pw = pool_scale.shape[-1]
    dh = rw_width // RET_HEADS
    half = d_model // 2

    w_r = jnp.concatenate(
        [w_re, w_rg, jnp.zeros((d_model, LANES - N_EXPERTS - N_EXPERT_GROUPS), F32)], axis=1)
    wr_hi = w_r.astype(BF16)
    wr = jnp.concatenate([wr_hi, (w_r - wr_hi.astype(F32)).astype(BF16)], axis=1)
    consts = dict(
        g1=norm1_g.reshape(1, d_model), w_in=w_in.astype(BF16), gret=ret_norm_g.reshape(1, rw_width),
        w_pool=w_pool.astype(BF16), pscale=pool_scale.reshape(1, pw), w_out=w_out.astype(BF16),
        g2=norm2_g.reshape(1, d_model), wr=wr)

    gf = final_g.reshape(1, d_model)

    ts = bs * dseq
    b_lead = bp - 1
    t_lead, t_rest = b_lead * seq, (bp - b_lead) * seq
    zeros = lambda nb: (jnp.zeros((nb, RET_HEADS, dh, dh), F32), jnp.zeros((nb, HIST_ROWS, pw), F32))
    h0s = jnp.pad(c_pool, ((0, 0), (HIST_ROWS - POOL_HIST, 0), (0, 0)))
    prompt_tile = dict(bb=1, tl=PROMPT_TILE, chunk=256)

    def stream(layer_out, t):
        x1, h2, ri, rw, st, hist, cnt = layer_out
        return dict(x1=x1.reshape(t, d_model), route=(h2.reshape(t, half), ri, cnt),
                    rw=rw.reshape(-1, EXPERTS_PER_GROUP, rw.shape[-1]), st=st, hist=hist)

    pa = stream(_layer_call(xp, 0, b_lead, *zeros(b_lead), 0, consts, gf, **prompt_tile), t_lead)
    group0, pos0 = _route([pa["route"]])
    pb = stream(_layer_call(xp, b_lead, bp - b_lead, *zeros(bp - b_lead), 0, consts, pa["route"][2], **prompt_tile),
                t_rest)
    sm = stream(_layer_call(xs, 0, bs, s_ret, h0s, past_len, consts, pb["route"][2],
                            bb=bs, tl=dseq, chunk=min(64, dseq)), ts)
    group1, pos1 = _route([pb["route"], sm["route"]])
    ys0, ys1 = _moe_call([group0, group1], w_g, w_u, w_d)

    tp = bp * seq
    yp = _combine_call(pb["x1"], pb["rw"], 0, t_rest, _gather_tokens(ys1, pos1, 0, t_rest), gf, tp, t_lead)
    ysm = _combine_call(sm["x1"], sm["rw"], 0, ts, _gather_tokens(ys1, pos1, t_rest, ts), gf, ts, 0)
    row0 = 0
    for nb in COMBINE_CHUNKS:
        n = min(nb * seq, t_lead - row0)
        if n > 0:
            yp = _combine_call(pa["x1"], pa["rw"], row0, n, _gather_tokens(ys0, pos0, row0, n), gf, tp, row0,
                               prev_out=yp)
            row0 += n
    assert row0 == t_lead
    st_p = jnp.concatenate([pa["st"], pb["st"]], axis=0)
    hist_p = jnp.concatenate([pa["hist"], pb["hist"]], axis=0)
    return (yp.reshape(bp, seq, d_model), ysm.reshape(bs, dseq, d_model),
            st_p, hist_p[:, HIST_ROWS - POOL_HIST:], sm["st"], sm["hist"][:, HIST_ROWS - POOL_HIST:])


def kernel(x_prompt, x_sample, state_ret, cache_pool, norm1_g, w_in, ret_norm_g, w_pool, pool_scale, w_out,
           norm2_g, w_router_group, w_router_expert, w_exp_gate, w_exp_up, w_exp_down, final_norm_g):
    depth = w_in.shape[0]
    assert depth == 1, "the final RMSNorm is fused into the layer's combine kernel"
    assert x_prompt.shape[0] >= 2 and x_prompt.shape[1] % PROMPT_TILE == 0
    yp, ys, s_p, h_p, s_s, h_s = _one_layer(
        x_prompt, x_sample, state_ret[0], cache_pool[0], norm1_g[0], w_in[0], ret_norm_g[0], w_pool[0],
        pool_scale[0], w_out[0], norm2_g[0], w_router_group[0], w_router_expert[0],
        w_exp_gate[0], w_exp_up[0], w_exp_down[0], final_norm_g, PAST_LEN)
    return (yp, ys, s_p[None], h_p[None], s_s[None], h_s[None])
```

```python
import functools

import jax
import jax.numpy as jnp
from jax import lax
from jax.experimental import pallas as pl
from jax.experimental.pallas import tpu as pltpu
from jax.experimental.pallas import tpu_sc as plsc

F32 = jnp.float32
BF16 = jnp.bfloat16
I32 = jnp.int32
U32 = jnp.uint32

EPS = 1e-6
ROPE_BASE = 10000.0
RET_HEADS = 4
POOL_WINDOWS = (2, 4, 8, 16)
POOL_HIST = max(POOL_WINDOWS) - 1
N_EXPERT_GROUPS = 4
EXPERTS_PER_GROUP = 8
N_EXPERTS = N_EXPERT_GROUPS * EXPERTS_PER_GROUP
EXPERT_SHIFT = EXPERTS_PER_GROUP.bit_length() - 1
PAST_LEN = 1024

LANES = 128
HIST_ROWS = 16
MOE_TILE = 256
MOE_BUFFERS = 4
MOE_LOOKAHEAD = MOE_BUFFERS - 1
PROMPT_TILE = 1024
COMBINE_CHUNKS = (1, 2)
SC_UNIT = 32
VMEM_LIMIT = 56 * 1024 * 1024


def _rms(x, g):
    return x * lax.rsqrt(jnp.mean(x * x, axis=-1, keepdims=True) + EPS) * g


def _sigmoid(x):
    return 1.0 / (1.0 + jnp.exp(-x))


def _pack_bf16_pair(lo, hi):
    lo_b = lax.bitcast_convert_type(lo.astype(BF16).astype(F32), U32)
    hi_b = lax.bitcast_convert_type(hi.astype(BF16).astype(F32), U32)
    return hi_b | (lo_b >> 16)


def _unpack_bf16_pair(p):
    lo = lax.bitcast_convert_type(p << 16, F32)
    hi = lax.bitcast_convert_type(p & jnp.uint32(0xFFFF0000), F32)
    return lo, hi


def _layer_kernel(dc_ref, x_ref, s0_ref, h0_ref, rb_ref, rc_ref, rs_ref, rcs_ref, rss_ref,
                  dintra_ref, dq_ref, dk_ref,
                  g1_ref, win_ref, gret_ref, wpool_ref, pscale_ref, wout_ref, g2_ref,
                  wr_ref, tri_ref, after_ref,
                  x1_ref, h2_ref, ri_ref, rw_ref, st_ref, hist_ref, cnt_ref,
                  ue_ref, q_ref, k_ref, v_ref, o_ref, a_ref,
                  *, bb, tl, chunk, pos0):
    b_idx = pl.program_id(0)
    l_idx = pl.program_id(1)
    rows = bb * tl
    d_model = x_ref.shape[-1]
    rw_width = q_ref.shape[-1]
    dh = rw_width // RET_HEADS
    pw = ue_ref.shape[-1]
    gw = pw // len(POOL_WINDOWS)
    n_chunks = tl // chunk

    @pl.when(l_idx == 0)
    def _():
        st_ref[...] = s0_ref[...]
        ue_ref[:, 0:HIST_ROWS, :] = h0_ref[...]

    @pl.when((l_idx == 0) & (b_idx == 0))
    def _():
        cnt_ref[...] = jnp.zeros_like(cnt_ref)

    x = x_ref[...].reshape(rows, d_model)
    hb = _rms(x, g1_ref[...]).astype(BF16)

    def project(c0, c1):
        return jnp.dot(hb, win_ref[:, c0:c1], preferred_element_type=F32)

    proj = project(0, 2 * rw_width)

    cos_b = rb_ref[0, 0:1, :]
    sin_b = rb_ref[0, 1:2, :]
    cosf = (cos_b * rc_ref[...] - sin_b * rs_ref[...])[None]
    sinf = (sin_b * rcs_ref[...] + cos_b * rss_ref[...])[None]
    k_scale = dh ** -0.5
    for hh in range(RET_HEADS):
        qh = proj[:, hh * dh:(hh + 1) * dh]
        kh = proj[:, rw_width + hh * dh:rw_width + (hh + 1) * dh]
        qr = (qh.reshape(bb, tl, dh) * cosf
              + pltpu.roll(qh, dh // 2, 1).reshape(bb, tl, dh) * sinf).reshape(rows, dh)
        kr = (kh.reshape(bb, tl, dh) * cosf
              + pltpu.roll(kh, dh // 2, 1).reshape(bb, tl, dh) * sinf).reshape(rows, dh)
        q_ref[:, hh * dh:(hh + 1) * dh] = qr.astype(BF16)
        k_ref[:, hh * dh:(hh + 1) * dh] = kr * k_scale
    v_ref[...] = project(2 * rw_width, 3 * rw_width).astype(BF16)
    gate = project(3 * rw_width, 4 * rw_width)
    u = project(4 * rw_width, 4 * rw_width + pw)

    def ret_block(b, c):
        r0 = b * tl + c * chunk
        if not isinstance(r0, int):
            r0 = pl.multiple_of(r0, chunk)
        for hh in range(RET_HEADS):
            cs = slice(hh * dh, (hh + 1) * dh)
            qc = q_ref[pl.ds(r0, chunk), cs]
            kf = k_ref[pl.ds(r0, chunk), cs]
            vc = v_ref[pl.ds(r0, chunk), cs]
            s_old = st_ref[b, hh]
            sc = lax.dot_general(qc, kf.astype(BF16), (((1,), (1,)), ((), ())),
                                 preferred_element_type=F32) * dintra_ref[hh]
            o = (jnp.dot(sc.astype(BF16), vc, preferred_element_type=F32)
                 + dq_ref[hh] * jnp.dot(qc, s_old.astype(BF16), preferred_element_type=F32))
            kd = (kf * dk_ref[hh]).astype(BF16)
            s_new = dc_ref[hh] * s_old + lax.dot_general(
                kd, vc, (((0,), (0,)), ((), ())), preferred_element_type=F32)
            st_ref[b, hh] = s_new
            o_ref[pl.ds(r0, chunk), cs] = o

    if bb * n_chunks <= 4:
        for b in range(bb):
            for c in range(n_chunks):
                ret_block(b, c)
    else:
        def body(i, carry):
            ret_block(i // n_chunks, i % n_chunks)
            return carry
        lax.fori_loop(0, bb * n_chunks, body, 0)

    for hh in range(RET_HEADS):
        cs = slice(hh * dh, (hh + 1) * dh)
        oh = o_ref[:, cs]
        mu = jnp.mean(oh, axis=-1, keepdims=True)
        oc = oh - mu
        var = jnp.mean(oc * oc, axis=-1, keepdims=True)
        y = oc * lax.rsqrt(var + EPS) * gret_ref[:, cs]
        g = gate[:, cs]
        a_ref[:, cs] = (g * _sigmoid(g) * y).astype(BF16)

    ue_ref[:, HIST_ROWS:HIST_ROWS + tl, :] = u.reshape(bb, tl, pw)
    pos = pos0 + l_idx * tl + lax.broadcasted_iota(I32, (1, tl, 1), 1)
    for gi, w in enumerate(POOL_WINDOWS):
        cs = slice(gi * gw, (gi + 1) * gw)
        acc = ue_ref[:, HIST_ROWS:HIST_ROWS + tl, cs]
        for j in range(1, w):
            acc = acc + ue_ref[:, HIST_ROWS - j:HIST_ROWS - j + tl, cs]
        inv_cnt = 1.0 / jnp.minimum(pos + 1, w).astype(F32)
        p = (acc * inv_cnt).reshape(rows, gw) - u[:, cs]
        z = jnp.dot(p.astype(BF16), wpool_ref[gi], preferred_element_type=F32) * pscale_ref[:, cs]
        a_ref[:, rw_width + gi * gw:rw_width + (gi + 1) * gw] = z.astype(BF16)
    tail = ue_ref[:, tl:tl + HIST_ROWS, :]
    ue_ref[:, 0:HIST_ROWS, :] = tail
    hist_ref[...] = tail

    x1 = x + jnp.dot(a_ref[...], wout_ref[...], preferred_element_type=F32)
    x1_ref[...] = x1.reshape(bb, tl, d_model)
    h2 = _rms(x1, g2_ref[...])
    h2_ref[...] = _pack_bf16_pair(h2[:, 0:d_model // 2], h2[:, d_model // 2:]).reshape(bb, tl, d_model // 2)

    h2_hi = h2.astype(BF16)
    h2_lo = (h2 - h2_hi.astype(F32)).astype(BF16)
    two = jnp.dot(h2_hi, wr_ref[...], preferred_element_type=F32)
    logits = (two[:, 0:LANES] + two[:, LANES:2 * LANES]
              + jnp.dot(h2_lo, wr_ref[:, 0:LANES], preferred_element_type=F32))
    lt = logits.T
    neg = jnp.float32(-jnp.inf)
    big = jnp.float32(1e9)
    sub = lax.broadcasted_iota(I32, (EXPERTS_PER_GROUP, rows), 0).astype(F32)
    gl = jnp.where(sub < N_EXPERT_GROUPS, lt[N_EXPERTS:N_EXPERTS + EXPERTS_PER_GROUP], neg)
    gmax = jnp.max(gl, axis=0, keepdims=True)
    gidx = jnp.min(jnp.where(gl == gmax, sub, big), axis=0, keepdims=True)
    p_sel = 1.0 / jnp.sum(jnp.exp(gl - gmax), axis=0, keepdims=True)
    el = lt[0:EXPERTS_PER_GROUP]
    for g in range(1, N_EXPERT_GROUPS):
        el = jnp.where(gidx == g, lt[g * EXPERTS_PER_GROUP:(g + 1) * EXPERTS_PER_GROUP], el)
    m1 = jnp.max(el, axis=0, keepdims=True)
    t1 = jnp.min(jnp.where(el == m1, sub, big), axis=0, keepdims=True)
    el2 = jnp.where(sub == t1, neg, el)
    m2 = jnp.max(el2, axis=0, keepdims=True)
    t2 = jnp.min(jnp.where(el2 == m2, sub, big), axis=0, keepdims=True)
    e2 = jnp.exp(m2 - m1)
    w1 = p_sel / (1.0 + e2)
    w2 = p_sel * e2 / (1.0 + e2)
    i1 = gidx * EXPERTS_PER_GROUP + t1
    i2 = gidx * EXPERTS_PER_GROUP + t2

    eid = lax.broadcasted_iota(I32, (N_EXPERTS, rows), 0).astype(F32)
    hit1 = eid == i1
    hit2 = eid == i2
    onehot = (hit1 | hit2).astype(BF16)
    before = jnp.dot(onehot, tri_ref[...], preferred_element_type=F32) + cnt_ref[...]
    r1 = jnp.sum(jnp.where(hit1, before, 0.0), axis=0, keepdims=True)
    r2 = jnp.sum(jnp.where(hit2, before, 0.0), axis=0, keepdims=True)
    cnt_ref[...] = cnt_ref[...] + jnp.sum(onehot.astype(F32), axis=1, keepdims=True)

    ri = jnp.where(sub == 0, i1, jnp.where(sub == 1, i2, jnp.where(sub == 2, r1, jnp.where(sub == 3, r2, 0.0))))
    ri_ref[...] = ri.astype(I32).reshape(ri_ref.shape)
    rw_ref[...] = jnp.where(sub == 0, w1, jnp.where(sub == 1, w2, 0.0)).reshape(rw_ref.shape)


def _rope_tables(pos0, seq, tl, dh):
    half = dh // 2
    inv = ROPE_BASE ** (-jnp.arange(half, dtype=F32) / half)
    ang_t = jnp.arange(tl, dtype=F32)[:, None] * inv[None, :]
    ang_b = (pos0 + tl * jnp.arange(seq // tl)).astype(F32)[:, None] * inv[None, :]
    dup = lambda a: jnp.concatenate([a, a], axis=-1)
    sgn = lambda a: jnp.concatenate([-a, a], axis=-1)
    base = jnp.stack([dup(jnp.cos(ang_b)), dup(jnp.sin(ang_b))], axis=1)
    base = jnp.pad(base, ((0, 0), (0, 8 - base.shape[1]), (0, 0)))
    cos_t, sin_t = jnp.cos(ang_t), jnp.sin(ang_t)
    return base, dup(cos_t), dup(sin_t), sgn(cos_t), sgn(sin_t)


def _layer_call(x, b0, nb, s0, h0, pos0, consts, after, *, bb, tl, chunk):
    _, seq, d_model = x.shape
    bsz = nb
    blk0 = b0 // bb
    rows = bb * tl
    rw_width = consts["gret"].shape[-1]
    pw = consts["pscale"].shape[-1]
    dh = rw_width // RET_HEADS

    rope = _rope_tables(pos0, seq, tl, dh)

    lg = jnp.log1p(-jnp.exp2(-5.0 - jnp.arange(RET_HEADS, dtype=F32)))
    idx = jnp.arange(chunk, dtype=F32)
    diff = idx[:, None] - idx[None, :]
    d_intra = jnp.where(diff[None] >= 0, jnp.exp(jnp.maximum(diff, 0.0)[None] * lg[:, None, None]), 0.0)
    d_q = jnp.broadcast_to(jnp.exp((idx + 1.0)[None, :] * lg[:, None])[:, :, None], (RET_HEADS, chunk, dh))
    d_k = jnp.broadcast_to(jnp.exp((chunk - 1.0 - idx)[None, :] * lg[:, None])[:, :, None], (RET_HEADS, chunk, dh))
    d_c = jnp.exp(chunk * lg)
    tri = jnp.triu(jnp.ones((rows, rows), BF16), 1)

    const2 = lambda b, l, *_: (0, 0)
    const3 = lambda b, l, *_: (0, 0, 0)
    grid_spec = pltpu.PrefetchScalarGridSpec(
        num_scalar_prefetch=0,
        grid=(bsz // bb, seq // tl),
        in_specs=[
            pl.BlockSpec(memory_space=pltpu.SMEM),
            pl.BlockSpec((bb, tl, d_model), lambda b, l: (blk0 + b, l, 0)),
            pl.BlockSpec((bb, RET_HEADS, dh, dh), lambda b, l: (b, 0, 0, 0)),
            pl.BlockSpec((bb, HIST_ROWS, pw), lambda b, l: (b, 0, 0)),
            pl.BlockSpec((1, 8, dh), lambda b, l: (l, 0, 0)),
            pl.BlockSpec((tl, dh), const2),
            pl.BlockSpec((tl, dh), const2),
            pl.BlockSpec((tl, dh), const2),
            pl.BlockSpec((tl, dh), const2),
            pl.BlockSpec((RET_HEADS, chunk, chunk), const3),
            pl.BlockSpec((RET_HEADS, chunk, dh), const3),
            pl.BlockSpec((RET_HEADS, chunk, dh), const3),
            pl.BlockSpec((1, d_model), const2),
            pl.BlockSpec(consts["w_in"].shape, const2),
            pl.BlockSpec((1, rw_width), const2),
            pl.BlockSpec(consts["w_pool"].shape, const3),
            pl.BlockSpec((1, pw), const2),
            pl.BlockSpec(consts["w_out"].shape, const2),
            pl.BlockSpec((1, d_model), const2),
            pl.BlockSpec((d_model, 2 * LANES), const2),
            pl.BlockSpec((rows, rows), const2),
            pl.BlockSpec(memory_space=pl.ANY),
        ],
        out_specs=[
            pl.BlockSpec((bb, tl, d_model), lambda b, l: (b, l, 0)),
            pl.BlockSpec((bb, tl, d_model // 2), lambda b, l: (b, l, 0)),
            pl.BlockSpec((1, 1, EXPERTS_PER_GROUP, rows), lambda b, l: (b, l, 0, 0)),
            pl.BlockSpec((1, 1, EXPERTS_PER_GROUP, rows), lambda b, l: (b, l, 0, 0)),
            pl.BlockSpec((bb, RET_HEADS, dh, dh), lambda b, l: (b, 0, 0, 0)),
            pl.BlockSpec((bb, HIST_ROWS, pw), lambda b, l: (b, 0, 0)),
            pl.BlockSpec((N_EXPERTS, rows), const2),
        ],
        scratch_shapes=[
            pltpu.VMEM((bb, HIST_ROWS + tl, pw), F32),
            pltpu.VMEM((rows, rw_width), BF16),
            pltpu.VMEM((rows, rw_width), F32),
            pltpu.VMEM((rows, rw_width), BF16),
            pltpu.VMEM((rows, rw_width), F32),
            pltpu.VMEM((rows, d_model), BF16),
        ],
    )
    out_shape = [
        jax.ShapeDtypeStruct((bsz, seq, d_model), F32),
        jax.ShapeDtypeStruct((bsz, seq, d_model // 2), U32),
        jax.ShapeDtypeStruct((bsz // bb, seq // tl, EXPERTS_PER_GROUP, rows), I32),
        jax.ShapeDtypeStruct((bsz // bb, seq // tl, EXPERTS_PER_GROUP, rows), F32),
        jax.ShapeDtypeStruct((bsz, RET_HEADS, dh, dh), F32),
        jax.ShapeDtypeStruct((bsz, HIST_ROWS, pw), F32),
        jax.ShapeDtypeStruct((N_EXPERTS, rows), F32),
    ]
    kern = functools.partial(_layer_kernel, bb=bb, tl=tl, chunk=chunk, pos0=pos0)
    return pl.pallas_call(
        kern, grid_spec=grid_spec, out_shape=out_shape, name=f"layer_pos{pos0}_b{b0}",
        compiler_params=pltpu.CompilerParams(
            dimension_semantics=("arbitrary", "arbitrary"), vmem_limit_bytes=VMEM_LIMIT),
    )(d_c, x, s0, h0, *rope, d_intra, d_q, d_k,
      consts["g1"], consts["w_in"], consts["gret"], consts["w_pool"], consts["pscale"],
      consts["w_out"], consts["g2"], consts["wr"], tri, after)


def _sc_partition(n_units):
    info = plsc.get_sparse_core_info()
    nc, nw = info.num_cores, info.num_cores * info.num_subcores
    upw = -(-n_units // nw)
    upw += upw % 2
    return nc, nw, upw


def _units_by_worker(idx, n_units, upw, nw):
    idx = jnp.pad(idx.reshape(n_units, SC_UNIT), ((0, nw * upw - n_units), (0, 0)))
    return idx.reshape(upw, nw, SC_UNIT).transpose(1, 0, 2)


def _sc_dispatch(srcs, idx0, idx1, n_out_rows):
    assert 1 <= len(srcs) <= 2
    d = srcs[0].shape[1]
    dtype = srcs[0].dtype
    assert all(src.shape[0] % SC_UNIT == 0 for src in srcs)
    units_a = srcs[0].shape[0] // SC_UNIT
    n_units = sum(src.shape[0] for src in srcs) // SC_UNIT
    nc, nw, upw = _sc_partition(n_units)
    idx0 = _units_by_worker(idx0, n_units, upw, nw)
    idx1 = _units_by_worker(idx1, n_units, upw, nw)
    mesh = plsc.VectorSubcoreMesh(core_axis_name="c", subcore_axis_name="s")
    dma = pltpu.SemaphoreType.DMA

    @functools.partial(
        pl.kernel, mesh=mesh,
        out_type=jax.ShapeDtypeStruct((n_out_rows, d), dtype),
        scratch_types=[
            pltpu.VMEM((upw, SC_UNIT), I32),
            pltpu.VMEM((upw, SC_UNIT), I32),
            pltpu.VMEM((SC_UNIT, d), dtype),
            pltpu.VMEM((SC_UNIT, d), dtype),
            dma, dma, dma, dma, dma, dma,
        ],
    )
    def k(*refs):
        src_hbm = refs[:len(srcs)]
        i0_hbm, i1_hbm, out_hbm, i0_v, i1_v, rows0, rows1, l0, l1, p0, p1, q0, q1 = refs[len(srcs):]
        wid = lax.axis_index("s") * nc + lax.axis_index("c")
        pltpu.sync_copy(i0_hbm.at[wid], i0_v)
        pltpu.sync_copy(i1_hbm.at[wid], i1_v)
        rows, lsem, psem, qsem = (rows0, rows1), (l0, l1), (p0, p1), (q0, q1)

        def live(j):
            return j * nw + wid < n_units

        def load(j, b, op):
            unit = j * nw + wid

            @pl.when(live(j) & (unit < units_a))
            def _():
                op(pltpu.make_async_copy(
                    src_hbm[0].at[pl.ds(pl.multiple_of(unit * SC_UNIT, 8), SC_UNIT)], rows[b], lsem[b]))

            if len(srcs) == 2:
                @pl.when(live(j) & (unit >= units_a))
                def _():
                    op(pltpu.make_async_copy(
                        src_hbm[1].at[pl.ds(pl.multiple_of((unit - units_a) * SC_UNIT, 8), SC_UNIT)],
                        rows[b], lsem[b]))

        def scatter(j, b, op):
            @pl.when(live(j))
            def _():
                op(pltpu.make_async_copy(rows[b], out_hbm.at[i0_v.at[j]], psem[b]))
                op(pltpu.make_async_copy(rows[b], out_hbm.at[i1_v.at[j]], qsem[b]))

        start = lambda c: c.start()
        wait = lambda c: c.wait()
        load(0, 0, start)

        @pl.loop(0, upw, step=2)
        def _(j):
            @pl.when(j > 0)
            def _():
                scatter(j - 1, 1, wait)
            load(j + 1, 1, start)
            load(j, 0, wait)
            scatter(j, 0, start)
            scatter(j, 0, wait)

            @pl.when(j + 2 < upw)
            def _():
                load(j + 2, 0, start)
            load(j + 1, 1, wait)
            scatter(j + 1, 1, start)

        scatter(upw - 1, 1, wait)

    return k(*srcs, idx0, idx1), idx1


def _sc_gather(table, idx):
    n = idx.shape[0]
    d = table.shape[1]
    assert n % SC_UNIT == 0
    n_units = n // SC_UNIT
    nc, nw, upw = _sc_partition(n_units)
    idx = _units_by_worker(idx, n_units, upw, nw)
    mesh = plsc.VectorSubcoreMesh(core_axis_name="c", subcore_axis_name="s")
    dma = pltpu.SemaphoreType.DMA

    @functools.partial(
        pl.kernel, mesh=mesh,
        out_type=jax.ShapeDtypeStruct((n, d), table.dtype),
        scratch_types=[
            pltpu.VMEM((upw, SC_UNIT), I32),
            pltpu.VMEM((SC_UNIT, d), table.dtype),
            pltpu.VMEM((SC_UNIT, d), table.dtype),
            dma, dma, dma, dma,
        ],
    )
    def k(t_hbm, i_hbm, out_hbm, i_v, rows0, rows1, g0, g1, w0, w1):
        wid = lax.axis_index("s") * nc + lax.axis_index("c")
        pltpu.sync_copy(i_hbm.at[wid], i_v)
        rows, gsem, wsem = (rows0, rows1), (g0, g1), (w0, w1)

        def live(j):
            return j * nw + wid < n_units

        def gather(j, b, op):
            @pl.when(live(j))
            def _():
                op(pltpu.make_async_copy(t_hbm.at[i_v.at[j]], rows[b], gsem[b]))

        def write(j, b, op):
            @pl.when(live(j))
            def _():
                op(pltpu.make_async_copy(
                    rows[b], out_hbm.at[pl.ds(pl.multiple_of((j * nw + wid) * SC_UNIT, 8), SC_UNIT)], wsem[b]))

        start = lambda c: c.start()
        wait = lambda c: c.wait()
        gather(0, 0, start)

        @pl.loop(0, upw, step=2)
        def _(j):
            @pl.when(j > 0)
            def _():
                write(j - 1, 1, wait)
            gather(j + 1, 1, start)
            gather(j, 0, wait)
            write(j, 0, start)
            write(j, 0, wait)

            @pl.when(j + 2 < upw)
            def _():
                gather(j + 2, 0, start)
            gather(j + 1, 1, wait)
            write(j + 1, 1, start)

        write(upw - 1, 1, wait)

    return k(table, idx)


def _moe_kernel(kfirst_ref, ktot_ref, tbuf_ref, trow_ref, tvalid_ref, xs0_hbm, xs1_hbm, wg_ref, wu_ref, wd_ref,
                ys0_hbm, ys1_hbm, wgu_s, wd_s, xbuf, ybuf, sem_in, sem_out):
    e = pl.program_id(0)
    k_first = kfirst_ref[e]
    n_tiles = kfirst_ref[e + 1] - k_first
    k_total = ktot_ref[0]
    hidden = wd_s.shape[0]
    half = xbuf.shape[-1]

    def rows_of(k):
        return pl.ds(pl.multiple_of(trow_ref[k] * MOE_TILE, MOE_TILE), MOE_TILE)

    def start_in(k):
        slot = k % MOE_BUFFERS
        for buf, xs_hbm in enumerate((xs0_hbm, xs1_hbm)):
            @pl.when(tbuf_ref[k] == buf)
            def _():
                pltpu.make_async_copy(xs_hbm.at[rows_of(k)], xbuf.at[slot], sem_in.at[slot]).start()

    def wait_in(k):
        slot = k % MOE_BUFFERS
        pltpu.make_async_copy(xs0_hbm.at[pl.ds(0, MOE_TILE)], xbuf.at[slot], sem_in.at[slot]).wait()

    def start_out(k):
        slot = k % MOE_BUFFERS
        for buf, ys_hbm in enumerate((ys0_hbm, ys1_hbm)):
            @pl.when(tbuf_ref[k] == buf)
            def _():
                pltpu.make_async_copy(ybuf.at[slot], ys_hbm.at[rows_of(k)], sem_out.at[slot]).start()

    def wait_out(k):
        slot = k % MOE_BUFFERS
        pltpu.make_async_copy(ybuf.at[slot], ys0_hbm.at[pl.ds(0, MOE_TILE)], sem_out.at[slot]).wait()

    @pl.when(e == 0)
    def _():
        for k in range(MOE_LOOKAHEAD):
            @pl.when(k < k_total)
            def _():
                start_in(k)

    @pl.when(n_tiles > 0)
    def _():
        wgu_s[:, 0:hidden] = wg_ref[0].astype(BF16)
        wgu_s[:, hidden:2 * hidden] = wu_ref[0].astype(BF16)
        wd_s[...] = wd_ref[0].astype(BF16)

    def tile(t, carry):
        k = k_first + t

        @pl.when(k + MOE_LOOKAHEAD < k_total)
        def _():
            start_in(k + MOE_LOOKAHEAD)

        wait_in(k)

        @pl.when(k >= MOE_BUFFERS)
        def _():
            wait_out(k - MOE_BUFFERS)

        slot = k % MOE_BUFFERS
        row = lax.broadcasted_iota(I32, (MOE_TILE, half), 0)
        x_lo, x_hi = _unpack_bf16_pair(jnp.where(row < tvalid_ref[k], xbuf[slot], jnp.uint32(0)))
        ab = (jnp.dot(x_lo.astype(BF16), wgu_s[0:half, :], preferred_element_type=F32)
              + jnp.dot(x_hi.astype(BF16), wgu_s[half:2 * half, :], preferred_element_type=F32))
        a = ab[:, 0:hidden]
        he = a * _sigmoid(a) * ab[:, hidden:2 * hidden]
        y = jnp.dot(he.astype(BF16), wd_s[...], preferred_element_type=F32)
        ybuf[slot] = _pack_bf16_pair(y[:, 0:half], y[:, half:2 * half])
        start_out(k)
        return carry

    lax.fori_loop(0, n_tiles, tile, 0)

    @pl.when(e == pl.num_programs(0) - 1)
    def _():
        for j in range(1, MOE_BUFFERS + 1):
            @pl.when(k_total >= j)
            def _():
                wait_out(k_total - j)


def _moe_call(groups, w_g, w_u, w_d):
    (xs0, starts0, cnt0), (xs1, starts1, cnt1) = groups
    half = xs0.shape[1]
    n_experts, d_model, hidden = w_g.shape
    tiles_of = lambda cnt: (cnt + MOE_TILE - 1) // MOE_TILE
    n0, n1 = tiles_of(cnt0), tiles_of(cnt1)
    k_end = jnp.cumsum(n0 + n1)
    k_first = jnp.concatenate([jnp.zeros((1,), I32), k_end]).astype(I32)
    k_max = (xs0.shape[0] + xs1.shape[0]) // MOE_TILE
    k = jnp.arange(k_max, dtype=I32)
    e_of_k = jnp.minimum(jnp.sum(k[:, None] >= k_end[None, :], axis=1), n_experts - 1)
    onehot = e_of_k[:, None] == jnp.arange(n_experts, dtype=I32)[None, :]
    pick = lambda v: jnp.sum(jnp.where(onehot, v[None, :], 0), axis=1)
    j0 = k - pick(k_first[:-1])
    j1 = j0 - pick(n0)
    in0 = j1 < 0
    t_buf = jnp.where(in0, 0, 1).astype(I32)
    t_row = jnp.where(in0, jnp.minimum(pick(starts0 // MOE_TILE) + j0, xs0.shape[0] // MOE_TILE - 1),
                      jnp.clip(pick(starts1 // MOE_TILE) + j1, 0, xs1.shape[0] // MOE_TILE - 1)).astype(I32)
    t_valid = jnp.where(in0, pick(cnt0) - j0 * MOE_TILE, pick(cnt1) - j1 * MOE_TILE).astype(I32)

    wspec = lambda shape: pl.BlockSpec(shape, lambda e, *_: (e, 0, 0))
    grid_spec = pltpu.PrefetchScalarGridSpec(
        num_scalar_prefetch=5,
        grid=(n_experts,),
        in_specs=[
            pl.BlockSpec(memory_space=pl.ANY),
            pl.BlockSpec(memory_space=pl.ANY),
            wspec((1, d_model, hidden)),
            wspec((1, d_model, hidden)),
            wspec((1, hidden, d_model)),
        ],
        out_specs=[pl.BlockSpec(memory_space=pl.ANY), pl.BlockSpec(memory_space=pl.ANY)],
        scratch_shapes=[
            pltpu.VMEM((d_model, 2 * hidden), BF16),
            pltpu.VMEM((hidden, d_model), BF16),
            pltpu.VMEM((MOE_BUFFERS, MOE_TILE, half), U32),
            pltpu.VMEM((MOE_BUFFERS, MOE_TILE, half), U32),
            pltpu.SemaphoreType.DMA((MOE_BUFFERS,)),
            pltpu.SemaphoreType.DMA((MOE_BUFFERS,)),
        ],
    )
    return pl.pallas_call(
        _moe_kernel, grid_spec=grid_spec,
        out_shape=[jax.ShapeDtypeStruct(xs0.shape, U32), jax.ShapeDtypeStruct(xs1.shape, U32)], name="moe_experts",
        compiler_params=pltpu.CompilerParams(
            dimension_semantics=("arbitrary",), vmem_limit_bytes=VMEM_LIMIT),
    )(k_first, k_end[-1:].astype(I32), t_buf, t_row, t_valid, xs0, xs1, w_g, w_u, w_d)


def _combine_kernel(x1_ref, y0_ref, y1_ref, rw_ref, gf_ref, *rest):
    out_ref = rest[-1]
    tr = x1_ref.shape[0]
    w_rows = jnp.concatenate([rw_ref[0], jnp.zeros((LANES - rw_ref.shape[1], tr), F32)], axis=0)
    w_cols = w_rows.T
    w0, w1 = w_cols[:, 0:1], w_cols[:, 1:2]
    a_lo, a_hi = _unpack_bf16_pair(y0_ref[0])
    b_lo, b_hi = _unpack_bf16_pair(y1_ref[0])
    moe = jnp.concatenate([w0 * a_lo + w1 * b_lo, w0 * a_hi + w1 * b_hi], axis=-1)
    out_ref[...] = _rms(x1_ref[...] + moe, gf_ref[...])


def _combine_call(x1, rw, row0, n, yg, gf, out_rows, out_row0, prev_out=None):
    t, d_model = x1.shape
    tr = rw.shape[-1]
    half = yg.shape[-1]
    assert t % tr == 0 and row0 % tr == 0 and n % tr == 0 and rw.shape == (t // tr, EXPERTS_PER_GROUP, tr)
    assert yg.shape == (2, n, half) and out_row0 % tr == 0
    off = row0 // tr
    ooff = out_row0 // tr
    in_specs = [
        pl.BlockSpec((tr, d_model), lambda i: (off + i, 0)),
        pl.BlockSpec((1, tr, half), lambda i: (0, i, 0)),
        pl.BlockSpec((1, tr, half), lambda i: (1, i, 0)),
        pl.BlockSpec((1, EXPERTS_PER_GROUP, tr), lambda i: (off + i, 0, 0)),
        pl.BlockSpec((1, d_model), lambda i: (0, 0)),
    ]
    args = [x1, yg, yg, rw, gf]
    aliases = {}
    if prev_out is not None:
        in_specs.append(pl.BlockSpec(memory_space=pl.ANY))
        args.append(prev_out)
        aliases = {len(args) - 1: 0}
    return pl.pallas_call(
        _combine_kernel,
        grid=(n // tr,),
        in_specs=in_specs,
        out_specs=pl.BlockSpec((tr, d_model), lambda i: (ooff + i, 0)),
        out_shape=jax.ShapeDtypeStruct((out_rows, d_model), F32), name=f"combine_row{out_row0}_of{out_rows}",
        input_output_aliases=aliases,
        compiler_params=pltpu.CompilerParams(
            dimension_semantics=("arbitrary",), vmem_limit_bytes=VMEM_LIMIT),
    )(*args)


def _route(streams):
    tokens = [h2.shape[0] for h2, _, _ in streams]
    counts = [cnt[:, 0].astype(I32) for _, _, cnt in streams]
    total = sum(counts)
    padded = ((total + MOE_TILE - 1) // MOE_TILE) * MOE_TILE
    starts = (jnp.cumsum(padded) - padded).astype(I32)
    experts = jnp.arange(N_EXPERTS, dtype=I32)[None, :, None]
    pos, base = [], starts
    for (_, ri, _), t, cnt in zip(streams, tokens, counts):
        ri = jnp.moveaxis(ri, 2, 0).reshape(ri.shape[2], t)
        first_row = jnp.sum(jnp.where(ri[0:2, None, :] == experts, base[None, :, None], 0), axis=1)
        pos.append(ri[2:4] + first_row)
        base = base + cnt
    pos = jnp.concatenate(pos, axis=1)
    n_rows = ((2 * sum(tokens) + N_EXPERTS * (MOE_TILE - 1)) // MOE_TILE) * MOE_TILE
    xs_sorted, ready = _sc_dispatch([h2 for h2, _, _ in streams], pos[0], pos[1], n_rows)
    return (xs_sorted, starts, total), pos, ready


def _gather_tokens(ys_sorted, pos, t0, n):
    return _sc_gather(ys_sorted, pos[:, t0:t0 + n].reshape(2 * n)).reshape(2, n, ys_sorted.shape[-1])


def _one_layer(xp, xs, s_ret, c_pool, norm1_g, w_in, ret_norm_g, w_pool, pool_scale, w_out, norm2_g,
               w_rg, w_re, w_g, w_u, w_d, final_g, past_len):
    bp, seq, d_model = xp.shape
    bs, dseq, _ = xs.shape
    rw_width = ret_norm_g.shape[-1]
    pw = pool_scale.shape[-1]
    dh = rw_width // RET_HEADS
    half = d_model // 2

    w_r = jnp.concatenate(
        [w_re, w_rg, jnp.zeros((d_model, LANES - N_EXPERTS - N_EXPERT_GROUPS), F32)], axis=1)
    wr_hi = w_r.astype(BF16)
    wr = jnp.concatenate([wr_hi, (w_r - wr_hi.astype(F32)).astype(BF16)], axis=1)
    consts = dict(
        g1=norm1_g.reshape(1, d_model), w_in=w_in.astype(BF16), gret=ret_norm_g.reshape(1, rw_width),
        w_pool=w_pool.astype(BF16), pscale=pool_scale.reshape(1, pw), w_out=w_out.astype(BF16),
        g2=norm2_g.reshape(1, d_model), wr=wr)

    gf = final_g.reshape(1, d_model)

    ts = bs * dseq
    b_lead = bp - 1
    t_lead, t_rest = b_lead * seq, (bp - b_lead) * seq
    zeros = lambda nb: (jnp.zeros((nb, RET_HEADS, dh, dh), F32), jnp.zeros((nb, HIST_ROWS, pw), F32))
    h0s = jnp.pad(c_pool, ((0, 0), (HIST_ROWS - POOL_HIST, 0), (0, 0)))
    prompt_tile = dict(bb=1, tl=PROMPT_TILE, chunk=256)

    def stream(layer_out, t):
        x1, h2, ri, rw, st, hist, cnt = layer_out
        return dict(x1=x1.reshape(t, d_model), route=(h2.reshape(t, half), ri, cnt),
                    rw=rw.reshape(-1, EXPERTS_PER_GROUP, rw.shape[-1]), st=st, hist=hist)

    pa = stream(_layer_call(xp, 0, b_lead, *zeros(b_lead), 0, consts, gf, **prompt_tile), t_lead)
    group0, pos0, ready0 = _route([pa["route"]])
    pb = stream(_layer_call(xp, b_lead, bp - b_lead, *zeros(bp - b_lead), 0, consts, ready0, **prompt_tile),
                t_rest)
    sm = stream(_layer_call(xs, 0, bs, s_ret, h0s, past_len, consts, pb["route"][2],
                            bb=bs, tl=dseq, chunk=min(64, dseq)), ts)
    group1, pos1, _ = _route([pb["route"], sm["route"]])
    ys0, ys1 = _moe_call([group0, group1], w_g, w_u, w_d)

    tp = bp * seq
    yp = _combine_call(pb["x1"], pb["rw"], 0, t_rest, _gather_tokens(ys1, pos1, 0, t_rest), gf, tp, t_lead)
    ysm = _combine_call(sm["x1"], sm["rw"], 0, ts, _gather_tokens(ys1, pos1, t_rest, ts), gf, ts, 0)
    row0 = 0
    for nb in COMBINE_CHUNKS:
        n = min(nb * seq, t_lead - row0)
        if n > 0:
            yp = _combine_call(pa["x1"], pa["rw"], row0, n, _gather_tokens(ys0, pos0, row0, n), gf, tp, row0,
                               prev_out=yp)
            row0 += n
    assert row0 == t_lead
    st_p = jnp.concatenate([pa["st"], pb["st"]], axis=0)
    hist_p = jnp.concatenate([pa["hist"], pb["hist"]], axis=0)
    return (yp.reshape(bp, seq, d_model), ysm.reshape(bs, dseq, d_model),
            st_p, hist_p[:, HIST_ROWS - POOL_HIST:], sm["st"], sm["hist"][:, HIST_ROWS - POOL_HIST:])


def kernel(x_prompt, x_sample, state_ret, cache_pool, norm1_g, w_in, ret_norm_g, w_pool, pool_scale, w_out,
           norm2_g, w_router_group, w_router_expert, w_exp_gate, w_exp_up, w_exp_down, final_norm_g):
    depth = w_in.shape[0]
    assert depth == 1, "the final RMSNorm is fused into the layer's combine kernel"
    assert x_prompt.shape[0] >= 2 and x_prompt.shape[1] % PROMPT_TILE == 0
    yp, ys, s_p, h_p, s_s, h_s = _one_layer(
        x_prompt, x_sample, state_ret[0], cache_pool[0], norm1_g[0], w_in[0], ret_norm_g[0], w_pool[0],
        pool_scale[0], w_out[0], norm2_g[0], w_router_group[0], w_router_expert[0],
        w_exp_gate[0], w_exp_up[0], w_exp_down[0], final_norm_g, PAST_LEN)
    return (yp, ys, s_p[None], h_p[None], s_s[None], h_s[None])
```

```python
import functools

import jax
import jax.numpy as jnp
from jax import lax
from jax.experimental import pallas as pl
from jax.experimental.pallas import tpu as pltpu
from jax.experimental.pallas import tpu_sc as plsc

F32 = jnp.float32
BF16 = jnp.bfloat16
I32 = jnp.int32
U32 = jnp.uint32

EPS = 1e-6
ROPE_BASE = 10000.0
RET_HEADS = 4
POOL_WINDOWS = (2, 4, 8, 16)
POOL_HIST = max(POOL_WINDOWS) - 1
N_EXPERT_GROUPS = 4
EXPERTS_PER_GROUP = 8
N_EXPERTS = N_EXPERT_GROUPS * EXPERTS_PER_GROUP
EXPERT_SHIFT = EXPERTS_PER_GROUP.bit_length() - 1
PAST_LEN = 1024

LANES = 128
HIST_ROWS = 16
MOE_TILE = 256
MOE_BUFFERS = 4
MOE_LOOKAHEAD = MOE_BUFFERS - 1
PROMPT_TILE = 1024
COMBINE_CHUNKS = (1, 2)
SC_UNIT = 32
VMEM_LIMIT = 56 * 1024 * 1024


def _nbytes(*arrays):
    return sum(a.size * a.dtype.itemsize for a in arrays)


def _rms(x, g):
    return x * lax.rsqrt(jnp.mean(x * x, axis=-1, keepdims=True) + EPS) * g


def _sigmoid(x):
    return 1.0 / (1.0 + jnp.exp(-x))


def _pack_bf16_pair(lo, hi):
    lo_b = lax.bitcast_convert_type(lo.astype(BF16).astype(F32), U32)
    hi_b = lax.bitcast_convert_type(hi.astype(BF16).astype(F32), U32)
    return hi_b | (lo_b >> 16)


def _unpack_bf16_pair(p):
    lo = lax.bitcast_convert_type(p << 16, F32)
    hi = lax.bitcast_convert_type(p & jnp.uint32(0xFFFF0000), F32)
    return lo, hi


def _layer_kernel(dc_ref, x_ref, s0_ref, h0_ref, rb_ref, rc_ref, rs_ref, rcs_ref, rss_ref,
                  dintra_ref, dq_ref, dk_ref,
                  g1_ref, win_ref, gret_ref, wpool_ref, pscale_ref, wout_ref, g2_ref,
                  wr_ref, tri_ref, after_ref,
                  x1_ref, h2_ref, ri_ref, rw_ref, st_ref, hist_ref, cnt_ref,
                  ue_ref, q_ref, k_ref, v_ref, o_ref, a_ref,
                  *, bb, tl, chunk, pos0):
    b_idx = pl.program_id(0)
    l_idx = pl.program_id(1)
    rows = bb * tl
    d_model = x_ref.shape[-1]
    rw_width = q_ref.shape[-1]
    dh = rw_width // RET_HEADS
    pw = ue_ref.shape[-1]
    gw = pw // len(POOL_WINDOWS)
    n_chunks = tl // chunk

    @pl.when(l_idx == 0)
    def _():
        st_ref[...] = s0_ref[...]
        ue_ref[:, 0:HIST_ROWS, :] = h0_ref[...]

    @pl.when((l_idx == 0) & (b_idx == 0))
    def _():
        cnt_ref[...] = jnp.zeros_like(cnt_ref)

    x = x_ref[...].reshape(rows, d_model)
    hb = _rms(x, g1_ref[...]).astype(BF16)

    def project(c0, c1):
        return jnp.dot(hb, win_ref[:, c0:c1], preferred_element_type=F32)

    proj = project(0, 2 * rw_width)

    cos_b = rb_ref[0, 0:1, :]
    sin_b = rb_ref[0, 1:2, :]
    cosf = (cos_b * rc_ref[...] - sin_b * rs_ref[...])[None]
    sinf = (sin_b * rcs_ref[...] + cos_b * rss_ref[...])[None]
    k_scale = dh ** -0.5
    for hh in range(RET_HEADS):
        qh = proj[:, hh * dh:(hh + 1) * dh]
        kh = proj[:, rw_width + hh * dh:rw_width + (hh + 1) * dh]
        qr = (qh.reshape(bb, tl, dh) * cosf
              + pltpu.roll(qh, dh // 2, 1).reshape(bb, tl, dh) * sinf).reshape(rows, dh)
        kr = (kh.reshape(bb, tl, dh) * cosf
              + pltpu.roll(kh, dh // 2, 1).reshape(bb, tl, dh) * sinf).reshape(rows, dh)
        q_ref[:, hh * dh:(hh + 1) * dh] = qr.astype(BF16)
        k_ref[:, hh * dh:(hh + 1) * dh] = kr * k_scale
    v_ref[...] = project(2 * rw_width, 3 * rw_width).astype(BF16)
    gate = project(3 * rw_width, 4 * rw_width)
    u = project(4 * rw_width, 4 * rw_width + pw)

    def ret_block(b, c):
        r0 = b * tl + c * chunk
        if not isinstance(r0, int):
            r0 = pl.multiple_of(r0, chunk)
        for hh in range(RET_HEADS):
            cs = slice(hh * dh, (hh + 1) * dh)
            qc = q_ref[pl.ds(r0, chunk), cs]
            kf = k_ref[pl.ds(r0, chunk), cs]
            vc = v_ref[pl.ds(r0, chunk), cs]
            s_old = st_ref[b, hh]
            sc = lax.dot_general(qc, kf.astype(BF16), (((1,), (1,)), ((), ())),
                                 preferred_element_type=F32) * dintra_ref[hh]
            o = (jnp.dot(sc.astype(BF16), vc, preferred_element_type=F32)
                 + dq_ref[hh] * jnp.dot(qc, s_old.astype(BF16), preferred_element_type=F32))
            kd = (kf * dk_ref[hh]).astype(BF16)
            s_new = dc_ref[hh] * s_old + lax.dot_general(
                kd, vc, (((0,), (0,)), ((), ())), preferred_element_type=F32)
            st_ref[b, hh] = s_new
            o_ref[pl.ds(r0, chunk), cs] = o

    if bb * n_chunks <= 4:
        for b in range(bb):
            for c in range(n_chunks):
                ret_block(b, c)
    else:
        def body(i, carry):
            ret_block(i // n_chunks, i % n_chunks)
            return carry
        lax.fori_loop(0, bb * n_chunks, body, 0)

    for hh in range(RET_HEADS):
        cs = slice(hh * dh, (hh + 1) * dh)
        oh = o_ref[:, cs]
        mu = jnp.mean(oh, axis=-1, keepdims=True)
        oc = oh - mu
        var = jnp.mean(oc * oc, axis=-1, keepdims=True)
        y = oc * lax.rsqrt(var + EPS) * gret_ref[:, cs]
        g = gate[:, cs]
        a_ref[:, cs] = (g * _sigmoid(g) * y).astype(BF16)

    ue_ref[:, HIST_ROWS:HIST_ROWS + tl, :] = u.reshape(bb, tl, pw)
    pos = pos0 + l_idx * tl + lax.broadcasted_iota(I32, (1, tl, 1), 1)
    for gi, w in enumerate(POOL_WINDOWS):
        cs = slice(gi * gw, (gi + 1) * gw)
        acc = ue_ref[:, HIST_ROWS:HIST_ROWS + tl, cs]
        for j in range(1, w):
            acc = acc + ue_ref[:, HIST_ROWS - j:HIST_ROWS - j + tl, cs]
        inv_cnt = 1.0 / jnp.minimum(pos + 1, w).astype(F32)
        p = (acc * inv_cnt).reshape(rows, gw) - u[:, cs]
        z = jnp.dot(p.astype(BF16), wpool_ref[gi], preferred_element_type=F32) * pscale_ref[:, cs]
        a_ref[:, rw_width + gi * gw:rw_width + (gi + 1) * gw] = z.astype(BF16)
    tail = ue_ref[:, tl:tl + HIST_ROWS, :]
    ue_ref[:, 0:HIST_ROWS, :] = tail
    hist_ref[...] = tail

    x1 = x + jnp.dot(a_ref[...], wout_ref[...], preferred_element_type=F32)
    x1_ref[...] = x1.reshape(bb, tl, d_model)
    h2 = _rms(x1, g2_ref[...])
    h2_ref[...] = _pack_bf16_pair(h2[:, 0:d_model // 2], h2[:, d_model // 2:]).reshape(bb, tl, d_model // 2)

    h2_hi = h2.astype(BF16)
    h2_lo = (h2 - h2_hi.astype(F32)).astype(BF16)
    two = jnp.dot(h2_hi, wr_ref[...], preferred_element_type=F32)
    logits = (two[:, 0:LANES] + two[:, LANES:2 * LANES]
              + jnp.dot(h2_lo, wr_ref[:, 0:LANES], preferred_element_type=F32))
    lt = logits.T
    neg = jnp.float32(-jnp.inf)
    big = jnp.float32(1e9)
    sub = lax.broadcasted_iota(I32, (EXPERTS_PER_GROUP, rows), 0).astype(F32)
    gl = jnp.where(sub < N_EXPERT_GROUPS, lt[N_EXPERTS:N_EXPERTS + EXPERTS_PER_GROUP], neg)
    gmax = jnp.max(gl, axis=0, keepdims=True)
    gidx = jnp.min(jnp.where(gl == gmax, sub, big), axis=0, keepdims=True)
    p_sel = 1.0 / jnp.sum(jnp.exp(gl - gmax), axis=0, keepdims=True)
    el = lt[0:EXPERTS_PER_GROUP]
    for g in range(1, N_EXPERT_GROUPS):
        el = jnp.where(gidx == g, lt[g * EXPERTS_PER_GROUP:(g + 1) * EXPERTS_PER_GROUP], el)
    m1 = jnp.max(el, axis=0, keepdims=True)
    t1 = jnp.min(jnp.where(el == m1, sub, big), axis=0, keepdims=True)
    el2 = jnp.where(sub == t1, neg, el)
    m2 = jnp.max(el2, axis=0, keepdims=True)
    t2 = jnp.min(jnp.where(el2 == m2, sub, big), axis=0, keepdims=True)
    e2 = jnp.exp(m2 - m1)
    w1 = p_sel / (1.0 + e2)
    w2 = p_sel * e2 / (1.0 + e2)
    i1 = gidx * EXPERTS_PER_GROUP + t1
    i2 = gidx * EXPERTS_PER_GROUP + t2

    eid = lax.broadcasted_iota(I32, (N_EXPERTS, rows), 0).astype(F32)
    hit1 = eid == i1
    hit2 = eid == i2
    onehot = (hit1 | hit2).astype(BF16)
    before = jnp.dot(onehot, tri_ref[...], preferred_element_type=F32) + cnt_ref[...]
    r1 = jnp.sum(jnp.where(hit1, before, 0.0), axis=0, keepdims=True)
    r2 = jnp.sum(jnp.where(hit2, before, 0.0), axis=0, keepdims=True)
    cnt_ref[...] = cnt_ref[...] + jnp.sum(onehot.astype(F32), axis=1, keepdims=True)

    ri = jnp.where(sub == 0, i1, jnp.where(sub == 1, i2, jnp.where(sub == 2, r1, jnp.where(sub == 3, r2, 0.0))))
    ri_ref[...] = ri.astype(I32).reshape(ri_ref.shape)
    rw_ref[...] = jnp.where(sub == 0, w1, jnp.where(sub == 1, w2, 0.0)).reshape(rw_ref.shape)


def _rope_tables(pos0, seq, tl, dh):
    half = dh // 2
    inv = ROPE_BASE ** (-jnp.arange(half, dtype=F32) / half)
    ang_t = jnp.arange(tl, dtype=F32)[:, None] * inv[None, :]
    ang_b = (pos0 + tl * jnp.arange(seq // tl)).astype(F32)[:, None] * inv[None, :]
    dup = lambda a: jnp.concatenate([a, a], axis=-1)
    sgn = lambda a: jnp.concatenate([-a, a], axis=-1)
    base = jnp.stack([dup(jnp.cos(ang_b)), dup(jnp.sin(ang_b))], axis=1)
    base = jnp.pad(base, ((0, 0), (0, 8 - base.shape[1]), (0, 0)))
    cos_t, sin_t = jnp.cos(ang_t), jnp.sin(ang_t)
    return base, dup(cos_t), dup(sin_t), sgn(cos_t), sgn(sin_t)


def _layer_call(x, b0, nb, s0, h0, pos0, consts, after, *, bb, tl, chunk):
    _, seq, d_model = x.shape
    bsz = nb
    blk0 = b0 // bb
    rows = bb * tl
    rw_width = consts["gret"].shape[-1]
    pw = consts["pscale"].shape[-1]
    dh = rw_width // RET_HEADS

    rope = _rope_tables(pos0, seq, tl, dh)

    lg = jnp.log1p(-jnp.exp2(-5.0 - jnp.arange(RET_HEADS, dtype=F32)))
    idx = jnp.arange(chunk, dtype=F32)
    diff = idx[:, None] - idx[None, :]
    d_intra = jnp.where(diff[None] >= 0, jnp.exp(jnp.maximum(diff, 0.0)[None] * lg[:, None, None]), 0.0)
    d_q = jnp.broadcast_to(jnp.exp((idx + 1.0)[None, :] * lg[:, None])[:, :, None], (RET_HEADS, chunk, dh))
    d_k = jnp.broadcast_to(jnp.exp((chunk - 1.0 - idx)[None, :] * lg[:, None])[:, :, None], (RET_HEADS, chunk, dh))
    d_c = jnp.exp(chunk * lg)
    tri = jnp.triu(jnp.ones((rows, rows), BF16), 1)

    const2 = lambda b, l, *_: (0, 0)
    const3 = lambda b, l, *_: (0, 0, 0)
    grid_spec = pltpu.PrefetchScalarGridSpec(
        num_scalar_prefetch=0,
        grid=(bsz // bb, seq // tl),
        in_specs=[
            pl.BlockSpec(memory_space=pltpu.SMEM),
            pl.BlockSpec((bb, tl, d_model), lambda b, l: (blk0 + b, l, 0)),
            pl.BlockSpec((bb, RET_HEADS, dh, dh), lambda b, l: (b, 0, 0, 0)),
            pl.BlockSpec((bb, HIST_ROWS, pw), lambda b, l: (b, 0, 0)),
            pl.BlockSpec((1, 8, dh), lambda b, l: (l, 0, 0)),
            pl.BlockSpec((tl, dh), const2),
            pl.BlockSpec((tl, dh), const2),
            pl.BlockSpec((tl, dh), const2),
            pl.BlockSpec((tl, dh), const2),
            pl.BlockSpec((RET_HEADS, chunk, chunk), const3),
            pl.BlockSpec((RET_HEADS, chunk, dh), const3),
            pl.BlockSpec((RET_HEADS, chunk, dh), const3),
            pl.BlockSpec((1, d_model), const2),
            pl.BlockSpec(consts["w_in"].shape, const2),
            pl.BlockSpec((1, rw_width), const2),
            pl.BlockSpec(consts["w_pool"].shape, const3),
            pl.BlockSpec((1, pw), const2),
            pl.BlockSpec(consts["w_out"].shape, const2),
            pl.BlockSpec((1, d_model), const2),
            pl.BlockSpec((d_model, 2 * LANES), const2),
            pl.BlockSpec((rows, rows), const2),
            pl.BlockSpec(memory_space=pl.ANY),
        ],
        out_specs=[
            pl.BlockSpec((bb, tl, d_model), lambda b, l: (b, l, 0)),
            pl.BlockSpec((bb, tl, d_model // 2), lambda b, l: (b, l, 0)),
            pl.BlockSpec((1, 1, EXPERTS_PER_GROUP, rows), lambda b, l: (b, l, 0, 0)),
            pl.BlockSpec((1, 1, EXPERTS_PER_GROUP, rows), lambda b, l: (b, l, 0, 0)),
            pl.BlockSpec((bb, RET_HEADS, dh, dh), lambda b, l: (b, 0, 0, 0)),
            pl.BlockSpec((bb, HIST_ROWS, pw), lambda b, l: (b, 0, 0)),
            pl.BlockSpec((N_EXPERTS, rows), const2),
        ],
        scratch_shapes=[
            pltpu.VMEM((bb, HIST_ROWS + tl, pw), F32),
            pltpu.VMEM((rows, rw_width), BF16),
            pltpu.VMEM((rows, rw_width), F32),
            pltpu.VMEM((rows, rw_width), BF16),
            pltpu.VMEM((rows, rw_width), F32),
            pltpu.VMEM((rows, d_model), BF16),
        ],
    )
    out_shape = [
        jax.ShapeDtypeStruct((bsz, seq, d_model), F32),
        jax.ShapeDtypeStruct((bsz, seq, d_model // 2), U32),
        jax.ShapeDtypeStruct((bsz // bb, seq // tl, EXPERTS_PER_GROUP, rows), I32),
        jax.ShapeDtypeStruct((bsz // bb, seq // tl, EXPERTS_PER_GROUP, rows), F32),
        jax.ShapeDtypeStruct((bsz, RET_HEADS, dh, dh), F32),
        jax.ShapeDtypeStruct((bsz, HIST_ROWS, pw), F32),
        jax.ShapeDtypeStruct((N_EXPERTS, rows), F32),
    ]
    kern = functools.partial(_layer_kernel, bb=bb, tl=tl, chunk=chunk, pos0=pos0)
    operands = (d_c, x, s0, h0, *rope, d_intra, d_q, d_k,
                consts["g1"], consts["w_in"], consts["gret"], consts["w_pool"], consts["pscale"],
                consts["w_out"], consts["g2"], consts["wr"], tri, after)
    n_tok = bsz * seq
    mm_flops_per_token = 2 * (d_model * consts["w_in"].shape[1] + d_model * d_model + 2 * d_model * LANES + pw * pw // 4
                              + rw_width * (2 * chunk + 2 * dh) + N_EXPERTS * rows)
    cost = pl.CostEstimate(
        flops=n_tok * mm_flops_per_token, transcendentals=n_tok * (rw_width + 2 * N_EXPERT_GROUPS),
        bytes_accessed=_nbytes(*operands) - _nbytes(x, after) + n_tok * d_model * 4 + _nbytes(*out_shape))
    return pl.pallas_call(
        kern, grid_spec=grid_spec, out_shape=out_shape, name=f"layer_pos{pos0}_b{b0}", cost_estimate=cost,
        compiler_params=pltpu.CompilerParams(
            dimension_semantics=("arbitrary", "arbitrary"), vmem_limit_bytes=VMEM_LIMIT),
    )(*operands)


def _sc_partition(n_units):
    info = plsc.get_sparse_core_info()
    nc, nw = info.num_cores, info.num_cores * info.num_subcores
    upw = -(-n_units // nw)
    upw += upw % 2
    return nc, nw, upw


def _units_by_worker(idx, n_units, upw, nw):
    idx = jnp.pad(idx.reshape(n_units, SC_UNIT), ((0, nw * upw - n_units), (0, 0)))
    return idx.reshape(upw, nw, SC_UNIT).transpose(1, 0, 2)


def _sc_dispatch(srcs, idx0, idx1, n_out_rows, after=None):
    assert 1 <= len(srcs) <= 2
    d = srcs[0].shape[1]
    dtype = srcs[0].dtype
    assert all(src.shape[0] % SC_UNIT == 0 for src in srcs)
    units_a = srcs[0].shape[0] // SC_UNIT
    n_units = sum(src.shape[0] for src in srcs) // SC_UNIT
    nc, nw, upw = _sc_partition(n_units)
    idx0 = _units_by_worker(idx0, n_units, upw, nw)
    idx1 = _units_by_worker(idx1, n_units, upw, nw)
    mesh = plsc.VectorSubcoreMesh(core_axis_name="c", subcore_axis_name="s")
    dma = pltpu.SemaphoreType.DMA
    extra = [] if after is None else [after]

    moved = n_units * SC_UNIT * d * jnp.dtype(dtype).itemsize
    @functools.partial(
        pl.kernel, mesh=mesh,
        cost_estimate=pl.CostEstimate(flops=0, transcendentals=0, bytes_accessed=3 * moved + _nbytes(idx0, idx1)),
        out_type=jax.ShapeDtypeStruct((n_out_rows, d), dtype),
        scratch_types=[
            pltpu.VMEM((upw, SC_UNIT), I32),
            pltpu.VMEM((upw, SC_UNIT), I32),
            pltpu.VMEM((SC_UNIT, d), dtype),
            pltpu.VMEM((SC_UNIT, d), dtype),
            dma, dma, dma, dma, dma, dma,
        ],
    )
    def k(*refs):
        src_hbm = refs[:len(srcs)]
        i0_hbm, i1_hbm, out_hbm, i0_v, i1_v, rows0, rows1, l0, l1, p0, p1, q0, q1 = refs[len(srcs) + len(extra):]
        wid = lax.axis_index("s") * nc + lax.axis_index("c")
        pltpu.sync_copy(i0_hbm.at[wid], i0_v)
        pltpu.sync_copy(i1_hbm.at[wid], i1_v)
        rows, lsem, psem, qsem = (rows0, rows1), (l0, l1), (p0, p1), (q0, q1)

        def live(j):
            return j * nw + wid < n_units

        def load(j, b, op):
            unit = j * nw + wid

            @pl.when(live(j) & (unit < units_a))
            def _():
                op(pltpu.make_async_copy(
                    src_hbm[0].at[pl.ds(pl.multiple_of(unit * SC_UNIT, 8), SC_UNIT)], rows[b], lsem[b]))

            if len(srcs) == 2:
                @pl.when(live(j) & (unit >= units_a))
                def _():
                    op(pltpu.make_async_copy(
                        src_hbm[1].at[pl.ds(pl.multiple_of((unit - units_a) * SC_UNIT, 8), SC_UNIT)],
                        rows[b], lsem[b]))

        def scatter(j, b, op):
            @pl.when(live(j))
            def _():
                op(pltpu.make_async_copy(rows[b], out_hbm.at[i0_v.at[j]], psem[b]))
                op(pltpu.make_async_copy(rows[b], out_hbm.at[i1_v.at[j]], qsem[b]))

        start = lambda c: c.start()
        wait = lambda c: c.wait()
        load(0, 0, start)

        @pl.loop(0, upw, step=2)
        def _(j):
            @pl.when(j > 0)
            def _():
                scatter(j - 1, 1, wait)
            load(j + 1, 1, start)
            load(j, 0, wait)
            scatter(j, 0, start)
            scatter(j, 0, wait)

            @pl.when(j + 2 < upw)
            def _():
                load(j + 2, 0, start)
            load(j + 1, 1, wait)
            scatter(j + 1, 1, start)

        scatter(upw - 1, 1, wait)

    return k(*srcs, *extra, idx0, idx1), idx1


def _sc_gather(table, idx):
    n = idx.shape[0]
    d = table.shape[1]
    assert n % SC_UNIT == 0
    n_units = n // SC_UNIT
    nc, nw, upw = _sc_partition(n_units)
    idx = _units_by_worker(idx, n_units, upw, nw)
    mesh = plsc.VectorSubcoreMesh(core_axis_name="c", subcore_axis_name="s")
    dma = pltpu.SemaphoreType.DMA

    @functools.partial(
        pl.kernel, mesh=mesh,
        cost_estimate=pl.CostEstimate(flops=0, transcendentals=0,
                                      bytes_accessed=2 * n * d * table.dtype.itemsize + _nbytes(idx)),
        out_type=jax.ShapeDtypeStruct((n, d), table.dtype),
        scratch_types=[
            pltpu.VMEM((upw, SC_UNIT), I32),
            pltpu.VMEM((SC_UNIT, d), table.dtype),
            pltpu.VMEM((SC_UNIT, d), table.dtype),
            dma, dma, dma, dma,
        ],
    )
    def k(t_hbm, i_hbm, out_hbm, i_v, rows0, rows1, g0, g1, w0, w1):
        wid = lax.axis_index("s") * nc + lax.axis_index("c")
        pltpu.sync_copy(i_hbm.at[wid], i_v)
        rows, gsem, wsem = (rows0, rows1), (g0, g1), (w0, w1)

        def live(j):
            return j * nw + wid < n_units

        def gather(j, b, op):
            @pl.when(live(j))
            def _():
                op(pltpu.make_async_copy(t_hbm.at[i_v.at[j]], rows[b], gsem[b]))

        def write(j, b, op):
            @pl.when(live(j))
            def _():
                op(pltpu.make_async_copy(
                    rows[b], out_hbm.at[pl.ds(pl.multiple_of((j * nw + wid) * SC_UNIT, 8), SC_UNIT)], wsem[b]))

        start = lambda c: c.start()
        wait = lambda c: c.wait()
        gather(0, 0, start)

        @pl.loop(0, upw, step=2)
        def _(j):
            @pl.when(j > 0)
            def _():
                write(j - 1, 1, wait)
            gather(j + 1, 1, start)
            gather(j, 0, wait)
            write(j, 0, start)
            write(j, 0, wait)

            @pl.when(j + 2 < upw)
            def _():
                gather(j + 2, 0, start)
            gather(j + 1, 1, wait)
            write(j + 1, 1, start)

        write(upw - 1, 1, wait)

    return k(table, idx)


def _moe_kernel(kfirst_ref, ktot_ref, tbuf_ref, trow_ref, tvalid_ref, xs0_hbm, xs1_hbm, wg_ref, wu_ref, wd_ref,
                ys0_hbm, ys1_hbm, wgu_s, wd_s, xbuf, ybuf, sem_in, sem_out):
    e = pl.program_id(0)
    k_first = kfirst_ref[e]
    n_tiles = kfirst_ref[e + 1] - k_first
    k_total = ktot_ref[0]
    hidden = wd_s.shape[0]
    half = xbuf.shape[-1]

    def rows_of(k):
        return pl.ds(pl.multiple_of(trow_ref[k] * MOE_TILE, MOE_TILE), MOE_TILE)

    def start_in(k):
        slot = k % MOE_BUFFERS
        for buf, xs_hbm in enumerate((xs0_hbm, xs1_hbm)):
            @pl.when(tbuf_ref[k] == buf)
            def _():
                pltpu.make_async_copy(xs_hbm.at[rows_of(k)], xbuf.at[slot], sem_in.at[slot]).start()

    def wait_in(k):
        slot = k % MOE_BUFFERS
        pltpu.make_async_copy(xs0_hbm.at[pl.ds(0, MOE_TILE)], xbuf.at[slot], sem_in.at[slot]).wait()

    def start_out(k):
        slot = k % MOE_BUFFERS
        for buf, ys_hbm in enumerate((ys0_hbm, ys1_hbm)):
            @pl.when(tbuf_ref[k] == buf)
            def _():
                pltpu.make_async_copy(ybuf.at[slot], ys_hbm.at[rows_of(k)], sem_out.at[slot]).start()

    def wait_out(k):
        slot = k % MOE_BUFFERS
        pltpu.make_async_copy(ybuf.at[slot], ys0_hbm.at[pl.ds(0, MOE_TILE)], sem_out.at[slot]).wait()

    @pl.when(e == 0)
    def _():
        for k in range(MOE_LOOKAHEAD):
            @pl.when(k < k_total)
            def _():
                start_in(k)

    @pl.when(n_tiles > 0)
    def _():
        wgu_s[:, 0:hidden] = wg_ref[0].astype(BF16)
        wgu_s[:, hidden:2 * hidden] = wu_ref[0].astype(BF16)
        wd_s[...] = wd_ref[0].astype(BF16)

    def tile(t, carry):
        k = k_first + t

        @pl.when(k + MOE_LOOKAHEAD < k_total)
        def _():
            start_in(k + MOE_LOOKAHEAD)

        wait_in(k)

        @pl.when(k >= MOE_BUFFERS)
        def _():
            wait_out(k - MOE_BUFFERS)

        slot = k % MOE_BUFFERS
        row = lax.broadcasted_iota(I32, (MOE_TILE, half), 0)
        x_lo, x_hi = _unpack_bf16_pair(jnp.where(row < tvalid_ref[k], xbuf[slot], jnp.uint32(0)))
        ab = (jnp.dot(x_lo.astype(BF16), wgu_s[0:half, :], preferred_element_type=F32)
              + jnp.dot(x_hi.astype(BF16), wgu_s[half:2 * half, :], preferred_element_type=F32))
        a = ab[:, 0:hidden]
        he = a * _sigmoid(a) * ab[:, hidden:2 * hidden]
        y = jnp.dot(he.astype(BF16), wd_s[...], preferred_element_type=F32)
        ybuf[slot] = _pack_bf16_pair(y[:, 0:half], y[:, half:2 * half])
        start_out(k)
        return carry

    lax.fori_loop(0, n_tiles, tile, 0)

    @pl.when(e == pl.num_programs(0) - 1)
    def _():
        for j in range(1, MOE_BUFFERS + 1):
            @pl.when(k_total >= j)
            def _():
                wait_out(k_total - j)


def _moe_call(groups, w_g, w_u, w_d):
    (xs0, starts0, cnt0), (xs1, starts1, cnt1) = groups
    half = xs0.shape[1]
    n_experts, d_model, hidden = w_g.shape
    tiles_of = lambda cnt: (cnt + MOE_TILE - 1) // MOE_TILE
    n0, n1 = tiles_of(cnt0), tiles_of(cnt1)
    k_end = jnp.cumsum(n0 + n1)
    k_first = jnp.concatenate([jnp.zeros((1,), I32), k_end]).astype(I32)
    k_max = (xs0.shape[0] + xs1.shape[0]) // MOE_TILE
    k = jnp.arange(k_max, dtype=I32)
    e_of_k = jnp.minimum(jnp.sum(k[:, None] >= k_end[None, :], axis=1), n_experts - 1)
    onehot = e_of_k[:, None] == jnp.arange(n_experts, dtype=I32)[None, :]
    pick = lambda v: jnp.sum(jnp.where(onehot, v[None, :], 0), axis=1)
    j0 = k - pick(k_first[:-1])
    j1 = j0 - pick(n0)
    in0 = j1 < 0
    t_buf = jnp.where(in0, 0, 1).astype(I32)
    t_row = jnp.where(in0, jnp.minimum(pick(starts0 // MOE_TILE) + j0, xs0.shape[0] // MOE_TILE - 1),
                      jnp.clip(pick(starts1 // MOE_TILE) + j1, 0, xs1.shape[0] // MOE_TILE - 1)).astype(I32)
    t_valid = jnp.where(in0, pick(cnt0) - j0 * MOE_TILE, pick(cnt1) - j1 * MOE_TILE).astype(I32)

    wspec = lambda shape: pl.BlockSpec(shape, lambda e, *_: (e, 0, 0))
    grid_spec = pltpu.PrefetchScalarGridSpec(
        num_scalar_prefetch=5,
        grid=(n_experts,),
        in_specs=[
            pl.BlockSpec(memory_space=pl.ANY),
            pl.BlockSpec(memory_space=pl.ANY),
            wspec((1, d_model, hidden)),
            wspec((1, d_model, hidden)),
            wspec((1, hidden, d_model)),
        ],
        out_specs=[pl.BlockSpec(memory_space=pl.ANY), pl.BlockSpec(memory_space=pl.ANY)],
        scratch_shapes=[
            pltpu.VMEM((d_model, 2 * hidden), BF16),
            pltpu.VMEM((hidden, d_model), BF16),
            pltpu.VMEM((MOE_BUFFERS, MOE_TILE, half), U32),
            pltpu.VMEM((MOE_BUFFERS, MOE_TILE, half), U32),
            pltpu.SemaphoreType.DMA((MOE_BUFFERS,)),
            pltpu.SemaphoreType.DMA((MOE_BUFFERS,)),
        ],
    )
    n_rows = xs0.shape[0] + xs1.shape[0]
    cost = pl.CostEstimate(flops=n_rows * 6 * d_model * hidden, transcendentals=n_rows * hidden,
                           bytes_accessed=2 * _nbytes(xs0, xs1) + _nbytes(w_g, w_u, w_d))
    return pl.pallas_call(
        _moe_kernel, grid_spec=grid_spec, cost_estimate=cost,
        out_shape=[jax.ShapeDtypeStruct(xs0.shape, U32), jax.ShapeDtypeStruct(xs1.shape, U32)], name="moe_experts",
        compiler_params=pltpu.CompilerParams(
            dimension_semantics=("arbitrary",), vmem_limit_bytes=VMEM_LIMIT),
    )(k_first, k_end[-1:].astype(I32), t_buf, t_row, t_valid, xs0, xs1, w_g, w_u, w_d)


def _combine_kernel(x1_ref, y0_ref, y1_ref, rw_ref, gf_ref, *rest):
    out_ref = rest[-1]
    tr = x1_ref.shape[0]
    w_rows = jnp.concatenate([rw_ref[0], jnp.zeros((LANES - rw_ref.shape[1], tr), F32)], axis=0)
    w_cols = w_rows.T
    w0, w1 = w_cols[:, 0:1], w_cols[:, 1:2]
    a_lo, a_hi = _unpack_bf16_pair(y0_ref[0])
    b_lo, b_hi = _unpack_bf16_pair(y1_ref[0])
    moe = jnp.concatenate([w0 * a_lo + w1 * b_lo, w0 * a_hi + w1 * b_hi], axis=-1)
    out_ref[...] = _rms(x1_ref[...] + moe, gf_ref[...])


def _combine_call(x1, rw, row0, n, yg, gf, out_rows, out_row0, prev_out=None):
    t, d_model = x1.shape
    tr = rw.shape[-1]
    half = yg.shape[-1]
    assert t % tr == 0 and row0 % tr == 0 and n % tr == 0 and rw.shape == (t // tr, EXPERTS_PER_GROUP, tr)
    assert yg.shape == (2, n, half) and out_row0 % tr == 0
    off = row0 // tr
    ooff = out_row0 // tr
    in_specs = [
        pl.BlockSpec((tr, d_model), lambda i: (off + i, 0)),
        pl.BlockSpec((1, tr, half), lambda i: (0, i, 0)),
        pl.BlockSpec((1, tr, half), lambda i: (1, i, 0)),
        pl.BlockSpec((1, EXPERTS_PER_GROUP, tr), lambda i: (off + i, 0, 0)),
        pl.BlockSpec((1, d_model), lambda i: (0, 0)),
    ]
    args = [x1, yg, yg, rw, gf]
    aliases = {}
    if prev_out is not None:
        in_specs.append(pl.BlockSpec(memory_space=pl.ANY))
        args.append(prev_out)
        aliases = {len(args) - 1: 0}
    return pl.pallas_call(
        _combine_kernel,
        grid=(n // tr,),
        in_specs=in_specs,
        out_specs=pl.BlockSpec((tr, d_model), lambda i: (ooff + i, 0)),
        out_shape=jax.ShapeDtypeStruct((out_rows, d_model), F32), name=f"combine_row{out_row0}_of{out_rows}",
        cost_estimate=pl.CostEstimate(flops=8 * n * d_model, transcendentals=n,
                                      bytes_accessed=2 * n * d_model * 4 + _nbytes(yg) + n * 4 * EXPERTS_PER_GROUP),
        input_output_aliases=aliases,
        compiler_params=pltpu.CompilerParams(
            dimension_semantics=("arbitrary",), vmem_limit_bytes=VMEM_LIMIT),
    )(*args)


def _route(streams, after=None):
    tokens = [h2.shape[0] for h2, _, _ in streams]
    counts = [cnt[:, 0].astype(I32) for _, _, cnt in streams]
    total = sum(counts)
    padded = ((total + MOE_TILE - 1) // MOE_TILE) * MOE_TILE
    starts = (jnp.cumsum(padded) - padded).astype(I32)
    experts = jnp.arange(N_EXPERTS, dtype=I32)[None, :, None]
    pos, base = [], starts
    for (_, ri, _), t, cnt in zip(streams, tokens, counts):
        ri = jnp.moveaxis(ri, 2, 0).reshape(ri.shape[2], t)
        first_row = jnp.sum(jnp.where(ri[0:2, None, :] == experts, base[None, :, None], 0), axis=1)
        pos.append(ri[2:4] + first_row)
        base = base + cnt
    pos = jnp.concatenate(pos, axis=1)
    n_rows = ((2 * sum(tokens) + N_EXPERTS * (MOE_TILE - 1)) // MOE_TILE) * MOE_TILE
    xs_sorted, ready = _sc_dispatch([h2 for h2, _, _ in streams], pos[0], pos[1], n_rows, after)
    return (xs_sorted, starts, total), pos, ready


def _gather_tokens(ys_sorted, pos, t0, n):
    return _sc_gather(ys_sorted, pos[:, t0:t0 + n].reshape(2 * n)).reshape(2, n, ys_sorted.shape[-1])


def _one_layer(xp, xs, s_ret, c_pool, norm1_g, w_in, ret_norm_g, w_pool, pool_scale, w_out, norm2_g,
               w_rg, w_re, w_g, w_u, w_d, final_g, past_len):
    bp, seq, d_model = xp.shape
    bs, dseq, _ = xs.shape
    rw_width = ret_norm_g.shape[-1]
    pw = pool_scale.shape[-1]
    dh = rw_width // RET_HEADS
    half = d_model // 2

    w_r = jnp.concatenate(
        [w_re, w_rg, jnp.zeros((d_model, LANES - N_EXPERTS - N_EXPERT_GROUPS), F32)], axis=1)
    wr_hi = w_r.astype(BF16)
    wr = jnp.concatenate([wr_hi, (w_r - wr_hi.astype(F32)).astype(BF16)], axis=1)
    consts = dict(
        g1=norm1_g.reshape(1, d_model), w_in=w_in.astype(BF16), gret=ret_norm_g.reshape(1, rw_width),
        w_pool=w_pool.astype(BF16), pscale=pool_scale.reshape(1, pw), w_out=w_out.astype(BF16),
        g2=norm2_g.reshape(1, d_model), wr=wr)

    gf = final_g.reshape(1, d_model)

    ts = bs * dseq
    b_lead = bp - 1
    t_lead, t_rest = b_lead * seq, (bp - b_lead) * seq
    zeros = lambda nb: (jnp.zeros((nb, RET_HEADS, dh, dh), F32), jnp.zeros((nb, HIST_ROWS, pw), F32))
    h0s = jnp.pad(c_pool, ((0, 0), (HIST_ROWS - POOL_HIST, 0), (0, 0)))
    prompt_tile = dict(bb=1, tl=PROMPT_TILE, chunk=256)

    def stream(layer_out, t):
        x1, h2, ri, rw, st, hist, cnt = layer_out
        return dict(x1=x1.reshape(t, d_model), route=(h2.reshape(t, half), ri, cnt),
                    rw=rw.reshape(-1, EXPERTS_PER_GROUP, rw.shape[-1]), st=st, hist=hist)

    pa = stream(_layer_call(xp, 0, b_lead, *zeros(b_lead), 0, consts, gf, **prompt_tile), t_lead)
    group0, pos0, ready0 = _route([pa["route"]])
    pb = stream(_layer_call(xp, b_lead, bp - b_lead, *zeros(bp - b_lead), 0, consts, ready0, **prompt_tile),
                t_rest)
    sm = stream(_layer_call(xs, 0, bs, s_ret, h0s, past_len, consts, pb["route"][2],
                            bb=bs, tl=dseq, chunk=min(64, dseq)), ts)
    group1, pos1, _ = _route([pb["route"], sm["route"]], after=group0[0])
    ys0, ys1 = _moe_call([group0, group1], w_g, w_u, w_d)

    tp = bp * seq
    yp = _combine_call(pb["x1"], pb["rw"], 0, t_rest, _gather_tokens(ys1, pos1, 0, t_rest), gf, tp, t_lead)
    ysm = _combine_call(sm["x1"], sm["rw"], 0, ts, _gather_tokens(ys1, pos1, t_rest, ts), gf, ts, 0)
    row0 = 0
    for nb in COMBINE_CHUNKS:
        n = min(nb * seq, t_lead - row0)
        if n > 0:
            yp = _combine_call(pa["x1"], pa["rw"], row0, n, _gather_tokens(ys0, pos0, row0, n), gf, tp, row0,
                               prev_out=yp)
            row0 += n
    assert row0 == t_lead
    st_p = jnp.concatenate([pa["st"], pb["st"]], axis=0)
    hist_p = jnp.concatenate([pa["hist"], pb["hist"]], axis=0)
    return (yp.reshape(bp, seq, d_model), ysm.reshape(bs, dseq, d_model),
            st_p, hist_p[:, HIST_ROWS - POOL_HIST:], sm["st"], sm["hist"][:, HIST_ROWS - POOL_HIST:])


def kernel(x_prompt, x_sample, state_ret, cache_pool, norm1_g, w_in, ret_norm_g, w_pool, pool_scale, w_out,
           norm2_g, w_router_group, w_router_expert, w_exp_gate, w_exp_up, w_exp_down, final_norm_g):
    depth = w_in.shape[0]
    assert depth == 1, "the final RMSNorm is fused into the layer's combine kernel"
    assert x_prompt.shape[0] >= 2 and x_prompt.shape[1] % PROMPT_TILE == 0
    yp, ys, s_p, h_p, s_s, h_s = _one_layer(
        x_prompt, x_sample, state_ret[0], cache_pool[0], norm1_g[0], w_in[0], ret_norm_g[0], w_pool[0],
        pool_scale[0], w_out[0], norm2_g[0], w_router_group[0], w_router_expert[0],
        w_exp_gate[0], w_exp_up[0], w_exp_down[0], final_norm_g, PAST_LEN)
    return (yp, ys, s_p[None], h_p[None], s_s[None], h_s[None])
```

```python
import functools

import jax
import jax.numpy as jnp
from jax import lax
from jax.experimental import pallas as pl
from jax.experimental.pallas import tpu as pltpu
from jax.experimental.pallas import tpu_sc as plsc

F32 = jnp.float32
BF16 = jnp.bfloat16
I32 = jnp.int32
U32 = jnp.uint32

EPS = 1e-6
ROPE_BASE = 10000.0
RET_HEADS = 4
POOL_WINDOWS = (2, 4, 8, 16)
POOL_HIST = max(POOL_WINDOWS) - 1
N_EXPERT_GROUPS = 4
EXPERTS_PER_GROUP = 8
N_EXPERTS = N_EXPERT_GROUPS * EXPERTS_PER_GROUP
EXPERT_SHIFT = EXPERTS_PER_GROUP.bit_length() - 1
PAST_LEN = 1024

LANES = 128
HIST_ROWS = 16
MOE_TILE = 256
MOE_BUFFERS = 4
MOE_LOOKAHEAD = MOE_BUFFERS - 1
PROMPT_TILE = 1024
COMBINE_CHUNKS = (1, 2)
SC_UNIT = 32
VMEM_LIMIT = 56 * 1024 * 1024


def _nbytes(*arrays):
    return sum(a.size * a.dtype.itemsize for a in arrays)


def _rms(x, g):
    return x * lax.rsqrt(jnp.mean(x * x, axis=-1, keepdims=True) + EPS) * g


def _sigmoid(x):
    return 1.0 / (1.0 + jnp.exp(-x))


def _pack_bf16_pair(lo, hi):
    lo_b = lax.bitcast_convert_type(lo.astype(BF16).astype(F32), U32)
    hi_b = lax.bitcast_convert_type(hi.astype(BF16).astype(F32), U32)
    return hi_b | (lo_b >> 16)


def _unpack_bf16_pair(p):
    lo = lax.bitcast_convert_type(p << 16, F32)
    hi = lax.bitcast_convert_type(p & jnp.uint32(0xFFFF0000), F32)
    return lo, hi


def _layer_kernel(dc_ref, x_ref, s0_ref, h0_ref, rb_ref, rc_ref, rs_ref, rcs_ref, rss_ref,
                  dintra_ref, dq_ref, dk_ref,
                  g1_ref, win_ref, gret_ref, wpool_ref, pscale_ref, wout_ref, g2_ref,
                  wr_ref, tri_ref, after_ref,
                  x1_ref, h2_ref, ri_ref, rw_ref, st_ref, hist_ref, cnt_ref,
                  ue_ref, q_ref, k_ref, v_ref, o_ref, a_ref,
                  *, bb, tl, chunk, pos0):
    b_idx = pl.program_id(0)
    l_idx = pl.program_id(1)
    rows = bb * tl
    d_model = x_ref.shape[-1]
    rw_width = q_ref.shape[-1]
    dh = rw_width // RET_HEADS
    pw = ue_ref.shape[-1]
    gw = pw // len(POOL_WINDOWS)
    n_chunks = tl // chunk

    @pl.when(l_idx == 0)
    def _():
        st_ref[...] = s0_ref[...]
        ue_ref[:, 0:HIST_ROWS, :] = h0_ref[...]

    @pl.when((l_idx == 0) & (b_idx == 0))
    def _():
        cnt_ref[...] = jnp.zeros_like(cnt_ref)

    x = x_ref[...].reshape(rows, d_model)
    hb = _rms(x, g1_ref[...]).astype(BF16)

    def project(c0, c1):
        return jnp.dot(hb, win_ref[:, c0:c1], preferred_element_type=F32)

    proj = project(0, 2 * rw_width)

    cos_b = rb_ref[0, 0:1, :]
    sin_b = rb_ref[0, 1:2, :]
    cosf = (cos_b * rc_ref[...] - sin_b * rs_ref[...])[None]
    sinf = (sin_b * rcs_ref[...] + cos_b * rss_ref[...])[None]
    k_scale = dh ** -0.5
    for hh in range(RET_HEADS):
        qh = proj[:, hh * dh:(hh + 1) * dh]
        kh = proj[:, rw_width + hh * dh:rw_width + (hh + 1) * dh]
        qr = (qh.reshape(bb, tl, dh) * cosf
              + pltpu.roll(qh, dh // 2, 1).reshape(bb, tl, dh) * sinf).reshape(rows, dh)
        kr = (kh.reshape(bb, tl, dh) * cosf
              + pltpu.roll(kh, dh // 2, 1).reshape(bb, tl, dh) * sinf).reshape(rows, dh)
        q_ref[:, hh * dh:(hh + 1) * dh] = qr.astype(BF16)
        k_ref[:, hh * dh:(hh + 1) * dh] = kr * k_scale
    v_ref[...] = project(2 * rw_width, 3 * rw_width).astype(BF16)
    gate = project(3 * rw_width, 4 * rw_width)
    u = project(4 * rw_width, 4 * rw_width + pw)

    def ret_block(b, c):
        r0 = b * tl + c * chunk
        if not isinstance(r0, int):
            r0 = pl.multiple_of(r0, chunk)
        for hh in range(RET_HEADS):
            cs = slice(hh * dh, (hh + 1) * dh)
            qc = q_ref[pl.ds(r0, chunk), cs]
            kf = k_ref[pl.ds(r0, chunk), cs]
            vc = v_ref[pl.ds(r0, chunk), cs]
            s_old = st_ref[b, hh]
            sc = lax.dot_general(qc, kf.astype(BF16), (((1,), (1,)), ((), ())),
                                 preferred_element_type=F32) * dintra_ref[hh]
            o = (jnp.dot(sc.astype(BF16), vc, preferred_element_type=F32)
                 + dq_ref[hh] * jnp.dot(qc, s_old.astype(BF16), preferred_element_type=F32))
            kd = (kf * dk_ref[hh]).astype(BF16)
            s_new = dc_ref[hh] * s_old + lax.dot_general(
                kd, vc, (((0,), (0,)), ((), ())), preferred_element_type=F32)
            st_ref[b, hh] = s_new
            o_ref[pl.ds(r0, chunk), cs] = o

    if bb * n_chunks <= 4:
        for b in range(bb):
            for c in range(n_chunks):
                ret_block(b, c)
    else:
        def body(i, carry):
            ret_block(i // n_chunks, i % n_chunks)
            return carry
        lax.fori_loop(0, bb * n_chunks, body, 0)

    for hh in range(RET_HEADS):
        cs = slice(hh * dh, (hh + 1) * dh)
        oh = o_ref[:, cs]
        mu = jnp.mean(oh, axis=-1, keepdims=True)
        oc = oh - mu
        var = jnp.mean(oc * oc, axis=-1, keepdims=True)
        y = oc * lax.rsqrt(var + EPS) * gret_ref[:, cs]
        g = gate[:, cs]
        a_ref[:, cs] = (g * _sigmoid(g) * y).astype(BF16)

    ue_ref[:, HIST_ROWS:HIST_ROWS + tl, :] = u.reshape(bb, tl, pw)
    pos = pos0 + l_idx * tl + lax.broadcasted_iota(I32, (1, tl, 1), 1)
    for gi, w in enumerate(POOL_WINDOWS):
        cs = slice(gi * gw, (gi + 1) * gw)
        acc = ue_ref[:, HIST_ROWS:HIST_ROWS + tl, cs]
        for j in range(1, w):
            acc = acc + ue_ref[:, HIST_ROWS - j:HIST_ROWS - j + tl, cs]
        inv_cnt = 1.0 / jnp.minimum(pos + 1, w).astype(F32)
        p = (acc * inv_cnt).reshape(rows, gw) - u[:, cs]
        z = jnp.dot(p.astype(BF16), wpool_ref[gi], preferred_element_type=F32) * pscale_ref[:, cs]
        a_ref[:, rw_width + gi * gw:rw_width + (gi + 1) * gw] = z.astype(BF16)
    tail = ue_ref[:, tl:tl + HIST_ROWS, :]
    ue_ref[:, 0:HIST_ROWS, :] = tail
    hist_ref[...] = tail

    x1 = x + jnp.dot(a_ref[...], wout_ref[...], preferred_element_type=F32)
    x1_ref[...] = x1.reshape(bb, tl, d_model)
    h2 = _rms(x1, g2_ref[...])
    h2_ref[...] = _pack_bf16_pair(h2[:, 0:d_model // 2], h2[:, d_model // 2:]).reshape(bb, tl, d_model // 2)

    h2_hi = h2.astype(BF16)
    h2_lo = (h2 - h2_hi.astype(F32)).astype(BF16)
    two = jnp.dot(h2_hi, wr_ref[...], preferred_element_type=F32)
    logits = (two[:, 0:LANES] + two[:, LANES:2 * LANES]
              + jnp.dot(h2_lo, wr_ref[:, 0:LANES], preferred_element_type=F32))
    lt = logits.T
    neg = jnp.float32(-jnp.inf)
    big = jnp.float32(1e9)
    sub = lax.broadcasted_iota(I32, (EXPERTS_PER_GROUP, rows), 0).astype(F32)
    gl = jnp.where(sub < N_EXPERT_GROUPS, lt[N_EXPERTS:N_EXPERTS + EXPERTS_PER_GROUP], neg)
    gmax = jnp.max(gl, axis=0, keepdims=True)
    gidx = jnp.min(jnp.where(gl == gmax, sub, big), axis=0, keepdims=True)
    p_sel = 1.0 / jnp.sum(jnp.exp(gl - gmax), axis=0, keepdims=True)
    el = lt[0:EXPERTS_PER_GROUP]
    for g in range(1, N_EXPERT_GROUPS):
        el = jnp.where(gidx == g, lt[g * EXPERTS_PER_GROUP:(g + 1) * EXPERTS_PER_GROUP], el)
    m1 = jnp.max(el, axis=0, keepdims=True)
    t1 = jnp.min(jnp.where(el == m1, sub, big), axis=0, keepdims=True)
    el2 = jnp.where(sub == t1, neg, el)
    m2 = jnp.max(el2, axis=0, keepdims=True)
    t2 = jnp.min(jnp.where(el2 == m2, sub, big), axis=0, keepdims=True)
    e2 = jnp.exp(m2 - m1)
    w1 = p_sel / (1.0 + e2)
    w2 = p_sel * e2 / (1.0 + e2)
    i1 = gidx * EXPERTS_PER_GROUP + t1
    i2 = gidx * EXPERTS_PER_GROUP + t2

    eid = lax.broadcasted_iota(I32, (N_EXPERTS, rows), 0).astype(F32)
    hit1 = eid == i1
    hit2 = eid == i2
    onehot = (hit1 | hit2).astype(BF16)
    before = jnp.dot(onehot, tri_ref[...], preferred_element_type=F32) + cnt_ref[...]
    r1 = jnp.sum(jnp.where(hit1, before, 0.0), axis=0, keepdims=True)
    r2 = jnp.sum(jnp.where(hit2, before, 0.0), axis=0, keepdims=True)
    cnt_ref[...] = cnt_ref[...] + jnp.sum(onehot.astype(F32), axis=1, keepdims=True)

    ri = jnp.where(sub == 0, i1, jnp.where(sub == 1, i2, jnp.where(sub == 2, r1, jnp.where(sub == 3, r2, 0.0))))
    ri_ref[...] = ri.astype(I32).reshape(ri_ref.shape)
    rw_ref[...] = jnp.where(sub == 0, w1, jnp.where(sub == 1, w2, 0.0)).reshape(rw_ref.shape)


def _rope_tables(pos0, seq, tl, dh):
    half = dh // 2
    inv = ROPE_BASE ** (-jnp.arange(half, dtype=F32) / half)
    ang_t = jnp.arange(tl, dtype=F32)[:, None] * inv[None, :]
    ang_b = (pos0 + tl * jnp.arange(seq // tl)).astype(F32)[:, None] * inv[None, :]
    dup = lambda a: jnp.concatenate([a, a], axis=-1)
    sgn = lambda a: jnp.concatenate([-a, a], axis=-1)
    base = jnp.stack([dup(jnp.cos(ang_b)), dup(jnp.sin(ang_b))], axis=1)
    base = jnp.pad(base, ((0, 0), (0, 8 - base.shape[1]), (0, 0)))
    cos_t, sin_t = jnp.cos(ang_t), jnp.sin(ang_t)
    return base, dup(cos_t), dup(sin_t), sgn(cos_t), sgn(sin_t)


def _layer_call(x, b0, nb, s0, h0, pos0, consts, after, *, bb, tl, chunk):
    _, seq, d_model = x.shape
    bsz = nb
    blk0 = b0 // bb
    rows = bb * tl
    rw_width = consts["gret"].shape[-1]
    pw = consts["pscale"].shape[-1]
    dh = rw_width // RET_HEADS

    rope = _rope_tables(pos0, seq, tl, dh)

    lg = jnp.log1p(-jnp.exp2(-5.0 - jnp.arange(RET_HEADS, dtype=F32)))
    idx = jnp.arange(chunk, dtype=F32)
    diff = idx[:, None] - idx[None, :]
    d_intra = jnp.where(diff[None] >= 0, jnp.exp(jnp.maximum(diff, 0.0)[None] * lg[:, None, None]), 0.0)
    d_q = jnp.broadcast_to(jnp.exp((idx + 1.0)[None, :] * lg[:, None])[:, :, None], (RET_HEADS, chunk, dh))
    d_k = jnp.broadcast_to(jnp.exp((chunk - 1.0 - idx)[None, :] * lg[:, None])[:, :, None], (RET_HEADS, chunk, dh))
    d_c = jnp.exp(chunk * lg)
    tri = jnp.triu(jnp.ones((rows, rows), BF16), 1)

    const2 = lambda b, l, *_: (0, 0)
    const3 = lambda b, l, *_: (0, 0, 0)
    grid_spec = pltpu.PrefetchScalarGridSpec(
        num_scalar_prefetch=0,
        grid=(bsz // bb, seq // tl),
        in_specs=[
            pl.BlockSpec(memory_space=pltpu.SMEM),
            pl.BlockSpec((bb, tl, d_model), lambda b, l: (blk0 + b, l, 0)),
            pl.BlockSpec((bb, RET_HEADS, dh, dh), lambda b, l: (b, 0, 0, 0)),
            pl.BlockSpec((bb, HIST_ROWS, pw), lambda b, l: (b, 0, 0)),
            pl.BlockSpec((1, 8, dh), lambda b, l: (l, 0, 0)),
            pl.BlockSpec((tl, dh), const2),
            pl.BlockSpec((tl, dh), const2),
            pl.BlockSpec((tl, dh), const2),
            pl.BlockSpec((tl, dh), const2),
            pl.BlockSpec((RET_HEADS, chunk, chunk), const3),
            pl.BlockSpec((RET_HEADS, chunk, dh), const3),
            pl.BlockSpec((RET_HEADS, chunk, dh), const3),
            pl.BlockSpec((1, d_model), const2),
            pl.BlockSpec(consts["w_in"].shape, const2),
            pl.BlockSpec((1, rw_width), const2),
            pl.BlockSpec(consts["w_pool"].shape, const3),
            pl.BlockSpec((1, pw), const2),
            pl.BlockSpec(consts["w_out"].shape, const2),
            pl.BlockSpec((1, d_model), const2),
            pl.BlockSpec((d_model, 2 * LANES), const2),
            pl.BlockSpec((rows, rows), const2),
            pl.BlockSpec(memory_space=pl.ANY),
        ],
        out_specs=[
            pl.BlockSpec((bb, tl, d_model), lambda b, l: (b, l, 0)),
            pl.BlockSpec((bb, tl, d_model // 2), lambda b, l: (b, l, 0)),
            pl.BlockSpec((1, 1, EXPERTS_PER_GROUP, rows), lambda b, l: (b, l, 0, 0)),
            pl.BlockSpec((1, 1, EXPERTS_PER_GROUP, rows), lambda b, l: (b, l, 0, 0)),
            pl.BlockSpec((bb, RET_HEADS, dh, dh), lambda b, l: (b, 0, 0, 0)),
            pl.BlockSpec((bb, HIST_ROWS, pw), lambda b, l: (b, 0, 0)),
            pl.BlockSpec((N_EXPERTS, rows), const2),
        ],
        scratch_shapes=[
            pltpu.VMEM((bb, HIST_ROWS + tl, pw), F32),
            pltpu.VMEM((rows, rw_width), BF16),
            pltpu.VMEM((rows, rw_width), F32),
            pltpu.VMEM((rows, rw_width), BF16),
            pltpu.VMEM((rows, rw_width), F32),
            pltpu.VMEM((rows, d_model), BF16),
        ],
    )
    out_shape = [
        jax.ShapeDtypeStruct((bsz, seq, d_model), F32),
        jax.ShapeDtypeStruct((bsz, seq, d_model // 2), U32),
        jax.ShapeDtypeStruct((bsz // bb, seq // tl, EXPERTS_PER_GROUP, rows), I32),
        jax.ShapeDtypeStruct((bsz // bb, seq // tl, EXPERTS_PER_GROUP, rows), F32),
        jax.ShapeDtypeStruct((bsz, RET_HEADS, dh, dh), F32),
        jax.ShapeDtypeStruct((bsz, HIST_ROWS, pw), F32),
        jax.ShapeDtypeStruct((N_EXPERTS, rows), F32),
    ]
    kern = functools.partial(_layer_kernel, bb=bb, tl=tl, chunk=chunk, pos0=pos0)
    operands = (d_c, x, s0, h0, *rope, d_intra, d_q, d_k,
                consts["g1"], consts["w_in"], consts["gret"], consts["w_pool"], consts["pscale"],
                consts["w_out"], consts["g2"], consts["wr"], tri, after)
    n_tok = bsz * seq
    mm_flops_per_token = 2 * (d_model * consts["w_in"].shape[1] + d_model * d_model + 2 * d_model * LANES + pw * pw // 4
                              + rw_width * (2 * chunk + 2 * dh) + N_EXPERTS * rows)
    cost = pl.CostEstimate(
        flops=n_tok * mm_flops_per_token, transcendentals=n_tok * (rw_width + 2 * N_EXPERT_GROUPS),
        bytes_accessed=_nbytes(*operands) - _nbytes(x, after) + n_tok * d_model * 4 + _nbytes(*out_shape))
    return pl.pallas_call(
        kern, grid_spec=grid_spec, out_shape=out_shape, name=f"layer_pos{pos0}_b{b0}", cost_estimate=cost,
        compiler_params=pltpu.CompilerParams(
            dimension_semantics=("arbitrary", "arbitrary"), vmem_limit_bytes=VMEM_LIMIT),
    )(*operands)


def _sc_partition(n_units):
    info = plsc.get_sparse_core_info()
    nc, nw = info.num_cores, info.num_cores * info.num_subcores
    upw = -(-n_units // nw)
    upw += upw % 2
    return nc, nw, upw


def _units_by_worker(idx, n_units, upw, nw):
    idx = jnp.pad(idx.reshape(n_units, SC_UNIT), ((0, nw * upw - n_units), (0, 0)))
    return idx.reshape(upw, nw, SC_UNIT).transpose(1, 0, 2)


def _sc_dispatch(srcs, idx0, idx1, n_out_rows, after=None):
    assert 1 <= len(srcs) <= 2
    d = srcs[0].shape[1]
    dtype = srcs[0].dtype
    assert all(src.shape[0] % SC_UNIT == 0 for src in srcs)
    units_a = srcs[0].shape[0] // SC_UNIT
    n_units = sum(src.shape[0] for src in srcs) // SC_UNIT
    nc, nw, upw = _sc_partition(n_units)
    idx0 = _units_by_worker(idx0, n_units, upw, nw)
    idx1 = _units_by_worker(idx1, n_units, upw, nw)
    mesh = plsc.VectorSubcoreMesh(core_axis_name="c", subcore_axis_name="s")
    dma = pltpu.SemaphoreType.DMA
    extra = [] if after is None else [after]

    moved = n_units * SC_UNIT * d * jnp.dtype(dtype).itemsize
    @functools.partial(
        pl.kernel, mesh=mesh,
        cost_estimate=pl.CostEstimate(flops=0, transcendentals=0, bytes_accessed=3 * moved + _nbytes(idx0, idx1)),
        out_type=jax.ShapeDtypeStruct((n_out_rows, d), dtype),
        scratch_types=[
            pltpu.VMEM((upw, SC_UNIT), I32),
            pltpu.VMEM((upw, SC_UNIT), I32),
            pltpu.VMEM((SC_UNIT, d), dtype),
            pltpu.VMEM((SC_UNIT, d), dtype),
            dma, dma, dma, dma, dma, dma,
        ],
    )
    def k(*refs):
        src_hbm = refs[:len(srcs)]
        i0_hbm, i1_hbm, out_hbm, i0_v, i1_v, rows0, rows1, l0, l1, p0, p1, q0, q1 = refs[len(srcs) + len(extra):]
        wid = lax.axis_index("s") * nc + lax.axis_index("c")
        pltpu.sync_copy(i0_hbm.at[wid], i0_v)
        pltpu.sync_copy(i1_hbm.at[wid], i1_v)
        rows, lsem, psem, qsem = (rows0, rows1), (l0, l1), (p0, p1), (q0, q1)

        def live(j):
            return j * nw + wid < n_units

        def load(j, b, op):
            unit = j * nw + wid

            @pl.when(live(j) & (unit < units_a))
            def _():
                op(pltpu.make_async_copy(
                    src_hbm[0].at[pl.ds(pl.multiple_of(unit * SC_UNIT, 8), SC_UNIT)], rows[b], lsem[b]))

            if len(srcs) == 2:
                @pl.when(live(j) & (unit >= units_a))
                def _():
                    op(pltpu.make_async_copy(
                        src_hbm[1].at[pl.ds(pl.multiple_of((unit - units_a) * SC_UNIT, 8), SC_UNIT)],
                        rows[b], lsem[b]))

        def scatter(j, b, op):
            @pl.when(live(j))
            def _():
                op(pltpu.make_async_copy(rows[b], out_hbm.at[i0_v.at[j]], psem[b]))
                op(pltpu.make_async_copy(rows[b], out_hbm.at[i1_v.at[j]], qsem[b]))

        start = lambda c: c.start()
        wait = lambda c: c.wait()
        load(0, 0, start)

        @pl.loop(0, upw, step=2)
        def _(j):
            @pl.when(j > 0)
            def _():
                scatter(j - 1, 1, wait)
            load(j + 1, 1, start)
            load(j, 0, wait)
            scatter(j, 0, start)
            scatter(j, 0, wait)

            @pl.when(j + 2 < upw)
            def _():
                load(j + 2, 0, start)
            load(j + 1, 1, wait)
            scatter(j + 1, 1, start)

        scatter(upw - 1, 1, wait)

    return k(*srcs, *extra, idx0, idx1), idx1


def _sc_gather(table, idx):
    n = idx.shape[0]
    d = table.shape[1]
    assert n % SC_UNIT == 0
    n_units = n // SC_UNIT
    nc, nw, upw = _sc_partition(n_units)
    idx = _units_by_worker(idx, n_units, upw, nw)
    mesh = plsc.VectorSubcoreMesh(core_axis_name="c", subcore_axis_name="s")
    dma = pltpu.SemaphoreType.DMA

    @functools.partial(
        pl.kernel, mesh=mesh,
        cost_estimate=pl.CostEstimate(flops=0, transcendentals=0,
                                      bytes_accessed=2 * n * d * table.dtype.itemsize + _nbytes(idx)),
        out_type=jax.ShapeDtypeStruct((n, d), table.dtype),
        scratch_types=[
            pltpu.VMEM((upw, SC_UNIT), I32),
            pltpu.VMEM((SC_UNIT, d), table.dtype),
            pltpu.VMEM((SC_UNIT, d), table.dtype),
            dma, dma, dma, dma,
        ],
    )
    def k(t_hbm, i_hbm, out_hbm, i_v, rows0, rows1, g0, g1, w0, w1):
        wid = lax.axis_index("s") * nc + lax.axis_index("c")
        pltpu.sync_copy(i_hbm.at[wid], i_v)
        rows, gsem, wsem = (rows0, rows1), (g0, g1), (w0, w1)

        def live(j):
            return j * nw + wid < n_units

        def gather(j, b, op):
            @pl.when(live(j))
            def _():
                op(pltpu.make_async_copy(t_hbm.at[i_v.at[j]], rows[b], gsem[b]))

        def write(j, b, op):
            @pl.when(live(j))
            def _():
                op(pltpu.make_async_copy(
                    rows[b], out_hbm.at[pl.ds(pl.multiple_of((j * nw + wid) * SC_UNIT, 8), SC_UNIT)], wsem[b]))

        start = lambda c: c.start()
        wait = lambda c: c.wait()
        gather(0, 0, start)

        @pl.loop(0, upw, step=2)
        def _(j):
            @pl.when(j > 0)
            def _():
                write(j - 1, 1, wait)
            gather(j + 1, 1, start)
            gather(j, 0, wait)
            write(j, 0, start)
            write(j, 0, wait)

            @pl.when(j + 2 < upw)
            def _():
                gather(j + 2, 0, start)
            gather(j + 1, 1, wait)
            write(j + 1, 1, start)

        write(upw - 1, 1, wait)

    return k(table, idx)


def _moe_kernel(start0_ref, count0_ref, gtot0_ref, start1_ref, count1_ref, gtot1_ref,
                xs0_hbm, xs1_hbm, wg_ref, wu_ref, wd_ref, ys0_hbm, ys1_hbm,
                wgu_s, wd_s, xbuf0, ybuf0, xbuf1, ybuf1, sem_in0, sem_out0, sem_in1, sem_out1):
    e = pl.program_id(0)
    last = pl.num_programs(0) - 1
    hidden = wd_s.shape[0]
    half = xbuf0.shape[-1]
    segments = (
        (xs0_hbm, ys0_hbm, xbuf0, ybuf0, sem_in0, sem_out0, start0_ref[e], count0_ref[e], gtot0_ref[0]),
        (xs1_hbm, ys1_hbm, xbuf1, ybuf1, sem_in1, sem_out1, start1_ref[e], count1_ref[e], gtot1_ref[0]),
    )

    def rows_of(g):
        return pl.ds(pl.multiple_of(g * MOE_TILE, MOE_TILE), MOE_TILE)

    def pipeline(xs_hbm, ys_hbm, xbuf, ybuf, sem_in, sem_out):
        def copy_in(g):
            slot = g % MOE_BUFFERS
            return pltpu.make_async_copy(xs_hbm.at[rows_of(g)], xbuf.at[slot], sem_in.at[slot])

        def copy_out(g):
            slot = g % MOE_BUFFERS
            return pltpu.make_async_copy(ybuf.at[slot], ys_hbm.at[rows_of(g)], sem_out.at[slot])
        return copy_in, copy_out

    @pl.when(e == 0)
    def _():
        for xs_hbm, ys_hbm, xbuf, ybuf, sem_in, sem_out, _, _, g_total in segments:
            copy_in, _ = pipeline(xs_hbm, ys_hbm, xbuf, ybuf, sem_in, sem_out)
            for g in range(MOE_LOOKAHEAD):
                @pl.when(g < g_total)
                def _():
                    copy_in(g).start()

    @pl.when(segments[0][7] + segments[1][7] > 0)
    def _():
        wgu_s[:, 0:hidden] = wg_ref[0].astype(BF16)
        wgu_s[:, hidden:2 * hidden] = wu_ref[0].astype(BF16)
        wd_s[...] = wd_ref[0].astype(BF16)

    for xs_hbm, ys_hbm, xbuf, ybuf, sem_in, sem_out, start, count, g_total in segments:
        copy_in, copy_out = pipeline(xs_hbm, ys_hbm, xbuf, ybuf, sem_in, sem_out)
        g_first = start // MOE_TILE
        n_tiles = (count + MOE_TILE - 1) // MOE_TILE

        def tile(t, carry, copy_in=copy_in, copy_out=copy_out, xbuf=xbuf, ybuf=ybuf,
                 g_first=g_first, count=count, g_total=g_total):
            g = g_first + t

            @pl.when(g + MOE_LOOKAHEAD < g_total)
            def _():
                copy_in(g + MOE_LOOKAHEAD).start()

            copy_in(g).wait()

            @pl.when(g >= MOE_BUFFERS)
            def _():
                copy_out(g - MOE_BUFFERS).wait()

            slot = g % MOE_BUFFERS
            row = lax.broadcasted_iota(I32, (MOE_TILE, half), 0)
            x_lo, x_hi = _unpack_bf16_pair(jnp.where(row < count - t * MOE_TILE, xbuf[slot], jnp.uint32(0)))
            ab = (jnp.dot(x_lo.astype(BF16), wgu_s[0:half, :], preferred_element_type=F32)
                  + jnp.dot(x_hi.astype(BF16), wgu_s[half:2 * half, :], preferred_element_type=F32))
            a = ab[:, 0:hidden]
            he = a * _sigmoid(a) * ab[:, hidden:2 * hidden]
            y = jnp.dot(he.astype(BF16), wd_s[...], preferred_element_type=F32)
            ybuf[slot] = _pack_bf16_pair(y[:, 0:half], y[:, half:2 * half])
            copy_out(g).start()
            return carry

        lax.fori_loop(0, n_tiles, tile, 0)

        @pl.when(e == last)
        def _(copy_out=copy_out, g_total=g_total):
            for j in range(1, MOE_BUFFERS + 1):
                @pl.when(g_total >= j)
                def _():
                    copy_out(g_total - j).wait()


def _moe_call(groups, w_g, w_u, w_d):
    (xs0, starts0, cnt0), (xs1, starts1, cnt1) = groups
    half = xs0.shape[1]
    n_experts, d_model, hidden = w_g.shape

    def tiles_total(starts, cnt):
        return ((starts[-1:] + cnt[-1:] + MOE_TILE - 1) // MOE_TILE).astype(I32)

    wspec = lambda shape: pl.BlockSpec(shape, lambda e, *_: (e, 0, 0))
    tile_bufs = [pltpu.VMEM((MOE_BUFFERS, MOE_TILE, half), U32)] * 4
    grid_spec = pltpu.PrefetchScalarGridSpec(
        num_scalar_prefetch=6,
        grid=(n_experts,),
        in_specs=[
            pl.BlockSpec(memory_space=pl.ANY),
            pl.BlockSpec(memory_space=pl.ANY),
            wspec((1, d_model, hidden)),
            wspec((1, d_model, hidden)),
            wspec((1, hidden, d_model)),
        ],
        out_specs=[pl.BlockSpec(memory_space=pl.ANY), pl.BlockSpec(memory_space=pl.ANY)],
        scratch_shapes=[
            pltpu.VMEM((d_model, 2 * hidden), BF16),
            pltpu.VMEM((hidden, d_model), BF16),
            *tile_bufs,
            *[pltpu.SemaphoreType.DMA((MOE_BUFFERS,))] * 4,
        ],
    )
    n_rows = xs0.shape[0] + xs1.shape[0]
    cost = pl.CostEstimate(flops=n_rows * 6 * d_model * hidden, transcendentals=n_rows * hidden,
                           bytes_accessed=2 * _nbytes(xs0, xs1) + _nbytes(w_g, w_u, w_d))
    return pl.pallas_call(
        _moe_kernel, grid_spec=grid_spec, cost_estimate=cost,
        out_shape=[jax.ShapeDtypeStruct(xs0.shape, U32), jax.ShapeDtypeStruct(xs1.shape, U32)], name="moe_experts",
        compiler_params=pltpu.CompilerParams(
            dimension_semantics=("arbitrary",), vmem_limit_bytes=VMEM_LIMIT),
    )(starts0, cnt0, tiles_total(starts0, cnt0), starts1, cnt1, tiles_total(starts1, cnt1),
      xs0, xs1, w_g, w_u, w_d)


def _combine_kernel(x1_ref, y0_ref, y1_ref, rw_ref, gf_ref, *rest):
    out_ref = rest[-1]
    tr = x1_ref.shape[0]
    w_rows = jnp.concatenate([rw_ref[0], jnp.zeros((LANES - rw_ref.shape[1], tr), F32)], axis=0)
    w_cols = w_rows.T
    w0, w1 = w_cols[:, 0:1], w_cols[:, 1:2]
    a_lo, a_hi = _unpack_bf16_pair(y0_ref[0])
    b_lo, b_hi = _unpack_bf16_pair(y1_ref[0])
    moe = jnp.concatenate([w0 * a_lo + w1 * b_lo, w0 * a_hi + w1 * b_hi], axis=-1)
    out_ref[...] = _rms(x1_ref[...] + moe, gf_ref[...])


def _combine_call(x1, rw, row0, n, yg, gf, out_rows, out_row0, prev_out=None):
    t, d_model = x1.shape
    tr = rw.shape[-1]
    half = yg.shape[-1]
    assert t % tr == 0 and row0 % tr == 0 and n % tr == 0 and rw.shape == (t // tr, EXPERTS_PER_GROUP, tr)
    assert yg.shape == (2, n, half) and out_row0 % tr == 0
    off = row0 // tr
    ooff = out_row0 // tr
    in_specs = [
        pl.BlockSpec((tr, d_model), lambda i: (off + i, 0)),
        pl.BlockSpec((1, tr, half), lambda i: (0, i, 0)),
        pl.BlockSpec((1, tr, half), lambda i: (1, i, 0)),
        pl.BlockSpec((1, EXPERTS_PER_GROUP, tr), lambda i: (off + i, 0, 0)),
        pl.BlockSpec((1, d_model), lambda i: (0, 0)),
    ]
    args = [x1, yg, yg, rw, gf]
    aliases = {}
    if prev_out is not None:
        in_specs.append(pl.BlockSpec(memory_space=pl.ANY))
        args.append(prev_out)
        aliases = {len(args) - 1: 0}
    return pl.pallas_call(
        _combine_kernel,
        grid=(n // tr,),
        in_specs=in_specs,
        out_specs=pl.BlockSpec((tr, d_model), lambda i: (ooff + i, 0)),
        out_shape=jax.ShapeDtypeStruct((out_rows, d_model), F32), name=f"combine_row{out_row0}_of{out_rows}",
        cost_estimate=pl.CostEstimate(flops=8 * n * d_model, transcendentals=n,
                                      bytes_accessed=2 * n * d_model * 4 + _nbytes(yg) + n * 4 * EXPERTS_PER_GROUP),
        input_output_aliases=aliases,
        compiler_params=pltpu.CompilerParams(
            dimension_semantics=("arbitrary",), vmem_limit_bytes=VMEM_LIMIT),
    )(*args)


def _route(streams, after=None):
    tokens = [h2.shape[0] for h2, _, _ in streams]
    counts = [cnt[:, 0].astype(I32) for _, _, cnt in streams]
    total = sum(counts)
    padded = ((total + MOE_TILE - 1) // MOE_TILE) * MOE_TILE
    starts = (jnp.cumsum(padded) - padded).astype(I32)
    experts = jnp.arange(N_EXPERTS, dtype=I32)[None, :, None]
    pos, base = [], starts
    for (_, ri, _), t, cnt in zip(streams, tokens, counts):
        ri = jnp.moveaxis(ri, 2, 0).reshape(ri.shape[2], t)
        first_row = jnp.sum(jnp.where(ri[0:2, None, :] == experts, base[None, :, None], 0), axis=1)
        pos.append(ri[2:4] + first_row)
        base = base + cnt
    pos = jnp.concatenate(pos, axis=1)
    n_rows = ((2 * sum(tokens) + N_EXPERTS * (MOE_TILE - 1)) // MOE_TILE) * MOE_TILE
    xs_sorted, ready = _sc_dispatch([h2 for h2, _, _ in streams], pos[0], pos[1], n_rows, after)
    return (xs_sorted, starts, total), pos, ready


def _gather_tokens(ys_sorted, pos, t0, n):
    return _sc_gather(ys_sorted, pos[:, t0:t0 + n].reshape(2 * n)).reshape(2, n, ys_sorted.shape[-1])


def _one_layer(xp, xs, s_ret, c_pool, norm1_g, w_in, ret_norm_g, w_pool, pool_scale, w_out, norm2_g,
               w_rg, w_re, w_g, w_u, w_d, final_g, past_len):
    bp, seq, d_model = xp.shape
    bs, dseq, _ = xs.shape
    rw_width = ret_norm_g.shape[-1]
    pw = pool_scale.shape[-1]
    dh = rw_width // RET_HEADS
    half = d_model // 2

    w_r = jnp.concatenate(
        [w_re, w_rg, jnp.zeros((d_model, LANES - N_EXPERTS - N_EXPERT_GROUPS), F32)], axis=1)
    wr_hi = w_r.astype(BF16)
    wr = jnp.concatenate([wr_hi, (w_r - wr_hi.astype(F32)).astype(BF16)], axis=1)
    consts = dict(
        g1=norm1_g.reshape(1, d_model), w_in=w_in.astype(BF16), gret=ret_norm_g.reshape(1, rw_width),
        w_pool=w_pool.astype(BF16), pscale=pool_scale.reshape(1, pw), w_out=w_out.astype(BF16),
        g2=norm2_g.reshape(1, d_model), wr=wr)

    gf = final_g.reshape(1, d_model)

    ts = bs * dseq
    b_lead = bp - 1
    t_lead, t_rest = b_lead * seq, (bp - b_lead) * seq
    zeros = lambda nb: (jnp.zeros((nb, RET_HEADS, dh, dh), F32), jnp.zeros((nb, HIST_ROWS, pw), F32))
    h0s = jnp.pad(c_pool, ((0, 0), (HIST_ROWS - POOL_HIST, 0), (0, 0)))
    prompt_tile = dict(bb=1, tl=PROMPT_TILE, chunk=256)

    def stream(layer_out, t):
        x1, h2, ri, rw, st, hist, cnt = layer_out
        return dict(x1=x1.reshape(t, d_model), route=(h2.reshape(t, half), ri, cnt),
                    rw=rw.reshape(-1, EXPERTS_PER_GROUP, rw.shape[-1]), st=st, hist=hist)

    pa = stream(_layer_call(xp, 0, b_lead, *zeros(b_lead), 0, consts, gf, **prompt_tile), t_lead)
    group0, pos0, ready0 = _route([pa["route"]])
    pb = stream(_layer_call(xp, b_lead, bp - b_lead, *zeros(bp - b_lead), 0, consts, ready0, **prompt_tile),
                t_rest)
    sm = stream(_layer_call(xs, 0, bs, s_ret, h0s, past_len, consts, pb["route"][2],
                            bb=bs, tl=dseq, chunk=min(64, dseq)), ts)
    group1, pos1, _ = _route([pb["route"], sm["route"]], after=group0[0])
    ys0, ys1 = _moe_call([group0, group1], w_g, w_u, w_d)

    tp = bp * seq
    yp = _combine_call(pb["x1"], pb["rw"], 0, t_rest, _gather_tokens(ys1, pos1, 0, t_rest), gf, tp, t_lead)
    ysm = _combine_call(sm["x1"], sm["rw"], 0, ts, _gather_tokens(ys1, pos1, t_rest, ts), gf, ts, 0)
    row0 = 0
    for nb in COMBINE_CHUNKS:
        n = min(nb * seq, t_lead - row0)
        if n > 0:
            yp = _combine_call(pa["x1"], pa["rw"], row0, n, _gather_tokens(ys0, pos0, row0, n), gf, tp, row0,
                               prev_out=yp)
            row0 += n
    assert row0 == t_lead
    st_p = jnp.concatenate([pa["st"], pb["st"]], axis=0)
    hist_p = jnp.concatenate([pa["hist"], pb["hist"]], axis=0)
    return (yp.reshape(bp, seq, d_model), ysm.reshape(bs, dseq, d_model),
            st_p, hist_p[:, HIST_ROWS - POOL_HIST:], sm["st"], sm["hist"][:, HIST_ROWS - POOL_HIST:])


def kernel(x_prompt, x_sample, state_ret, cache_pool, norm1_g, w_in, ret_norm_g, w_pool, pool_scale, w_out,
           norm2_g, w_router_group, w_router_expert, w_exp_gate, w_exp_up, w_exp_down, final_norm_g):
    depth = w_in.shape[0]
    assert depth == 1, "the final RMSNorm is fused into the layer's combine kernel"
    assert x_prompt.shape[0] >= 2 and x_prompt.shape[1] % PROMPT_TILE == 0
    yp, ys, s_p, h_p, s_s, h_s = _one_layer(
        x_prompt, x_sample, state_ret[0], cache_pool[0], norm1_g[0], w_in[0], ret_norm_g[0], w_pool[0],
        pool_scale[0], w_out[0], norm2_g[0], w_router_group[0], w_router_expert[0],
        w_exp_gate[0], w_exp_up[0], w_exp_down[0], final_norm_g, PAST_LEN)
    return (yp, ys, s_p[None], h_p[None], s_s[None], h_s[None])
```

```python
import functools

import jax
import jax.numpy as jnp
from jax import lax
from jax.experimental import pallas as pl
from jax.experimental.pallas import tpu as pltpu
from jax.experimental.pallas import tpu_sc as plsc

F32 = jnp.float32
BF16 = jnp.bfloat16
I32 = jnp.int32
U32 = jnp.uint32

EPS = 1e-6
ROPE_BASE = 10000.0
RET_HEADS = 4
POOL_WINDOWS = (2, 4, 8, 16)
POOL_HIST = max(POOL_WINDOWS) - 1
N_EXPERT_GROUPS = 4
EXPERTS_PER_GROUP = 8
N_EXPERTS = N_EXPERT_GROUPS * EXPERTS_PER_GROUP
EXPERT_SHIFT = EXPERTS_PER_GROUP.bit_length() - 1
PAST_LEN = 1024

LANES = 128
HIST_ROWS = 16
MOE_TILE = 256
MOE_BUFFERS = 6
MOE_UNROLL = 2
MOE_LOOKAHEAD = MOE_BUFFERS - MOE_UNROLL
PROMPT_TILE = 1024
COMBINE_CHUNKS = (1, 2)
SC_UNIT = 32
VMEM_LIMIT = 56 * 1024 * 1024


def _nbytes(*arrays):
    return sum(a.size * a.dtype.itemsize for a in arrays)


def _rms(x, g):
    return x * lax.rsqrt(jnp.mean(x * x, axis=-1, keepdims=True) + EPS) * g


def _sigmoid(x):
    return 1.0 / (1.0 + jnp.exp(-x))


def _pack_bf16_pair(lo, hi):
    lo_b = lax.bitcast_convert_type(lo.astype(BF16).astype(F32), U32)
    hi_b = lax.bitcast_convert_type(hi.astype(BF16).astype(F32), U32)
    return hi_b | (lo_b >> 16)


def _unpack_bf16_pair(p):
    lo = lax.bitcast_convert_type(p << 16, F32)
    hi = lax.bitcast_convert_type(p & jnp.uint32(0xFFFF0000), F32)
    return lo, hi


def _layer_kernel(dc_ref, x_ref, s0_ref, h0_ref, rb_ref, rc_ref, rs_ref, rcs_ref, rss_ref,
                  dintra_ref, dq_ref, dk_ref,
                  g1_ref, win_ref, gret_ref, wpool_ref, pscale_ref, wout_ref, g2_ref,
                  wr_ref, tri_ref, after_ref,
                  x1_ref, h2_ref, ri_ref, rw_ref, st_ref, hist_ref, cnt_ref,
                  ue_ref, q_ref, k_ref, v_ref, o_ref, a_ref,
                  *, bb, tl, chunk, pos0):
    b_idx = pl.program_id(0)
    l_idx = pl.program_id(1)
    rows = bb * tl
    d_model = x_ref.shape[-1]
    rw_width = q_ref.shape[-1]
    dh = rw_width // RET_HEADS
    pw = ue_ref.shape[-1]
    gw = pw // len(POOL_WINDOWS)
    n_chunks = tl // chunk

    @pl.when(l_idx == 0)
    def _():
        st_ref[...] = s0_ref[...]
        ue_ref[:, 0:HIST_ROWS, :] = h0_ref[...]

    @pl.when((l_idx == 0) & (b_idx == 0))
    def _():
        cnt_ref[...] = jnp.zeros_like(cnt_ref)

    x = x_ref[...].reshape(rows, d_model)
    hb = _rms(x, g1_ref[...]).astype(BF16)

    def project(c0, c1):
        return jnp.dot(hb, win_ref[:, c0:c1], preferred_element_type=F32)

    proj = project(0, 2 * rw_width)

    cos_b = rb_ref[0, 0:1, :]
    sin_b = rb_ref[0, 1:2, :]
    cosf = (cos_b * rc_ref[...] - sin_b * rs_ref[...])[None]
    sinf = (sin_b * rcs_ref[...] + cos_b * rss_ref[...])[None]
    k_scale = dh ** -0.5
    for hh in range(RET_HEADS):
        qh = proj[:, hh * dh:(hh + 1) * dh]
        kh = proj[:, rw_width + hh * dh:rw_width + (hh + 1) * dh]
        qr = (qh.reshape(bb, tl, dh) * cosf
              + pltpu.roll(qh, dh // 2, 1).reshape(bb, tl, dh) * sinf).reshape(rows, dh)
        kr = (kh.reshape(bb, tl, dh) * cosf
              + pltpu.roll(kh, dh // 2, 1).reshape(bb, tl, dh) * sinf).reshape(rows, dh)
        q_ref[:, hh * dh:(hh + 1) * dh] = qr.astype(BF16)
        k_ref[:, hh * dh:(hh + 1) * dh] = kr * k_scale
    v_ref[...] = project(2 * rw_width, 3 * rw_width).astype(BF16)
    gate = project(3 * rw_width, 4 * rw_width)
    u = project(4 * rw_width, 4 * rw_width + pw)

    def ret_block(b, c):
        r0 = b * tl + c * chunk
        if not isinstance(r0, int):
            r0 = pl.multiple_of(r0, chunk)
        for hh in range(RET_HEADS):
            cs = slice(hh * dh, (hh + 1) * dh)
            qc = q_ref[pl.ds(r0, chunk), cs]
            kf = k_ref[pl.ds(r0, chunk), cs]
            vc = v_ref[pl.ds(r0, chunk), cs]
            s_old = st_ref[b, hh]
            sc = lax.dot_general(qc, kf.astype(BF16), (((1,), (1,)), ((), ())),
                                 preferred_element_type=F32) * dintra_ref[hh]
            o = (jnp.dot(sc.astype(BF16), vc, preferred_element_type=F32)
                 + dq_ref[hh] * jnp.dot(qc, s_old.astype(BF16), preferred_element_type=F32))
            kd = (kf * dk_ref[hh]).astype(BF16)
            s_new = dc_ref[hh] * s_old + lax.dot_general(
                kd, vc, (((0,), (0,)), ((), ())), preferred_element_type=F32)
            st_ref[b, hh] = s_new
            o_ref[pl.ds(r0, chunk), cs] = o

    if bb * n_chunks <= 4:
        for b in range(bb):
            for c in range(n_chunks):
                ret_block(b, c)
    else:
        def body(i, carry):
            ret_block(i // n_chunks, i % n_chunks)
            return carry
        lax.fori_loop(0, bb * n_chunks, body, 0)

    for hh in range(RET_HEADS):
        cs = slice(hh * dh, (hh + 1) * dh)
        oh = o_ref[:, cs]
        mu = jnp.mean(oh, axis=-1, keepdims=True)
        oc = oh - mu
        var = jnp.mean(oc * oc, axis=-1, keepdims=True)
        y = oc * lax.rsqrt(var + EPS) * gret_ref[:, cs]
        g = gate[:, cs]
        a_ref[:, cs] = (g * _sigmoid(g) * y).astype(BF16)

    ue_ref[:, HIST_ROWS:HIST_ROWS + tl, :] = u.reshape(bb, tl, pw)
    pos = pos0 + l_idx * tl + lax.broadcasted_iota(I32, (1, tl, 1), 1)
    for gi, w in enumerate(POOL_WINDOWS):
        cs = slice(gi * gw, (gi + 1) * gw)
        acc = ue_ref[:, HIST_ROWS:HIST_ROWS + tl, cs]
        for j in range(1, w):
            acc = acc + ue_ref[:, HIST_ROWS - j:HIST_ROWS - j + tl, cs]
        inv_cnt = 1.0 / jnp.minimum(pos + 1, w).astype(F32)
        p = (acc * inv_cnt).reshape(rows, gw) - u[:, cs]
        z = jnp.dot(p.astype(BF16), wpool_ref[gi], preferred_element_type=F32) * pscale_ref[:, cs]
        a_ref[:, rw_width + gi * gw:rw_width + (gi + 1) * gw] = z.astype(BF16)
    tail = ue_ref[:, tl:tl + HIST_ROWS, :]
    ue_ref[:, 0:HIST_ROWS, :] = tail
    hist_ref[...] = tail

    x1 = x + jnp.dot(a_ref[...], wout_ref[...], preferred_element_type=F32)
    x1_ref[...] = x1.reshape(bb, tl, d_model)
    h2 = _rms(x1, g2_ref[...])
    h2_ref[...] = _pack_bf16_pair(h2[:, 0:d_model // 2], h2[:, d_model // 2:]).reshape(bb, tl, d_model // 2)

    h2_hi = h2.astype(BF16)
    h2_lo = (h2 - h2_hi.astype(F32)).astype(BF16)
    two = jnp.dot(h2_hi, wr_ref[...], preferred_element_type=F32)
    logits = (two[:, 0:LANES] + two[:, LANES:2 * LANES]
              + jnp.dot(h2_lo, wr_ref[:, 0:LANES], preferred_element_type=F32))
    lt = logits.T
    neg = jnp.float32(-jnp.inf)
    big = jnp.float32(1e9)
    sub = lax.broadcasted_iota(I32, (EXPERTS_PER_GROUP, rows), 0).astype(F32)
    gl = jnp.where(sub < N_EXPERT_GROUPS, lt[N_EXPERTS:N_EXPERTS + EXPERTS_PER_GROUP], neg)
    gmax = jnp.max(gl, axis=0, keepdims=True)
    gidx = jnp.min(jnp.where(gl == gmax, sub, big), axis=0, keepdims=True)
    p_sel = 1.0 / jnp.sum(jnp.exp(gl - gmax), axis=0, keepdims=True)
    el = lt[0:EXPERTS_PER_GROUP]
    for g in range(1, N_EXPERT_GROUPS):
        el = jnp.where(gidx == g, lt[g * EXPERTS_PER_GROUP:(g + 1) * EXPERTS_PER_GROUP], el)
    m1 = jnp.max(el, axis=0, keepdims=True)
    t1 = jnp.min(jnp.where(el == m1, sub, big), axis=0, keepdims=True)
    el2 = jnp.where(sub == t1, neg, el)
    m2 = jnp.max(el2, axis=0, keepdims=True)
    t2 = jnp.min(jnp.where(el2 == m2, sub, big), axis=0, keepdims=True)
    e2 = jnp.exp(m2 - m1)
    w1 = p_sel / (1.0 + e2)
    w2 = p_sel * e2 / (1.0 + e2)
    i1 = gidx * EXPERTS_PER_GROUP + t1
    i2 = gidx * EXPERTS_PER_GROUP + t2

    eid = lax.broadcasted_iota(I32, (N_EXPERTS, rows), 0).astype(F32)
    hit1 = eid == i1
    hit2 = eid == i2
    onehot = (hit1 | hit2).astype(BF16)
    before = jnp.dot(onehot, tri_ref[...], preferred_element_type=F32) + cnt_ref[...]
    r1 = jnp.sum(jnp.where(hit1, before, 0.0), axis=0, keepdims=True)
    r2 = jnp.sum(jnp.where(hit2, before, 0.0), axis=0, keepdims=True)
    cnt_ref[...] = cnt_ref[...] + jnp.sum(onehot.astype(F32), axis=1, keepdims=True)

    ri = jnp.where(sub == 0, i1, jnp.where(sub == 1, i2, jnp.where(sub == 2, r1, jnp.where(sub == 3, r2, 0.0))))
    ri_ref[...] = ri.astype(I32).reshape(ri_ref.shape)
    rw_ref[...] = jnp.where(sub == 0, w1, jnp.where(sub == 1, w2, 0.0)).reshape(rw_ref.shape)


def _rope_tables(pos0, seq, tl, dh):
    half = dh // 2
    inv = ROPE_BASE ** (-jnp.arange(half, dtype=F32) / half)
    ang_t = jnp.arange(tl, dtype=F32)[:, None] * inv[None, :]
    ang_b = (pos0 + tl * jnp.arange(seq // tl)).astype(F32)[:, None] * inv[None, :]
    dup = lambda a: jnp.concatenate([a, a], axis=-1)
    sgn = lambda a: jnp.concatenate([-a, a], axis=-1)
    base = jnp.stack([dup(jnp.cos(ang_b)), dup(jnp.sin(ang_b))], axis=1)
    base = jnp.pad(base, ((0, 0), (0, 8 - base.shape[1]), (0, 0)))
    cos_t, sin_t = jnp.cos(ang_t), jnp.sin(ang_t)
    return base, dup(cos_t), dup(sin_t), sgn(cos_t), sgn(sin_t)


def _layer_call(x, b0, nb, s0, h0, pos0, consts, after, *, bb, tl, chunk):
    _, seq, d_model = x.shape
    bsz = nb
    blk0 = b0 // bb
    rows = bb * tl
    rw_width = consts["gret"].shape[-1]
    pw = consts["pscale"].shape[-1]
    dh = rw_width // RET_HEADS

    rope = _rope_tables(pos0, seq, tl, dh)

    lg = jnp.log1p(-jnp.exp2(-5.0 - jnp.arange(RET_HEADS, dtype=F32)))
    idx = jnp.arange(chunk, dtype=F32)
    diff = idx[:, None] - idx[None, :]
    d_intra = jnp.where(diff[None] >= 0, jnp.exp(jnp.maximum(diff, 0.0)[None] * lg[:, None, None]), 0.0)
    d_q = jnp.broadcast_to(jnp.exp((idx + 1.0)[None, :] * lg[:, None])[:, :, None], (RET_HEADS, chunk, dh))
    d_k = jnp.broadcast_to(jnp.exp((chunk - 1.0 - idx)[None, :] * lg[:, None])[:, :, None], (RET_HEADS, chunk, dh))
    d_c = jnp.exp(chunk * lg)
    tri = jnp.triu(jnp.ones((rows, rows), BF16), 1)

    const2 = lambda b, l, *_: (0, 0)
    const3 = lambda b, l, *_: (0, 0, 0)
    grid_spec = pltpu.PrefetchScalarGridSpec(
        num_scalar_prefetch=0,
        grid=(bsz // bb, seq // tl),
        in_specs=[
            pl.BlockSpec(memory_space=pltpu.SMEM),
            pl.BlockSpec((bb, tl, d_model), lambda b, l: (blk0 + b, l, 0)),
            pl.BlockSpec((bb, RET_HEADS, dh, dh), lambda b, l: (b, 0, 0, 0)),
            pl.BlockSpec((bb, HIST_ROWS, pw), lambda b, l: (b, 0, 0)),
            pl.BlockSpec((1, 8, dh), lambda b, l: (l, 0, 0)),
            pl.BlockSpec((tl, dh), const2),
            pl.BlockSpec((tl, dh), const2),
            pl.BlockSpec((tl, dh), const2),
            pl.BlockSpec((tl, dh), const2),
            pl.BlockSpec((RET_HEADS, chunk, chunk), const3),
            pl.BlockSpec((RET_HEADS, chunk, dh), const3),
            pl.BlockSpec((RET_HEADS, chunk, dh), const3),
            pl.BlockSpec((1, d_model), const2),
            pl.BlockSpec(consts["w_in"].shape, const2),
            pl.BlockSpec((1, rw_width), const2),
            pl.BlockSpec(consts["w_pool"].shape, const3),
            pl.BlockSpec((1, pw), const2),
            pl.BlockSpec(consts["w_out"].shape, const2),
            pl.BlockSpec((1, d_model), const2),
            pl.BlockSpec((d_model, 2 * LANES), const2),
            pl.BlockSpec((rows, rows), const2),
            pl.BlockSpec(memory_space=pl.ANY),
        ],
        out_specs=[
            pl.BlockSpec((bb, tl, d_model), lambda b, l: (b, l, 0)),
            pl.BlockSpec((bb, tl, d_model // 2), lambda b, l: (b, l, 0)),
            pl.BlockSpec((1, 1, EXPERTS_PER_GROUP, rows), lambda b, l: (b, l, 0, 0)),
            pl.BlockSpec((1, 1, EXPERTS_PER_GROUP, rows), lambda b, l: (b, l, 0, 0)),
            pl.BlockSpec((bb, RET_HEADS, dh, dh), lambda b, l: (b, 0, 0, 0)),
            pl.BlockSpec((bb, HIST_ROWS, pw), lambda b, l: (b, 0, 0)),
            pl.BlockSpec((N_EXPERTS, rows), const2),
        ],
        scratch_shapes=[
            pltpu.VMEM((bb, HIST_ROWS + tl, pw), F32),
            pltpu.VMEM((rows, rw_width), BF16),
            pltpu.VMEM((rows, rw_width), F32),
            pltpu.VMEM((rows, rw_width), BF16),
            pltpu.VMEM((rows, rw_width), F32),
            pltpu.VMEM((rows, d_model), BF16),
        ],
    )
    out_shape = [
        jax.ShapeDtypeStruct((bsz, seq, d_model), F32),
        jax.ShapeDtypeStruct((bsz, seq, d_model // 2), U32),
        jax.ShapeDtypeStruct((bsz // bb, seq // tl, EXPERTS_PER_GROUP, rows), I32),
        jax.ShapeDtypeStruct((bsz // bb, seq // tl, EXPERTS_PER_GROUP, rows), F32),
        jax.ShapeDtypeStruct((bsz, RET_HEADS, dh, dh), F32),
        jax.ShapeDtypeStruct((bsz, HIST_ROWS, pw), F32),
        jax.ShapeDtypeStruct((N_EXPERTS, rows), F32),
    ]
    kern = functools.partial(_layer_kernel, bb=bb, tl=tl, chunk=chunk, pos0=pos0)
    operands = (d_c, x, s0, h0, *rope, d_intra, d_q, d_k,
                consts["g1"], consts["w_in"], consts["gret"], consts["w_pool"], consts["pscale"],
                consts["w_out"], consts["g2"], consts["wr"], tri, after)
    n_tok = bsz * seq
    mm_flops_per_token = 2 * (d_model * consts["w_in"].shape[1] + d_model * d_model + 2 * d_model * LANES + pw * pw // 4
                              + rw_width * (2 * chunk + 2 * dh) + N_EXPERTS * rows)
    cost = pl.CostEstimate(
        flops=n_tok * mm_flops_per_token, transcendentals=n_tok * (rw_width + 2 * N_EXPERT_GROUPS),
        bytes_accessed=_nbytes(*operands) - _nbytes(x, after) + n_tok * d_model * 4 + _nbytes(*out_shape))
    return pl.pallas_call(
        kern, grid_spec=grid_spec, out_shape=out_shape, name=f"layer_pos{pos0}_b{b0}", cost_estimate=cost,
        compiler_params=pltpu.CompilerParams(
            dimension_semantics=("arbitrary", "arbitrary"), vmem_limit_bytes=VMEM_LIMIT),
    )(*operands)


def _sc_partition(n_units):
    info = plsc.get_sparse_core_info()
    nc, nw = info.num_cores, info.num_cores * info.num_subcores
    upw = -(-n_units // nw)
    upw += upw % 2
    return nc, nw, upw


def _units_by_worker(idx, n_units, upw, nw):
    idx = jnp.pad(idx.reshape(n_units, SC_UNIT), ((0, nw * upw - n_units), (0, 0)))
    return idx.reshape(upw, nw, SC_UNIT).transpose(1, 0, 2)


def _sc_dispatch(srcs, idx0, idx1, n_out_rows, after=None):
    assert 1 <= len(srcs) <= 2
    d = srcs[0].shape[1]
    dtype = srcs[0].dtype
    assert all(src.shape[0] % SC_UNIT == 0 for src in srcs)
    units_a = srcs[0].shape[0] // SC_UNIT
    n_units = sum(src.shape[0] for src in srcs) // SC_UNIT
    nc, nw, upw = _sc_partition(n_units)
    idx0 = _units_by_worker(idx0, n_units, upw, nw)
    idx1 = _units_by_worker(idx1, n_units, upw, nw)
    mesh = plsc.VectorSubcoreMesh(core_axis_name="c", subcore_axis_name="s")
    dma = pltpu.SemaphoreType.DMA
    extra = [] if after is None else [after]

    moved = n_units * SC_UNIT * d * jnp.dtype(dtype).itemsize
    @functools.partial(
        pl.kernel, mesh=mesh,
        cost_estimate=pl.CostEstimate(flops=0, transcendentals=0, bytes_accessed=3 * moved + _nbytes(idx0, idx1)),
        out_type=jax.ShapeDtypeStruct((n_out_rows, d), dtype),
        scratch_types=[
            pltpu.VMEM((upw, SC_UNIT), I32),
            pltpu.VMEM((upw, SC_UNIT), I32),
            pltpu.VMEM((SC_UNIT, d), dtype),
            pltpu.VMEM((SC_UNIT, d), dtype),
            dma, dma, dma, dma, dma, dma,
        ],
    )
    def k(*refs):
        src_hbm = refs[:len(srcs)]
        i0_hbm, i1_hbm, out_hbm, i0_v, i1_v, rows0, rows1, l0, l1, p0, p1, q0, q1 = refs[len(srcs) + len(extra):]
        wid = lax.axis_index("s") * nc + lax.axis_index("c")
        pltpu.sync_copy(i0_hbm.at[wid], i0_v)
        pltpu.sync_copy(i1_hbm.at[wid], i1_v)
        rows, lsem, psem, qsem = (rows0, rows1), (l0, l1), (p0, p1), (q0, q1)

        def live(j):
            return j * nw + wid < n_units

        def load(j, b, op):
            unit = j * nw + wid

            @pl.when(live(j) & (unit < units_a))
            def _():
                op(pltpu.make_async_copy(
                    src_hbm[0].at[pl.ds(pl.multiple_of(unit * SC_UNIT, 8), SC_UNIT)], rows[b], lsem[b]))

            if len(srcs) == 2:
                @pl.when(live(j) & (unit >= units_a))
                def _():
                    op(pltpu.make_async_copy(
                        src_hbm[1].at[pl.ds(pl.multiple_of((unit - units_a) * SC_UNIT, 8), SC_UNIT)],
                        rows[b], lsem[b]))

        def scatter(j, b, op):
            @pl.when(live(j))
            def _():
                op(pltpu.make_async_copy(rows[b], out_hbm.at[i0_v.at[j]], psem[b]))
                op(pltpu.make_async_copy(rows[b], out_hbm.at[i1_v.at[j]], qsem[b]))

        start = lambda c: c.start()
        wait = lambda c: c.wait()
        load(0, 0, start)

        @pl.loop(0, upw, step=2)
        def _(j):
            @pl.when(j > 0)
            def _():
                scatter(j - 1, 1, wait)
            load(j + 1, 1, start)
            load(j, 0, wait)
            scatter(j, 0, start)
            scatter(j, 0, wait)

            @pl.when(j + 2 < upw)
            def _():
                load(j + 2, 0, start)
            load(j + 1, 1, wait)
            scatter(j + 1, 1, start)

        scatter(upw - 1, 1, wait)

    return k(*srcs, *extra, idx0, idx1), idx1


def _sc_gather(table, idx):
    n = idx.shape[0]
    d = table.shape[1]
    assert n % SC_UNIT == 0
    n_units = n // SC_UNIT
    nc, nw, upw = _sc_partition(n_units)
    idx = _units_by_worker(idx, n_units, upw, nw)
    mesh = plsc.VectorSubcoreMesh(core_axis_name="c", subcore_axis_name="s")
    dma = pltpu.SemaphoreType.DMA

    @functools.partial(
        pl.kernel, mesh=mesh,
        cost_estimate=pl.CostEstimate(flops=0, transcendentals=0,
                                      bytes_accessed=2 * n * d * table.dtype.itemsize + _nbytes(idx)),
        out_type=jax.ShapeDtypeStruct((n, d), table.dtype),
        scratch_types=[
            pltpu.VMEM((upw, SC_UNIT), I32),
            pltpu.VMEM((SC_UNIT, d), table.dtype),
            pltpu.VMEM((SC_UNIT, d), table.dtype),
            dma, dma, dma, dma,
        ],
    )
    def k(t_hbm, i_hbm, out_hbm, i_v, rows0, rows1, g0, g1, w0, w1):
        wid = lax.axis_index("s") * nc + lax.axis_index("c")
        pltpu.sync_copy(i_hbm.at[wid], i_v)
        rows, gsem, wsem = (rows0, rows1), (g0, g1), (w0, w1)

        def live(j):
            return j * nw + wid < n_units

        def gather(j, b, op):
            @pl.when(live(j))
            def _():
                op(pltpu.make_async_copy(t_hbm.at[i_v.at[j]], rows[b], gsem[b]))

        def write(j, b, op):
            @pl.when(live(j))
            def _():
                op(pltpu.make_async_copy(
                    rows[b], out_hbm.at[pl.ds(pl.multiple_of((j * nw + wid) * SC_UNIT, 8), SC_UNIT)], wsem[b]))

        start = lambda c: c.start()
        wait = lambda c: c.wait()
        gather(0, 0, start)

        @pl.loop(0, upw, step=2)
        def _(j):
            @pl.when(j > 0)
            def _():
                write(j - 1, 1, wait)
            gather(j + 1, 1, start)
            gather(j, 0, wait)
            write(j, 0, start)
            write(j, 0, wait)

            @pl.when(j + 2 < upw)
            def _():
                gather(j + 2, 0, start)
            gather(j + 1, 1, wait)
            write(j + 1, 1, start)

        write(upw - 1, 1, wait)

    return k(table, idx)


def _moe_kernel(start0_ref, count0_ref, gtot0_ref, start1_ref, count1_ref, gtot1_ref,
                xs0_hbm, xs1_hbm, wg_ref, wu_ref, wd_ref, ys0_hbm, ys1_hbm,
                wgu_s, wd_s, xbuf0, ybuf0, xbuf1, ybuf1, sem_in0, sem_out0, sem_in1, sem_out1):
    e = pl.program_id(0)
    last = pl.num_programs(0) - 1
    hidden = wd_s.shape[0]
    half = xbuf0.shape[-1]
    segments = (
        (xs0_hbm, ys0_hbm, xbuf0, ybuf0, sem_in0, sem_out0, start0_ref[e], count0_ref[e], gtot0_ref[0]),
        (xs1_hbm, ys1_hbm, xbuf1, ybuf1, sem_in1, sem_out1, start1_ref[e], count1_ref[e], gtot1_ref[0]),
    )

    def rows_of(g):
        return pl.ds(pl.multiple_of(g * MOE_TILE, MOE_TILE), MOE_TILE)

    def pipeline(xs_hbm, ys_hbm, xbuf, ybuf, sem_in, sem_out):
        def copy_in(g):
            slot = g % MOE_BUFFERS
            return pltpu.make_async_copy(xs_hbm.at[rows_of(g)], xbuf.at[slot], sem_in.at[slot])

        def copy_out(g):
            slot = g % MOE_BUFFERS
            return pltpu.make_async_copy(ybuf.at[slot], ys_hbm.at[rows_of(g)], sem_out.at[slot])
        return copy_in, copy_out

    @pl.when(e == 0)
    def _():
        for xs_hbm, ys_hbm, xbuf, ybuf, sem_in, sem_out, _, _, g_total in segments:
            copy_in, _ = pipeline(xs_hbm, ys_hbm, xbuf, ybuf, sem_in, sem_out)
            for g in range(MOE_LOOKAHEAD):
                @pl.when(g < g_total)
                def _():
                    copy_in(g).start()

    @pl.when(segments[0][7] + segments[1][7] > 0)
    def _():
        wgu_s[:, 0:hidden] = wg_ref[0].astype(BF16)
        wgu_s[:, hidden:2 * hidden] = wu_ref[0].astype(BF16)
        wd_s[...] = wd_ref[0].astype(BF16)

    def expert_rows(xbuf, ybuf, slot, valid):
        row = lax.broadcasted_iota(I32, (MOE_TILE, half), 0)
        x_lo, x_hi = _unpack_bf16_pair(jnp.where(row < valid, xbuf[slot], jnp.uint32(0)))
        ab = (jnp.dot(x_lo.astype(BF16), wgu_s[0:half, :], preferred_element_type=F32)
              + jnp.dot(x_hi.astype(BF16), wgu_s[half:2 * half, :], preferred_element_type=F32))
        a = ab[:, 0:hidden]
        he = a * _sigmoid(a) * ab[:, hidden:2 * hidden]
        y = jnp.dot(he.astype(BF16), wd_s[...], preferred_element_type=F32)
        ybuf[slot] = _pack_bf16_pair(y[:, 0:half], y[:, half:2 * half])

    for xs_hbm, ys_hbm, xbuf, ybuf, sem_in, sem_out, start, count, g_total in segments:
        copy_in, copy_out = pipeline(xs_hbm, ys_hbm, xbuf, ybuf, sem_in, sem_out)
        g_first = start // MOE_TILE
        n_tiles = (count + MOE_TILE - 1) // MOE_TILE

        def tiles(t, width, copy_in=copy_in, copy_out=copy_out, xbuf=xbuf, ybuf=ybuf,
                  g_first=g_first, count=count, g_total=g_total):
            gs = [g_first + t + i for i in range(width)]
            for g in gs:
                @pl.when(g + MOE_LOOKAHEAD < g_total)
                def _():
                    copy_in(g + MOE_LOOKAHEAD).start()
            for g in gs:
                copy_in(g).wait()

                @pl.when(g >= MOE_BUFFERS)
                def _():
                    copy_out(g - MOE_BUFFERS).wait()
            for i, g in enumerate(gs):
                expert_rows(xbuf, ybuf, g % MOE_BUFFERS, count - (t + i) * MOE_TILE)
            for g in gs:
                copy_out(g).start()

        def pair(p, carry, tiles=tiles):
            tiles(MOE_UNROLL * p, MOE_UNROLL)
            return carry

        lax.fori_loop(0, n_tiles // MOE_UNROLL, pair, 0)

        def single(r, carry, tiles=tiles, n_tiles=n_tiles):
            tiles(n_tiles // MOE_UNROLL * MOE_UNROLL + r, 1)
            return carry

        lax.fori_loop(0, n_tiles % MOE_UNROLL, single, 0)

        @pl.when(e == last)
        def _(copy_out=copy_out, g_total=g_total):
            for j in range(1, MOE_BUFFERS + 1):
                @pl.when(g_total >= j)
                def _():
                    copy_out(g_total - j).wait()


def _moe_call(groups, w_g, w_u, w_d):
    (xs0, starts0, cnt0), (xs1, starts1, cnt1) = groups
    half = xs0.shape[1]
    n_experts, d_model, hidden = w_g.shape

    def tiles_total(starts, cnt):
        return ((starts[-1:] + cnt[-1:] + MOE_TILE - 1) // MOE_TILE).astype(I32)

    wspec = lambda shape: pl.BlockSpec(shape, lambda e, *_: (e, 0, 0))
    tile_bufs = [pltpu.VMEM((MOE_BUFFERS, MOE_TILE, half), U32)] * 4
    grid_spec = pltpu.PrefetchScalarGridSpec(
        num_scalar_prefetch=6,
        grid=(n_experts,),
        in_specs=[
            pl.BlockSpec(memory_space=pl.ANY),
            pl.BlockSpec(memory_space=pl.ANY),
            wspec((1, d_model, hidden)),
            wspec((1, d_model, hidden)),
            wspec((1, hidden, d_model)),
        ],
        out_specs=[pl.BlockSpec(memory_space=pl.ANY), pl.BlockSpec(memory_space=pl.ANY)],
        scratch_shapes=[
            pltpu.VMEM((d_model, 2 * hidden), BF16),
            pltpu.VMEM((hidden, d_model), BF16),
            *tile_bufs,
            *[pltpu.SemaphoreType.DMA((MOE_BUFFERS,))] * 4,
        ],
    )
    n_rows = xs0.shape[0] + xs1.shape[0]
    cost = pl.CostEstimate(flops=n_rows * 6 * d_model * hidden, transcendentals=n_rows * hidden,
                           bytes_accessed=2 * _nbytes(xs0, xs1) + _nbytes(w_g, w_u, w_d))
    return pl.pallas_call(
        _moe_kernel, grid_spec=grid_spec, cost_estimate=cost,
        out_shape=[jax.ShapeDtypeStruct(xs0.shape, U32), jax.ShapeDtypeStruct(xs1.shape, U32)], name="moe_experts",
        compiler_params=pltpu.CompilerParams(
            dimension_semantics=("arbitrary",), vmem_limit_bytes=VMEM_LIMIT),
    )(starts0, cnt0, tiles_total(starts0, cnt0), starts1, cnt1, tiles_total(starts1, cnt1),
      xs0, xs1, w_g, w_u, w_d)


def _combine_kernel(x1_ref, y0_ref, y1_ref, rw_ref, gf_ref, *rest):
    out_ref = rest[-1]
    tr = x1_ref.shape[0]
    w_rows = jnp.concatenate([rw_ref[0], jnp.zeros((LANES - rw_ref.shape[1], tr), F32)], axis=0)
    w_cols = w_rows.T
    w0, w1 = w_cols[:, 0:1], w_cols[:, 1:2]
    a_lo, a_hi = _unpack_bf16_pair(y0_ref[0])
    b_lo, b_hi = _unpack_bf16_pair(y1_ref[0])
    moe = jnp.concatenate([w0 * a_lo + w1 * b_lo, w0 * a_hi + w1 * b_hi], axis=-1)
    out_ref[...] = _rms(x1_ref[...] + moe, gf_ref[...])


def _combine_call(x1, rw, row0, n, yg, gf, out_rows, out_row0, prev_out=None):
    t, d_model = x1.shape
    tr = rw.shape[-1]
    half = yg.shape[-1]
    assert t % tr == 0 and row0 % tr == 0 and n % tr == 0 and rw.shape == (t // tr, EXPERTS_PER_GROUP, tr)
    assert yg.shape == (2, n, half) and out_row0 % tr == 0
    off = row0 // tr
    ooff = out_row0 // tr
    in_specs = [
        pl.BlockSpec((tr, d_model), lambda i: (off + i, 0)),
        pl.BlockSpec((1, tr, half), lambda i: (0, i, 0)),
        pl.BlockSpec((1, tr, half), lambda i: (1, i, 0)),
        pl.BlockSpec((1, EXPERTS_PER_GROUP, tr), lambda i: (off + i, 0, 0)),
        pl.BlockSpec((1, d_model), lambda i: (0, 0)),
    ]
    args = [x1, yg, yg, rw, gf]
    aliases = {}
    if prev_out is not None:
        in_specs.append(pl.BlockSpec(memory_space=pl.ANY))
        args.append(prev_out)
        aliases = {len(args) - 1: 0}
    return pl.pallas_call(
        _combine_kernel,
        grid=(n // tr,),
        in_specs=in_specs,
        out_specs=pl.BlockSpec((tr, d_model), lambda i: (ooff + i, 0)),
        out_shape=jax.ShapeDtypeStruct((out_rows, d_model), F32), name=f"combine_row{out_row0}_of{out_rows}",
        cost_estimate=pl.CostEstimate(flops=8 * n * d_model, transcendentals=n,
                                      bytes_accessed=2 * n * d_model * 4 + _nbytes(yg) + n * 4 * EXPERTS_PER_GROUP),
        input_output_aliases=aliases,
        compiler_params=pltpu.CompilerParams(
            dimension_semantics=("arbitrary",), vmem_limit_bytes=VMEM_LIMIT),
    )(*args)


def _route(streams, after=None):
    tokens = [h2.shape[0] for h2, _, _ in streams]
    counts = [cnt[:, 0].astype(I32) for _, _, cnt in streams]
    total = sum(counts)
    padded = ((total + MOE_TILE - 1) // MOE_TILE) * MOE_TILE
    starts = (jnp.cumsum(padded) - padded).astype(I32)
    experts = jnp.arange(N_EXPERTS, dtype=I32)[None, :, None]
    pos, base = [], starts
    for (_, ri, _), t, cnt in zip(streams, tokens, counts):
        ri = jnp.moveaxis(ri, 2, 0).reshape(ri.shape[2], t)
        first_row = jnp.sum(jnp.where(ri[0:2, None, :] == experts, base[None, :, None], 0), axis=1)
        pos.append(ri[2:4] + first_row)
        base = base + cnt
    pos = jnp.concatenate(pos, axis=1)
    n_rows = ((2 * sum(tokens) + N_EXPERTS * (MOE_TILE - 1)) // MOE_TILE) * MOE_TILE
    xs_sorted, ready = _sc_dispatch([h2 for h2, _, _ in streams], pos[0], pos[1], n_rows, after)
    return (xs_sorted, starts, total), pos, ready


def _gather_tokens(ys_sorted, pos, t0, n):
    return _sc_gather(ys_sorted, pos[:, t0:t0 + n].reshape(2 * n)).reshape(2, n, ys_sorted.shape[-1])


def _one_layer(xp, xs, s_ret, c_pool, norm1_g, w_in, ret_norm_g, w_pool, pool_scale, w_out, norm2_g,
               w_rg, w_re, w_g, w_u, w_d, final_g, past_len):
    bp, seq, d_model = xp.shape
    bs, dseq, _ = xs.shape
    rw_width = ret_norm_g.shape[-1]
    pw = pool_scale.shape[-1]
    dh = rw_width // RET_HEADS
    half = d_model // 2

    w_r = jnp.concatenate(
        [w_re, w_rg, jnp.zeros((d_model, LANES - N_EXPERTS - N_EXPERT_GROUPS), F32)], axis=1)
    wr_hi = w_r.astype(BF16)
    wr = jnp.concatenate([wr_hi, (w_r - wr_hi.astype(F32)).astype(BF16)], axis=1)
    consts = dict(
        g1=norm1_g.reshape(1, d_model), w_in=w_in.astype(BF16), gret=ret_norm_g.reshape(1, rw_width),
        w_pool=w_pool.astype(BF16), pscale=pool_scale.reshape(1, pw), w_out=w_out.astype(BF16),
        g2=norm2_g.reshape(1, d_model), wr=wr)

    gf = final_g.reshape(1, d_model)

    ts = bs * dseq
    b_lead = bp - 1
    t_lead, t_rest = b_lead * seq, (bp - b_lead) * seq
    zeros = lambda nb: (jnp.zeros((nb, RET_HEADS, dh, dh), F32), jnp.zeros((nb, HIST_ROWS, pw), F32))
    h0s = jnp.pad(c_pool, ((0, 0), (HIST_ROWS - POOL_HIST, 0), (0, 0)))
    prompt_tile = dict(bb=1, tl=PROMPT_TILE, chunk=256)

    def stream(layer_out, t):
        x1, h2, ri, rw, st, hist, cnt = layer_out
        return dict(x1=x1.reshape(t, d_model), route=(h2.reshape(t, half), ri, cnt),
                    rw=rw.reshape(-1, EXPERTS_PER_GROUP, rw.shape[-1]), st=st, hist=hist)

    pa = stream(_layer_call(xp, 0, b_lead, *zeros(b_lead), 0, consts, gf, **prompt_tile), t_lead)
    group0, pos0, ready0 = _route([pa["route"]])
    pb = stream(_layer_call(xp, b_lead, bp - b_lead, *zeros(bp - b_lead), 0, consts, ready0, **prompt_tile),
                t_rest)
    sm = stream(_layer_call(xs, 0, bs, s_ret, h0s, past_len, consts, pb["route"][2],
                            bb=bs, tl=dseq, chunk=min(64, dseq)), ts)
    group1, pos1, _ = _route([pb["route"], sm["route"]], after=group0[0])
    ys0, ys1 = _moe_call([group0, group1], w_g, w_u, w_d)

    tp = bp * seq
    yp = _combine_call(pb["x1"], pb["rw"], 0, t_rest, _gather_tokens(ys1, pos1, 0, t_rest), gf, tp, t_lead)
    ysm = _combine_call(sm["x1"], sm["rw"], 0, ts, _gather_tokens(ys1, pos1, t_rest, ts), gf, ts, 0)
    row0 = 0
    for nb in COMBINE_CHUNKS:
        n = min(nb * seq, t_lead - row0)
        if n > 0:
            yp = _combine_call(pa["x1"], pa["rw"], row0, n, _gather_tokens(ys0, pos0, row0, n), gf, tp, row0,
                               prev_out=yp)
            row0 += n
    assert row0 == t_lead
    st_p = jnp.concatenate([pa["st"], pb["st"]], axis=0)
    hist_p = jnp.concatenate([pa["hist"], pb["hist"]], axis=0)
    return (yp.reshape(bp, seq, d_model), ysm.reshape(bs, dseq, d_model),
            st_p, hist_p[:, HIST_ROWS - POOL_HIST:], sm["st"], sm["hist"][:, HIST_ROWS - POOL_HIST:])


def kernel(x_prompt, x_sample, state_ret, cache_pool, norm1_g, w_in, ret_norm_g, w_pool, pool_scale, w_out,
           norm2_g, w_router_group, w_router_expert, w_exp_gate, w_exp_up, w_exp_down, final_norm_g):
    depth = w_in.shape[0]
    assert depth == 1, "the final RMSNorm is fused into the layer's combine kernel"
    assert x_prompt.shape[0] >= 2 and x_prompt.shape[1] % PROMPT_TILE == 0
    yp, ys, s_p, h_p, s_s, h_s = _one_layer(
        x_prompt, x_sample, state_ret[0], cache_pool[0], norm1_g[0], w_in[0], ret_norm_g[0], w_pool[0],
        pool_scale[0], w_out[0], norm2_g[0], w_router_group[0], w_router_expert[0],
        w_exp_gate[0], w_exp_up[0], w_exp_down[0], final_norm_g, PAST_LEN)
    return (yp, ys, s_p[None], h_p[None], s_s[None], h_s[None])
```

```python
import functools

import jax
import jax.numpy as jnp
from jax import lax
from jax.experimental import pallas as pl
from jax.experimental.pallas import tpu as pltpu
from jax.experimental.pallas import tpu_sc as plsc

F32 = jnp.float32
BF16 = jnp.bfloat16
I32 = jnp.int32
U32 = jnp.uint32

EPS = 1e-6
ROPE_BASE = 10000.0
RET_HEADS = 4
POOL_WINDOWS = (2, 4, 8, 16)
POOL_HIST = max(POOL_WINDOWS) - 1
N_EXPERT_GROUPS = 4
EXPERTS_PER_GROUP = 8
N_EXPERTS = N_EXPERT_GROUPS * EXPERTS_PER_GROUP
ROUTER_ROWS = 48
PAST_LEN = 1024

LANES = 128
HIST_ROWS = 16
MOE_TILE = 256
MOE_BUFFERS = 6
MOE_UNROLL = 2
MOE_LOOKAHEAD = MOE_BUFFERS - MOE_UNROLL
PREP_ROWS = 512
PROMPT_TILE = 1024
COMBINE_CHUNKS = (1, 2)
SC_UNIT = 32
VMEM_LIMIT = 56 * 1024 * 1024


def _nbytes(*arrays):
    return sum(a.size * a.dtype.itemsize for a in arrays)


def _rms(x, g):
    return x * lax.rsqrt(jnp.mean(x * x, axis=-1, keepdims=True) + EPS) * g


def _sigmoid(x):
    return 1.0 / (1.0 + jnp.exp(-x))


def _pack_bf16_pair(lo, hi):
    lo_b = lax.bitcast_convert_type(lo.astype(BF16).astype(F32), U32)
    hi_b = lax.bitcast_convert_type(hi.astype(BF16).astype(F32), U32)
    return hi_b | (lo_b >> 16)


def _unpack_bf16_pair(p):
    lo = lax.bitcast_convert_type(p << 16, F32)
    hi = lax.bitcast_convert_type(p & jnp.uint32(0xFFFF0000), F32)
    return lo, hi


def _layer_kernel(dc_ref, x_ref, s0_ref, h0_ref, rb_ref, rc_ref, rs_ref, rcs_ref, rss_ref,
                  dintra_ref, dq_ref, dk_ref,
                  g1_ref, win_ref, gret_ref, wpool_ref, pscale_ref, wout_ref, g2_ref,
                  wr_ref, tri_ref, after_ref,
                  x1_ref, h2_ref, ri_ref, rw_ref, st_ref, hist_ref, cnt_ref,
                  ue_ref, q_ref, k_ref, v_ref, gate_ref, o_ref, a_ref,
                  *, bb, tl, chunk, pos0):
    b_idx = pl.program_id(0)
    l_idx = pl.program_id(1)
    rows = bb * tl
    d_model = x_ref.shape[-1]
    rw_width = q_ref.shape[-1]
    dh = rw_width // RET_HEADS
    pw = ue_ref.shape[-1]
    gw = pw // len(POOL_WINDOWS)
    n_chunks = tl // chunk

    @pl.when(l_idx == 0)
    def _():
        st_ref[...] = s0_ref[...]
        ue_ref[:, 0:HIST_ROWS, :] = h0_ref[...]

    @pl.when((l_idx == 0) & (b_idx == 0))
    def _():
        cnt_ref[...] = jnp.zeros_like(cnt_ref)

    cos_b = rb_ref[0, 0:1, :]
    sin_b = rb_ref[0, 1:2, :]
    cosf = cos_b * rc_ref[...] - sin_b * rs_ref[...]
    sinf = sin_b * rcs_ref[...] + cos_b * rss_ref[...]
    k_scale = dh ** -0.5
    n_blocks = max(1, tl // PREP_ROWS) if bb == 1 else 1
    block = rows // n_blocks
    for blk in range(n_blocks):
        rs = slice(blk * block, (blk + 1) * block)
        xb = x_ref[0, rs, :] if bb == 1 else x_ref[...].reshape(rows, d_model)
        hb = _rms(xb, g1_ref[...]).astype(BF16)
        proj = jnp.dot(hb, win_ref[...], preferred_element_type=F32)

        def rotate(a):
            if bb == 1:
                return a * cosf[rs] + pltpu.roll(a, dh // 2, 1) * sinf[rs]
            return (a.reshape(bb, tl, dh) * cosf[None]
                    + pltpu.roll(a, dh // 2, 1).reshape(bb, tl, dh) * sinf[None]).reshape(rows, dh)

        for hh in range(RET_HEADS):
            cs = slice(hh * dh, (hh + 1) * dh)
            q_ref[rs, cs] = rotate(proj[:, hh * dh:(hh + 1) * dh]).astype(BF16)
            k_ref[rs, cs] = rotate(proj[:, rw_width + hh * dh:rw_width + (hh + 1) * dh]) * k_scale
        v_ref[rs, :] = proj[:, 2 * rw_width:3 * rw_width].astype(BF16)
        gate_ref[rs, :] = proj[:, 3 * rw_width:4 * rw_width]
        u = proj[:, 4 * rw_width:4 * rw_width + pw]
        if bb == 1:
            ue_ref[0, HIST_ROWS + blk * block:HIST_ROWS + (blk + 1) * block, :] = u
        else:
            ue_ref[:, HIST_ROWS:HIST_ROWS + tl, :] = u.reshape(bb, tl, pw)

    def ret_block(b, c):
        r0 = b * tl + c * chunk
        if not isinstance(r0, int):
            r0 = pl.multiple_of(r0, chunk)
        for hh in range(RET_HEADS):
            cs = slice(hh * dh, (hh + 1) * dh)
            qc = q_ref[pl.ds(r0, chunk), cs]
            kf = k_ref[pl.ds(r0, chunk), cs]
            vc = v_ref[pl.ds(r0, chunk), cs]
            s_old = st_ref[b, hh]
            sc = lax.dot_general(qc, kf.astype(BF16), (((1,), (1,)), ((), ())),
                                 preferred_element_type=F32) * dintra_ref[hh]
            o = (jnp.dot(sc.astype(BF16), vc, preferred_element_type=F32)
                 + dq_ref[hh] * jnp.dot(qc, s_old.astype(BF16), preferred_element_type=F32))
            kd = (kf * dk_ref[hh]).astype(BF16)
            s_new = dc_ref[hh] * s_old + lax.dot_general(
                kd, vc, (((0,), (0,)), ((), ())), preferred_element_type=F32)
            st_ref[b, hh] = s_new
            o_ref[pl.ds(r0, chunk), cs] = o

    if bb * n_chunks <= 4:
        for b in range(bb):
            for c in range(n_chunks):
                ret_block(b, c)
    else:
        def body(i, carry):
            ret_block(i // n_chunks, i % n_chunks)
            return carry
        lax.fori_loop(0, bb * n_chunks, body, 0)

    for hh in range(RET_HEADS):
        cs = slice(hh * dh, (hh + 1) * dh)
        oh = o_ref[:, cs]
        mu = jnp.mean(oh, axis=-1, keepdims=True)
        oc = oh - mu
        var = jnp.mean(oc * oc, axis=-1, keepdims=True)
        y = oc * lax.rsqrt(var + EPS) * gret_ref[:, cs]
        g = gate_ref[:, cs]
        a_ref[:, cs] = (g * _sigmoid(g) * y).astype(BF16)

    pos = pos0 + l_idx * tl + lax.broadcasted_iota(I32, (1, tl, 1), 1)
    for gi, w in enumerate(POOL_WINDOWS):
        cs = slice(gi * gw, (gi + 1) * gw)
        u_g = ue_ref[:, HIST_ROWS:HIST_ROWS + tl, cs]
        acc = u_g
        for j in range(1, w):
            acc = acc + ue_ref[:, HIST_ROWS - j:HIST_ROWS - j + tl, cs]
        inv_cnt = 1.0 / jnp.minimum(pos + 1, w).astype(F32)
        p = (acc * inv_cnt - u_g).reshape(rows, gw)
        z = jnp.dot(p.astype(BF16), wpool_ref[gi], preferred_element_type=F32) * pscale_ref[:, cs]
        a_ref[:, rw_width + gi * gw:rw_width + (gi + 1) * gw] = z.astype(BF16)
    tail = ue_ref[:, tl:tl + HIST_ROWS, :]
    ue_ref[:, 0:HIST_ROWS, :] = tail
    hist_ref[...] = tail

    x1 = x_ref[...].reshape(rows, d_model) + jnp.dot(a_ref[...], wout_ref[...], preferred_element_type=F32)
    x1_ref[...] = x1.reshape(bb, tl, d_model)
    h2 = _rms(x1, g2_ref[...])
    h2_ref[...] = _pack_bf16_pair(h2[:, 0:d_model // 2], h2[:, d_model // 2:]).reshape(bb, tl, d_model // 2)

    h2_hi = h2.astype(BF16)
    h2_lo = (h2 - h2_hi.astype(F32)).astype(BF16)
    nt = (((1,), (1,)), ((), ()))
    two = lax.dot_general(wr_ref[...], h2_hi, nt, preferred_element_type=F32)
    lt = (two[0:ROUTER_ROWS] + two[ROUTER_ROWS:2 * ROUTER_ROWS]
          + lax.dot_general(wr_ref[0:ROUTER_ROWS, :], h2_lo, nt, preferred_element_type=F32))
    neg = jnp.float32(-jnp.inf)
    big = jnp.float32(1e9)
    sub = lax.broadcasted_iota(I32, (EXPERTS_PER_GROUP, rows), 0).astype(F32)
    gl = jnp.where(sub < N_EXPERT_GROUPS, lt[N_EXPERTS:N_EXPERTS + EXPERTS_PER_GROUP], neg)
    gmax = jnp.max(gl, axis=0, keepdims=True)
    gidx = jnp.min(jnp.where(gl == gmax, sub, big), axis=0, keepdims=True)
    p_sel = 1.0 / jnp.sum(jnp.exp(gl - gmax), axis=0, keepdims=True)
    el = lt[0:EXPERTS_PER_GROUP]
    for g in range(1, N_EXPERT_GROUPS):
        el = jnp.where(gidx == g, lt[g * EXPERTS_PER_GROUP:(g + 1) * EXPERTS_PER_GROUP], el)
    m1 = jnp.max(el, axis=0, keepdims=True)
    t1 = jnp.min(jnp.where(el == m1, sub, big), axis=0, keepdims=True)
    el2 = jnp.where(sub == t1, neg, el)
    m2 = jnp.max(el2, axis=0, keepdims=True)
    t2 = jnp.min(jnp.where(el2 == m2, sub, big), axis=0, keepdims=True)
    e2 = jnp.exp(m2 - m1)
    w1 = p_sel / (1.0 + e2)
    w2 = p_sel * e2 / (1.0 + e2)
    i1 = gidx * EXPERTS_PER_GROUP + t1
    i2 = gidx * EXPERTS_PER_GROUP + t2

    eid = lax.broadcasted_iota(I32, (N_EXPERTS, rows), 0).astype(F32)
    hit1 = eid == i1
    hit2 = eid == i2
    onehot = (hit1 | hit2).astype(BF16)
    before = jnp.dot(onehot, tri_ref[...], preferred_element_type=F32) + cnt_ref[...]
    r1 = jnp.sum(jnp.where(hit1, before, 0.0), axis=0, keepdims=True)
    r2 = jnp.sum(jnp.where(hit2, before, 0.0), axis=0, keepdims=True)
    cnt_ref[...] = cnt_ref[...] + jnp.sum(onehot.astype(F32), axis=1, keepdims=True)

    ri = jnp.where(sub == 0, i1, jnp.where(sub == 1, i2, jnp.where(sub == 2, r1, jnp.where(sub == 3, r2, 0.0))))
    ri_ref[...] = ri.astype(I32).reshape(ri_ref.shape)
    rw_ref[...] = jnp.where(sub == 0, w1, jnp.where(sub == 1, w2, 0.0)).reshape(rw_ref.shape)


def _rope_tables(pos0, seq, tl, dh):
    half = dh // 2
    inv = ROPE_BASE ** (-jnp.arange(half, dtype=F32) / half)
    ang_t = jnp.arange(tl, dtype=F32)[:, None] * inv[None, :]
    ang_b = (pos0 + tl * jnp.arange(seq // tl)).astype(F32)[:, None] * inv[None, :]
    dup = lambda a: jnp.concatenate([a, a], axis=-1)
    sgn = lambda a: jnp.concatenate([-a, a], axis=-1)
    base = jnp.stack([dup(jnp.cos(ang_b)), dup(jnp.sin(ang_b))], axis=1)
    base = jnp.pad(base, ((0, 0), (0, 8 - base.shape[1]), (0, 0)))
    cos_t, sin_t = jnp.cos(ang_t), jnp.sin(ang_t)
    return base, dup(cos_t), dup(sin_t), sgn(cos_t), sgn(sin_t)


def _layer_call(x, b0, nb, s0, h0, pos0, consts, after, *, bb, tl, chunk):
    _, seq, d_model = x.shape
    bsz = nb
    blk0 = b0 // bb
    rows = bb * tl
    rw_width = consts["gret"].shape[-1]
    pw = consts["pscale"].shape[-1]
    dh = rw_width // RET_HEADS

    rope = _rope_tables(pos0, seq, tl, dh)

    lg = jnp.log1p(-jnp.exp2(-5.0 - jnp.arange(RET_HEADS, dtype=F32)))
    idx = jnp.arange(chunk, dtype=F32)
    diff = idx[:, None] - idx[None, :]
    d_intra = jnp.where(diff[None] >= 0, jnp.exp(jnp.maximum(diff, 0.0)[None] * lg[:, None, None]), 0.0)
    d_q = jnp.broadcast_to(jnp.exp((idx + 1.0)[None, :] * lg[:, None])[:, :, None], (RET_HEADS, chunk, dh))
    d_k = jnp.broadcast_to(jnp.exp((chunk - 1.0 - idx)[None, :] * lg[:, None])[:, :, None], (RET_HEADS, chunk, dh))
    d_c = jnp.exp(chunk * lg)
    tri = jnp.triu(jnp.ones((rows, rows), BF16), 1)

    const2 = lambda b, l, *_: (0, 0)
    const3 = lambda b, l, *_: (0, 0, 0)
    grid_spec = pltpu.PrefetchScalarGridSpec(
        num_scalar_prefetch=0,
        grid=(bsz // bb, seq // tl),
        in_specs=[
            pl.BlockSpec(memory_space=pltpu.SMEM),
            pl.BlockSpec((bb, tl, d_model), lambda b, l: (blk0 + b, l, 0)),
            pl.BlockSpec((bb, RET_HEADS, dh, dh), lambda b, l: (b, 0, 0, 0)),
            pl.BlockSpec((bb, HIST_ROWS, pw), lambda b, l: (b, 0, 0)),
            pl.BlockSpec((1, 8, dh), lambda b, l: (l, 0, 0)),
            pl.BlockSpec((tl, dh), const2),
            pl.BlockSpec((tl, dh), const2),
            pl.BlockSpec((tl, dh), const2),
            pl.BlockSpec((tl, dh), const2),
            pl.BlockSpec((RET_HEADS, chunk, chunk), const3),
            pl.BlockSpec((RET_HEADS, chunk, dh), const3),
            pl.BlockSpec((RET_HEADS, chunk, dh), const3),
            pl.BlockSpec((1, d_model), const2),
            pl.BlockSpec(consts["w_in"].shape, const2),
            pl.BlockSpec((1, rw_width), const2),
            pl.BlockSpec(consts["w_pool"].shape, const3),
            pl.BlockSpec((1, pw), const2),
            pl.BlockSpec(consts["w_out"].shape, const2),
            pl.BlockSpec((1, d_model), const2),
            pl.BlockSpec((2 * ROUTER_ROWS, d_model), const2),
            pl.BlockSpec((rows, rows), const2),
            pl.BlockSpec(memory_space=pl.ANY),
        ],
        out_specs=[
            pl.BlockSpec((bb, tl, d_model), lambda b, l: (b, l, 0)),
            pl.BlockSpec((bb, tl, d_model // 2), lambda b, l: (b, l, 0)),
            pl.BlockSpec((1, 1, EXPERTS_PER_GROUP, rows), lambda b, l: (b, l, 0, 0)),
            pl.BlockSpec((1, 1, EXPERTS_PER_GROUP, rows), lambda b, l: (b, l, 0, 0)),
            pl.BlockSpec((bb, RET_HEADS, dh, dh), lambda b, l: (b, 0, 0, 0)),
            pl.BlockSpec((bb, HIST_ROWS, pw), lambda b, l: (b, 0, 0)),
            pl.BlockSpec((N_EXPERTS, rows), const2),
        ],
        scratch_shapes=[
            pltpu.VMEM((bb, HIST_ROWS + tl, pw), F32),
            pltpu.VMEM((rows, rw_width), BF16),
            pltpu.VMEM((rows, rw_width), F32),
            pltpu.VMEM((rows, rw_width), BF16),
            pltpu.VMEM((rows, rw_width), F32),
            pltpu.VMEM((rows, rw_width), F32),
            pltpu.VMEM((rows, d_model), BF16),
        ],
    )
    out_shape = [
        jax.ShapeDtypeStruct((bsz, seq, d_model), F32),
        jax.ShapeDtypeStruct((bsz, seq, d_model // 2), U32),
        jax.ShapeDtypeStruct((bsz // bb, seq // tl, EXPERTS_PER_GROUP, rows), I32),
        jax.ShapeDtypeStruct((bsz // bb, seq // tl, EXPERTS_PER_GROUP, rows), F32),
        jax.ShapeDtypeStruct((bsz, RET_HEADS, dh, dh), F32),
        jax.ShapeDtypeStruct((bsz, HIST_ROWS, pw), F32),
        jax.ShapeDtypeStruct((N_EXPERTS, rows), F32),
    ]
    kern = functools.partial(_layer_kernel, bb=bb, tl=tl, chunk=chunk, pos0=pos0)
    operands = (d_c, x, s0, h0, *rope, d_intra, d_q, d_k,
                consts["g1"], consts["w_in"], consts["gret"], consts["w_pool"], consts["pscale"],
                consts["w_out"], consts["g2"], consts["wr"], tri, after)
    n_tok = bsz * seq
    mm_flops_per_token = 2 * (d_model * consts["w_in"].shape[1] + d_model * d_model + 3 * d_model * ROUTER_ROWS
                              + pw * pw // 4
                              + rw_width * (2 * chunk + 2 * dh) + N_EXPERTS * rows)
    cost = pl.CostEstimate(
        flops=n_tok * mm_flops_per_token, transcendentals=n_tok * (rw_width + 2 * N_EXPERT_GROUPS),
        bytes_accessed=_nbytes(*operands) - _nbytes(x, after) + n_tok * d_model * 4 + _nbytes(*out_shape))
    return pl.pallas_call(
        kern, grid_spec=grid_spec, out_shape=out_shape, name=f"layer_pos{pos0}_b{b0}", cost_estimate=cost,
        compiler_params=pltpu.CompilerParams(
            dimension_semantics=("arbitrary", "arbitrary"), vmem_limit_bytes=VMEM_LIMIT),
    )(*operands)


def _sc_partition(n_units):
    info = plsc.get_sparse_core_info()
    nc, nw = info.num_cores, info.num_cores * info.num_subcores
    upw = -(-n_units // nw)
    upw += upw % 2
    return nc, nw, upw


def _units_by_worker(idx, n_units, upw, nw):
    idx = jnp.pad(idx.reshape(n_units, SC_UNIT), ((0, nw * upw - n_units), (0, 0)))
    return idx.reshape(upw, nw, SC_UNIT).transpose(1, 0, 2)


def _sc_dispatch(srcs, idx0, idx1, n_out_rows, after=None):
    assert 1 <= len(srcs) <= 2
    d = srcs[0].shape[1]
    dtype = srcs[0].dtype
    assert all(src.shape[0] % SC_UNIT == 0 for src in srcs)
    units_a = srcs[0].shape[0] // SC_UNIT
    n_units = sum(src.shape[0] for src in srcs) // SC_UNIT
    nc, nw, upw = _sc_partition(n_units)
    idx0 = _units_by_worker(idx0, n_units, upw, nw)
    idx1 = _units_by_worker(idx1, n_units, upw, nw)
    mesh = plsc.VectorSubcoreMesh(core_axis_name="c", subcore_axis_name="s")
    dma = pltpu.SemaphoreType.DMA
    extra = [] if after is None else [after]

    moved = n_units * SC_UNIT * d * jnp.dtype(dtype).itemsize
    @functools.partial(
        pl.kernel, mesh=mesh,
        cost_estimate=pl.CostEstimate(flops=0, transcendentals=0, bytes_accessed=3 * moved + _nbytes(idx0, idx1)),
        out_type=jax.ShapeDtypeStruct((n_out_rows, d), dtype),
        scratch_types=[
            pltpu.VMEM((upw, SC_UNIT), I32),
            pltpu.VMEM((upw, SC_UNIT), I32),
            pltpu.VMEM((SC_UNIT, d), dtype),
            pltpu.VMEM((SC_UNIT, d), dtype),
            dma, dma, dma, dma, dma, dma,
        ],
    )
    def k(*refs):
        src_hbm = refs[:len(srcs)]
        i0_hbm, i1_hbm, out_hbm, i0_v, i1_v, rows0, rows1, l0, l1, p0, p1, q0, q1 = refs[len(srcs) + len(extra):]
        wid = lax.axis_index("s") * nc + lax.axis_index("c")
        pltpu.sync_copy(i0_hbm.at[wid], i0_v)
        pltpu.sync_copy(i1_hbm.at[wid], i1_v)
        rows, lsem, psem, qsem = (rows0, rows1), (l0, l1), (p0, p1), (q0, q1)

        def live(j):
            return j * nw + wid < n_units

        def load(j, b, op):
            unit = j * nw + wid

            @pl.when(live(j) & (unit < units_a))
            def _():
                op(pltpu.make_async_copy(
                    src_hbm[0].at[pl.ds(pl.multiple_of(unit * SC_UNIT, 8), SC_UNIT)], rows[b], lsem[b]))

            if len(srcs) == 2:
                @pl.when(live(j) & (unit >= units_a))
                def _():
                    op(pltpu.make_async_copy(
                        src_hbm[1].at[pl.ds(pl.multiple_of((unit - units_a) * SC_UNIT, 8), SC_UNIT)],
                        rows[b], lsem[b]))

        def scatter(j, b, op):
            @pl.when(live(j))
            def _():
                op(pltpu.make_async_copy(rows[b], out_hbm.at[i0_v.at[j]], psem[b]))
                op(pltpu.make_async_copy(rows[b], out_hbm.at[i1_v.at[j]], qsem[b]))

        start = lambda c: c.start()
        wait = lambda c: c.wait()
        load(0, 0, start)

        @pl.loop(0, upw, step=2)
        def _(j):
            @pl.when(j > 0)
            def _():
                scatter(j - 1, 1, wait)
            load(j + 1, 1, start)
            load(j, 0, wait)
            scatter(j, 0, start)
            scatter(j, 0, wait)

            @pl.when(j + 2 < upw)
            def _():
                load(j + 2, 0, start)
            load(j + 1, 1, wait)
            scatter(j + 1, 1, start)

        scatter(upw - 1, 1, wait)

    return k(*srcs, *extra, idx0, idx1), idx1


def _sc_gather(table, idx):
    n = idx.shape[0]
    d = table.shape[1]
    assert n % SC_UNIT == 0
    n_units = n // SC_UNIT
    nc, nw, upw = _sc_partition(n_units)
    idx = _units_by_worker(idx, n_units, upw, nw)
    mesh = plsc.VectorSubcoreMesh(core_axis_name="c", subcore_axis_name="s")
    dma = pltpu.SemaphoreType.DMA

    @functools.partial(
        pl.kernel, mesh=mesh,
        cost_estimate=pl.CostEstimate(flops=0, transcendentals=0,
                                      bytes_accessed=2 * n * d * table.dtype.itemsize + _nbytes(idx)),
        out_type=jax.ShapeDtypeStruct((n, d), table.dtype),
        scratch_types=[
            pltpu.VMEM((upw, SC_UNIT), I32),
            pltpu.VMEM((SC_UNIT, d), table.dtype),
            pltpu.VMEM((SC_UNIT, d), table.dtype),
            dma, dma, dma, dma,
        ],
    )
    def k(t_hbm, i_hbm, out_hbm, i_v, rows0, rows1, g0, g1, w0, w1):
        wid = lax.axis_index("s") * nc + lax.axis_index("c")
        pltpu.sync_copy(i_hbm.at[wid], i_v)
        rows, gsem, wsem = (rows0, rows1), (g0, g1), (w0, w1)

        def live(j):
            return j * nw + wid < n_units

        def gather(j, b, op):
            @pl.when(live(j))
            def _():
                op(pltpu.make_async_copy(t_hbm.at[i_v.at[j]], rows[b], gsem[b]))

        def write(j, b, op):
            @pl.when(live(j))
            def _():
                op(pltpu.make_async_copy(
                    rows[b], out_hbm.at[pl.ds(pl.multiple_of((j * nw + wid) * SC_UNIT, 8), SC_UNIT)], wsem[b]))

        start = lambda c: c.start()
        wait = lambda c: c.wait()
        gather(0, 0, start)

        @pl.loop(0, upw, step=2)
        def _(j):
            @pl.when(j > 0)
            def _():
                write(j - 1, 1, wait)
            gather(j + 1, 1, start)
            gather(j, 0, wait)
            write(j, 0, start)
            write(j, 0, wait)

            @pl.when(j + 2 < upw)
            def _():
                gather(j + 2, 0, start)
            gather(j + 1, 1, wait)
            write(j + 1, 1, start)

        write(upw - 1, 1, wait)

    return k(table, idx)


def _moe_kernel(start0_ref, count0_ref, gtot0_ref, start1_ref, count1_ref, gtot1_ref,
                xs0_hbm, xs1_hbm, wg_ref, wu_ref, wd_ref, ys0_hbm, ys1_hbm,
                wgu_s, wd_s, xbuf0, ybuf0, xbuf1, ybuf1, sem_in0, sem_out0, sem_in1, sem_out1):
    e = pl.program_id(0)
    last = pl.num_programs(0) - 1
    hidden = wd_s.shape[0]
    half = xbuf0.shape[-1]
    segments = (
        (xs0_hbm, ys0_hbm, xbuf0, ybuf0, sem_in0, sem_out0, start0_ref[e], count0_ref[e], gtot0_ref[0]),
        (xs1_hbm, ys1_hbm, xbuf1, ybuf1, sem_in1, sem_out1, start1_ref[e], count1_ref[e], gtot1_ref[0]),
    )

    def rows_of(g):
        return pl.ds(pl.multiple_of(g * MOE_TILE, MOE_TILE), MOE_TILE)

    def pipeline(xs_hbm, ys_hbm, xbuf, ybuf, sem_in, sem_out):
        def copy_in(g):
            slot = g % MOE_BUFFERS
            return pltpu.make_async_copy(xs_hbm.at[rows_of(g)], xbuf.at[slot], sem_in.at[slot])

        def copy_out(g):
            slot = g % MOE_BUFFERS
            return pltpu.make_async_copy(ybuf.at[slot], ys_hbm.at[rows_of(g)], sem_out.at[slot])
        return copy_in, copy_out

    @pl.when(e == 0)
    def _():
        for xs_hbm, ys_hbm, xbuf, ybuf, sem_in, sem_out, _, _, g_total in segments:
            copy_in, _ = pipeline(xs_hbm, ys_hbm, xbuf, ybuf, sem_in, sem_out)
            for g in range(MOE_LOOKAHEAD):
                @pl.when(g < g_total)
                def _():
                    copy_in(g).start()

    @pl.when(segments[0][7] + segments[1][7] > 0)
    def _():
        wgu_s[:, 0:hidden] = wg_ref[0].astype(BF16)
        wgu_s[:, hidden:2 * hidden] = wu_ref[0].astype(BF16)
        wd_s[...] = wd_ref[0].astype(BF16)

    def expert_rows(xbuf, ybuf, slot, valid):
        row = lax.broadcasted_iota(I32, (MOE_TILE, half), 0)
        x_lo, x_hi = _unpack_bf16_pair(jnp.where(row < valid, xbuf[slot], jnp.uint32(0)))
        ab = (jnp.dot(x_lo.astype(BF16), wgu_s[0:half, :], preferred_element_type=F32)
              + jnp.dot(x_hi.astype(BF16), wgu_s[half:2 * half, :], preferred_element_type=F32))
        a = ab[:, 0:hidden]
        he = a * _sigmoid(a) * ab[:, hidden:2 * hidden]
        y = jnp.dot(he.astype(BF16), wd_s[...], preferred_element_type=F32)
        ybuf[slot] = _pack_bf16_pair(y[:, 0:half], y[:, half:2 * half])

    for xs_hbm, ys_hbm, xbuf, ybuf, sem_in, sem_out, start, count, g_total in segments:
        copy_in, copy_out = pipeline(xs_hbm, ys_hbm, xbuf, ybuf, sem_in, sem_out)
        g_first = start // MOE_TILE
        n_tiles = (count + MOE_TILE - 1) // MOE_TILE

        def tiles(t, width, copy_in=copy_in, copy_out=copy_out, xbuf=xbuf, ybuf=ybuf,
                  g_first=g_first, count=count, g_total=g_total):
            gs = [g_first + t + i for i in range(width)]
            for g in gs:
                @pl.when(g + MOE_LOOKAHEAD < g_total)
                def _():
                    copy_in(g + MOE_LOOKAHEAD).start()
            for g in gs:
                copy_in(g).wait()

                @pl.when(g >= MOE_BUFFERS)
                def _():
                    copy_out(g - MOE_BUFFERS).wait()
            for i, g in enumerate(gs):
                expert_rows(xbuf, ybuf, g % MOE_BUFFERS, count - (t + i) * MOE_TILE)
            for g in gs:
                copy_out(g).start()

        def pair(p, carry, tiles=tiles):
            tiles(MOE_UNROLL * p, MOE_UNROLL)
            return carry

        lax.fori_loop(0, n_tiles // MOE_UNROLL, pair, 0)

        def single(r, carry, tiles=tiles, n_tiles=n_tiles):
            tiles(n_tiles // MOE_UNROLL * MOE_UNROLL + r, 1)
            return carry

        lax.fori_loop(0, n_tiles % MOE_UNROLL, single, 0)

        @pl.when(e == last)
        def _(copy_out=copy_out, g_total=g_total):
            for j in range(1, MOE_BUFFERS + 1):
                @pl.when(g_total >= j)
                def _():
                    copy_out(g_total - j).wait()


def _moe_call(groups, w_g, w_u, w_d):
    (xs0, starts0, cnt0), (xs1, starts1, cnt1) = groups
    half = xs0.shape[1]
    n_experts, d_model, hidden = w_g.shape

    def tiles_total(starts, cnt):
        return ((starts[-1:] + cnt[-1:] + MOE_TILE - 1) // MOE_TILE).astype(I32)

    wspec = lambda shape: pl.BlockSpec(shape, lambda e, *_: (e, 0, 0))
    tile_bufs = [pltpu.VMEM((MOE_BUFFERS, MOE_TILE, half), U32)] * 4
    grid_spec = pltpu.PrefetchScalarGridSpec(
        num_scalar_prefetch=6,
        grid=(n_experts,),
        in_specs=[
            pl.BlockSpec(memory_space=pl.ANY),
            pl.BlockSpec(memory_space=pl.ANY),
            wspec((1, d_model, hidden)),
            wspec((1, d_model, hidden)),
            wspec((1, hidden, d_model)),
        ],
        out_specs=[pl.BlockSpec(memory_space=pl.ANY), pl.BlockSpec(memory_space=pl.ANY)],
        scratch_shapes=[
            pltpu.VMEM((d_model, 2 * hidden), BF16),
            pltpu.VMEM((hidden, d_model), BF16),
            *tile_bufs,
            *[pltpu.SemaphoreType.DMA((MOE_BUFFERS,))] * 4,
        ],
    )
    n_rows = xs0.shape[0] + xs1.shape[0]
    cost = pl.CostEstimate(flops=n_rows * 6 * d_model * hidden, transcendentals=n_rows * hidden,
                           bytes_accessed=2 * _nbytes(xs0, xs1) + _nbytes(w_g, w_u, w_d))
    return pl.pallas_call(
        _moe_kernel, grid_spec=grid_spec, cost_estimate=cost,
        out_shape=[jax.ShapeDtypeStruct(xs0.shape, U32), jax.ShapeDtypeStruct(xs1.shape, U32)], name="moe_experts",
        compiler_params=pltpu.CompilerParams(
            dimension_semantics=("arbitrary",), vmem_limit_bytes=VMEM_LIMIT),
    )(starts0, cnt0, tiles_total(starts0, cnt0), starts1, cnt1, tiles_total(starts1, cnt1),
      xs0, xs1, w_g, w_u, w_d)


def _combine_kernel(x1_ref, y0_ref, y1_ref, rw_ref, gf_ref, *rest):
    out_ref = rest[-1]
    tr = x1_ref.shape[0]
    w_rows = jnp.concatenate([rw_ref[0], jnp.zeros((LANES - rw_ref.shape[1], tr), F32)], axis=0)
    w_cols = w_rows.T
    w0, w1 = w_cols[:, 0:1], w_cols[:, 1:2]
    a_lo, a_hi = _unpack_bf16_pair(y0_ref[0])
    b_lo, b_hi = _unpack_bf16_pair(y1_ref[0])
    moe = jnp.concatenate([w0 * a_lo + w1 * b_lo, w0 * a_hi + w1 * b_hi], axis=-1)
    out_ref[...] = _rms(x1_ref[...] + moe, gf_ref[...])


def _combine_call(x1, rw, row0, n, yg, gf, out_rows, out_row0, prev_out=None):
    t, d_model = x1.shape
    tr = rw.shape[-1]
    half = yg.shape[-1]
    assert t % tr == 0 and row0 % tr == 0 and n % tr == 0 and rw.shape == (t // tr, EXPERTS_PER_GROUP, tr)
    assert yg.shape == (2, n, half) and out_row0 % tr == 0
    off = row0 // tr
    ooff = out_row0 // tr
    in_specs = [
        pl.BlockSpec((tr, d_model), lambda i: (off + i, 0)),
        pl.BlockSpec((1, tr, half), lambda i: (0, i, 0)),
        pl.BlockSpec((1, tr, half), lambda i: (1, i, 0)),
        pl.BlockSpec((1, EXPERTS_PER_GROUP, tr), lambda i: (off + i, 0, 0)),
        pl.BlockSpec((1, d_model), lambda i: (0, 0)),
    ]
    args = [x1, yg, yg, rw, gf]
    aliases = {}
    if prev_out is not None:
        in_specs.append(pl.BlockSpec(memory_space=pl.ANY))
        args.append(prev_out)
        aliases = {len(args) - 1: 0}
    return pl.pallas_call(
        _combine_kernel,
        grid=(n // tr,),
        in_specs=in_specs,
        out_specs=pl.BlockSpec((tr, d_model), lambda i: (ooff + i, 0)),
        out_shape=jax.ShapeDtypeStruct((out_rows, d_model), F32), name=f"combine_row{out_row0}_of{out_rows}",
        cost_estimate=pl.CostEstimate(flops=8 * n * d_model, transcendentals=n,
                                      bytes_accessed=2 * n * d_model * 4 + _nbytes(yg) + n * 4 * EXPERTS_PER_GROUP),
        input_output_aliases=aliases,
        compiler_params=pltpu.CompilerParams(
            dimension_semantics=("arbitrary",), vmem_limit_bytes=VMEM_LIMIT),
    )(*args)


def _route(streams, after=None):
    tokens = [h2.shape[0] for h2, _, _ in streams]
    counts = [cnt[:, 0].astype(I32) for _, _, cnt in streams]
    total = sum(counts)
    padded = ((total + MOE_TILE - 1) // MOE_TILE) * MOE_TILE
    starts = (jnp.cumsum(padded) - padded).astype(I32)
    experts = jnp.arange(N_EXPERTS, dtype=I32)[None, :, None]
    pos, base = [], starts
    for (_, ri, _), t, cnt in zip(streams, tokens, counts):
        ri = jnp.moveaxis(ri, 2, 0).reshape(ri.shape[2], t)
        first_row = jnp.sum(jnp.where(ri[0:2, None, :] == experts, base[None, :, None], 0), axis=1)
        pos.append(ri[2:4] + first_row)
        base = base + cnt
    pos = jnp.concatenate(pos, axis=1)
    n_rows = ((2 * sum(tokens) + N_EXPERTS * (MOE_TILE - 1)) // MOE_TILE) * MOE_TILE
    xs_sorted, ready = _sc_dispatch([h2 for h2, _, _ in streams], pos[0], pos[1], n_rows, after)
    return (xs_sorted, starts, total), pos, ready


def _gather_tokens(ys_sorted, pos, t0, n):
    return _sc_gather(ys_sorted, pos[:, t0:t0 + n].reshape(2 * n)).reshape(2, n, ys_sorted.shape[-1])


def _one_layer(xp, xs, s_ret, c_pool, norm1_g, w_in, ret_norm_g, w_pool, pool_scale, w_out, norm2_g,
               w_rg, w_re, w_g, w_u, w_d, final_g, past_len):
    bp, seq, d_model = xp.shape
    bs, dseq, _ = xs.shape
    rw_width = ret_norm_g.shape[-1]
    pw = pool_scale.shape[-1]
    dh = rw_width // RET_HEADS
    half = d_model // 2

    w_r = jnp.concatenate(
        [w_re.T, w_rg.T, jnp.zeros((ROUTER_ROWS - N_EXPERTS - N_EXPERT_GROUPS, d_model), F32)], axis=0)
    wr_hi = w_r.astype(BF16)
    wr = jnp.concatenate([wr_hi, (w_r - wr_hi.astype(F32)).astype(BF16)], axis=0)
    consts = dict(
        g1=norm1_g.reshape(1, d_model), w_in=w_in.astype(BF16), gret=ret_norm_g.reshape(1, rw_width),
        w_pool=w_pool.astype(BF16), pscale=pool_scale.reshape(1, pw), w_out=w_out.astype(BF16),
        g2=norm2_g.reshape(1, d_model), wr=wr)

    gf = final_g.reshape(1, d_model)

    ts = bs * dseq
    b_lead = bp - 1
    t_lead, t_rest = b_lead * seq, (bp - b_lead) * seq
    zeros = lambda nb: (jnp.zeros((nb, RET_HEADS, dh, dh), F32), jnp.zeros((nb, HIST_ROWS, pw), F32))
    h0s = jnp.pad(c_pool, ((0, 0), (HIST_ROWS - POOL_HIST, 0), (0, 0)))
    prompt_tile = dict(bb=1, tl=PROMPT_TILE, chunk=256)

    def stream(layer_out, t):
        x1, h2, ri, rw, st, hist, cnt = layer_out
        return dict(x1=x1.reshape(t, d_model), route=(h2.reshape(t, half), ri, cnt),
                    rw=rw.reshape(-1, EXPERTS_PER_GROUP, rw.shape[-1]), st=st, hist=hist)

    pa = stream(_layer_call(xp, 0, b_lead, *zeros(b_lead), 0, consts, gf, **prompt_tile), t_lead)
    group0, pos0, ready0 = _route([pa["route"]])
    pb = stream(_layer_call(xp, b_lead, bp - b_lead, *zeros(bp - b_lead), 0, consts, ready0, **prompt_tile),
                t_rest)
    sm = stream(_layer_call(xs, 0, bs, s_ret, h0s, past_len, consts, pb["route"][2],
                            bb=bs, tl=dseq, chunk=min(64, dseq)), ts)
    group1, pos1, _ = _route([pb["route"], sm["route"]], after=group0[0])
    ys0, ys1 = _moe_call([group0, group1], w_g, w_u, w_d)

    tp = bp * seq
    yp = _combine_call(pb["x1"], pb["rw"], 0, t_rest, _gather_tokens(ys1, pos1, 0, t_rest), gf, tp, t_lead)
    ysm = _combine_call(sm["x1"], sm["rw"], 0, ts, _gather_tokens(ys1, pos1, t_rest, ts), gf, ts, 0)
    row0 = 0
    for nb in COMBINE_CHUNKS:
        n = min(nb * seq, t_lead - row0)
        if n > 0:
            yp = _combine_call(pa["x1"], pa["rw"], row0, n, _gather_tokens(ys0, pos0, row0, n), gf, tp, row0,
                               prev_out=yp)
            row0 += n
    assert row0 == t_lead
    st_p = jnp.concatenate([pa["st"], pb["st"]], axis=0)
    hist_p = jnp.concatenate([pa["hist"], pb["hist"]], axis=0)
    return (yp.reshape(bp, seq, d_model), ysm.reshape(bs, dseq, d_model),
            st_p, hist_p[:, HIST_ROWS - POOL_HIST:], sm["st"], sm["hist"][:, HIST_ROWS - POOL_HIST:])


def kernel(x_prompt, x_sample, state_ret, cache_pool, norm1_g, w_in, ret_norm_g, w_pool, pool_scale, w_out,
           norm2_g, w_router_group, w_router_expert, w_exp_gate, w_exp_up, w_exp_down, final_norm_g):
    depth = w_in.shape[0]
    assert depth == 1, "the final RMSNorm is fused into the layer's combine kernel"
    assert x_prompt.shape[0] >= 2 and x_prompt.shape[1] % PROMPT_TILE == 0
    yp, ys, s_p, h_p, s_s, h_s = _one_layer(
        x_prompt, x_sample, state_ret[0], cache_pool[0], norm1_g[0], w_in[0], ret_norm_g[0], w_pool[0],
        pool_scale[0], w_out[0], norm2_g[0], w_router_group[0], w_router_expert[0],
        w_exp_gate[0], w_exp_up[0], w_exp_down[0], final_norm_g, PAST_LEN)
    return (yp, ys, s_p[None], h_p[None], s_s[None], h_s[None])
```

```python
import functools

import jax
import jax.numpy as jnp
from jax import lax
from jax.experimental import pallas as pl
from jax.experimental.pallas import tpu as pltpu
from jax.experimental.pallas import tpu_sc as plsc

F32 = jnp.float32
BF16 = jnp.bfloat16
I32 = jnp.int32
U32 = jnp.uint32

EPS = 1e-6
ROPE_BASE = 10000.0
RET_HEADS = 4
POOL_WINDOWS = (2, 4, 8, 16)
POOL_HIST = max(POOL_WINDOWS) - 1
N_EXPERT_GROUPS = 4
EXPERTS_PER_GROUP = 8
N_EXPERTS = N_EXPERT_GROUPS * EXPERTS_PER_GROUP
ROUTER_ROWS = 48
PAST_LEN = 1024

LANES = 128
HIST_ROWS = 16
MOE_TILE = 256
MOE_BUFFERS = 6
MOE_UNROLL = 2
MOE_LOOKAHEAD = MOE_BUFFERS - MOE_UNROLL
PREP_ROWS = 512
PROMPT_TILE = 1024
COMBINE_CHUNKS = (1, 2)
SC_UNIT = 32
VMEM_LIMIT = 56 * 1024 * 1024


def _nbytes(*arrays):
    return sum(a.size * a.dtype.itemsize for a in arrays)


def _rms(x, g):
    return x * lax.rsqrt(jnp.mean(x * x, axis=-1, keepdims=True) + EPS) * g


def _sigmoid(x):
    return 1.0 / (1.0 + jnp.exp(-x))


def _pack_bf16_pair(lo, hi):
    lo_b = lax.bitcast_convert_type(lo.astype(BF16).astype(F32), U32)
    hi_b = lax.bitcast_convert_type(hi.astype(BF16).astype(F32), U32)
    return hi_b | (lo_b >> 16)


def _unpack_bf16_pair(p):
    lo = lax.bitcast_convert_type(p << 16, F32)
    hi = lax.bitcast_convert_type(p & jnp.uint32(0xFFFF0000), F32)
    return lo, hi


def _layer_kernel(dc_ref, x_ref, s0_ref, h0_ref, rb_ref, rc_ref, rs_ref, rcs_ref, rss_ref,
                  dintra_ref, dq_ref, dk_ref,
                  g1_ref, win_ref, gret_ref, wpool_ref, pscale_ref, wout_ref, g2_ref,
                  wr_ref, tri_ref, after_ref,
                  x1_ref, h2_ref, ri_ref, rw_ref, st_ref, hist_ref, cnt_ref,
                  ue_ref, q_ref, k_ref, v_ref, gate_ref, a_ref,
                  *, bb, tl, chunk, pos0):
    b_idx = pl.program_id(0)
    l_idx = pl.program_id(1)
    rows = bb * tl
    d_model = x_ref.shape[-1]
    rw_width = q_ref.shape[-1]
    dh = rw_width // RET_HEADS
    pw = ue_ref.shape[-1]
    gw = pw // len(POOL_WINDOWS)
    n_chunks = tl // chunk

    @pl.when(l_idx == 0)
    def _():
        st_ref[...] = s0_ref[...]
        ue_ref[:, 0:HIST_ROWS, :] = h0_ref[...]

    @pl.when((l_idx == 0) & (b_idx == 0))
    def _():
        cnt_ref[...] = jnp.zeros_like(cnt_ref)

    cos_b = rb_ref[0, 0:1, :]
    sin_b = rb_ref[0, 1:2, :]
    cosf = cos_b * rc_ref[...] - sin_b * rs_ref[...]
    sinf = sin_b * rcs_ref[...] + cos_b * rss_ref[...]
    k_scale = dh ** -0.5
    n_blocks = max(1, tl // PREP_ROWS) if bb == 1 else 1
    block = rows // n_blocks
    for blk in range(n_blocks):
        rs = slice(blk * block, (blk + 1) * block)
        xb = x_ref[0, rs, :] if bb == 1 else x_ref[...].reshape(rows, d_model)
        hb = _rms(xb, g1_ref[...]).astype(BF16)
        proj = jnp.dot(hb, win_ref[...], preferred_element_type=F32)

        def rotate(a):
            if bb == 1:
                return a * cosf[rs] + pltpu.roll(a, dh // 2, 1) * sinf[rs]
            return (a.reshape(bb, tl, dh) * cosf[None]
                    + pltpu.roll(a, dh // 2, 1).reshape(bb, tl, dh) * sinf[None]).reshape(rows, dh)

        for hh in range(RET_HEADS):
            cs = slice(hh * dh, (hh + 1) * dh)
            q_ref[rs, cs] = rotate(proj[:, hh * dh:(hh + 1) * dh]).astype(BF16)
            k_ref[rs, cs] = rotate(proj[:, rw_width + hh * dh:rw_width + (hh + 1) * dh]) * k_scale
        v_ref[rs, :] = proj[:, 2 * rw_width:3 * rw_width].astype(BF16)
        gate_ref[rs, :] = proj[:, 3 * rw_width:4 * rw_width]
        u = proj[:, 4 * rw_width:4 * rw_width + pw]
        if bb == 1:
            ue_ref[0, HIST_ROWS + blk * block:HIST_ROWS + (blk + 1) * block, :] = u
        else:
            ue_ref[:, HIST_ROWS:HIST_ROWS + tl, :] = u.reshape(bb, tl, pw)

    def ret_block(b, c):
        r0 = b * tl + c * chunk
        if not isinstance(r0, int):
            r0 = pl.multiple_of(r0, chunk)
        rr = pl.ds(r0, chunk)
        for hh in range(RET_HEADS):
            cs = slice(hh * dh, (hh + 1) * dh)
            qc = q_ref[rr, cs]
            kf = k_ref[rr, cs]
            vc = v_ref[rr, cs]
            s_old = st_ref[b, hh]
            sc = lax.dot_general(qc, kf.astype(BF16), (((1,), (1,)), ((), ())),
                                 preferred_element_type=F32) * dintra_ref[hh]
            o = (jnp.dot(sc.astype(BF16), vc, preferred_element_type=F32)
                 + dq_ref[hh] * jnp.dot(qc, s_old.astype(BF16), preferred_element_type=F32))
            kd = (kf * dk_ref[hh]).astype(BF16)
            s_new = dc_ref[hh] * s_old + lax.dot_general(
                kd, vc, (((0,), (0,)), ((), ())), preferred_element_type=F32)
            st_ref[b, hh] = s_new
            oc = o - jnp.mean(o, axis=-1, keepdims=True)
            var = jnp.mean(oc * oc, axis=-1, keepdims=True)
            y = oc * lax.rsqrt(var + EPS) * gret_ref[:, cs]
            g = gate_ref[rr, cs]
            a_ref[rr, cs] = (g * _sigmoid(g) * y).astype(BF16)

    if bb * n_chunks <= 4:
        for b in range(bb):
            for c in range(n_chunks):
                ret_block(b, c)
    else:
        def body(i, carry):
            ret_block(i // n_chunks, i % n_chunks)
            return carry
        lax.fori_loop(0, bb * n_chunks, body, 0)

    nt = (((1,), (1,)), ((), ()))
    neg = jnp.float32(-jnp.inf)
    big = jnp.float32(1e9)
    sub = lax.broadcasted_iota(I32, (EXPERTS_PER_GROUP, block), 0).astype(F32)
    eid = lax.broadcasted_iota(I32, (N_EXPERTS, block), 0).astype(F32)
    for blk in range(n_blocks):
        lo = blk * block
        rs = slice(lo, lo + block)

        if bb == 1:
            pos = pos0 + l_idx * tl + lo + lax.broadcasted_iota(I32, (block, 1), 0)
            window = lambda off, cs: ue_ref[0, HIST_ROWS + lo - off:HIST_ROWS + lo - off + block, cs]
        else:
            pos = pos0 + l_idx * tl + lax.broadcasted_iota(I32, (1, tl, 1), 1)
            window = lambda off, cs: ue_ref[:, HIST_ROWS - off:HIST_ROWS - off + tl, cs]
        for gi, w in enumerate(POOL_WINDOWS):
            cs = slice(gi * gw, (gi + 1) * gw)
            u_g = window(0, cs)
            acc = u_g
            for j in range(1, w):
                acc = acc + window(j, cs)
            inv_cnt = 1.0 / jnp.minimum(pos + 1, w).astype(F32)
            p = (acc * inv_cnt - u_g).reshape(block, gw)
            z = jnp.dot(p.astype(BF16), wpool_ref[gi], preferred_element_type=F32) * pscale_ref[:, cs]
            a_ref[rs, rw_width + gi * gw:rw_width + (gi + 1) * gw] = z.astype(BF16)

        xb = x_ref[0, rs, :] if bb == 1 else x_ref[...].reshape(rows, d_model)
        x1 = xb + jnp.dot(a_ref[rs, :], wout_ref[...], preferred_element_type=F32)
        h2 = _rms(x1, g2_ref[...])
        h2_packed = _pack_bf16_pair(h2[:, 0:d_model // 2], h2[:, d_model // 2:])
        if bb == 1:
            x1_ref[0, rs, :] = x1
            h2_ref[0, rs, :] = h2_packed
        else:
            x1_ref[...] = x1.reshape(bb, tl, d_model)
            h2_ref[...] = h2_packed.reshape(bb, tl, d_model // 2)

        h2_hi = h2.astype(BF16)
        h2_lo = (h2 - h2_hi.astype(F32)).astype(BF16)
        two = lax.dot_general(wr_ref[...], h2_hi, nt, preferred_element_type=F32)
        lt = (two[0:ROUTER_ROWS] + two[ROUTER_ROWS:2 * ROUTER_ROWS]
              + lax.dot_general(wr_ref[0:ROUTER_ROWS, :], h2_lo, nt, preferred_element_type=F32))
        gl = jnp.where(sub < N_EXPERT_GROUPS, lt[N_EXPERTS:N_EXPERTS + EXPERTS_PER_GROUP], neg)
        gmax = jnp.max(gl, axis=0, keepdims=True)
        gidx = jnp.min(jnp.where(gl == gmax, sub, big), axis=0, keepdims=True)
        p_sel = 1.0 / jnp.sum(jnp.exp(gl - gmax), axis=0, keepdims=True)
        el = lt[0:EXPERTS_PER_GROUP]
        for g in range(1, N_EXPERT_GROUPS):
            el = jnp.where(gidx == g, lt[g * EXPERTS_PER_GROUP:(g + 1) * EXPERTS_PER_GROUP], el)
        m1 = jnp.max(el, axis=0, keepdims=True)
        t1 = jnp.min(jnp.where(el == m1, sub, big), axis=0, keepdims=True)
        el2 = jnp.where(sub == t1, neg, el)
        m2 = jnp.max(el2, axis=0, keepdims=True)
        t2 = jnp.min(jnp.where(el2 == m2, sub, big), axis=0, keepdims=True)
        e2 = jnp.exp(m2 - m1)
        w1 = p_sel / (1.0 + e2)
        w2 = p_sel * e2 / (1.0 + e2)
        i1 = gidx * EXPERTS_PER_GROUP + t1
        i2 = gidx * EXPERTS_PER_GROUP + t2

        hit1 = eid == i1
        hit2 = eid == i2
        onehot = (hit1 | hit2).astype(BF16)
        before = jnp.dot(onehot, tri_ref[...], preferred_element_type=F32) + cnt_ref[...]
        r1 = jnp.sum(jnp.where(hit1, before, 0.0), axis=0, keepdims=True)
        r2 = jnp.sum(jnp.where(hit2, before, 0.0), axis=0, keepdims=True)
        cnt_ref[...] = cnt_ref[...] + jnp.sum(onehot.astype(F32), axis=1, keepdims=True)

        ri = jnp.where(sub == 0, i1, jnp.where(sub == 1, i2, jnp.where(sub == 2, r1, jnp.where(sub == 3, r2, 0.0))))
        ri_ref[0, 0, :, rs] = ri.astype(I32)
        rw_ref[0, 0, :, rs] = jnp.where(sub == 0, w1, jnp.where(sub == 1, w2, 0.0))

    tail = ue_ref[:, tl:tl + HIST_ROWS, :]
    ue_ref[:, 0:HIST_ROWS, :] = tail
    hist_ref[...] = tail


def _rope_tables(pos0, seq, tl, dh):
    half = dh // 2
    inv = ROPE_BASE ** (-jnp.arange(half, dtype=F32) / half)
    ang_t = jnp.arange(tl, dtype=F32)[:, None] * inv[None, :]
    ang_b = (pos0 + tl * jnp.arange(seq // tl)).astype(F32)[:, None] * inv[None, :]
    dup = lambda a: jnp.concatenate([a, a], axis=-1)
    sgn = lambda a: jnp.concatenate([-a, a], axis=-1)
    base = jnp.stack([dup(jnp.cos(ang_b)), dup(jnp.sin(ang_b))], axis=1)
    base = jnp.pad(base, ((0, 0), (0, 8 - base.shape[1]), (0, 0)))
    cos_t, sin_t = jnp.cos(ang_t), jnp.sin(ang_t)
    return base, dup(cos_t), dup(sin_t), sgn(cos_t), sgn(sin_t)


def _layer_call(x, b0, nb, s0, h0, pos0, consts, after, *, bb, tl, chunk):
    _, seq, d_model = x.shape
    bsz = nb
    blk0 = b0 // bb
    rows = bb * tl
    rw_width = consts["gret"].shape[-1]
    pw = consts["pscale"].shape[-1]
    dh = rw_width // RET_HEADS

    rope = _rope_tables(pos0, seq, tl, dh)

    lg = jnp.log1p(-jnp.exp2(-5.0 - jnp.arange(RET_HEADS, dtype=F32)))
    idx = jnp.arange(chunk, dtype=F32)
    diff = idx[:, None] - idx[None, :]
    d_intra = jnp.where(diff[None] >= 0, jnp.exp(jnp.maximum(diff, 0.0)[None] * lg[:, None, None]), 0.0)
    d_q = jnp.broadcast_to(jnp.exp((idx + 1.0)[None, :] * lg[:, None])[:, :, None], (RET_HEADS, chunk, dh))
    d_k = jnp.broadcast_to(jnp.exp((chunk - 1.0 - idx)[None, :] * lg[:, None])[:, :, None], (RET_HEADS, chunk, dh))
    d_c = jnp.exp(chunk * lg)
    block = rows // (max(1, tl // PREP_ROWS) if bb == 1 else 1)
    tri = jnp.triu(jnp.ones((block, block), BF16), 1)

    const2 = lambda b, l, *_: (0, 0)
    const3 = lambda b, l, *_: (0, 0, 0)
    grid_spec = pltpu.PrefetchScalarGridSpec(
        num_scalar_prefetch=0,
        grid=(bsz // bb, seq // tl),
        in_specs=[
            pl.BlockSpec(memory_space=pltpu.SMEM),
            pl.BlockSpec((bb, tl, d_model), lambda b, l: (blk0 + b, l, 0)),
            pl.BlockSpec((bb, RET_HEADS, dh, dh), lambda b, l: (b, 0, 0, 0)),
            pl.BlockSpec((bb, HIST_ROWS, pw), lambda b, l: (b, 0, 0)),
            pl.BlockSpec((1, 8, dh), lambda b, l: (l, 0, 0)),
            pl.BlockSpec((tl, dh), const2),
            pl.BlockSpec((tl, dh), const2),
            pl.BlockSpec((tl, dh), const2),
            pl.BlockSpec((tl, dh), const2),
            pl.BlockSpec((RET_HEADS, chunk, chunk), const3),
            pl.BlockSpec((RET_HEADS, chunk, dh), const3),
            pl.BlockSpec((RET_HEADS, chunk, dh), const3),
            pl.BlockSpec((1, d_model), const2),
            pl.BlockSpec(consts["w_in"].shape, const2),
            pl.BlockSpec((1, rw_width), const2),
            pl.BlockSpec(consts["w_pool"].shape, const3),
            pl.BlockSpec((1, pw), const2),
            pl.BlockSpec(consts["w_out"].shape, const2),
            pl.BlockSpec((1, d_model), const2),
            pl.BlockSpec((2 * ROUTER_ROWS, d_model), const2),
            pl.BlockSpec((block, block), const2),
            pl.BlockSpec(memory_space=pl.ANY),
        ],
        out_specs=[
            pl.BlockSpec((bb, tl, d_model), lambda b, l: (b, l, 0)),
            pl.BlockSpec((bb, tl, d_model // 2), lambda b, l: (b, l, 0)),
            pl.BlockSpec((1, 1, EXPERTS_PER_GROUP, rows), lambda b, l: (b, l, 0, 0)),
            pl.BlockSpec((1, 1, EXPERTS_PER_GROUP, rows), lambda b, l: (b, l, 0, 0)),
            pl.BlockSpec((bb, RET_HEADS, dh, dh), lambda b, l: (b, 0, 0, 0)),
            pl.BlockSpec((bb, HIST_ROWS, pw), lambda b, l: (b, 0, 0)),
            pl.BlockSpec((N_EXPERTS, block), const2),
        ],
        scratch_shapes=[
            pltpu.VMEM((bb, HIST_ROWS + tl, pw), F32),
            pltpu.VMEM((rows, rw_width), BF16),
            pltpu.VMEM((rows, rw_width), F32),
            pltpu.VMEM((rows, rw_width), BF16),
            pltpu.VMEM((rows, rw_width), F32),
            pltpu.VMEM((rows, d_model), BF16),
        ],
    )
    out_shape = [
        jax.ShapeDtypeStruct((bsz, seq, d_model), F32),
        jax.ShapeDtypeStruct((bsz, seq, d_model // 2), U32),
        jax.ShapeDtypeStruct((bsz // bb, seq // tl, EXPERTS_PER_GROUP, rows), I32),
        jax.ShapeDtypeStruct((bsz // bb, seq // tl, EXPERTS_PER_GROUP, rows), F32),
        jax.ShapeDtypeStruct((bsz, RET_HEADS, dh, dh), F32),
        jax.ShapeDtypeStruct((bsz, HIST_ROWS, pw), F32),
        jax.ShapeDtypeStruct((N_EXPERTS, block), F32),
    ]
    kern = functools.partial(_layer_kernel, bb=bb, tl=tl, chunk=chunk, pos0=pos0)
    operands = (d_c, x, s0, h0, *rope, d_intra, d_q, d_k,
                consts["g1"], consts["w_in"], consts["gret"], consts["w_pool"], consts["pscale"],
                consts["w_out"], consts["g2"], consts["wr"], tri, after)
    n_tok = bsz * seq
    mm_flops_per_token = 2 * (d_model * consts["w_in"].shape[1] + d_model * d_model + 3 * d_model * ROUTER_ROWS
                              + pw * pw // 4
                              + rw_width * (2 * chunk + 2 * dh) + N_EXPERTS * block)
    cost = pl.CostEstimate(
        flops=n_tok * mm_flops_per_token, transcendentals=n_tok * (rw_width + 2 * N_EXPERT_GROUPS),
        bytes_accessed=_nbytes(*operands) - _nbytes(x, after) + n_tok * d_model * 4 + _nbytes(*out_shape))
    return pl.pallas_call(
        kern, grid_spec=grid_spec, out_shape=out_shape, name=f"layer_pos{pos0}_b{b0}", cost_estimate=cost,
        compiler_params=pltpu.CompilerParams(
            dimension_semantics=("arbitrary", "arbitrary"), vmem_limit_bytes=VMEM_LIMIT),
    )(*operands)


def _sc_partition(n_units):
    info = plsc.get_sparse_core_info()
    nc, nw = info.num_cores, info.num_cores * info.num_subcores
    upw = -(-n_units // nw)
    upw += upw % 2
    return nc, nw, upw


def _units_by_worker(idx, n_units, upw, nw):
    idx = jnp.pad(idx.reshape(n_units, SC_UNIT), ((0, nw * upw - n_units), (0, 0)))
    return idx.reshape(upw, nw, SC_UNIT).transpose(1, 0, 2)


def _sc_dispatch(srcs, idx0, idx1, n_out_rows, after=None):
    assert 1 <= len(srcs) <= 2
    d = srcs[0].shape[1]
    dtype = srcs[0].dtype
    assert all(src.shape[0] % SC_UNIT == 0 for src in srcs)
    units_a = srcs[0].shape[0] // SC_UNIT
    n_units = sum(src.shape[0] for src in srcs) // SC_UNIT
    nc, nw, upw = _sc_partition(n_units)
    idx0 = _units_by_worker(idx0, n_units, upw, nw)
    idx1 = _units_by_worker(idx1, n_units, upw, nw)
    mesh = plsc.VectorSubcoreMesh(core_axis_name="c", subcore_axis_name="s")
    dma = pltpu.SemaphoreType.DMA
    extra = [] if after is None else [after]

    moved = n_units * SC_UNIT * d * jnp.dtype(dtype).itemsize
    @functools.partial(
        pl.kernel, mesh=mesh,
        cost_estimate=pl.CostEstimate(flops=0, transcendentals=0, bytes_accessed=3 * moved + _nbytes(idx0, idx1)),
        out_type=jax.ShapeDtypeStruct((n_out_rows, d), dtype),
        scratch_types=[
            pltpu.VMEM((upw, SC_UNIT), I32),
            pltpu.VMEM((upw, SC_UNIT), I32),
            pltpu.VMEM((SC_UNIT, d), dtype),
            pltpu.VMEM((SC_UNIT, d), dtype),
            dma, dma, dma, dma, dma, dma,
        ],
    )
    def k(*refs):
        src_hbm = refs[:len(srcs)]
        i0_hbm, i1_hbm, out_hbm, i0_v, i1_v, rows0, rows1, l0, l1, p0, p1, q0, q1 = refs[len(srcs) + len(extra):]
        wid = lax.axis_index("s") * nc + lax.axis_index("c")
        pltpu.sync_copy(i0_hbm.at[wid], i0_v)
        pltpu.sync_copy(i1_hbm.at[wid], i1_v)
        rows, lsem, psem, qsem = (rows0, rows1), (l0, l1), (p0, p1), (q0, q1)

        def live(j):
            return j * nw + wid < n_units

        def load(j, b, op):
            unit = j * nw + wid

            @pl.when(live(j) & (unit < units_a))
            def _():
                op(pltpu.make_async_copy(
                    src_hbm[0].at[pl.ds(pl.multiple_of(unit * SC_UNIT, 8), SC_UNIT)], rows[b], lsem[b]))

            if len(srcs) == 2:
                @pl.when(live(j) & (unit >= units_a))
                def _():
                    op(pltpu.make_async_copy(
                        src_hbm[1].at[pl.ds(pl.multiple_of((unit - units_a) * SC_UNIT, 8), SC_UNIT)],
                        rows[b], lsem[b]))

        def scatter(j, b, op):
            @pl.when(live(j))
            def _():
                op(pltpu.make_async_copy(rows[b], out_hbm.at[i0_v.at[j]], psem[b]))
                op(pltpu.make_async_copy(rows[b], out_hbm.at[i1_v.at[j]], qsem[b]))

        start = lambda c: c.start()
        wait = lambda c: c.wait()
        load(0, 0, start)

        @pl.loop(0, upw, step=2)
        def _(j):
            @pl.when(j > 0)
            def _():
                scatter(j - 1, 1, wait)
            load(j + 1, 1, start)
            load(j, 0, wait)
            scatter(j, 0, start)
            scatter(j, 0, wait)

            @pl.when(j + 2 < upw)
            def _():
                load(j + 2, 0, start)
            load(j + 1, 1, wait)
            scatter(j + 1, 1, start)

        scatter(upw - 1, 1, wait)

    return k(*srcs, *extra, idx0, idx1), idx1


def _sc_gather(table, idx):
    n = idx.shape[0]
    d = table.shape[1]
    assert n % SC_UNIT == 0
    n_units = n // SC_UNIT
    nc, nw, upw = _sc_partition(n_units)
    idx = _units_by_worker(idx, n_units, upw, nw)
    mesh = plsc.VectorSubcoreMesh(core_axis_name="c", subcore_axis_name="s")
    dma = pltpu.SemaphoreType.DMA

    @functools.partial(
        pl.kernel, mesh=mesh,
        cost_estimate=pl.CostEstimate(flops=0, transcendentals=0,
                                      bytes_accessed=2 * n * d * table.dtype.itemsize + _nbytes(idx)),
        out_type=jax.ShapeDtypeStruct((n, d), table.dtype),
        scratch_types=[
            pltpu.VMEM((upw, SC_UNIT), I32),
            pltpu.VMEM((SC_UNIT, d), table.dtype),
            pltpu.VMEM((SC_UNIT, d), table.dtype),
            dma, dma, dma, dma,
        ],
    )
    def k(t_hbm, i_hbm, out_hbm, i_v, rows0, rows1, g0, g1, w0, w1):
        wid = lax.axis_index("s") * nc + lax.axis_index("c")
        pltpu.sync_copy(i_hbm.at[wid], i_v)
        rows, gsem, wsem = (rows0, rows1), (g0, g1), (w0, w1)

        def live(j):
            return j * nw + wid < n_units

        def gather(j, b, op):
            @pl.when(live(j))
            def _():
                op(pltpu.make_async_copy(t_hbm.at[i_v.at[j]], rows[b], gsem[b]))

        def write(j, b, op):
            @pl.when(live(j))
            def _():
                op(pltpu.make_async_copy(
                    rows[b], out_hbm.at[pl.ds(pl.multiple_of((j * nw + wid) * SC_UNIT, 8), SC_UNIT)], wsem[b]))

        start = lambda c: c.start()
        wait = lambda c: c.wait()
        gather(0, 0, start)

        @pl.loop(0, upw, step=2)
        def _(j):
            @pl.when(j > 0)
            def _():
                write(j - 1, 1, wait)
            gather(j + 1, 1, start)
            gather(j, 0, wait)
            write(j, 0, start)
            write(j, 0, wait)

            @pl.when(j + 2 < upw)
            def _():
                gather(j + 2, 0, start)
            gather(j + 1, 1, wait)
            write(j + 1, 1, start)

        write(upw - 1, 1, wait)

    return k(table, idx)


def _moe_kernel(start0_ref, count0_ref, gtot0_ref, start1_ref, count1_ref, gtot1_ref,
                xs0_hbm, xs1_hbm, wg_ref, wu_ref, wd_ref, ys0_hbm, ys1_hbm,
                wgu_s, wd_s, xbuf0, ybuf0, xbuf1, ybuf1, sem_in0, sem_out0, sem_in1, sem_out1):
    e = pl.program_id(0)
    last = pl.num_programs(0) - 1
    hidden = wd_s.shape[0]
    half = xbuf0.shape[-1]
    segments = (
        (xs0_hbm, ys0_hbm, xbuf0, ybuf0, sem_in0, sem_out0, start0_ref[e], count0_ref[e], gtot0_ref[0]),
        (xs1_hbm, ys1_hbm, xbuf1, ybuf1, sem_in1, sem_out1, start1_ref[e], count1_ref[e], gtot1_ref[0]),
    )

    def rows_of(g):
        return pl.ds(pl.multiple_of(g * MOE_TILE, MOE_TILE), MOE_TILE)

    def pipeline(xs_hbm, ys_hbm, xbuf, ybuf, sem_in, sem_out):
        def copy_in(g):
            slot = g % MOE_BUFFERS
            return pltpu.make_async_copy(xs_hbm.at[rows_of(g)], xbuf.at[slot], sem_in.at[slot])

        def copy_out(g):
            slot = g % MOE_BUFFERS
            return pltpu.make_async_copy(ybuf.at[slot], ys_hbm.at[rows_of(g)], sem_out.at[slot])
        return copy_in, copy_out

    @pl.when(e == 0)
    def _():
        for xs_hbm, ys_hbm, xbuf, ybuf, sem_in, sem_out, _, _, g_total in segments:
            copy_in, _ = pipeline(xs_hbm, ys_hbm, xbuf, ybuf, sem_in, sem_out)
            for g in range(MOE_LOOKAHEAD):
                @pl.when(g < g_total)
                def _():
                    copy_in(g).start()

    @pl.when(segments[0][7] + segments[1][7] > 0)
    def _():
        wgu_s[:, 0:hidden] = wg_ref[0].astype(BF16)
        wgu_s[:, hidden:2 * hidden] = wu_ref[0].astype(BF16)
        wd_s[...] = wd_ref[0].astype(BF16)

    def expert_rows(xbuf, ybuf, slot, valid):
        row = lax.broadcasted_iota(I32, (MOE_TILE, half), 0)
        x_lo, x_hi = _unpack_bf16_pair(jnp.where(row < valid, xbuf[slot], jnp.uint32(0)))
        ab = (jnp.dot(x_lo.astype(BF16), wgu_s[0:half, :], preferred_element_type=F32)
              + jnp.dot(x_hi.astype(BF16), wgu_s[half:2 * half, :], preferred_element_type=F32))
        a = ab[:, 0:hidden]
        he = a * _sigmoid(a) * ab[:, hidden:2 * hidden]
        y = jnp.dot(he.astype(BF16), wd_s[...], preferred_element_type=F32)
        ybuf[slot] = _pack_bf16_pair(y[:, 0:half], y[:, half:2 * half])

    for xs_hbm, ys_hbm, xbuf, ybuf, sem_in, sem_out, start, count, g_total in segments:
        copy_in, copy_out = pipeline(xs_hbm, ys_hbm, xbuf, ybuf, sem_in, sem_out)
        g_first = start // MOE_TILE
        n_tiles = (count + MOE_TILE - 1) // MOE_TILE

        def tiles(t, width, copy_in=copy_in, copy_out=copy_out, xbuf=xbuf, ybuf=ybuf,
                  g_first=g_first, count=count, g_total=g_total):
            gs = [g_first + t + i for i in range(width)]
            for g in gs:
                @pl.when(g + MOE_LOOKAHEAD < g_total)
                def _():
                    copy_in(g + MOE_LOOKAHEAD).start()
            for g in gs:
                copy_in(g).wait()

                @pl.when(g >= MOE_BUFFERS)
                def _():
                    copy_out(g - MOE_BUFFERS).wait()
            for i, g in enumerate(gs):
                expert_rows(xbuf, ybuf, g % MOE_BUFFERS, count - (t + i) * MOE_TILE)
            for g in gs:
                copy_out(g).start()

        def pair(p, carry, tiles=tiles):
            tiles(MOE_UNROLL * p, MOE_UNROLL)
            return carry

        lax.fori_loop(0, n_tiles // MOE_UNROLL, pair, 0)

        def single(r, carry, tiles=tiles, n_tiles=n_tiles):
            tiles(n_tiles // MOE_UNROLL * MOE_UNROLL + r, 1)
            return carry

        lax.fori_loop(0, n_tiles % MOE_UNROLL, single, 0)

        @pl.when(e == last)
        def _(copy_out=copy_out, g_total=g_total):
            for j in range(1, MOE_BUFFERS + 1):
                @pl.when(g_total >= j)
                def _():
                    copy_out(g_total - j).wait()


def _moe_call(groups, w_g, w_u, w_d):
    (xs0, starts0, cnt0), (xs1, starts1, cnt1) = groups
    half = xs0.shape[1]
    n_experts, d_model, hidden = w_g.shape

    def tiles_total(starts, cnt):
        return ((starts[-1:] + cnt[-1:] + MOE_TILE - 1) // MOE_TILE).astype(I32)

    wspec = lambda shape: pl.BlockSpec(shape, lambda e, *_: (e, 0, 0))
    tile_bufs = [pltpu.VMEM((MOE_BUFFERS, MOE_TILE, half), U32)] * 4
    grid_spec = pltpu.PrefetchScalarGridSpec(
        num_scalar_prefetch=6,
        grid=(n_experts,),
        in_specs=[
            pl.BlockSpec(memory_space=pl.ANY),
            pl.BlockSpec(memory_space=pl.ANY),
            wspec((1, d_model, hidden)),
            wspec((1, d_model, hidden)),
            wspec((1, hidden, d_model)),
        ],
        out_specs=[pl.BlockSpec(memory_space=pl.ANY), pl.BlockSpec(memory_space=pl.ANY)],
        scratch_shapes=[
            pltpu.VMEM((d_model, 2 * hidden), BF16),
            pltpu.VMEM((hidden, d_model), BF16),
            *tile_bufs,
            *[pltpu.SemaphoreType.DMA((MOE_BUFFERS,))] * 4,
        ],
    )
    n_rows = xs0.shape[0] + xs1.shape[0]
    cost = pl.CostEstimate(flops=n_rows * 6 * d_model * hidden, transcendentals=n_rows * hidden,
                           bytes_accessed=2 * _nbytes(xs0, xs1) + _nbytes(w_g, w_u, w_d))
    return pl.pallas_call(
        _moe_kernel, grid_spec=grid_spec, cost_estimate=cost,
        out_shape=[jax.ShapeDtypeStruct(xs0.shape, U32), jax.ShapeDtypeStruct(xs1.shape, U32)], name="moe_experts",
        compiler_params=pltpu.CompilerParams(
            dimension_semantics=("arbitrary",), vmem_limit_bytes=VMEM_LIMIT),
    )(starts0, cnt0, tiles_total(starts0, cnt0), starts1, cnt1, tiles_total(starts1, cnt1),
      xs0, xs1, w_g, w_u, w_d)


def _combine_kernel(x1_ref, y0_ref, y1_ref, rw_ref, gf_ref, *rest):
    out_ref = rest[-1]
    tr = x1_ref.shape[0]
    w_rows = jnp.concatenate([rw_ref[0], jnp.zeros((LANES - rw_ref.shape[1], tr), F32)], axis=0)
    w_cols = w_rows.T
    w0, w1 = w_cols[:, 0:1], w_cols[:, 1:2]
    a_lo, a_hi = _unpack_bf16_pair(y0_ref[0])
    b_lo, b_hi = _unpack_bf16_pair(y1_ref[0])
    moe = jnp.concatenate([w0 * a_lo + w1 * b_lo, w0 * a_hi + w1 * b_hi], axis=-1)
    out_ref[...] = _rms(x1_ref[...] + moe, gf_ref[...])


def _combine_call(x1, rw, row0, n, yg, gf, out_rows, out_row0, prev_out=None):
    t, d_model = x1.shape
    tr = rw.shape[-1]
    half = yg.shape[-1]
    assert t % tr == 0 and row0 % tr == 0 and n % tr == 0 and rw.shape == (t // tr, EXPERTS_PER_GROUP, tr)
    assert yg.shape == (2, n, half) and out_row0 % tr == 0
    off = row0 // tr
    ooff = out_row0 // tr
    in_specs = [
        pl.BlockSpec((tr, d_model), lambda i: (off + i, 0)),
        pl.BlockSpec((1, tr, half), lambda i: (0, i, 0)),
        pl.BlockSpec((1, tr, half), lambda i: (1, i, 0)),
        pl.BlockSpec((1, EXPERTS_PER_GROUP, tr), lambda i: (off + i, 0, 0)),
        pl.BlockSpec((1, d_model), lambda i: (0, 0)),
    ]
    args = [x1, yg, yg, rw, gf]
    aliases = {}
    if prev_out is not None:
        in_specs.append(pl.BlockSpec(memory_space=pl.ANY))
        args.append(prev_out)
        aliases = {len(args) - 1: 0}
    return pl.pallas_call(
        _combine_kernel,
        grid=(n // tr,),
        in_specs=in_specs,
        out_specs=pl.BlockSpec((tr, d_model), lambda i: (ooff + i, 0)),
        out_shape=jax.ShapeDtypeStruct((out_rows, d_model), F32), name=f"combine_row{out_row0}_of{out_rows}",
        cost_estimate=pl.CostEstimate(flops=8 * n * d_model, transcendentals=n,
                                      bytes_accessed=2 * n * d_model * 4 + _nbytes(yg) + n * 4 * EXPERTS_PER_GROUP),
        input_output_aliases=aliases,
        compiler_params=pltpu.CompilerParams(
            dimension_semantics=("arbitrary",), vmem_limit_bytes=VMEM_LIMIT),
    )(*args)


def _route(streams, after=None):
    tokens = [h2.shape[0] for h2, _, _ in streams]
    counts = [cnt[:, 0].astype(I32) for _, _, cnt in streams]
    total = sum(counts)
    padded = ((total + MOE_TILE - 1) // MOE_TILE) * MOE_TILE
    starts = (jnp.cumsum(padded) - padded).astype(I32)
    experts = jnp.arange(N_EXPERTS, dtype=I32)[None, :, None]
    pos, base = [], starts
    for (_, ri, _), t, cnt in zip(streams, tokens, counts):
        ri = jnp.moveaxis(ri, 2, 0).reshape(ri.shape[2], t)
        first_row = jnp.sum(jnp.where(ri[0:2, None, :] == experts, base[None, :, None], 0), axis=1)
        pos.append(ri[2:4] + first_row)
        base = base + cnt
    pos = jnp.concatenate(pos, axis=1)
    n_rows = ((2 * sum(tokens) + N_EXPERTS * (MOE_TILE - 1)) // MOE_TILE) * MOE_TILE
    xs_sorted, ready = _sc_dispatch([h2 for h2, _, _ in streams], pos[0], pos[1], n_rows, after)
    return (xs_sorted, starts, total), pos, ready


def _gather_tokens(ys_sorted, pos, t0, n):
    return _sc_gather(ys_sorted, pos[:, t0:t0 + n].reshape(2 * n)).reshape(2, n, ys_sorted.shape[-1])


def _one_layer(xp, xs, s_ret, c_pool, norm1_g, w_in, ret_norm_g, w_pool, pool_scale, w_out, norm2_g,
               w_rg, w_re, w_g, w_u, w_d, final_g, past_len):
    bp, seq, d_model = xp.shape
    bs, dseq, _ = xs.shape
    rw_width = ret_norm_g.shape[-1]
    pw = pool_scale.shape[-1]
    dh = rw_width // RET_HEADS
    half = d_model // 2

    w_r = jnp.concatenate(
        [w_re.T, w_rg.T, jnp.zeros((ROUTER_ROWS - N_EXPERTS - N_EXPERT_GROUPS, d_model), F32)], axis=0)
    wr_hi = w_r.astype(BF16)
    wr = jnp.concatenate([wr_hi, (w_r - wr_hi.astype(F32)).astype(BF16)], axis=0)
    consts = dict(
        g1=norm1_g.reshape(1, d_model), w_in=w_in.astype(BF16), gret=ret_norm_g.reshape(1, rw_width),
        w_pool=w_pool.astype(BF16), pscale=pool_scale.reshape(1, pw), w_out=w_out.astype(BF16),
        g2=norm2_g.reshape(1, d_model), wr=wr)

    gf = final_g.reshape(1, d_model)

    ts = bs * dseq
    b_lead = bp - 1
    t_lead, t_rest = b_lead * seq, (bp - b_lead) * seq
    zeros = lambda nb: (jnp.zeros((nb, RET_HEADS, dh, dh), F32), jnp.zeros((nb, HIST_ROWS, pw), F32))
    h0s = jnp.pad(c_pool, ((0, 0), (HIST_ROWS - POOL_HIST, 0), (0, 0)))
    prompt_tile = dict(bb=1, tl=PROMPT_TILE, chunk=256)

    def stream(layer_out, t):
        x1, h2, ri, rw, st, hist, cnt = layer_out
        return dict(x1=x1.reshape(t, d_model), route=(h2.reshape(t, half), ri, cnt),
                    rw=rw.reshape(-1, EXPERTS_PER_GROUP, rw.shape[-1]), st=st, hist=hist)

    pa = stream(_layer_call(xp, 0, b_lead, *zeros(b_lead), 0, consts, gf, **prompt_tile), t_lead)
    group0, pos0, ready0 = _route([pa["route"]])
    pb = stream(_layer_call(xp, b_lead, bp - b_lead, *zeros(bp - b_lead), 0, consts, ready0, **prompt_tile),
                t_rest)
    sm = stream(_layer_call(xs, 0, bs, s_ret, h0s, past_len, consts, pb["route"][2],
                            bb=bs, tl=dseq, chunk=min(64, dseq)), ts)
    group1, pos1, _ = _route([pb["route"], sm["route"]], after=group0[0])
    ys0, ys1 = _moe_call([group0, group1], w_g, w_u, w_d)

    tp = bp * seq
    yp = _combine_call(pb["x1"], pb["rw"], 0, t_rest, _gather_tokens(ys1, pos1, 0, t_rest), gf, tp, t_lead)
    ysm = _combine_call(sm["x1"], sm["rw"], 0, ts, _gather_tokens(ys1, pos1, t_rest, ts), gf, ts, 0)
    row0 = 0
    for nb in COMBINE_CHUNKS:
        n = min(nb * seq, t_lead - row0)
        if n > 0:
            yp = _combine_call(pa["x1"], pa["rw"], row0, n, _gather_tokens(ys0, pos0, row0, n), gf, tp, row0,
                               prev_out=yp)
            row0 += n
    assert row0 == t_lead
    st_p = jnp.concatenate([pa["st"], pb["st"]], axis=0)
    hist_p = jnp.concatenate([pa["hist"], pb["hist"]], axis=0)
    return (yp.reshape(bp, seq, d_model), ysm.reshape(bs, dseq, d_model),
            st_p, hist_p[:, HIST_ROWS - POOL_HIST:], sm["st"], sm["hist"][:, HIST_ROWS - POOL_HIST:])


def kernel(x_prompt, x_sample, state_ret, cache_pool, norm1_g, w_in, ret_norm_g, w_pool, pool_scale, w_out,
           norm2_g, w_router_group, w_router_expert, w_exp_gate, w_exp_up, w_exp_down, final_norm_g):
    depth = w_in.shape[0]
    assert depth == 1, "the final RMSNorm is fused into the layer's combine kernel"
    assert x_prompt.shape[0] >= 2 and x_prompt.shape[1] % PROMPT_TILE == 0
    yp, ys, s_p, h_p, s_s, h_s = _one_layer(
        x_prompt, x_sample, state_ret[0], cache_pool[0], norm1_g[0], w_in[0], ret_norm_g[0], w_pool[0],
        pool_scale[0], w_out[0], norm2_g[0], w_router_group[0], w_router_expert[0],
        w_exp_gate[0], w_exp_up[0], w_exp_down[0], final_norm_g, PAST_LEN)
    return (yp, ys, s_p[None], h_p[None], s_s[None], h_s[None])
```

```python
import functools

import jax
import jax.numpy as jnp
from jax import lax
from jax.experimental import pallas as pl
from jax.experimental.pallas import tpu as pltpu
from jax.experimental.pallas import tpu_sc as plsc

F32 = jnp.float32
BF16 = jnp.bfloat16
I32 = jnp.int32
U32 = jnp.uint32

EPS = 1e-6
ROPE_BASE = 10000.0
RET_HEADS = 4
POOL_WINDOWS = (2, 4, 8, 16)
POOL_HIST = max(POOL_WINDOWS) - 1
N_EXPERT_GROUPS = 4
EXPERTS_PER_GROUP = 8
N_EXPERTS = N_EXPERT_GROUPS * EXPERTS_PER_GROUP
ROUTER_ROWS = 48
PAST_LEN = 1024

LANES = 128
HIST_ROWS = 24
POOL_EXT = 16
POOL_SLACK = 8
MOE_TILE = 256
MOE_BUFFERS = 6
MOE_UNROLL = 2
MOE_LOOKAHEAD = MOE_BUFFERS - MOE_UNROLL
PREP_ROWS = 512
PROMPT_TILE = 1024
COMBINE_CHUNKS = (1, 2)
SC_UNIT = 32
VMEM_LIMIT = 56 * 1024 * 1024


def _nbytes(*arrays):
    return sum(a.size * a.dtype.itemsize for a in arrays)


def _rms(x, g):
    return x * lax.rsqrt(jnp.mean(x * x, axis=-1, keepdims=True) + EPS) * g


def _sigmoid(x):
    return 1.0 / (1.0 + jnp.exp(-x))


def _pack_bf16_pair(lo, hi):
    lo_b = lax.bitcast_convert_type(lo.astype(BF16).astype(F32), U32)
    hi_b = lax.bitcast_convert_type(hi.astype(BF16).astype(F32), U32)
    return hi_b | (lo_b >> 16)


def _unpack_bf16_pair(p):
    lo = lax.bitcast_convert_type(p << 16, F32)
    hi = lax.bitcast_convert_type(p & jnp.uint32(0xFFFF0000), F32)
    return lo, hi


def _layer_kernel(dc_ref, x_ref, s0_ref, h0_ref, rb_ref, rc_ref, rs_ref, rcs_ref, rss_ref,
                  dintra_ref, dq_ref, dk_ref,
                  g1_ref, win_ref, gret_ref, wpool_ref, pscale_ref, wout_ref, g2_ref,
                  wr_ref, tri_ref, after_ref,
                  x1_ref, h2_ref, ri_ref, rw_ref, st_ref, hist_ref, cnt_ref,
                  ue_ref, ps_ref, q_ref, k_ref, v_ref, gate_ref, a_ref,
                  *, bb, tl, chunk, pos0):
    b_idx = pl.program_id(0)
    l_idx = pl.program_id(1)
    rows = bb * tl
    d_model = x_ref.shape[-1]
    rw_width = q_ref.shape[-1]
    dh = rw_width // RET_HEADS
    pw = ue_ref.shape[-1]
    gw = pw // len(POOL_WINDOWS)
    n_chunks = tl // chunk

    @pl.when(l_idx == 0)
    def _():
        st_ref[...] = s0_ref[...]
        ue_ref[:, 0:HIST_ROWS, :] = h0_ref[...]
        ps_ref[:, :, 0:POOL_SLACK, :] = jnp.zeros((2, bb, POOL_SLACK, gw), F32)

    @pl.when((l_idx == 0) & (b_idx == 0))
    def _():
        cnt_ref[...] = jnp.zeros_like(cnt_ref)

    cos_b = rb_ref[0, 0:1, :]
    sin_b = rb_ref[0, 1:2, :]
    cosf = cos_b * rc_ref[...] - sin_b * rs_ref[...]
    sinf = sin_b * rcs_ref[...] + cos_b * rss_ref[...]
    k_scale = dh ** -0.5
    n_blocks = max(1, tl // PREP_ROWS) if bb == 1 else 1
    block = rows // n_blocks
    for blk in range(n_blocks):
        rs = slice(blk * block, (blk + 1) * block)
        xb = x_ref[0, rs, :] if bb == 1 else x_ref[...].reshape(rows, d_model)
        hb = _rms(xb, g1_ref[...]).astype(BF16)
        proj = jnp.dot(hb, win_ref[...], preferred_element_type=F32)

        def rotate(a):
            if bb == 1:
                return a * cosf[rs] + pltpu.roll(a, dh // 2, 1) * sinf[rs]
            return (a.reshape(bb, tl, dh) * cosf[None]
                    + pltpu.roll(a, dh // 2, 1).reshape(bb, tl, dh) * sinf[None]).reshape(rows, dh)

        for hh in range(RET_HEADS):
            cs = slice(hh * dh, (hh + 1) * dh)
            q_ref[rs, cs] = rotate(proj[:, hh * dh:(hh + 1) * dh]).astype(BF16)
            k_ref[rs, cs] = rotate(proj[:, rw_width + hh * dh:rw_width + (hh + 1) * dh]) * k_scale
        v_ref[rs, :] = proj[:, 2 * rw_width:3 * rw_width].astype(BF16)
        gate_ref[rs, :] = proj[:, 3 * rw_width:4 * rw_width]
        u = proj[:, 4 * rw_width:4 * rw_width + pw]
        if bb == 1:
            ue_ref[0, HIST_ROWS + blk * block:HIST_ROWS + (blk + 1) * block, :] = u
        else:
            ue_ref[:, HIST_ROWS:HIST_ROWS + tl, :] = u.reshape(bb, tl, pw)

    def ret_block(b, c):
        r0 = b * tl + c * chunk
        if not isinstance(r0, int):
            r0 = pl.multiple_of(r0, chunk)
        rr = pl.ds(r0, chunk)
        for hh in range(RET_HEADS):
            cs = slice(hh * dh, (hh + 1) * dh)
            qc = q_ref[rr, cs]
            kf = k_ref[rr, cs]
            vc = v_ref[rr, cs]
            s_old = st_ref[b, hh]
            sc = lax.dot_general(qc, kf.astype(BF16), (((1,), (1,)), ((), ())),
                                 preferred_element_type=F32) * dintra_ref[hh]
            o = (jnp.dot(sc.astype(BF16), vc, preferred_element_type=F32)
                 + dq_ref[hh] * jnp.dot(qc, s_old.astype(BF16), preferred_element_type=F32))
            kd = (kf * dk_ref[hh]).astype(BF16)
            s_new = dc_ref[hh] * s_old + lax.dot_general(
                kd, vc, (((0,), (0,)), ((), ())), preferred_element_type=F32)
            st_ref[b, hh] = s_new
            oc = o - jnp.mean(o, axis=-1, keepdims=True)
            var = jnp.mean(oc * oc, axis=-1, keepdims=True)
            y = oc * lax.rsqrt(var + EPS) * gret_ref[:, cs]
            g = gate_ref[rr, cs]
            a_ref[rr, cs] = (g * _sigmoid(g) * y).astype(BF16)

    if bb * n_chunks <= 4:
        for b in range(bb):
            for c in range(n_chunks):
                ret_block(b, c)
    else:
        def body(i, carry):
            ret_block(i // n_chunks, i % n_chunks)
            return carry
        lax.fori_loop(0, bb * n_chunks, body, 0)

    nt = (((1,), (1,)), ((), ()))
    neg = jnp.float32(-jnp.inf)
    big = jnp.float32(1e9)
    sub = lax.broadcasted_iota(I32, (EXPERTS_PER_GROUP, block), 0).astype(F32)
    eid = lax.broadcasted_iota(I32, (N_EXPERTS, block), 0).astype(F32)
    for blk in range(n_blocks):
        lo = blk * block
        rs = slice(lo, lo + block)

        bsel = 0 if bb == 1 else slice(None)
        base_row = HIST_ROWS + (lo if bb == 1 else 0)
        if bb == 1:
            pos = pos0 + l_idx * tl + lo + lax.broadcasted_iota(I32, (block, 1), 0)
        else:
            pos = pos0 + l_idx * tl + lax.broadcasted_iota(I32, (1, tl, 1), 1)
        n_rows = block if bb == 1 else tl
        for gi, w in enumerate(POOL_WINDOWS):
            cs = slice(gi * gw, (gi + 1) * gw)
            read = lambda r0, n, cs=cs: ue_ref[bsel, base_row + r0:base_row + r0 + n, cs]
            u_g = read(0, n_rows)
            shifts = [w >> (k + 1) for k in range(w.bit_length() - 1)]
            for k, sh in enumerate(shifts):
                if k == len(shifts) - 1:
                    acc = read(0, n_rows) + read(-sh, n_rows)
                else:
                    level = read(-POOL_EXT, POOL_EXT + n_rows) + read(-POOL_EXT - sh, POOL_EXT + n_rows)
                    ps_ref[k % 2, bsel, POOL_SLACK:POOL_SLACK + POOL_EXT + n_rows, :] = level
                    read = lambda r0, n, k=k: ps_ref[k % 2, bsel,
                                                     POOL_SLACK + POOL_EXT + r0:POOL_SLACK + POOL_EXT + r0 + n, :]
            inv_cnt = 1.0 / jnp.minimum(pos + 1, w).astype(F32)
            p = (acc * inv_cnt - u_g).reshape(block, gw)
            z = jnp.dot(p.astype(BF16), wpool_ref[gi], preferred_element_type=F32) * pscale_ref[:, cs]
            a_ref[rs, rw_width + gi * gw:rw_width + (gi + 1) * gw] = z.astype(BF16)

        xb = x_ref[0, rs, :] if bb == 1 else x_ref[...].reshape(rows, d_model)
        x1 = xb + jnp.dot(a_ref[rs, :], wout_ref[...], preferred_element_type=F32)
        h2 = _rms(x1, g2_ref[...])
        h2_packed = _pack_bf16_pair(h2[:, 0:d_model // 2], h2[:, d_model // 2:])
        if bb == 1:
            x1_ref[0, rs, :] = x1
            h2_ref[0, rs, :] = h2_packed
        else:
            x1_ref[...] = x1.reshape(bb, tl, d_model)
            h2_ref[...] = h2_packed.reshape(bb, tl, d_model // 2)

        h2_hi = h2.astype(BF16)
        h2_lo = (h2 - h2_hi.astype(F32)).astype(BF16)
        two = lax.dot_general(wr_ref[...], h2_hi, nt, preferred_element_type=F32)
        lt = (two[0:ROUTER_ROWS] + two[ROUTER_ROWS:2 * ROUTER_ROWS]
              + lax.dot_general(wr_ref[0:ROUTER_ROWS, :], h2_lo, nt, preferred_element_type=F32))
        gl = jnp.where(sub < N_EXPERT_GROUPS, lt[N_EXPERTS:N_EXPERTS + EXPERTS_PER_GROUP], neg)
        gmax = jnp.max(gl, axis=0, keepdims=True)
        gidx = jnp.min(jnp.where(gl == gmax, sub, big), axis=0, keepdims=True)
        p_sel = 1.0 / jnp.sum(jnp.exp(gl - gmax), axis=0, keepdims=True)
        el = lt[0:EXPERTS_PER_GROUP]
        for g in range(1, N_EXPERT_GROUPS):
            el = jnp.where(gidx == g, lt[g * EXPERTS_PER_GROUP:(g + 1) * EXPERTS_PER_GROUP], el)
        m1 = jnp.max(el, axis=0, keepdims=True)
        t1 = jnp.min(jnp.where(el == m1, sub, big), axis=0, keepdims=True)
        el2 = jnp.where(sub == t1, neg, el)
        m2 = jnp.max(el2, axis=0, keepdims=True)
        t2 = jnp.min(jnp.where(el2 == m2, sub, big), axis=0, keepdims=True)
        e2 = jnp.exp(m2 - m1)
        w1 = p_sel / (1.0 + e2)
        w2 = p_sel * e2 / (1.0 + e2)
        i1 = gidx * EXPERTS_PER_GROUP + t1
        i2 = gidx * EXPERTS_PER_GROUP + t2

        hit1 = eid == i1
        hit2 = eid == i2
        onehot = (hit1 | hit2).astype(BF16)
        before = jnp.dot(onehot, tri_ref[...], preferred_element_type=F32) + cnt_ref[...]
        r1 = jnp.sum(jnp.where(hit1, before, 0.0), axis=0, keepdims=True)
        r2 = jnp.sum(jnp.where(hit2, before, 0.0), axis=0, keepdims=True)
        cnt_ref[...] = cnt_ref[...] + jnp.sum(onehot.astype(F32), axis=1, keepdims=True)

        ri = jnp.where(sub == 0, i1, jnp.where(sub == 1, i2, jnp.where(sub == 2, r1, jnp.where(sub == 3, r2, 0.0))))
        ri_ref[0, 0, :, rs] = ri.astype(I32)
        rw_ref[0, 0, :, rs] = jnp.where(sub == 0, w1, jnp.where(sub == 1, w2, 0.0))

    tail = ue_ref[:, tl:tl + HIST_ROWS, :]
    ue_ref[:, 0:HIST_ROWS, :] = tail
    hist_ref[...] = tail


def _rope_tables(pos0, seq, tl, dh):
    half = dh // 2
    inv = ROPE_BASE ** (-jnp.arange(half, dtype=F32) / half)
    ang_t = jnp.arange(tl, dtype=F32)[:, None] * inv[None, :]
    ang_b = (pos0 + tl * jnp.arange(seq // tl)).astype(F32)[:, None] * inv[None, :]
    dup = lambda a: jnp.concatenate([a, a], axis=-1)
    sgn = lambda a: jnp.concatenate([-a, a], axis=-1)
    base = jnp.stack([dup(jnp.cos(ang_b)), dup(jnp.sin(ang_b))], axis=1)
    base = jnp.pad(base, ((0, 0), (0, 8 - base.shape[1]), (0, 0)))
    cos_t, sin_t = jnp.cos(ang_t), jnp.sin(ang_t)
    return base, dup(cos_t), dup(sin_t), sgn(cos_t), sgn(sin_t)


def _layer_call(x, b0, nb, s0, h0, pos0, consts, after, *, bb, tl, chunk):
    _, seq, d_model = x.shape
    bsz = nb
    blk0 = b0 // bb
    rows = bb * tl
    rw_width = consts["gret"].shape[-1]
    pw = consts["pscale"].shape[-1]
    dh = rw_width // RET_HEADS

    rope = _rope_tables(pos0, seq, tl, dh)

    lg = jnp.log1p(-jnp.exp2(-5.0 - jnp.arange(RET_HEADS, dtype=F32)))
    idx = jnp.arange(chunk, dtype=F32)
    diff = idx[:, None] - idx[None, :]
    d_intra = jnp.where(diff[None] >= 0, jnp.exp(jnp.maximum(diff, 0.0)[None] * lg[:, None, None]), 0.0)
    d_q = jnp.broadcast_to(jnp.exp((idx + 1.0)[None, :] * lg[:, None])[:, :, None], (RET_HEADS, chunk, dh))
    d_k = jnp.broadcast_to(jnp.exp((chunk - 1.0 - idx)[None, :] * lg[:, None])[:, :, None], (RET_HEADS, chunk, dh))
    d_c = jnp.exp(chunk * lg)
    block = rows // (max(1, tl // PREP_ROWS) if bb == 1 else 1)
    tri = jnp.triu(jnp.ones((block, block), BF16), 1)

    const2 = lambda b, l, *_: (0, 0)
    const3 = lambda b, l, *_: (0, 0, 0)
    grid_spec = pltpu.PrefetchScalarGridSpec(
        num_scalar_prefetch=0,
        grid=(bsz // bb, seq // tl),
        in_specs=[
            pl.BlockSpec(memory_space=pltpu.SMEM),
            pl.BlockSpec((bb, tl, d_model), lambda b, l: (blk0 + b, l, 0)),
            pl.BlockSpec((bb, RET_HEADS, dh, dh), lambda b, l: (b, 0, 0, 0)),
            pl.BlockSpec((bb, HIST_ROWS, pw), lambda b, l: (b, 0, 0)),
            pl.BlockSpec((1, 8, dh), lambda b, l: (l, 0, 0)),
            pl.BlockSpec((tl, dh), const2),
            pl.BlockSpec((tl, dh), const2),
            pl.BlockSpec((tl, dh), const2),
            pl.BlockSpec((tl, dh), const2),
            pl.BlockSpec((RET_HEADS, chunk, chunk), const3),
            pl.BlockSpec((RET_HEADS, chunk, dh), const3),
            pl.BlockSpec((RET_HEADS, chunk, dh), const3),
            pl.BlockSpec((1, d_model), const2),
            pl.BlockSpec(consts["w_in"].shape, const2),
            pl.BlockSpec((1, rw_width), const2),
            pl.BlockSpec(consts["w_pool"].shape, const3),
            pl.BlockSpec((1, pw), const2),
            pl.BlockSpec(consts["w_out"].shape, const2),
            pl.BlockSpec((1, d_model), const2),
            pl.BlockSpec((2 * ROUTER_ROWS, d_model), const2),
            pl.BlockSpec((block, block), const2),
            pl.BlockSpec(memory_space=pl.ANY),
        ],
        out_specs=[
            pl.BlockSpec((bb, tl, d_model), lambda b, l: (b, l, 0)),
            pl.BlockSpec((bb, tl, d_model // 2), lambda b, l: (b, l, 0)),
            pl.BlockSpec((1, 1, EXPERTS_PER_GROUP, rows), lambda b, l: (b, l, 0, 0)),
            pl.BlockSpec((1, 1, EXPERTS_PER_GROUP, rows), lambda b, l: (b, l, 0, 0)),
            pl.BlockSpec((bb, RET_HEADS, dh, dh), lambda b, l: (b, 0, 0, 0)),
            pl.BlockSpec((bb, HIST_ROWS, pw), lambda b, l: (b, 0, 0)),
            pl.BlockSpec((N_EXPERTS, block), const2),
        ],
        scratch_shapes=[
            pltpu.VMEM((bb, HIST_ROWS + tl, pw), F32),
            pltpu.VMEM((2, bb, POOL_SLACK + POOL_EXT + (block if bb == 1 else tl), pw // len(POOL_WINDOWS)), F32),
            pltpu.VMEM((rows, rw_width), BF16),
            pltpu.VMEM((rows, rw_width), F32),
            pltpu.VMEM((rows, rw_width), BF16),
            pltpu.VMEM((rows, rw_width), F32),
            pltpu.VMEM((rows, d_model), BF16),
        ],
    )
    out_shape = [
        jax.ShapeDtypeStruct((bsz, seq, d_model), F32),
        jax.ShapeDtypeStruct((bsz, seq, d_model // 2), U32),
        jax.ShapeDtypeStruct((bsz // bb, seq // tl, EXPERTS_PER_GROUP, rows), I32),
        jax.ShapeDtypeStruct((bsz // bb, seq // tl, EXPERTS_PER_GROUP, rows), F32),
        jax.ShapeDtypeStruct((bsz, RET_HEADS, dh, dh), F32),
        jax.ShapeDtypeStruct((bsz, HIST_ROWS, pw), F32),
        jax.ShapeDtypeStruct((N_EXPERTS, block), F32),
    ]
    kern = functools.partial(_layer_kernel, bb=bb, tl=tl, chunk=chunk, pos0=pos0)
    operands = (d_c, x, s0, h0, *rope, d_intra, d_q, d_k,
                consts["g1"], consts["w_in"], consts["gret"], consts["w_pool"], consts["pscale"],
                consts["w_out"], consts["g2"], consts["wr"], tri, after)
    n_tok = bsz * seq
    mm_flops_per_token = 2 * (d_model * consts["w_in"].shape[1] + d_model * d_model + 3 * d_model * ROUTER_ROWS
                              + pw * pw // 4
                              + rw_width * (2 * chunk + 2 * dh) + N_EXPERTS * block)
    cost = pl.CostEstimate(
        flops=n_tok * mm_flops_per_token, transcendentals=n_tok * (rw_width + 2 * N_EXPERT_GROUPS),
        bytes_accessed=_nbytes(*operands) - _nbytes(x, after) + n_tok * d_model * 4 + _nbytes(*out_shape))
    return pl.pallas_call(
        kern, grid_spec=grid_spec, out_shape=out_shape, name=f"layer_pos{pos0}_b{b0}", cost_estimate=cost,
        compiler_params=pltpu.CompilerParams(
            dimension_semantics=("arbitrary", "arbitrary"), vmem_limit_bytes=VMEM_LIMIT),
    )(*operands)


def _sc_partition(n_units):
    info = plsc.get_sparse_core_info()
    nc, nw = info.num_cores, info.num_cores * info.num_subcores
    upw = -(-n_units // nw)
    upw += upw % 2
    return nc, nw, upw


def _units_by_worker(idx, n_units, upw, nw):
    idx = jnp.pad(idx.reshape(n_units, SC_UNIT), ((0, nw * upw - n_units), (0, 0)))
    return idx.reshape(upw, nw, SC_UNIT).transpose(1, 0, 2)


def _sc_dispatch(srcs, idx0, idx1, n_out_rows, after=None):
    assert 1 <= len(srcs) <= 2
    d = srcs[0].shape[1]
    dtype = srcs[0].dtype
    assert all(src.shape[0] % SC_UNIT == 0 for src in srcs)
    units_a = srcs[0].shape[0] // SC_UNIT
    n_units = sum(src.shape[0] for src in srcs) // SC_UNIT
    nc, nw, upw = _sc_partition(n_units)
    idx0 = _units_by_worker(idx0, n_units, upw, nw)
    idx1 = _units_by_worker(idx1, n_units, upw, nw)
    mesh = plsc.VectorSubcoreMesh(core_axis_name="c", subcore_axis_name="s")
    dma = pltpu.SemaphoreType.DMA
    extra = [] if after is None else [after]

    moved = n_units * SC_UNIT * d * jnp.dtype(dtype).itemsize
    @functools.partial(
        pl.kernel, mesh=mesh,
        cost_estimate=pl.CostEstimate(flops=0, transcendentals=0, bytes_accessed=3 * moved + _nbytes(idx0, idx1)),
        out_type=jax.ShapeDtypeStruct((n_out_rows, d), dtype),
        scratch_types=[
            pltpu.VMEM((upw, SC_UNIT), I32),
            pltpu.VMEM((upw, SC_UNIT), I32),
            pltpu.VMEM((SC_UNIT, d), dtype),
            pltpu.VMEM((SC_UNIT, d), dtype),
            dma, dma, dma, dma, dma, dma,
        ],
    )
    def k(*refs):
        src_hbm = refs[:len(srcs)]
        i0_hbm, i1_hbm, out_hbm, i0_v, i1_v, rows0, rows1, l0, l1, p0, p1, q0, q1 = refs[len(srcs) + len(extra):]
        wid = lax.axis_index("s") * nc + lax.axis_index("c")
        pltpu.sync_copy(i0_hbm.at[wid], i0_v)
        pltpu.sync_copy(i1_hbm.at[wid], i1_v)
        rows, lsem, psem, qsem = (rows0, rows1), (l0, l1), (p0, p1), (q0, q1)

        def live(j):
            return j * nw + wid < n_units

        def load(j, b, op):
            unit = j * nw + wid

            @pl.when(live(j) & (unit < units_a))
            def _():
                op(pltpu.make_async_copy(
                    src_hbm[0].at[pl.ds(pl.multiple_of(unit * SC_UNIT, 8), SC_UNIT)], rows[b], lsem[b]))

            if len(srcs) == 2:
                @pl.when(live(j) & (unit >= units_a))
                def _():
                    op(pltpu.make_async_copy(
                        src_hbm[1].at[pl.ds(pl.multiple_of((unit - units_a) * SC_UNIT, 8), SC_UNIT)],
                        rows[b], lsem[b]))

        def scatter(j, b, op):
            @pl.when(live(j))
            def _():
                op(pltpu.make_async_copy(rows[b], out_hbm.at[i0_v.at[j]], psem[b]))
                op(pltpu.make_async_copy(rows[b], out_hbm.at[i1_v.at[j]], qsem[b]))

        start = lambda c: c.start()
        wait = lambda c: c.wait()
        load(0, 0, start)

        @pl.loop(0, upw, step=2)
        def _(j):
            @pl.when(j > 0)
            def _():
                scatter(j - 1, 1, wait)
            load(j + 1, 1, start)
            load(j, 0, wait)
            scatter(j, 0, start)
            scatter(j, 0, wait)

            @pl.when(j + 2 < upw)
            def _():
                load(j + 2, 0, start)
            load(j + 1, 1, wait)
            scatter(j + 1, 1, start)

        scatter(upw - 1, 1, wait)

    return k(*srcs, *extra, idx0, idx1), idx1


def _sc_gather(table, idx):
    n = idx.shape[0]
    d = table.shape[1]
    assert n % SC_UNIT == 0
    n_units = n // SC_UNIT
    nc, nw, upw = _sc_partition(n_units)
    idx = _units_by_worker(idx, n_units, upw, nw)
    mesh = plsc.VectorSubcoreMesh(core_axis_name="c", subcore_axis_name="s")
    dma = pltpu.SemaphoreType.DMA

    @functools.partial(
        pl.kernel, mesh=mesh,
        cost_estimate=pl.CostEstimate(flops=0, transcendentals=0,
                                      bytes_accessed=2 * n * d * table.dtype.itemsize + _nbytes(idx)),
        out_type=jax.ShapeDtypeStruct((n, d), table.dtype),
        scratch_types=[
            pltpu.VMEM((upw, SC_UNIT), I32),
            pltpu.VMEM((SC_UNIT, d), table.dtype),
            pltpu.VMEM((SC_UNIT, d), table.dtype),
            dma, dma, dma, dma,
        ],
    )
    def k(t_hbm, i_hbm, out_hbm, i_v, rows0, rows1, g0, g1, w0, w1):
        wid = lax.axis_index("s") * nc + lax.axis_index("c")
        pltpu.sync_copy(i_hbm.at[wid], i_v)
        rows, gsem, wsem = (rows0, rows1), (g0, g1), (w0, w1)

        def live(j):
            return j * nw + wid < n_units

        def gather(j, b, op):
            @pl.when(live(j))
            def _():
                op(pltpu.make_async_copy(t_hbm.at[i_v.at[j]], rows[b], gsem[b]))

        def write(j, b, op):
            @pl.when(live(j))
            def _():
                op(pltpu.make_async_copy(
                    rows[b], out_hbm.at[pl.ds(pl.multiple_of((j * nw + wid) * SC_UNIT, 8), SC_UNIT)], wsem[b]))

        start = lambda c: c.start()
        wait = lambda c: c.wait()
        gather(0, 0, start)

        @pl.loop(0, upw, step=2)
        def _(j):
            @pl.when(j > 0)
            def _():
                write(j - 1, 1, wait)
            gather(j + 1, 1, start)
            gather(j, 0, wait)
            write(j, 0, start)
            write(j, 0, wait)

            @pl.when(j + 2 < upw)
            def _():
                gather(j + 2, 0, start)
            gather(j + 1, 1, wait)
            write(j + 1, 1, start)

        write(upw - 1, 1, wait)

    return k(table, idx)


def _moe_kernel(start0_ref, count0_ref, gtot0_ref, start1_ref, count1_ref, gtot1_ref,
                xs0_hbm, xs1_hbm, wg_ref, wu_ref, wd_ref, ys0_hbm, ys1_hbm,
                wgu_s, wd_s, xbuf0, ybuf0, xbuf1, ybuf1, sem_in0, sem_out0, sem_in1, sem_out1):
    e = pl.program_id(0)
    last = pl.num_programs(0) - 1
    hidden = wd_s.shape[0]
    half = xbuf0.shape[-1]
    segments = (
        (xs0_hbm, ys0_hbm, xbuf0, ybuf0, sem_in0, sem_out0, start0_ref[e], count0_ref[e], gtot0_ref[0]),
        (xs1_hbm, ys1_hbm, xbuf1, ybuf1, sem_in1, sem_out1, start1_ref[e], count1_ref[e], gtot1_ref[0]),
    )

    def rows_of(g):
        return pl.ds(pl.multiple_of(g * MOE_TILE, MOE_TILE), MOE_TILE)

    def pipeline(xs_hbm, ys_hbm, xbuf, ybuf, sem_in, sem_out):
        def copy_in(g):
            slot = g % MOE_BUFFERS
            return pltpu.make_async_copy(xs_hbm.at[rows_of(g)], xbuf.at[slot], sem_in.at[slot])

        def copy_out(g):
            slot = g % MOE_BUFFERS
            return pltpu.make_async_copy(ybuf.at[slot], ys_hbm.at[rows_of(g)], sem_out.at[slot])
        return copy_in, copy_out

    @pl.when(e == 0)
    def _():
        for xs_hbm, ys_hbm, xbuf, ybuf, sem_in, sem_out, _, _, g_total in segments:
            copy_in, _ = pipeline(xs_hbm, ys_hbm, xbuf, ybuf, sem_in, sem_out)
            for g in range(MOE_LOOKAHEAD):
                @pl.when(g < g_total)
                def _():
                    copy_in(g).start()

    @pl.when(segments[0][7] + segments[1][7] > 0)
    def _():
        wgu_s[:, 0:hidden] = wg_ref[0].astype(BF16)
        wgu_s[:, hidden:2 * hidden] = wu_ref[0].astype(BF16)
        wd_s[...] = wd_ref[0].astype(BF16)

    def expert_rows(xbuf, ybuf, slot, valid):
        row = lax.broadcasted_iota(I32, (MOE_TILE, half), 0)
        x_lo, x_hi = _unpack_bf16_pair(jnp.where(row < valid, xbuf[slot], jnp.uint32(0)))
        ab = (jnp.dot(x_lo.astype(BF16), wgu_s[0:half, :], preferred_element_type=F32)
              + jnp.dot(x_hi.astype(BF16), wgu_s[half:2 * half, :], preferred_element_type=F32))
        a = ab[:, 0:hidden]
        he = a * _sigmoid(a) * ab[:, hidden:2 * hidden]
        y = jnp.dot(he.astype(BF16), wd_s[...], preferred_element_type=F32)
        ybuf[slot] = _pack_bf16_pair(y[:, 0:half], y[:, half:2 * half])

    for xs_hbm, ys_hbm, xbuf, ybuf, sem_in, sem_out, start, count, g_total in segments:
        copy_in, copy_out = pipeline(xs_hbm, ys_hbm, xbuf, ybuf, sem_in, sem_out)
        g_first = start // MOE_TILE
        n_tiles = (count + MOE_TILE - 1) // MOE_TILE

        def tiles(t, width, copy_in=copy_in, copy_out=copy_out, xbuf=xbuf, ybuf=ybuf,
                  g_first=g_first, count=count, g_total=g_total):
            gs = [g_first + t + i for i in range(width)]
            for g in gs:
                @pl.when(g + MOE_LOOKAHEAD < g_total)
                def _():
                    copy_in(g + MOE_LOOKAHEAD).start()
            for g in gs:
                copy_in(g).wait()

                @pl.when(g >= MOE_BUFFERS)
                def _():
                    copy_out(g - MOE_BUFFERS).wait()
            for i, g in enumerate(gs):
                expert_rows(xbuf, ybuf, g % MOE_BUFFERS, count - (t + i) * MOE_TILE)
            for g in gs:
                copy_out(g).start()

        def pair(p, carry, tiles=tiles):
            tiles(MOE_UNROLL * p, MOE_UNROLL)
            return carry

        lax.fori_loop(0, n_tiles // MOE_UNROLL, pair, 0)

        def single(r, carry, tiles=tiles, n_tiles=n_tiles):
            tiles(n_tiles // MOE_UNROLL * MOE_UNROLL + r, 1)
            return carry

        lax.fori_loop(0, n_tiles % MOE_UNROLL, single, 0)

        @pl.when(e == last)
        def _(copy_out=copy_out, g_total=g_total):
            for j in range(1, MOE_BUFFERS + 1):
                @pl.when(g_total >= j)
                def _():
                    copy_out(g_total - j).wait()


def _moe_call(groups, w_g, w_u, w_d):
    (xs0, starts0, cnt0), (xs1, starts1, cnt1) = groups
    half = xs0.shape[1]
    n_experts, d_model, hidden = w_g.shape

    def tiles_total(starts, cnt):
        return ((starts[-1:] + cnt[-1:] + MOE_TILE - 1) // MOE_TILE).astype(I32)

    wspec = lambda shape: pl.BlockSpec(shape, lambda e, *_: (e, 0, 0))
    tile_bufs = [pltpu.VMEM((MOE_BUFFERS, MOE_TILE, half), U32)] * 4
    grid_spec = pltpu.PrefetchScalarGridSpec(
        num_scalar_prefetch=6,
        grid=(n_experts,),
        in_specs=[
            pl.BlockSpec(memory_space=pl.ANY),
            pl.BlockSpec(memory_space=pl.ANY),
            wspec((1, d_model, hidden)),
            wspec((1, d_model, hidden)),
            wspec((1, hidden, d_model)),
        ],
        out_specs=[pl.BlockSpec(memory_space=pl.ANY), pl.BlockSpec(memory_space=pl.ANY)],
        scratch_shapes=[
            pltpu.VMEM((d_model, 2 * hidden), BF16),
            pltpu.VMEM((hidden, d_model), BF16),
            *tile_bufs,
            *[pltpu.SemaphoreType.DMA((MOE_BUFFERS,))] * 4,
        ],
    )
    n_rows = xs0.shape[0] + xs1.shape[0]
    cost = pl.CostEstimate(flops=n_rows * 6 * d_model * hidden, transcendentals=n_rows * hidden,
                           bytes_accessed=2 * _nbytes(xs0, xs1) + _nbytes(w_g, w_u, w_d))
    return pl.pallas_call(
        _moe_kernel, grid_spec=grid_spec, cost_estimate=cost,
        out_shape=[jax.ShapeDtypeStruct(xs0.shape, U32), jax.ShapeDtypeStruct(xs1.shape, U32)], name="moe_experts",
        compiler_params=pltpu.CompilerParams(
            dimension_semantics=("arbitrary",), vmem_limit_bytes=VMEM_LIMIT),
    )(starts0, cnt0, tiles_total(starts0, cnt0), starts1, cnt1, tiles_total(starts1, cnt1),
      xs0, xs1, w_g, w_u, w_d)


def _combine_kernel(x1_ref, y0_ref, y1_ref, rw_ref, gf_ref, *rest):
    out_ref = rest[-1]
    tr = x1_ref.shape[0]
    w_rows = jnp.concatenate([rw_ref[0], jnp.zeros((LANES - rw_ref.shape[1], tr), F32)], axis=0)
    w_cols = w_rows.T
    w0, w1 = w_cols[:, 0:1], w_cols[:, 1:2]
    a_lo, a_hi = _unpack_bf16_pair(y0_ref[0])
    b_lo, b_hi = _unpack_bf16_pair(y1_ref[0])
    moe = jnp.concatenate([w0 * a_lo + w1 * b_lo, w0 * a_hi + w1 * b_hi], axis=-1)
    out_ref[...] = _rms(x1_ref[...] + moe, gf_ref[...])


def _combine_call(x1, rw, row0, n, yg, gf, out_rows, out_row0, prev_out=None):
    t, d_model = x1.shape
    tr = rw.shape[-1]
    half = yg.shape[-1]
    assert t % tr == 0 and row0 % tr == 0 and n % tr == 0 and rw.shape == (t // tr, EXPERTS_PER_GROUP, tr)
    assert yg.shape == (2, n, half) and out_row0 % tr == 0
    off = row0 // tr
    ooff = out_row0 // tr
    in_specs = [
        pl.BlockSpec((tr, d_model), lambda i: (off + i, 0)),
        pl.BlockSpec((1, tr, half), lambda i: (0, i, 0)),
        pl.BlockSpec((1, tr, half), lambda i: (1, i, 0)),
        pl.BlockSpec((1, EXPERTS_PER_GROUP, tr), lambda i: (off + i, 0, 0)),
        pl.BlockSpec((1, d_model), lambda i: (0, 0)),
    ]
    args = [x1, yg, yg, rw, gf]
    aliases = {}
    if prev_out is not None:
        in_specs.append(pl.BlockSpec(memory_space=pl.ANY))
        args.append(prev_out)
        aliases = {len(args) - 1: 0}
    return pl.pallas_call(
        _combine_kernel,
        grid=(n // tr,),
        in_specs=in_specs,
        out_specs=pl.BlockSpec((tr, d_model), lambda i: (ooff + i, 0)),
        out_shape=jax.ShapeDtypeStruct((out_rows, d_model), F32), name=f"combine_row{out_row0}_of{out_rows}",
        cost_estimate=pl.CostEstimate(flops=8 * n * d_model, transcendentals=n,
                                      bytes_accessed=2 * n * d_model * 4 + _nbytes(yg) + n * 4 * EXPERTS_PER_GROUP),
        input_output_aliases=aliases,
        compiler_params=pltpu.CompilerParams(
            dimension_semantics=("arbitrary",), vmem_limit_bytes=VMEM_LIMIT),
    )(*args)


def _route(streams, after=None):
    tokens = [h2.shape[0] for h2, _, _ in streams]
    counts = [cnt[:, 0].astype(I32) for _, _, cnt in streams]
    total = sum(counts)
    padded = ((total + MOE_TILE - 1) // MOE_TILE) * MOE_TILE
    starts = (jnp.cumsum(padded) - padded).astype(I32)
    experts = jnp.arange(N_EXPERTS, dtype=I32)[None, :, None]
    pos, base = [], starts
    for (_, ri, _), t, cnt in zip(streams, tokens, counts):
        ri = jnp.moveaxis(ri, 2, 0).reshape(ri.shape[2], t)
        first_row = jnp.sum(jnp.where(ri[0:2, None, :] == experts, base[None, :, None], 0), axis=1)
        pos.append(ri[2:4] + first_row)
        base = base + cnt
    pos = jnp.concatenate(pos, axis=1)
    n_rows = ((2 * sum(tokens) + N_EXPERTS * (MOE_TILE - 1)) // MOE_TILE) * MOE_TILE
    xs_sorted, ready = _sc_dispatch([h2 for h2, _, _ in streams], pos[0], pos[1], n_rows, after)
    return (xs_sorted, starts, total), pos, ready


def _gather_tokens(ys_sorted, pos, t0, n):
    return _sc_gather(ys_sorted, pos[:, t0:t0 + n].reshape(2 * n)).reshape(2, n, ys_sorted.shape[-1])


def _one_layer(xp, xs, s_ret, c_pool, norm1_g, w_in, ret_norm_g, w_pool, pool_scale, w_out, norm2_g,
               w_rg, w_re, w_g, w_u, w_d, final_g, past_len):
    bp, seq, d_model = xp.shape
    bs, dseq, _ = xs.shape
    rw_width = ret_norm_g.shape[-1]
    pw = pool_scale.shape[-1]
    dh = rw_width // RET_HEADS
    half = d_model // 2

    w_r = jnp.concatenate(
        [w_re.T, w_rg.T, jnp.zeros((ROUTER_ROWS - N_EXPERTS - N_EXPERT_GROUPS, d_model), F32)], axis=0)
    wr_hi = w_r.astype(BF16)
    wr = jnp.concatenate([wr_hi, (w_r - wr_hi.astype(F32)).astype(BF16)], axis=0)
    consts = dict(
        g1=norm1_g.reshape(1, d_model), w_in=w_in.astype(BF16), gret=ret_norm_g.reshape(1, rw_width),
        w_pool=w_pool.astype(BF16), pscale=pool_scale.reshape(1, pw), w_out=w_out.astype(BF16),
        g2=norm2_g.reshape(1, d_model), wr=wr)

    gf = final_g.reshape(1, d_model)

    ts = bs * dseq
    b_lead = bp - 1
    t_lead, t_rest = b_lead * seq, (bp - b_lead) * seq
    zeros = lambda nb: (jnp.zeros((nb, RET_HEADS, dh, dh), F32), jnp.zeros((nb, HIST_ROWS, pw), F32))
    h0s = jnp.pad(c_pool, ((0, 0), (HIST_ROWS - POOL_HIST, 0), (0, 0)))
    prompt_tile = dict(bb=1, tl=PROMPT_TILE, chunk=256)

    def stream(layer_out, t):
        x1, h2, ri, rw, st, hist, cnt = layer_out
        return dict(x1=x1.reshape(t, d_model), route=(h2.reshape(t, half), ri, cnt),
                    rw=rw.reshape(-1, EXPERTS_PER_GROUP, rw.shape[-1]), st=st, hist=hist)

    pa = stream(_layer_call(xp, 0, b_lead, *zeros(b_lead), 0, consts, gf, **prompt_tile), t_lead)
    group0, pos0, ready0 = _route([pa["route"]])
    pb = stream(_layer_call(xp, b_lead, bp - b_lead, *zeros(bp - b_lead), 0, consts, ready0, **prompt_tile),
                t_rest)
    sm = stream(_layer_call(xs, 0, bs, s_ret, h0s, past_len, consts, pb["route"][2],
                            bb=bs, tl=dseq, chunk=min(64, dseq)), ts)
    group1, pos1, _ = _route([pb["route"], sm["route"]], after=group0[0])
    ys0, ys1 = _moe_call([group0, group1], w_g, w_u, w_d)

    tp = bp * seq
    yp = _combine_call(pb["x1"], pb["rw"], 0, t_rest, _gather_tokens(ys1, pos1, 0, t_rest), gf, tp, t_lead)
    ysm = _combine_call(sm["x1"], sm["rw"], 0, ts, _gather_tokens(ys1, pos1, t_rest, ts), gf, ts, 0)
    row0 = 0
    for nb in COMBINE_CHUNKS:
        n = min(nb * seq, t_lead - row0)
        if n > 0:
            yp = _combine_call(pa["x1"], pa["rw"], row0, n, _gather_tokens(ys0, pos0, row0, n), gf, tp, row0,
                               prev_out=yp)
            row0 += n
    assert row0 == t_lead
    st_p = jnp.concatenate([pa["st"], pb["st"]], axis=0)
    hist_p = jnp.concatenate([pa["hist"], pb["hist"]], axis=0)
    return (yp.reshape(bp, seq, d_model), ysm.reshape(bs, dseq, d_model),
            st_p, hist_p[:, HIST_ROWS - POOL_HIST:], sm["st"], sm["hist"][:, HIST_ROWS - POOL_HIST:])


def kernel(x_prompt, x_sample, state_ret, cache_pool, norm1_g, w_in, ret_norm_g, w_pool, pool_scale, w_out,
           norm2_g, w_router_group, w_router_expert, w_exp_gate, w_exp_up, w_exp_down, final_norm_g):
    depth = w_in.shape[0]
    assert depth == 1, "the final RMSNorm is fused into the layer's combine kernel"
    assert x_prompt.shape[0] >= 2 and x_prompt.shape[1] % PROMPT_TILE == 0
    yp, ys, s_p, h_p, s_s, h_s = _one_layer(
        x_prompt, x_sample, state_ret[0], cache_pool[0], norm1_g[0], w_in[0], ret_norm_g[0], w_pool[0],
        pool_scale[0], w_out[0], norm2_g[0], w_router_group[0], w_router_expert[0],
        w_exp_gate[0], w_exp_up[0], w_exp_down[0], final_norm_g, PAST_LEN)
    return (yp, ys, s_p[None], h_p[None], s_s[None], h_s[None])
```

```python
import functools

import jax
import jax.numpy as jnp
from jax import lax
from jax.experimental import pallas as pl
from jax.experimental.pallas import tpu as pltpu
from jax.experimental.pallas import tpu_sc as plsc

F32 = jnp.float32
BF16 = jnp.bfloat16
I32 = jnp.int32
U32 = jnp.uint32

EPS = 1e-6
ROPE_BASE = 10000.0
RET_HEADS = 4
POOL_WINDOWS = (2, 4, 8, 16)
POOL_HIST = max(POOL_WINDOWS) - 1
N_EXPERT_GROUPS = 4
EXPERTS_PER_GROUP = 8
N_EXPERTS = N_EXPERT_GROUPS * EXPERTS_PER_GROUP
ROUTER_ROWS = 48
PAST_LEN = 1024

LANES = 128
SUBLANES = 8
HIST_ROWS = 16
MOE_TILE = 256
MOE_BUFFERS = 10
MOE_UNROLL = 2
MOE_LOOKAHEAD = MOE_BUFFERS - MOE_UNROLL
PREP_ROWS = 512
PROMPT_TILE = 1024
COMBINE_CHUNKS = (1, 2)
SC_UNIT = 32
VMEM_LIMIT = 56 * 1024 * 1024


def _nbytes(*arrays):
    return sum(a.size * a.dtype.itemsize for a in arrays)


def _rms(x, g):
    return x * lax.rsqrt(jnp.mean(x * x, axis=-1, keepdims=True) + EPS) * g


def _sigmoid(x):
    return 1.0 / (1.0 + jnp.exp(-x))


def _pack_bf16_pair(lo, hi):
    lo_b = lax.bitcast_convert_type(lo.astype(BF16).astype(F32), U32)
    hi_b = lax.bitcast_convert_type(hi.astype(BF16).astype(F32), U32)
    return hi_b | (lo_b >> 16)


def _unpack_bf16_pair(p):
    lo = lax.bitcast_convert_type(p << 16, F32)
    hi = lax.bitcast_convert_type(p & jnp.uint32(0xFFFF0000), F32)
    return lo, hi


def _layer_kernel(dc_ref, x_ref, s0_ref, h0_ref, rb_ref, rc_ref, rs_ref, rcs_ref, rss_ref,
                  dintra_ref, dq_ref, dk_ref,
                  g1_ref, win_ref, gret_ref, wpool_ref, pscale_ref, wout_ref, g2_ref,
                  wr_ref, tri_ref, after_ref,
                  x1_ref, h2_ref, ri_ref, rw_ref, st_ref, hist_ref, cnt_ref,
                  ue_ref, q_ref, k_ref, v_ref, gate_ref, a_ref,
                  *, bb, tl, chunk, pos0):
    b_idx = pl.program_id(0)
    l_idx = pl.program_id(1)
    rows = bb * tl
    d_model = x_ref.shape[-1]
    rw_width = q_ref.shape[-1]
    dh = rw_width // RET_HEADS
    pw = ue_ref.shape[-1]
    gw = pw // len(POOL_WINDOWS)
    n_chunks = tl // chunk

    @pl.when(l_idx == 0)
    def _():
        st_ref[...] = s0_ref[...]
        ue_ref[:, 0:HIST_ROWS, :] = h0_ref[...]

    @pl.when((l_idx == 0) & (b_idx == 0))
    def _():
        cnt_ref[...] = jnp.zeros_like(cnt_ref)

    cos_b = rb_ref[0, 0:1, :]
    sin_b = rb_ref[0, 1:2, :]
    cosf = cos_b * rc_ref[...] - sin_b * rs_ref[...]
    sinf = sin_b * rcs_ref[...] + cos_b * rss_ref[...]
    k_scale = dh ** -0.5
    n_blocks = max(1, tl // PREP_ROWS) if bb == 1 else 1
    block = rows // n_blocks
    for blk in range(n_blocks):
        rs = slice(blk * block, (blk + 1) * block)
        xb = x_ref[0, rs, :] if bb == 1 else x_ref[...].reshape(rows, d_model)
        hb = _rms(xb, g1_ref[...]).astype(BF16)
        proj = jnp.dot(hb, win_ref[...], preferred_element_type=F32)

        def rotate(a):
            if bb == 1:
                return a * cosf[rs] + pltpu.roll(a, dh // 2, 1) * sinf[rs]
            return (a.reshape(bb, tl, dh) * cosf[None]
                    + pltpu.roll(a, dh // 2, 1).reshape(bb, tl, dh) * sinf[None]).reshape(rows, dh)

        for hh in range(RET_HEADS):
            cs = slice(hh * dh, (hh + 1) * dh)
            q_ref[rs, cs] = rotate(proj[:, hh * dh:(hh + 1) * dh]).astype(BF16)
            k_ref[rs, cs] = rotate(proj[:, rw_width + hh * dh:rw_width + (hh + 1) * dh]) * k_scale
        v_ref[rs, :] = proj[:, 2 * rw_width:3 * rw_width].astype(BF16)
        gate_ref[rs, :] = proj[:, 3 * rw_width:4 * rw_width]
        u = proj[:, 4 * rw_width:4 * rw_width + pw]
        if bb == 1:
            ue_ref[0, HIST_ROWS + blk * block:HIST_ROWS + (blk + 1) * block, :] = u
        else:
            ue_ref[:, HIST_ROWS:HIST_ROWS + tl, :] = u.reshape(bb, tl, pw)

    def ret_block(b, c):
        r0 = b * tl + c * chunk
        if not isinstance(r0, int):
            r0 = pl.multiple_of(r0, chunk)
        rr = pl.ds(r0, chunk)
        for hh in range(RET_HEADS):
            cs = slice(hh * dh, (hh + 1) * dh)
            qc = q_ref[rr, cs]
            kf = k_ref[rr, cs]
            vc = v_ref[rr, cs]
            s_old = st_ref[b, hh]
            sc = lax.dot_general(qc, kf.astype(BF16), (((1,), (1,)), ((), ())),
                                 preferred_element_type=F32) * dintra_ref[hh]
            o = (jnp.dot(sc.astype(BF16), vc, preferred_element_type=F32)
                 + dq_ref[hh] * jnp.dot(qc, s_old.astype(BF16), preferred_element_type=F32))
            kd = (kf * dk_ref[hh]).astype(BF16)
            s_new = dc_ref[hh] * s_old + lax.dot_general(
                kd, vc, (((0,), (0,)), ((), ())), preferred_element_type=F32)
            st_ref[b, hh] = s_new
            oc = o - jnp.mean(o, axis=-1, keepdims=True)
            var = jnp.mean(oc * oc, axis=-1, keepdims=True)
            y = oc * lax.rsqrt(var + EPS) * gret_ref[:, cs]
            g = gate_ref[rr, cs]
            a_ref[rr, cs] = (g * _sigmoid(g) * y).astype(BF16)

    if bb * n_chunks <= 4:
        for b in range(bb):
            for c in range(n_chunks):
                ret_block(b, c)
    else:
        def body(i, carry):
            ret_block(i // n_chunks, i % n_chunks)
            return carry
        lax.fori_loop(0, bb * n_chunks, body, 0)

    nt = (((1,), (1,)), ((), ()))
    neg = jnp.float32(-jnp.inf)
    big = jnp.float32(1e9)
    sub = lax.broadcasted_iota(I32, (EXPERTS_PER_GROUP, block), 0).astype(F32)
    eid = lax.broadcasted_iota(I32, (N_EXPERTS, block), 0).astype(F32)
    for blk in range(n_blocks):
        lo = blk * block
        rs = slice(lo, lo + block)

        if bb == 1:
            pos = pos0 + l_idx * tl + lo + lax.broadcasted_iota(I32, (block, 1), 0)
            window = lambda off, cs: ue_ref[0, HIST_ROWS + lo - off:HIST_ROWS + lo - off + block, cs]
        else:
            pos = pos0 + l_idx * tl + lax.broadcasted_iota(I32, (1, tl, 1), 1)
            window = lambda off, cs: ue_ref[:, HIST_ROWS - off:HIST_ROWS - off + tl, cs]
        for gi, w in enumerate(POOL_WINDOWS):
            cs = slice(gi * gw, (gi + 1) * gw)
            u_g = window(0, cs)
            acc = u_g
            for j in range(1, w):
                acc = acc + window(j, cs)
            inv_cnt = 1.0 / jnp.minimum(pos + 1, w).astype(F32)
            p = (acc * inv_cnt - u_g).reshape(block, gw)
            z = jnp.dot(p.astype(BF16), wpool_ref[gi], preferred_element_type=F32) * pscale_ref[:, cs]
            a_ref[rs, rw_width + gi * gw:rw_width + (gi + 1) * gw] = z.astype(BF16)

        xb = x_ref[0, rs, :] if bb == 1 else x_ref[...].reshape(rows, d_model)
        x1 = xb + jnp.dot(a_ref[rs, :], wout_ref[...], preferred_element_type=F32)
        h2 = _rms(x1, g2_ref[...])
        h2_packed = _pack_bf16_pair(h2[:, 0:d_model // 2], h2[:, d_model // 2:])
        if bb == 1:
            x1_ref[0, rs, :] = x1
            h2_ref[0, rs, :] = h2_packed
        else:
            x1_ref[...] = x1.reshape(bb, tl, d_model)
            h2_ref[...] = h2_packed.reshape(bb, tl, d_model // 2)

        h2_hi = h2.astype(BF16)
        h2_lo = (h2 - h2_hi.astype(F32)).astype(BF16)
        two = lax.dot_general(wr_ref[...], h2_hi, nt, preferred_element_type=F32)
        lt = (two[0:ROUTER_ROWS] + two[ROUTER_ROWS:2 * ROUTER_ROWS]
              + lax.dot_general(wr_ref[0:ROUTER_ROWS, :], h2_lo, nt, preferred_element_type=F32))
        gl = jnp.where(sub < N_EXPERT_GROUPS, lt[N_EXPERTS:N_EXPERTS + EXPERTS_PER_GROUP], neg)
        gmax = jnp.max(gl, axis=0, keepdims=True)
        gidx = jnp.min(jnp.where(gl == gmax, sub, big), axis=0, keepdims=True)
        p_sel = 1.0 / jnp.sum(jnp.exp(gl - gmax), axis=0, keepdims=True)
        el = lt[0:EXPERTS_PER_GROUP]
        for g in range(1, N_EXPERT_GROUPS):
            el = jnp.where(gidx == g, lt[g * EXPERTS_PER_GROUP:(g + 1) * EXPERTS_PER_GROUP], el)
        m1 = jnp.max(el, axis=0, keepdims=True)
        t1 = jnp.min(jnp.where(el == m1, sub, big), axis=0, keepdims=True)
        el2 = jnp.where(sub == t1, neg, el)
        m2 = jnp.max(el2, axis=0, keepdims=True)
        t2 = jnp.min(jnp.where(el2 == m2, sub, big), axis=0, keepdims=True)
        e2 = jnp.exp(m2 - m1)
        w1 = p_sel / (1.0 + e2)
        w2 = p_sel * e2 / (1.0 + e2)
        i1 = gidx * EXPERTS_PER_GROUP + t1
        i2 = gidx * EXPERTS_PER_GROUP + t2

        hit1 = eid == i1
        hit2 = eid == i2
        onehot = (hit1 | hit2).astype(BF16)
        before = jnp.dot(onehot, tri_ref[...], preferred_element_type=F32) + cnt_ref[...]
        r1 = jnp.sum(jnp.where(hit1, before, 0.0), axis=0, keepdims=True)
        r2 = jnp.sum(jnp.where(hit2, before, 0.0), axis=0, keepdims=True)
        cnt_ref[...] = cnt_ref[...] + jnp.sum(onehot.astype(F32), axis=1, keepdims=True)

        ri = jnp.where(sub == 0, i1, jnp.where(sub == 1, i2, jnp.where(sub == 2, r1, jnp.where(sub == 3, r2, 0.0))))
        ri_ref[0, 0, :, rs] = ri.astype(I32)
        rw_ref[0, 0, :, rs] = jnp.where(sub == 0, w1, jnp.where(sub == 1, w2, 0.0))

    tail = ue_ref[:, tl:tl + HIST_ROWS, :]
    ue_ref[:, 0:HIST_ROWS, :] = tail
    hist_ref[...] = tail


def _rope_tables(pos0, seq, tl, dh):
    half = dh // 2
    inv = ROPE_BASE ** (-jnp.arange(half, dtype=F32) / half)
    ang_t = jnp.arange(tl, dtype=F32)[:, None] * inv[None, :]
    ang_b = (pos0 + tl * jnp.arange(seq // tl)).astype(F32)[:, None] * inv[None, :]
    dup = lambda a: jnp.concatenate([a, a], axis=-1)
    sgn = lambda a: jnp.concatenate([-a, a], axis=-1)
    base = jnp.stack([dup(jnp.cos(ang_b)), dup(jnp.sin(ang_b))], axis=1)
    base = jnp.pad(base, ((0, 0), (0, SUBLANES - base.shape[1]), (0, 0)))
    cos_t, sin_t = jnp.cos(ang_t), jnp.sin(ang_t)
    return base, dup(cos_t), dup(sin_t), sgn(cos_t), sgn(sin_t)


def _block_rows(bb, tl):
    return bb * tl // (max(1, tl // PREP_ROWS) if bb == 1 else 1)


def _layer_tables(pos0, seq, dh, *, bb, tl, chunk):
    lg = jnp.log1p(-jnp.exp2(-5.0 - jnp.arange(RET_HEADS, dtype=F32)))
    idx = jnp.arange(chunk, dtype=F32)
    diff = idx[:, None] - idx[None, :]
    d_intra = jnp.where(diff[None] >= 0, jnp.exp(jnp.maximum(diff, 0.0)[None] * lg[:, None, None]), 0.0)
    d_q = jnp.broadcast_to(jnp.exp((idx + 1.0)[None, :] * lg[:, None])[:, :, None], (RET_HEADS, chunk, dh))
    d_k = jnp.broadcast_to(jnp.exp((chunk - 1.0 - idx)[None, :] * lg[:, None])[:, :, None], (RET_HEADS, chunk, dh))
    d_c = jnp.exp(chunk * lg)
    block = _block_rows(bb, tl)
    tri = jnp.triu(jnp.ones((block, block), BF16), 1)
    return dict(rope=_rope_tables(pos0, seq, tl, dh), d_intra=d_intra, d_q=d_q, d_k=d_k, d_c=d_c, tri=tri)


def _layer_call(x, b0, nb, s0, h0, pos0, consts, tables, after, *, bb, tl, chunk):
    _, seq, d_model = x.shape
    bsz = nb
    blk0 = b0 // bb
    rows = bb * tl
    rw_width = consts["gret"].shape[-1]
    pw = consts["pscale"].shape[-1]
    dh = rw_width // RET_HEADS
    block = _block_rows(bb, tl)
    rope, d_intra, d_q, d_k, d_c, tri = (tables[k] for k in ("rope", "d_intra", "d_q", "d_k", "d_c", "tri"))

    const2 = lambda b, l, *_: (0, 0)
    const3 = lambda b, l, *_: (0, 0, 0)
    grid_spec = pltpu.PrefetchScalarGridSpec(
        num_scalar_prefetch=0,
        grid=(bsz // bb, seq // tl),
        in_specs=[
            pl.BlockSpec(memory_space=pltpu.SMEM),
            pl.BlockSpec((bb, tl, d_model), lambda b, l: (blk0 + b, l, 0)),
            pl.BlockSpec((bb, RET_HEADS, dh, dh), lambda b, l: (b, 0, 0, 0)),
            pl.BlockSpec((bb, HIST_ROWS, pw), lambda b, l: (b, 0, 0)),
            pl.BlockSpec((1, SUBLANES, dh), lambda b, l: (l, 0, 0)),
            pl.BlockSpec((tl, dh), const2),
            pl.BlockSpec((tl, dh), const2),
            pl.BlockSpec((tl, dh), const2),
            pl.BlockSpec((tl, dh), const2),
            pl.BlockSpec((RET_HEADS, chunk, chunk), const3),
            pl.BlockSpec((RET_HEADS, chunk, dh), const3),
            pl.BlockSpec((RET_HEADS, chunk, dh), const3),
            pl.BlockSpec((1, d_model), const2),
            pl.BlockSpec(consts["w_in"].shape, const2),
            pl.BlockSpec((1, rw_width), const2),
            pl.BlockSpec(consts["w_pool"].shape, const3),
            pl.BlockSpec((1, pw), const2),
            pl.BlockSpec(consts["w_out"].shape, const2),
            pl.BlockSpec((1, d_model), const2),
            pl.BlockSpec((2 * ROUTER_ROWS, d_model), const2),
            pl.BlockSpec((block, block), const2),
            pl.BlockSpec(memory_space=pl.ANY),
        ],
        out_specs=[
            pl.BlockSpec((bb, tl, d_model), lambda b, l: (b, l, 0)),
            pl.BlockSpec((bb, tl, d_model // 2), lambda b, l: (b, l, 0)),
            pl.BlockSpec((1, 1, EXPERTS_PER_GROUP, rows), lambda b, l: (b, l, 0, 0)),
            pl.BlockSpec((1, 1, EXPERTS_PER_GROUP, rows), lambda b, l: (b, l, 0, 0)),
            pl.BlockSpec((bb, RET_HEADS, dh, dh), lambda b, l: (b, 0, 0, 0)),
            pl.BlockSpec((bb, HIST_ROWS, pw), lambda b, l: (b, 0, 0)),
            pl.BlockSpec((N_EXPERTS, block), const2),
        ],
        scratch_shapes=[
            pltpu.VMEM((bb, HIST_ROWS + tl, pw), F32),
            pltpu.VMEM((rows, rw_width), BF16),
            pltpu.VMEM((rows, rw_width), F32),
            pltpu.VMEM((rows, rw_width), BF16),
            pltpu.VMEM((rows, rw_width), F32),
            pltpu.VMEM((rows, d_model), BF16),
        ],
    )
    out_shape = [
        jax.ShapeDtypeStruct((bsz, seq, d_model), F32),
        jax.ShapeDtypeStruct((bsz, seq, d_model // 2), U32),
        jax.ShapeDtypeStruct((bsz // bb, seq // tl, EXPERTS_PER_GROUP, rows), I32),
        jax.ShapeDtypeStruct((bsz // bb, seq // tl, EXPERTS_PER_GROUP, rows), F32),
        jax.ShapeDtypeStruct((bsz, RET_HEADS, dh, dh), F32),
        jax.ShapeDtypeStruct((bsz, HIST_ROWS, pw), F32),
        jax.ShapeDtypeStruct((N_EXPERTS, block), F32),
    ]
    kern = functools.partial(_layer_kernel, bb=bb, tl=tl, chunk=chunk, pos0=pos0)
    operands = (d_c, x, s0, h0, *rope, d_intra, d_q, d_k,
                consts["g1"], consts["w_in"], consts["gret"], consts["w_pool"], consts["pscale"],
                consts["w_out"], consts["g2"], consts["wr"], tri, after)
    n_tok = bsz * seq
    mm_flops_per_token = 2 * (d_model * consts["w_in"].shape[1] + d_model * d_model + 3 * d_model * ROUTER_ROWS
                              + pw * pw // 4
                              + rw_width * (2 * chunk + 2 * dh) + N_EXPERTS * block)
    cost = pl.CostEstimate(
        flops=n_tok * mm_flops_per_token, transcendentals=n_tok * (rw_width + 2 * N_EXPERT_GROUPS),
        bytes_accessed=_nbytes(*operands) - _nbytes(x, after) + n_tok * d_model * 4 + _nbytes(*out_shape))
    return pl.pallas_call(
        kern, grid_spec=grid_spec, out_shape=out_shape, name=f"layer_pos{pos0}_b{b0}", cost_estimate=cost,
        compiler_params=pltpu.CompilerParams(
            dimension_semantics=("arbitrary", "arbitrary"), vmem_limit_bytes=VMEM_LIMIT),
    )(*operands)


def _sc_partition(n_units):
    info = plsc.get_sparse_core_info()
    nc, nw = info.num_cores, info.num_cores * info.num_subcores
    upw = -(-n_units // nw)
    upw += upw % 2
    return nc, nw, upw


def _units_by_worker(idx, n_units, upw, nw):
    idx = jnp.pad(idx.reshape(n_units, SC_UNIT), ((0, nw * upw - n_units), (0, 0)))
    return idx.reshape(upw, nw, SC_UNIT).transpose(1, 0, 2)


def _sc_dispatch(srcs, idx0, idx1, n_out_rows, after=None):
    assert 1 <= len(srcs) <= 2
    d = srcs[0].shape[1]
    dtype = srcs[0].dtype
    assert all(src.shape[0] % SC_UNIT == 0 for src in srcs)
    units_a = srcs[0].shape[0] // SC_UNIT
    n_units = sum(src.shape[0] for src in srcs) // SC_UNIT
    nc, nw, upw = _sc_partition(n_units)
    idx0 = _units_by_worker(idx0, n_units, upw, nw)
    idx1 = _units_by_worker(idx1, n_units, upw, nw)
    mesh = plsc.VectorSubcoreMesh(core_axis_name="c", subcore_axis_name="s")
    dma = pltpu.SemaphoreType.DMA
    extra = [] if after is None else [after]

    moved = n_units * SC_UNIT * d * jnp.dtype(dtype).itemsize
    @functools.partial(
        pl.kernel, mesh=mesh,
        cost_estimate=pl.CostEstimate(flops=0, transcendentals=0, bytes_accessed=3 * moved + _nbytes(idx0, idx1)),
        out_type=jax.ShapeDtypeStruct((n_out_rows, d), dtype),
        scratch_types=[
            pltpu.VMEM((upw, SC_UNIT), I32),
            pltpu.VMEM((upw, SC_UNIT), I32),
            pltpu.VMEM((SC_UNIT, d), dtype),
            pltpu.VMEM((SC_UNIT, d), dtype),
            dma, dma, dma, dma, dma, dma,
        ],
    )
    def k(*refs):
        src_hbm = refs[:len(srcs)]
        i0_hbm, i1_hbm, out_hbm, i0_v, i1_v, rows0, rows1, l0, l1, p0, p1, q0, q1 = refs[len(srcs) + len(extra):]
        wid = lax.axis_index("s") * nc + lax.axis_index("c")
        pltpu.sync_copy(i0_hbm.at[wid], i0_v)
        pltpu.sync_copy(i1_hbm.at[wid], i1_v)
        rows, lsem, psem, qsem = (rows0, rows1), (l0, l1), (p0, p1), (q0, q1)

        def live(j):
            return j * nw + wid < n_units

        def load(j, b, op):
            unit = j * nw + wid

            @pl.when(live(j) & (unit < units_a))
            def _():
                op(pltpu.make_async_copy(
                    src_hbm[0].at[pl.ds(pl.multiple_of(unit * SC_UNIT, 8), SC_UNIT)], rows[b], lsem[b]))

            if len(srcs) == 2:
                @pl.when(live(j) & (unit >= units_a))
                def _():
                    op(pltpu.make_async_copy(
                        src_hbm[1].at[pl.ds(pl.multiple_of((unit - units_a) * SC_UNIT, 8), SC_UNIT)],
                        rows[b], lsem[b]))

        def scatter(j, b, op):
            @pl.when(live(j))
            def _():
                op(pltpu.make_async_copy(rows[b], out_hbm.at[i0_v.at[j]], psem[b]))
                op(pltpu.make_async_copy(rows[b], out_hbm.at[i1_v.at[j]], qsem[b]))

        start = lambda c: c.start()
        wait = lambda c: c.wait()
        load(0, 0, start)

        @pl.loop(0, upw, step=2)
        def _(j):
            @pl.when(j > 0)
            def _():
                scatter(j - 1, 1, wait)
            load(j + 1, 1, start)
            load(j, 0, wait)
            scatter(j, 0, start)
            scatter(j, 0, wait)

            @pl.when(j + 2 < upw)
            def _():
                load(j + 2, 0, start)
            load(j + 1, 1, wait)
            scatter(j + 1, 1, start)

        scatter(upw - 1, 1, wait)

    return k(*srcs, *extra, idx0, idx1), idx1


def _sc_gather(table, idx):
    n = idx.shape[0]
    d = table.shape[1]
    assert n % SC_UNIT == 0
    n_units = n // SC_UNIT
    nc, nw, upw = _sc_partition(n_units)
    idx = _units_by_worker(idx, n_units, upw, nw)
    mesh = plsc.VectorSubcoreMesh(core_axis_name="c", subcore_axis_name="s")
    dma = pltpu.SemaphoreType.DMA

    @functools.partial(
        pl.kernel, mesh=mesh,
        cost_estimate=pl.CostEstimate(flops=0, transcendentals=0,
                                      bytes_accessed=2 * n * d * table.dtype.itemsize + _nbytes(idx)),
        out_type=jax.ShapeDtypeStruct((n, d), table.dtype),
        scratch_types=[
            pltpu.VMEM((upw, SC_UNIT), I32),
            pltpu.VMEM((SC_UNIT, d), table.dtype),
            pltpu.VMEM((SC_UNIT, d), table.dtype),
            dma, dma, dma, dma,
        ],
    )
    def k(t_hbm, i_hbm, out_hbm, i_v, rows0, rows1, g0, g1, w0, w1):
        wid = lax.axis_index("s") * nc + lax.axis_index("c")
        pltpu.sync_copy(i_hbm.at[wid], i_v)
        rows, gsem, wsem = (rows0, rows1), (g0, g1), (w0, w1)

        def live(j):
            return j * nw + wid < n_units

        def gather(j, b, op):
            @pl.when(live(j))
            def _():
                op(pltpu.make_async_copy(t_hbm.at[i_v.at[j]], rows[b], gsem[b]))

        def write(j, b, op):
            @pl.when(live(j))
            def _():
                op(pltpu.make_async_copy(
                    rows[b], out_hbm.at[pl.ds(pl.multiple_of((j * nw + wid) * SC_UNIT, 8), SC_UNIT)], wsem[b]))

        start = lambda c: c.start()
        wait = lambda c: c.wait()
        gather(0, 0, start)

        @pl.loop(0, upw, step=2)
        def _(j):
            @pl.when(j > 0)
            def _():
                write(j - 1, 1, wait)
            gather(j + 1, 1, start)
            gather(j, 0, wait)
            write(j, 0, start)
            write(j, 0, wait)

            @pl.when(j + 2 < upw)
            def _():
                gather(j + 2, 0, start)
            gather(j + 1, 1, wait)
            write(j + 1, 1, start)

        write(upw - 1, 1, wait)

    return k(table, idx)


def _moe_kernel(start0_ref, count0_ref, gtot0_ref, start1_ref, count1_ref, gtot1_ref,
                xs0_hbm, xs1_hbm, wg_ref, wu_ref, wd_ref, ys0_hbm, ys1_hbm,
                wgu_s, wd_s, xbuf0, ybuf0, xbuf1, ybuf1, sem_in0, sem_out0, sem_in1, sem_out1):
    e = pl.program_id(0)
    last = pl.num_programs(0) - 1
    hidden = wd_s.shape[0]
    half = xbuf0.shape[-1]
    segments = (
        (xs0_hbm, ys0_hbm, xbuf0, ybuf0, sem_in0, sem_out0, start0_ref[e], count0_ref[e], gtot0_ref[0]),
        (xs1_hbm, ys1_hbm, xbuf1, ybuf1, sem_in1, sem_out1, start1_ref[e], count1_ref[e], gtot1_ref[0]),
    )

    def rows_of(g):
        return pl.ds(pl.multiple_of(g * MOE_TILE, MOE_TILE), MOE_TILE)

    def pipeline(xs_hbm, ys_hbm, xbuf, ybuf, sem_in, sem_out):
        def copy_in(g):
            slot = g % MOE_BUFFERS
            return pltpu.make_async_copy(xs_hbm.at[rows_of(g)], xbuf.at[slot], sem_in.at[slot])

        def copy_out(g):
            slot = g % MOE_BUFFERS
            return pltpu.make_async_copy(ybuf.at[slot], ys_hbm.at[rows_of(g)], sem_out.at[slot])
        return copy_in, copy_out

    @pl.when(e == 0)
    def _():
        for xs_hbm, ys_hbm, xbuf, ybuf, sem_in, sem_out, _, _, g_total in segments:
            copy_in, _ = pipeline(xs_hbm, ys_hbm, xbuf, ybuf, sem_in, sem_out)
            for g in range(MOE_LOOKAHEAD):
                @pl.when(g < g_total)
                def _():
                    copy_in(g).start()

    @pl.when(segments[0][7] + segments[1][7] > 0)
    def _():
        wgu_s[:, 0:hidden] = wg_ref[0].astype(BF16)
        wgu_s[:, hidden:2 * hidden] = wu_ref[0].astype(BF16)
        wd_s[...] = wd_ref[0].astype(BF16)

    def expert_rows(xbuf, ybuf, slot, valid):
        row = lax.broadcasted_iota(I32, (MOE_TILE, half), 0)
        x_lo, x_hi = _unpack_bf16_pair(jnp.where(row < valid, xbuf[slot], jnp.uint32(0)))
        ab = (jnp.dot(x_lo.astype(BF16), wgu_s[0:half, :], preferred_element_type=F32)
              + jnp.dot(x_hi.astype(BF16), wgu_s[half:2 * half, :], preferred_element_type=F32))
        a = ab[:, 0:hidden]
        he = a * _sigmoid(a) * ab[:, hidden:2 * hidden]
        y = jnp.dot(he.astype(BF16), wd_s[...], preferred_element_type=F32)
        ybuf[slot] = _pack_bf16_pair(y[:, 0:half], y[:, half:2 * half])

    for xs_hbm, ys_hbm, xbuf, ybuf, sem_in, sem_out, start, count, g_total in segments:
        copy_in, copy_out = pipeline(xs_hbm, ys_hbm, xbuf, ybuf, sem_in, sem_out)
        g_first = start // MOE_TILE
        n_tiles = (count + MOE_TILE - 1) // MOE_TILE

        def tiles(t, width, copy_in=copy_in, copy_out=copy_out, xbuf=xbuf, ybuf=ybuf,
                  g_first=g_first, count=count, g_total=g_total):
            gs = [g_first + t + i for i in range(width)]
            for g in gs:
                @pl.when(g + MOE_LOOKAHEAD < g_total)
                def _():
                    copy_in(g + MOE_LOOKAHEAD).start()
            for g in gs:
                copy_in(g).wait()

                @pl.when(g >= MOE_BUFFERS)
                def _():
                    copy_out(g - MOE_BUFFERS).wait()
            for i, g in enumerate(gs):
                expert_rows(xbuf, ybuf, g % MOE_BUFFERS, count - (t + i) * MOE_TILE)
            for g in gs:
                copy_out(g).start()

        def pair(p, carry, tiles=tiles):
            tiles(MOE_UNROLL * p, MOE_UNROLL)
            return carry

        lax.fori_loop(0, n_tiles // MOE_UNROLL, pair, 0)

        def single(r, carry, tiles=tiles, n_tiles=n_tiles):
            tiles(n_tiles // MOE_UNROLL * MOE_UNROLL + r, 1)
            return carry

        lax.fori_loop(0, n_tiles % MOE_UNROLL, single, 0)

        @pl.when(e == last)
        def _(copy_out=copy_out, g_total=g_total):
            for j in range(1, MOE_BUFFERS + 1):
                @pl.when(g_total >= j)
                def _():
                    copy_out(g_total - j).wait()


def _moe_call(groups, w_g, w_u, w_d):
    (xs0, starts0, cnt0), (xs1, starts1, cnt1) = groups
    half = xs0.shape[1]
    n_experts, d_model, hidden = w_g.shape

    def tiles_total(starts, cnt):
        return ((starts[-1:] + cnt[-1:] + MOE_TILE - 1) // MOE_TILE).astype(I32)

    wspec = lambda shape: pl.BlockSpec(shape, lambda e, *_: (e, 0, 0))
    tile_bufs = [pltpu.VMEM((MOE_BUFFERS, MOE_TILE, half), U32)] * 4
    grid_spec = pltpu.PrefetchScalarGridSpec(
        num_scalar_prefetch=6,
        grid=(n_experts,),
        in_specs=[
            pl.BlockSpec(memory_space=pl.ANY),
            pl.BlockSpec(memory_space=pl.ANY),
            wspec((1, d_model, hidden)),
            wspec((1, d_model, hidden)),
            wspec((1, hidden, d_model)),
        ],
        out_specs=[pl.BlockSpec(memory_space=pl.ANY), pl.BlockSpec(memory_space=pl.ANY)],
        scratch_shapes=[
            pltpu.VMEM((d_model, 2 * hidden), BF16),
            pltpu.VMEM((hidden, d_model), BF16),
            *tile_bufs,
            *[pltpu.SemaphoreType.DMA((MOE_BUFFERS,))] * 4,
        ],
    )
    n_rows = xs0.shape[0] + xs1.shape[0]
    cost = pl.CostEstimate(flops=n_rows * 6 * d_model * hidden, transcendentals=n_rows * hidden,
                           bytes_accessed=2 * _nbytes(xs0, xs1) + _nbytes(w_g, w_u, w_d))
    return pl.pallas_call(
        _moe_kernel, grid_spec=grid_spec, cost_estimate=cost,
        out_shape=[jax.ShapeDtypeStruct(xs0.shape, U32), jax.ShapeDtypeStruct(xs1.shape, U32)], name="moe_experts",
        compiler_params=pltpu.CompilerParams(
            dimension_semantics=("arbitrary",), vmem_limit_bytes=VMEM_LIMIT),
    )(starts0, cnt0, tiles_total(starts0, cnt0), starts1, cnt1, tiles_total(starts1, cnt1),
      xs0, xs1, w_g, w_u, w_d)


def _combine_kernel(x1_ref, y0_ref, y1_ref, rw_ref, gf_ref, *rest):
    out_ref = rest[-1]
    tr = x1_ref.shape[0]
    w_rows = jnp.concatenate([rw_ref[0], jnp.zeros((LANES - rw_ref.shape[1], tr), F32)], axis=0)
    w_cols = w_rows.T
    w0, w1 = w_cols[:, 0:1], w_cols[:, 1:2]
    a_lo, a_hi = _unpack_bf16_pair(y0_ref[0])
    b_lo, b_hi = _unpack_bf16_pair(y1_ref[0])
    moe = jnp.concatenate([w0 * a_lo + w1 * b_lo, w0 * a_hi + w1 * b_hi], axis=-1)
    out_ref[...] = _rms(x1_ref[...] + moe, gf_ref[...])


def _combine_call(x1, rw, row0, n, yg, gf, out_rows, out_row0, prev_out=None):
    t, d_model = x1.shape
    tr = rw.shape[-1]
    half = yg.shape[-1]
    assert t % tr == 0 and row0 % tr == 0 and n % tr == 0 and rw.shape == (t // tr, EXPERTS_PER_GROUP, tr)
    assert yg.shape == (2, n, half) and out_row0 % tr == 0
    off = row0 // tr
    ooff = out_row0 // tr
    in_specs = [
        pl.BlockSpec((tr, d_model), lambda i: (off + i, 0)),
        pl.BlockSpec((1, tr, half), lambda i: (0, i, 0)),
        pl.BlockSpec((1, tr, half), lambda i: (1, i, 0)),
        pl.BlockSpec((1, EXPERTS_PER_GROUP, tr), lambda i: (off + i, 0, 0)),
        pl.BlockSpec((1, d_model), lambda i: (0, 0)),
    ]
    args = [x1, yg, yg, rw, gf]
    aliases = {}
    if prev_out is not None:
        in_specs.append(pl.BlockSpec(memory_space=pl.ANY))
        args.append(prev_out)
        aliases = {len(args) - 1: 0}
    return pl.pallas_call(
        _combine_kernel,
        grid=(n // tr,),
        in_specs=in_specs,
        out_specs=pl.BlockSpec((tr, d_model), lambda i: (ooff + i, 0)),
        out_shape=jax.ShapeDtypeStruct((out_rows, d_model), F32), name=f"combine_row{out_row0}_of{out_rows}",
        cost_estimate=pl.CostEstimate(flops=8 * n * d_model, transcendentals=n,
                                      bytes_accessed=2 * n * d_model * 4 + _nbytes(yg) + n * 4 * EXPERTS_PER_GROUP),
        input_output_aliases=aliases,
        compiler_params=pltpu.CompilerParams(
            dimension_semantics=("arbitrary",), vmem_limit_bytes=VMEM_LIMIT),
    )(*args)


def _route(streams, after=None):
    tokens = [h2.shape[0] for h2, _, _ in streams]
    counts = [cnt[:, 0].astype(I32) for _, _, cnt in streams]
    total = sum(counts)
    padded = ((total + MOE_TILE - 1) // MOE_TILE) * MOE_TILE
    starts = (jnp.cumsum(padded) - padded).astype(I32)
    experts = jnp.arange(N_EXPERTS, dtype=I32)[None, :, None]
    pos, base = [], starts
    for (_, ri, _), t, cnt in zip(streams, tokens, counts):
        ri = jnp.moveaxis(ri, 2, 0).reshape(ri.shape[2], t)
        first_row = jnp.sum(jnp.where(ri[0:2, None, :] == experts, base[None, :, None], 0), axis=1)
        pos.append(ri[2:4] + first_row)
        base = base + cnt
    pos = jnp.concatenate(pos, axis=1)
    n_rows = ((2 * sum(tokens) + N_EXPERTS * (MOE_TILE - 1)) // MOE_TILE) * MOE_TILE
    xs_sorted, ready = _sc_dispatch([h2 for h2, _, _ in streams], pos[0], pos[1], n_rows, after)
    return (xs_sorted, starts, total), pos, ready


def _gather_tokens(ys_sorted, pos, t0, n):
    return _sc_gather(ys_sorted, pos[:, t0:t0 + n].reshape(2 * n)).reshape(2, n, ys_sorted.shape[-1])


def _one_layer(xp, xs, s_ret, c_pool, norm1_g, w_in, ret_norm_g, w_pool, pool_scale, w_out, norm2_g,
               w_rg, w_re, w_g, w_u, w_d, final_g, past_len):
    bp, seq, d_model = xp.shape
    bs, dseq, _ = xs.shape
    rw_width = ret_norm_g.shape[-1]
    pw = pool_scale.shape[-1]
    dh = rw_width // RET_HEADS
    half = d_model // 2

    w_r = jnp.concatenate(
        [w_re.T, w_rg.T, jnp.zeros((ROUTER_ROWS - N_EXPERTS - N_EXPERT_GROUPS, d_model), F32)], axis=0)
    wr_hi = w_r.astype(BF16)
    wr = jnp.concatenate([wr_hi, (w_r - wr_hi.astype(F32)).astype(BF16)], axis=0)
    consts = dict(
        g1=norm1_g.reshape(1, d_model), w_in=w_in.astype(BF16), gret=ret_norm_g.reshape(1, rw_width),
        w_pool=w_pool.astype(BF16), pscale=pool_scale.reshape(1, pw), w_out=w_out.astype(BF16),
        g2=norm2_g.reshape(1, d_model), wr=wr)

    gf = final_g.reshape(1, d_model)

    ts = bs * dseq
    b_lead = bp - 1
    t_lead, t_rest = b_lead * seq, (bp - b_lead) * seq
    zeros = lambda nb: (jnp.zeros((nb, RET_HEADS, dh, dh), F32), jnp.zeros((nb, HIST_ROWS, pw), F32))
    h0s = jnp.pad(c_pool, ((0, 0), (HIST_ROWS - POOL_HIST, 0), (0, 0)))
    prompt_tile = dict(bb=1, tl=PROMPT_TILE, chunk=256)
    sample_tile = dict(bb=bs, tl=dseq, chunk=min(64, dseq))
    prompt_tables = _layer_tables(0, seq, dh, **prompt_tile)
    sample_tables = _layer_tables(past_len, dseq, dh, **sample_tile)

    def stream(layer_out, t):
        x1, h2, ri, rw, st, hist, cnt = layer_out
        return dict(x1=x1.reshape(t, d_model), route=(h2.reshape(t, half), ri, cnt),
                    rw=rw.reshape(-1, EXPERTS_PER_GROUP, rw.shape[-1]), st=st, hist=hist)

    pa = stream(_layer_call(xp, 0, b_lead, *zeros(b_lead), 0, consts, prompt_tables, gf, **prompt_tile), t_lead)
    group0, pos0, ready0 = _route([pa["route"]])
    pb = stream(_layer_call(xp, b_lead, bp - b_lead, *zeros(bp - b_lead), 0, consts, prompt_tables, ready0,
                            **prompt_tile), t_rest)
    sm = stream(_layer_call(xs, 0, bs, s_ret, h0s, past_len, consts, sample_tables, pb["route"][2], **sample_tile),
                ts)
    group1, pos1, _ = _route([pb["route"], sm["route"]], after=group0[0])
    ys0, ys1 = _moe_call([group0, group1], w_g, w_u, w_d)

    tp = bp * seq
    yp = _combine_call(pb["x1"], pb["rw"], 0, t_rest, _gather_tokens(ys1, pos1, 0, t_rest), gf, tp, t_lead)
    ysm = _combine_call(sm["x1"], sm["rw"], 0, ts, _gather_tokens(ys1, pos1, t_rest, ts), gf, ts, 0)
    row0 = 0
    for nb in COMBINE_CHUNKS:
        n = min(nb * seq, t_lead - row0)
        if n > 0:
            yp = _combine_call(pa["x1"], pa["rw"], row0, n, _gather_tokens(ys0, pos0, row0, n), gf, tp, row0,
                               prev_out=yp)
            row0 += n
    assert row0 == t_lead
    st_p = jnp.concatenate([pa["st"], pb["st"]], axis=0)
    hist_p = jnp.concatenate([pa["hist"], pb["hist"]], axis=0)
    return (yp.reshape(bp, seq, d_model), ysm.reshape(bs, dseq, d_model),
            st_p, hist_p[:, HIST_ROWS - POOL_HIST:], sm["st"], sm["hist"][:, HIST_ROWS - POOL_HIST:])


def kernel(x_prompt, x_sample, state_ret, cache_pool, norm1_g, w_in, ret_norm_g, w_pool, pool_scale, w_out,
           norm2_g, w_router_group, w_router_expert, w_exp_gate, w_exp_up, w_exp_down, final_norm_g):
    depth = w_in.shape[0]
    assert depth == 1, "the final RMSNorm is fused into the layer's combine kernel"
    assert x_prompt.shape[0] >= 2 and x_prompt.shape[1] % PROMPT_TILE == 0
    yp, ys, s_p, h_p, s_s, h_s = _one_layer(
        x_prompt, x_sample, state_ret[0], cache_pool[0], norm1_g[0], w_in[0], ret_norm_g[0], w_pool[0],
        pool_scale[0], w_out[0], norm2_g[0], w_router_group[0], w_router_expert[0],
        w_exp_gate[0], w_exp_up[0], w_exp_down[0], final_norm_g, PAST_LEN)
    return (yp, ys, s_p[None], h_p[None], s_s[None], h_s[None])
```

```python
import functools

import jax
import jax.numpy as jnp
from jax import lax
from jax.experimental import pallas as pl
from jax.experimental.pallas import tpu as pltpu
from jax.experimental.pallas import tpu_sc as plsc

F32 = jnp.float32
BF16 = jnp.bfloat16
I32 = jnp.int32
U32 = jnp.uint32

EPS = 1e-6
ROPE_BASE = 10000.0
RET_HEADS = 4
POOL_WINDOWS = (2, 4, 8, 16)
POOL_HIST = max(POOL_WINDOWS) - 1
N_EXPERT_GROUPS = 4
EXPERTS_PER_GROUP = 8
N_EXPERTS = N_EXPERT_GROUPS * EXPERTS_PER_GROUP
ROUTER_ROWS = 48
PAST_LEN = 1024

LANES = 128
SUBLANES = 8
HIST_ROWS = 16
MOE_TILE = 256
MOE_BUFFERS = 10
MOE_UNROLL = 2
MOE_LOOKAHEAD = MOE_BUFFERS - MOE_UNROLL
PREP_ROWS = 512
PROMPT_TILE = 1024
REST_CHUNK_ROWS = (2048, 1 << 30)
LEAD_CHUNK_ROWS = (8192, 1 << 30)
SC_UNIT = 32
VMEM_LIMIT = 56 * 1024 * 1024


def _nbytes(*arrays):
    return sum(a.size * a.dtype.itemsize for a in arrays)


def _rms(x, g):
    return x * lax.rsqrt(jnp.mean(x * x, axis=-1, keepdims=True) + EPS) * g


def _sigmoid(x):
    return 1.0 / (1.0 + jnp.exp(-x))


def _pack_bf16_pair(lo, hi):
    lo_b = lax.bitcast_convert_type(lo.astype(BF16).astype(F32), U32)
    hi_b = lax.bitcast_convert_type(hi.astype(BF16).astype(F32), U32)
    return hi_b | (lo_b >> 16)


def _unpack_bf16_pair(p):
    lo = lax.bitcast_convert_type(p << 16, F32)
    hi = lax.bitcast_convert_type(p & jnp.uint32(0xFFFF0000), F32)
    return lo, hi


def _layer_kernel(dc_ref, x_ref, s0_ref, h0_ref, rb_ref, rc_ref, rs_ref, rcs_ref, rss_ref,
                  dintra_ref, dq_ref, dk_ref,
                  g1_ref, win_ref, gret_ref, wpool_ref, pscale_ref, wout_ref, g2_ref,
                  wr_ref, tri_ref, after_ref,
                  x1_ref, h2_ref, ri_ref, rw_ref, st_ref, hist_ref, cnt_ref,
                  ue_ref, q_ref, k_ref, v_ref, gate_ref, a_ref,
                  *, bb, tl, chunk, pos0):
    b_idx = pl.program_id(0)
    l_idx = pl.program_id(1)
    rows = bb * tl
    d_model = x_ref.shape[-1]
    rw_width = q_ref.shape[-1]
    dh = rw_width // RET_HEADS
    pw = ue_ref.shape[-1]
    gw = pw // len(POOL_WINDOWS)
    n_chunks = tl // chunk

    @pl.when(l_idx == 0)
    def _():
        st_ref[...] = s0_ref[...]
        ue_ref[:, 0:HIST_ROWS, :] = h0_ref[...]

    @pl.when((l_idx == 0) & (b_idx == 0))
    def _():
        cnt_ref[...] = jnp.zeros_like(cnt_ref)

    cos_b = rb_ref[0, 0:1, :]
    sin_b = rb_ref[0, 1:2, :]
    cosf = cos_b * rc_ref[...] - sin_b * rs_ref[...]
    sinf = sin_b * rcs_ref[...] + cos_b * rss_ref[...]
    k_scale = dh ** -0.5
    n_blocks = max(1, tl // PREP_ROWS) if bb == 1 else 1
    block = rows // n_blocks
    for blk in range(n_blocks):
        rs = slice(blk * block, (blk + 1) * block)
        xb = x_ref[0, rs, :] if bb == 1 else x_ref[...].reshape(rows, d_model)
        hb = _rms(xb, g1_ref[...]).astype(BF16)
        proj = jnp.dot(hb, win_ref[...], preferred_element_type=F32)

        def rotate(a):
            if bb == 1:
                return a * cosf[rs] + pltpu.roll(a, dh // 2, 1) * sinf[rs]
            return (a.reshape(bb, tl, dh) * cosf[None]
                    + pltpu.roll(a, dh // 2, 1).reshape(bb, tl, dh) * sinf[None]).reshape(rows, dh)

        for hh in range(RET_HEADS):
            cs = slice(hh * dh, (hh + 1) * dh)
            q_ref[rs, cs] = rotate(proj[:, hh * dh:(hh + 1) * dh]).astype(BF16)
            k_ref[rs, cs] = rotate(proj[:, rw_width + hh * dh:rw_width + (hh + 1) * dh]) * k_scale
        v_ref[rs, :] = proj[:, 2 * rw_width:3 * rw_width].astype(BF16)
        gate_ref[rs, :] = proj[:, 3 * rw_width:4 * rw_width]
        u = proj[:, 4 * rw_width:4 * rw_width + pw]
        if bb == 1:
            ue_ref[0, HIST_ROWS + blk * block:HIST_ROWS + (blk + 1) * block, :] = u
        else:
            ue_ref[:, HIST_ROWS:HIST_ROWS + tl, :] = u.reshape(bb, tl, pw)

    def ret_block(b, c):
        r0 = b * tl + c * chunk
        if not isinstance(r0, int):
            r0 = pl.multiple_of(r0, chunk)
        rr = pl.ds(r0, chunk)
        for hh in range(RET_HEADS):
            cs = slice(hh * dh, (hh + 1) * dh)
            qc = q_ref[rr, cs]
            kf = k_ref[rr, cs]
            vc = v_ref[rr, cs]
            s_old = st_ref[b, hh]
            sc = lax.dot_general(qc, kf.astype(BF16), (((1,), (1,)), ((), ())),
                                 preferred_element_type=F32) * dintra_ref[hh]
            o = (jnp.dot(sc.astype(BF16), vc, preferred_element_type=F32)
                 + dq_ref[hh] * jnp.dot(qc, s_old.astype(BF16), preferred_element_type=F32))
            kd = (kf * dk_ref[hh]).astype(BF16)
            s_new = dc_ref[hh] * s_old + lax.dot_general(
                kd, vc, (((0,), (0,)), ((), ())), preferred_element_type=F32)
            st_ref[b, hh] = s_new
            oc = o - jnp.mean(o, axis=-1, keepdims=True)
            var = jnp.mean(oc * oc, axis=-1, keepdims=True)
            y = oc * lax.rsqrt(var + EPS) * gret_ref[:, cs]
            g = gate_ref[rr, cs]
            a_ref[rr, cs] = (g * _sigmoid(g) * y).astype(BF16)

    if bb * n_chunks <= 4:
        for b in range(bb):
            for c in range(n_chunks):
                ret_block(b, c)
    else:
        def body(i, carry):
            ret_block(i // n_chunks, i % n_chunks)
            return carry
        lax.fori_loop(0, bb * n_chunks, body, 0)

    nt = (((1,), (1,)), ((), ()))
    neg = jnp.float32(-jnp.inf)
    big = jnp.float32(1e9)
    sub = lax.broadcasted_iota(I32, (EXPERTS_PER_GROUP, block), 0).astype(F32)
    eid = lax.broadcasted_iota(I32, (N_EXPERTS, block), 0).astype(F32)
    for blk in range(n_blocks):
        lo = blk * block
        rs = slice(lo, lo + block)

        if bb == 1:
            pos = pos0 + l_idx * tl + lo + lax.broadcasted_iota(I32, (block, 1), 0)
            window = lambda off, cs: ue_ref[0, HIST_ROWS + lo - off:HIST_ROWS + lo - off + block, cs]
        else:
            pos = pos0 + l_idx * tl + lax.broadcasted_iota(I32, (1, tl, 1), 1)
            window = lambda off, cs: ue_ref[:, HIST_ROWS - off:HIST_ROWS - off + tl, cs]
        for gi, w in enumerate(POOL_WINDOWS):
            cs = slice(gi * gw, (gi + 1) * gw)
            u_g = window(0, cs)
            acc = u_g
            for j in range(1, w):
                acc = acc + window(j, cs)
            inv_cnt = 1.0 / jnp.minimum(pos + 1, w).astype(F32)
            p = (acc * inv_cnt - u_g).reshape(block, gw)
            z = jnp.dot(p.astype(BF16), wpool_ref[gi], preferred_element_type=F32) * pscale_ref[:, cs]
            a_ref[rs, rw_width + gi * gw:rw_width + (gi + 1) * gw] = z.astype(BF16)

        xb = x_ref[0, rs, :] if bb == 1 else x_ref[...].reshape(rows, d_model)
        x1 = xb + jnp.dot(a_ref[rs, :], wout_ref[...], preferred_element_type=F32)
        h2 = _rms(x1, g2_ref[...])
        h2_packed = _pack_bf16_pair(h2[:, 0:d_model // 2], h2[:, d_model // 2:])
        if bb == 1:
            x1_ref[0, rs, :] = x1
            h2_ref[0, rs, :] = h2_packed
        else:
            x1_ref[...] = x1.reshape(bb, tl, d_model)
            h2_ref[...] = h2_packed.reshape(bb, tl, d_model // 2)

        h2_hi = h2.astype(BF16)
        h2_lo = (h2 - h2_hi.astype(F32)).astype(BF16)
        two = lax.dot_general(wr_ref[...], h2_hi, nt, preferred_element_type=F32)
        lt = (two[0:ROUTER_ROWS] + two[ROUTER_ROWS:2 * ROUTER_ROWS]
              + lax.dot_general(wr_ref[0:ROUTER_ROWS, :], h2_lo, nt, preferred_element_type=F32))
        gl = jnp.where(sub < N_EXPERT_GROUPS, lt[N_EXPERTS:N_EXPERTS + EXPERTS_PER_GROUP], neg)
        gmax = jnp.max(gl, axis=0, keepdims=True)
        gidx = jnp.min(jnp.where(gl == gmax, sub, big), axis=0, keepdims=True)
        p_sel = 1.0 / jnp.sum(jnp.exp(gl - gmax), axis=0, keepdims=True)
        el = lt[0:EXPERTS_PER_GROUP]
        for g in range(1, N_EXPERT_GROUPS):
            el = jnp.where(gidx == g, lt[g * EXPERTS_PER_GROUP:(g + 1) * EXPERTS_PER_GROUP], el)
        m1 = jnp.max(el, axis=0, keepdims=True)
        t1 = jnp.min(jnp.where(el == m1, sub, big), axis=0, keepdims=True)
        el2 = jnp.where(sub == t1, neg, el)
        m2 = jnp.max(el2, axis=0, keepdims=True)
        t2 = jnp.min(jnp.where(el2 == m2, sub, big), axis=0, keepdims=True)
        e2 = jnp.exp(m2 - m1)
        w1 = p_sel / (1.0 + e2)
        w2 = p_sel * e2 / (1.0 + e2)
        i1 = gidx * EXPERTS_PER_GROUP + t1
        i2 = gidx * EXPERTS_PER_GROUP + t2

        hit1 = eid == i1
        hit2 = eid == i2
        onehot = (hit1 | hit2).astype(BF16)
        before = jnp.dot(onehot, tri_ref[...], preferred_element_type=F32) + cnt_ref[...]
        r1 = jnp.sum(jnp.where(hit1, before, 0.0), axis=0, keepdims=True)
        r2 = jnp.sum(jnp.where(hit2, before, 0.0), axis=0, keepdims=True)
        cnt_ref[...] = cnt_ref[...] + jnp.sum(onehot.astype(F32), axis=1, keepdims=True)

        ri = jnp.where(sub == 0, i1, jnp.where(sub == 1, i2, jnp.where(sub == 2, r1, jnp.where(sub == 3, r2, 0.0))))
        ri_ref[0, 0, :, rs] = ri.astype(I32)
        rw_ref[0, 0, :, rs] = jnp.where(sub == 0, w1, jnp.where(sub == 1, w2, 0.0))

    tail = ue_ref[:, tl:tl + HIST_ROWS, :]
    ue_ref[:, 0:HIST_ROWS, :] = tail
    hist_ref[...] = tail


def _rope_tables(pos0, seq, tl, dh):
    half = dh // 2
    inv = ROPE_BASE ** (-jnp.arange(half, dtype=F32) / half)
    ang_t = jnp.arange(tl, dtype=F32)[:, None] * inv[None, :]
    ang_b = (pos0 + tl * jnp.arange(seq // tl)).astype(F32)[:, None] * inv[None, :]
    dup = lambda a: jnp.concatenate([a, a], axis=-1)
    sgn = lambda a: jnp.concatenate([-a, a], axis=-1)
    base = jnp.stack([dup(jnp.cos(ang_b)), dup(jnp.sin(ang_b))], axis=1)
    base = jnp.pad(base, ((0, 0), (0, SUBLANES - base.shape[1]), (0, 0)))
    cos_t, sin_t = jnp.cos(ang_t), jnp.sin(ang_t)
    return base, dup(cos_t), dup(sin_t), sgn(cos_t), sgn(sin_t)


def _block_rows(bb, tl):
    return bb * tl // (max(1, tl // PREP_ROWS) if bb == 1 else 1)


def _layer_tables(pos0, seq, dh, *, bb, tl, chunk):
    lg = jnp.log1p(-jnp.exp2(-5.0 - jnp.arange(RET_HEADS, dtype=F32)))
    idx = jnp.arange(chunk, dtype=F32)
    diff = idx[:, None] - idx[None, :]
    d_intra = jnp.where(diff[None] >= 0, jnp.exp(jnp.maximum(diff, 0.0)[None] * lg[:, None, None]), 0.0)
    d_q = jnp.broadcast_to(jnp.exp((idx + 1.0)[None, :] * lg[:, None])[:, :, None], (RET_HEADS, chunk, dh))
    d_k = jnp.broadcast_to(jnp.exp((chunk - 1.0 - idx)[None, :] * lg[:, None])[:, :, None], (RET_HEADS, chunk, dh))
    d_c = jnp.exp(chunk * lg)
    block = _block_rows(bb, tl)
    tri = jnp.triu(jnp.ones((block, block), BF16), 1)
    return dict(rope=_rope_tables(pos0, seq, tl, dh), d_intra=d_intra, d_q=d_q, d_k=d_k, d_c=d_c, tri=tri)


def _layer_call(x, b0, nb, s0, h0, pos0, consts, tables, after, *, bb, tl, chunk):
    _, seq, d_model = x.shape
    bsz = nb
    blk0 = b0 // bb
    rows = bb * tl
    rw_width = consts["gret"].shape[-1]
    pw = consts["pscale"].shape[-1]
    dh = rw_width // RET_HEADS
    block = _block_rows(bb, tl)
    rope, d_intra, d_q, d_k, d_c, tri = (tables[k] for k in ("rope", "d_intra", "d_q", "d_k", "d_c", "tri"))

    const2 = lambda b, l, *_: (0, 0)
    const3 = lambda b, l, *_: (0, 0, 0)
    grid_spec = pltpu.PrefetchScalarGridSpec(
        num_scalar_prefetch=0,
        grid=(bsz // bb, seq // tl),
        in_specs=[
            pl.BlockSpec(memory_space=pltpu.SMEM),
            pl.BlockSpec((bb, tl, d_model), lambda b, l: (blk0 + b, l, 0)),
            pl.BlockSpec((bb, RET_HEADS, dh, dh), lambda b, l: (b, 0, 0, 0)),
            pl.BlockSpec((bb, HIST_ROWS, pw), lambda b, l: (b, 0, 0)),
            pl.BlockSpec((1, SUBLANES, dh), lambda b, l: (l, 0, 0)),
            pl.BlockSpec((tl, dh), const2),
            pl.BlockSpec((tl, dh), const2),
            pl.BlockSpec((tl, dh), const2),
            pl.BlockSpec((tl, dh), const2),
            pl.BlockSpec((RET_HEADS, chunk, chunk), const3),
            pl.BlockSpec((RET_HEADS, chunk, dh), const3),
            pl.BlockSpec((RET_HEADS, chunk, dh), const3),
            pl.BlockSpec((1, d_model), const2),
            pl.BlockSpec(consts["w_in"].shape, const2),
            pl.BlockSpec((1, rw_width), const2),
            pl.BlockSpec(consts["w_pool"].shape, const3),
            pl.BlockSpec((1, pw), const2),
            pl.BlockSpec(consts["w_out"].shape, const2),
            pl.BlockSpec((1, d_model), const2),
            pl.BlockSpec((2 * ROUTER_ROWS, d_model), const2),
            pl.BlockSpec((block, block), const2),
            pl.BlockSpec(memory_space=pl.ANY),
        ],
        out_specs=[
            pl.BlockSpec((bb, tl, d_model), lambda b, l: (b, l, 0)),
            pl.BlockSpec((bb, tl, d_model // 2), lambda b, l: (b, l, 0)),
            pl.BlockSpec((1, 1, EXPERTS_PER_GROUP, rows), lambda b, l: (b, l, 0, 0)),
            pl.BlockSpec((1, 1, EXPERTS_PER_GROUP, rows), lambda b, l: (b, l, 0, 0)),
            pl.BlockSpec((bb, RET_HEADS, dh, dh), lambda b, l: (b, 0, 0, 0)),
            pl.BlockSpec((bb, HIST_ROWS, pw), lambda b, l: (b, 0, 0)),
            pl.BlockSpec((N_EXPERTS, block), const2),
        ],
        scratch_shapes=[
            pltpu.VMEM((bb, HIST_ROWS + tl, pw), F32),
            pltpu.VMEM((rows, rw_width), BF16),
            pltpu.VMEM((rows, rw_width), F32),
            pltpu.VMEM((rows, rw_width), BF16),
            pltpu.VMEM((rows, rw_width), F32),
            pltpu.VMEM((rows, d_model), BF16),
        ],
    )
    out_shape = [
        jax.ShapeDtypeStruct((bsz, seq, d_model), F32),
        jax.ShapeDtypeStruct((bsz, seq, d_model // 2), U32),
        jax.ShapeDtypeStruct((bsz // bb, seq // tl, EXPERTS_PER_GROUP, rows), I32),
        jax.ShapeDtypeStruct((bsz // bb, seq // tl, EXPERTS_PER_GROUP, rows), F32),
        jax.ShapeDtypeStruct((bsz, RET_HEADS, dh, dh), F32),
        jax.ShapeDtypeStruct((bsz, HIST_ROWS, pw), F32),
        jax.ShapeDtypeStruct((N_EXPERTS, block), F32),
    ]
    kern = functools.partial(_layer_kernel, bb=bb, tl=tl, chunk=chunk, pos0=pos0)
    operands = (d_c, x, s0, h0, *rope, d_intra, d_q, d_k,
                consts["g1"], consts["w_in"], consts["gret"], consts["w_pool"], consts["pscale"],
                consts["w_out"], consts["g2"], consts["wr"], tri, after)
    n_tok = bsz * seq
    mm_flops_per_token = 2 * (d_model * consts["w_in"].shape[1] + d_model * d_model + 3 * d_model * ROUTER_ROWS
                              + pw * pw // 4
                              + rw_width * (2 * chunk + 2 * dh) + N_EXPERTS * block)
    cost = pl.CostEstimate(
        flops=n_tok * mm_flops_per_token, transcendentals=n_tok * (rw_width + 2 * N_EXPERT_GROUPS),
        bytes_accessed=_nbytes(*operands) - _nbytes(x, after) + n_tok * d_model * 4 + _nbytes(*out_shape))
    return pl.pallas_call(
        kern, grid_spec=grid_spec, out_shape=out_shape, name=f"layer_pos{pos0}_b{b0}", cost_estimate=cost,
        compiler_params=pltpu.CompilerParams(
            dimension_semantics=("arbitrary", "arbitrary"), vmem_limit_bytes=VMEM_LIMIT),
    )(*operands)


def _sc_partition(n_units):
    info = plsc.get_sparse_core_info()
    nc, nw = info.num_cores, info.num_cores * info.num_subcores
    upw = -(-n_units // nw)
    upw += upw % 2
    return nc, nw, upw


def _units_by_worker(idx, n_units, upw, nw):
    idx = jnp.pad(idx.reshape(n_units, SC_UNIT), ((0, nw * upw - n_units), (0, 0)))
    return idx.reshape(upw, nw, SC_UNIT).transpose(1, 0, 2)


def _sc_dispatch(srcs, idx0, idx1, n_out_rows, after=None):
    assert 1 <= len(srcs) <= 2
    d = srcs[0].shape[1]
    dtype = srcs[0].dtype
    assert all(src.shape[0] % SC_UNIT == 0 for src in srcs)
    units_a = srcs[0].shape[0] // SC_UNIT
    n_units = sum(src.shape[0] for src in srcs) // SC_UNIT
    nc, nw, upw = _sc_partition(n_units)
    idx0 = _units_by_worker(idx0, n_units, upw, nw)
    idx1 = _units_by_worker(idx1, n_units, upw, nw)
    mesh = plsc.VectorSubcoreMesh(core_axis_name="c", subcore_axis_name="s")
    dma = pltpu.SemaphoreType.DMA
    extra = [] if after is None else [after]

    moved = n_units * SC_UNIT * d * jnp.dtype(dtype).itemsize
    @functools.partial(
        pl.kernel, mesh=mesh,
        cost_estimate=pl.CostEstimate(flops=0, transcendentals=0, bytes_accessed=3 * moved + _nbytes(idx0, idx1)),
        out_type=jax.ShapeDtypeStruct((n_out_rows, d), dtype),
        scratch_types=[
            pltpu.VMEM((upw, SC_UNIT), I32),
            pltpu.VMEM((upw, SC_UNIT), I32),
            pltpu.VMEM((SC_UNIT, d), dtype),
            pltpu.VMEM((SC_UNIT, d), dtype),
            dma, dma, dma, dma, dma, dma,
        ],
    )
    def k(*refs):
        src_hbm = refs[:len(srcs)]
        i0_hbm, i1_hbm, out_hbm, i0_v, i1_v, rows0, rows1, l0, l1, p0, p1, q0, q1 = refs[len(srcs) + len(extra):]
        wid = lax.axis_index("s") * nc + lax.axis_index("c")
        pltpu.sync_copy(i0_hbm.at[wid], i0_v)
        pltpu.sync_copy(i1_hbm.at[wid], i1_v)
        rows, lsem, psem, qsem = (rows0, rows1), (l0, l1), (p0, p1), (q0, q1)

        def live(j):
            return j * nw + wid < n_units

        def load(j, b, op):
            unit = j * nw + wid

            @pl.when(live(j) & (unit < units_a))
            def _():
                op(pltpu.make_async_copy(
                    src_hbm[0].at[pl.ds(pl.multiple_of(unit * SC_UNIT, 8), SC_UNIT)], rows[b], lsem[b]))

            if len(srcs) == 2:
                @pl.when(live(j) & (unit >= units_a))
                def _():
                    op(pltpu.make_async_copy(
                        src_hbm[1].at[pl.ds(pl.multiple_of((unit - units_a) * SC_UNIT, 8), SC_UNIT)],
                        rows[b], lsem[b]))

        def scatter(j, b, op):
            @pl.when(live(j))
            def _():
                op(pltpu.make_async_copy(rows[b], out_hbm.at[i0_v.at[j]], psem[b]))
                op(pltpu.make_async_copy(rows[b], out_hbm.at[i1_v.at[j]], qsem[b]))

        start = lambda c: c.start()
        wait = lambda c: c.wait()
        load(0, 0, start)

        @pl.loop(0, upw, step=2)
        def _(j):
            @pl.when(j > 0)
            def _():
                scatter(j - 1, 1, wait)
            load(j + 1, 1, start)
            load(j, 0, wait)
            scatter(j, 0, start)
            scatter(j, 0, wait)

            @pl.when(j + 2 < upw)
            def _():
                load(j + 2, 0, start)
            load(j + 1, 1, wait)
            scatter(j + 1, 1, start)

        scatter(upw - 1, 1, wait)

    return k(*srcs, *extra, idx0, idx1), idx1


def _sc_gather(table, idx, after=None):
    n = idx.shape[0]
    d = table.shape[1]
    assert n % SC_UNIT == 0
    n_units = n // SC_UNIT
    nc, nw, upw = _sc_partition(n_units)
    idx = _units_by_worker(idx, n_units, upw, nw)
    mesh = plsc.VectorSubcoreMesh(core_axis_name="c", subcore_axis_name="s")
    dma = pltpu.SemaphoreType.DMA

    @functools.partial(
        pl.kernel, mesh=mesh,
        cost_estimate=pl.CostEstimate(flops=0, transcendentals=0,
                                      bytes_accessed=2 * n * d * table.dtype.itemsize + _nbytes(idx)),
        out_type=jax.ShapeDtypeStruct((n, d), table.dtype),
        scratch_types=[
            pltpu.VMEM((upw, SC_UNIT), I32),
            pltpu.VMEM((SC_UNIT, d), table.dtype),
            pltpu.VMEM((SC_UNIT, d), table.dtype),
            dma, dma, dma, dma,
        ],
    )
    def k(t_hbm, i_hbm, *rest):
        out_hbm, i_v, rows0, rows1, g0, g1, w0, w1 = rest[-8:]
        wid = lax.axis_index("s") * nc + lax.axis_index("c")
        pltpu.sync_copy(i_hbm.at[wid], i_v)
        rows, gsem, wsem = (rows0, rows1), (g0, g1), (w0, w1)

        def live(j):
            return j * nw + wid < n_units

        def gather(j, b, op):
            @pl.when(live(j))
            def _():
                op(pltpu.make_async_copy(t_hbm.at[i_v.at[j]], rows[b], gsem[b]))

        def write(j, b, op):
            @pl.when(live(j))
            def _():
                op(pltpu.make_async_copy(
                    rows[b], out_hbm.at[pl.ds(pl.multiple_of((j * nw + wid) * SC_UNIT, 8), SC_UNIT)], wsem[b]))

        start = lambda c: c.start()
        wait = lambda c: c.wait()
        gather(0, 0, start)

        @pl.loop(0, upw, step=2)
        def _(j):
            @pl.when(j > 0)
            def _():
                write(j - 1, 1, wait)
            gather(j + 1, 1, start)
            gather(j, 0, wait)
            write(j, 0, start)
            write(j, 0, wait)

            @pl.when(j + 2 < upw)
            def _():
                gather(j + 2, 0, start)
            gather(j + 1, 1, wait)
            write(j + 1, 1, start)

        write(upw - 1, 1, wait)

    return k(table, idx, *([] if after is None else [after]))


def _moe_kernel(start0_ref, count0_ref, gtot0_ref, start1_ref, count1_ref, gtot1_ref,
                xs0_hbm, xs1_hbm, wg_ref, wu_ref, wd_ref, ys0_hbm, ys1_hbm,
                wgu_s, wd_s, xbuf0, ybuf0, xbuf1, ybuf1, sem_in0, sem_out0, sem_in1, sem_out1):
    e = pl.program_id(0)
    last = pl.num_programs(0) - 1
    hidden = wd_s.shape[0]
    half = xbuf0.shape[-1]
    segments = (
        (xs0_hbm, ys0_hbm, xbuf0, ybuf0, sem_in0, sem_out0, start0_ref[e], count0_ref[e], gtot0_ref[0]),
        (xs1_hbm, ys1_hbm, xbuf1, ybuf1, sem_in1, sem_out1, start1_ref[e], count1_ref[e], gtot1_ref[0]),
    )

    def rows_of(g):
        return pl.ds(pl.multiple_of(g * MOE_TILE, MOE_TILE), MOE_TILE)

    def pipeline(xs_hbm, ys_hbm, xbuf, ybuf, sem_in, sem_out):
        def copy_in(g):
            slot = g % MOE_BUFFERS
            return pltpu.make_async_copy(xs_hbm.at[rows_of(g)], xbuf.at[slot], sem_in.at[slot])

        def copy_out(g):
            slot = g % MOE_BUFFERS
            return pltpu.make_async_copy(ybuf.at[slot], ys_hbm.at[rows_of(g)], sem_out.at[slot])
        return copy_in, copy_out

    @pl.when(e == 0)
    def _():
        for xs_hbm, ys_hbm, xbuf, ybuf, sem_in, sem_out, _, _, g_total in segments:
            copy_in, _ = pipeline(xs_hbm, ys_hbm, xbuf, ybuf, sem_in, sem_out)
            for g in range(MOE_LOOKAHEAD):
                @pl.when(g < g_total)
                def _():
                    copy_in(g).start()

    @pl.when(segments[0][7] + segments[1][7] > 0)
    def _():
        wgu_s[:, 0:hidden] = wg_ref[0].astype(BF16)
        wgu_s[:, hidden:2 * hidden] = wu_ref[0].astype(BF16)
        wd_s[...] = wd_ref[0].astype(BF16)

    def expert_rows(xbuf, ybuf, slot, valid):
        row = lax.broadcasted_iota(I32, (MOE_TILE, half), 0)
        x_lo, x_hi = _unpack_bf16_pair(jnp.where(row < valid, xbuf[slot], jnp.uint32(0)))
        ab = (jnp.dot(x_lo.astype(BF16), wgu_s[0:half, :], preferred_element_type=F32)
              + jnp.dot(x_hi.astype(BF16), wgu_s[half:2 * half, :], preferred_element_type=F32))
        a = ab[:, 0:hidden]
        he = a * _sigmoid(a) * ab[:, hidden:2 * hidden]
        y = jnp.dot(he.astype(BF16), wd_s[...], preferred_element_type=F32)
        ybuf[slot] = _pack_bf16_pair(y[:, 0:half], y[:, half:2 * half])

    for xs_hbm, ys_hbm, xbuf, ybuf, sem_in, sem_out, start, count, g_total in segments:
        copy_in, copy_out = pipeline(xs_hbm, ys_hbm, xbuf, ybuf, sem_in, sem_out)
        g_first = start // MOE_TILE
        n_tiles = (count + MOE_TILE - 1) // MOE_TILE

        def tiles(t, width, copy_in=copy_in, copy_out=copy_out, xbuf=xbuf, ybuf=ybuf,
                  g_first=g_first, count=count, g_total=g_total):
            gs = [g_first + t + i for i in range(width)]
            for g in gs:
                @pl.when(g + MOE_LOOKAHEAD < g_total)
                def _():
                    copy_in(g + MOE_LOOKAHEAD).start()
            for g in gs:
                copy_in(g).wait()

                @pl.when(g >= MOE_BUFFERS)
                def _():
                    copy_out(g - MOE_BUFFERS).wait()
            for i, g in enumerate(gs):
                expert_rows(xbuf, ybuf, g % MOE_BUFFERS, count - (t + i) * MOE_TILE)
            for g in gs:
                copy_out(g).start()

        def pair(p, carry, tiles=tiles):
            tiles(MOE_UNROLL * p, MOE_UNROLL)
            return carry

        lax.fori_loop(0, n_tiles // MOE_UNROLL, pair, 0)

        def single(r, carry, tiles=tiles, n_tiles=n_tiles):
            tiles(n_tiles // MOE_UNROLL * MOE_UNROLL + r, 1)
            return carry

        lax.fori_loop(0, n_tiles % MOE_UNROLL, single, 0)

        @pl.when(e == last)
        def _(copy_out=copy_out, g_total=g_total):
            for j in range(1, MOE_BUFFERS + 1):
                @pl.when(g_total >= j)
                def _():
                    copy_out(g_total - j).wait()


def _moe_call(groups, w_g, w_u, w_d):
    (xs0, starts0, cnt0), (xs1, starts1, cnt1) = groups
    half = xs0.shape[1]
    n_experts, d_model, hidden = w_g.shape

    def tiles_total(starts, cnt):
        return ((starts[-1:] + cnt[-1:] + MOE_TILE - 1) // MOE_TILE).astype(I32)

    wspec = lambda shape: pl.BlockSpec(shape, lambda e, *_: (e, 0, 0))
    tile_bufs = [pltpu.VMEM((MOE_BUFFERS, MOE_TILE, half), U32)] * 4
    grid_spec = pltpu.PrefetchScalarGridSpec(
        num_scalar_prefetch=6,
        grid=(n_experts,),
        in_specs=[
            pl.BlockSpec(memory_space=pl.ANY),
            pl.BlockSpec(memory_space=pl.ANY),
            wspec((1, d_model, hidden)),
            wspec((1, d_model, hidden)),
            wspec((1, hidden, d_model)),
        ],
        out_specs=[pl.BlockSpec(memory_space=pl.ANY), pl.BlockSpec(memory_space=pl.ANY)],
        scratch_shapes=[
            pltpu.VMEM((d_model, 2 * hidden), BF16),
            pltpu.VMEM((hidden, d_model), BF16),
            *tile_bufs,
            *[pltpu.SemaphoreType.DMA((MOE_BUFFERS,))] * 4,
        ],
    )
    n_rows = xs0.shape[0] + xs1.shape[0]
    cost = pl.CostEstimate(flops=n_rows * 6 * d_model * hidden, transcendentals=n_rows * hidden,
                           bytes_accessed=2 * _nbytes(xs0, xs1) + _nbytes(w_g, w_u, w_d))
    return pl.pallas_call(
        _moe_kernel, grid_spec=grid_spec, cost_estimate=cost,
        out_shape=[jax.ShapeDtypeStruct(xs0.shape, U32), jax.ShapeDtypeStruct(xs1.shape, U32)], name="moe_experts",
        compiler_params=pltpu.CompilerParams(
            dimension_semantics=("arbitrary",), vmem_limit_bytes=VMEM_LIMIT),
    )(starts0, cnt0, tiles_total(starts0, cnt0), starts1, cnt1, tiles_total(starts1, cnt1),
      xs0, xs1, w_g, w_u, w_d)


def _combine_kernel(x1_ref, y0_ref, y1_ref, rw_ref, gf_ref, *rest):
    out_ref = rest[-1]
    tr = x1_ref.shape[0]
    w_rows = jnp.concatenate([rw_ref[0], jnp.zeros((LANES - rw_ref.shape[1], tr), F32)], axis=0)
    w_cols = w_rows.T
    w0, w1 = w_cols[:, 0:1], w_cols[:, 1:2]
    a_lo, a_hi = _unpack_bf16_pair(y0_ref[0])
    b_lo, b_hi = _unpack_bf16_pair(y1_ref[0])
    moe = jnp.concatenate([w0 * a_lo + w1 * b_lo, w0 * a_hi + w1 * b_hi], axis=-1)
    out_ref[...] = _rms(x1_ref[...] + moe, gf_ref[...])


def _combine_call(x1, rw, row0, n, yg, gf, out_rows, out_row0, prev_out=None):
    t, d_model = x1.shape
    tr = rw.shape[-1]
    half = yg.shape[-1]
    assert t % tr == 0 and row0 % tr == 0 and n % tr == 0 and rw.shape == (t // tr, EXPERTS_PER_GROUP, tr)
    assert yg.shape == (2, n, half) and out_row0 % tr == 0
    off = row0 // tr
    ooff = out_row0 // tr
    in_specs = [
        pl.BlockSpec((tr, d_model), lambda i: (off + i, 0)),
        pl.BlockSpec((1, tr, half), lambda i: (0, i, 0)),
        pl.BlockSpec((1, tr, half), lambda i: (1, i, 0)),
        pl.BlockSpec((1, EXPERTS_PER_GROUP, tr), lambda i: (off + i, 0, 0)),
        pl.BlockSpec((1, d_model), lambda i: (0, 0)),
    ]
    args = [x1, yg, yg, rw, gf]
    aliases = {}
    if prev_out is not None:
        in_specs.append(pl.BlockSpec(memory_space=pl.ANY))
        args.append(prev_out)
        aliases = {len(args) - 1: 0}
    return pl.pallas_call(
        _combine_kernel,
        grid=(n // tr,),
        in_specs=in_specs,
        out_specs=pl.BlockSpec((tr, d_model), lambda i: (ooff + i, 0)),
        out_shape=jax.ShapeDtypeStruct((out_rows, d_model), F32), name=f"combine_row{out_row0}_of{out_rows}",
        cost_estimate=pl.CostEstimate(flops=8 * n * d_model, transcendentals=n,
                                      bytes_accessed=2 * n * d_model * 4 + _nbytes(yg) + n * 4 * EXPERTS_PER_GROUP),
        input_output_aliases=aliases,
        compiler_params=pltpu.CompilerParams(
            dimension_semantics=("arbitrary",), vmem_limit_bytes=VMEM_LIMIT),
    )(*args)


def _route(streams, after=None):
    tokens = [h2.shape[0] for h2, _, _ in streams]
    counts = [cnt[:, 0].astype(I32) for _, _, cnt in streams]
    total = sum(counts)
    padded = ((total + MOE_TILE - 1) // MOE_TILE) * MOE_TILE
    starts = (jnp.cumsum(padded) - padded).astype(I32)
    experts = jnp.arange(N_EXPERTS, dtype=I32)[None, :, None]
    pos, base = [], starts
    for (_, ri, _), t, cnt in zip(streams, tokens, counts):
        ri = jnp.moveaxis(ri, 2, 0).reshape(ri.shape[2], t)
        first_row = jnp.sum(jnp.where(ri[0:2, None, :] == experts, base[None, :, None], 0), axis=1)
        pos.append(ri[2:4] + first_row)
        base = base + cnt
    pos = jnp.concatenate(pos, axis=1)
    n_rows = ((2 * sum(tokens) + N_EXPERTS * (MOE_TILE - 1)) // MOE_TILE) * MOE_TILE
    xs_sorted, ready = _sc_dispatch([h2 for h2, _, _ in streams], pos[0], pos[1], n_rows, after)
    return (xs_sorted, starts, total), pos, ready


def _gather_tokens(ys_sorted, pos, t0, n, after=None):
    return _sc_gather(ys_sorted, pos[:, t0:t0 + n].reshape(2 * n), after).reshape(2, n, ys_sorted.shape[-1])


def _one_layer(xp, xs, s_ret, c_pool, norm1_g, w_in, ret_norm_g, w_pool, pool_scale, w_out, norm2_g,
               w_rg, w_re, w_g, w_u, w_d, final_g, past_len):
    bp, seq, d_model = xp.shape
    bs, dseq, _ = xs.shape
    rw_width = ret_norm_g.shape[-1]
    pw = pool_scale.shape[-1]
    dh = rw_width // RET_HEADS
    half = d_model // 2

    w_r = jnp.concatenate(
        [w_re.T, w_rg.T, jnp.zeros((ROUTER_ROWS - N_EXPERTS - N_EXPERT_GROUPS, d_model), F32)], axis=0)
    wr_hi = w_r.astype(BF16)
    wr = jnp.concatenate([wr_hi, (w_r - wr_hi.astype(F32)).astype(BF16)], axis=0)
    consts = dict(
        g1=norm1_g.reshape(1, d_model), w_in=w_in.astype(BF16), gret=ret_norm_g.reshape(1, rw_width),
        w_pool=w_pool.astype(BF16), pscale=pool_scale.reshape(1, pw), w_out=w_out.astype(BF16),
        g2=norm2_g.reshape(1, d_model), wr=wr)

    gf = final_g.reshape(1, d_model)

    ts = bs * dseq
    b_lead = bp - 1
    t_lead, t_rest = b_lead * seq, (bp - b_lead) * seq
    zeros = lambda nb: (jnp.zeros((nb, RET_HEADS, dh, dh), F32), jnp.zeros((nb, HIST_ROWS, pw), F32))
    h0s = jnp.pad(c_pool, ((0, 0), (HIST_ROWS - POOL_HIST, 0), (0, 0)))
    prompt_tile = dict(bb=1, tl=PROMPT_TILE, chunk=256)
    sample_tile = dict(bb=bs, tl=dseq, chunk=min(64, dseq))
    prompt_tables = _layer_tables(0, seq, dh, **prompt_tile)
    sample_tables = _layer_tables(past_len, dseq, dh, **sample_tile)

    def stream(layer_out, t):
        x1, h2, ri, rw, st, hist, cnt = layer_out
        return dict(x1=x1.reshape(t, d_model), route=(h2.reshape(t, half), ri, cnt),
                    rw=rw.reshape(-1, EXPERTS_PER_GROUP, rw.shape[-1]), st=st, hist=hist)

    pa = stream(_layer_call(xp, 0, b_lead, *zeros(b_lead), 0, consts, prompt_tables, gf, **prompt_tile), t_lead)
    group0, pos0, ready0 = _route([pa["route"]])
    pb = stream(_layer_call(xp, b_lead, bp - b_lead, *zeros(bp - b_lead), 0, consts, prompt_tables, ready0,
                            **prompt_tile), t_rest)
    sm = stream(_layer_call(xs, 0, bs, s_ret, h0s, past_len, consts, sample_tables, pb["route"][2], **sample_tile),
                ts)
    group1, pos1, _ = _route([pb["route"], sm["route"]], after=group0[0])
    ys0, ys1 = _moe_call([group0, group1], w_g, w_u, w_d)

    tp = bp * seq
    ysm_g = _gather_tokens(ys1, pos1, t_rest, ts)
    ysm = _combine_call(sm["x1"], sm["rw"], 0, ts, ysm_g, gf, ts, 0)
    yp, prev = None, ysm_g
    chunks = [(pb, ys1, pos1, t_lead, t_rest, REST_CHUNK_ROWS), (pa, ys0, pos0, 0, t_lead, LEAD_CHUNK_ROWS)]
    for st, ys, pos, out0, t_stream, sizes in chunks:
        row0 = 0
        for size in sizes:
            n = min(size, t_stream - row0)
            if n > 0:
                prev = _gather_tokens(ys, pos, row0, n, after=prev)
                yp = _combine_call(st["x1"], st["rw"], row0, n, prev, gf, tp, out0 + row0, prev_out=yp)
                row0 += n
        assert row0 == t_stream
    st_p = jnp.concatenate([pa["st"], pb["st"]], axis=0)
    hist_p = jnp.concatenate([pa["hist"], pb["hist"]], axis=0)
    return (yp.reshape(bp, seq, d_model), ysm.reshape(bs, dseq, d_model),
            st_p, hist_p[:, HIST_ROWS - POOL_HIST:], sm["st"], sm["hist"][:, HIST_ROWS - POOL_HIST:])


def kernel(x_prompt, x_sample, state_ret, cache_pool, norm1_g, w_in, ret_norm_g, w_pool, pool_scale, w_out,
           norm2_g, w_router_group, w_router_expert, w_exp_gate, w_exp_up, w_exp_down, final_norm_g):
    depth = w_in.shape[0]
    assert depth == 1, "the final RMSNorm is fused into the layer's combine kernel"
    assert x_prompt.shape[0] >= 2 and x_prompt.shape[1] % PROMPT_TILE == 0
    yp, ys, s_p, h_p, s_s, h_s = _one_layer(
        x_prompt, x_sample, state_ret[0], cache_pool[0], norm1_g[0], w_in[0], ret_norm_g[0], w_pool[0],
        pool_scale[0], w_out[0], norm2_g[0], w_router_group[0], w_router_expert[0],
        w_exp_gate[0], w_exp_up[0], w_exp_down[0], final_norm_g, PAST_LEN)
    return (yp, ys, s_p[None], h_p[None], s_s[None], h_s[None])
```

```python
import functools

import jax
import jax.numpy as jnp
from jax import lax
from jax.experimental import pallas as pl
from jax.experimental.pallas import tpu as pltpu
from jax.experimental.pallas import tpu_sc as plsc

F32 = jnp.float32
BF16 = jnp.bfloat16
I32 = jnp.int32
U32 = jnp.uint32

EPS = 1e-6
ROPE_BASE = 10000.0
RET_HEADS = 4
POOL_WINDOWS = (2, 4, 8, 16)
POOL_HIST = max(POOL_WINDOWS) - 1
N_EXPERT_GROUPS = 4
EXPERTS_PER_GROUP = 8
N_EXPERTS = N_EXPERT_GROUPS * EXPERTS_PER_GROUP
ROUTER_ROWS = 48
PAST_LEN = 1024

LANES = 128
SUBLANES = 8
HIST_ROWS = 16
MOE_TILE = 256
MOE_BUFFERS = 10
MOE_UNROLL = 2
MOE_LOOKAHEAD = MOE_BUFFERS - MOE_UNROLL
PREP_ROWS = 512
PROMPT_TILE = 1024
COMBINE_CHUNKS = (1, 2)
SC_UNIT = 32
VMEM_LIMIT = 56 * 1024 * 1024


def _nbytes(*arrays):
    return sum(a.size * a.dtype.itemsize for a in arrays)


def _rms(x, g):
    return x * lax.rsqrt(jnp.mean(x * x, axis=-1, keepdims=True) + EPS) * g


def _sigmoid(x):
    return 1.0 / (1.0 + jnp.exp(-x))


def _pack_bf16_pair(lo, hi):
    lo_b = lax.bitcast_convert_type(lo.astype(BF16).astype(F32), U32)
    hi_b = lax.bitcast_convert_type(hi.astype(BF16).astype(F32), U32)
    return hi_b | (lo_b >> 16)


def _unpack_bf16_pair(p):
    lo = lax.bitcast_convert_type(p << 16, F32)
    hi = lax.bitcast_convert_type(p & jnp.uint32(0xFFFF0000), F32)
    return lo, hi


def _layer_kernel(dc_ref, x_ref, s0_ref, h0_ref, rb_ref, rc_ref, rs_ref, rcs_ref, rss_ref,
                  dintra_ref, dq_ref, dk_ref,
                  g1_ref, win_ref, gret_ref, wpool_ref, pscale_ref, wout_ref, g2_ref,
                  wr_ref, tri_ref, after_ref,
                  x1_ref, h2_ref, ri_ref, rw_ref, st_ref, hist_ref, cnt_ref,
                  ue_ref, q_ref, k_ref, v_ref, gate_ref, a_ref,
                  *, bb, tl, chunk, pos0):
    b_idx = pl.program_id(0)
    l_idx = pl.program_id(1)
    rows = bb * tl
    d_model = x_ref.shape[-1]
    rw_width = q_ref.shape[-1]
    dh = rw_width // RET_HEADS
    pw = ue_ref.shape[-1]
    gw = pw // len(POOL_WINDOWS)
    n_chunks = tl // chunk

    @pl.when(l_idx == 0)
    def _():
        st_ref[...] = s0_ref[...]
        ue_ref[:, 0:HIST_ROWS, :] = h0_ref[...]

    @pl.when((l_idx == 0) & (b_idx == 0))
    def _():
        cnt_ref[...] = jnp.zeros_like(cnt_ref)

    cos_b = rb_ref[0, 0:1, :]
    sin_b = rb_ref[0, 1:2, :]
    cosf = cos_b * rc_ref[...] - sin_b * rs_ref[...]
    sinf = sin_b * rcs_ref[...] + cos_b * rss_ref[...]
    k_scale = dh ** -0.5
    n_blocks = max(1, tl // PREP_ROWS) if bb == 1 else 1
    block = rows // n_blocks
    for blk in range(n_blocks):
        rs = slice(blk * block, (blk + 1) * block)
        xb = x_ref[0, rs, :] if bb == 1 else x_ref[...].reshape(rows, d_model)
        hb = _rms(xb, g1_ref[...]).astype(BF16)
        proj = jnp.dot(hb, win_ref[...], preferred_element_type=F32)

        def rotate(a):
            if bb == 1:
                return a * cosf[rs] + pltpu.roll(a, dh // 2, 1) * sinf[rs]
            return (a.reshape(bb, tl, dh) * cosf[None]
                    + pltpu.roll(a, dh // 2, 1).reshape(bb, tl, dh) * sinf[None]).reshape(rows, dh)

        for hh in range(RET_HEADS):
            cs = slice(hh * dh, (hh + 1) * dh)
            q_ref[rs, cs] = rotate(proj[:, hh * dh:(hh + 1) * dh]).astype(BF16)
            k_ref[rs, cs] = rotate(proj[:, rw_width + hh * dh:rw_width + (hh + 1) * dh]) * k_scale
        v_ref[rs, :] = proj[:, 2 * rw_width:3 * rw_width].astype(BF16)
        gate_ref[rs, :] = proj[:, 3 * rw_width:4 * rw_width]
        u = proj[:, 4 * rw_width:4 * rw_width + pw]
        if bb == 1:
            ue_ref[0, HIST_ROWS + blk * block:HIST_ROWS + (blk + 1) * block, :] = u
        else:
            ue_ref[:, HIST_ROWS:HIST_ROWS + tl, :] = u.reshape(bb, tl, pw)

    def ret_block(b, c):
        r0 = b * tl + c * chunk
        if not isinstance(r0, int):
            r0 = pl.multiple_of(r0, chunk)
        rr = pl.ds(r0, chunk)
        for hh in range(RET_HEADS):
            cs = slice(hh * dh, (hh + 1) * dh)
            qc = q_ref[rr, cs]
            kf = k_ref[rr, cs]
            vc = v_ref[rr, cs]
            s_old = st_ref[b, hh]
            sc = lax.dot_general(qc, kf.astype(BF16), (((1,), (1,)), ((), ())),
                                 preferred_element_type=F32) * dintra_ref[hh]
            o = (jnp.dot(sc.astype(BF16), vc, preferred_element_type=F32)
                 + dq_ref[hh] * jnp.dot(qc, s_old.astype(BF16), preferred_element_type=F32))
            kd = (kf * dk_ref[hh]).astype(BF16)
            s_new = dc_ref[hh] * s_old + lax.dot_general(
                kd, vc, (((0,), (0,)), ((), ())), preferred_element_type=F32)
            st_ref[b, hh] = s_new
            oc = o - jnp.mean(o, axis=-1, keepdims=True)
            var = jnp.mean(oc * oc, axis=-1, keepdims=True)
            y = oc * lax.rsqrt(var + EPS) * gret_ref[:, cs]
            g = gate_ref[rr, cs]
            a_ref[rr, cs] = (g * _sigmoid(g) * y).astype(BF16)

    if bb * n_chunks <= 4:
        for b in range(bb):
            for c in range(n_chunks):
                ret_block(b, c)
    else:
        def body(i, carry):
            ret_block(i // n_chunks, i % n_chunks)
            return carry
        lax.fori_loop(0, bb * n_chunks, body, 0)

    nt = (((1,), (1,)), ((), ()))
    neg = jnp.float32(-jnp.inf)
    big = jnp.float32(1e9)
    sub = lax.broadcasted_iota(I32, (EXPERTS_PER_GROUP, block), 0).astype(F32)
    eid = lax.broadcasted_iota(I32, (N_EXPERTS, block), 0).astype(F32)
    for blk in range(n_blocks):
        lo = blk * block
        rs = slice(lo, lo + block)

        if bb == 1:
            pos = pos0 + l_idx * tl + lo + lax.broadcasted_iota(I32, (block, 1), 0)
            window = lambda off, cs: ue_ref[0, HIST_ROWS + lo - off:HIST_ROWS + lo - off + block, cs]
        else:
            pos = pos0 + l_idx * tl + lax.broadcasted_iota(I32, (1, tl, 1), 1)
            window = lambda off, cs: ue_ref[:, HIST_ROWS - off:HIST_ROWS - off + tl, cs]
        for gi, w in enumerate(POOL_WINDOWS):
            cs = slice(gi * gw, (gi + 1) * gw)
            u_g = window(0, cs)
            acc = u_g
            for j in range(1, w):
                acc = acc + window(j, cs)
            inv_cnt = 1.0 / jnp.minimum(pos + 1, w).astype(F32)
            p = (acc * inv_cnt - u_g).reshape(block, gw)
            z = jnp.dot(p.astype(BF16), wpool_ref[gi], preferred_element_type=F32) * pscale_ref[:, cs]
            a_ref[rs, rw_width + gi * gw:rw_width + (gi + 1) * gw] = z.astype(BF16)

        xb = x_ref[0, rs, :] if bb == 1 else x_ref[...].reshape(rows, d_model)
        x1 = xb + jnp.dot(a_ref[rs, :], wout_ref[...], preferred_element_type=F32)
        h2 = _rms(x1, g2_ref[...])
        h2_packed = _pack_bf16_pair(h2[:, 0:d_model // 2], h2[:, d_model // 2:])
        if bb == 1:
            x1_ref[0, rs, :] = x1
            h2_ref[0, rs, :] = h2_packed
        else:
            x1_ref[...] = x1.reshape(bb, tl, d_model)
            h2_ref[...] = h2_packed.reshape(bb, tl, d_model // 2)

        lt = lax.dot_general(wr_ref[...], h2.astype(BF16), nt, preferred_element_type=F32)
        gl = jnp.where(sub < N_EXPERT_GROUPS, lt[N_EXPERTS:N_EXPERTS + EXPERTS_PER_GROUP], neg)
        gmax = jnp.max(gl, axis=0, keepdims=True)
        gidx = jnp.min(jnp.where(gl == gmax, sub, big), axis=0, keepdims=True)
        p_sel = 1.0 / jnp.sum(jnp.exp(gl - gmax), axis=0, keepdims=True)
        el = lt[0:EXPERTS_PER_GROUP]
        for g in range(1, N_EXPERT_GROUPS):
            el = jnp.where(gidx == g, lt[g * EXPERTS_PER_GROUP:(g + 1) * EXPERTS_PER_GROUP], el)
        m1 = jnp.max(el, axis=0, keepdims=True)
        t1 = jnp.min(jnp.where(el == m1, sub, big), axis=0, keepdims=True)
        el2 = jnp.where(sub == t1, neg, el)
        m2 = jnp.max(el2, axis=0, keepdims=True)
        t2 = jnp.min(jnp.where(el2 == m2, sub, big), axis=0, keepdims=True)
        e2 = jnp.exp(m2 - m1)
        w1 = p_sel / (1.0 + e2)
        w2 = p_sel * e2 / (1.0 + e2)
        i1 = gidx * EXPERTS_PER_GROUP + t1
        i2 = gidx * EXPERTS_PER_GROUP + t2

        hit1 = eid == i1
        hit2 = eid == i2
        onehot = (hit1 | hit2).astype(BF16)
        before = jnp.dot(onehot, tri_ref[...], preferred_element_type=F32) + cnt_ref[...]
        r1 = jnp.sum(jnp.where(hit1, before, 0.0), axis=0, keepdims=True)
        r2 = jnp.sum(jnp.where(hit2, before, 0.0), axis=0, keepdims=True)
        cnt_ref[...] = cnt_ref[...] + jnp.sum(onehot.astype(F32), axis=1, keepdims=True)

        ri = jnp.where(sub == 0, i1, jnp.where(sub == 1, i2, jnp.where(sub == 2, r1, jnp.where(sub == 3, r2, 0.0))))
        ri_ref[0, 0, :, rs] = ri.astype(I32)
        rw_ref[0, 0, :, rs] = jnp.where(sub == 0, w1, jnp.where(sub == 1, w2, 0.0))

    tail = ue_ref[:, tl:tl + HIST_ROWS, :]
    ue_ref[:, 0:HIST_ROWS, :] = tail
    hist_ref[...] = tail


def _rope_tables(pos0, seq, tl, dh):
    half = dh // 2
    inv = ROPE_BASE ** (-jnp.arange(half, dtype=F32) / half)
    ang_t = jnp.arange(tl, dtype=F32)[:, None] * inv[None, :]
    ang_b = (pos0 + tl * jnp.arange(seq // tl)).astype(F32)[:, None] * inv[None, :]
    dup = lambda a: jnp.concatenate([a, a], axis=-1)
    sgn = lambda a: jnp.concatenate([-a, a], axis=-1)
    base = jnp.stack([dup(jnp.cos(ang_b)), dup(jnp.sin(ang_b))], axis=1)
    base = jnp.pad(base, ((0, 0), (0, SUBLANES - base.shape[1]), (0, 0)))
    cos_t, sin_t = jnp.cos(ang_t), jnp.sin(ang_t)
    return base, dup(cos_t), dup(sin_t), sgn(cos_t), sgn(sin_t)


def _block_rows(bb, tl):
    return bb * tl // (max(1, tl // PREP_ROWS) if bb == 1 else 1)


def _layer_tables(pos0, seq, dh, *, bb, tl, chunk):
    lg = jnp.log1p(-jnp.exp2(-5.0 - jnp.arange(RET_HEADS, dtype=F32)))
    idx = jnp.arange(chunk, dtype=F32)
    diff = idx[:, None] - idx[None, :]
    d_intra = jnp.where(diff[None] >= 0, jnp.exp(jnp.maximum(diff, 0.0)[None] * lg[:, None, None]), 0.0)
    d_q = jnp.broadcast_to(jnp.exp((idx + 1.0)[None, :] * lg[:, None])[:, :, None], (RET_HEADS, chunk, dh))
    d_k = jnp.broadcast_to(jnp.exp((chunk - 1.0 - idx)[None, :] * lg[:, None])[:, :, None], (RET_HEADS, chunk, dh))
    d_c = jnp.exp(chunk * lg)
    block = _block_rows(bb, tl)
    tri = jnp.triu(jnp.ones((block, block), BF16), 1)
    return dict(rope=_rope_tables(pos0, seq, tl, dh), d_intra=d_intra, d_q=d_q, d_k=d_k, d_c=d_c, tri=tri)


def _layer_call(x, b0, nb, s0, h0, pos0, consts, tables, after, *, bb, tl, chunk):
    _, seq, d_model = x.shape
    bsz = nb
    blk0 = b0 // bb
    rows = bb * tl
    rw_width = consts["gret"].shape[-1]
    pw = consts["pscale"].shape[-1]
    dh = rw_width // RET_HEADS
    block = _block_rows(bb, tl)
    rope, d_intra, d_q, d_k, d_c, tri = (tables[k] for k in ("rope", "d_intra", "d_q", "d_k", "d_c", "tri"))

    const2 = lambda b, l, *_: (0, 0)
    const3 = lambda b, l, *_: (0, 0, 0)
    grid_spec = pltpu.PrefetchScalarGridSpec(
        num_scalar_prefetch=0,
        grid=(bsz // bb, seq // tl),
        in_specs=[
            pl.BlockSpec(memory_space=pltpu.SMEM),
            pl.BlockSpec((bb, tl, d_model), lambda b, l: (blk0 + b, l, 0)),
            pl.BlockSpec((bb, RET_HEADS, dh, dh), lambda b, l: (b, 0, 0, 0)),
            pl.BlockSpec((bb, HIST_ROWS, pw), lambda b, l: (b, 0, 0)),
            pl.BlockSpec((1, SUBLANES, dh), lambda b, l: (l, 0, 0)),
            pl.BlockSpec((tl, dh), const2),
            pl.BlockSpec((tl, dh), const2),
            pl.BlockSpec((tl, dh), const2),
            pl.BlockSpec((tl, dh), const2),
            pl.BlockSpec((RET_HEADS, chunk, chunk), const3),
            pl.BlockSpec((RET_HEADS, chunk, dh), const3),
            pl.BlockSpec((RET_HEADS, chunk, dh), const3),
            pl.BlockSpec((1, d_model), const2),
            pl.BlockSpec(consts["w_in"].shape, const2),
            pl.BlockSpec((1, rw_width), const2),
            pl.BlockSpec(consts["w_pool"].shape, const3),
            pl.BlockSpec((1, pw), const2),
            pl.BlockSpec(consts["w_out"].shape, const2),
            pl.BlockSpec((1, d_model), const2),
            pl.BlockSpec((ROUTER_ROWS, d_model), const2),
            pl.BlockSpec((block, block), const2),
            pl.BlockSpec(memory_space=pl.ANY),
        ],
        out_specs=[
            pl.BlockSpec((bb, tl, d_model), lambda b, l: (b, l, 0)),
            pl.BlockSpec((bb, tl, d_model // 2), lambda b, l: (b, l, 0)),
            pl.BlockSpec((1, 1, EXPERTS_PER_GROUP, rows), lambda b, l: (b, l, 0, 0)),
            pl.BlockSpec((1, 1, EXPERTS_PER_GROUP, rows), lambda b, l: (b, l, 0, 0)),
            pl.BlockSpec((bb, RET_HEADS, dh, dh), lambda b, l: (b, 0, 0, 0)),
            pl.BlockSpec((bb, HIST_ROWS, pw), lambda b, l: (b, 0, 0)),
            pl.BlockSpec((N_EXPERTS, block), const2),
        ],
        scratch_shapes=[
            pltpu.VMEM((bb, HIST_ROWS + tl, pw), F32),
            pltpu.VMEM((rows, rw_width), BF16),
            pltpu.VMEM((rows, rw_width), F32),
            pltpu.VMEM((rows, rw_width), BF16),
            pltpu.VMEM((rows, rw_width), F32),
            pltpu.VMEM((rows, d_model), BF16),
        ],
    )
    out_shape = [
        jax.ShapeDtypeStruct((bsz, seq, d_model), F32),
        jax.ShapeDtypeStruct((bsz, seq, d_model // 2), U32),
        jax.ShapeDtypeStruct((bsz // bb, seq // tl, EXPERTS_PER_GROUP, rows), I32),
        jax.ShapeDtypeStruct((bsz // bb, seq // tl, EXPERTS_PER_GROUP, rows), F32),
        jax.ShapeDtypeStruct((bsz, RET_HEADS, dh, dh), F32),
        jax.ShapeDtypeStruct((bsz, HIST_ROWS, pw), F32),
        jax.ShapeDtypeStruct((N_EXPERTS, block), F32),
    ]
    kern = functools.partial(_layer_kernel, bb=bb, tl=tl, chunk=chunk, pos0=pos0)
    operands = (d_c, x, s0, h0, *rope, d_intra, d_q, d_k,
                consts["g1"], consts["w_in"], consts["gret"], consts["w_pool"], consts["pscale"],
                consts["w_out"], consts["g2"], consts["wr"], tri, after)
    n_tok = bsz * seq
    mm_flops_per_token = 2 * (d_model * consts["w_in"].shape[1] + d_model * d_model + d_model * ROUTER_ROWS
                              + pw * pw // 4
                              + rw_width * (2 * chunk + 2 * dh) + N_EXPERTS * block)
    cost = pl.CostEstimate(
        flops=n_tok * mm_flops_per_token, transcendentals=n_tok * (rw_width + 2 * N_EXPERT_GROUPS),
        bytes_accessed=_nbytes(*operands) - _nbytes(x, after) + n_tok * d_model * 4 + _nbytes(*out_shape))
    return pl.pallas_call(
        kern, grid_spec=grid_spec, out_shape=out_shape, name=f"layer_pos{pos0}_b{b0}", cost_estimate=cost,
        compiler_params=pltpu.CompilerParams(
            dimension_semantics=("arbitrary", "arbitrary"), vmem_limit_bytes=VMEM_LIMIT),
    )(*operands)


def _sc_partition(n_units):
    info = plsc.get_sparse_core_info()
    nc, nw = info.num_cores, info.num_cores * info.num_subcores
    upw = -(-n_units // nw)
    upw += upw % 2
    return nc, nw, upw


def _units_by_worker(idx, n_units, upw, nw):
    idx = jnp.pad(idx.reshape(n_units, SC_UNIT), ((0, nw * upw - n_units), (0, 0)))
    return idx.reshape(upw, nw, SC_UNIT).transpose(1, 0, 2)


def _sc_dispatch(srcs, idx0, idx1, n_out_rows, after=None):
    assert 1 <= len(srcs) <= 2
    d = srcs[0].shape[1]
    dtype = srcs[0].dtype
    assert all(src.shape[0] % SC_UNIT == 0 for src in srcs)
    units_a = srcs[0].shape[0] // SC_UNIT
    n_units = sum(src.shape[0] for src in srcs) // SC_UNIT
    nc, nw, upw = _sc_partition(n_units)
    idx0 = _units_by_worker(idx0, n_units, upw, nw)
    idx1 = _units_by_worker(idx1, n_units, upw, nw)
    mesh = plsc.VectorSubcoreMesh(core_axis_name="c", subcore_axis_name="s")
    dma = pltpu.SemaphoreType.DMA
    extra = [] if after is None else [after]

    moved = n_units * SC_UNIT * d * jnp.dtype(dtype).itemsize
    @functools.partial(
        pl.kernel, mesh=mesh,
        cost_estimate=pl.CostEstimate(flops=0, transcendentals=0, bytes_accessed=3 * moved + _nbytes(idx0, idx1)),
        out_type=jax.ShapeDtypeStruct((n_out_rows, d), dtype),
        scratch_types=[
            pltpu.VMEM((upw, SC_UNIT), I32),
            pltpu.VMEM((upw, SC_UNIT), I32),
            pltpu.VMEM((SC_UNIT, d), dtype),
            pltpu.VMEM((SC_UNIT, d), dtype),
            dma, dma, dma, dma, dma, dma,
        ],
    )
    def k(*refs):
        src_hbm = refs[:len(srcs)]
        i0_hbm, i1_hbm, out_hbm, i0_v, i1_v, rows0, rows1, l0, l1, p0, p1, q0, q1 = refs[len(srcs) + len(extra):]
        wid = lax.axis_index("s") * nc + lax.axis_index("c")
        pltpu.sync_copy(i0_hbm.at[wid], i0_v)
        pltpu.sync_copy(i1_hbm.at[wid], i1_v)
        rows, lsem, psem, qsem = (rows0, rows1), (l0, l1), (p0, p1), (q0, q1)

        def live(j):
            return j * nw + wid < n_units

        def load(j, b, op):
            unit = j * nw + wid

            @pl.when(live(j) & (unit < units_a))
            def _():
                op(pltpu.make_async_copy(
                    src_hbm[0].at[pl.ds(pl.multiple_of(unit * SC_UNIT, 8), SC_UNIT)], rows[b], lsem[b]))

            if len(srcs) == 2:
                @pl.when(live(j) & (unit >= units_a))
                def _():
                    op(pltpu.make_async_copy(
                        src_hbm[1].at[pl.ds(pl.multiple_of((unit - units_a) * SC_UNIT, 8), SC_UNIT)],
                        rows[b], lsem[b]))

        def scatter(j, b, op):
            @pl.when(live(j))
            def _():
                op(pltpu.make_async_copy(rows[b], out_hbm.at[i0_v.at[j]], psem[b]))
                op(pltpu.make_async_copy(rows[b], out_hbm.at[i1_v.at[j]], qsem[b]))

        start = lambda c: c.start()
        wait = lambda c: c.wait()
        load(0, 0, start)

        @pl.loop(0, upw, step=2)
        def _(j):
            @pl.when(j > 0)
            def _():
                scatter(j - 1, 1, wait)
            load(j + 1, 1, start)
            load(j, 0, wait)
            scatter(j, 0, start)
            scatter(j, 0, wait)

            @pl.when(j + 2 < upw)
            def _():
                load(j + 2, 0, start)
            load(j + 1, 1, wait)
            scatter(j + 1, 1, start)

        scatter(upw - 1, 1, wait)

    return k(*srcs, *extra, idx0, idx1), idx1


def _sc_gather(table, idx):
    n = idx.shape[0]
    d = table.shape[1]
    assert n % SC_UNIT == 0
    n_units = n // SC_UNIT
    nc, nw, upw = _sc_partition(n_units)
    idx = _units_by_worker(idx, n_units, upw, nw)
    mesh = plsc.VectorSubcoreMesh(core_axis_name="c", subcore_axis_name="s")
    dma = pltpu.SemaphoreType.DMA

    @functools.partial(
        pl.kernel, mesh=mesh,
        cost_estimate=pl.CostEstimate(flops=0, transcendentals=0,
                                      bytes_accessed=2 * n * d * table.dtype.itemsize + _nbytes(idx)),
        out_type=jax.ShapeDtypeStruct((n, d), table.dtype),
        scratch_types=[
            pltpu.VMEM((upw, SC_UNIT), I32),
            pltpu.VMEM((SC_UNIT, d), table.dtype),
            pltpu.VMEM((SC_UNIT, d), table.dtype),
            dma, dma, dma, dma,
        ],
    )
    def k(t_hbm, i_hbm, out_hbm, i_v, rows0, rows1, g0, g1, w0, w1):
        wid = lax.axis_index("s") * nc + lax.axis_index("c")
        pltpu.sync_copy(i_hbm.at[wid], i_v)
        rows, gsem, wsem = (rows0, rows1), (g0, g1), (w0, w1)

        def live(j):
            return j * nw + wid < n_units

        def gather(j, b, op):
            @pl.when(live(j))
            def _():
                op(pltpu.make_async_copy(t_hbm.at[i_v.at[j]], rows[b], gsem[b]))

        def write(j, b, op):
            @pl.when(live(j))
            def _():
                op(pltpu.make_async_copy(
                    rows[b], out_hbm.at[pl.ds(pl.multiple_of((j * nw + wid) * SC_UNIT, 8), SC_UNIT)], wsem[b]))

        start = lambda c: c.start()
        wait = lambda c: c.wait()
        gather(0, 0, start)

        @pl.loop(0, upw, step=2)
        def _(j):
            @pl.when(j > 0)
            def _():
                write(j - 1, 1, wait)
            gather(j + 1, 1, start)
            gather(j, 0, wait)
            write(j, 0, start)
            write(j, 0, wait)

            @pl.when(j + 2 < upw)
            def _():
                gather(j + 2, 0, start)
            gather(j + 1, 1, wait)
            write(j + 1, 1, start)

        write(upw - 1, 1, wait)

    return k(table, idx)


def _moe_kernel(start0_ref, count0_ref, gtot0_ref, start1_ref, count1_ref, gtot1_ref,
                xs0_hbm, xs1_hbm, wg_ref, wu_ref, wd_ref, ys0_hbm, ys1_hbm,
                wgu_s, wd_s, xbuf0, ybuf0, xbuf1, ybuf1, sem_in0, sem_out0, sem_in1, sem_out1):
    e = pl.program_id(0)
    last = pl.num_programs(0) - 1
    hidden = wd_s.shape[0]
    half = xbuf0.shape[-1]
    segments = (
        (xs0_hbm, ys0_hbm, xbuf0, ybuf0, sem_in0, sem_out0, start0_ref[e], count0_ref[e], gtot0_ref[0]),
        (xs1_hbm, ys1_hbm, xbuf1, ybuf1, sem_in1, sem_out1, start1_ref[e], count1_ref[e], gtot1_ref[0]),
    )

    def rows_of(g):
        return pl.ds(pl.multiple_of(g * MOE_TILE, MOE_TILE), MOE_TILE)

    def pipeline(xs_hbm, ys_hbm, xbuf, ybuf, sem_in, sem_out):
        def copy_in(g):
            slot = g % MOE_BUFFERS
            return pltpu.make_async_copy(xs_hbm.at[rows_of(g)], xbuf.at[slot], sem_in.at[slot])

        def copy_out(g):
            slot = g % MOE_BUFFERS
            return pltpu.make_async_copy(ybuf.at[slot], ys_hbm.at[rows_of(g)], sem_out.at[slot])
        return copy_in, copy_out

    @pl.when(e == 0)
    def _():
        for xs_hbm, ys_hbm, xbuf, ybuf, sem_in, sem_out, _, _, g_total in segments:
            copy_in, _ = pipeline(xs_hbm, ys_hbm, xbuf, ybuf, sem_in, sem_out)
            for g in range(MOE_LOOKAHEAD):
                @pl.when(g < g_total)
                def _():
                    copy_in(g).start()

    @pl.when(segments[0][7] + segments[1][7] > 0)
    def _():
        wgu_s[:, 0:hidden] = wg_ref[0].astype(BF16)
        wgu_s[:, hidden:2 * hidden] = wu_ref[0].astype(BF16)
        wd_s[...] = wd_ref[0].astype(BF16)

    def expert_rows(xbuf, ybuf, slot, valid):
        row = lax.broadcasted_iota(I32, (MOE_TILE, half), 0)
        x_lo, x_hi = _unpack_bf16_pair(jnp.where(row < valid, xbuf[slot], jnp.uint32(0)))
        ab = (jnp.dot(x_lo.astype(BF16), wgu_s[0:half, :], preferred_element_type=F32)
              + jnp.dot(x_hi.astype(BF16), wgu_s[half:2 * half, :], preferred_element_type=F32))
        a = ab[:, 0:hidden]
        he = a * _sigmoid(a) * ab[:, hidden:2 * hidden]
        y = jnp.dot(he.astype(BF16), wd_s[...], preferred_element_type=F32)
        ybuf[slot] = _pack_bf16_pair(y[:, 0:half], y[:, half:2 * half])

    for xs_hbm, ys_hbm, xbuf, ybuf, sem_in, sem_out, start, count, g_total in segments:
        copy_in, copy_out = pipeline(xs_hbm, ys_hbm, xbuf, ybuf, sem_in, sem_out)
        g_first = start // MOE_TILE
        n_tiles = (count + MOE_TILE - 1) // MOE_TILE

        def tiles(t, width, copy_in=copy_in, copy_out=copy_out, xbuf=xbuf, ybuf=ybuf,
                  g_first=g_first, count=count, g_total=g_total):
            gs = [g_first + t + i for i in range(width)]
            for g in gs:
                @pl.when(g + MOE_LOOKAHEAD < g_total)
                def _():
                    copy_in(g + MOE_LOOKAHEAD).start()
            for g in gs:
                copy_in(g).wait()

                @pl.when(g >= MOE_BUFFERS)
                def _():
                    copy_out(g - MOE_BUFFERS).wait()
            for i, g in enumerate(gs):
                expert_rows(xbuf, ybuf, g % MOE_BUFFERS, count - (t + i) * MOE_TILE)
            for g in gs:
                copy_out(g).start()

        def pair(p, carry, tiles=tiles):
            tiles(MOE_UNROLL * p, MOE_UNROLL)
            return carry

        lax.fori_loop(0, n_tiles // MOE_UNROLL, pair, 0)

        def single(r, carry, tiles=tiles, n_tiles=n_tiles):
            tiles(n_tiles // MOE_UNROLL * MOE_UNROLL + r, 1)
            return carry

        lax.fori_loop(0, n_tiles % MOE_UNROLL, single, 0)

        @pl.when(e == last)
        def _(copy_out=copy_out, g_total=g_total):
            for j in range(1, MOE_BUFFERS + 1):
                @pl.when(g_total >= j)
                def _():
                    copy_out(g_total - j).wait()


def _moe_call(groups, w_g, w_u, w_d):
    (xs0, starts0, cnt0), (xs1, starts1, cnt1) = groups
    half = xs0.shape[1]
    n_experts, d_model, hidden = w_g.shape

    def tiles_total(starts, cnt):
        return ((starts[-1:] + cnt[-1:] + MOE_TILE - 1) // MOE_TILE).astype(I32)

    wspec = lambda shape: pl.BlockSpec(shape, lambda e, *_: (e, 0, 0))
    tile_bufs = [pltpu.VMEM((MOE_BUFFERS, MOE_TILE, half), U32)] * 4
    grid_spec = pltpu.PrefetchScalarGridSpec(
        num_scalar_prefetch=6,
        grid=(n_experts,),
        in_specs=[
            pl.BlockSpec(memory_space=pl.ANY),
            pl.BlockSpec(memory_space=pl.ANY),
            wspec((1, d_model, hidden)),
            wspec((1, d_model, hidden)),
            wspec((1, hidden, d_model)),
        ],
        out_specs=[pl.BlockSpec(memory_space=pl.ANY), pl.BlockSpec(memory_space=pl.ANY)],
        scratch_shapes=[
            pltpu.VMEM((d_model, 2 * hidden), BF16),
            pltpu.VMEM((hidden, d_model), BF16),
            *tile_bufs,
            *[pltpu.SemaphoreType.DMA((MOE_BUFFERS,))] * 4,
        ],
    )
    n_rows = xs0.shape[0] + xs1.shape[0]
    cost = pl.CostEstimate(flops=n_rows * 6 * d_model * hidden, transcendentals=n_rows * hidden,
                           bytes_accessed=2 * _nbytes(xs0, xs1) + _nbytes(w_g, w_u, w_d))
    return pl.pallas_call(
        _moe_kernel, grid_spec=grid_spec, cost_estimate=cost,
        out_shape=[jax.ShapeDtypeStruct(xs0.shape, U32), jax.ShapeDtypeStruct(xs1.shape, U32)], name="moe_experts",
        compiler_params=pltpu.CompilerParams(
            dimension_semantics=("arbitrary",), vmem_limit_bytes=VMEM_LIMIT),
    )(starts0, cnt0, tiles_total(starts0, cnt0), starts1, cnt1, tiles_total(starts1, cnt1),
      xs0, xs1, w_g, w_u, w_d)


def _combine_kernel(x1_ref, y0_ref, y1_ref, rw_ref, gf_ref, *rest):
    out_ref = rest[-1]
    tr = x1_ref.shape[0]
    w_rows = jnp.concatenate([rw_ref[0], jnp.zeros((LANES - rw_ref.shape[1], tr), F32)], axis=0)
    w_cols = w_rows.T
    w0, w1 = w_cols[:, 0:1], w_cols[:, 1:2]
    a_lo, a_hi = _unpack_bf16_pair(y0_ref[0])
    b_lo, b_hi = _unpack_bf16_pair(y1_ref[0])
    moe = jnp.concatenate([w0 * a_lo + w1 * b_lo, w0 * a_hi + w1 * b_hi], axis=-1)
    out_ref[...] = _rms(x1_ref[...] + moe, gf_ref[...])


def _combine_call(x1, rw, row0, n, yg, gf, out_rows, out_row0, prev_out=None):
    t, d_model = x1.shape
    tr = rw.shape[-1]
    half = yg.shape[-1]
    assert t % tr == 0 and row0 % tr == 0 and n % tr == 0 and rw.shape == (t // tr, EXPERTS_PER_GROUP, tr)
    assert yg.shape == (2, n, half) and out_row0 % tr == 0
    off = row0 // tr
    ooff = out_row0 // tr
    in_specs = [
        pl.BlockSpec((tr, d_model), lambda i: (off + i, 0)),
        pl.BlockSpec((1, tr, half), lambda i: (0, i, 0)),
        pl.BlockSpec((1, tr, half), lambda i: (1, i, 0)),
        pl.BlockSpec((1, EXPERTS_PER_GROUP, tr), lambda i: (off + i, 0, 0)),
        pl.BlockSpec((1, d_model), lambda i: (0, 0)),
    ]
    args = [x1, yg, yg, rw, gf]
    aliases = {}
    if prev_out is not None:
        in_specs.append(pl.BlockSpec(memory_space=pl.ANY))
        args.append(prev_out)
        aliases = {len(args) - 1: 0}
    return pl.pallas_call(
        _combine_kernel,
        grid=(n // tr,),
        in_specs=in_specs,
        out_specs=pl.BlockSpec((tr, d_model), lambda i: (ooff + i, 0)),
        out_shape=jax.ShapeDtypeStruct((out_rows, d_model), F32), name=f"combine_row{out_row0}_of{out_rows}",
        cost_estimate=pl.CostEstimate(flops=8 * n * d_model, transcendentals=n,
                                      bytes_accessed=2 * n * d_model * 4 + _nbytes(yg) + n * 4 * EXPERTS_PER_GROUP),
        input_output_aliases=aliases,
        compiler_params=pltpu.CompilerParams(
            dimension_semantics=("arbitrary",), vmem_limit_bytes=VMEM_LIMIT),
    )(*args)


def _route(streams, after=None):
    tokens = [h2.shape[0] for h2, _, _ in streams]
    counts = [cnt[:, 0].astype(I32) for _, _, cnt in streams]
    total = sum(counts)
    padded = ((total + MOE_TILE - 1) // MOE_TILE) * MOE_TILE
    starts = (jnp.cumsum(padded) - padded).astype(I32)
    experts = jnp.arange(N_EXPERTS, dtype=I32)[None, :, None]
    pos, base = [], starts
    for (_, ri, _), t, cnt in zip(streams, tokens, counts):
        ri = jnp.moveaxis(ri, 2, 0).reshape(ri.shape[2], t)
        first_row = jnp.sum(jnp.where(ri[0:2, None, :] == experts, base[None, :, None], 0), axis=1)
        pos.append(ri[2:4] + first_row)
        base = base + cnt
    pos = jnp.concatenate(pos, axis=1)
    n_rows = ((2 * sum(tokens) + N_EXPERTS * (MOE_TILE - 1)) // MOE_TILE) * MOE_TILE
    xs_sorted, ready = _sc_dispatch([h2 for h2, _, _ in streams], pos[0], pos[1], n_rows, after)
    return (xs_sorted, starts, total), pos, ready


def _gather_tokens(ys_sorted, pos, t0, n):
    return _sc_gather(ys_sorted, pos[:, t0:t0 + n].reshape(2 * n)).reshape(2, n, ys_sorted.shape[-1])


def _one_layer(xp, xs, s_ret, c_pool, norm1_g, w_in, ret_norm_g, w_pool, pool_scale, w_out, norm2_g,
               w_rg, w_re, w_g, w_u, w_d, final_g, past_len):
    bp, seq, d_model = xp.shape
    bs, dseq, _ = xs.shape
    rw_width = ret_norm_g.shape[-1]
    pw = pool_scale.shape[-1]
    dh = rw_width // RET_HEADS
    half = d_model // 2

    w_r = jnp.concatenate(
        [w_re.T, w_rg.T, jnp.zeros((ROUTER_ROWS - N_EXPERTS - N_EXPERT_GROUPS, d_model), F32)], axis=0)
    wr = w_r.astype(BF16)
    consts = dict(
        g1=norm1_g.reshape(1, d_model), w_in=w_in.astype(BF16), gret=ret_norm_g.reshape(1, rw_width),
        w_pool=w_pool.astype(BF16), pscale=pool_scale.reshape(1, pw), w_out=w_out.astype(BF16),
        g2=norm2_g.reshape(1, d_model), wr=wr)

    gf = final_g.reshape(1, d_model)

    ts = bs * dseq
    b_lead = bp - 1
    t_lead, t_rest = b_lead * seq, (bp - b_lead) * seq
    zeros = lambda nb: (jnp.zeros((nb, RET_HEADS, dh, dh), F32), jnp.zeros((nb, HIST_ROWS, pw), F32))
    h0s = jnp.pad(c_pool, ((0, 0), (HIST_ROWS - POOL_HIST, 0), (0, 0)))
    prompt_tile = dict(bb=1, tl=PROMPT_TILE, chunk=256)
    sample_tile = dict(bb=bs, tl=dseq, chunk=min(64, dseq))
    prompt_tables = _layer_tables(0, seq, dh, **prompt_tile)
    sample_tables = _layer_tables(past_len, dseq, dh, **sample_tile)

    def stream(layer_out, t):
        x1, h2, ri, rw, st, hist, cnt = layer_out
        return dict(x1=x1.reshape(t, d_model), route=(h2.reshape(t, half), ri, cnt),
                    rw=rw.reshape(-1, EXPERTS_PER_GROUP, rw.shape[-1]), st=st, hist=hist)

    pa = stream(_layer_call(xp, 0, b_lead, *zeros(b_lead), 0, consts, prompt_tables, gf, **prompt_tile), t_lead)
    group0, pos0, ready0 = _route([pa["route"]])
    pb = stream(_layer_call(xp, b_lead, bp - b_lead, *zeros(bp - b_lead), 0, consts, prompt_tables, ready0,
                            **prompt_tile), t_rest)
    sm = stream(_layer_call(xs, 0, bs, s_ret, h0s, past_len, consts, sample_tables, pb["route"][2], **sample_tile),
                ts)
    group1, pos1, _ = _route([pb["route"], sm["route"]], after=group0[0])
    ys0, ys1 = _moe_call([group0, group1], w_g, w_u, w_d)

    tp = bp * seq
    yp = _combine_call(pb["x1"], pb["rw"], 0, t_rest, _gather_tokens(ys1, pos1, 0, t_rest), gf, tp, t_lead)
    ysm = _combine_call(sm["x1"], sm["rw"], 0, ts, _gather_tokens(ys1, pos1, t_rest, ts), gf, ts, 0)
    row0 = 0
    for nb in COMBINE_CHUNKS:
        n = min(nb * seq, t_lead - row0)
        if n > 0:
            yp = _combine_call(pa["x1"], pa["rw"], row0, n, _gather_tokens(ys0, pos0, row0, n), gf, tp, row0,
                               prev_out=yp)
            row0 += n
    assert row0 == t_lead
    st_p = jnp.concatenate([pa["st"], pb["st"]], axis=0)
    hist_p = jnp.concatenate([pa["hist"], pb["hist"]], axis=0)
    return (yp.reshape(bp, seq, d_model), ysm.reshape(bs, dseq, d_model),
            st_p, hist_p[:, HIST_ROWS - POOL_HIST:], sm["st"], sm["hist"][:, HIST_ROWS - POOL_HIST:])


def kernel(x_prompt, x_sample, state_ret, cache_pool, norm1_g, w_in, ret_norm_g, w_pool, pool_scale, w_out,
           norm2_g, w_router_group, w_router_expert, w_exp_gate, w_exp_up, w_exp_down, final_norm_g):
    depth = w_in.shape[0]
    assert depth == 1, "the final RMSNorm is fused into the layer's combine kernel"
    assert x_prompt.shape[0] >= 2 and x_prompt.shape[1] % PROMPT_TILE == 0
    yp, ys, s_p, h_p, s_s, h_s = _one_layer(
        x_prompt, x_sample, state_ret[0], cache_pool[0], norm1_g[0], w_in[0], ret_norm_g[0], w_pool[0],
        pool_scale[0], w_out[0], norm2_g[0], w_router_group[0], w_router_expert[0],
        w_exp_gate[0], w_exp_up[0], w_exp_down[0], final_norm_g, PAST_LEN)
    return (yp, ys, s_p[None], h_p[None], s_s[None], h_s[None])
```

```python
import functools

import jax
import jax.numpy as jnp
from jax import lax
from jax.experimental import pallas as pl
from jax.experimental.pallas import tpu as pltpu
from jax.experimental.pallas import tpu_sc as plsc

F32 = jnp.float32
BF16 = jnp.bfloat16
I32 = jnp.int32
U32 = jnp.uint32

EPS = 1e-6
ROPE_BASE = 10000.0
RET_HEADS = 4
POOL_WINDOWS = (2, 4, 8, 16)
POOL_HIST = max(POOL_WINDOWS) - 1
N_EXPERT_GROUPS = 4
EXPERTS_PER_GROUP = 8
N_EXPERTS = N_EXPERT_GROUPS * EXPERTS_PER_GROUP
ROUTER_ROWS = 48
PAST_LEN = 1024

LANES = 128
SUBLANES = 8
HIST_ROWS = 16
MOE_TILE = 256
MOE_BUFFERS = 16
MOE_UNROLL = 2
MOE_LOOKAHEAD = MOE_BUFFERS - MOE_UNROLL
PREP_ROWS = 512
PROMPT_TILE = 1024
COMBINE_CHUNKS = (1, 2)
SC_UNIT = 32
VMEM_LIMIT = 56 * 1024 * 1024


def _nbytes(*arrays):
    return sum(a.size * a.dtype.itemsize for a in arrays)


def _rms(x, g):
    return x * lax.rsqrt(jnp.mean(x * x, axis=-1, keepdims=True) + EPS) * g


def _sigmoid(x):
    return 1.0 / (1.0 + jnp.exp(-x))


def _pack_bf16_pair(lo, hi):
    lo_b = lax.bitcast_convert_type(lo.astype(BF16).astype(F32), U32)
    hi_b = lax.bitcast_convert_type(hi.astype(BF16).astype(F32), U32)
    return hi_b | (lo_b >> 16)


def _unpack_bf16_pair(p):
    lo = lax.bitcast_convert_type(p << 16, F32)
    hi = lax.bitcast_convert_type(p & jnp.uint32(0xFFFF0000), F32)
    return lo, hi


def _layer_kernel(dc_ref, x_ref, s0_ref, h0_ref, rb_ref, rc_ref, rs_ref, rcs_ref, rss_ref,
                  dintra_ref, dq_ref, dk_ref,
                  g1_ref, win_ref, gret_ref, wpool_ref, pscale_ref, wout_ref, g2_ref,
                  wr_ref, tri_ref, after_ref,
                  x1_ref, h2_ref, ri_ref, rw_ref, st_ref, hist_ref, cnt_ref,
                  ue_ref, q_ref, k_ref, v_ref, gate_ref, a_ref,
                  *, bb, tl, chunk, pos0):
    b_idx = pl.program_id(0)
    l_idx = pl.program_id(1)
    rows = bb * tl
    d_model = x_ref.shape[-1]
    rw_width = q_ref.shape[-1]
    dh = rw_width // RET_HEADS
    pw = ue_ref.shape[-1]
    gw = pw // len(POOL_WINDOWS)
    n_chunks = tl // chunk

    @pl.when(l_idx == 0)
    def _():
        st_ref[...] = s0_ref[...]
        ue_ref[:, 0:HIST_ROWS, :] = h0_ref[...]

    @pl.when((l_idx == 0) & (b_idx == 0))
    def _():
        cnt_ref[...] = jnp.zeros_like(cnt_ref)

    cos_b = rb_ref[0, 0:1, :]
    sin_b = rb_ref[0, 1:2, :]
    cosf = cos_b * rc_ref[...] - sin_b * rs_ref[...]
    sinf = sin_b * rcs_ref[...] + cos_b * rss_ref[...]
    k_scale = dh ** -0.5
    n_blocks = max(1, tl // PREP_ROWS) if bb == 1 else 1
    block = rows // n_blocks
    for blk in range(n_blocks):
        rs = slice(blk * block, (blk + 1) * block)
        xb = x_ref[0, rs, :] if bb == 1 else x_ref[...].reshape(rows, d_model)
        hb = _rms(xb, g1_ref[...]).astype(BF16)
        proj = jnp.dot(hb, win_ref[...], preferred_element_type=F32)

        def rotate(a):
            if bb == 1:
                return a * cosf[rs] + pltpu.roll(a, dh // 2, 1) * sinf[rs]
            return (a.reshape(bb, tl, dh) * cosf[None]
                    + pltpu.roll(a, dh // 2, 1).reshape(bb, tl, dh) * sinf[None]).reshape(rows, dh)

        for hh in range(RET_HEADS):
            cs = slice(hh * dh, (hh + 1) * dh)
            q_ref[rs, cs] = rotate(proj[:, hh * dh:(hh + 1) * dh]).astype(BF16)
            k_ref[rs, cs] = rotate(proj[:, rw_width + hh * dh:rw_width + (hh + 1) * dh]) * k_scale
        v_ref[rs, :] = proj[:, 2 * rw_width:3 * rw_width].astype(BF16)
        gate_ref[rs, :] = proj[:, 3 * rw_width:4 * rw_width]
        u = proj[:, 4 * rw_width:4 * rw_width + pw]
        if bb == 1:
            ue_ref[0, HIST_ROWS + blk * block:HIST_ROWS + (blk + 1) * block, :] = u
        else:
            ue_ref[:, HIST_ROWS:HIST_ROWS + tl, :] = u.reshape(bb, tl, pw)

    def ret_block(b, c):
        r0 = b * tl + c * chunk
        if not isinstance(r0, int):
            r0 = pl.multiple_of(r0, chunk)
        rr = pl.ds(r0, chunk)
        for hh in range(RET_HEADS):
            cs = slice(hh * dh, (hh + 1) * dh)
            qc = q_ref[rr, cs]
            kf = k_ref[rr, cs]
            vc = v_ref[rr, cs]
            s_old = st_ref[b, hh]
            sc = lax.dot_general(qc, kf.astype(BF16), (((1,), (1,)), ((), ())),
                                 preferred_element_type=F32) * dintra_ref[hh]
            o = (jnp.dot(sc.astype(BF16), vc, preferred_element_type=F32)
                 + dq_ref[hh] * jnp.dot(qc, s_old.astype(BF16), preferred_element_type=F32))
            kd = (kf * dk_ref[hh]).astype(BF16)
            s_new = dc_ref[hh] * s_old + lax.dot_general(
                kd, vc, (((0,), (0,)), ((), ())), preferred_element_type=F32)
            st_ref[b, hh] = s_new
            oc = o - jnp.mean(o, axis=-1, keepdims=True)
            var = jnp.mean(oc * oc, axis=-1, keepdims=True)
            y = oc * lax.rsqrt(var + EPS) * gret_ref[:, cs]
            g = gate_ref[rr, cs]
            a_ref[rr, cs] = (g * _sigmoid(g) * y).astype(BF16)

    if bb * n_chunks <= 4:
        for b in range(bb):
            for c in range(n_chunks):
                ret_block(b, c)
    else:
        def body(i, carry):
            ret_block(i // n_chunks, i % n_chunks)
            return carry
        lax.fori_loop(0, bb * n_chunks, body, 0)

    nt = (((1,), (1,)), ((), ()))
    neg = jnp.float32(-jnp.inf)
    big = jnp.float32(1e9)
    sub = lax.broadcasted_iota(I32, (EXPERTS_PER_GROUP, block), 0).astype(F32)
    eid = lax.broadcasted_iota(I32, (N_EXPERTS, block), 0).astype(F32)
    for blk in range(n_blocks):
        lo = blk * block
        rs = slice(lo, lo + block)

        if bb == 1:
            pos = pos0 + l_idx * tl + lo + lax.broadcasted_iota(I32, (block, 1), 0)
            window = lambda off, cs: ue_ref[0, HIST_ROWS + lo - off:HIST_ROWS + lo - off + block, cs]
        else:
            pos = pos0 + l_idx * tl + lax.broadcasted_iota(I32, (1, tl, 1), 1)
            window = lambda off, cs: ue_ref[:, HIST_ROWS - off:HIST_ROWS - off + tl, cs]
        for gi, w in enumerate(POOL_WINDOWS):
            cs = slice(gi * gw, (gi + 1) * gw)
            u_g = window(0, cs)
            acc = u_g
            for j in range(1, w):
                acc = acc + window(j, cs)
            inv_cnt = 1.0 / jnp.minimum(pos + 1, w).astype(F32)
            p = (acc * inv_cnt - u_g).reshape(block, gw)
            z = jnp.dot(p.astype(BF16), wpool_ref[gi], preferred_element_type=F32) * pscale_ref[:, cs]
            a_ref[rs, rw_width + gi * gw:rw_width + (gi + 1) * gw] = z.astype(BF16)

        xb = x_ref[0, rs, :] if bb == 1 else x_ref[...].reshape(rows, d_model)
        x1 = xb + jnp.dot(a_ref[rs, :], wout_ref[...], preferred_element_type=F32)
        h2 = _rms(x1, g2_ref[...])
        h2_packed = _pack_bf16_pair(h2[:, 0:d_model // 2], h2[:, d_model // 2:])
        if bb == 1:
            x1_ref[0, rs, :] = x1
            h2_ref[0, rs, :] = h2_packed
        else:
            x1_ref[...] = x1.reshape(bb, tl, d_model)
            h2_ref[...] = h2_packed.reshape(bb, tl, d_model // 2)

        lt = lax.dot_general(wr_ref[...], h2.astype(BF16), nt, preferred_element_type=F32)
        gl = jnp.where(sub < N_EXPERT_GROUPS, lt[N_EXPERTS:N_EXPERTS + EXPERTS_PER_GROUP], neg)
        gmax = jnp.max(gl, axis=0, keepdims=True)
        gidx = jnp.min(jnp.where(gl == gmax, sub, big), axis=0, keepdims=True)
        p_sel = 1.0 / jnp.sum(jnp.exp(gl - gmax), axis=0, keepdims=True)
        el = lt[0:EXPERTS_PER_GROUP]
        for g in range(1, N_EXPERT_GROUPS):
            el = jnp.where(gidx == g, lt[g * EXPERTS_PER_GROUP:(g + 1) * EXPERTS_PER_GROUP], el)
        m1 = jnp.max(el, axis=0, keepdims=True)
        t1 = jnp.min(jnp.where(el == m1, sub, big), axis=0, keepdims=True)
        el2 = jnp.where(sub == t1, neg, el)
        m2 = jnp.max(el2, axis=0, keepdims=True)
        t2 = jnp.min(jnp.where(el2 == m2, sub, big), axis=0, keepdims=True)
        e2 = jnp.exp(m2 - m1)
        w1 = p_sel / (1.0 + e2)
        w2 = p_sel * e2 / (1.0 + e2)
        i1 = gidx * EXPERTS_PER_GROUP + t1
        i2 = gidx * EXPERTS_PER_GROUP + t2

        hit1 = eid == i1
        hit2 = eid == i2
        onehot = (hit1 | hit2).astype(BF16)
        before = jnp.dot(onehot, tri_ref[...], preferred_element_type=F32) + cnt_ref[...]
        r1 = jnp.sum(jnp.where(hit1, before, 0.0), axis=0, keepdims=True)
        r2 = jnp.sum(jnp.where(hit2, before, 0.0), axis=0, keepdims=True)
        cnt_ref[...] = cnt_ref[...] + jnp.sum(onehot.astype(F32), axis=1, keepdims=True)

        ri = jnp.where(sub == 0, i1, jnp.where(sub == 1, i2, jnp.where(sub == 2, r1, jnp.where(sub == 3, r2, 0.0))))
        ri_ref[0, 0, :, rs] = ri.astype(I32)
        rw_ref[0, 0, :, rs] = jnp.where(sub == 0, w1, jnp.where(sub == 1, w2, 0.0))

    tail = ue_ref[:, tl:tl + HIST_ROWS, :]
    ue_ref[:, 0:HIST_ROWS, :] = tail
    hist_ref[...] = tail


def _rope_tables(pos0, seq, tl, dh):
    half = dh // 2
    inv = ROPE_BASE ** (-jnp.arange(half, dtype=F32) / half)
    ang_t = jnp.arange(tl, dtype=F32)[:, None] * inv[None, :]
    ang_b = (pos0 + tl * jnp.arange(seq // tl)).astype(F32)[:, None] * inv[None, :]
    dup = lambda a: jnp.concatenate([a, a], axis=-1)
    sgn = lambda a: jnp.concatenate([-a, a], axis=-1)
    base = jnp.stack([dup(jnp.cos(ang_b)), dup(jnp.sin(ang_b))], axis=1)
    base = jnp.pad(base, ((0, 0), (0, SUBLANES - base.shape[1]), (0, 0)))
    cos_t, sin_t = jnp.cos(ang_t), jnp.sin(ang_t)
    return base, dup(cos_t), dup(sin_t), sgn(cos_t), sgn(sin_t)


def _block_rows(bb, tl):
    return bb * tl // (max(1, tl // PREP_ROWS) if bb == 1 else 1)


def _layer_tables(pos0, seq, dh, *, bb, tl, chunk):
    lg = jnp.log1p(-jnp.exp2(-5.0 - jnp.arange(RET_HEADS, dtype=F32)))
    idx = jnp.arange(chunk, dtype=F32)
    diff = idx[:, None] - idx[None, :]
    d_intra = jnp.where(diff[None] >= 0, jnp.exp(jnp.maximum(diff, 0.0)[None] * lg[:, None, None]), 0.0)
    d_q = jnp.broadcast_to(jnp.exp((idx + 1.0)[None, :] * lg[:, None])[:, :, None], (RET_HEADS, chunk, dh))
    d_k = jnp.broadcast_to(jnp.exp((chunk - 1.0 - idx)[None, :] * lg[:, None])[:, :, None], (RET_HEADS, chunk, dh))
    d_c = jnp.exp(chunk * lg)
    block = _block_rows(bb, tl)
    tri = jnp.triu(jnp.ones((block, block), BF16), 1)
    return dict(rope=_rope_tables(pos0, seq, tl, dh), d_intra=d_intra, d_q=d_q, d_k=d_k, d_c=d_c, tri=tri)


def _layer_call(x, b0, nb, s0, h0, pos0, consts, tables, after, *, bb, tl, chunk):
    _, seq, d_model = x.shape
    bsz = nb
    blk0 = b0 // bb
    rows = bb * tl
    rw_width = consts["gret"].shape[-1]
    pw = consts["pscale"].shape[-1]
    dh = rw_width // RET_HEADS
    block = _block_rows(bb, tl)
    rope, d_intra, d_q, d_k, d_c, tri = (tables[k] for k in ("rope", "d_intra", "d_q", "d_k", "d_c", "tri"))

    const2 = lambda b, l, *_: (0, 0)
    const3 = lambda b, l, *_: (0, 0, 0)
    grid_spec = pltpu.PrefetchScalarGridSpec(
        num_scalar_prefetch=0,
        grid=(bsz // bb, seq // tl),
        in_specs=[
            pl.BlockSpec(memory_space=pltpu.SMEM),
            pl.BlockSpec((bb, tl, d_model), lambda b, l: (blk0 + b, l, 0)),
            pl.BlockSpec((bb, RET_HEADS, dh, dh), lambda b, l: (b, 0, 0, 0)),
            pl.BlockSpec((bb, HIST_ROWS, pw), lambda b, l: (b, 0, 0)),
            pl.BlockSpec((1, SUBLANES, dh), lambda b, l: (l, 0, 0)),
            pl.BlockSpec((tl, dh), const2),
            pl.BlockSpec((tl, dh), const2),
            pl.BlockSpec((tl, dh), const2),
            pl.BlockSpec((tl, dh), const2),
            pl.BlockSpec((RET_HEADS, chunk, chunk), const3),
            pl.BlockSpec((RET_HEADS, chunk, dh), const3),
            pl.BlockSpec((RET_HEADS, chunk, dh), const3),
            pl.BlockSpec((1, d_model), const2),
            pl.BlockSpec(consts["w_in"].shape, const2),
            pl.BlockSpec((1, rw_width), const2),
            pl.BlockSpec(consts["w_pool"].shape, const3),
            pl.BlockSpec((1, pw), const2),
            pl.BlockSpec(consts["w_out"].shape, const2),
            pl.BlockSpec((1, d_model), const2),
            pl.BlockSpec((ROUTER_ROWS, d_model), const2),
            pl.BlockSpec((block, block), const2),
            pl.BlockSpec(memory_space=pl.ANY),
        ],
        out_specs=[
            pl.BlockSpec((bb, tl, d_model), lambda b, l: (b, l, 0)),
            pl.BlockSpec((bb, tl, d_model // 2), lambda b, l: (b, l, 0)),
            pl.BlockSpec((1, 1, EXPERTS_PER_GROUP, rows), lambda b, l: (b, l, 0, 0)),
            pl.BlockSpec((1, 1, EXPERTS_PER_GROUP, rows), lambda b, l: (b, l, 0, 0)),
            pl.BlockSpec((bb, RET_HEADS, dh, dh), lambda b, l: (b, 0, 0, 0)),
            pl.BlockSpec((bb, HIST_ROWS, pw), lambda b, l: (b, 0, 0)),
            pl.BlockSpec((N_EXPERTS, block), const2),
        ],
        scratch_shapes=[
            pltpu.VMEM((bb, HIST_ROWS + tl, pw), F32),
            pltpu.VMEM((rows, rw_width), BF16),
            pltpu.VMEM((rows, rw_width), F32),
            pltpu.VMEM((rows, rw_width), BF16),
            pltpu.VMEM((rows, rw_width), F32),
            pltpu.VMEM((rows, d_model), BF16),
        ],
    )
    out_shape = [
        jax.ShapeDtypeStruct((bsz, seq, d_model), F32),
        jax.ShapeDtypeStruct((bsz, seq, d_model // 2), U32),
        jax.ShapeDtypeStruct((bsz // bb, seq // tl, EXPERTS_PER_GROUP, rows), I32),
        jax.ShapeDtypeStruct((bsz // bb, seq // tl, EXPERTS_PER_GROUP, rows), F32),
        jax.ShapeDtypeStruct((bsz, RET_HEADS, dh, dh), F32),
        jax.ShapeDtypeStruct((bsz, HIST_ROWS, pw), F32),
        jax.ShapeDtypeStruct((N_EXPERTS, block), F32),
    ]
    kern = functools.partial(_layer_kernel, bb=bb, tl=tl, chunk=chunk, pos0=pos0)
    operands = (d_c, x, s0, h0, *rope, d_intra, d_q, d_k,
                consts["g1"], consts["w_in"], consts["gret"], consts["w_pool"], consts["pscale"],
                consts["w_out"], consts["g2"], consts["wr"], tri, after)
    n_tok = bsz * seq
    mm_flops_per_token = 2 * (d_model * consts["w_in"].shape[1] + d_model * d_model + d_model * ROUTER_ROWS
                              + pw * pw // 4
                              + rw_width * (2 * chunk + 2 * dh) + N_EXPERTS * block)
    cost = pl.CostEstimate(
        flops=n_tok * mm_flops_per_token, transcendentals=n_tok * (rw_width + 2 * N_EXPERT_GROUPS),
        bytes_accessed=_nbytes(*operands) - _nbytes(x, after) + n_tok * d_model * 4 + _nbytes(*out_shape))
    return pl.pallas_call(
        kern, grid_spec=grid_spec, out_shape=out_shape, name=f"layer_pos{pos0}_b{b0}", cost_estimate=cost,
        compiler_params=pltpu.CompilerParams(
            dimension_semantics=("arbitrary", "arbitrary"), vmem_limit_bytes=VMEM_LIMIT),
    )(*operands)


def _sc_partition(n_units):
    info = plsc.get_sparse_core_info()
    nc, nw = info.num_cores, info.num_cores * info.num_subcores
    upw = -(-n_units // nw)
    upw += upw % 2
    return nc, nw, upw


def _units_by_worker(idx, n_units, upw, nw):
    idx = jnp.pad(idx.reshape(n_units, SC_UNIT), ((0, nw * upw - n_units), (0, 0)))
    return idx.reshape(upw, nw, SC_UNIT).transpose(1, 0, 2)


def _sc_dispatch(srcs, idx0, idx1, n_out_rows, after=None):
    assert 1 <= len(srcs) <= 2
    d = srcs[0].shape[1]
    dtype = srcs[0].dtype
    assert all(src.shape[0] % SC_UNIT == 0 for src in srcs)
    units_a = srcs[0].shape[0] // SC_UNIT
    n_units = sum(src.shape[0] for src in srcs) // SC_UNIT
    nc, nw, upw = _sc_partition(n_units)
    idx0 = _units_by_worker(idx0, n_units, upw, nw)
    idx1 = _units_by_worker(idx1, n_units, upw, nw)
    mesh = plsc.VectorSubcoreMesh(core_axis_name="c", subcore_axis_name="s")
    dma = pltpu.SemaphoreType.DMA
    extra = [] if after is None else [after]

    moved = n_units * SC_UNIT * d * jnp.dtype(dtype).itemsize
    @functools.partial(
        pl.kernel, mesh=mesh,
        cost_estimate=pl.CostEstimate(flops=0, transcendentals=0, bytes_accessed=3 * moved + _nbytes(idx0, idx1)),
        out_type=jax.ShapeDtypeStruct((n_out_rows, d), dtype),
        scratch_types=[
            pltpu.VMEM((upw, SC_UNIT), I32),
            pltpu.VMEM((upw, SC_UNIT), I32),
            pltpu.VMEM((SC_UNIT, d), dtype),
            pltpu.VMEM((SC_UNIT, d), dtype),
            dma, dma, dma, dma, dma, dma,
        ],
    )
    def k(*refs):
        src_hbm = refs[:len(srcs)]
        i0_hbm, i1_hbm, out_hbm, i0_v, i1_v, rows0, rows1, l0, l1, p0, p1, q0, q1 = refs[len(srcs) + len(extra):]
        wid = lax.axis_index("s") * nc + lax.axis_index("c")
        pltpu.sync_copy(i0_hbm.at[wid], i0_v)
        pltpu.sync_copy(i1_hbm.at[wid], i1_v)
        rows, lsem, psem, qsem = (rows0, rows1), (l0, l1), (p0, p1), (q0, q1)

        def live(j):
            return j * nw + wid < n_units

        def load(j, b, op):
            unit = j * nw + wid

            @pl.when(live(j) & (unit < units_a))
            def _():
                op(pltpu.make_async_copy(
                    src_hbm[0].at[pl.ds(pl.multiple_of(unit * SC_UNIT, 8), SC_UNIT)], rows[b], lsem[b]))

            if len(srcs) == 2:
                @pl.when(live(j) & (unit >= units_a))
                def _():
                    op(pltpu.make_async_copy(
                        src_hbm[1].at[pl.ds(pl.multiple_of((unit - units_a) * SC_UNIT, 8), SC_UNIT)],
                        rows[b], lsem[b]))

        def scatter(j, b, op):
            @pl.when(live(j))
            def _():
                op(pltpu.make_async_copy(rows[b], out_hbm.at[i0_v.at[j]], psem[b]))
                op(pltpu.make_async_copy(rows[b], out_hbm.at[i1_v.at[j]], qsem[b]))

        start = lambda c: c.start()
        wait = lambda c: c.wait()
        load(0, 0, start)

        @pl.loop(0, upw, step=2)
        def _(j):
            @pl.when(j > 0)
            def _():
                scatter(j - 1, 1, wait)
            load(j + 1, 1, start)
            load(j, 0, wait)
            scatter(j, 0, start)
            scatter(j, 0, wait)

            @pl.when(j + 2 < upw)
            def _():
                load(j + 2, 0, start)
            load(j + 1, 1, wait)
            scatter(j + 1, 1, start)

        scatter(upw - 1, 1, wait)

    return k(*srcs, *extra, idx0, idx1), idx1


def _sc_gather(table, idx):
    n = idx.shape[0]
    d = table.shape[1]
    assert n % SC_UNIT == 0
    n_units = n // SC_UNIT
    nc, nw, upw = _sc_partition(n_units)
    idx = _units_by_worker(idx, n_units, upw, nw)
    mesh = plsc.VectorSubcoreMesh(core_axis_name="c", subcore_axis_name="s")
    dma = pltpu.SemaphoreType.DMA

    @functools.partial(
        pl.kernel, mesh=mesh,
        cost_estimate=pl.CostEstimate(flops=0, transcendentals=0,
                                      bytes_accessed=2 * n * d * table.dtype.itemsize + _nbytes(idx)),
        out_type=jax.ShapeDtypeStruct((n, d), table.dtype),
        scratch_types=[
            pltpu.VMEM((upw, SC_UNIT), I32),
            pltpu.VMEM((SC_UNIT, d), table.dtype),
            pltpu.VMEM((SC_UNIT, d), table.dtype),
            dma, dma, dma, dma,
        ],
    )
    def k(t_hbm, i_hbm, out_hbm, i_v, rows0, rows1, g0, g1, w0, w1):
        wid = lax.axis_index("s") * nc + lax.axis_index("c")
        pltpu.sync_copy(i_hbm.at[wid], i_v)
        rows, gsem, wsem = (rows0, rows1), (g0, g1), (w0, w1)

        def live(j):
            return j * nw + wid < n_units

        def gather(j, b, op):
            @pl.when(live(j))
            def _():
                op(pltpu.make_async_copy(t_hbm.at[i_v.at[j]], rows[b], gsem[b]))

        def write(j, b, op):
            @pl.when(live(j))
            def _():
                op(pltpu.make_async_copy(
                    rows[b], out_hbm.at[pl.ds(pl.multiple_of((j * nw + wid) * SC_UNIT, 8), SC_UNIT)], wsem[b]))

        start = lambda c: c.start()
        wait = lambda c: c.wait()
        gather(0, 0, start)

        @pl.loop(0, upw, step=2)
        def _(j):
            @pl.when(j > 0)
            def _():
                write(j - 1, 1, wait)
            gather(j + 1, 1, start)
            gather(j, 0, wait)
            write(j, 0, start)
            write(j, 0, wait)

            @pl.when(j + 2 < upw)
            def _():
                gather(j + 2, 0, start)
            gather(j + 1, 1, wait)
            write(j + 1, 1, start)

        write(upw - 1, 1, wait)

    return k(table, idx)


def _moe_kernel(start0_ref, count0_ref, gtot0_ref, start1_ref, count1_ref, gtot1_ref,
                xs0_hbm, xs1_hbm, wg_ref, wu_ref, wd_ref, ys0_hbm, ys1_hbm,
                wgu_s, wd_s, xbuf0, ybuf0, xbuf1, ybuf1, sem_in0, sem_out0, sem_in1, sem_out1):
    e = pl.program_id(0)
    last = pl.num_programs(0) - 1
    hidden = wd_s.shape[0]
    half = xbuf0.shape[-1]
    segments = (
        (xs0_hbm, ys0_hbm, xbuf0, ybuf0, sem_in0, sem_out0, start0_ref[e], count0_ref[e], gtot0_ref[0]),
        (xs1_hbm, ys1_hbm, xbuf1, ybuf1, sem_in1, sem_out1, start1_ref[e], count1_ref[e], gtot1_ref[0]),
    )

    def rows_of(g):
        return pl.ds(pl.multiple_of(g * MOE_TILE, MOE_TILE), MOE_TILE)

    def pipeline(xs_hbm, ys_hbm, xbuf, ybuf, sem_in, sem_out):
        def copy_in(g):
            slot = g % MOE_BUFFERS
            return pltpu.make_async_copy(xs_hbm.at[rows_of(g)], xbuf.at[slot], sem_in.at[slot])

        def copy_out(g):
            slot = g % MOE_BUFFERS
            return pltpu.make_async_copy(ybuf.at[slot], ys_hbm.at[rows_of(g)], sem_out.at[slot])
        return copy_in, copy_out

    @pl.when(e == 0)
    def _():
        for xs_hbm, ys_hbm, xbuf, ybuf, sem_in, sem_out, _, _, g_total in segments:
            copy_in, _ = pipeline(xs_hbm, ys_hbm, xbuf, ybuf, sem_in, sem_out)
            for g in range(MOE_LOOKAHEAD):
                @pl.when(g < g_total)
                def _():
                    copy_in(g).start()

    @pl.when(segments[0][7] + segments[1][7] > 0)
    def _():
        wgu_s[:, 0:hidden] = wg_ref[0].astype(BF16)
        wgu_s[:, hidden:2 * hidden] = wu_ref[0].astype(BF16)
        wd_s[...] = wd_ref[0].astype(BF16)

    def expert_rows(xbuf, ybuf, slot, valid):
        row = lax.broadcasted_iota(I32, (MOE_TILE, half), 0)
        x_lo, x_hi = _unpack_bf16_pair(jnp.where(row < valid, xbuf[slot], jnp.uint32(0)))
        ab = (jnp.dot(x_lo.astype(BF16), wgu_s[0:half, :], preferred_element_type=F32)
              + jnp.dot(x_hi.astype(BF16), wgu_s[half:2 * half, :], preferred_element_type=F32))
        a = ab[:, 0:hidden]
        he = a * _sigmoid(a) * ab[:, hidden:2 * hidden]
        y = jnp.dot(he.astype(BF16), wd_s[...], preferred_element_type=F32)
        ybuf[slot] = _pack_bf16_pair(y[:, 0:half], y[:, half:2 * half])

    for xs_hbm, ys_hbm, xbuf, ybuf, sem_in, sem_out, start, count, g_total in segments:
        copy_in, copy_out = pipeline(xs_hbm, ys_hbm, xbuf, ybuf, sem_in, sem_out)
        g_first = start // MOE_TILE
        n_tiles = (count + MOE_TILE - 1) // MOE_TILE

        def tiles(t, width, copy_in=copy_in, copy_out=copy_out, xbuf=xbuf, ybuf=ybuf,
                  g_first=g_first, count=count, g_total=g_total):
            gs = [g_first + t + i for i in range(width)]
            for g in gs:
                @pl.when(g + MOE_LOOKAHEAD < g_total)
                def _():
                    copy_in(g + MOE_LOOKAHEAD).start()
            for g in gs:
                copy_in(g).wait()

                @pl.when(g >= MOE_BUFFERS)
                def _():
                    copy_out(g - MOE_BUFFERS).wait()
            for i, g in enumerate(gs):
                expert_rows(xbuf, ybuf, g % MOE_BUFFERS, count - (t + i) * MOE_TILE)
            for g in gs:
                copy_out(g).start()

        def pair(p, carry, tiles=tiles):
            tiles(MOE_UNROLL * p, MOE_UNROLL)
            return carry

        lax.fori_loop(0, n_tiles // MOE_UNROLL, pair, 0)

        def single(r, carry, tiles=tiles, n_tiles=n_tiles):
            tiles(n_tiles // MOE_UNROLL * MOE_UNROLL + r, 1)
            return carry

        lax.fori_loop(0, n_tiles % MOE_UNROLL, single, 0)

        @pl.when(e == last)
        def _(copy_out=copy_out, g_total=g_total):
            for j in range(1, MOE_BUFFERS + 1):
                @pl.when(g_total >= j)
                def _():
                    copy_out(g_total - j).wait()


def _moe_call(groups, w_g, w_u, w_d):
    (xs0, starts0, cnt0), (xs1, starts1, cnt1) = groups
    half = xs0.shape[1]
    n_experts, d_model, hidden = w_g.shape

    def tiles_total(starts, cnt):
        return ((starts[-1:] + cnt[-1:] + MOE_TILE - 1) // MOE_TILE).astype(I32)

    wspec = lambda shape: pl.BlockSpec(shape, lambda e, *_: (e, 0, 0))
    tile_bufs = [pltpu.VMEM((MOE_BUFFERS, MOE_TILE, half), U32)] * 4
    grid_spec = pltpu.PrefetchScalarGridSpec(
        num_scalar_prefetch=6,
        grid=(n_experts,),
        in_specs=[
            pl.BlockSpec(memory_space=pl.ANY),
            pl.BlockSpec(memory_space=pl.ANY),
            wspec((1, d_model, hidden)),
            wspec((1, d_model, hidden)),
            wspec((1, hidden, d_model)),
        ],
        out_specs=[pl.BlockSpec(memory_space=pl.ANY), pl.BlockSpec(memory_space=pl.ANY)],
        scratch_shapes=[
            pltpu.VMEM((d_model, 2 * hidden), BF16),
            pltpu.VMEM((hidden, d_model), BF16),
            *tile_bufs,
            *[pltpu.SemaphoreType.DMA((MOE_BUFFERS,))] * 4,
        ],
    )
    n_rows = xs0.shape[0] + xs1.shape[0]
    cost = pl.CostEstimate(flops=n_rows * 6 * d_model * hidden, transcendentals=n_rows * hidden,
                           bytes_accessed=2 * _nbytes(xs0, xs1) + _nbytes(w_g, w_u, w_d))
    return pl.pallas_call(
        _moe_kernel, grid_spec=grid_spec, cost_estimate=cost,
        out_shape=[jax.ShapeDtypeStruct(xs0.shape, U32), jax.ShapeDtypeStruct(xs1.shape, U32)], name="moe_experts",
        compiler_params=pltpu.CompilerParams(
            dimension_semantics=("arbitrary",), vmem_limit_bytes=VMEM_LIMIT),
    )(starts0, cnt0, tiles_total(starts0, cnt0), starts1, cnt1, tiles_total(starts1, cnt1),
      xs0, xs1, w_g, w_u, w_d)


def _combine_kernel(x1_ref, y0_ref, y1_ref, rw_ref, gf_ref, *rest):
    out_ref = rest[-1]
    tr = x1_ref.shape[0]
    w_rows = jnp.concatenate([rw_ref[0], jnp.zeros((LANES - rw_ref.shape[1], tr), F32)], axis=0)
    w_cols = w_rows.T
    w0, w1 = w_cols[:, 0:1], w_cols[:, 1:2]
    a_lo, a_hi = _unpack_bf16_pair(y0_ref[0])
    b_lo, b_hi = _unpack_bf16_pair(y1_ref[0])
    moe = jnp.concatenate([w0 * a_lo + w1 * b_lo, w0 * a_hi + w1 * b_hi], axis=-1)
    out_ref[...] = _rms(x1_ref[...] + moe, gf_ref[...])


def _combine_call(x1, rw, row0, n, yg, gf, out_rows, out_row0, prev_out=None):
    t, d_model = x1.shape
    tr = rw.shape[-1]
    half = yg.shape[-1]
    assert t % tr == 0 and row0 % tr == 0 and n % tr == 0 and rw.shape == (t // tr, EXPERTS_PER_GROUP, tr)
    assert yg.shape == (2, n, half) and out_row0 % tr == 0
    off = row0 // tr
    ooff = out_row0 // tr
    in_specs = [
        pl.BlockSpec((tr, d_model), lambda i: (off + i, 0)),
        pl.BlockSpec((1, tr, half), lambda i: (0, i, 0)),
        pl.BlockSpec((1, tr, half), lambda i: (1, i, 0)),
        pl.BlockSpec((1, EXPERTS_PER_GROUP, tr), lambda i: (off + i, 0, 0)),
        pl.BlockSpec((1, d_model), lambda i: (0, 0)),
    ]
    args = [x1, yg, yg, rw, gf]
    aliases = {}
    if prev_out is not None:
        in_specs.append(pl.BlockSpec(memory_space=pl.ANY))
        args.append(prev_out)
        aliases = {len(args) - 1: 0}
    return pl.pallas_call(
        _combine_kernel,
        grid=(n // tr,),
        in_specs=in_specs,
        out_specs=pl.BlockSpec((tr, d_model), lambda i: (ooff + i, 0)),
        out_shape=jax.ShapeDtypeStruct((out_rows, d_model), F32), name=f"combine_row{out_row0}_of{out_rows}",
        cost_estimate=pl.CostEstimate(flops=8 * n * d_model, transcendentals=n,
                                      bytes_accessed=2 * n * d_model * 4 + _nbytes(yg) + n * 4 * EXPERTS_PER_GROUP),
        input_output_aliases=aliases,
        compiler_params=pltpu.CompilerParams(
            dimension_semantics=("arbitrary",), vmem_limit_bytes=VMEM_LIMIT),
    )(*args)


def _route(streams, after=None):
    tokens = [h2.shape[0] for h2, _, _ in streams]
    counts = [cnt[:, 0].astype(I32) for _, _, cnt in streams]
    total = sum(counts)
    padded = ((total + MOE_TILE - 1) // MOE_TILE) * MOE_TILE
    starts = (jnp.cumsum(padded) - padded).astype(I32)
    experts = jnp.arange(N_EXPERTS, dtype=I32)[None, :, None]
    pos, base = [], starts
    for (_, ri, _), t, cnt in zip(streams, tokens, counts):
        ri = jnp.moveaxis(ri, 2, 0).reshape(ri.shape[2], t)
        first_row = jnp.sum(jnp.where(ri[0:2, None, :] == experts, base[None, :, None], 0), axis=1)
        pos.append(ri[2:4] + first_row)
        base = base + cnt
    pos = jnp.concatenate(pos, axis=1)
    n_rows = ((2 * sum(tokens) + N_EXPERTS * (MOE_TILE - 1)) // MOE_TILE) * MOE_TILE
    xs_sorted, ready = _sc_dispatch([h2 for h2, _, _ in streams], pos[0], pos[1], n_rows, after)
    return (xs_sorted, starts, total), pos, ready


def _gather_tokens(ys_sorted, pos, t0, n):
    return _sc_gather(ys_sorted, pos[:, t0:t0 + n].reshape(2 * n)).reshape(2, n, ys_sorted.shape[-1])


def _one_layer(xp, xs, s_ret, c_pool, norm1_g, w_in, ret_norm_g, w_pool, pool_scale, w_out, norm2_g,
               w_rg, w_re, w_g, w_u, w_d, final_g, past_len):
    bp, seq, d_model = xp.shape
    bs, dseq, _ = xs.shape
    rw_width = ret_norm_g.shape[-1]
    pw = pool_scale.shape[-1]
    dh = rw_width // RET_HEADS
    half = d_model // 2

    w_r = jnp.concatenate(
        [w_re.T, w_rg.T, jnp.zeros((ROUTER_ROWS - N_EXPERTS - N_EXPERT_GROUPS, d_model), F32)], axis=0)
    wr = w_r.astype(BF16)
    consts = dict(
        g1=norm1_g.reshape(1, d_model), w_in=w_in.astype(BF16), gret=ret_norm_g.reshape(1, rw_width),
        w_pool=w_pool.astype(BF16), pscale=pool_scale.reshape(1, pw), w_out=w_out.astype(BF16),
        g2=norm2_g.reshape(1, d_model), wr=wr)

    gf = final_g.reshape(1, d_model)

    ts = bs * dseq
    b_lead = bp - 1
    t_lead, t_rest = b_lead * seq, (bp - b_lead) * seq
    zeros = lambda nb: (jnp.zeros((nb, RET_HEADS, dh, dh), F32), jnp.zeros((nb, HIST_ROWS, pw), F32))
    h0s = jnp.pad(c_pool, ((0, 0), (HIST_ROWS - POOL_HIST, 0), (0, 0)))
    prompt_tile = dict(bb=1, tl=PROMPT_TILE, chunk=256)
    sample_tile = dict(bb=bs, tl=dseq, chunk=min(64, dseq))
    prompt_tables = _layer_tables(0, seq, dh, **prompt_tile)
    sample_tables = _layer_tables(past_len, dseq, dh, **sample_tile)

    def stream(layer_out, t):
        x1, h2, ri, rw, st, hist, cnt = layer_out
        return dict(x1=x1.reshape(t, d_model), route=(h2.reshape(t, half), ri, cnt),
                    rw=rw.reshape(-1, EXPERTS_PER_GROUP, rw.shape[-1]), st=st, hist=hist)

    pa = stream(_layer_call(xp, 0, b_lead, *zeros(b_lead), 0, consts, prompt_tables, gf, **prompt_tile), t_lead)
    group0, pos0, ready0 = _route([pa["route"]])
    pb = stream(_layer_call(xp, b_lead, bp - b_lead, *zeros(bp - b_lead), 0, consts, prompt_tables, ready0,
                            **prompt_tile), t_rest)
    sm = stream(_layer_call(xs, 0, bs, s_ret, h0s, past_len, consts, sample_tables, pb["route"][2], **sample_tile),
                ts)
    group1, pos1, _ = _route([pb["route"], sm["route"]], after=group0[0])
    ys0, ys1 = _moe_call([group0, group1], w_g, w_u, w_d)

    tp = bp * seq
    yp = _combine_call(pb["x1"], pb["rw"], 0, t_rest, _gather_tokens(ys1, pos1, 0, t_rest), gf, tp, t_lead)
    ysm = _combine_call(sm["x1"], sm["rw"], 0, ts, _gather_tokens(ys1, pos1, t_rest, ts), gf, ts, 0)
    row0 = 0
    for nb in COMBINE_CHUNKS:
        n = min(nb * seq, t_lead - row0)
        if n > 0:
            yp = _combine_call(pa["x1"], pa["rw"], row0, n, _gather_tokens(ys0, pos0, row0, n), gf, tp, row0,
                               prev_out=yp)
            row0 += n
    assert row0 == t_lead
    st_p = jnp.concatenate([pa["st"], pb["st"]], axis=0)
    hist_p = jnp.concatenate([pa["hist"], pb["hist"]], axis=0)
    return (yp.reshape(bp, seq, d_model), ysm.reshape(bs, dseq, d_model),
            st_p, hist_p[:, HIST_ROWS - POOL_HIST:], sm["st"], sm["hist"][:, HIST_ROWS - POOL_HIST:])


def kernel(x_prompt, x_sample, state_ret, cache_pool, norm1_g, w_in, ret_norm_g, w_pool, pool_scale, w_out,
           norm2_g, w_router_group, w_router_expert, w_exp_gate, w_exp_up, w_exp_down, final_norm_g):
    depth = w_in.shape[0]
    assert depth == 1, "the final RMSNorm is fused into the layer's combine kernel"
    assert x_prompt.shape[0] >= 2 and x_prompt.shape[1] % PROMPT_TILE == 0
    yp, ys, s_p, h_p, s_s, h_s = _one_layer(
        x_prompt, x_sample, state_ret[0], cache_pool[0], norm1_g[0], w_in[0], ret_norm_g[0], w_pool[0],
        pool_scale[0], w_out[0], norm2_g[0], w_router_group[0], w_router_expert[0],
        w_exp_gate[0], w_exp_up[0], w_exp_down[0], final_norm_g, PAST_LEN)
    return (yp, ys, s_p[None], h_p[None], s_s[None], h_s[None])
```

```python
import functools

import jax
import jax.numpy as jnp
from jax import lax
from jax.experimental import pallas as pl
from jax.experimental.pallas import tpu as pltpu
from jax.experimental.pallas import tpu_sc as plsc

F32 = jnp.float32
BF16 = jnp.bfloat16
I32 = jnp.int32
U32 = jnp.uint32

EPS = 1e-6
ROPE_BASE = 10000.0
RET_HEADS = 4
POOL_WINDOWS = (2, 4, 8, 16)
POOL_HIST = max(POOL_WINDOWS) - 1
N_EXPERT_GROUPS = 4
EXPERTS_PER_GROUP = 8
N_EXPERTS = N_EXPERT_GROUPS * EXPERTS_PER_GROUP
ROUTER_ROWS = 48
PAST_LEN = 1024

LANES = 128
SUBLANES = 8
HIST_ROWS = 16
MOE_TILE = 256
MOE_BUFFERS = 16
MOE_UNROLL = 2
MOE_LOOKAHEAD = MOE_BUFFERS - MOE_UNROLL
RET_CHUNK = 256
PREP_ROWS = 512
PROMPT_TILE = 1024
REST_CHUNK_ROWS = (2048, 1 << 30)
LEAD_CHUNK_ROWS = (8192, 1 << 30)
SC_UNIT = 32
V7X_VMEM_BYTES = 64 * 1024 * 1024
VMEM_LIMIT = V7X_VMEM_BYTES * 7 // 8


def _nbytes(*arrays):
    return sum(a.size * a.dtype.itemsize for a in arrays)


def _rms(x, g):
    return x * lax.rsqrt(jnp.mean(x * x, axis=-1, keepdims=True) + EPS) * g


def _sigmoid(x):
    return 1.0 / (1.0 + jnp.exp(-x))


def _pack_bf16_pair(lo, hi):
    lo_b = lax.bitcast_convert_type(lo.astype(BF16).astype(F32), U32)
    hi_b = lax.bitcast_convert_type(hi.astype(BF16).astype(F32), U32)
    return hi_b | (lo_b >> 16)


def _unpack_bf16_pair(p):
    lo = lax.bitcast_convert_type(p << 16, F32)
    hi = lax.bitcast_convert_type(p & jnp.uint32(0xFFFF0000), F32)
    return lo, hi


def _layer_kernel(dc_ref, x_ref, s0_ref, h0_ref, rb_ref, rc_ref, rs_ref, rcs_ref, rss_ref,
                  dintra_ref, dq_ref, dk_ref,
                  g1_ref, win_ref, gret_ref, wpool_ref, pscale_ref, wout_ref, g2_ref,
                  wr_ref, tri_ref, after_ref,
                  x1_ref, h2_ref, ri_ref, rw_ref, st_ref, hist_ref, cnt_ref,
                  ue_ref, q_ref, k_ref, v_ref, gate_ref, a_ref,
                  *, bb, tl, chunk, pos0):
    b_idx = pl.program_id(0)
    l_idx = pl.program_id(1)
    rows = bb * tl
    d_model = x_ref.shape[-1]
    rw_width = q_ref.shape[-1]
    dh = rw_width // RET_HEADS
    pw = ue_ref.shape[-1]
    gw = pw // len(POOL_WINDOWS)
    n_chunks = tl // chunk

    @pl.when(l_idx == 0)
    def _():
        st_ref[...] = s0_ref[...]
        ue_ref[:, 0:HIST_ROWS, :] = h0_ref[...]

    @pl.when((l_idx == 0) & (b_idx == 0))
    def _():
        cnt_ref[...] = jnp.zeros_like(cnt_ref)

    cos_b = rb_ref[0, 0:1, :]
    sin_b = rb_ref[0, 1:2, :]
    cosf = cos_b * rc_ref[...] - sin_b * rs_ref[...]
    sinf = sin_b * rcs_ref[...] + cos_b * rss_ref[...]
    k_scale = dh ** -0.5
    n_blocks = max(1, tl // PREP_ROWS) if bb == 1 else 1
    block = rows // n_blocks
    for blk in range(n_blocks):
        rs = slice(blk * block, (blk + 1) * block)
        xb = x_ref[0, rs, :] if bb == 1 else x_ref[...].reshape(rows, d_model)
        hb = _rms(xb, g1_ref[...]).astype(BF16)
        proj = jnp.dot(hb, win_ref[...], preferred_element_type=F32)

        def rotate(a):
            if bb == 1:
                return a * cosf[rs] + pltpu.roll(a, dh // 2, 1) * sinf[rs]
            return (a.reshape(bb, tl, dh) * cosf[None]
                    + pltpu.roll(a, dh // 2, 1).reshape(bb, tl, dh) * sinf[None]).reshape(rows, dh)

        for hh in range(RET_HEADS):
            cs = slice(hh * dh, (hh + 1) * dh)
            q_ref[rs, cs] = rotate(proj[:, hh * dh:(hh + 1) * dh]).astype(BF16)
            k_ref[rs, cs] = rotate(proj[:, rw_width + hh * dh:rw_width + (hh + 1) * dh]) * k_scale
        v_ref[rs, :] = proj[:, 2 * rw_width:3 * rw_width].astype(BF16)
        gate_ref[rs, :] = proj[:, 3 * rw_width:4 * rw_width]
        u = proj[:, 4 * rw_width:4 * rw_width + pw]
        if bb == 1:
            ue_ref[0, HIST_ROWS + blk * block:HIST_ROWS + (blk + 1) * block, :] = u
        else:
            ue_ref[:, HIST_ROWS:HIST_ROWS + tl, :] = u.reshape(bb, tl, pw)

    def ret_block(b, c):
        r0 = b * tl + c * chunk
        if not isinstance(r0, int):
            r0 = pl.multiple_of(r0, chunk)
        rr = pl.ds(r0, chunk)
        for hh in range(RET_HEADS):
            cs = slice(hh * dh, (hh + 1) * dh)
            qc = q_ref[rr, cs]
            kf = k_ref[rr, cs]
            vc = v_ref[rr, cs]
            s_old = st_ref[b, hh]
            sc = lax.dot_general(qc, kf.astype(BF16), (((1,), (1,)), ((), ())),
                                 preferred_element_type=F32) * dintra_ref[hh]
            o = (jnp.dot(sc.astype(BF16), vc, preferred_element_type=F32)
                 + dq_ref[hh] * jnp.dot(qc, s_old.astype(BF16), preferred_element_type=F32))
            kd = (kf * dk_ref[hh]).astype(BF16)
            s_new = dc_ref[hh] * s_old + lax.dot_general(
                kd, vc, (((0,), (0,)), ((), ())), preferred_element_type=F32)
            st_ref[b, hh] = s_new
            oc = o - jnp.mean(o, axis=-1, keepdims=True)
            var = jnp.mean(oc * oc, axis=-1, keepdims=True)
            y = oc * lax.rsqrt(var + EPS) * gret_ref[:, cs]
            g = gate_ref[rr, cs]
            a_ref[rr, cs] = (g * _sigmoid(g) * y).astype(BF16)

    if bb * n_chunks <= 4:
        for b in range(bb):
            for c in range(n_chunks):
                ret_block(b, c)
    else:
        def body(i, carry):
            ret_block(i // n_chunks, i % n_chunks)
            return carry
        lax.fori_loop(0, bb * n_chunks, body, 0)

    nt = (((1,), (1,)), ((), ()))
    neg = jnp.float32(-jnp.inf)
    big = jnp.float32(1e9)
    sub = lax.broadcasted_iota(I32, (EXPERTS_PER_GROUP, block), 0).astype(F32)
    eid = lax.broadcasted_iota(I32, (N_EXPERTS, block), 0).astype(F32)
    for blk in range(n_blocks):
        lo = blk * block
        rs = slice(lo, lo + block)

        if bb == 1:
            pos = pos0 + l_idx * tl + lo + lax.broadcasted_iota(I32, (block, 1), 0)
            window = lambda off, cs: ue_ref[0, HIST_ROWS + lo - off:HIST_ROWS + lo - off + block, cs]
        else:
            pos = pos0 + l_idx * tl + lax.broadcasted_iota(I32, (1, tl, 1), 1)
            window = lambda off, cs: ue_ref[:, HIST_ROWS - off:HIST_ROWS - off + tl, cs]
        for gi, w in enumerate(POOL_WINDOWS):
            cs = slice(gi * gw, (gi + 1) * gw)
            u_g = window(0, cs)
            acc = u_g
            for j in range(1, w):
                acc = acc + window(j, cs)
            inv_cnt = 1.0 / jnp.minimum(pos + 1, w).astype(F32)
            p = (acc * inv_cnt - u_g).reshape(block, gw)
            z = jnp.dot(p.astype(BF16), wpool_ref[gi], preferred_element_type=F32) * pscale_ref[:, cs]
            a_ref[rs, rw_width + gi * gw:rw_width + (gi + 1) * gw] = z.astype(BF16)

        xb = x_ref[0, rs, :] if bb == 1 else x_ref[...].reshape(rows, d_model)
        x1 = xb + jnp.dot(a_ref[rs, :], wout_ref[...], preferred_element_type=F32)
        h2 = _rms(x1, g2_ref[...])
        h2_packed = _pack_bf16_pair(h2[:, 0:d_model // 2], h2[:, d_model // 2:])
        if bb == 1:
            x1_ref[0, rs, :] = x1
            h2_ref[0, rs, :] = h2_packed
        else:
            x1_ref[...] = x1.reshape(bb, tl, d_model)
            h2_ref[...] = h2_packed.reshape(bb, tl, d_model // 2)

        lt = lax.dot_general(wr_ref[...], h2.astype(BF16), nt, preferred_element_type=F32)
        gl = jnp.where(sub < N_EXPERT_GROUPS, lt[N_EXPERTS:N_EXPERTS + EXPERTS_PER_GROUP], neg)
        gmax = jnp.max(gl, axis=0, keepdims=True)
        gidx = jnp.min(jnp.where(gl == gmax, sub, big), axis=0, keepdims=True)
        p_sel = 1.0 / jnp.sum(jnp.exp(gl - gmax), axis=0, keepdims=True)
        el = lt[0:EXPERTS_PER_GROUP]
        for g in range(1, N_EXPERT_GROUPS):
            el = jnp.where(gidx == g, lt[g * EXPERTS_PER_GROUP:(g + 1) * EXPERTS_PER_GROUP], el)
        m1 = jnp.max(el, axis=0, keepdims=True)
        t1 = jnp.min(jnp.where(el == m1, sub, big), axis=0, keepdims=True)
        el2 = jnp.where(sub == t1, neg, el)
        m2 = jnp.max(el2, axis=0, keepdims=True)
        t2 = jnp.min(jnp.where(el2 == m2, sub, big), axis=0, keepdims=True)
        e2 = jnp.exp(m2 - m1)
        w1 = p_sel / (1.0 + e2)
        w2 = p_sel * e2 / (1.0 + e2)
        i1 = gidx * EXPERTS_PER_GROUP + t1
        i2 = gidx * EXPERTS_PER_GROUP + t2

        hit1 = eid == i1
        hit2 = eid == i2
        onehot = (hit1 | hit2).astype(BF16)
        before = jnp.dot(onehot, tri_ref[...], preferred_element_type=F32) + cnt_ref[...]
        r1 = jnp.sum(jnp.where(hit1, before, 0.0), axis=0, keepdims=True)
        r2 = jnp.sum(jnp.where(hit2, before, 0.0), axis=0, keepdims=True)
        cnt_ref[...] = cnt_ref[...] + jnp.sum(onehot.astype(F32), axis=1, keepdims=True)

        ri = jnp.where(sub == 0, i1, jnp.where(sub == 1, i2, jnp.where(sub == 2, r1, jnp.where(sub == 3, r2, 0.0))))
        ri_ref[0, 0, :, rs] = ri.astype(I32)
        rw_ref[0, 0, :, rs] = jnp.where(sub == 0, w1, jnp.where(sub == 1, w2, 0.0))

    tail = ue_ref[:, tl:tl + HIST_ROWS, :]
    ue_ref[:, 0:HIST_ROWS, :] = tail
    hist_ref[...] = tail


def _rope_tables(pos0, seq, tl, dh):
    half = dh // 2
    inv = ROPE_BASE ** (-jnp.arange(half, dtype=F32) / half)
    ang_t = jnp.arange(tl, dtype=F32)[:, None] * inv[None, :]
    ang_b = (pos0 + tl * jnp.arange(seq // tl)).astype(F32)[:, None] * inv[None, :]
    dup = lambda a: jnp.concatenate([a, a], axis=-1)
    sgn = lambda a: jnp.concatenate([-a, a], axis=-1)
    base = jnp.stack([dup(jnp.cos(ang_b)), dup(jnp.sin(ang_b))], axis=1)
    base = jnp.pad(base, ((0, 0), (0, SUBLANES - base.shape[1]), (0, 0)))
    cos_t, sin_t = jnp.cos(ang_t), jnp.sin(ang_t)
    return base, dup(cos_t), dup(sin_t), sgn(cos_t), sgn(sin_t)


def _block_rows(bb, tl):
    return bb * tl // (max(1, tl // PREP_ROWS) if bb == 1 else 1)


def _layer_tables(pos0, seq, dh, *, bb, tl, chunk):
    lg = jnp.log1p(-jnp.exp2(-5.0 - jnp.arange(RET_HEADS, dtype=F32)))
    idx = jnp.arange(chunk, dtype=F32)
    diff = idx[:, None] - idx[None, :]
    d_intra = jnp.where(diff[None] >= 0, jnp.exp(jnp.maximum(diff, 0.0)[None] * lg[:, None, None]), 0.0)
    d_q = jnp.broadcast_to(jnp.exp((idx + 1.0)[None, :] * lg[:, None])[:, :, None], (RET_HEADS, chunk, dh))
    d_k = jnp.broadcast_to(jnp.exp((chunk - 1.0 - idx)[None, :] * lg[:, None])[:, :, None], (RET_HEADS, chunk, dh))
    d_c = jnp.exp(chunk * lg)
    block = _block_rows(bb, tl)
    tri = jnp.triu(jnp.ones((block, block), BF16), 1)
    return dict(rope=_rope_tables(pos0, seq, tl, dh), d_intra=d_intra, d_q=d_q, d_k=d_k, d_c=d_c, tri=tri)


def _layer_call(x, b0, nb, s0, h0, pos0, consts, tables, after, *, bb, tl, chunk):
    _, seq, d_model = x.shape
    bsz = nb
    blk0 = b0 // bb
    rows = bb * tl
    rw_width = consts["gret"].shape[-1]
    pw = consts["pscale"].shape[-1]
    dh = rw_width // RET_HEADS
    block = _block_rows(bb, tl)
    rope, d_intra, d_q, d_k, d_c, tri = (tables[k] for k in ("rope", "d_intra", "d_q", "d_k", "d_c", "tri"))

    const2 = lambda b, l, *_: (0, 0)
    const3 = lambda b, l, *_: (0, 0, 0)
    grid_spec = pltpu.PrefetchScalarGridSpec(
        num_scalar_prefetch=0,
        grid=(bsz // bb, seq // tl),
        in_specs=[
            pl.BlockSpec(memory_space=pltpu.SMEM),
            pl.BlockSpec((bb, tl, d_model), lambda b, l: (blk0 + b, l, 0)),
            pl.BlockSpec((bb, RET_HEADS, dh, dh), lambda b, l: (b, 0, 0, 0)),
            pl.BlockSpec((bb, HIST_ROWS, pw), lambda b, l: (b, 0, 0)),
            pl.BlockSpec((1, SUBLANES, dh), lambda b, l: (l, 0, 0)),
            pl.BlockSpec((tl, dh), const2),
            pl.BlockSpec((tl, dh), const2),
            pl.BlockSpec((tl, dh), const2),
            pl.BlockSpec((tl, dh), const2),
            pl.BlockSpec((RET_HEADS, chunk, chunk), const3),
            pl.BlockSpec((RET_HEADS, chunk, dh), const3),
            pl.BlockSpec((RET_HEADS, chunk, dh), const3),
            pl.BlockSpec((1, d_model), const2),
            pl.BlockSpec(consts["w_in"].shape, const2),
            pl.BlockSpec((1, rw_width), const2),
            pl.BlockSpec(consts["w_pool"].shape, const3),
            pl.BlockSpec((1, pw), const2),
            pl.BlockSpec(consts["w_out"].shape, const2),
            pl.BlockSpec((1, d_model), const2),
            pl.BlockSpec((ROUTER_ROWS, d_model), const2),
            pl.BlockSpec((block, block), const2),
            pl.BlockSpec(memory_space=pl.ANY),
        ],
        out_specs=[
            pl.BlockSpec((bb, tl, d_model), lambda b, l: (b, l, 0)),
            pl.BlockSpec((bb, tl, d_model // 2), lambda b, l: (b, l, 0)),
            pl.BlockSpec((1, 1, EXPERTS_PER_GROUP, rows), lambda b, l: (b, l, 0, 0)),
            pl.BlockSpec((1, 1, EXPERTS_PER_GROUP, rows), lambda b, l: (b, l, 0, 0)),
            pl.BlockSpec((bb, RET_HEADS, dh, dh), lambda b, l: (b, 0, 0, 0)),
            pl.BlockSpec((bb, HIST_ROWS, pw), lambda b, l: (b, 0, 0)),
            pl.BlockSpec((N_EXPERTS, block), const2),
        ],
        scratch_shapes=[
            pltpu.VMEM((bb, HIST_ROWS + tl, pw), F32),
            pltpu.VMEM((rows, rw_width), BF16),
            pltpu.VMEM((rows, rw_width), F32),
            pltpu.VMEM((rows, rw_width), BF16),
            pltpu.VMEM((rows, rw_width), F32),
            pltpu.VMEM((rows, d_model), BF16),
        ],
    )
    out_shape = [
        jax.ShapeDtypeStruct((bsz, seq, d_model), F32),
        jax.ShapeDtypeStruct((bsz, seq, d_model // 2), U32),
        jax.ShapeDtypeStruct((bsz // bb, seq // tl, EXPERTS_PER_GROUP, rows), I32),
        jax.ShapeDtypeStruct((bsz // bb, seq // tl, EXPERTS_PER_GROUP, rows), F32),
        jax.ShapeDtypeStruct((bsz, RET_HEADS, dh, dh), F32),
        jax.ShapeDtypeStruct((bsz, HIST_ROWS, pw), F32),
        jax.ShapeDtypeStruct((N_EXPERTS, block), F32),
    ]
    kern = functools.partial(_layer_kernel, bb=bb, tl=tl, chunk=chunk, pos0=pos0)
    operands = (d_c, x, s0, h0, *rope, d_intra, d_q, d_k,
                consts["g1"], consts["w_in"], consts["gret"], consts["w_pool"], consts["pscale"],
                consts["w_out"], consts["g2"], consts["wr"], tri, after)
    n_tok = bsz * seq
    mm_flops_per_token = 2 * (d_model * consts["w_in"].shape[1] + d_model * d_model + d_model * ROUTER_ROWS
                              + pw * pw // 4
                              + rw_width * (2 * chunk + 2 * dh) + N_EXPERTS * block)
    cost = pl.CostEstimate(
        flops=n_tok * mm_flops_per_token, transcendentals=n_tok * (rw_width + 2 * N_EXPERT_GROUPS),
        bytes_accessed=_nbytes(*operands) - _nbytes(x, after) + n_tok * d_model * 4 + _nbytes(*out_shape))
    return pl.pallas_call(
        kern, grid_spec=grid_spec, out_shape=out_shape, name=f"layer_pos{pos0}_b{b0}", cost_estimate=cost,
        compiler_params=pltpu.CompilerParams(
            dimension_semantics=("arbitrary", "arbitrary"), vmem_limit_bytes=VMEM_LIMIT),
    )(*operands)


def _sc_partition(n_units):
    info = plsc.get_sparse_core_info()
    nc, nw = info.num_cores, info.num_cores * info.num_subcores
    upw = -(-n_units // nw)
    upw += upw % 2
    return nc, nw, upw


def _units_by_worker(idx, n_units, upw, nw):
    idx = jnp.pad(idx.reshape(n_units, SC_UNIT), ((0, nw * upw - n_units), (0, 0)))
    return idx.reshape(upw, nw, SC_UNIT).transpose(1, 0, 2)


def _sc_dispatch(srcs, idx0, idx1, n_out_rows, after=None):
    assert 1 <= len(srcs) <= 2
    d = srcs[0].shape[1]
    dtype = srcs[0].dtype
    assert all(src.shape[0] % SC_UNIT == 0 for src in srcs)
    units_a = srcs[0].shape[0] // SC_UNIT
    n_units = sum(src.shape[0] for src in srcs) // SC_UNIT
    nc, nw, upw = _sc_partition(n_units)
    idx0 = _units_by_worker(idx0, n_units, upw, nw)
    idx1 = _units_by_worker(idx1, n_units, upw, nw)
    mesh = plsc.VectorSubcoreMesh(core_axis_name="c", subcore_axis_name="s")
    dma = pltpu.SemaphoreType.DMA
    extra = [] if after is None else [after]

    moved = n_units * SC_UNIT * d * jnp.dtype(dtype).itemsize
    @functools.partial(
        pl.kernel, mesh=mesh,
        cost_estimate=pl.CostEstimate(flops=0, transcendentals=0, bytes_accessed=3 * moved + _nbytes(idx0, idx1)),
        out_type=jax.ShapeDtypeStruct((n_out_rows, d), dtype),
        scratch_types=[
            pltpu.VMEM((upw, SC_UNIT), I32),
            pltpu.VMEM((upw, SC_UNIT), I32),
            pltpu.VMEM((SC_UNIT, d), dtype),
            pltpu.VMEM((SC_UNIT, d), dtype),
            dma, dma, dma, dma, dma, dma,
        ],
    )
    def k(*refs):
        src_hbm = refs[:len(srcs)]
        i0_hbm, i1_hbm, out_hbm, i0_v, i1_v, rows0, rows1, l0, l1, p0, p1, q0, q1 = refs[len(srcs) + len(extra):]
        wid = lax.axis_index("s") * nc + lax.axis_index("c")
        pltpu.sync_copy(i0_hbm.at[wid], i0_v)
        pltpu.sync_copy(i1_hbm.at[wid], i1_v)
        rows, lsem, psem, qsem = (rows0, rows1), (l0, l1), (p0, p1), (q0, q1)

        def live(j):
            return j * nw + wid < n_units

        def load(j, b, op):
            unit = j * nw + wid

            @pl.when(live(j) & (unit < units_a))
            def _():
                op(pltpu.make_async_copy(
                    src_hbm[0].at[pl.ds(pl.multiple_of(unit * SC_UNIT, 8), SC_UNIT)], rows[b], lsem[b]))

            if len(srcs) == 2:
                @pl.when(live(j) & (unit >= units_a))
                def _():
                    op(pltpu.make_async_copy(
                        src_hbm[1].at[pl.ds(pl.multiple_of((unit - units_a) * SC_UNIT, 8), SC_UNIT)],
                        rows[b], lsem[b]))

        def scatter(j, b, op):
            @pl.when(live(j))
            def _():
                op(pltpu.make_async_copy(rows[b], out_hbm.at[i0_v.at[j]], psem[b]))
                op(pltpu.make_async_copy(rows[b], out_hbm.at[i1_v.at[j]], qsem[b]))

        start = lambda c: c.start()
        wait = lambda c: c.wait()
        load(0, 0, start)

        @pl.loop(0, upw, step=2)
        def _(j):
            @pl.when(j > 0)
            def _():
                scatter(j - 1, 1, wait)
            load(j + 1, 1, start)
            load(j, 0, wait)
            scatter(j, 0, start)
            scatter(j, 0, wait)

            @pl.when(j + 2 < upw)
            def _():
                load(j + 2, 0, start)
            load(j + 1, 1, wait)
            scatter(j + 1, 1, start)

        scatter(upw - 1, 1, wait)

    return k(*srcs, *extra, idx0, idx1), idx1


def _sc_gather(table, idx):
    n = idx.shape[0]
    d = table.shape[1]
    assert n % SC_UNIT == 0
    n_units = n // SC_UNIT
    nc, nw, upw = _sc_partition(n_units)
    idx = _units_by_worker(idx, n_units, upw, nw)
    mesh = plsc.VectorSubcoreMesh(core_axis_name="c", subcore_axis_name="s")
    dma = pltpu.SemaphoreType.DMA

    @functools.partial(
        pl.kernel, mesh=mesh,
        cost_estimate=pl.CostEstimate(flops=0, transcendentals=0,
                                      bytes_accessed=2 * n * d * table.dtype.itemsize + _nbytes(idx)),
        out_type=jax.ShapeDtypeStruct((n, d), table.dtype),
        scratch_types=[
            pltpu.VMEM((upw, SC_UNIT), I32),
            pltpu.VMEM((SC_UNIT, d), table.dtype),
            pltpu.VMEM((SC_UNIT, d), table.dtype),
            dma, dma, dma, dma,
        ],
    )
    def k(t_hbm, i_hbm, out_hbm, i_v, rows0, rows1, g0, g1, w0, w1):
        wid = lax.axis_index("s") * nc + lax.axis_index("c")
        pltpu.sync_copy(i_hbm.at[wid], i_v)
        rows, gsem, wsem = (rows0, rows1), (g0, g1), (w0, w1)

        def live(j):
            return j * nw + wid < n_units

        def gather(j, b, op):
            @pl.when(live(j))
            def _():
                op(pltpu.make_async_copy(t_hbm.at[i_v.at[j]], rows[b], gsem[b]))

        def write(j, b, op):
            @pl.when(live(j))
            def _():
                op(pltpu.make_async_copy(
                    rows[b], out_hbm.at[pl.ds(pl.multiple_of((j * nw + wid) * SC_UNIT, 8), SC_UNIT)], wsem[b]))

        start = lambda c: c.start()
        wait = lambda c: c.wait()
        gather(0, 0, start)

        @pl.loop(0, upw, step=2)
        def _(j):
            @pl.when(j > 0)
            def _():
                write(j - 1, 1, wait)
            gather(j + 1, 1, start)
            gather(j, 0, wait)
            write(j, 0, start)
            write(j, 0, wait)

            @pl.when(j + 2 < upw)
            def _():
                gather(j + 2, 0, start)
            gather(j + 1, 1, wait)
            write(j + 1, 1, start)

        write(upw - 1, 1, wait)

    return k(table, idx)


def _moe_kernel(start0_ref, count0_ref, gtot0_ref, start1_ref, count1_ref, gtot1_ref,
                xs0_hbm, xs1_hbm, wg_ref, wu_ref, wd_ref, ys0_hbm, ys1_hbm,
                wgu_s, wd_s, xbuf0, ybuf0, xbuf1, ybuf1, sem_in0, sem_out0, sem_in1, sem_out1):
    e = pl.program_id(0)
    last = pl.num_programs(0) - 1
    hidden = wd_s.shape[0]
    half = xbuf0.shape[-1]
    segments = (
        (xs0_hbm, ys0_hbm, xbuf0, ybuf0, sem_in0, sem_out0, start0_ref[e], count0_ref[e], gtot0_ref[0]),
        (xs1_hbm, ys1_hbm, xbuf1, ybuf1, sem_in1, sem_out1, start1_ref[e], count1_ref[e], gtot1_ref[0]),
    )

    def rows_of(g):
        return pl.ds(pl.multiple_of(g * MOE_TILE, MOE_TILE), MOE_TILE)

    def pipeline(xs_hbm, ys_hbm, xbuf, ybuf, sem_in, sem_out):
        def copy_in(g):
            slot = g % MOE_BUFFERS
            return pltpu.make_async_copy(xs_hbm.at[rows_of(g)], xbuf.at[slot], sem_in.at[slot])

        def copy_out(g):
            slot = g % MOE_BUFFERS
            return pltpu.make_async_copy(ybuf.at[slot], ys_hbm.at[rows_of(g)], sem_out.at[slot])
        return copy_in, copy_out

    @pl.when(e == 0)
    def _():
        for xs_hbm, ys_hbm, xbuf, ybuf, sem_in, sem_out, _, _, g_total in segments:
            copy_in, _ = pipeline(xs_hbm, ys_hbm, xbuf, ybuf, sem_in, sem_out)
            for g in range(MOE_LOOKAHEAD):
                @pl.when(g < g_total)
                def _():
                    copy_in(g).start()

    @pl.when(segments[0][7] + segments[1][7] > 0)
    def _():
        wgu_s[:, 0:hidden] = wg_ref[0].astype(BF16)
        wgu_s[:, hidden:2 * hidden] = wu_ref[0].astype(BF16)
        wd_s[...] = wd_ref[0].astype(BF16)

    def expert_rows(xbuf, ybuf, slot, valid):
        row = lax.broadcasted_iota(I32, (MOE_TILE, half), 0)
        x_lo, x_hi = _unpack_bf16_pair(jnp.where(row < valid, xbuf[slot], jnp.uint32(0)))
        ab = (jnp.dot(x_lo.astype(BF16), wgu_s[0:half, :], preferred_element_type=F32)
              + jnp.dot(x_hi.astype(BF16), wgu_s[half:2 * half, :], preferred_element_type=F32))
        a = ab[:, 0:hidden]
        he = a * _sigmoid(a) * ab[:, hidden:2 * hidden]
        y = jnp.dot(he.astype(BF16), wd_s[...], preferred_element_type=F32)
        ybuf[slot] = _pack_bf16_pair(y[:, 0:half], y[:, half:2 * half])

    for xs_hbm, ys_hbm, xbuf, ybuf, sem_in, sem_out, start, count, g_total in segments:
        copy_in, copy_out = pipeline(xs_hbm, ys_hbm, xbuf, ybuf, sem_in, sem_out)
        g_first = start // MOE_TILE
        n_tiles = (count + MOE_TILE - 1) // MOE_TILE

        def tiles(t, width, copy_in=copy_in, copy_out=copy_out, xbuf=xbuf, ybuf=ybuf,
                  g_first=g_first, count=count, g_total=g_total):
            gs = [g_first + t + i for i in range(width)]
            for g in gs:
                @pl.when(g + MOE_LOOKAHEAD < g_total)
                def _():
                    copy_in(g + MOE_LOOKAHEAD).start()
            for g in gs:
                copy_in(g).wait()

                @pl.when(g >= MOE_BUFFERS)
                def _():
                    copy_out(g - MOE_BUFFERS).wait()
            for i, g in enumerate(gs):
                expert_rows(xbuf, ybuf, g % MOE_BUFFERS, count - (t + i) * MOE_TILE)
            for g in gs:
                copy_out(g).start()

        def pair(p, carry, tiles=tiles):
            tiles(MOE_UNROLL * p, MOE_UNROLL)
            return carry

        lax.fori_loop(0, n_tiles // MOE_UNROLL, pair, 0)

        def single(r, carry, tiles=tiles, n_tiles=n_tiles):
            tiles(n_tiles // MOE_UNROLL * MOE_UNROLL + r, 1)
            return carry

        lax.fori_loop(0, n_tiles % MOE_UNROLL, single, 0)

        @pl.when(e == last)
        def _(copy_out=copy_out, g_total=g_total):
            for j in range(1, MOE_BUFFERS + 1):
                @pl.when(g_total >= j)
                def _():
                    copy_out(g_total - j).wait()


def _moe_call(groups, w_g, w_u, w_d):
    (xs0, starts0, cnt0), (xs1, starts1, cnt1) = groups
    half = xs0.shape[1]
    n_experts, d_model, hidden = w_g.shape

    def tiles_total(starts, cnt):
        return ((starts[-1:] + cnt[-1:] + MOE_TILE - 1) // MOE_TILE).astype(I32)

    wspec = lambda shape: pl.BlockSpec(shape, lambda e, *_: (e, 0, 0))
    tile_bufs = [pltpu.VMEM((MOE_BUFFERS, MOE_TILE, half), U32)] * 4
    grid_spec = pltpu.PrefetchScalarGridSpec(
        num_scalar_prefetch=6,
        grid=(n_experts,),
        in_specs=[
            pl.BlockSpec(memory_space=pl.ANY),
            pl.BlockSpec(memory_space=pl.ANY),
            wspec((1, d_model, hidden)),
            wspec((1, d_model, hidden)),
            wspec((1, hidden, d_model)),
        ],
        out_specs=[pl.BlockSpec(memory_space=pl.ANY), pl.BlockSpec(memory_space=pl.ANY)],
        scratch_shapes=[
            pltpu.VMEM((d_model, 2 * hidden), BF16),
            pltpu.VMEM((hidden, d_model), BF16),
            *tile_bufs,
            *[pltpu.SemaphoreType.DMA((MOE_BUFFERS,))] * 4,
        ],
    )
    n_rows = xs0.shape[0] + xs1.shape[0]
    cost = pl.CostEstimate(flops=n_rows * 6 * d_model * hidden, transcendentals=n_rows * hidden,
                           bytes_accessed=2 * _nbytes(xs0, xs1) + _nbytes(w_g, w_u, w_d))
    return pl.pallas_call(
        _moe_kernel, grid_spec=grid_spec, cost_estimate=cost,
        out_shape=[jax.ShapeDtypeStruct(xs0.shape, U32), jax.ShapeDtypeStruct(xs1.shape, U32)], name="moe_experts",
        compiler_params=pltpu.CompilerParams(
            dimension_semantics=("arbitrary",), vmem_limit_bytes=VMEM_LIMIT),
    )(starts0, cnt0, tiles_total(starts0, cnt0), starts1, cnt1, tiles_total(starts1, cnt1),
      xs0, xs1, w_g, w_u, w_d)


def _combine_kernel(x1_ref, y0_ref, y1_ref, rw_ref, gf_ref, *rest):
    out_ref = rest[-1]
    tr = x1_ref.shape[0]
    w_rows = jnp.concatenate([rw_ref[0], jnp.zeros((LANES - rw_ref.shape[1], tr), F32)], axis=0)
    w_cols = w_rows.T
    w0, w1 = w_cols[:, 0:1], w_cols[:, 1:2]
    a_lo, a_hi = _unpack_bf16_pair(y0_ref[0])
    b_lo, b_hi = _unpack_bf16_pair(y1_ref[0])
    moe = jnp.concatenate([w0 * a_lo + w1 * b_lo, w0 * a_hi + w1 * b_hi], axis=-1)
    out_ref[...] = _rms(x1_ref[...] + moe, gf_ref[...])


def _combine_call(x1, rw, row0, n, yg, gf, out_rows, out_row0, prev_out=None):
    t, d_model = x1.shape
    tr = rw.shape[-1]
    half = yg.shape[-1]
    assert t % tr == 0 and row0 % tr == 0 and n % tr == 0 and rw.shape == (t // tr, EXPERTS_PER_GROUP, tr)
    assert yg.shape == (2, n, half) and out_row0 % tr == 0
    off = row0 // tr
    ooff = out_row0 // tr
    in_specs = [
        pl.BlockSpec((tr, d_model), lambda i: (off + i, 0)),
        pl.BlockSpec((1, tr, half), lambda i: (0, i, 0)),
        pl.BlockSpec((1, tr, half), lambda i: (1, i, 0)),
        pl.BlockSpec((1, EXPERTS_PER_GROUP, tr), lambda i: (off + i, 0, 0)),
        pl.BlockSpec((1, d_model), lambda i: (0, 0)),
    ]
    args = [x1, yg, yg, rw, gf]
    aliases = {}
    if prev_out is not None:
        in_specs.append(pl.BlockSpec(memory_space=pl.ANY))
        args.append(prev_out)
        aliases = {len(args) - 1: 0}
    return pl.pallas_call(
        _combine_kernel,
        grid=(n // tr,),
        in_specs=in_specs,
        out_specs=pl.BlockSpec((tr, d_model), lambda i: (ooff + i, 0)),
        out_shape=jax.ShapeDtypeStruct((out_rows, d_model), F32), name=f"combine_row{out_row0}_of{out_rows}",
        cost_estimate=pl.CostEstimate(flops=8 * n * d_model, transcendentals=n,
                                      bytes_accessed=2 * n * d_model * 4 + _nbytes(yg) + n * 4 * EXPERTS_PER_GROUP),
        input_output_aliases=aliases,
        compiler_params=pltpu.CompilerParams(
            dimension_semantics=("arbitrary",), vmem_limit_bytes=VMEM_LIMIT),
    )(*args)


def _route(streams, after=None):
    tokens = [h2.shape[0] for h2, _, _ in streams]
    counts = [cnt[:, 0].astype(I32) for _, _, cnt in streams]
    total = sum(counts)
    padded = ((total + MOE_TILE - 1) // MOE_TILE) * MOE_TILE
    starts = (jnp.cumsum(padded) - padded).astype(I32)
    experts = jnp.arange(N_EXPERTS, dtype=I32)[None, :, None]
    pos, base = [], starts
    for (_, ri, _), t, cnt in zip(streams, tokens, counts):
        ri = jnp.moveaxis(ri, 2, 0).reshape(ri.shape[2], t)
        first_row = jnp.sum(jnp.where(ri[0:2, None, :] == experts, base[None, :, None], 0), axis=1)
        pos.append(ri[2:4] + first_row)
        base = base + cnt
    pos = jnp.concatenate(pos, axis=1)
    n_rows = ((2 * sum(tokens) + N_EXPERTS * (MOE_TILE - 1)) // MOE_TILE) * MOE_TILE
    xs_sorted, ready = _sc_dispatch([h2 for h2, _, _ in streams], pos[0], pos[1], n_rows, after)
    return (xs_sorted, starts, total), pos, ready


def _gather_tokens(ys_sorted, pos, t0, n):
    return _sc_gather(ys_sorted, pos[:, t0:t0 + n].reshape(2 * n)).reshape(2, n, ys_sorted.shape[-1])


def _one_layer(xp, xs, s_ret, c_pool, norm1_g, w_in, ret_norm_g, w_pool, pool_scale, w_out, norm2_g,
               w_rg, w_re, w_g, w_u, w_d, final_g, past_len):
    bp, seq, d_model = xp.shape
    bs, dseq, _ = xs.shape
    rw_width = ret_norm_g.shape[-1]
    pw = pool_scale.shape[-1]
    dh = rw_width // RET_HEADS
    half = d_model // 2

    w_r = jnp.concatenate(
        [w_re.T, w_rg.T, jnp.zeros((ROUTER_ROWS - N_EXPERTS - N_EXPERT_GROUPS, d_model), F32)], axis=0)
    wr = w_r.astype(BF16)
    consts = dict(
        g1=norm1_g.reshape(1, d_model), w_in=w_in.astype(BF16), gret=ret_norm_g.reshape(1, rw_width),
        w_pool=w_pool.astype(BF16), pscale=pool_scale.reshape(1, pw), w_out=w_out.astype(BF16),
        g2=norm2_g.reshape(1, d_model), wr=wr)

    gf = final_g.reshape(1, d_model)

    ts = bs * dseq
    b_lead = bp - 1
    t_lead, t_rest = b_lead * seq, (bp - b_lead) * seq
    zeros = lambda nb: (jnp.zeros((nb, RET_HEADS, dh, dh), F32), jnp.zeros((nb, HIST_ROWS, pw), F32))
    h0s = jnp.pad(c_pool, ((0, 0), (HIST_ROWS - POOL_HIST, 0), (0, 0)))
    prompt_tile = dict(bb=1, tl=PROMPT_TILE, chunk=min(RET_CHUNK, seq))
    sample_tile = dict(bb=bs, tl=dseq, chunk=min(RET_CHUNK, dseq))
    prompt_tables = _layer_tables(0, seq, dh, **prompt_tile)
    sample_tables = _layer_tables(past_len, dseq, dh, **sample_tile)

    def stream(layer_out, t):
        x1, h2, ri, rw, st, hist, cnt = layer_out
        return dict(x1=x1.reshape(t, d_model), route=(h2.reshape(t, half), ri, cnt),
                    rw=rw.reshape(-1, EXPERTS_PER_GROUP, rw.shape[-1]), st=st, hist=hist)

    pa = stream(_layer_call(xp, 0, b_lead, *zeros(b_lead), 0, consts, prompt_tables, gf, **prompt_tile), t_lead)
    group0, pos0, ready0 = _route([pa["route"]])
    pb = stream(_layer_call(xp, b_lead, bp - b_lead, *zeros(bp - b_lead), 0, consts, prompt_tables, ready0,
                            **prompt_tile), t_rest)
    sm = stream(_layer_call(xs, 0, bs, s_ret, h0s, past_len, consts, sample_tables, pb["route"][2], **sample_tile),
                ts)
    group1, pos1, _ = _route([pb["route"], sm["route"]], after=group0[0])
    ys0, ys1 = _moe_call([group0, group1], w_g, w_u, w_d)

    tp = bp * seq
    yp = None
    for st, ys, pos, out0, t_stream, sizes in ((pb, ys1, pos1, t_lead, t_rest, REST_CHUNK_ROWS),
                                               (pa, ys0, pos0, 0, t_lead, LEAD_CHUNK_ROWS)):
        row0 = 0
        for size in sizes:
            n = min(size, t_stream - row0)
            if n > 0:
                yp = _combine_call(st["x1"], st["rw"], row0, n, _gather_tokens(ys, pos, row0, n), gf,
                                   tp, out0 + row0, prev_out=yp)
                row0 += n
        assert row0 == t_stream
    ysm = _combine_call(sm["x1"], sm["rw"], 0, ts, _gather_tokens(ys1, pos1, t_rest, ts), gf, ts, 0)
    st_p = jnp.concatenate([pa["st"], pb["st"]], axis=0)
    hist_p = jnp.concatenate([pa["hist"], pb["hist"]], axis=0)
    return (yp.reshape(bp, seq, d_model), ysm.reshape(bs, dseq, d_model),
            st_p, hist_p[:, HIST_ROWS - POOL_HIST:], sm["st"], sm["hist"][:, HIST_ROWS - POOL_HIST:])


def kernel(x_prompt, x_sample, state_ret, cache_pool, norm1_g, w_in, ret_norm_g, w_pool, pool_scale, w_out,
           norm2_g, w_router_group, w_router_expert, w_exp_gate, w_exp_up, w_exp_down, final_norm_g):
    depth = w_in.shape[0]
    assert depth == 1, "the final RMSNorm is fused into the layer's combine kernel"
    assert x_prompt.shape[0] >= 2 and x_prompt.shape[1] % PROMPT_TILE == 0
    yp, ys, s_p, h_p, s_s, h_s = _one_layer(
        x_prompt, x_sample, state_ret[0], cache_pool[0], norm1_g[0], w_in[0], ret_norm_g[0], w_pool[0],
        pool_scale[0], w_out[0], norm2_g[0], w_router_group[0], w_router_expert[0],
        w_exp_gate[0], w_exp_up[0], w_exp_down[0], final_norm_g, PAST_LEN)
    return (yp, ys, s_p[None], h_p[None], s_s[None], h_s[None])
```

```python
import functools

import jax
import jax.numpy as jnp
from jax import lax
from jax.experimental import pallas as pl
from jax.experimental.pallas import tpu as pltpu
from jax.experimental.pallas import tpu_sc as plsc

F32 = jnp.float32
BF16 = jnp.bfloat16
I32 = jnp.int32
U32 = jnp.uint32

EPS = 1e-6
ROPE_BASE = 10000.0
RET_HEADS = 4
POOL_WINDOWS = (2, 4, 8, 16)
POOL_HIST = max(POOL_WINDOWS) - 1
N_EXPERT_GROUPS = 4
EXPERTS_PER_GROUP = 8
N_EXPERTS = N_EXPERT_GROUPS * EXPERTS_PER_GROUP
ROUTER_ROWS = 48
PAST_LEN = 1024

LANES = 128
SUBLANES = 8
HIST_ROWS = 16
MOE_TILE = 256
MOE_BUFFERS = 16
MOE_UNROLL = 2
MOE_LOOKAHEAD = MOE_BUFFERS - MOE_UNROLL
RET_CHUNK = 256
PREP_ROWS = 512
PROMPT_TILE = 1024
REST_CHUNK_ROWS = (1 << 30,)
LEAD_CHUNK_ROWS = (8192, 1 << 30)
SC_UNIT = 32
V7X_VMEM_BYTES = 64 * 1024 * 1024
VMEM_LIMIT = V7X_VMEM_BYTES * 7 // 8


def _nbytes(*arrays):
    return sum(a.size * a.dtype.itemsize for a in arrays)


def _rms(x, g):
    return x * lax.rsqrt(jnp.mean(x * x, axis=-1, keepdims=True) + EPS) * g


def _sigmoid(x):
    return 1.0 / (1.0 + jnp.exp(-x))


def _pack_bf16_pair(lo, hi):
    lo_b = lax.bitcast_convert_type(lo.astype(BF16).astype(F32), U32)
    hi_b = lax.bitcast_convert_type(hi.astype(BF16).astype(F32), U32)
    return hi_b | (lo_b >> 16)


def _unpack_bf16_pair(p):
    lo = lax.bitcast_convert_type(p << 16, F32)
    hi = lax.bitcast_convert_type(p & jnp.uint32(0xFFFF0000), F32)
    return lo, hi


def _layer_kernel(dc_ref, x_ref, s0_ref, h0_ref, rb_ref, rc_ref, rs_ref, rcs_ref, rss_ref,
                  dintra_ref, dq_ref, dk_ref,
                  g1_ref, win_ref, gret_ref, wpool_ref, pscale_ref, wout_ref, g2_ref,
                  wr_ref, tri_ref, after_ref,
                  x1_ref, h2_ref, ri_ref, rw_ref, st_ref, hist_ref, cnt_ref,
                  ue_ref, q_ref, k_ref, v_ref, gate_ref, a_ref,
                  *, bb, tl, chunk, pos0):
    b_idx = pl.program_id(0)
    l_idx = pl.program_id(1)
    rows = bb * tl
    d_model = x_ref.shape[-1]
    rw_width = q_ref.shape[-1]
    dh = rw_width // RET_HEADS
    pw = ue_ref.shape[-1]
    gw = pw // len(POOL_WINDOWS)
    n_chunks = tl // chunk

    @pl.when(l_idx == 0)
    def _():
        st_ref[...] = s0_ref[...]
        ue_ref[:, 0:HIST_ROWS, :] = h0_ref[...]

    @pl.when((l_idx == 0) & (b_idx == 0))
    def _():
        cnt_ref[...] = jnp.zeros_like(cnt_ref)

    cos_b = rb_ref[0, 0:1, :]
    sin_b = rb_ref[0, 1:2, :]
    cosf = cos_b * rc_ref[...] - sin_b * rs_ref[...]
    sinf = sin_b * rcs_ref[...] + cos_b * rss_ref[...]
    k_scale = dh ** -0.5
    n_blocks = max(1, tl // PREP_ROWS) if bb == 1 else 1
    block = rows // n_blocks
    for blk in range(n_blocks):
        rs = slice(blk * block, (blk + 1) * block)
        xb = x_ref[0, rs, :] if bb == 1 else x_ref[...].reshape(rows, d_model)
        hb = _rms(xb, g1_ref[...]).astype(BF16)
        proj = jnp.dot(hb, win_ref[...], preferred_element_type=F32)

        def rotate(a):
            if bb == 1:
                return a * cosf[rs] + pltpu.roll(a, dh // 2, 1) * sinf[rs]
            return (a.reshape(bb, tl, dh) * cosf[None]
                    + pltpu.roll(a, dh // 2, 1).reshape(bb, tl, dh) * sinf[None]).reshape(rows, dh)

        for hh in range(RET_HEADS):
            cs = slice(hh * dh, (hh + 1) * dh)
            q_ref[rs, cs] = rotate(proj[:, hh * dh:(hh + 1) * dh]).astype(BF16)
            k_ref[rs, cs] = rotate(proj[:, rw_width + hh * dh:rw_width + (hh + 1) * dh]) * k_scale
        v_ref[rs, :] = proj[:, 2 * rw_width:3 * rw_width].astype(BF16)
        gate_ref[rs, :] = proj[:, 3 * rw_width:4 * rw_width]
        u = proj[:, 4 * rw_width:4 * rw_width + pw]
        if bb == 1:
            ue_ref[0, HIST_ROWS + blk * block:HIST_ROWS + (blk + 1) * block, :] = u
        else:
            ue_ref[:, HIST_ROWS:HIST_ROWS + tl, :] = u.reshape(bb, tl, pw)

    def ret_block(b, c):
        r0 = b * tl + c * chunk
        if not isinstance(r0, int):
            r0 = pl.multiple_of(r0, chunk)
        rr = pl.ds(r0, chunk)
        for hh in range(RET_HEADS):
            cs = slice(hh * dh, (hh + 1) * dh)
            qc = q_ref[rr, cs]
            kf = k_ref[rr, cs]
            vc = v_ref[rr, cs]
            s_old = st_ref[b, hh]
            sc = lax.dot_general(qc, kf.astype(BF16), (((1,), (1,)), ((), ())),
                                 preferred_element_type=F32) * dintra_ref[hh]
            o = (jnp.dot(sc.astype(BF16), vc, preferred_element_type=F32)
                 + dq_ref[hh] * jnp.dot(qc, s_old.astype(BF16), preferred_element_type=F32))
            kd = (kf * dk_ref[hh]).astype(BF16)
            s_new = dc_ref[hh] * s_old + lax.dot_general(
                kd, vc, (((0,), (0,)), ((), ())), preferred_element_type=F32)
            st_ref[b, hh] = s_new
            oc = o - jnp.mean(o, axis=-1, keepdims=True)
            var = jnp.mean(oc * oc, axis=-1, keepdims=True)
            y = oc * lax.rsqrt(var + EPS) * gret_ref[:, cs]
            g = gate_ref[rr, cs]
            a_ref[rr, cs] = (g * _sigmoid(g) * y).astype(BF16)

    if bb * n_chunks <= 4:
        for b in range(bb):
            for c in range(n_chunks):
                ret_block(b, c)
    else:
        def body(i, carry):
            ret_block(i // n_chunks, i % n_chunks)
            return carry
        lax.fori_loop(0, bb * n_chunks, body, 0)

    nt = (((1,), (1,)), ((), ()))
    neg = jnp.float32(-jnp.inf)
    big = jnp.float32(1e9)
    sub = lax.broadcasted_iota(I32, (EXPERTS_PER_GROUP, block), 0).astype(F32)
    eid = lax.broadcasted_iota(I32, (N_EXPERTS, block), 0).astype(F32)
    for blk in range(n_blocks):
        lo = blk * block
        rs = slice(lo, lo + block)

        if bb == 1:
            pos = pos0 + l_idx * tl + lo + lax.broadcasted_iota(I32, (block, 1), 0)
            window = lambda off, cs: ue_ref[0, HIST_ROWS + lo - off:HIST_ROWS + lo - off + block, cs]
        else:
            pos = pos0 + l_idx * tl + lax.broadcasted_iota(I32, (1, tl, 1), 1)
            window = lambda off, cs: ue_ref[:, HIST_ROWS - off:HIST_ROWS - off + tl, cs]
        for gi, w in enumerate(POOL_WINDOWS):
            cs = slice(gi * gw, (gi + 1) * gw)
            u_g = window(0, cs)
            acc = u_g
            for j in range(1, w):
                acc = acc + window(j, cs)
            inv_cnt = 1.0 / jnp.minimum(pos + 1, w).astype(F32)
            p = (acc * inv_cnt - u_g).reshape(block, gw)
            z = jnp.dot(p.astype(BF16), wpool_ref[gi], preferred_element_type=F32) * pscale_ref[:, cs]
            a_ref[rs, rw_width + gi * gw:rw_width + (gi + 1) * gw] = z.astype(BF16)

        xb = x_ref[0, rs, :] if bb == 1 else x_ref[...].reshape(rows, d_model)
        x1 = xb + jnp.dot(a_ref[rs, :], wout_ref[...], preferred_element_type=F32)
        h2 = _rms(x1, g2_ref[...])
        h2_packed = _pack_bf16_pair(h2[:, 0:d_model // 2], h2[:, d_model // 2:])
        if bb == 1:
            x1_ref[0, rs, :] = x1
            h2_ref[0, rs, :] = h2_packed
        else:
            x1_ref[...] = x1.reshape(bb, tl, d_model)
            h2_ref[...] = h2_packed.reshape(bb, tl, d_model // 2)

        lt = lax.dot_general(wr_ref[...], h2.astype(BF16), nt, preferred_element_type=F32)
        gl = jnp.where(sub < N_EXPERT_GROUPS, lt[N_EXPERTS:N_EXPERTS + EXPERTS_PER_GROUP], neg)
        gmax = jnp.max(gl, axis=0, keepdims=True)
        gidx = jnp.min(jnp.where(gl == gmax, sub, big), axis=0, keepdims=True)
        p_sel = 1.0 / jnp.sum(jnp.exp(gl - gmax), axis=0, keepdims=True)
        el = lt[0:EXPERTS_PER_GROUP]
        for g in range(1, N_EXPERT_GROUPS):
            el = jnp.where(gidx == g, lt[g * EXPERTS_PER_GROUP:(g + 1) * EXPERTS_PER_GROUP], el)
        m1 = jnp.max(el, axis=0, keepdims=True)
        t1 = jnp.min(jnp.where(el == m1, sub, big), axis=0, keepdims=True)
        el2 = jnp.where(sub == t1, neg, el)
        m2 = jnp.max(el2, axis=0, keepdims=True)
        t2 = jnp.min(jnp.where(el2 == m2, sub, big), axis=0, keepdims=True)
        e2 = jnp.exp(m2 - m1)
        w1 = p_sel / (1.0 + e2)
        w2 = p_sel * e2 / (1.0 + e2)
        i1 = gidx * EXPERTS_PER_GROUP + t1
        i2 = gidx * EXPERTS_PER_GROUP + t2

        hit1 = eid == i1
        hit2 = eid == i2
        onehot = (hit1 | hit2).astype(BF16)
        before = jnp.dot(onehot, tri_ref[...], preferred_element_type=F32) + cnt_ref[...]
        r1 = jnp.sum(jnp.where(hit1, before, 0.0), axis=0, keepdims=True)
        r2 = jnp.sum(jnp.where(hit2, before, 0.0), axis=0, keepdims=True)
        cnt_ref[...] = cnt_ref[...] + jnp.sum(onehot.astype(F32), axis=1, keepdims=True)

        ri = jnp.where(sub == 0, i1, jnp.where(sub == 1, i2, jnp.where(sub == 2, r1, jnp.where(sub == 3, r2, 0.0))))
        ri_ref[0, 0, :, rs] = ri.astype(I32)
        rw_ref[0, 0, :, rs] = jnp.where(sub == 0, w1, jnp.where(sub == 1, w2, 0.0))

    tail = ue_ref[:, tl:tl + HIST_ROWS, :]
    ue_ref[:, 0:HIST_ROWS, :] = tail
    hist_ref[...] = tail


def _rope_tables(pos0, seq, tl, dh):
    half = dh // 2
    inv = ROPE_BASE ** (-jnp.arange(half, dtype=F32) / half)
    ang_t = jnp.arange(tl, dtype=F32)[:, None] * inv[None, :]
    ang_b = (pos0 + tl * jnp.arange(seq // tl)).astype(F32)[:, None] * inv[None, :]
    dup = lambda a: jnp.concatenate([a, a], axis=-1)
    sgn = lambda a: jnp.concatenate([-a, a], axis=-1)
    base = jnp.stack([dup(jnp.cos(ang_b)), dup(jnp.sin(ang_b))], axis=1)
    base = jnp.pad(base, ((0, 0), (0, SUBLANES - base.shape[1]), (0, 0)))
    cos_t, sin_t = jnp.cos(ang_t), jnp.sin(ang_t)
    return base, dup(cos_t), dup(sin_t), sgn(cos_t), sgn(sin_t)


def _block_rows(bb, tl):
    return bb * tl // (max(1, tl // PREP_ROWS) if bb == 1 else 1)


def _layer_tables(pos0, seq, dh, *, bb, tl, chunk):
    lg = jnp.log1p(-jnp.exp2(-5.0 - jnp.arange(RET_HEADS, dtype=F32)))
    idx = jnp.arange(chunk, dtype=F32)
    diff = idx[:, None] - idx[None, :]
    d_intra = jnp.where(diff[None] >= 0, jnp.exp(jnp.maximum(diff, 0.0)[None] * lg[:, None, None]), 0.0)
    d_q = jnp.broadcast_to(jnp.exp((idx + 1.0)[None, :] * lg[:, None])[:, :, None], (RET_HEADS, chunk, dh))
    d_k = jnp.broadcast_to(jnp.exp((chunk - 1.0 - idx)[None, :] * lg[:, None])[:, :, None], (RET_HEADS, chunk, dh))
    d_c = jnp.exp(chunk * lg)
    block = _block_rows(bb, tl)
    tri = jnp.triu(jnp.ones((block, block), BF16), 1)
    return dict(rope=_rope_tables(pos0, seq, tl, dh), d_intra=d_intra, d_q=d_q, d_k=d_k, d_c=d_c, tri=tri)


def _layer_call(x, b0, nb, s0, h0, pos0, consts, tables, after, *, bb, tl, chunk):
    _, seq, d_model = x.shape
    bsz = nb
    blk0 = b0 // bb
    rows = bb * tl
    rw_width = consts["gret"].shape[-1]
    pw = consts["pscale"].shape[-1]
    dh = rw_width // RET_HEADS
    block = _block_rows(bb, tl)
    rope, d_intra, d_q, d_k, d_c, tri = (tables[k] for k in ("rope", "d_intra", "d_q", "d_k", "d_c", "tri"))

    const2 = lambda b, l, *_: (0, 0)
    const3 = lambda b, l, *_: (0, 0, 0)
    grid_spec = pltpu.PrefetchScalarGridSpec(
        num_scalar_prefetch=0,
        grid=(bsz // bb, seq // tl),
        in_specs=[
            pl.BlockSpec(memory_space=pltpu.SMEM),
            pl.BlockSpec((bb, tl, d_model), lambda b, l: (blk0 + b, l, 0)),
            pl.BlockSpec((bb, RET_HEADS, dh, dh), lambda b, l: (b, 0, 0, 0)),
            pl.BlockSpec((bb, HIST_ROWS, pw), lambda b, l: (b, 0, 0)),
            pl.BlockSpec((1, SUBLANES, dh), lambda b, l: (l, 0, 0)),
            pl.BlockSpec((tl, dh), const2),
            pl.BlockSpec((tl, dh), const2),
            pl.BlockSpec((tl, dh), const2),
            pl.BlockSpec((tl, dh), const2),
            pl.BlockSpec((RET_HEADS, chunk, chunk), const3),
            pl.BlockSpec((RET_HEADS, chunk, dh), const3),
            pl.BlockSpec((RET_HEADS, chunk, dh), const3),
            pl.BlockSpec((1, d_model), const2),
            pl.BlockSpec(consts["w_in"].shape, const2),
            pl.BlockSpec((1, rw_width), const2),
            pl.BlockSpec(consts["w_pool"].shape, const3),
            pl.BlockSpec((1, pw), const2),
            pl.BlockSpec(consts["w_out"].shape, const2),
            pl.BlockSpec((1, d_model), const2),
            pl.BlockSpec((ROUTER_ROWS, d_model), const2),
            pl.BlockSpec((block, block), const2),
            pl.BlockSpec(memory_space=pl.ANY),
        ],
        out_specs=[
            pl.BlockSpec((bb, tl, d_model), lambda b, l: (b, l, 0)),
            pl.BlockSpec((bb, tl, d_model // 2), lambda b, l: (b, l, 0)),
            pl.BlockSpec((1, 1, EXPERTS_PER_GROUP, rows), lambda b, l: (b, l, 0, 0)),
            pl.BlockSpec((1, 1, EXPERTS_PER_GROUP, rows), lambda b, l: (b, l, 0, 0)),
            pl.BlockSpec((bb, RET_HEADS, dh, dh), lambda b, l: (b, 0, 0, 0)),
            pl.BlockSpec((bb, HIST_ROWS, pw), lambda b, l: (b, 0, 0)),
            pl.BlockSpec((N_EXPERTS, block), const2),
        ],
        scratch_shapes=[
            pltpu.VMEM((bb, HIST_ROWS + tl, pw), F32),
            pltpu.VMEM((rows, rw_width), BF16),
            pltpu.VMEM((rows, rw_width), F32),
            pltpu.VMEM((rows, rw_width), BF16),
            pltpu.VMEM((rows, rw_width), F32),
            pltpu.VMEM((rows, d_model), BF16),
        ],
    )
    out_shape = [
        jax.ShapeDtypeStruct((bsz, seq, d_model), F32),
        jax.ShapeDtypeStruct((bsz, seq, d_model // 2), U32),
        jax.ShapeDtypeStruct((bsz // bb, seq // tl, EXPERTS_PER_GROUP, rows), I32),
        jax.ShapeDtypeStruct((bsz // bb, seq // tl, EXPERTS_PER_GROUP, rows), F32),
        jax.ShapeDtypeStruct((bsz, RET_HEADS, dh, dh), F32),
        jax.ShapeDtypeStruct((bsz, HIST_ROWS, pw), F32),
        jax.ShapeDtypeStruct((N_EXPERTS, block), F32),
    ]
    kern = functools.partial(_layer_kernel, bb=bb, tl=tl, chunk=chunk, pos0=pos0)
    operands = (d_c, x, s0, h0, *rope, d_intra, d_q, d_k,
                consts["g1"], consts["w_in"], consts["gret"], consts["w_pool"], consts["pscale"],
                consts["w_out"], consts["g2"], consts["wr"], tri, after)
    n_tok = bsz * seq
    mm_flops_per_token = 2 * (d_model * consts["w_in"].shape[1] + d_model * d_model + d_model * ROUTER_ROWS
                              + pw * pw // 4
                              + rw_width * (2 * chunk + 2 * dh) + N_EXPERTS * block)
    cost = pl.CostEstimate(
        flops=n_tok * mm_flops_per_token, transcendentals=n_tok * (rw_width + 2 * N_EXPERT_GROUPS),
        bytes_accessed=_nbytes(*operands) - _nbytes(x, after) + n_tok * d_model * 4 + _nbytes(*out_shape))
    return pl.pallas_call(
        kern, grid_spec=grid_spec, out_shape=out_shape, name=f"layer_pos{pos0}_b{b0}", cost_estimate=cost,
        compiler_params=pltpu.CompilerParams(
            dimension_semantics=("arbitrary", "arbitrary"), vmem_limit_bytes=VMEM_LIMIT),
    )(*operands)


def _sc_partition(n_units):
    info = plsc.get_sparse_core_info()
    nc, nw = info.num_cores, info.num_cores * info.num_subcores
    upw = -(-n_units // nw)
    upw += upw % 2
    return nc, nw, upw


def _units_by_worker(idx, n_units, upw, nw):
    idx = jnp.pad(idx.reshape(n_units, SC_UNIT), ((0, nw * upw - n_units), (0, 0)))
    return idx.reshape(upw, nw, SC_UNIT).transpose(1, 0, 2)


def _sc_dispatch(srcs, idx0, idx1, n_out_rows, after=None):
    assert 1 <= len(srcs) <= 2
    d = srcs[0].shape[1]
    dtype = srcs[0].dtype
    assert all(src.shape[0] % SC_UNIT == 0 for src in srcs)
    units_a = srcs[0].shape[0] // SC_UNIT
    n_units = sum(src.shape[0] for src in srcs) // SC_UNIT
    nc, nw, upw = _sc_partition(n_units)
    idx0 = _units_by_worker(idx0, n_units, upw, nw)
    idx1 = _units_by_worker(idx1, n_units, upw, nw)
    mesh = plsc.VectorSubcoreMesh(core_axis_name="c", subcore_axis_name="s")
    dma = pltpu.SemaphoreType.DMA
    extra = [] if after is None else [after]

    moved = n_units * SC_UNIT * d * jnp.dtype(dtype).itemsize
    @functools.partial(
        pl.kernel, mesh=mesh,
        cost_estimate=pl.CostEstimate(flops=0, transcendentals=0, bytes_accessed=3 * moved + _nbytes(idx0, idx1)),
        out_type=jax.ShapeDtypeStruct((n_out_rows, d), dtype),
        scratch_types=[
            pltpu.VMEM((upw, SC_UNIT), I32),
            pltpu.VMEM((upw, SC_UNIT), I32),
            pltpu.VMEM((SC_UNIT, d), dtype),
            pltpu.VMEM((SC_UNIT, d), dtype),
            dma, dma, dma, dma, dma, dma,
        ],
    )
    def k(*refs):
        src_hbm = refs[:len(srcs)]
        i0_hbm, i1_hbm, out_hbm, i0_v, i1_v, rows0, rows1, l0, l1, p0, p1, q0, q1 = refs[len(srcs) + len(extra):]
        wid = lax.axis_index("s") * nc + lax.axis_index("c")
        pltpu.sync_copy(i0_hbm.at[wid], i0_v)
        pltpu.sync_copy(i1_hbm.at[wid], i1_v)
        rows, lsem, psem, qsem = (rows0, rows1), (l0, l1), (p0, p1), (q0, q1)

        def live(j):
            return j * nw + wid < n_units

        def load(j, b, op):
            unit = j * nw + wid

            @pl.when(live(j) & (unit < units_a))
            def _():
                op(pltpu.make_async_copy(
                    src_hbm[0].at[pl.ds(pl.multiple_of(unit * SC_UNIT, 8), SC_UNIT)], rows[b], lsem[b]))

            if len(srcs) == 2:
                @pl.when(live(j) & (unit >= units_a))
                def _():
                    op(pltpu.make_async_copy(
                        src_hbm[1].at[pl.ds(pl.multiple_of((unit - units_a) * SC_UNIT, 8), SC_UNIT)],
                        rows[b], lsem[b]))

        def scatter(j, b, op):
            @pl.when(live(j))
            def _():
                op(pltpu.make_async_copy(rows[b], out_hbm.at[i0_v.at[j]], psem[b]))
                op(pltpu.make_async_copy(rows[b], out_hbm.at[i1_v.at[j]], qsem[b]))

        start = lambda c: c.start()
        wait = lambda c: c.wait()
        load(0, 0, start)

        @pl.loop(0, upw, step=2)
        def _(j):
            @pl.when(j > 0)
            def _():
                scatter(j - 1, 1, wait)
            load(j + 1, 1, start)
            load(j, 0, wait)
            scatter(j, 0, start)
            scatter(j, 0, wait)

            @pl.when(j + 2 < upw)
            def _():
                load(j + 2, 0, start)
            load(j + 1, 1, wait)
            scatter(j + 1, 1, start)

        scatter(upw - 1, 1, wait)

    return k(*srcs, *extra, idx0, idx1), idx1


def _sc_gather(table, idx):
    n = idx.shape[0]
    d = table.shape[1]
    assert n % SC_UNIT == 0
    n_units = n // SC_UNIT
    nc, nw, upw = _sc_partition(n_units)
    idx = _units_by_worker(idx, n_units, upw, nw)
    mesh = plsc.VectorSubcoreMesh(core_axis_name="c", subcore_axis_name="s")
    dma = pltpu.SemaphoreType.DMA

    @functools.partial(
        pl.kernel, mesh=mesh,
        cost_estimate=pl.CostEstimate(flops=0, transcendentals=0,
                                      bytes_accessed=2 * n * d * table.dtype.itemsize + _nbytes(idx)),
        out_type=jax.ShapeDtypeStruct((n, d), table.dtype),
        scratch_types=[
            pltpu.VMEM((upw, SC_UNIT), I32),
            pltpu.VMEM((SC_UNIT, d), table.dtype),
            pltpu.VMEM((SC_UNIT, d), table.dtype),
            dma, dma, dma, dma,
        ],
    )
    def k(t_hbm, i_hbm, out_hbm, i_v, rows0, rows1, g0, g1, w0, w1):
        wid = lax.axis_index("s") * nc + lax.axis_index("c")
        pltpu.sync_copy(i_hbm.at[wid], i_v)
        rows, gsem, wsem = (rows0, rows1), (g0, g1), (w0, w1)

        def live(j):
            return j * nw + wid < n_units

        def gather(j, b, op):
            @pl.when(live(j))
            def _():
                op(pltpu.make_async_copy(t_hbm.at[i_v.at[j]], rows[b], gsem[b]))

        def write(j, b, op):
            @pl.when(live(j))
            def _():
                op(pltpu.make_async_copy(
                    rows[b], out_hbm.at[pl.ds(pl.multiple_of((j * nw + wid) * SC_UNIT, 8), SC_UNIT)], wsem[b]))

        start = lambda c: c.start()
        wait = lambda c: c.wait()
        gather(0, 0, start)

        @pl.loop(0, upw, step=2)
        def _(j):
            @pl.when(j > 0)
            def _():
                write(j - 1, 1, wait)
            gather(j + 1, 1, start)
            gather(j, 0, wait)
            write(j, 0, start)
            write(j, 0, wait)

            @pl.when(j + 2 < upw)
            def _():
                gather(j + 2, 0, start)
            gather(j + 1, 1, wait)
            write(j + 1, 1, start)

        write(upw - 1, 1, wait)

    return k(table, idx)


def _moe_kernel(start0_ref, count0_ref, gtot0_ref, start1_ref, count1_ref, gtot1_ref,
                xs0_hbm, xs1_hbm, wg_ref, wu_ref, wd_ref, ys0_hbm, ys1_hbm,
                wgu_s, wd_s, xbuf0, ybuf0, xbuf1, ybuf1, sem_in0, sem_out0, sem_in1, sem_out1):
    e = pl.program_id(0)
    last = pl.num_programs(0) - 1
    hidden = wd_s.shape[0]
    half = xbuf0.shape[-1]
    segments = (
        (xs0_hbm, ys0_hbm, xbuf0, ybuf0, sem_in0, sem_out0, start0_ref[e], count0_ref[e], gtot0_ref[0]),
        (xs1_hbm, ys1_hbm, xbuf1, ybuf1, sem_in1, sem_out1, start1_ref[e], count1_ref[e], gtot1_ref[0]),
    )

    def rows_of(g):
        return pl.ds(pl.multiple_of(g * MOE_TILE, MOE_TILE), MOE_TILE)

    def pipeline(xs_hbm, ys_hbm, xbuf, ybuf, sem_in, sem_out):
        def copy_in(g):
            slot = g % MOE_BUFFERS
            return pltpu.make_async_copy(xs_hbm.at[rows_of(g)], xbuf.at[slot], sem_in.at[slot])

        def copy_out(g):
            slot = g % MOE_BUFFERS
            return pltpu.make_async_copy(ybuf.at[slot], ys_hbm.at[rows_of(g)], sem_out.at[slot])
        return copy_in, copy_out

    @pl.when(e == 0)
    def _():
        for xs_hbm, ys_hbm, xbuf, ybuf, sem_in, sem_out, _, _, g_total in segments:
            copy_in, _ = pipeline(xs_hbm, ys_hbm, xbuf, ybuf, sem_in, sem_out)
            for g in range(MOE_LOOKAHEAD):
                @pl.when(g < g_total)
                def _():
                    copy_in(g).start()

    @pl.when(segments[0][7] + segments[1][7] > 0)
    def _():
        wgu_s[:, 0:hidden] = wg_ref[0].astype(BF16)
        wgu_s[:, hidden:2 * hidden] = wu_ref[0].astype(BF16)
        wd_s[...] = wd_ref[0].astype(BF16)

    def expert_rows(xbuf, ybuf, slot, valid):
        row = lax.broadcasted_iota(I32, (MOE_TILE, half), 0)
        x_lo, x_hi = _unpack_bf16_pair(jnp.where(row < valid, xbuf[slot], jnp.uint32(0)))
        ab = (jnp.dot(x_lo.astype(BF16), wgu_s[0:half, :], preferred_element_type=F32)
              + jnp.dot(x_hi.astype(BF16), wgu_s[half:2 * half, :], preferred_element_type=F32))
        a = ab[:, 0:hidden]
        he = a * _sigmoid(a) * ab[:, hidden:2 * hidden]
        y = jnp.dot(he.astype(BF16), wd_s[...], preferred_element_type=F32)
        ybuf[slot] = _pack_bf16_pair(y[:, 0:half], y[:, half:2 * half])

    for xs_hbm, ys_hbm, xbuf, ybuf, sem_in, sem_out, start, count, g_total in segments:
        copy_in, copy_out = pipeline(xs_hbm, ys_hbm, xbuf, ybuf, sem_in, sem_out)
        g_first = start // MOE_TILE
        n_tiles = (count + MOE_TILE - 1) // MOE_TILE

        def tiles(t, width, copy_in=copy_in, copy_out=copy_out, xbuf=xbuf, ybuf=ybuf,
                  g_first=g_first, count=count, g_total=g_total):
            gs = [g_first + t + i for i in range(width)]
            for g in gs:
                @pl.when(g + MOE_LOOKAHEAD < g_total)
                def _():
                    copy_in(g + MOE_LOOKAHEAD).start()
            for g in gs:
                copy_in(g).wait()

                @pl.when(g >= MOE_BUFFERS)
                def _():
                    copy_out(g - MOE_BUFFERS).wait()
            for i, g in enumerate(gs):
                expert_rows(xbuf, ybuf, g % MOE_BUFFERS, count - (t + i) * MOE_TILE)
            for g in gs:
                copy_out(g).start()

        def pair(p, carry, tiles=tiles):
            tiles(MOE_UNROLL * p, MOE_UNROLL)
            return carry

        lax.fori_loop(0, n_tiles // MOE_UNROLL, pair, 0)

        def single(r, carry, tiles=tiles, n_tiles=n_tiles):
            tiles(n_tiles // MOE_UNROLL * MOE_UNROLL + r, 1)
            return carry

        lax.fori_loop(0, n_tiles % MOE_UNROLL, single, 0)

        @pl.when(e == last)
        def _(copy_out=copy_out, g_total=g_total):
            for j in range(1, MOE_BUFFERS + 1):
                @pl.when(g_total >= j)
                def _():
                    copy_out(g_total - j).wait()


def _moe_call(groups, w_g, w_u, w_d):
    (xs0, starts0, cnt0), (xs1, starts1, cnt1) = groups
    half = xs0.shape[1]
    n_experts, d_model, hidden = w_g.shape

    def tiles_total(starts, cnt):
        return ((starts[-1:] + cnt[-1:] + MOE_TILE - 1) // MOE_TILE).astype(I32)

    wspec = lambda shape: pl.BlockSpec(shape, lambda e, *_: (e, 0, 0))
    tile_bufs = [pltpu.VMEM((MOE_BUFFERS, MOE_TILE, half), U32)] * 4
    grid_spec = pltpu.PrefetchScalarGridSpec(
        num_scalar_prefetch=6,
        grid=(n_experts,),
        in_specs=[
            pl.BlockSpec(memory_space=pl.ANY),
            pl.BlockSpec(memory_space=pl.ANY),
            wspec((1, d_model, hidden)),
            wspec((1, d_model, hidden)),
            wspec((1, hidden, d_model)),
        ],
        out_specs=[pl.BlockSpec(memory_space=pl.ANY), pl.BlockSpec(memory_space=pl.ANY)],
        scratch_shapes=[
            pltpu.VMEM((d_model, 2 * hidden), BF16),
            pltpu.VMEM((hidden, d_model), BF16),
            *tile_bufs,
            *[pltpu.SemaphoreType.DMA((MOE_BUFFERS,))] * 4,
        ],
    )
    n_rows = xs0.shape[0] + xs1.shape[0]
    cost = pl.CostEstimate(flops=n_rows * 6 * d_model * hidden, transcendentals=n_rows * hidden,
                           bytes_accessed=2 * _nbytes(xs0, xs1) + _nbytes(w_g, w_u, w_d))
    return pl.pallas_call(
        _moe_kernel, grid_spec=grid_spec, cost_estimate=cost,
        out_shape=[jax.ShapeDtypeStruct(xs0.shape, U32), jax.ShapeDtypeStruct(xs1.shape, U32)], name="moe_experts",
        compiler_params=pltpu.CompilerParams(
            dimension_semantics=("arbitrary",), vmem_limit_bytes=VMEM_LIMIT),
    )(starts0, cnt0, tiles_total(starts0, cnt0), starts1, cnt1, tiles_total(starts1, cnt1),
      xs0, xs1, w_g, w_u, w_d)


def _combine_kernel(x1_ref, y0_ref, y1_ref, rw_ref, gf_ref, *rest):
    out_ref = rest[-1]
    tr = x1_ref.shape[0]
    w_rows = jnp.concatenate([rw_ref[0], jnp.zeros((LANES - rw_ref.shape[1], tr), F32)], axis=0)
    w_cols = w_rows.T
    w0, w1 = w_cols[:, 0:1], w_cols[:, 1:2]
    a_lo, a_hi = _unpack_bf16_pair(y0_ref[0])
    b_lo, b_hi = _unpack_bf16_pair(y1_ref[0])
    moe = jnp.concatenate([w0 * a_lo + w1 * b_lo, w0 * a_hi + w1 * b_hi], axis=-1)
    out_ref[...] = _rms(x1_ref[...] + moe, gf_ref[...])


def _combine_call(x1, rw, row0, n, yg, gf, out_rows, out_row0, prev_out=None):
    t, d_model = x1.shape
    tr = rw.shape[-1]
    half = yg.shape[-1]
    assert t % tr == 0 and row0 % tr == 0 and n % tr == 0 and rw.shape == (t // tr, EXPERTS_PER_GROUP, tr)
    assert yg.shape == (2, n, half) and out_row0 % tr == 0
    off = row0 // tr
    ooff = out_row0 // tr
    in_specs = [
        pl.BlockSpec((tr, d_model), lambda i: (off + i, 0)),
        pl.BlockSpec((1, tr, half), lambda i: (0, i, 0)),
        pl.BlockSpec((1, tr, half), lambda i: (1, i, 0)),
        pl.BlockSpec((1, EXPERTS_PER_GROUP, tr), lambda i: (off + i, 0, 0)),
        pl.BlockSpec((1, d_model), lambda i: (0, 0)),
    ]
    args = [x1, yg, yg, rw, gf]
    aliases = {}
    if prev_out is not None:
        in_specs.append(pl.BlockSpec(memory_space=pl.ANY))
        args.append(prev_out)
        aliases = {len(args) - 1: 0}
    return pl.pallas_call(
        _combine_kernel,
        grid=(n // tr,),
        in_specs=in_specs,
        out_specs=pl.BlockSpec((tr, d_model), lambda i: (ooff + i, 0)),
        out_shape=jax.ShapeDtypeStruct((out_rows, d_model), F32), name=f"combine_row{out_row0}_of{out_rows}",
        cost_estimate=pl.CostEstimate(flops=8 * n * d_model, transcendentals=n,
                                      bytes_accessed=2 * n * d_model * 4 + _nbytes(yg) + n * 4 * EXPERTS_PER_GROUP),
        input_output_aliases=aliases,
        compiler_params=pltpu.CompilerParams(
            dimension_semantics=("arbitrary",), vmem_limit_bytes=VMEM_LIMIT),
    )(*args)


def _route(streams, after=None):
    tokens = [h2.shape[0] for h2, _, _ in streams]
    counts = [cnt[:, 0].astype(I32) for _, _, cnt in streams]
    total = sum(counts)
    padded = ((total + MOE_TILE - 1) // MOE_TILE) * MOE_TILE
    starts = (jnp.cumsum(padded) - padded).astype(I32)
    experts = jnp.arange(N_EXPERTS, dtype=I32)[None, :, None]
    pos, base = [], starts
    for (_, ri, _), t, cnt in zip(streams, tokens, counts):
        ri = jnp.moveaxis(ri, 2, 0).reshape(ri.shape[2], t)
        first_row = jnp.sum(jnp.where(ri[0:2, None, :] == experts, base[None, :, None], 0), axis=1)
        pos.append(ri[2:4] + first_row)
        base = base + cnt
    pos = jnp.concatenate(pos, axis=1)
    n_rows = ((2 * sum(tokens) + N_EXPERTS * (MOE_TILE - 1)) // MOE_TILE) * MOE_TILE
    xs_sorted, ready = _sc_dispatch([h2 for h2, _, _ in streams], pos[0], pos[1], n_rows, after)
    return (xs_sorted, starts, total), pos, ready


def _gather_tokens(ys_sorted, pos, t0, n):
    return _sc_gather(ys_sorted, pos[:, t0:t0 + n].reshape(2 * n)).reshape(2, n, ys_sorted.shape[-1])


def _one_layer(xp, xs, s_ret, c_pool, norm1_g, w_in, ret_norm_g, w_pool, pool_scale, w_out, norm2_g,
               w_rg, w_re, w_g, w_u, w_d, final_g, past_len):
    bp, seq, d_model = xp.shape
    bs, dseq, _ = xs.shape
    rw_width = ret_norm_g.shape[-1]
    pw = pool_scale.shape[-1]
    dh = rw_width // RET_HEADS
    half = d_model // 2

    w_r = jnp.concatenate(
        [w_re.T, w_rg.T, jnp.zeros((ROUTER_ROWS - N_EXPERTS - N_EXPERT_GROUPS, d_model), F32)], axis=0)
    wr = w_r.astype(BF16)
    consts = dict(
        g1=norm1_g.reshape(1, d_model), w_in=w_in.astype(BF16), gret=ret_norm_g.reshape(1, rw_width),
        w_pool=w_pool.astype(BF16), pscale=pool_scale.reshape(1, pw), w_out=w_out.astype(BF16),
        g2=norm2_g.reshape(1, d_model), wr=wr)

    gf = final_g.reshape(1, d_model)

    ts = bs * dseq
    b_lead = bp - 1
    t_lead, t_rest = b_lead * seq, (bp - b_lead) * seq
    zeros = lambda nb: (jnp.zeros((nb, RET_HEADS, dh, dh), F32), jnp.zeros((nb, HIST_ROWS, pw), F32))
    h0s = jnp.pad(c_pool, ((0, 0), (HIST_ROWS - POOL_HIST, 0), (0, 0)))
    prompt_tile = dict(bb=1, tl=PROMPT_TILE, chunk=min(RET_CHUNK, seq))
    sample_tile = dict(bb=bs, tl=dseq, chunk=min(RET_CHUNK, dseq))
    prompt_tables = _layer_tables(0, seq, dh, **prompt_tile)
    sample_tables = _layer_tables(past_len, dseq, dh, **sample_tile)

    def stream(layer_out, t):
        x1, h2, ri, rw, st, hist, cnt = layer_out
        return dict(x1=x1.reshape(t, d_model), route=(h2.reshape(t, half), ri, cnt),
                    rw=rw.reshape(-1, EXPERTS_PER_GROUP, rw.shape[-1]), st=st, hist=hist)

    pa = stream(_layer_call(xp, 0, b_lead, *zeros(b_lead), 0, consts, prompt_tables, gf, **prompt_tile), t_lead)
    group0, pos0, ready0 = _route([pa["route"]])
    pb = stream(_layer_call(xp, b_lead, bp - b_lead, *zeros(bp - b_lead), 0, consts, prompt_tables, ready0,
                            **prompt_tile), t_rest)
    sm = stream(_layer_call(xs, 0, bs, s_ret, h0s, past_len, consts, sample_tables, pb["route"][2], **sample_tile),
                ts)
    group1, pos1, _ = _route([pb["route"], sm["route"]], after=group0[0])
    ys0, ys1 = _moe_call([group0, group1], w_g, w_u, w_d)

    tp = bp * seq
    yp = None
    for st, ys, pos, out0, t_stream, sizes in ((pb, ys1, pos1, t_lead, t_rest, REST_CHUNK_ROWS),
                                               (pa, ys0, pos0, 0, t_lead, LEAD_CHUNK_ROWS)):
        row0 = 0
        for size in sizes:
            n = min(size, t_stream - row0)
            if n > 0:
                yp = _combine_call(st["x1"], st["rw"], row0, n, _gather_tokens(ys, pos, row0, n), gf,
                                   tp, out0 + row0, prev_out=yp)
                row0 += n
        assert row0 == t_stream
    ysm = _combine_call(sm["x1"], sm["rw"], 0, ts, _gather_tokens(ys1, pos1, t_rest, ts), gf, ts, 0)
    st_p = jnp.concatenate([pa["st"], pb["st"]], axis=0)
    hist_p = jnp.concatenate([pa["hist"], pb["hist"]], axis=0)
    return (yp.reshape(bp, seq, d_model), ysm.reshape(bs, dseq, d_model),
            st_p, hist_p[:, HIST_ROWS - POOL_HIST:], sm["st"], sm["hist"][:, HIST_ROWS - POOL_HIST:])


def kernel(x_prompt, x_sample, state_ret, cache_pool, norm1_g, w_in, ret_norm_g, w_pool, pool_scale, w_out,
           norm2_g, w_router_group, w_router_expert, w_exp_gate, w_exp_up, w_exp_down, final_norm_g):
    depth = w_in.shape[0]
    assert depth == 1, "the final RMSNorm is fused into the layer's combine kernel"
    assert x_prompt.shape[0] >= 2 and x_prompt.shape[1] % PROMPT_TILE == 0
    yp, ys, s_p, h_p, s_s, h_s = _one_layer(
        x_prompt, x_sample, state_ret[0], cache_pool[0], norm1_g[0], w_in[0], ret_norm_g[0], w_pool[0],
        pool_scale[0], w_out[0], norm2_g[0], w_router_group[0], w_router_expert[0],
        w_exp_gate[0], w_exp_up[0], w_exp_down[0], final_norm_g, PAST_LEN)
    return (yp, ys, s_p[None], h_p[None], s_s[None], h_s[None])
```

```python
import functools

import jax
import jax.numpy as jnp
from jax import lax
from jax.experimental import pallas as pl
from jax.experimental.pallas import tpu as pltpu
from jax.experimental.pallas import tpu_sc as plsc

F32 = jnp.float32
BF16 = jnp.bfloat16
I32 = jnp.int32
U32 = jnp.uint32

EPS = 1e-6
ROPE_BASE = 10000.0
RET_HEADS = 4
POOL_WINDOWS = (2, 4, 8, 16)
POOL_HIST = max(POOL_WINDOWS) - 1
N_EXPERT_GROUPS = 4
EXPERTS_PER_GROUP = 8
N_EXPERTS = N_EXPERT_GROUPS * EXPERTS_PER_GROUP
ROUTER_ROWS = 48
PAST_LEN = 1024

LANES = 128
SUBLANES = 8
HIST_ROWS = 16
MOE_TILE = 256
MOE_BUFFERS = 16
MOE_UNROLL = 2
MOE_LOOKAHEAD = MOE_BUFFERS - MOE_UNROLL
RET_CHUNK = 256
RET_UNROLL_MAX = 16
PREP_ROWS = 512
PROMPT_TILE = 1024
REST_CHUNK_ROWS = (1 << 30,)
LEAD_CHUNK_ROWS = (8192, 1 << 30)
SC_UNIT = 32
V7X_VMEM_BYTES = 64 * 1024 * 1024
VMEM_LIMIT = V7X_VMEM_BYTES * 7 // 8


def _nbytes(*arrays):
    return sum(a.size * a.dtype.itemsize for a in arrays)


def _rms(x, g):
    return x * lax.rsqrt(jnp.mean(x * x, axis=-1, keepdims=True) + EPS) * g


def _sigmoid(x):
    return 1.0 / (1.0 + jnp.exp(-x))


def _pack_bf16_pair(lo, hi):
    lo_b = lax.bitcast_convert_type(lo.astype(BF16).astype(F32), U32)
    hi_b = lax.bitcast_convert_type(hi.astype(BF16).astype(F32), U32)
    return hi_b | (lo_b >> 16)


def _unpack_bf16_pair(p):
    lo = lax.bitcast_convert_type(p << 16, F32)
    hi = lax.bitcast_convert_type(p & jnp.uint32(0xFFFF0000), F32)
    return lo, hi


def _layer_kernel(dc_ref, x_ref, s0_ref, h0_ref, rb_ref, rc_ref, rs_ref, rcs_ref, rss_ref,
                  dintra_ref, dq_ref, dk_ref,
                  g1_ref, win_ref, gret_ref, wpool_ref, pscale_ref, wout_ref, g2_ref,
                  wr_ref, tri_ref, after_ref,
                  x1_ref, h2_ref, ri_ref, rw_ref, st_ref, hist_ref, cnt_ref,
                  ue_ref, q_ref, k_ref, v_ref, gate_ref, a_ref,
                  *, bb, tl, chunk, pos0):
    b_idx = pl.program_id(0)
    l_idx = pl.program_id(1)
    rows = bb * tl
    d_model = x_ref.shape[-1]
    rw_width = q_ref.shape[-1]
    dh = rw_width // RET_HEADS
    pw = ue_ref.shape[-1]
    gw = pw // len(POOL_WINDOWS)
    n_chunks = tl // chunk

    @pl.when(l_idx == 0)
    def _():
        st_ref[...] = s0_ref[...]
        ue_ref[:, 0:HIST_ROWS, :] = h0_ref[...]

    @pl.when((l_idx == 0) & (b_idx == 0))
    def _():
        cnt_ref[...] = jnp.zeros_like(cnt_ref)

    cos_b = rb_ref[0, 0:1, :]
    sin_b = rb_ref[0, 1:2, :]
    cosf = cos_b * rc_ref[...] - sin_b * rs_ref[...]
    sinf = sin_b * rcs_ref[...] + cos_b * rss_ref[...]
    k_scale = dh ** -0.5
    n_blocks = max(1, tl // PREP_ROWS) if bb == 1 else 1
    block = rows // n_blocks
    for blk in range(n_blocks):
        rs = slice(blk * block, (blk + 1) * block)
        xb = x_ref[0, rs, :] if bb == 1 else x_ref[...].reshape(rows, d_model)
        hb = _rms(xb, g1_ref[...]).astype(BF16)
        proj = jnp.dot(hb, win_ref[...], preferred_element_type=F32)

        def rotate(a):
            if bb == 1:
                return a * cosf[rs] + pltpu.roll(a, dh // 2, 1) * sinf[rs]
            return (a.reshape(bb, tl, dh) * cosf[None]
                    + pltpu.roll(a, dh // 2, 1).reshape(bb, tl, dh) * sinf[None]).reshape(rows, dh)

        for hh in range(RET_HEADS):
            cs = slice(hh * dh, (hh + 1) * dh)
            q_ref[rs, cs] = rotate(proj[:, hh * dh:(hh + 1) * dh]).astype(BF16)
            k_ref[rs, cs] = rotate(proj[:, rw_width + hh * dh:rw_width + (hh + 1) * dh]) * k_scale
        v_ref[rs, :] = proj[:, 2 * rw_width:3 * rw_width].astype(BF16)
        gate_ref[rs, :] = proj[:, 3 * rw_width:4 * rw_width]
        u = proj[:, 4 * rw_width:4 * rw_width + pw]
        if bb == 1:
            ue_ref[0, HIST_ROWS + blk * block:HIST_ROWS + (blk + 1) * block, :] = u
        else:
            ue_ref[:, HIST_ROWS:HIST_ROWS + tl, :] = u.reshape(bb, tl, pw)

    def ret_block(b, c):
        r0 = b * tl + c * chunk
        if not isinstance(r0, int):
            r0 = pl.multiple_of(r0, chunk)
        rr = pl.ds(r0, chunk)
        for hh in range(RET_HEADS):
            cs = slice(hh * dh, (hh + 1) * dh)
            qc = q_ref[rr, cs]
            kf = k_ref[rr, cs]
            vc = v_ref[rr, cs]
            s_old = st_ref[b, hh]
            sc = lax.dot_general(qc, kf.astype(BF16), (((1,), (1,)), ((), ())),
                                 preferred_element_type=F32) * dintra_ref[hh]
            o = (jnp.dot(sc.astype(BF16), vc, preferred_element_type=F32)
                 + dq_ref[hh] * jnp.dot(qc, s_old.astype(BF16), preferred_element_type=F32))
            kd = (kf * dk_ref[hh]).astype(BF16)
            s_new = dc_ref[hh] * s_old + lax.dot_general(
                kd, vc, (((0,), (0,)), ((), ())), preferred_element_type=F32)
            st_ref[b, hh] = s_new
            oc = o - jnp.mean(o, axis=-1, keepdims=True)
            var = jnp.mean(oc * oc, axis=-1, keepdims=True)
            y = oc * lax.rsqrt(var + EPS) * gret_ref[:, cs]
            g = gate_ref[rr, cs]
            a_ref[rr, cs] = (g * _sigmoid(g) * y).astype(BF16)

    if bb * n_chunks <= RET_UNROLL_MAX:
        for b in range(bb):
            for c in range(n_chunks):
                ret_block(b, c)
    else:
        def body(i, carry):
            ret_block(i // n_chunks, i % n_chunks)
            return carry
        lax.fori_loop(0, bb * n_chunks, body, 0)

    nt = (((1,), (1,)), ((), ()))
    neg = jnp.float32(-jnp.inf)
    big = jnp.float32(1e9)
    sub = lax.broadcasted_iota(I32, (EXPERTS_PER_GROUP, block), 0).astype(F32)
    eid = lax.broadcasted_iota(I32, (N_EXPERTS, block), 0).astype(F32)
    for blk in range(n_blocks):
        lo = blk * block
        rs = slice(lo, lo + block)

        if bb == 1:
            pos = pos0 + l_idx * tl + lo + lax.broadcasted_iota(I32, (block, 1), 0)
            window = lambda off, cs: ue_ref[0, HIST_ROWS + lo - off:HIST_ROWS + lo - off + block, cs]
        else:
            pos = pos0 + l_idx * tl + lax.broadcasted_iota(I32, (1, tl, 1), 1)
            window = lambda off, cs: ue_ref[:, HIST_ROWS - off:HIST_ROWS - off + tl, cs]
        for gi, w in enumerate(POOL_WINDOWS):
            cs = slice(gi * gw, (gi + 1) * gw)
            u_g = window(0, cs)
            acc = u_g
            for j in range(1, w):
                acc = acc + window(j, cs)
            inv_cnt = 1.0 / jnp.minimum(pos + 1, w).astype(F32)
            p = (acc * inv_cnt - u_g).reshape(block, gw)
            z = jnp.dot(p.astype(BF16), wpool_ref[gi], preferred_element_type=F32) * pscale_ref[:, cs]
            a_ref[rs, rw_width + gi * gw:rw_width + (gi + 1) * gw] = z.astype(BF16)

        xb = x_ref[0, rs, :] if bb == 1 else x_ref[...].reshape(rows, d_model)
        x1 = xb + jnp.dot(a_ref[rs, :], wout_ref[...], preferred_element_type=F32)
        h2 = _rms(x1, g2_ref[...])
        h2_packed = _pack_bf16_pair(h2[:, 0:d_model // 2], h2[:, d_model // 2:])
        if bb == 1:
            x1_ref[0, rs, :] = x1
            h2_ref[0, rs, :] = h2_packed
        else:
            x1_ref[...] = x1.reshape(bb, tl, d_model)
            h2_ref[...] = h2_packed.reshape(bb, tl, d_model // 2)

        lt = lax.dot_general(wr_ref[...], h2.astype(BF16), nt, preferred_element_type=F32)
        gl = jnp.where(sub < N_EXPERT_GROUPS, lt[N_EXPERTS:N_EXPERTS + EXPERTS_PER_GROUP], neg)
        gmax = jnp.max(gl, axis=0, keepdims=True)
        gidx = jnp.min(jnp.where(gl == gmax, sub, big), axis=0, keepdims=True)
        p_sel = 1.0 / jnp.sum(jnp.exp(gl - gmax), axis=0, keepdims=True)
        el = lt[0:EXPERTS_PER_GROUP]
        for g in range(1, N_EXPERT_GROUPS):
            el = jnp.where(gidx == g, lt[g * EXPERTS_PER_GROUP:(g + 1) * EXPERTS_PER_GROUP], el)
        m1 = jnp.max(el, axis=0, keepdims=True)
        t1 = jnp.min(jnp.where(el == m1, sub, big), axis=0, keepdims=True)
        el2 = jnp.where(sub == t1, neg, el)
        m2 = jnp.max(el2, axis=0, keepdims=True)
        t2 = jnp.min(jnp.where(el2 == m2, sub, big), axis=0, keepdims=True)
        e2 = jnp.exp(m2 - m1)
        w1 = p_sel / (1.0 + e2)
        w2 = p_sel * e2 / (1.0 + e2)
        i1 = gidx * EXPERTS_PER_GROUP + t1
        i2 = gidx * EXPERTS_PER_GROUP + t2

        hit1 = eid == i1
        hit2 = eid == i2
        onehot = (hit1 | hit2).astype(BF16)
        before = jnp.dot(onehot, tri_ref[...], preferred_element_type=F32) + cnt_ref[...]
        r1 = jnp.sum(jnp.where(hit1, before, 0.0), axis=0, keepdims=True)
        r2 = jnp.sum(jnp.where(hit2, before, 0.0), axis=0, keepdims=True)
        cnt_ref[...] = cnt_ref[...] + jnp.sum(onehot.astype(F32), axis=1, keepdims=True)

        ri = jnp.where(sub == 0, i1, jnp.where(sub == 1, i2, jnp.where(sub == 2, r1, jnp.where(sub == 3, r2, 0.0))))
        ri_ref[0, 0, :, rs] = ri.astype(I32)
        rw_ref[0, 0, :, rs] = jnp.where(sub == 0, w1, jnp.where(sub == 1, w2, 0.0))

    tail = ue_ref[:, tl:tl + HIST_ROWS, :]
    ue_ref[:, 0:HIST_ROWS, :] = tail
    hist_ref[...] = tail


def _rope_tables(pos0, seq, tl, dh):
    half = dh // 2
    inv = ROPE_BASE ** (-jnp.arange(half, dtype=F32) / half)
    ang_t = jnp.arange(tl, dtype=F32)[:, None] * inv[None, :]
    ang_b = (pos0 + tl * jnp.arange(seq // tl)).astype(F32)[:, None] * inv[None, :]
    dup = lambda a: jnp.concatenate([a, a], axis=-1)
    sgn = lambda a: jnp.concatenate([-a, a], axis=-1)
    base = jnp.stack([dup(jnp.cos(ang_b)), dup(jnp.sin(ang_b))], axis=1)
    base = jnp.pad(base, ((0, 0), (0, SUBLANES - base.shape[1]), (0, 0)))
    cos_t, sin_t = jnp.cos(ang_t), jnp.sin(ang_t)
    return base, dup(cos_t), dup(sin_t), sgn(cos_t), sgn(sin_t)


def _block_rows(bb, tl):
    return bb * tl // (max(1, tl // PREP_ROWS) if bb == 1 else 1)


def _layer_tables(pos0, seq, dh, *, bb, tl, chunk):
    lg = jnp.log1p(-jnp.exp2(-5.0 - jnp.arange(RET_HEADS, dtype=F32)))
    idx = jnp.arange(chunk, dtype=F32)
    diff = idx[:, None] - idx[None, :]
    d_intra = jnp.where(diff[None] >= 0, jnp.exp(jnp.maximum(diff, 0.0)[None] * lg[:, None, None]), 0.0)
    d_q = jnp.broadcast_to(jnp.exp((idx + 1.0)[None, :] * lg[:, None])[:, :, None], (RET_HEADS, chunk, dh))
    d_k = jnp.broadcast_to(jnp.exp((chunk - 1.0 - idx)[None, :] * lg[:, None])[:, :, None], (RET_HEADS, chunk, dh))
    d_c = jnp.exp(chunk * lg)
    block = _block_rows(bb, tl)
    tri = jnp.triu(jnp.ones((block, block), BF16), 1)
    return dict(rope=_rope_tables(pos0, seq, tl, dh), d_intra=d_intra, d_q=d_q, d_k=d_k, d_c=d_c, tri=tri)


def _layer_call(x, b0, nb, s0, h0, pos0, consts, tables, after, *, bb, tl, chunk):
    _, seq, d_model = x.shape
    bsz = nb
    blk0 = b0 // bb
    rows = bb * tl
    rw_width = consts["gret"].shape[-1]
    pw = consts["pscale"].shape[-1]
    dh = rw_width // RET_HEADS
    block = _block_rows(bb, tl)
    rope, d_intra, d_q, d_k, d_c, tri = (tables[k] for k in ("rope", "d_intra", "d_q", "d_k", "d_c", "tri"))

    const2 = lambda b, l, *_: (0, 0)
    const3 = lambda b, l, *_: (0, 0, 0)
    grid_spec = pltpu.PrefetchScalarGridSpec(
        num_scalar_prefetch=0,
        grid=(bsz // bb, seq // tl),
        in_specs=[
            pl.BlockSpec(memory_space=pltpu.SMEM),
            pl.BlockSpec((bb, tl, d_model), lambda b, l: (blk0 + b, l, 0)),
            pl.BlockSpec((bb, RET_HEADS, dh, dh), lambda b, l: (b, 0, 0, 0)),
            pl.BlockSpec((bb, HIST_ROWS, pw), lambda b, l: (b, 0, 0)),
            pl.BlockSpec((1, SUBLANES, dh), lambda b, l: (l, 0, 0)),
            pl.BlockSpec((tl, dh), const2),
            pl.BlockSpec((tl, dh), const2),
            pl.BlockSpec((tl, dh), const2),
            pl.BlockSpec((tl, dh), const2),
            pl.BlockSpec((RET_HEADS, chunk, chunk), const3),
            pl.BlockSpec((RET_HEADS, chunk, dh), const3),
            pl.BlockSpec((RET_HEADS, chunk, dh), const3),
            pl.BlockSpec((1, d_model), const2),
            pl.BlockSpec(consts["w_in"].shape, const2),
            pl.BlockSpec((1, rw_width), const2),
            pl.BlockSpec(consts["w_pool"].shape, const3),
            pl.BlockSpec((1, pw), const2),
            pl.BlockSpec(consts["w_out"].shape, const2),
            pl.BlockSpec((1, d_model), const2),
            pl.BlockSpec((ROUTER_ROWS, d_model), const2),
            pl.BlockSpec((block, block), const2),
            pl.BlockSpec(memory_space=pl.ANY),
        ],
        out_specs=[
            pl.BlockSpec((bb, tl, d_model), lambda b, l: (b, l, 0)),
            pl.BlockSpec((bb, tl, d_model // 2), lambda b, l: (b, l, 0)),
            pl.BlockSpec((1, 1, EXPERTS_PER_GROUP, rows), lambda b, l: (b, l, 0, 0)),
            pl.BlockSpec((1, 1, EXPERTS_PER_GROUP, rows), lambda b, l: (b, l, 0, 0)),
            pl.BlockSpec((bb, RET_HEADS, dh, dh), lambda b, l: (b, 0, 0, 0)),
            pl.BlockSpec((bb, HIST_ROWS, pw), lambda b, l: (b, 0, 0)),
            pl.BlockSpec((N_EXPERTS, block), const2),
        ],
        scratch_shapes=[
            pltpu.VMEM((bb, HIST_ROWS + tl, pw), F32),
            pltpu.VMEM((rows, rw_width), BF16),
            pltpu.VMEM((rows, rw_width), F32),
            pltpu.VMEM((rows, rw_width), BF16),
            pltpu.VMEM((rows, rw_width), F32),
            pltpu.VMEM((rows, d_model), BF16),
        ],
    )
    out_shape = [
        jax.ShapeDtypeStruct((bsz, seq, d_model), F32),
        jax.ShapeDtypeStruct((bsz, seq, d_model // 2), U32),
        jax.ShapeDtypeStruct((bsz // bb, seq // tl, EXPERTS_PER_GROUP, rows), I32),
        jax.ShapeDtypeStruct((bsz // bb, seq // tl, EXPERTS_PER_GROUP, rows), F32),
        jax.ShapeDtypeStruct((bsz, RET_HEADS, dh, dh), F32),
        jax.ShapeDtypeStruct((bsz, HIST_ROWS, pw), F32),
        jax.ShapeDtypeStruct((N_EXPERTS, block), F32),
    ]
    kern = functools.partial(_layer_kernel, bb=bb, tl=tl, chunk=chunk, pos0=pos0)
    operands = (d_c, x, s0, h0, *rope, d_intra, d_q, d_k,
                consts["g1"], consts["w_in"], consts["gret"], consts["w_pool"], consts["pscale"],
                consts["w_out"], consts["g2"], consts["wr"], tri, after)
    n_tok = bsz * seq
    mm_flops_per_token = 2 * (d_model * consts["w_in"].shape[1] + d_model * d_model + d_model * ROUTER_ROWS
                              + pw * pw // 4
                              + rw_width * (2 * chunk + 2 * dh) + N_EXPERTS * block)
    cost = pl.CostEstimate(
        flops=n_tok * mm_flops_per_token, transcendentals=n_tok * (rw_width + 2 * N_EXPERT_GROUPS),
        bytes_accessed=_nbytes(*operands) - _nbytes(x, after) + n_tok * d_model * 4 + _nbytes(*out_shape))
    return pl.pallas_call(
        kern, grid_spec=grid_spec, out_shape=out_shape, name=f"layer_pos{pos0}_b{b0}", cost_estimate=cost,
        compiler_params=pltpu.CompilerParams(
            dimension_semantics=("arbitrary", "arbitrary"), vmem_limit_bytes=VMEM_LIMIT),
    )(*operands)


def _sc_partition(n_units):
    info = plsc.get_sparse_core_info()
    nc, nw = info.num_cores, info.num_cores * info.num_subcores
    upw = -(-n_units // nw)
    upw += upw % 2
    return nc, nw, upw


def _units_by_worker(idx, n_units, upw, nw):
    idx = jnp.pad(idx.reshape(n_units, SC_UNIT), ((0, nw * upw - n_units), (0, 0)))
    return idx.reshape(upw, nw, SC_UNIT).transpose(1, 0, 2)


def _sc_dispatch(srcs, idx0, idx1, n_out_rows, after=None):
    assert 1 <= len(srcs) <= 2
    d = srcs[0].shape[1]
    dtype = srcs[0].dtype
    assert all(src.shape[0] % SC_UNIT == 0 for src in srcs)
    units_a = srcs[0].shape[0] // SC_UNIT
    n_units = sum(src.shape[0] for src in srcs) // SC_UNIT
    nc, nw, upw = _sc_partition(n_units)
    idx0 = _units_by_worker(idx0, n_units, upw, nw)
    idx1 = _units_by_worker(idx1, n_units, upw, nw)
    mesh = plsc.VectorSubcoreMesh(core_axis_name="c", subcore_axis_name="s")
    dma = pltpu.SemaphoreType.DMA
    extra = [] if after is None else [after]

    moved = n_units * SC_UNIT * d * jnp.dtype(dtype).itemsize
    @functools.partial(
        pl.kernel, mesh=mesh,
        cost_estimate=pl.CostEstimate(flops=0, transcendentals=0, bytes_accessed=3 * moved + _nbytes(idx0, idx1)),
        out_type=jax.ShapeDtypeStruct((n_out_rows, d), dtype),
        scratch_types=[
            pltpu.VMEM((upw, SC_UNIT), I32),
            pltpu.VMEM((upw, SC_UNIT), I32),
            pltpu.VMEM((SC_UNIT, d), dtype),
            pltpu.VMEM((SC_UNIT, d), dtype),
            dma, dma, dma, dma, dma, dma,
        ],
    )
    def k(*refs):
        src_hbm = refs[:len(srcs)]
        i0_hbm, i1_hbm, out_hbm, i0_v, i1_v, rows0, rows1, l0, l1, p0, p1, q0, q1 = refs[len(srcs) + len(extra):]
        wid = lax.axis_index("s") * nc + lax.axis_index("c")
        pltpu.sync_copy(i0_hbm.at[wid], i0_v)
        pltpu.sync_copy(i1_hbm.at[wid], i1_v)
        rows, lsem, psem, qsem = (rows0, rows1), (l0, l1), (p0, p1), (q0, q1)

        def live(j):
            return j * nw + wid < n_units

        def load(j, b, op):
            unit = j * nw + wid

            @pl.when(live(j) & (unit < units_a))
            def _():
                op(pltpu.make_async_copy(
                    src_hbm[0].at[pl.ds(pl.multiple_of(unit * SC_UNIT, 8), SC_UNIT)], rows[b], lsem[b]))

            if len(srcs) == 2:
                @pl.when(live(j) & (unit >= units_a))
                def _():
                    op(pltpu.make_async_copy(
                        src_hbm[1].at[pl.ds(pl.multiple_of((unit - units_a) * SC_UNIT, 8), SC_UNIT)],
                        rows[b], lsem[b]))

        def scatter(j, b, op):
            @pl.when(live(j))
            def _():
                op(pltpu.make_async_copy(rows[b], out_hbm.at[i0_v.at[j]], psem[b]))
                op(pltpu.make_async_copy(rows[b], out_hbm.at[i1_v.at[j]], qsem[b]))

        start = lambda c: c.start()
        wait = lambda c: c.wait()
        load(0, 0, start)

        @pl.loop(0, upw, step=2)
        def _(j):
            @pl.when(j > 0)
            def _():
                scatter(j - 1, 1, wait)
            load(j + 1, 1, start)
            load(j, 0, wait)
            scatter(j, 0, start)
            scatter(j, 0, wait)

            @pl.when(j + 2 < upw)
            def _():
                load(j + 2, 0, start)
            load(j + 1, 1, wait)
            scatter(j + 1, 1, start)

        scatter(upw - 1, 1, wait)

    return k(*srcs, *extra, idx0, idx1), idx1


def _sc_gather(table, idx):
    n = idx.shape[0]
    d = table.shape[1]
    assert n % SC_UNIT == 0
    n_units = n // SC_UNIT
    nc, nw, upw = _sc_partition(n_units)
    idx = _units_by_worker(idx, n_units, upw, nw)
    mesh = plsc.VectorSubcoreMesh(core_axis_name="c", subcore_axis_name="s")
    dma = pltpu.SemaphoreType.DMA

    @functools.partial(
        pl.kernel, mesh=mesh,
        cost_estimate=pl.CostEstimate(flops=0, transcendentals=0,
                                      bytes_accessed=2 * n * d * table.dtype.itemsize + _nbytes(idx)),
        out_type=jax.ShapeDtypeStruct((n, d), table.dtype),
        scratch_types=[
            pltpu.VMEM((upw, SC_UNIT), I32),
            pltpu.VMEM((SC_UNIT, d), table.dtype),
            pltpu.VMEM((SC_UNIT, d), table.dtype),
            dma, dma, dma, dma,
        ],
    )
    def k(t_hbm, i_hbm, out_hbm, i_v, rows0, rows1, g0, g1, w0, w1):
        wid = lax.axis_index("s") * nc + lax.axis_index("c")
        pltpu.sync_copy(i_hbm.at[wid], i_v)
        rows, gsem, wsem = (rows0, rows1), (g0, g1), (w0, w1)

        def live(j):
            return j * nw + wid < n_units

        def gather(j, b, op):
            @pl.when(live(j))
            def _():
                op(pltpu.make_async_copy(t_hbm.at[i_v.at[j]], rows[b], gsem[b]))

        def write(j, b, op):
            @pl.when(live(j))
            def _():
                op(pltpu.make_async_copy(
                    rows[b], out_hbm.at[pl.ds(pl.multiple_of((j * nw + wid) * SC_UNIT, 8), SC_UNIT)], wsem[b]))

        start = lambda c: c.start()
        wait = lambda c: c.wait()
        gather(0, 0, start)

        @pl.loop(0, upw, step=2)
        def _(j):
            @pl.when(j > 0)
            def _():
                write(j - 1, 1, wait)
            gather(j + 1, 1, start)
            gather(j, 0, wait)
            write(j, 0, start)
            write(j, 0, wait)

            @pl.when(j + 2 < upw)
            def _():
                gather(j + 2, 0, start)
            gather(j + 1, 1, wait)
            write(j + 1, 1, start)

        write(upw - 1, 1, wait)

    return k(table, idx)


def _moe_kernel(start0_ref, count0_ref, gtot0_ref, start1_ref, count1_ref, gtot1_ref,
                xs0_hbm, xs1_hbm, wg_ref, wu_ref, wd_ref, ys0_hbm, ys1_hbm,
                wgu_s, wd_s, xbuf0, ybuf0, xbuf1, ybuf1, sem_in0, sem_out0, sem_in1, sem_out1):
    e = pl.program_id(0)
    last = pl.num_programs(0) - 1
    hidden = wd_s.shape[0]
    half = xbuf0.shape[-1]
    segments = (
        (xs0_hbm, ys0_hbm, xbuf0, ybuf0, sem_in0, sem_out0, start0_ref[e], count0_ref[e], gtot0_ref[0]),
        (xs1_hbm, ys1_hbm, xbuf1, ybuf1, sem_in1, sem_out1, start1_ref[e], count1_ref[e], gtot1_ref[0]),
    )

    def rows_of(g):
        return pl.ds(pl.multiple_of(g * MOE_TILE, MOE_TILE), MOE_TILE)

    def pipeline(xs_hbm, ys_hbm, xbuf, ybuf, sem_in, sem_out):
        def copy_in(g):
            slot = g % MOE_BUFFERS
            return pltpu.make_async_copy(xs_hbm.at[rows_of(g)], xbuf.at[slot], sem_in.at[slot])

        def copy_out(g):
            slot = g % MOE_BUFFERS
            return pltpu.make_async_copy(ybuf.at[slot], ys_hbm.at[rows_of(g)], sem_out.at[slot])
        return copy_in, copy_out

    @pl.when(e == 0)
    def _():
        for xs_hbm, ys_hbm, xbuf, ybuf, sem_in, sem_out, _, _, g_total in segments:
            copy_in, _ = pipeline(xs_hbm, ys_hbm, xbuf, ybuf, sem_in, sem_out)
            for g in range(MOE_LOOKAHEAD):
                @pl.when(g < g_total)
                def _():
                    copy_in(g).start()

    @pl.when(segments[0][7] + segments[1][7] > 0)
    def _():
        wgu_s[:, 0:hidden] = wg_ref[0].astype(BF16)
        wgu_s[:, hidden:2 * hidden] = wu_ref[0].astype(BF16)
        wd_s[...] = wd_ref[0].astype(BF16)

    def expert_rows(xbuf, ybuf, slot, valid):
        row = lax.broadcasted_iota(I32, (MOE_TILE, half), 0)
        x_lo, x_hi = _unpack_bf16_pair(jnp.where(row < valid, xbuf[slot], jnp.uint32(0)))
        ab = (jnp.dot(x_lo.astype(BF16), wgu_s[0:half, :], preferred_element_type=F32)
              + jnp.dot(x_hi.astype(BF16), wgu_s[half:2 * half, :], preferred_element_type=F32))
        a = ab[:, 0:hidden]
        he = a * _sigmoid(a) * ab[:, hidden:2 * hidden]
        y = jnp.dot(he.astype(BF16), wd_s[...], preferred_element_type=F32)
        ybuf[slot] = _pack_bf16_pair(y[:, 0:half], y[:, half:2 * half])

    for xs_hbm, ys_hbm, xbuf, ybuf, sem_in, sem_out, start, count, g_total in segments:
        copy_in, copy_out = pipeline(xs_hbm, ys_hbm, xbuf, ybuf, sem_in, sem_out)
        g_first = start // MOE_TILE
        n_tiles = (count + MOE_TILE - 1) // MOE_TILE

        def tiles(t, width, copy_in=copy_in, copy_out=copy_out, xbuf=xbuf, ybuf=ybuf,
                  g_first=g_first, count=count, g_total=g_total):
            gs = [g_first + t + i for i in range(width)]
            for g in gs:
                @pl.when(g + MOE_LOOKAHEAD < g_total)
                def _():
                    copy_in(g + MOE_LOOKAHEAD).start()
            for g in gs:
                copy_in(g).wait()

                @pl.when(g >= MOE_BUFFERS)
                def _():
                    copy_out(g - MOE_BUFFERS).wait()
            for i, g in enumerate(gs):
                expert_rows(xbuf, ybuf, g % MOE_BUFFERS, count - (t + i) * MOE_TILE)
            for g in gs:
                copy_out(g).start()

        def pair(p, carry, tiles=tiles):
            tiles(MOE_UNROLL * p, MOE_UNROLL)
            return carry

        lax.fori_loop(0, n_tiles // MOE_UNROLL, pair, 0)

        def single(r, carry, tiles=tiles, n_tiles=n_tiles):
            tiles(n_tiles // MOE_UNROLL * MOE_UNROLL + r, 1)
            return carry

        lax.fori_loop(0, n_tiles % MOE_UNROLL, single, 0)

        @pl.when(e == last)
        def _(copy_out=copy_out, g_total=g_total):
            for j in range(1, MOE_BUFFERS + 1):
                @pl.when(g_total >= j)
                def _():
                    copy_out(g_total - j).wait()


def _moe_call(groups, w_g, w_u, w_d):
    (xs0, starts0, cnt0), (xs1, starts1, cnt1) = groups
    half = xs0.shape[1]
    n_experts, d_model, hidden = w_g.shape

    def tiles_total(starts, cnt):
        return ((starts[-1:] + cnt[-1:] + MOE_TILE - 1) // MOE_TILE).astype(I32)

    wspec = lambda shape: pl.BlockSpec(shape, lambda e, *_: (e, 0, 0))
    tile_bufs = [pltpu.VMEM((MOE_BUFFERS, MOE_TILE, half), U32)] * 4
    grid_spec = pltpu.PrefetchScalarGridSpec(
        num_scalar_prefetch=6,
        grid=(n_experts,),
        in_specs=[
            pl.BlockSpec(memory_space=pl.ANY),
            pl.BlockSpec(memory_space=pl.ANY),
            wspec((1, d_model, hidden)),
            wspec((1, d_model, hidden)),
            wspec((1, hidden, d_model)),
        ],
        out_specs=[pl.BlockSpec(memory_space=pl.ANY), pl.BlockSpec(memory_space=pl.ANY)],
        scratch_shapes=[
            pltpu.VMEM((d_model, 2 * hidden), BF16),
            pltpu.VMEM((hidden, d_model), BF16),
            *tile_bufs,
            *[pltpu.SemaphoreType.DMA((MOE_BUFFERS,))] * 4,
        ],
    )
    n_rows = xs0.shape[0] + xs1.shape[0]
    cost = pl.CostEstimate(flops=n_rows * 6 * d_model * hidden, transcendentals=n_rows * hidden,
                           bytes_accessed=2 * _nbytes(xs0, xs1) + _nbytes(w_g, w_u, w_d))
    return pl.pallas_call(
        _moe_kernel, grid_spec=grid_spec, cost_estimate=cost,
        out_shape=[jax.ShapeDtypeStruct(xs0.shape, U32), jax.ShapeDtypeStruct(xs1.shape, U32)], name="moe_experts",
        compiler_params=pltpu.CompilerParams(
            dimension_semantics=("arbitrary",), vmem_limit_bytes=VMEM_LIMIT),
    )(starts0, cnt0, tiles_total(starts0, cnt0), starts1, cnt1, tiles_total(starts1, cnt1),
      xs0, xs1, w_g, w_u, w_d)


def _combine_kernel(x1_ref, y0_ref, y1_ref, rw_ref, gf_ref, *rest):
    out_ref = rest[-1]
    tr = x1_ref.shape[0]
    w_rows = jnp.concatenate([rw_ref[0], jnp.zeros((LANES - rw_ref.shape[1], tr), F32)], axis=0)
    w_cols = w_rows.T
    w0, w1 = w_cols[:, 0:1], w_cols[:, 1:2]
    a_lo, a_hi = _unpack_bf16_pair(y0_ref[0])
    b_lo, b_hi = _unpack_bf16_pair(y1_ref[0])
    moe = jnp.concatenate([w0 * a_lo + w1 * b_lo, w0 * a_hi + w1 * b_hi], axis=-1)
    out_ref[...] = _rms(x1_ref[...] + moe, gf_ref[...])


def _combine_call(x1, rw, row0, n, yg, gf, out_rows, out_row0, prev_out=None):
    t, d_model = x1.shape
    tr = rw.shape[-1]
    half = yg.shape[-1]
    assert t % tr == 0 and row0 % tr == 0 and n % tr == 0 and rw.shape == (t // tr, EXPERTS_PER_GROUP, tr)
    assert yg.shape == (2, n, half) and out_row0 % tr == 0
    off = row0 // tr
    ooff = out_row0 // tr
    in_specs = [
        pl.BlockSpec((tr, d_model), lambda i: (off + i, 0)),
        pl.BlockSpec((1, tr, half), lambda i: (0, i, 0)),
        pl.BlockSpec((1, tr, half), lambda i: (1, i, 0)),
        pl.BlockSpec((1, EXPERTS_PER_GROUP, tr), lambda i: (off + i, 0, 0)),
        pl.BlockSpec((1, d_model), lambda i: (0, 0)),
    ]
    args = [x1, yg, yg, rw, gf]
    aliases = {}
    if prev_out is not None:
        in_specs.append(pl.BlockSpec(memory_space=pl.ANY))
        args.append(prev_out)
        aliases = {len(args) - 1: 0}
    return pl.pallas_call(
        _combine_kernel,
        grid=(n // tr,),
        in_specs=in_specs,
        out_specs=pl.BlockSpec((tr, d_model), lambda i: (ooff + i, 0)),
        out_shape=jax.ShapeDtypeStruct((out_rows, d_model), F32), name=f"combine_row{out_row0}_of{out_rows}",
        cost_estimate=pl.CostEstimate(flops=8 * n * d_model, transcendentals=n,
                                      bytes_accessed=2 * n * d_model * 4 + _nbytes(yg) + n * 4 * EXPERTS_PER_GROUP),
        input_output_aliases=aliases,
        compiler_params=pltpu.CompilerParams(
            dimension_semantics=("arbitrary",), vmem_limit_bytes=VMEM_LIMIT),
    )(*args)


def _route(streams, after=None):
    tokens = [h2.shape[0] for h2, _, _ in streams]
    counts = [cnt[:, 0].astype(I32) for _, _, cnt in streams]
    total = sum(counts)
    padded = ((total + MOE_TILE - 1) // MOE_TILE) * MOE_TILE
    starts = (jnp.cumsum(padded) - padded).astype(I32)
    experts = jnp.arange(N_EXPERTS, dtype=I32)[None, :, None]
    pos, base = [], starts
    for (_, ri, _), t, cnt in zip(streams, tokens, counts):
        ri = jnp.moveaxis(ri, 2, 0).reshape(ri.shape[2], t)
        first_row = jnp.sum(jnp.where(ri[0:2, None, :] == experts, base[None, :, None], 0), axis=1)
        pos.append(ri[2:4] + first_row)
        base = base + cnt
    pos = jnp.concatenate(pos, axis=1)
    n_rows = ((2 * sum(tokens) + N_EXPERTS * (MOE_TILE - 1)) // MOE_TILE) * MOE_TILE
    xs_sorted, ready = _sc_dispatch([h2 for h2, _, _ in streams], pos[0], pos[1], n_rows, after)
    return (xs_sorted, starts, total), pos, ready


def _gather_tokens(ys_sorted, pos, t0, n):
    return _sc_gather(ys_sorted, pos[:, t0:t0 + n].reshape(2 * n)).reshape(2, n, ys_sorted.shape[-1])


def _one_layer(xp, xs, s_ret, c_pool, norm1_g, w_in, ret_norm_g, w_pool, pool_scale, w_out, norm2_g,
               w_rg, w_re, w_g, w_u, w_d, final_g, past_len):
    bp, seq, d_model = xp.shape
    bs, dseq, _ = xs.shape
    rw_width = ret_norm_g.shape[-1]
    pw = pool_scale.shape[-1]
    dh = rw_width // RET_HEADS
    half = d_model // 2

    w_r = jnp.concatenate(
        [w_re.T, w_rg.T, jnp.zeros((ROUTER_ROWS - N_EXPERTS - N_EXPERT_GROUPS, d_model), F32)], axis=0)
    wr = w_r.astype(BF16)
    consts = dict(
        g1=norm1_g.reshape(1, d_model), w_in=w_in.astype(BF16), gret=ret_norm_g.reshape(1, rw_width),
        w_pool=w_pool.astype(BF16), pscale=pool_scale.reshape(1, pw), w_out=w_out.astype(BF16),
        g2=norm2_g.reshape(1, d_model), wr=wr)

    gf = final_g.reshape(1, d_model)

    ts = bs * dseq
    b_lead = bp - 1
    t_lead, t_rest = b_lead * seq, (bp - b_lead) * seq
    zeros = lambda nb: (jnp.zeros((nb, RET_HEADS, dh, dh), F32), jnp.zeros((nb, HIST_ROWS, pw), F32))
    h0s = jnp.pad(c_pool, ((0, 0), (HIST_ROWS - POOL_HIST, 0), (0, 0)))
    prompt_tile = dict(bb=1, tl=PROMPT_TILE, chunk=min(RET_CHUNK, seq))
    sample_tile = dict(bb=bs, tl=dseq, chunk=min(RET_CHUNK, dseq))
    prompt_tables = _layer_tables(0, seq, dh, **prompt_tile)
    sample_tables = _layer_tables(past_len, dseq, dh, **sample_tile)

    def stream(layer_out, t):
        x1, h2, ri, rw, st, hist, cnt = layer_out
        return dict(x1=x1.reshape(t, d_model), route=(h2.reshape(t, half), ri, cnt),
                    rw=rw.reshape(-1, EXPERTS_PER_GROUP, rw.shape[-1]), st=st, hist=hist)

    pa = stream(_layer_call(xp, 0, b_lead, *zeros(b_lead), 0, consts, prompt_tables, gf, **prompt_tile), t_lead)
    group0, pos0, ready0 = _route([pa["route"]])
    pb = stream(_layer_call(xp, b_lead, bp - b_lead, *zeros(bp - b_lead), 0, consts, prompt_tables, ready0,
                            **prompt_tile), t_rest)
    sm = stream(_layer_call(xs, 0, bs, s_ret, h0s, past_len, consts, sample_tables, pb["route"][2], **sample_tile),
                ts)
    group1, pos1, _ = _route([pb["route"], sm["route"]], after=group0[0])
    ys0, ys1 = _moe_call([group0, group1], w_g, w_u, w_d)

    tp = bp * seq
    yp = None
    for st, ys, pos, out0, t_stream, sizes in ((pb, ys1, pos1, t_lead, t_rest, REST_CHUNK_ROWS),
                                               (pa, ys0, pos0, 0, t_lead, LEAD_CHUNK_ROWS)):
        row0 = 0
        for size in sizes:
            n = min(size, t_stream - row0)
            if n > 0:
                yp = _combine_call(st["x1"], st["rw"], row0, n, _gather_tokens(ys, pos, row0, n), gf,
                                   tp, out0 + row0, prev_out=yp)
                row0 += n
        assert row0 == t_stream
    ysm = _combine_call(sm["x1"], sm["rw"], 0, ts, _gather_tokens(ys1, pos1, t_rest, ts), gf, ts, 0)
    st_p = jnp.concatenate([pa["st"], pb["st"]], axis=0)
    hist_p = jnp.concatenate([pa["hist"], pb["hist"]], axis=0)
    return (yp.reshape(bp, seq, d_model), ysm.reshape(bs, dseq, d_model),
            st_p, hist_p[:, HIST_ROWS - POOL_HIST:], sm["st"], sm["hist"][:, HIST_ROWS - POOL_HIST:])


def kernel(x_prompt, x_sample, state_ret, cache_pool, norm1_g, w_in, ret_norm_g, w_pool, pool_scale, w_out,
           norm2_g, w_router_group, w_router_expert, w_exp_gate, w_exp_up, w_exp_down, final_norm_g):
    depth = w_in.shape[0]
    assert depth == 1, "the final RMSNorm is fused into the layer's combine kernel"
    assert x_prompt.shape[0] >= 2 and x_prompt.shape[1] % PROMPT_TILE == 0
    yp, ys, s_p, h_p, s_s, h_s = _one_layer(
        x_prompt, x_sample, state_ret[0], cache_pool[0], norm1_g[0], w_in[0], ret_norm_g[0], w_pool[0],
        pool_scale[0], w_out[0], norm2_g[0], w_router_group[0], w_router_expert[0],
        w_exp_gate[0], w_exp_up[0], w_exp_down[0], final_norm_g, PAST_LEN)
    return (yp, ys, s_p[None], h_p[None], s_s[None], h_s[None])
```

```python
import functools

import jax
import jax.numpy as jnp
from jax import lax
from jax.experimental import pallas as pl
from jax.experimental.pallas import tpu as pltpu
from jax.experimental.pallas import tpu_sc as plsc

F32 = jnp.float32
BF16 = jnp.bfloat16
I32 = jnp.int32
U32 = jnp.uint32

EPS = 1e-6
ROPE_BASE = 10000.0
RET_HEADS = 4
POOL_WINDOWS = (2, 4, 8, 16)
POOL_HIST = max(POOL_WINDOWS) - 1
N_EXPERT_GROUPS = 4
EXPERTS_PER_GROUP = 8
N_EXPERTS = N_EXPERT_GROUPS * EXPERTS_PER_GROUP
ROUTER_ROWS = 48
PAST_LEN = 1024

LANES = 128
SUBLANES = 8
HIST_ROWS = 16
MOE_TILE = 256
MOE_BUFFERS = 16
TILE_DMA_PRIORITY = 1
MOE_UNROLL = 2
MOE_LOOKAHEAD = MOE_BUFFERS - MOE_UNROLL
RET_CHUNK = 256
RET_UNROLL_MAX = 16
PREP_ROWS = 512
PROMPT_TILE = 1024
REST_CHUNK_ROWS = (1 << 30,)
LEAD_CHUNK_ROWS = (8192, 1 << 30)
SC_UNIT = 32
V7X_VMEM_BYTES = 64 * 1024 * 1024
VMEM_LIMIT = V7X_VMEM_BYTES * 7 // 8


def _nbytes(*arrays):
    return sum(a.size * a.dtype.itemsize for a in arrays)


def _rms(x, g):
    return x * lax.rsqrt(jnp.mean(x * x, axis=-1, keepdims=True) + EPS) * g


def _sigmoid(x):
    return 1.0 / (1.0 + jnp.exp(-x))


def _pack_bf16_pair(lo, hi):
    lo_b = lax.bitcast_convert_type(lo.astype(BF16).astype(F32), U32)
    hi_b = lax.bitcast_convert_type(hi.astype(BF16).astype(F32), U32)
    return hi_b | (lo_b >> 16)


def _unpack_bf16_pair(p):
    lo = lax.bitcast_convert_type(p << 16, F32)
    hi = lax.bitcast_convert_type(p & jnp.uint32(0xFFFF0000), F32)
    return lo, hi


def _layer_kernel(dc_ref, x_ref, s0_ref, h0_ref, rb_ref, rc_ref, rs_ref, rcs_ref, rss_ref,
                  dintra_ref, dq_ref, dk_ref,
                  g1_ref, win_ref, gret_ref, wpool_ref, pscale_ref, wout_ref, g2_ref,
                  wr_ref, tri_ref, after_ref,
                  x1_ref, h2_ref, ri_ref, rw_ref, st_ref, hist_ref, cnt_ref,
                  ue_ref, q_ref, k_ref, v_ref, gate_ref, a_ref,
                  *, bb, tl, chunk, pos0):
    b_idx = pl.program_id(0)
    l_idx = pl.program_id(1)
    rows = bb * tl
    d_model = x_ref.shape[-1]
    rw_width = q_ref.shape[-1]
    dh = rw_width // RET_HEADS
    pw = ue_ref.shape[-1]
    gw = pw // len(POOL_WINDOWS)
    n_chunks = tl // chunk

    @pl.when(l_idx == 0)
    def _():
        st_ref[...] = s0_ref[...]
        ue_ref[:, 0:HIST_ROWS, :] = h0_ref[...]

    @pl.when((l_idx == 0) & (b_idx == 0))
    def _():
        cnt_ref[...] = jnp.zeros_like(cnt_ref)

    cos_b = rb_ref[0, 0:1, :]
    sin_b = rb_ref[0, 1:2, :]
    cosf = cos_b * rc_ref[...] - sin_b * rs_ref[...]
    sinf = sin_b * rcs_ref[...] + cos_b * rss_ref[...]
    k_scale = dh ** -0.5
    n_blocks = max(1, tl // PREP_ROWS) if bb == 1 else 1
    block = rows // n_blocks
    for blk in range(n_blocks):
        rs = slice(blk * block, (blk + 1) * block)
        xb = x_ref[0, rs, :] if bb == 1 else x_ref[...].reshape(rows, d_model)
        hb = _rms(xb, g1_ref[...]).astype(BF16)
        proj = jnp.dot(hb, win_ref[...], preferred_element_type=F32)

        def rotate(a):
            if bb == 1:
                return a * cosf[rs] + pltpu.roll(a, dh // 2, 1) * sinf[rs]
            return (a.reshape(bb, tl, dh) * cosf[None]
                    + pltpu.roll(a, dh // 2, 1).reshape(bb, tl, dh) * sinf[None]).reshape(rows, dh)

        for hh in range(RET_HEADS):
            cs = slice(hh * dh, (hh + 1) * dh)
            q_ref[rs, cs] = rotate(proj[:, hh * dh:(hh + 1) * dh]).astype(BF16)
            k_ref[rs, cs] = rotate(proj[:, rw_width + hh * dh:rw_width + (hh + 1) * dh]) * k_scale
        v_ref[rs, :] = proj[:, 2 * rw_width:3 * rw_width].astype(BF16)
        gate_ref[rs, :] = proj[:, 3 * rw_width:4 * rw_width]
        u = proj[:, 4 * rw_width:4 * rw_width + pw]
        if bb == 1:
            ue_ref[0, HIST_ROWS + blk * block:HIST_ROWS + (blk + 1) * block, :] = u
        else:
            ue_ref[:, HIST_ROWS:HIST_ROWS + tl, :] = u.reshape(bb, tl, pw)

    def ret_block(b, c):
        r0 = b * tl + c * chunk
        if not isinstance(r0, int):
            r0 = pl.multiple_of(r0, chunk)
        rr = pl.ds(r0, chunk)
        for hh in range(RET_HEADS):
            cs = slice(hh * dh, (hh + 1) * dh)
            qc = q_ref[rr, cs]
            kf = k_ref[rr, cs]
            vc = v_ref[rr, cs]
            s_old = st_ref[b, hh]
            sc = lax.dot_general(qc, kf.astype(BF16), (((1,), (1,)), ((), ())),
                                 preferred_element_type=F32) * dintra_ref[hh]
            o = (jnp.dot(sc.astype(BF16), vc, preferred_element_type=F32)
                 + dq_ref[hh] * jnp.dot(qc, s_old.astype(BF16), preferred_element_type=F32))
            kd = (kf * dk_ref[hh]).astype(BF16)
            s_new = dc_ref[hh] * s_old + lax.dot_general(
                kd, vc, (((0,), (0,)), ((), ())), preferred_element_type=F32)
            st_ref[b, hh] = s_new
            oc = o - jnp.mean(o, axis=-1, keepdims=True)
            var = jnp.mean(oc * oc, axis=-1, keepdims=True)
            y = oc * lax.rsqrt(var + EPS) * gret_ref[:, cs]
            g = gate_ref[rr, cs]
            a_ref[rr, cs] = (g * _sigmoid(g) * y).astype(BF16)

    if bb * n_chunks <= RET_UNROLL_MAX:
        for b in range(bb):
            for c in range(n_chunks):
                ret_block(b, c)
    else:
        def body(i, carry):
            ret_block(i // n_chunks, i % n_chunks)
            return carry
        lax.fori_loop(0, bb * n_chunks, body, 0)

    nt = (((1,), (1,)), ((), ()))
    neg = jnp.float32(-jnp.inf)
    big = jnp.float32(1e9)
    sub = lax.broadcasted_iota(I32, (EXPERTS_PER_GROUP, block), 0).astype(F32)
    eid = lax.broadcasted_iota(I32, (N_EXPERTS, block), 0).astype(F32)
    for blk in range(n_blocks):
        lo = blk * block
        rs = slice(lo, lo + block)

        if bb == 1:
            pos = pos0 + l_idx * tl + lo + lax.broadcasted_iota(I32, (block, 1), 0)
            window = lambda off, cs: ue_ref[0, HIST_ROWS + lo - off:HIST_ROWS + lo - off + block, cs]
        else:
            pos = pos0 + l_idx * tl + lax.broadcasted_iota(I32, (1, tl, 1), 1)
            window = lambda off, cs: ue_ref[:, HIST_ROWS - off:HIST_ROWS - off + tl, cs]
        for gi, w in enumerate(POOL_WINDOWS):
            cs = slice(gi * gw, (gi + 1) * gw)
            u_g = window(0, cs)
            acc = u_g
            for j in range(1, w):
                acc = acc + window(j, cs)
            inv_cnt = 1.0 / jnp.minimum(pos + 1, w).astype(F32)
            p = (acc * inv_cnt - u_g).reshape(block, gw)
            z = jnp.dot(p.astype(BF16), wpool_ref[gi], preferred_element_type=F32) * pscale_ref[:, cs]
            a_ref[rs, rw_width + gi * gw:rw_width + (gi + 1) * gw] = z.astype(BF16)

        xb = x_ref[0, rs, :] if bb == 1 else x_ref[...].reshape(rows, d_model)
        x1 = xb + jnp.dot(a_ref[rs, :], wout_ref[...], preferred_element_type=F32)
        h2 = _rms(x1, g2_ref[...])
        h2_packed = _pack_bf16_pair(h2[:, 0:d_model // 2], h2[:, d_model // 2:])
        if bb == 1:
            x1_ref[0, rs, :] = x1
            h2_ref[0, rs, :] = h2_packed
        else:
            x1_ref[...] = x1.reshape(bb, tl, d_model)
            h2_ref[...] = h2_packed.reshape(bb, tl, d_model // 2)

        lt = lax.dot_general(wr_ref[...], h2.astype(BF16), nt, preferred_element_type=F32)
        gl = jnp.where(sub < N_EXPERT_GROUPS, lt[N_EXPERTS:N_EXPERTS + EXPERTS_PER_GROUP], neg)
        gmax = jnp.max(gl, axis=0, keepdims=True)
        gidx = jnp.min(jnp.where(gl == gmax, sub, big), axis=0, keepdims=True)
        p_sel = 1.0 / jnp.sum(jnp.exp(gl - gmax), axis=0, keepdims=True)
        el = lt[0:EXPERTS_PER_GROUP]
        for g in range(1, N_EXPERT_GROUPS):
            el = jnp.where(gidx == g, lt[g * EXPERTS_PER_GROUP:(g + 1) * EXPERTS_PER_GROUP], el)
        m1 = jnp.max(el, axis=0, keepdims=True)
        t1 = jnp.min(jnp.where(el == m1, sub, big), axis=0, keepdims=True)
        el2 = jnp.where(sub == t1, neg, el)
        m2 = jnp.max(el2, axis=0, keepdims=True)
        t2 = jnp.min(jnp.where(el2 == m2, sub, big), axis=0, keepdims=True)
        e2 = jnp.exp(m2 - m1)
        w1 = p_sel / (1.0 + e2)
        w2 = p_sel * e2 / (1.0 + e2)
        i1 = gidx * EXPERTS_PER_GROUP + t1
        i2 = gidx * EXPERTS_PER_GROUP + t2

        hit1 = eid == i1
        hit2 = eid == i2
        onehot = (hit1 | hit2).astype(BF16)
        before = jnp.dot(onehot, tri_ref[...], preferred_element_type=F32) + cnt_ref[...]
        r1 = jnp.sum(jnp.where(hit1, before, 0.0), axis=0, keepdims=True)
        r2 = jnp.sum(jnp.where(hit2, before, 0.0), axis=0, keepdims=True)
        cnt_ref[...] = cnt_ref[...] + jnp.sum(onehot.astype(F32), axis=1, keepdims=True)

        ri = jnp.where(sub == 0, i1, jnp.where(sub == 1, i2, jnp.where(sub == 2, r1, jnp.where(sub == 3, r2, 0.0))))
        ri_ref[0, 0, :, rs] = ri.astype(I32)
        rw_ref[0, 0, :, rs] = jnp.where(sub == 0, w1, jnp.where(sub == 1, w2, 0.0))

    tail = ue_ref[:, tl:tl + HIST_ROWS, :]
    ue_ref[:, 0:HIST_ROWS, :] = tail
    hist_ref[...] = tail


def _rope_tables(pos0, seq, tl, dh):
    half = dh // 2
    inv = ROPE_BASE ** (-jnp.arange(half, dtype=F32) / half)
    ang_t = jnp.arange(tl, dtype=F32)[:, None] * inv[None, :]
    ang_b = (pos0 + tl * jnp.arange(seq // tl)).astype(F32)[:, None] * inv[None, :]
    dup = lambda a: jnp.concatenate([a, a], axis=-1)
    sgn = lambda a: jnp.concatenate([-a, a], axis=-1)
    base = jnp.stack([dup(jnp.cos(ang_b)), dup(jnp.sin(ang_b))], axis=1)
    base = jnp.pad(base, ((0, 0), (0, SUBLANES - base.shape[1]), (0, 0)))
    cos_t, sin_t = jnp.cos(ang_t), jnp.sin(ang_t)
    return base, dup(cos_t), dup(sin_t), sgn(cos_t), sgn(sin_t)


def _block_rows(bb, tl):
    return bb * tl // (max(1, tl // PREP_ROWS) if bb == 1 else 1)


def _layer_tables(pos0, seq, dh, *, bb, tl, chunk):
    lg = jnp.log1p(-jnp.exp2(-5.0 - jnp.arange(RET_HEADS, dtype=F32)))
    idx = jnp.arange(chunk, dtype=F32)
    diff = idx[:, None] - idx[None, :]
    d_intra = jnp.where(diff[None] >= 0, jnp.exp(jnp.maximum(diff, 0.0)[None] * lg[:, None, None]), 0.0)
    d_q = jnp.broadcast_to(jnp.exp((idx + 1.0)[None, :] * lg[:, None])[:, :, None], (RET_HEADS, chunk, dh))
    d_k = jnp.broadcast_to(jnp.exp((chunk - 1.0 - idx)[None, :] * lg[:, None])[:, :, None], (RET_HEADS, chunk, dh))
    d_c = jnp.exp(chunk * lg)
    block = _block_rows(bb, tl)
    tri = jnp.triu(jnp.ones((block, block), BF16), 1)
    return dict(rope=_rope_tables(pos0, seq, tl, dh), d_intra=d_intra, d_q=d_q, d_k=d_k, d_c=d_c, tri=tri)


def _layer_call(x, b0, nb, s0, h0, pos0, consts, tables, after, *, bb, tl, chunk):
    _, seq, d_model = x.shape
    bsz = nb
    blk0 = b0 // bb
    rows = bb * tl
    rw_width = consts["gret"].shape[-1]
    pw = consts["pscale"].shape[-1]
    dh = rw_width // RET_HEADS
    block = _block_rows(bb, tl)
    rope, d_intra, d_q, d_k, d_c, tri = (tables[k] for k in ("rope", "d_intra", "d_q", "d_k", "d_c", "tri"))

    const2 = lambda b, l, *_: (0, 0)
    const3 = lambda b, l, *_: (0, 0, 0)
    grid_spec = pltpu.PrefetchScalarGridSpec(
        num_scalar_prefetch=0,
        grid=(bsz // bb, seq // tl),
        in_specs=[
            pl.BlockSpec(memory_space=pltpu.SMEM),
            pl.BlockSpec((bb, tl, d_model), lambda b, l: (blk0 + b, l, 0)),
            pl.BlockSpec((bb, RET_HEADS, dh, dh), lambda b, l: (b, 0, 0, 0)),
            pl.BlockSpec((bb, HIST_ROWS, pw), lambda b, l: (b, 0, 0)),
            pl.BlockSpec((1, SUBLANES, dh), lambda b, l: (l, 0, 0)),
            pl.BlockSpec((tl, dh), const2),
            pl.BlockSpec((tl, dh), const2),
            pl.BlockSpec((tl, dh), const2),
            pl.BlockSpec((tl, dh), const2),
            pl.BlockSpec((RET_HEADS, chunk, chunk), const3),
            pl.BlockSpec((RET_HEADS, chunk, dh), const3),
            pl.BlockSpec((RET_HEADS, chunk, dh), const3),
            pl.BlockSpec((1, d_model), const2),
            pl.BlockSpec(consts["w_in"].shape, const2),
            pl.BlockSpec((1, rw_width), const2),
            pl.BlockSpec(consts["w_pool"].shape, const3),
            pl.BlockSpec((1, pw), const2),
            pl.BlockSpec(consts["w_out"].shape, const2),
            pl.BlockSpec((1, d_model), const2),
            pl.BlockSpec((ROUTER_ROWS, d_model), const2),
            pl.BlockSpec((block, block), const2),
            pl.BlockSpec(memory_space=pl.ANY),
        ],
        out_specs=[
            pl.BlockSpec((bb, tl, d_model), lambda b, l: (b, l, 0)),
            pl.BlockSpec((bb, tl, d_model // 2), lambda b, l: (b, l, 0)),
            pl.BlockSpec((1, 1, EXPERTS_PER_GROUP, rows), lambda b, l: (b, l, 0, 0)),
            pl.BlockSpec((1, 1, EXPERTS_PER_GROUP, rows), lambda b, l: (b, l, 0, 0)),
            pl.BlockSpec((bb, RET_HEADS, dh, dh), lambda b, l: (b, 0, 0, 0)),
            pl.BlockSpec((bb, HIST_ROWS, pw), lambda b, l: (b, 0, 0)),
            pl.BlockSpec((N_EXPERTS, block), const2),
        ],
        scratch_shapes=[
            pltpu.VMEM((bb, HIST_ROWS + tl, pw), F32),
            pltpu.VMEM((rows, rw_width), BF16),
            pltpu.VMEM((rows, rw_width), F32),
            pltpu.VMEM((rows, rw_width), BF16),
            pltpu.VMEM((rows, rw_width), F32),
            pltpu.VMEM((rows, d_model), BF16),
        ],
    )
    out_shape = [
        jax.ShapeDtypeStruct((bsz, seq, d_model), F32),
        jax.ShapeDtypeStruct((bsz, seq, d_model // 2), U32),
        jax.ShapeDtypeStruct((bsz // bb, seq // tl, EXPERTS_PER_GROUP, rows), I32),
        jax.ShapeDtypeStruct((bsz // bb, seq // tl, EXPERTS_PER_GROUP, rows), F32),
        jax.ShapeDtypeStruct((bsz, RET_HEADS, dh, dh), F32),
        jax.ShapeDtypeStruct((bsz, HIST_ROWS, pw), F32),
        jax.ShapeDtypeStruct((N_EXPERTS, block), F32),
    ]
    kern = functools.partial(_layer_kernel, bb=bb, tl=tl, chunk=chunk, pos0=pos0)
    operands = (d_c, x, s0, h0, *rope, d_intra, d_q, d_k,
                consts["g1"], consts["w_in"], consts["gret"], consts["w_pool"], consts["pscale"],
                consts["w_out"], consts["g2"], consts["wr"], tri, after)
    n_tok = bsz * seq
    mm_flops_per_token = 2 * (d_model * consts["w_in"].shape[1] + d_model * d_model + d_model * ROUTER_ROWS
                              + pw * pw // 4
                              + rw_width * (2 * chunk + 2 * dh) + N_EXPERTS * block)
    cost = pl.CostEstimate(
        flops=n_tok * mm_flops_per_token, transcendentals=n_tok * (rw_width + 2 * N_EXPERT_GROUPS),
        bytes_accessed=_nbytes(*operands) - _nbytes(x, after) + n_tok * d_model * 4 + _nbytes(*out_shape))
    return pl.pallas_call(
        kern, grid_spec=grid_spec, out_shape=out_shape, name=f"layer_pos{pos0}_b{b0}", cost_estimate=cost,
        compiler_params=pltpu.CompilerParams(
            dimension_semantics=("arbitrary", "arbitrary"), vmem_limit_bytes=VMEM_LIMIT),
    )(*operands)


def _sc_partition(n_units):
    info = plsc.get_sparse_core_info()
    nc, nw = info.num_cores, info.num_cores * info.num_subcores
    upw = -(-n_units // nw)
    upw += upw % 2
    return nc, nw, upw


def _units_by_worker(idx, n_units, upw, nw):
    idx = jnp.pad(idx.reshape(n_units, SC_UNIT), ((0, nw * upw - n_units), (0, 0)))
    return idx.reshape(upw, nw, SC_UNIT).transpose(1, 0, 2)


def _sc_dispatch(srcs, idx0, idx1, n_out_rows, after=None):
    assert 1 <= len(srcs) <= 2
    d = srcs[0].shape[1]
    dtype = srcs[0].dtype
    assert all(src.shape[0] % SC_UNIT == 0 for src in srcs)
    units_a = srcs[0].shape[0] // SC_UNIT
    n_units = sum(src.shape[0] for src in srcs) // SC_UNIT
    nc, nw, upw = _sc_partition(n_units)
    idx0 = _units_by_worker(idx0, n_units, upw, nw)
    idx1 = _units_by_worker(idx1, n_units, upw, nw)
    mesh = plsc.VectorSubcoreMesh(core_axis_name="c", subcore_axis_name="s")
    dma = pltpu.SemaphoreType.DMA
    extra = [] if after is None else [after]

    moved = n_units * SC_UNIT * d * jnp.dtype(dtype).itemsize
    @functools.partial(
        pl.kernel, mesh=mesh,
        cost_estimate=pl.CostEstimate(flops=0, transcendentals=0, bytes_accessed=3 * moved + _nbytes(idx0, idx1)),
        out_type=jax.ShapeDtypeStruct((n_out_rows, d), dtype),
        scratch_types=[
            pltpu.VMEM((upw, SC_UNIT), I32),
            pltpu.VMEM((upw, SC_UNIT), I32),
            pltpu.VMEM((SC_UNIT, d), dtype),
            pltpu.VMEM((SC_UNIT, d), dtype),
            dma, dma, dma, dma, dma, dma,
        ],
    )
    def k(*refs):
        src_hbm = refs[:len(srcs)]
        i0_hbm, i1_hbm, out_hbm, i0_v, i1_v, rows0, rows1, l0, l1, p0, p1, q0, q1 = refs[len(srcs) + len(extra):]
        wid = lax.axis_index("s") * nc + lax.axis_index("c")
        pltpu.sync_copy(i0_hbm.at[wid], i0_v)
        pltpu.sync_copy(i1_hbm.at[wid], i1_v)
        rows, lsem, psem, qsem = (rows0, rows1), (l0, l1), (p0, p1), (q0, q1)

        def live(j):
            return j * nw + wid < n_units

        def load(j, b, op):
            unit = j * nw + wid

            @pl.when(live(j) & (unit < units_a))
            def _():
                op(pltpu.make_async_copy(
                    src_hbm[0].at[pl.ds(pl.multiple_of(unit * SC_UNIT, 8), SC_UNIT)], rows[b], lsem[b]))

            if len(srcs) == 2:
                @pl.when(live(j) & (unit >= units_a))
                def _():
                    op(pltpu.make_async_copy(
                        src_hbm[1].at[pl.ds(pl.multiple_of((unit - units_a) * SC_UNIT, 8), SC_UNIT)],
                        rows[b], lsem[b]))

        def scatter(j, b, op):
            @pl.when(live(j))
            def _():
                op(pltpu.make_async_copy(rows[b], out_hbm.at[i0_v.at[j]], psem[b]))
                op(pltpu.make_async_copy(rows[b], out_hbm.at[i1_v.at[j]], qsem[b]))

        start = lambda c: c.start()
        wait = lambda c: c.wait()
        load(0, 0, start)

        @pl.loop(0, upw, step=2)
        def _(j):
            @pl.when(j > 0)
            def _():
                scatter(j - 1, 1, wait)
            load(j + 1, 1, start)
            load(j, 0, wait)
            scatter(j, 0, start)
            scatter(j, 0, wait)

            @pl.when(j + 2 < upw)
            def _():
                load(j + 2, 0, start)
            load(j + 1, 1, wait)
            scatter(j + 1, 1, start)

        scatter(upw - 1, 1, wait)

    return k(*srcs, *extra, idx0, idx1), idx1


def _sc_gather(table, idx):
    n = idx.shape[0]
    d = table.shape[1]
    assert n % SC_UNIT == 0
    n_units = n // SC_UNIT
    nc, nw, upw = _sc_partition(n_units)
    idx = _units_by_worker(idx, n_units, upw, nw)
    mesh = plsc.VectorSubcoreMesh(core_axis_name="c", subcore_axis_name="s")
    dma = pltpu.SemaphoreType.DMA

    @functools.partial(
        pl.kernel, mesh=mesh,
        cost_estimate=pl.CostEstimate(flops=0, transcendentals=0,
                                      bytes_accessed=2 * n * d * table.dtype.itemsize + _nbytes(idx)),
        out_type=jax.ShapeDtypeStruct((n, d), table.dtype),
        scratch_types=[
            pltpu.VMEM((upw, SC_UNIT), I32),
            pltpu.VMEM((SC_UNIT, d), table.dtype),
            pltpu.VMEM((SC_UNIT, d), table.dtype),
            dma, dma, dma, dma,
        ],
    )
    def k(t_hbm, i_hbm, out_hbm, i_v, rows0, rows1, g0, g1, w0, w1):
        wid = lax.axis_index("s") * nc + lax.axis_index("c")
        pltpu.sync_copy(i_hbm.at[wid], i_v)
        rows, gsem, wsem = (rows0, rows1), (g0, g1), (w0, w1)

        def live(j):
            return j * nw + wid < n_units

        def gather(j, b, op):
            @pl.when(live(j))
            def _():
                op(pltpu.make_async_copy(t_hbm.at[i_v.at[j]], rows[b], gsem[b]))

        def write(j, b, op):
            @pl.when(live(j))
            def _():
                op(pltpu.make_async_copy(
                    rows[b], out_hbm.at[pl.ds(pl.multiple_of((j * nw + wid) * SC_UNIT, 8), SC_UNIT)], wsem[b]))

        start = lambda c: c.start()
        wait = lambda c: c.wait()
        gather(0, 0, start)

        @pl.loop(0, upw, step=2)
        def _(j):
            @pl.when(j > 0)
            def _():
                write(j - 1, 1, wait)
            gather(j + 1, 1, start)
            gather(j, 0, wait)
            write(j, 0, start)
            write(j, 0, wait)

            @pl.when(j + 2 < upw)
            def _():
                gather(j + 2, 0, start)
            gather(j + 1, 1, wait)
            write(j + 1, 1, start)

        write(upw - 1, 1, wait)

    return k(table, idx)


def _moe_kernel(start0_ref, count0_ref, gtot0_ref, start1_ref, count1_ref, gtot1_ref,
                xs0_hbm, xs1_hbm, wg_ref, wu_ref, wd_ref, ys0_hbm, ys1_hbm,
                wgu_s, wd_s, xbuf0, ybuf0, xbuf1, ybuf1, sem_in0, sem_out0, sem_in1, sem_out1):
    e = pl.program_id(0)
    last = pl.num_programs(0) - 1
    hidden = wd_s.shape[0]
    half = xbuf0.shape[-1]
    segments = (
        (xs0_hbm, ys0_hbm, xbuf0, ybuf0, sem_in0, sem_out0, start0_ref[e], count0_ref[e], gtot0_ref[0]),
        (xs1_hbm, ys1_hbm, xbuf1, ybuf1, sem_in1, sem_out1, start1_ref[e], count1_ref[e], gtot1_ref[0]),
    )

    def rows_of(g):
        return pl.ds(pl.multiple_of(g * MOE_TILE, MOE_TILE), MOE_TILE)

    def pipeline(xs_hbm, ys_hbm, xbuf, ybuf, sem_in, sem_out):
        def copy_in(g):
            slot = g % MOE_BUFFERS
            return pltpu.make_async_copy(xs_hbm.at[rows_of(g)], xbuf.at[slot], sem_in.at[slot])

        def copy_out(g):
            slot = g % MOE_BUFFERS
            return pltpu.make_async_copy(ybuf.at[slot], ys_hbm.at[rows_of(g)], sem_out.at[slot])
        return copy_in, copy_out

    @pl.when(e == 0)
    def _():
        for xs_hbm, ys_hbm, xbuf, ybuf, sem_in, sem_out, _, _, g_total in segments:
            copy_in, _ = pipeline(xs_hbm, ys_hbm, xbuf, ybuf, sem_in, sem_out)
            for g in range(MOE_LOOKAHEAD):
                @pl.when(g < g_total)
                def _():
                    copy_in(g).start(priority=TILE_DMA_PRIORITY)

    @pl.when(segments[0][7] + segments[1][7] > 0)
    def _():
        wgu_s[:, 0:hidden] = wg_ref[0].astype(BF16)
        wgu_s[:, hidden:2 * hidden] = wu_ref[0].astype(BF16)
        wd_s[...] = wd_ref[0].astype(BF16)

    def expert_rows(xbuf, ybuf, slot, valid):
        row = lax.broadcasted_iota(I32, (MOE_TILE, half), 0)
        x_lo, x_hi = _unpack_bf16_pair(jnp.where(row < valid, xbuf[slot], jnp.uint32(0)))
        ab = (jnp.dot(x_lo.astype(BF16), wgu_s[0:half, :], preferred_element_type=F32)
              + jnp.dot(x_hi.astype(BF16), wgu_s[half:2 * half, :], preferred_element_type=F32))
        a = ab[:, 0:hidden]
        he = a * _sigmoid(a) * ab[:, hidden:2 * hidden]
        y = jnp.dot(he.astype(BF16), wd_s[...], preferred_element_type=F32)
        ybuf[slot] = _pack_bf16_pair(y[:, 0:half], y[:, half:2 * half])

    for xs_hbm, ys_hbm, xbuf, ybuf, sem_in, sem_out, start, count, g_total in segments:
        copy_in, copy_out = pipeline(xs_hbm, ys_hbm, xbuf, ybuf, sem_in, sem_out)
        g_first = start // MOE_TILE
        n_tiles = (count + MOE_TILE - 1) // MOE_TILE

        def tiles(t, width, copy_in=copy_in, copy_out=copy_out, xbuf=xbuf, ybuf=ybuf,
                  g_first=g_first, count=count, g_total=g_total):
            gs = [g_first + t + i for i in range(width)]
            for g in gs:
                @pl.when(g + MOE_LOOKAHEAD < g_total)
                def _():
                    copy_in(g + MOE_LOOKAHEAD).start(priority=TILE_DMA_PRIORITY)
            for g in gs:
                copy_in(g).wait()

                @pl.when(g >= MOE_BUFFERS)
                def _():
                    copy_out(g - MOE_BUFFERS).wait()
            for i, g in enumerate(gs):
                expert_rows(xbuf, ybuf, g % MOE_BUFFERS, count - (t + i) * MOE_TILE)
            for g in gs:
                copy_out(g).start(priority=TILE_DMA_PRIORITY)

        def pair(p, carry, tiles=tiles):
            tiles(MOE_UNROLL * p, MOE_UNROLL)
            return carry

        lax.fori_loop(0, n_tiles // MOE_UNROLL, pair, 0)

        def single(r, carry, tiles=tiles, n_tiles=n_tiles):
            tiles(n_tiles // MOE_UNROLL * MOE_UNROLL + r, 1)
            return carry

        lax.fori_loop(0, n_tiles % MOE_UNROLL, single, 0)

        @pl.when(e == last)
        def _(copy_out=copy_out, g_total=g_total):
            for j in range(1, MOE_BUFFERS + 1):
                @pl.when(g_total >= j)
                def _():
                    copy_out(g_total - j).wait()


def _moe_call(groups, w_g, w_u, w_d):
    (xs0, starts0, cnt0), (xs1, starts1, cnt1) = groups
    half = xs0.shape[1]
    n_experts, d_model, hidden = w_g.shape

    def tiles_total(starts, cnt):
        return ((starts[-1:] + cnt[-1:] + MOE_TILE - 1) // MOE_TILE).astype(I32)

    wspec = lambda shape: pl.BlockSpec(shape, lambda e, *_: (e, 0, 0))
    tile_bufs = [pltpu.VMEM((MOE_BUFFERS, MOE_TILE, half), U32)] * 4
    grid_spec = pltpu.PrefetchScalarGridSpec(
        num_scalar_prefetch=6,
        grid=(n_experts,),
        in_specs=[
            pl.BlockSpec(memory_space=pl.ANY),
            pl.BlockSpec(memory_space=pl.ANY),
            wspec((1, d_model, hidden)),
            wspec((1, d_model, hidden)),
            wspec((1, hidden, d_model)),
        ],
        out_specs=[pl.BlockSpec(memory_space=pl.ANY), pl.BlockSpec(memory_space=pl.ANY)],
        scratch_shapes=[
            pltpu.VMEM((d_model, 2 * hidden), BF16),
            pltpu.VMEM((hidden, d_model), BF16),
            *tile_bufs,
            *[pltpu.SemaphoreType.DMA((MOE_BUFFERS,))] * 4,
        ],
    )
    n_rows = xs0.shape[0] + xs1.shape[0]
    cost = pl.CostEstimate(flops=n_rows * 6 * d_model * hidden, transcendentals=n_rows * hidden,
                           bytes_accessed=2 * _nbytes(xs0, xs1) + _nbytes(w_g, w_u, w_d))
    return pl.pallas_call(
        _moe_kernel, grid_spec=grid_spec, cost_estimate=cost,
        out_shape=[jax.ShapeDtypeStruct(xs0.shape, U32), jax.ShapeDtypeStruct(xs1.shape, U32)], name="moe_experts",
        compiler_params=pltpu.CompilerParams(
            dimension_semantics=("arbitrary",), vmem_limit_bytes=VMEM_LIMIT),
    )(starts0, cnt0, tiles_total(starts0, cnt0), starts1, cnt1, tiles_total(starts1, cnt1),
      xs0, xs1, w_g, w_u, w_d)


def _combine_kernel(x1_ref, y0_ref, y1_ref, rw_ref, gf_ref, *rest):
    out_ref = rest[-1]
    tr = x1_ref.shape[0]
    w_rows = jnp.concatenate([rw_ref[0], jnp.zeros((LANES - rw_ref.shape[1], tr), F32)], axis=0)
    w_cols = w_rows.T
    w0, w1 = w_cols[:, 0:1], w_cols[:, 1:2]
    a_lo, a_hi = _unpack_bf16_pair(y0_ref[0])
    b_lo, b_hi = _unpack_bf16_pair(y1_ref[0])
    moe = jnp.concatenate([w0 * a_lo + w1 * b_lo, w0 * a_hi + w1 * b_hi], axis=-1)
    out_ref[...] = _rms(x1_ref[...] + moe, gf_ref[...])


def _combine_call(x1, rw, row0, n, yg, gf, out_rows, out_row0, prev_out=None):
    t, d_model = x1.shape
    tr = rw.shape[-1]
    half = yg.shape[-1]
    assert t % tr == 0 and row0 % tr == 0 and n % tr == 0 and rw.shape == (t // tr, EXPERTS_PER_GROUP, tr)
    assert yg.shape == (2, n, half) and out_row0 % tr == 0
    off = row0 // tr
    ooff = out_row0 // tr
    in_specs = [
        pl.BlockSpec((tr, d_model), lambda i: (off + i, 0)),
        pl.BlockSpec((1, tr, half), lambda i: (0, i, 0)),
        pl.BlockSpec((1, tr, half), lambda i: (1, i, 0)),
        pl.BlockSpec((1, EXPERTS_PER_GROUP, tr), lambda i: (off + i, 0, 0)),
        pl.BlockSpec((1, d_model), lambda i: (0, 0)),
    ]
    args = [x1, yg, yg, rw, gf]
    aliases = {}
    if prev_out is not None:
        in_specs.append(pl.BlockSpec(memory_space=pl.ANY))
        args.append(prev_out)
        aliases = {len(args) - 1: 0}
    return pl.pallas_call(
        _combine_kernel,
        grid=(n // tr,),
        in_specs=in_specs,
        out_specs=pl.BlockSpec((tr, d_model), lambda i: (ooff + i, 0)),
        out_shape=jax.ShapeDtypeStruct((out_rows, d_model), F32), name=f"combine_row{out_row0}_of{out_rows}",
        cost_estimate=pl.CostEstimate(flops=8 * n * d_model, transcendentals=n,
                                      bytes_accessed=2 * n * d_model * 4 + _nbytes(yg) + n * 4 * EXPERTS_PER_GROUP),
        input_output_aliases=aliases,
        compiler_params=pltpu.CompilerParams(
            dimension_semantics=("arbitrary",), vmem_limit_bytes=VMEM_LIMIT),
    )(*args)


def _route(streams, after=None):
    tokens = [h2.shape[0] for h2, _, _ in streams]
    counts = [cnt[:, 0].astype(I32) for _, _, cnt in streams]
    total = sum(counts)
    padded = ((total + MOE_TILE - 1) // MOE_TILE) * MOE_TILE
    starts = (jnp.cumsum(padded) - padded).astype(I32)
    experts = jnp.arange(N_EXPERTS, dtype=I32)[None, :, None]
    pos, base = [], starts
    for (_, ri, _), t, cnt in zip(streams, tokens, counts):
        ri = jnp.moveaxis(ri, 2, 0).reshape(ri.shape[2], t)
        first_row = jnp.sum(jnp.where(ri[0:2, None, :] == experts, base[None, :, None], 0), axis=1)
        pos.append(ri[2:4] + first_row)
        base = base + cnt
    pos = jnp.concatenate(pos, axis=1)
    n_rows = ((2 * sum(tokens) + N_EXPERTS * (MOE_TILE - 1)) // MOE_TILE) * MOE_TILE
    xs_sorted, ready = _sc_dispatch([h2 for h2, _, _ in streams], pos[0], pos[1], n_rows, after)
    return (xs_sorted, starts, total), pos, ready


def _gather_tokens(ys_sorted, pos, t0, n):
    return _sc_gather(ys_sorted, pos[:, t0:t0 + n].reshape(2 * n)).reshape(2, n, ys_sorted.shape[-1])


def _one_layer(xp, xs, s_ret, c_pool, norm1_g, w_in, ret_norm_g, w_pool, pool_scale, w_out, norm2_g,
               w_rg, w_re, w_g, w_u, w_d, final_g, past_len):
    bp, seq, d_model = xp.shape
    bs, dseq, _ = xs.shape
    rw_width = ret_norm_g.shape[-1]
    pw = pool_scale.shape[-1]
    dh = rw_width // RET_HEADS
    half = d_model // 2

    w_r = jnp.concatenate(
        [w_re.T, w_rg.T, jnp.zeros((ROUTER_ROWS - N_EXPERTS - N_EXPERT_GROUPS, d_model), F32)], axis=0)
    wr = w_r.astype(BF16)
    consts = dict(
        g1=norm1_g.reshape(1, d_model), w_in=w_in.astype(BF16), gret=ret_norm_g.reshape(1, rw_width),
        w_pool=w_pool.astype(BF16), pscale=pool_scale.reshape(1, pw), w_out=w_out.astype(BF16),
        g2=norm2_g.reshape(1, d_model), wr=wr)

    gf = final_g.reshape(1, d_model)

    ts = bs * dseq
    b_lead = bp - 1
    t_lead, t_rest = b_lead * seq, (bp - b_lead) * seq
    zeros = lambda nb: (jnp.zeros((nb, RET_HEADS, dh, dh), F32), jnp.zeros((nb, HIST_ROWS, pw), F32))
    h0s = jnp.pad(c_pool, ((0, 0), (HIST_ROWS - POOL_HIST, 0), (0, 0)))
    prompt_tile = dict(bb=1, tl=PROMPT_TILE, chunk=min(RET_CHUNK, seq))
    sample_tile = dict(bb=bs, tl=dseq, chunk=min(RET_CHUNK, dseq))
    prompt_tables = _layer_tables(0, seq, dh, **prompt_tile)
    sample_tables = _layer_tables(past_len, dseq, dh, **sample_tile)

    def stream(layer_out, t):
        x1, h2, ri, rw, st, hist, cnt = layer_out
        return dict(x1=x1.reshape(t, d_model), route=(h2.reshape(t, half), ri, cnt),
                    rw=rw.reshape(-1, EXPERTS_PER_GROUP, rw.shape[-1]), st=st, hist=hist)

    pa = stream(_layer_call(xp, 0, b_lead, *zeros(b_lead), 0, consts, prompt_tables, gf, **prompt_tile), t_lead)
    group0, pos0, ready0 = _route([pa["route"]])
    pb = stream(_layer_call(xp, b_lead, bp - b_lead, *zeros(bp - b_lead), 0, consts, prompt_tables, ready0,
                            **prompt_tile), t_rest)
    sm = stream(_layer_call(xs, 0, bs, s_ret, h0s, past_len, consts, sample_tables, pb["route"][2], **sample_tile),
                ts)
    group1, pos1, _ = _route([pb["route"], sm["route"]], after=group0[0])
    ys0, ys1 = _moe_call([group0, group1], w_g, w_u, w_d)

    tp = bp * seq
    yp = None
    for st, ys, pos, out0, t_stream, sizes in ((pb, ys1, pos1, t_lead, t_rest, REST_CHUNK_ROWS),
                                               (pa, ys0, pos0, 0, t_lead, LEAD_CHUNK_ROWS)):
        row0 = 0
        for size in sizes:
            n = min(size, t_stream - row0)
            if n > 0:
                yp = _combine_call(st["x1"], st["rw"], row0, n, _gather_tokens(ys, pos, row0, n), gf,
                                   tp, out0 + row0, prev_out=yp)
                row0 += n
        assert row0 == t_stream
    ysm = _combine_call(sm["x1"], sm["rw"], 0, ts, _gather_tokens(ys1, pos1, t_rest, ts), gf, ts, 0)
    st_p = jnp.concatenate([pa["st"], pb["st"]], axis=0)
    hist_p = jnp.concatenate([pa["hist"], pb["hist"]], axis=0)
    return (yp.reshape(bp, seq, d_model), ysm.reshape(bs, dseq, d_model),
            st_p, hist_p[:, HIST_ROWS - POOL_HIST:], sm["st"], sm["hist"][:, HIST_ROWS - POOL_HIST:])


def kernel(x_prompt, x_sample, state_ret, cache_pool, norm1_g, w_in, ret_norm_g, w_pool, pool_scale, w_out,
           norm2_g, w_router_group, w_router_expert, w_exp_gate, w_exp_up, w_exp_down, final_norm_g):
    depth = w_in.shape[0]
    assert depth == 1, "the final RMSNorm is fused into the layer's combine kernel"
    assert x_prompt.shape[0] >= 2 and x_prompt.shape[1] % PROMPT_TILE == 0
    yp, ys, s_p, h_p, s_s, h_s = _one_layer(
        x_prompt, x_sample, state_ret[0], cache_pool[0], norm1_g[0], w_in[0], ret_norm_g[0], w_pool[0],
        pool_scale[0], w_out[0], norm2_g[0], w_router_group[0], w_router_expert[0],
        w_exp_gate[0], w_exp_up[0], w_exp_down[0], final_norm_g, PAST_LEN)
    return (yp, ys, s_p[None], h_p[None], s_s[None], h_s[None])
```

```python
import functools

import jax
import jax.numpy as jnp
from jax import lax
from jax.experimental import pallas as pl
from jax.experimental.pallas import tpu as pltpu
from jax.experimental.pallas import tpu_sc as plsc

F32 = jnp.float32
BF16 = jnp.bfloat16
I32 = jnp.int32
U32 = jnp.uint32

EPS = 1e-6
ROPE_BASE = 10000.0
RET_HEADS = 4
POOL_WINDOWS = (2, 4, 8, 16)
POOL_HIST = max(POOL_WINDOWS) - 1
N_EXPERT_GROUPS = 4
EXPERTS_PER_GROUP = 8
N_EXPERTS = N_EXPERT_GROUPS * EXPERTS_PER_GROUP
ROUTER_ROWS = 48
PAST_LEN = 1024

LANES = 128
SUBLANES = 8
HIST_ROWS = 16
MOE_TILE = 256
MOE_BUFFERS = 16
COMBINE_BLOCKS = 2
TILE_DMA_PRIORITY = 1
MOE_UNROLL = 2
MOE_LOOKAHEAD = MOE_BUFFERS - MOE_UNROLL
RET_CHUNK = 256
RET_UNROLL_MAX = 16
PREP_ROWS = 512
PROMPT_TILE = 1024
REST_CHUNK_ROWS = (1 << 30,)
LEAD_CHUNK_ROWS = (8192, 1 << 30)
SC_UNIT = 32
V7X_VMEM_BYTES = 64 * 1024 * 1024
VMEM_LIMIT = V7X_VMEM_BYTES * 7 // 8


def _nbytes(*arrays):
    return sum(a.size * a.dtype.itemsize for a in arrays)


def _rms(x, g):
    return x * lax.rsqrt(jnp.mean(x * x, axis=-1, keepdims=True) + EPS) * g


def _sigmoid(x):
    return 1.0 / (1.0 + jnp.exp(-x))


def _pack_bf16_pair(lo, hi):
    lo_b = lax.bitcast_convert_type(lo.astype(BF16).astype(F32), U32)
    hi_b = lax.bitcast_convert_type(hi.astype(BF16).astype(F32), U32)
    return hi_b | (lo_b >> 16)


def _unpack_bf16_pair(p):
    lo = lax.bitcast_convert_type(p << 16, F32)
    hi = lax.bitcast_convert_type(p & jnp.uint32(0xFFFF0000), F32)
    return lo, hi


def _layer_kernel(dc_ref, x_ref, s0_ref, h0_ref, rb_ref, rc_ref, rs_ref, rcs_ref, rss_ref,
                  dintra_ref, dq_ref, dk_ref,
                  g1_ref, win_ref, gret_ref, wpool_ref, pscale_ref, wout_ref, g2_ref,
                  wr_ref, tri_ref, after_ref,
                  x1_ref, h2_ref, ri_ref, rw_ref, st_ref, hist_ref, cnt_ref,
                  ue_ref, q_ref, k_ref, v_ref, gate_ref, a_ref,
                  *, bb, tl, chunk, pos0):
    b_idx = pl.program_id(0)
    l_idx = pl.program_id(1)
    rows = bb * tl
    d_model = x_ref.shape[-1]
    rw_width = q_ref.shape[-1]
    dh = rw_width // RET_HEADS
    pw = ue_ref.shape[-1]
    gw = pw // len(POOL_WINDOWS)
    n_chunks = tl // chunk

    @pl.when(l_idx == 0)
    def _():
        st_ref[...] = s0_ref[...]
        ue_ref[:, 0:HIST_ROWS, :] = h0_ref[...]

    @pl.when((l_idx == 0) & (b_idx == 0))
    def _():
        cnt_ref[...] = jnp.zeros_like(cnt_ref)

    cos_b = rb_ref[0, 0:1, :]
    sin_b = rb_ref[0, 1:2, :]
    cosf = cos_b * rc_ref[...] - sin_b * rs_ref[...]
    sinf = sin_b * rcs_ref[...] + cos_b * rss_ref[...]
    k_scale = dh ** -0.5
    n_blocks = max(1, tl // PREP_ROWS) if bb == 1 else 1
    block = rows // n_blocks
    for blk in range(n_blocks):
        rs = slice(blk * block, (blk + 1) * block)
        xb = x_ref[0, rs, :] if bb == 1 else x_ref[...].reshape(rows, d_model)
        hb = _rms(xb, g1_ref[...]).astype(BF16)
        proj = jnp.dot(hb, win_ref[...], preferred_element_type=F32)

        def rotate(a):
            if bb == 1:
                return a * cosf[rs] + pltpu.roll(a, dh // 2, 1) * sinf[rs]
            return (a.reshape(bb, tl, dh) * cosf[None]
                    + pltpu.roll(a, dh // 2, 1).reshape(bb, tl, dh) * sinf[None]).reshape(rows, dh)

        for hh in range(RET_HEADS):
            cs = slice(hh * dh, (hh + 1) * dh)
            q_ref[rs, cs] = rotate(proj[:, hh * dh:(hh + 1) * dh]).astype(BF16)
            k_ref[rs, cs] = rotate(proj[:, rw_width + hh * dh:rw_width + (hh + 1) * dh]) * k_scale
        v_ref[rs, :] = proj[:, 2 * rw_width:3 * rw_width].astype(BF16)
        gate_ref[rs, :] = proj[:, 3 * rw_width:4 * rw_width]
        u = proj[:, 4 * rw_width:4 * rw_width + pw]
        if bb == 1:
            ue_ref[0, HIST_ROWS + blk * block:HIST_ROWS + (blk + 1) * block, :] = u
        else:
            ue_ref[:, HIST_ROWS:HIST_ROWS + tl, :] = u.reshape(bb, tl, pw)

    def ret_block(b, c):
        r0 = b * tl + c * chunk
        if not isinstance(r0, int):
            r0 = pl.multiple_of(r0, chunk)
        rr = pl.ds(r0, chunk)
        for hh in range(RET_HEADS):
            cs = slice(hh * dh, (hh + 1) * dh)
            qc = q_ref[rr, cs]
            kf = k_ref[rr, cs]
            vc = v_ref[rr, cs]
            s_old = st_ref[b, hh]
            sc = lax.dot_general(qc, kf.astype(BF16), (((1,), (1,)), ((), ())),
                                 preferred_element_type=F32) * dintra_ref[hh]
            o = (jnp.dot(sc.astype(BF16), vc, preferred_element_type=F32)
                 + dq_ref[hh] * jnp.dot(qc, s_old.astype(BF16), preferred_element_type=F32))
            kd = (kf * dk_ref[hh]).astype(BF16)
            s_new = dc_ref[hh] * s_old + lax.dot_general(
                kd, vc, (((0,), (0,)), ((), ())), preferred_element_type=F32)
            st_ref[b, hh] = s_new
            oc = o - jnp.mean(o, axis=-1, keepdims=True)
            var = jnp.mean(oc * oc, axis=-1, keepdims=True)
            y = oc * lax.rsqrt(var + EPS) * gret_ref[:, cs]
            g = gate_ref[rr, cs]
            a_ref[rr, cs] = (g * _sigmoid(g) * y).astype(BF16)

    if bb * n_chunks <= RET_UNROLL_MAX:
        for b in range(bb):
            for c in range(n_chunks):
                ret_block(b, c)
    else:
        def body(i, carry):
            ret_block(i // n_chunks, i % n_chunks)
            return carry
        lax.fori_loop(0, bb * n_chunks, body, 0)

    nt = (((1,), (1,)), ((), ()))
    neg = jnp.float32(-jnp.inf)
    big = jnp.float32(1e9)
    sub = lax.broadcasted_iota(I32, (EXPERTS_PER_GROUP, block), 0).astype(F32)
    eid = lax.broadcasted_iota(I32, (N_EXPERTS, block), 0).astype(F32)
    for blk in range(n_blocks):
        lo = blk * block
        rs = slice(lo, lo + block)

        if bb == 1:
            pos = pos0 + l_idx * tl + lo + lax.broadcasted_iota(I32, (block, 1), 0)
            window = lambda off, cs: ue_ref[0, HIST_ROWS + lo - off:HIST_ROWS + lo - off + block, cs]
        else:
            pos = pos0 + l_idx * tl + lax.broadcasted_iota(I32, (1, tl, 1), 1)
            window = lambda off, cs: ue_ref[:, HIST_ROWS - off:HIST_ROWS - off + tl, cs]
        for gi, w in enumerate(POOL_WINDOWS):
            cs = slice(gi * gw, (gi + 1) * gw)
            u_g = window(0, cs)
            acc = u_g
            for j in range(1, w):
                acc = acc + window(j, cs)
            inv_cnt = 1.0 / jnp.minimum(pos + 1, w).astype(F32)
            p = (acc * inv_cnt - u_g).reshape(block, gw)
            z = jnp.dot(p.astype(BF16), wpool_ref[gi], preferred_element_type=F32) * pscale_ref[:, cs]
            a_ref[rs, rw_width + gi * gw:rw_width + (gi + 1) * gw] = z.astype(BF16)

        xb = x_ref[0, rs, :] if bb == 1 else x_ref[...].reshape(rows, d_model)
        x1 = xb + jnp.dot(a_ref[rs, :], wout_ref[...], preferred_element_type=F32)
        h2 = _rms(x1, g2_ref[...])
        h2_packed = _pack_bf16_pair(h2[:, 0:d_model // 2], h2[:, d_model // 2:])
        if bb == 1:
            x1_ref[0, rs, :] = x1
            h2_ref[0, rs, :] = h2_packed
        else:
            x1_ref[...] = x1.reshape(bb, tl, d_model)
            h2_ref[...] = h2_packed.reshape(bb, tl, d_model // 2)

        lt = lax.dot_general(wr_ref[...], h2.astype(BF16), nt, preferred_element_type=F32)
        gl = jnp.where(sub < N_EXPERT_GROUPS, lt[N_EXPERTS:N_EXPERTS + EXPERTS_PER_GROUP], neg)
        gmax = jnp.max(gl, axis=0, keepdims=True)
        gidx = jnp.min(jnp.where(gl == gmax, sub, big), axis=0, keepdims=True)
        p_sel = 1.0 / jnp.sum(jnp.exp(gl - gmax), axis=0, keepdims=True)
        el = lt[0:EXPERTS_PER_GROUP]
        for g in range(1, N_EXPERT_GROUPS):
            el = jnp.where(gidx == g, lt[g * EXPERTS_PER_GROUP:(g + 1) * EXPERTS_PER_GROUP], el)
        m1 = jnp.max(el, axis=0, keepdims=True)
        t1 = jnp.min(jnp.where(el == m1, sub, big), axis=0, keepdims=True)
        el2 = jnp.where(sub == t1, neg, el)
        m2 = jnp.max(el2, axis=0, keepdims=True)
        t2 = jnp.min(jnp.where(el2 == m2, sub, big), axis=0, keepdims=True)
        e2 = jnp.exp(m2 - m1)
        w1 = p_sel / (1.0 + e2)
        w2 = p_sel * e2 / (1.0 + e2)
        i1 = gidx * EXPERTS_PER_GROUP + t1
        i2 = gidx * EXPERTS_PER_GROUP + t2

        hit1 = eid == i1
        hit2 = eid == i2
        onehot = (hit1 | hit2).astype(BF16)
        before = jnp.dot(onehot, tri_ref[...], preferred_element_type=F32) + cnt_ref[...]
        r1 = jnp.sum(jnp.where(hit1, before, 0.0), axis=0, keepdims=True)
        r2 = jnp.sum(jnp.where(hit2, before, 0.0), axis=0, keepdims=True)
        cnt_ref[...] = cnt_ref[...] + jnp.sum(onehot.astype(F32), axis=1, keepdims=True)

        ri = jnp.where(sub == 0, i1, jnp.where(sub == 1, i2, jnp.where(sub == 2, r1, jnp.where(sub == 3, r2, 0.0))))
        ri_ref[0, 0, :, rs] = ri.astype(I32)
        rw_ref[0, 0, :, rs] = jnp.where(sub == 0, w1, jnp.where(sub == 1, w2, 0.0))

    tail = ue_ref[:, tl:tl + HIST_ROWS, :]
    ue_ref[:, 0:HIST_ROWS, :] = tail
    hist_ref[...] = tail


def _rope_tables(pos0, seq, tl, dh):
    half = dh // 2
    inv = ROPE_BASE ** (-jnp.arange(half, dtype=F32) / half)
    ang_t = jnp.arange(tl, dtype=F32)[:, None] * inv[None, :]
    ang_b = (pos0 + tl * jnp.arange(seq // tl)).astype(F32)[:, None] * inv[None, :]
    dup = lambda a: jnp.concatenate([a, a], axis=-1)
    sgn = lambda a: jnp.concatenate([-a, a], axis=-1)
    base = jnp.stack([dup(jnp.cos(ang_b)), dup(jnp.sin(ang_b))], axis=1)
    base = jnp.pad(base, ((0, 0), (0, SUBLANES - base.shape[1]), (0, 0)))
    cos_t, sin_t = jnp.cos(ang_t), jnp.sin(ang_t)
    return base, dup(cos_t), dup(sin_t), sgn(cos_t), sgn(sin_t)


def _block_rows(bb, tl):
    return bb * tl // (max(1, tl // PREP_ROWS) if bb == 1 else 1)


def _layer_tables(pos0, seq, dh, *, bb, tl, chunk):
    lg = jnp.log1p(-jnp.exp2(-5.0 - jnp.arange(RET_HEADS, dtype=F32)))
    idx = jnp.arange(chunk, dtype=F32)
    diff = idx[:, None] - idx[None, :]
    d_intra = jnp.where(diff[None] >= 0, jnp.exp(jnp.maximum(diff, 0.0)[None] * lg[:, None, None]), 0.0)
    d_q = jnp.broadcast_to(jnp.exp((idx + 1.0)[None, :] * lg[:, None])[:, :, None], (RET_HEADS, chunk, dh))
    d_k = jnp.broadcast_to(jnp.exp((chunk - 1.0 - idx)[None, :] * lg[:, None])[:, :, None], (RET_HEADS, chunk, dh))
    d_c = jnp.exp(chunk * lg)
    block = _block_rows(bb, tl)
    tri = jnp.triu(jnp.ones((block, block), BF16), 1)
    return dict(rope=_rope_tables(pos0, seq, tl, dh), d_intra=d_intra, d_q=d_q, d_k=d_k, d_c=d_c, tri=tri)


def _layer_call(x, b0, nb, s0, h0, pos0, consts, tables, after, *, bb, tl, chunk):
    _, seq, d_model = x.shape
    bsz = nb
    blk0 = b0 // bb
    rows = bb * tl
    rw_width = consts["gret"].shape[-1]
    pw = consts["pscale"].shape[-1]
    dh = rw_width // RET_HEADS
    block = _block_rows(bb, tl)
    rope, d_intra, d_q, d_k, d_c, tri = (tables[k] for k in ("rope", "d_intra", "d_q", "d_k", "d_c", "tri"))

    const2 = lambda b, l, *_: (0, 0)
    const3 = lambda b, l, *_: (0, 0, 0)
    grid_spec = pltpu.PrefetchScalarGridSpec(
        num_scalar_prefetch=0,
        grid=(bsz // bb, seq // tl),
        in_specs=[
            pl.BlockSpec(memory_space=pltpu.SMEM),
            pl.BlockSpec((bb, tl, d_model), lambda b, l: (blk0 + b, l, 0)),
            pl.BlockSpec((bb, RET_HEADS, dh, dh), lambda b, l: (b, 0, 0, 0)),
            pl.BlockSpec((bb, HIST_ROWS, pw), lambda b, l: (b, 0, 0)),
            pl.BlockSpec((1, SUBLANES, dh), lambda b, l: (l, 0, 0)),
            pl.BlockSpec((tl, dh), const2),
            pl.BlockSpec((tl, dh), const2),
            pl.BlockSpec((tl, dh), const2),
            pl.BlockSpec((tl, dh), const2),
            pl.BlockSpec((RET_HEADS, chunk, chunk), const3),
            pl.BlockSpec((RET_HEADS, chunk, dh), const3),
            pl.BlockSpec((RET_HEADS, chunk, dh), const3),
            pl.BlockSpec((1, d_model), const2),
            pl.BlockSpec(consts["w_in"].shape, const2),
            pl.BlockSpec((1, rw_width), const2),
            pl.BlockSpec(consts["w_pool"].shape, const3),
            pl.BlockSpec((1, pw), const2),
            pl.BlockSpec(consts["w_out"].shape, const2),
            pl.BlockSpec((1, d_model), const2),
            pl.BlockSpec((ROUTER_ROWS, d_model), const2),
            pl.BlockSpec((block, block), const2),
            pl.BlockSpec(memory_space=pl.ANY),
        ],
        out_specs=[
            pl.BlockSpec((bb, tl, d_model), lambda b, l: (b, l, 0)),
            pl.BlockSpec((bb, tl, d_model // 2), lambda b, l: (b, l, 0)),
            pl.BlockSpec((1, 1, EXPERTS_PER_GROUP, rows), lambda b, l: (b, l, 0, 0)),
            pl.BlockSpec((1, 1, EXPERTS_PER_GROUP, rows), lambda b, l: (b, l, 0, 0)),
            pl.BlockSpec((bb, RET_HEADS, dh, dh), lambda b, l: (b, 0, 0, 0)),
            pl.BlockSpec((bb, HIST_ROWS, pw), lambda b, l: (b, 0, 0)),
            pl.BlockSpec((N_EXPERTS, block), const2),
        ],
        scratch_shapes=[
            pltpu.VMEM((bb, HIST_ROWS + tl, pw), F32),
            pltpu.VMEM((rows, rw_width), BF16),
            pltpu.VMEM((rows, rw_width), F32),
            pltpu.VMEM((rows, rw_width), BF16),
            pltpu.VMEM((rows, rw_width), F32),
            pltpu.VMEM((rows, d_model), BF16),
        ],
    )
    out_shape = [
        jax.ShapeDtypeStruct((bsz, seq, d_model), F32),
        jax.ShapeDtypeStruct((bsz, seq, d_model // 2), U32),
        jax.ShapeDtypeStruct((bsz // bb, seq // tl, EXPERTS_PER_GROUP, rows), I32),
        jax.ShapeDtypeStruct((bsz // bb, seq // tl, EXPERTS_PER_GROUP, rows), F32),
        jax.ShapeDtypeStruct((bsz, RET_HEADS, dh, dh), F32),
        jax.ShapeDtypeStruct((bsz, HIST_ROWS, pw), F32),
        jax.ShapeDtypeStruct((N_EXPERTS, block), F32),
    ]
    kern = functools.partial(_layer_kernel, bb=bb, tl=tl, chunk=chunk, pos0=pos0)
    operands = (d_c, x, s0, h0, *rope, d_intra, d_q, d_k,
                consts["g1"], consts["w_in"], consts["gret"], consts["w_pool"], consts["pscale"],
                consts["w_out"], consts["g2"], consts["wr"], tri, after)
    n_tok = bsz * seq
    mm_flops_per_token = 2 * (d_model * consts["w_in"].shape[1] + d_model * d_model + d_model * ROUTER_ROWS
                              + pw * pw // 4
                              + rw_width * (2 * chunk + 2 * dh) + N_EXPERTS * block)
    cost = pl.CostEstimate(
        flops=n_tok * mm_flops_per_token, transcendentals=n_tok * (rw_width + 2 * N_EXPERT_GROUPS),
        bytes_accessed=_nbytes(*operands) - _nbytes(x, after) + n_tok * d_model * 4 + _nbytes(*out_shape))
    return pl.pallas_call(
        kern, grid_spec=grid_spec, out_shape=out_shape, name=f"layer_pos{pos0}_b{b0}", cost_estimate=cost,
        compiler_params=pltpu.CompilerParams(
            dimension_semantics=("arbitrary", "arbitrary"), vmem_limit_bytes=VMEM_LIMIT),
    )(*operands)


def _sc_partition(n_units):
    info = plsc.get_sparse_core_info()
    nc, nw = info.num_cores, info.num_cores * info.num_subcores
    upw = -(-n_units // nw)
    upw += upw % 2
    return nc, nw, upw


def _units_by_worker(idx, n_units, upw, nw):
    idx = jnp.pad(idx.reshape(n_units, SC_UNIT), ((0, nw * upw - n_units), (0, 0)))
    return idx.reshape(upw, nw, SC_UNIT).transpose(1, 0, 2)


def _sc_dispatch(srcs, idx0, idx1, n_out_rows, after=None):
    assert 1 <= len(srcs) <= 2
    d = srcs[0].shape[1]
    dtype = srcs[0].dtype
    assert all(src.shape[0] % SC_UNIT == 0 for src in srcs)
    units_a = srcs[0].shape[0] // SC_UNIT
    n_units = sum(src.shape[0] for src in srcs) // SC_UNIT
    nc, nw, upw = _sc_partition(n_units)
    idx0 = _units_by_worker(idx0, n_units, upw, nw)
    idx1 = _units_by_worker(idx1, n_units, upw, nw)
    mesh = plsc.VectorSubcoreMesh(core_axis_name="c", subcore_axis_name="s")
    dma = pltpu.SemaphoreType.DMA
    extra = [] if after is None else [after]

    moved = n_units * SC_UNIT * d * jnp.dtype(dtype).itemsize
    @functools.partial(
        pl.kernel, mesh=mesh,
        cost_estimate=pl.CostEstimate(flops=0, transcendentals=0, bytes_accessed=3 * moved + _nbytes(idx0, idx1)),
        out_type=jax.ShapeDtypeStruct((n_out_rows, d), dtype),
        scratch_types=[
            pltpu.VMEM((upw, SC_UNIT), I32),
            pltpu.VMEM((upw, SC_UNIT), I32),
            pltpu.VMEM((SC_UNIT, d), dtype),
            pltpu.VMEM((SC_UNIT, d), dtype),
            dma, dma, dma, dma, dma, dma,
        ],
    )
    def k(*refs):
        src_hbm = refs[:len(srcs)]
        i0_hbm, i1_hbm, out_hbm, i0_v, i1_v, rows0, rows1, l0, l1, p0, p1, q0, q1 = refs[len(srcs) + len(extra):]
        wid = lax.axis_index("s") * nc + lax.axis_index("c")
        pltpu.sync_copy(i0_hbm.at[wid], i0_v)
        pltpu.sync_copy(i1_hbm.at[wid], i1_v)
        rows, lsem, psem, qsem = (rows0, rows1), (l0, l1), (p0, p1), (q0, q1)

        def live(j):
            return j * nw + wid < n_units

        def load(j, b, op):
            unit = j * nw + wid

            @pl.when(live(j) & (unit < units_a))
            def _():
                op(pltpu.make_async_copy(
                    src_hbm[0].at[pl.ds(pl.multiple_of(unit * SC_UNIT, 8), SC_UNIT)], rows[b], lsem[b]))

            if len(srcs) == 2:
                @pl.when(live(j) & (unit >= units_a))
                def _():
                    op(pltpu.make_async_copy(
                        src_hbm[1].at[pl.ds(pl.multiple_of((unit - units_a) * SC_UNIT, 8), SC_UNIT)],
                        rows[b], lsem[b]))

        def scatter(j, b, op):
            @pl.when(live(j))
            def _():
                op(pltpu.make_async_copy(rows[b], out_hbm.at[i0_v.at[j]], psem[b]))
                op(pltpu.make_async_copy(rows[b], out_hbm.at[i1_v.at[j]], qsem[b]))

        start = lambda c: c.start()
        wait = lambda c: c.wait()
        load(0, 0, start)

        @pl.loop(0, upw, step=2)
        def _(j):
            @pl.when(j > 0)
            def _():
                scatter(j - 1, 1, wait)
            load(j + 1, 1, start)
            load(j, 0, wait)
            scatter(j, 0, start)
            scatter(j, 0, wait)

            @pl.when(j + 2 < upw)
            def _():
                load(j + 2, 0, start)
            load(j + 1, 1, wait)
            scatter(j + 1, 1, start)

        scatter(upw - 1, 1, wait)

    return k(*srcs, *extra, idx0, idx1), idx1


def _sc_gather(table, idx):
    n = idx.shape[0]
    d = table.shape[1]
    assert n % SC_UNIT == 0
    n_units = n // SC_UNIT
    nc, nw, upw = _sc_partition(n_units)
    idx = _units_by_worker(idx, n_units, upw, nw)
    mesh = plsc.VectorSubcoreMesh(core_axis_name="c", subcore_axis_name="s")
    dma = pltpu.SemaphoreType.DMA

    @functools.partial(
        pl.kernel, mesh=mesh,
        cost_estimate=pl.CostEstimate(flops=0, transcendentals=0,
                                      bytes_accessed=2 * n * d * table.dtype.itemsize + _nbytes(idx)),
        out_type=jax.ShapeDtypeStruct((n, d), table.dtype),
        scratch_types=[
            pltpu.VMEM((upw, SC_UNIT), I32),
            pltpu.VMEM((SC_UNIT, d), table.dtype),
            pltpu.VMEM((SC_UNIT, d), table.dtype),
            dma, dma, dma, dma,
        ],
    )
    def k(t_hbm, i_hbm, out_hbm, i_v, rows0, rows1, g0, g1, w0, w1):
        wid = lax.axis_index("s") * nc + lax.axis_index("c")
        pltpu.sync_copy(i_hbm.at[wid], i_v)
        rows, gsem, wsem = (rows0, rows1), (g0, g1), (w0, w1)

        def live(j):
            return j * nw + wid < n_units

        def gather(j, b, op):
            @pl.when(live(j))
            def _():
                op(pltpu.make_async_copy(t_hbm.at[i_v.at[j]], rows[b], gsem[b]))

        def write(j, b, op):
            @pl.when(live(j))
            def _():
                op(pltpu.make_async_copy(
                    rows[b], out_hbm.at[pl.ds(pl.multiple_of((j * nw + wid) * SC_UNIT, 8), SC_UNIT)], wsem[b]))

        start = lambda c: c.start()
        wait = lambda c: c.wait()
        gather(0, 0, start)

        @pl.loop(0, upw, step=2)
        def _(j):
            @pl.when(j > 0)
            def _():
                write(j - 1, 1, wait)
            gather(j + 1, 1, start)
            gather(j, 0, wait)
            write(j, 0, start)
            write(j, 0, wait)

            @pl.when(j + 2 < upw)
            def _():
                gather(j + 2, 0, start)
            gather(j + 1, 1, wait)
            write(j + 1, 1, start)

        write(upw - 1, 1, wait)

    return k(table, idx)


def _moe_kernel(start0_ref, count0_ref, gtot0_ref, start1_ref, count1_ref, gtot1_ref,
                xs0_hbm, xs1_hbm, wg_ref, wu_ref, wd_ref, ys0_hbm, ys1_hbm,
                wgu_s, wd_s, xbuf0, ybuf0, xbuf1, ybuf1, sem_in0, sem_out0, sem_in1, sem_out1):
    e = pl.program_id(0)
    last = pl.num_programs(0) - 1
    hidden = wd_s.shape[0]
    half = xbuf0.shape[-1]
    segments = (
        (xs0_hbm, ys0_hbm, xbuf0, ybuf0, sem_in0, sem_out0, start0_ref[e], count0_ref[e], gtot0_ref[0]),
        (xs1_hbm, ys1_hbm, xbuf1, ybuf1, sem_in1, sem_out1, start1_ref[e], count1_ref[e], gtot1_ref[0]),
    )

    def rows_of(g):
        return pl.ds(pl.multiple_of(g * MOE_TILE, MOE_TILE), MOE_TILE)

    def pipeline(xs_hbm, ys_hbm, xbuf, ybuf, sem_in, sem_out):
        def copy_in(g):
            slot = g % MOE_BUFFERS
            return pltpu.make_async_copy(xs_hbm.at[rows_of(g)], xbuf.at[slot], sem_in.at[slot])

        def copy_out(g):
            slot = g % MOE_BUFFERS
            return pltpu.make_async_copy(ybuf.at[slot], ys_hbm.at[rows_of(g)], sem_out.at[slot])
        return copy_in, copy_out

    @pl.when(e == 0)
    def _():
        for xs_hbm, ys_hbm, xbuf, ybuf, sem_in, sem_out, _, _, g_total in segments:
            copy_in, _ = pipeline(xs_hbm, ys_hbm, xbuf, ybuf, sem_in, sem_out)
            for g in range(MOE_LOOKAHEAD):
                @pl.when(g < g_total)
                def _():
                    copy_in(g).start(priority=TILE_DMA_PRIORITY)

    @pl.when(segments[0][7] + segments[1][7] > 0)
    def _():
        wgu_s[:, 0:hidden] = wg_ref[0].astype(BF16)
        wgu_s[:, hidden:2 * hidden] = wu_ref[0].astype(BF16)
        wd_s[...] = wd_ref[0].astype(BF16)

    def expert_rows(xbuf, ybuf, slot, valid):
        row = lax.broadcasted_iota(I32, (MOE_TILE, half), 0)
        x_lo, x_hi = _unpack_bf16_pair(jnp.where(row < valid, xbuf[slot], jnp.uint32(0)))
        ab = (jnp.dot(x_lo.astype(BF16), wgu_s[0:half, :], preferred_element_type=F32)
              + jnp.dot(x_hi.astype(BF16), wgu_s[half:2 * half, :], preferred_element_type=F32))
        a = ab[:, 0:hidden]
        he = a * _sigmoid(a) * ab[:, hidden:2 * hidden]
        y = jnp.dot(he.astype(BF16), wd_s[...], preferred_element_type=F32)
        ybuf[slot] = _pack_bf16_pair(y[:, 0:half], y[:, half:2 * half])

    for xs_hbm, ys_hbm, xbuf, ybuf, sem_in, sem_out, start, count, g_total in segments:
        copy_in, copy_out = pipeline(xs_hbm, ys_hbm, xbuf, ybuf, sem_in, sem_out)
        g_first = start // MOE_TILE
        n_tiles = (count + MOE_TILE - 1) // MOE_TILE

        def tiles(t, width, copy_in=copy_in, copy_out=copy_out, xbuf=xbuf, ybuf=ybuf,
                  g_first=g_first, count=count, g_total=g_total):
            gs = [g_first + t + i for i in range(width)]
            for g in gs:
                @pl.when(g + MOE_LOOKAHEAD < g_total)
                def _():
                    copy_in(g + MOE_LOOKAHEAD).start(priority=TILE_DMA_PRIORITY)
            for g in gs:
                copy_in(g).wait()

                @pl.when(g >= MOE_BUFFERS)
                def _():
                    copy_out(g - MOE_BUFFERS).wait()
            for i, g in enumerate(gs):
                expert_rows(xbuf, ybuf, g % MOE_BUFFERS, count - (t + i) * MOE_TILE)
            for g in gs:
                copy_out(g).start(priority=TILE_DMA_PRIORITY)

        def pair(p, carry, tiles=tiles):
            tiles(MOE_UNROLL * p, MOE_UNROLL)
            return carry

        lax.fori_loop(0, n_tiles // MOE_UNROLL, pair, 0)

        def single(r, carry, tiles=tiles, n_tiles=n_tiles):
            tiles(n_tiles // MOE_UNROLL * MOE_UNROLL + r, 1)
            return carry

        lax.fori_loop(0, n_tiles % MOE_UNROLL, single, 0)

        @pl.when(e == last)
        def _(copy_out=copy_out, g_total=g_total):
            for j in range(1, MOE_BUFFERS + 1):
                @pl.when(g_total >= j)
                def _():
                    copy_out(g_total - j).wait()


def _moe_call(groups, w_g, w_u, w_d):
    (xs0, starts0, cnt0), (xs1, starts1, cnt1) = groups
    half = xs0.shape[1]
    n_experts, d_model, hidden = w_g.shape

    def tiles_total(starts, cnt):
        return ((starts[-1:] + cnt[-1:] + MOE_TILE - 1) // MOE_TILE).astype(I32)

    wspec = lambda shape: pl.BlockSpec(shape, lambda e, *_: (e, 0, 0))
    tile_bufs = [pltpu.VMEM((MOE_BUFFERS, MOE_TILE, half), U32)] * 4
    grid_spec = pltpu.PrefetchScalarGridSpec(
        num_scalar_prefetch=6,
        grid=(n_experts,),
        in_specs=[
            pl.BlockSpec(memory_space=pl.ANY),
            pl.BlockSpec(memory_space=pl.ANY),
            wspec((1, d_model, hidden)),
            wspec((1, d_model, hidden)),
            wspec((1, hidden, d_model)),
        ],
        out_specs=[pl.BlockSpec(memory_space=pl.ANY), pl.BlockSpec(memory_space=pl.ANY)],
        scratch_shapes=[
            pltpu.VMEM((d_model, 2 * hidden), BF16),
            pltpu.VMEM((hidden, d_model), BF16),
            *tile_bufs,
            *[pltpu.SemaphoreType.DMA((MOE_BUFFERS,))] * 4,
        ],
    )
    n_rows = xs0.shape[0] + xs1.shape[0]
    cost = pl.CostEstimate(flops=n_rows * 6 * d_model * hidden, transcendentals=n_rows * hidden,
                           bytes_accessed=2 * _nbytes(xs0, xs1) + _nbytes(w_g, w_u, w_d))
    return pl.pallas_call(
        _moe_kernel, grid_spec=grid_spec, cost_estimate=cost,
        out_shape=[jax.ShapeDtypeStruct(xs0.shape, U32), jax.ShapeDtypeStruct(xs1.shape, U32)], name="moe_experts",
        compiler_params=pltpu.CompilerParams(
            dimension_semantics=("arbitrary",), vmem_limit_bytes=VMEM_LIMIT),
    )(starts0, cnt0, tiles_total(starts0, cnt0), starts1, cnt1, tiles_total(starts1, cnt1),
      xs0, xs1, w_g, w_u, w_d)


def _combine_kernel(x1_ref, y0_ref, y1_ref, rw_ref, gf_ref, *rest):
    out_ref = rest[-1]
    k, _, tr = rw_ref.shape
    for j in range(k):
        rs = pl.ds(j * tr, tr)
        w_rows = jnp.concatenate([rw_ref[j], jnp.zeros((LANES - rw_ref.shape[1], tr), F32)], axis=0)
        w_cols = w_rows.T
        w0, w1 = w_cols[:, 0:1], w_cols[:, 1:2]
        a_lo, a_hi = _unpack_bf16_pair(y0_ref[0, rs, :])
        b_lo, b_hi = _unpack_bf16_pair(y1_ref[0, rs, :])
        moe = jnp.concatenate([w0 * a_lo + w1 * b_lo, w0 * a_hi + w1 * b_hi], axis=-1)
        out_ref[rs, :] = _rms(x1_ref[rs, :] + moe, gf_ref[...])


def _combine_call(x1, rw, row0, n, yg, gf, out_rows, out_row0, prev_out=None):
    t, d_model = x1.shape
    tr = rw.shape[-1]
    half = yg.shape[-1]
    assert t % tr == 0 and row0 % tr == 0 and n % tr == 0 and rw.shape == (t // tr, EXPERTS_PER_GROUP, tr)
    assert yg.shape == (2, n, half) and out_row0 % tr == 0
    k = max(c for c in range(1, COMBINE_BLOCKS + 1) if all(v % (c * tr) == 0 for v in (t, row0, n, out_row0)))
    ct = k * tr
    off = row0 // ct
    ooff = out_row0 // ct
    in_specs = [
        pl.BlockSpec((ct, d_model), lambda i: (off + i, 0)),
        pl.BlockSpec((1, ct, half), lambda i: (0, i, 0)),
        pl.BlockSpec((1, ct, half), lambda i: (1, i, 0)),
        pl.BlockSpec((k, EXPERTS_PER_GROUP, tr), lambda i: (off + i, 0, 0)),
        pl.BlockSpec((1, d_model), lambda i: (0, 0)),
    ]
    args = [x1, yg, yg, rw, gf]
    aliases = {}
    if prev_out is not None:
        in_specs.append(pl.BlockSpec(memory_space=pl.ANY))
        args.append(prev_out)
        aliases = {len(args) - 1: 0}
    return pl.pallas_call(
        _combine_kernel,
        grid=(n // ct,),
        in_specs=in_specs,
        out_specs=pl.BlockSpec((ct, d_model), lambda i: (ooff + i, 0)),
        out_shape=jax.ShapeDtypeStruct((out_rows, d_model), F32), name=f"combine_row{out_row0}_of{out_rows}",
        cost_estimate=pl.CostEstimate(flops=8 * n * d_model, transcendentals=n,
                                      bytes_accessed=2 * n * d_model * 4 + _nbytes(yg) + n * 4 * EXPERTS_PER_GROUP),
        input_output_aliases=aliases,
        compiler_params=pltpu.CompilerParams(
            dimension_semantics=("arbitrary",), vmem_limit_bytes=VMEM_LIMIT),
    )(*args)


def _route(streams, after=None):
    tokens = [h2.shape[0] for h2, _, _ in streams]
    counts = [cnt[:, 0].astype(I32) for _, _, cnt in streams]
    total = sum(counts)
    padded = ((total + MOE_TILE - 1) // MOE_TILE) * MOE_TILE
    starts = (jnp.cumsum(padded) - padded).astype(I32)
    experts = jnp.arange(N_EXPERTS, dtype=I32)[None, :, None]
    pos, base = [], starts
    for (_, ri, _), t, cnt in zip(streams, tokens, counts):
        ri = jnp.moveaxis(ri, 2, 0).reshape(ri.shape[2], t)
        first_row = jnp.sum(jnp.where(ri[0:2, None, :] == experts, base[None, :, None], 0), axis=1)
        pos.append(ri[2:4] + first_row)
        base = base + cnt
    pos = jnp.concatenate(pos, axis=1)
    n_rows = ((2 * sum(tokens) + N_EXPERTS * (MOE_TILE - 1)) // MOE_TILE) * MOE_TILE
    xs_sorted, ready = _sc_dispatch([h2 for h2, _, _ in streams], pos[0], pos[1], n_rows, after)
    return (xs_sorted, starts, total), pos, ready


def _gather_tokens(ys_sorted, pos, t0, n):
    return _sc_gather(ys_sorted, pos[:, t0:t0 + n].reshape(2 * n)).reshape(2, n, ys_sorted.shape[-1])


def _one_layer(xp, xs, s_ret, c_pool, norm1_g, w_in, ret_norm_g, w_pool, pool_scale, w_out, norm2_g,
               w_rg, w_re, w_g, w_u, w_d, final_g, past_len):
    bp, seq, d_model = xp.shape
    bs, dseq, _ = xs.shape
    rw_width = ret_norm_g.shape[-1]
    pw = pool_scale.shape[-1]
    dh = rw_width // RET_HEADS
    half = d_model // 2

    w_r = jnp.concatenate(
        [w_re.T, w_rg.T, jnp.zeros((ROUTER_ROWS - N_EXPERTS - N_EXPERT_GROUPS, d_model), F32)], axis=0)
    wr = w_r.astype(BF16)
    consts = dict(
        g1=norm1_g.reshape(1, d_model), w_in=w_in.astype(BF16), gret=ret_norm_g.reshape(1, rw_width),
        w_pool=w_pool.astype(BF16), pscale=pool_scale.reshape(1, pw), w_out=w_out.astype(BF16),
        g2=norm2_g.reshape(1, d_model), wr=wr)

    gf = final_g.reshape(1, d_model)

    ts = bs * dseq
    b_lead = bp - 1
    t_lead, t_rest = b_lead * seq, (bp - b_lead) * seq
    zeros = lambda nb: (jnp.zeros((nb, RET_HEADS, dh, dh), F32), jnp.zeros((nb, HIST_ROWS, pw), F32))
    h0s = jnp.pad(c_pool, ((0, 0), (HIST_ROWS - POOL_HIST, 0), (0, 0)))
    prompt_tile = dict(bb=1, tl=PROMPT_TILE, chunk=min(RET_CHUNK, seq))
    sample_tile = dict(bb=bs, tl=dseq, chunk=min(RET_CHUNK, dseq))
    prompt_tables = _layer_tables(0, seq, dh, **prompt_tile)
    sample_tables = _layer_tables(past_len, dseq, dh, **sample_tile)

    def stream(layer_out, t):
        x1, h2, ri, rw, st, hist, cnt = layer_out
        return dict(x1=x1.reshape(t, d_model), route=(h2.reshape(t, half), ri, cnt),
                    rw=rw.reshape(-1, EXPERTS_PER_GROUP, rw.shape[-1]), st=st, hist=hist)

    pa = stream(_layer_call(xp, 0, b_lead, *zeros(b_lead), 0, consts, prompt_tables, gf, **prompt_tile), t_lead)
    group0, pos0, ready0 = _route([pa["route"]])
    pb = stream(_layer_call(xp, b_lead, bp - b_lead, *zeros(bp - b_lead), 0, consts, prompt_tables, ready0,
                            **prompt_tile), t_rest)
    sm = stream(_layer_call(xs, 0, bs, s_ret, h0s, past_len, consts, sample_tables, pb["route"][2], **sample_tile),
                ts)
    group1, pos1, _ = _route([pb["route"], sm["route"]], after=group0[0])
    ys0, ys1 = _moe_call([group0, group1], w_g, w_u, w_d)

    tp = bp * seq
    yp = None
    for st, ys, pos, out0, t_stream, sizes in ((pb, ys1, pos1, t_lead, t_rest, REST_CHUNK_ROWS),
                                               (pa, ys0, pos0, 0, t_lead, LEAD_CHUNK_ROWS)):
        row0 = 0
        for size in sizes:
            n = min(size, t_stream - row0)
            if n > 0:
                yp = _combine_call(st["x1"], st["rw"], row0, n, _gather_tokens(ys, pos, row0, n), gf,
                                   tp, out0 + row0, prev_out=yp)
                row0 += n
        assert row0 == t_stream
    ysm = _combine_call(sm["x1"], sm["rw"], 0, ts, _gather_tokens(ys1, pos1, t_rest, ts), gf, ts, 0)
    st_p = jnp.concatenate([pa["st"], pb["st"]], axis=0)
    hist_p = jnp.concatenate([pa["hist"], pb["hist"]], axis=0)
    return (yp.reshape(bp, seq, d_model), ysm.reshape(bs, dseq, d_model),
            st_p, hist_p[:, HIST_ROWS - POOL_HIST:], sm["st"], sm["hist"][:, HIST_ROWS - POOL_HIST:])


def kernel(x_prompt, x_sample, state_ret, cache_pool, norm1_g, w_in, ret_norm_g, w_pool, pool_scale, w_out,
           norm2_g, w_router_group, w_router_expert, w_exp_gate, w_exp_up, w_exp_down, final_norm_g):
    depth = w_in.shape[0]
    assert depth == 1, "the final RMSNorm is fused into the layer's combine kernel"
    assert x_prompt.shape[0] >= 2 and x_prompt.shape[1] % PROMPT_TILE == 0
    yp, ys, s_p, h_p, s_s, h_s = _one_layer(
        x_prompt, x_sample, state_ret[0], cache_pool[0], norm1_g[0], w_in[0], ret_norm_g[0], w_pool[0],
        pool_scale[0], w_out[0], norm2_g[0], w_router_group[0], w_router_expert[0],
        w_exp_gate[0], w_exp_up[0], w_exp_down[0], final_norm_g, PAST_LEN)
    return (yp, ys, s_p[None], h_p[None], s_s[None], h_s[None])
```

```python
import functools

import jax
import jax.numpy as jnp
from jax import lax
from jax.experimental import pallas as pl
from jax.experimental.pallas import tpu as pltpu
from jax.experimental.pallas import tpu_sc as plsc

F32 = jnp.float32
BF16 = jnp.bfloat16
I32 = jnp.int32
U32 = jnp.uint32

EPS = 1e-6
ROPE_BASE = 10000.0
RET_HEADS = 4
POOL_WINDOWS = (2, 4, 8, 16)
POOL_HIST = max(POOL_WINDOWS) - 1
N_EXPERT_GROUPS = 4
EXPERTS_PER_GROUP = 8
N_EXPERTS = N_EXPERT_GROUPS * EXPERTS_PER_GROUP
ROUTER_ROWS = 48
PAST_LEN = 1024

LANES = 128
SUBLANES = 8
HIST_ROWS = 16
MOE_TILE = 256
MOE_BUFFERS = 16
TILE_DMA_PRIORITY = 1
MOE_UNROLL = 2
MOE_LOOKAHEAD = MOE_BUFFERS - MOE_UNROLL
RET_CHUNK = 256
RET_UNROLL_MAX = 16
PREP_ROWS = 512
PROMPT_TILE = 1024
REST_CHUNK_ROWS = (1 << 30,)
LEAD_CHUNK_ROWS = (8192, 1 << 30)
SC_UNIT = 32
V7X_VMEM_BYTES = 64 * 1024 * 1024
VMEM_LIMIT = V7X_VMEM_BYTES * 7 // 8


def _nbytes(*arrays):
    return sum(a.size * a.dtype.itemsize for a in arrays)


def _rms(x, g):
    return x * lax.rsqrt(jnp.mean(x * x, axis=-1, keepdims=True) + EPS) * g


def _sigmoid(x):
    return 1.0 / (1.0 + jnp.exp(-x))


def _pack_bf16_pair(lo, hi):
    lo_b = lax.bitcast_convert_type(lo.astype(BF16).astype(F32), U32)
    hi_b = lax.bitcast_convert_type(hi.astype(BF16).astype(F32), U32)
    return hi_b | (lo_b >> 16)


def _unpack_bf16_pair(p):
    lo = lax.bitcast_convert_type(p << 16, F32)
    hi = lax.bitcast_convert_type(p & jnp.uint32(0xFFFF0000), F32)
    return lo, hi


def _layer_kernel(dc_ref, x_ref, s0_ref, h0_ref, rb_ref, rc_ref, rs_ref, rcs_ref, rss_ref,
                  dintra_ref, dq_ref, dk_ref,
                  g1_ref, win_ref, gret_ref, wpool_ref, pscale_ref, wout_ref, g2_ref,
                  wr_ref, tri_ref, after_ref,
                  x1_ref, h2_ref, ri_ref, rw_ref, st_ref, hist_ref, cnt_ref,
                  ue_ref, q_ref, k_ref, v_ref, gate_ref, a_ref,
                  *, bb, tl, chunk, pos0):
    b_idx = pl.program_id(0)
    l_idx = pl.program_id(1)
    rows = bb * tl
    d_model = x_ref.shape[-1]
    rw_width = q_ref.shape[-1]
    dh = rw_width // RET_HEADS
    pw = ue_ref.shape[-1]
    gw = pw // len(POOL_WINDOWS)
    n_chunks = tl // chunk

    @pl.when(l_idx == 0)
    def _():
        st_ref[...] = s0_ref[...]
        ue_ref[:, 0:HIST_ROWS, :] = h0_ref[...]

    @pl.when((l_idx == 0) & (b_idx == 0))
    def _():
        cnt_ref[...] = jnp.zeros_like(cnt_ref)

    cos_b = rb_ref[0, 0:1, :]
    sin_b = rb_ref[0, 1:2, :]
    cosf = cos_b * rc_ref[...] - sin_b * rs_ref[...]
    sinf = sin_b * rcs_ref[...] + cos_b * rss_ref[...]
    k_scale = dh ** -0.5
    n_blocks = max(1, tl // PREP_ROWS) if bb == 1 else 1
    block = rows // n_blocks
    for blk in range(n_blocks):
        rs = slice(blk * block, (blk + 1) * block)
        xb = x_ref[0, rs, :] if bb == 1 else x_ref[...].reshape(rows, d_model)
        hb = _rms(xb, g1_ref[...]).astype(BF16)
        proj = jnp.dot(hb, win_ref[...], preferred_element_type=F32)

        def rotate(a):
            if bb == 1:
                return a * cosf[rs] + pltpu.roll(a, dh // 2, 1) * sinf[rs]
            return (a.reshape(bb, tl, dh) * cosf[None]
                    + pltpu.roll(a, dh // 2, 1).reshape(bb, tl, dh) * sinf[None]).reshape(rows, dh)

        for hh in range(RET_HEADS):
            cs = slice(hh * dh, (hh + 1) * dh)
            q_ref[rs, cs] = rotate(proj[:, hh * dh:(hh + 1) * dh]).astype(BF16)
            k_ref[rs, cs] = rotate(proj[:, rw_width + hh * dh:rw_width + (hh + 1) * dh]) * k_scale
        v_ref[rs, :] = proj[:, 2 * rw_width:3 * rw_width].astype(BF16)
        gate_ref[rs, :] = proj[:, 3 * rw_width:4 * rw_width]
        u = proj[:, 4 * rw_width:4 * rw_width + pw]
        if bb == 1:
            ue_ref[0, HIST_ROWS + blk * block:HIST_ROWS + (blk + 1) * block, :] = u
        else:
            ue_ref[:, HIST_ROWS:HIST_ROWS + tl, :] = u.reshape(bb, tl, pw)

    def ret_block(b, c):
        r0 = b * tl + c * chunk
        if not isinstance(r0, int):
            r0 = pl.multiple_of(r0, chunk)
        rr = pl.ds(r0, chunk)
        for hh in range(RET_HEADS):
            cs = slice(hh * dh, (hh + 1) * dh)
            qc = q_ref[rr, cs]
            kf = k_ref[rr, cs]
            vc = v_ref[rr, cs]
            s_old = st_ref[b, hh]
            sc = lax.dot_general(qc, kf.astype(BF16), (((1,), (1,)), ((), ())),
                                 preferred_element_type=F32) * dintra_ref[hh]
            o = (jnp.dot(sc.astype(BF16), vc, preferred_element_type=F32)
                 + dq_ref[hh] * jnp.dot(qc, s_old.astype(BF16), preferred_element_type=F32))
            kd = (kf * dk_ref[hh]).astype(BF16)
            s_new = dc_ref[hh] * s_old + lax.dot_general(
                kd, vc, (((0,), (0,)), ((), ())), preferred_element_type=F32)
            st_ref[b, hh] = s_new
            oc = o - jnp.mean(o, axis=-1, keepdims=True)
            var = jnp.mean(oc * oc, axis=-1, keepdims=True)
            y = oc * lax.rsqrt(var + EPS) * gret_ref[:, cs]
            g = gate_ref[rr, cs]
            a_ref[rr, cs] = (g * _sigmoid(g) * y).astype(BF16)

    if bb * n_chunks <= RET_UNROLL_MAX:
        for b in range(bb):
            for c in range(n_chunks):
                ret_block(b, c)
    else:
        def body(i, carry):
            ret_block(i // n_chunks, i % n_chunks)
            return carry
        lax.fori_loop(0, bb * n_chunks, body, 0)

    nt = (((1,), (1,)), ((), ()))
    neg = jnp.float32(-jnp.inf)
    big = jnp.float32(1e9)
    sub = lax.broadcasted_iota(I32, (EXPERTS_PER_GROUP, block), 0).astype(F32)
    eid = lax.broadcasted_iota(I32, (N_EXPERTS, block), 0).astype(F32)
    for blk in range(n_blocks):
        lo = blk * block
        rs = slice(lo, lo + block)

        if bb == 1:
            pos = pos0 + l_idx * tl + lo + lax.broadcasted_iota(I32, (block, 1), 0)
            window = lambda off, cs: ue_ref[0, HIST_ROWS + lo - off:HIST_ROWS + lo - off + block, cs]
        else:
            pos = pos0 + l_idx * tl + lax.broadcasted_iota(I32, (1, tl, 1), 1)
            window = lambda off, cs: ue_ref[:, HIST_ROWS - off:HIST_ROWS - off + tl, cs]
        for gi, w in enumerate(POOL_WINDOWS):
            cs = slice(gi * gw, (gi + 1) * gw)
            u_g = window(0, cs)
            acc = u_g
            for j in range(1, w):
                acc = acc + window(j, cs)
            inv_cnt = 1.0 / jnp.minimum(pos + 1, w).astype(F32)
            p = (acc * inv_cnt - u_g).reshape(block, gw)
            z = jnp.dot(p.astype(BF16), wpool_ref[gi], preferred_element_type=F32) * pscale_ref[:, cs]
            a_ref[rs, rw_width + gi * gw:rw_width + (gi + 1) * gw] = z.astype(BF16)

        xb = x_ref[0, rs, :] if bb == 1 else x_ref[...].reshape(rows, d_model)
        x1 = xb + jnp.dot(a_ref[rs, :], wout_ref[...], preferred_element_type=F32)
        h2 = _rms(x1, g2_ref[...])
        h2_packed = _pack_bf16_pair(h2[:, 0:d_model // 2], h2[:, d_model // 2:])
        if bb == 1:
            x1_ref[0, rs, :] = x1
            h2_ref[0, rs, :] = h2_packed
        else:
            x1_ref[...] = x1.reshape(bb, tl, d_model)
            h2_ref[...] = h2_packed.reshape(bb, tl, d_model // 2)

        lt = lax.dot_general(wr_ref[...], h2.astype(BF16), nt, preferred_element_type=F32)
        gl = jnp.where(sub < N_EXPERT_GROUPS, lt[N_EXPERTS:N_EXPERTS + EXPERTS_PER_GROUP], neg)
        gmax = jnp.max(gl, axis=0, keepdims=True)
        gidx = jnp.min(jnp.where(gl == gmax, sub, big), axis=0, keepdims=True)
        p_sel = 1.0 / jnp.sum(jnp.exp(gl - gmax), axis=0, keepdims=True)
        el = lt[0:EXPERTS_PER_GROUP]
        for g in range(1, N_EXPERT_GROUPS):
            el = jnp.where(gidx == g, lt[g * EXPERTS_PER_GROUP:(g + 1) * EXPERTS_PER_GROUP], el)
        m1 = jnp.max(el, axis=0, keepdims=True)
        t1 = jnp.min(jnp.where(el == m1, sub, big), axis=0, keepdims=True)
        el2 = jnp.where(sub == t1, neg, el)
        m2 = jnp.max(el2, axis=0, keepdims=True)
        t2 = jnp.min(jnp.where(el2 == m2, sub, big), axis=0, keepdims=True)
        e2 = jnp.exp(m2 - m1)
        w1 = p_sel / (1.0 + e2)
        w2 = p_sel * e2 / (1.0 + e2)
        i1 = gidx * EXPERTS_PER_GROUP + t1
        i2 = gidx * EXPERTS_PER_GROUP + t2

        hit1 = eid == i1
        hit2 = eid == i2
        onehot = (hit1 | hit2).astype(BF16)
        before = jnp.dot(onehot, tri_ref[...], preferred_element_type=F32) + cnt_ref[...]
        r1 = jnp.sum(jnp.where(hit1, before, 0.0), axis=0, keepdims=True)
        r2 = jnp.sum(jnp.where(hit2, before, 0.0), axis=0, keepdims=True)
        cnt_ref[...] = cnt_ref[...] + jnp.sum(onehot.astype(F32), axis=1, keepdims=True)

        ri = jnp.where(sub == 0, i1, jnp.where(sub == 1, i2, jnp.where(sub == 2, r1, jnp.where(sub == 3, r2, 0.0))))
        ri_ref[0, 0, :, rs] = ri.astype(I32)
        rw_ref[0, 0, :, rs] = jnp.where(sub == 0, w1, jnp.where(sub == 1, w2, 0.0))

    tail = ue_ref[:, tl:tl + HIST_ROWS, :]
    ue_ref[:, 0:HIST_ROWS, :] = tail
    hist_ref[...] = tail


def _rope_tables(pos0, seq, tl, dh):
    half = dh // 2
    inv = ROPE_BASE ** (-jnp.arange(half, dtype=F32) / half)
    ang_t = jnp.arange(tl, dtype=F32)[:, None] * inv[None, :]
    ang_b = (pos0 + tl * jnp.arange(seq // tl)).astype(F32)[:, None] * inv[None, :]
    dup = lambda a: jnp.concatenate([a, a], axis=-1)
    sgn = lambda a: jnp.concatenate([-a, a], axis=-1)
    base = jnp.stack([dup(jnp.cos(ang_b)), dup(jnp.sin(ang_b))], axis=1)
    base = jnp.pad(base, ((0, 0), (0, SUBLANES - base.shape[1]), (0, 0)))
    cos_t, sin_t = jnp.cos(ang_t), jnp.sin(ang_t)
    return base, dup(cos_t), dup(sin_t), sgn(cos_t), sgn(sin_t)


def _block_rows(bb, tl):
    return bb * tl // (max(1, tl // PREP_ROWS) if bb == 1 else 1)


def _layer_tables(pos0, seq, dh, *, bb, tl, chunk):
    lg = jnp.log1p(-jnp.exp2(-5.0 - jnp.arange(RET_HEADS, dtype=F32)))
    idx = jnp.arange(chunk, dtype=F32)
    diff = idx[:, None] - idx[None, :]
    d_intra = jnp.where(diff[None] >= 0, jnp.exp(jnp.maximum(diff, 0.0)[None] * lg[:, None, None]), 0.0)
    d_q = jnp.broadcast_to(jnp.exp((idx + 1.0)[None, :] * lg[:, None])[:, :, None], (RET_HEADS, chunk, dh))
    d_k = jnp.broadcast_to(jnp.exp((chunk - 1.0 - idx)[None, :] * lg[:, None])[:, :, None], (RET_HEADS, chunk, dh))
    d_c = jnp.exp(chunk * lg)
    block = _block_rows(bb, tl)
    tri = jnp.triu(jnp.ones((block, block), BF16), 1)
    return dict(rope=_rope_tables(pos0, seq, tl, dh), d_intra=d_intra, d_q=d_q, d_k=d_k, d_c=d_c, tri=tri)


def _layer_call(x, b0, nb, s0, h0, pos0, consts, tables, after, *, bb, tl, chunk):
    _, seq, d_model = x.shape
    bsz = nb
    blk0 = b0 // bb
    rows = bb * tl
    rw_width = consts["gret"].shape[-1]
    pw = consts["pscale"].shape[-1]
    dh = rw_width // RET_HEADS
    block = _block_rows(bb, tl)
    rope, d_intra, d_q, d_k, d_c, tri = (tables[k] for k in ("rope", "d_intra", "d_q", "d_k", "d_c", "tri"))

    const2 = lambda b, l, *_: (0, 0)
    const3 = lambda b, l, *_: (0, 0, 0)
    grid_spec = pltpu.PrefetchScalarGridSpec(
        num_scalar_prefetch=0,
        grid=(bsz // bb, seq // tl),
        in_specs=[
            pl.BlockSpec(memory_space=pltpu.SMEM),
            pl.BlockSpec((bb, tl, d_model), lambda b, l: (blk0 + b, l, 0)),
            pl.BlockSpec((bb, RET_HEADS, dh, dh), lambda b, l: (b, 0, 0, 0)),
            pl.BlockSpec((bb, HIST_ROWS, pw), lambda b, l: (b, 0, 0)),
            pl.BlockSpec((1, SUBLANES, dh), lambda b, l: (l, 0, 0)),
            pl.BlockSpec((tl, dh), const2),
            pl.BlockSpec((tl, dh), const2),
            pl.BlockSpec((tl, dh), const2),
            pl.BlockSpec((tl, dh), const2),
            pl.BlockSpec((RET_HEADS, chunk, chunk), const3),
            pl.BlockSpec((RET_HEADS, chunk, dh), const3),
            pl.BlockSpec((RET_HEADS, chunk, dh), const3),
            pl.BlockSpec((1, d_model), const2),
            pl.BlockSpec(consts["w_in"].shape, const2),
            pl.BlockSpec((1, rw_width), const2),
            pl.BlockSpec(consts["w_pool"].shape, const3),
            pl.BlockSpec((1, pw), const2),
            pl.BlockSpec(consts["w_out"].shape, const2),
            pl.BlockSpec((1, d_model), const2),
            pl.BlockSpec((ROUTER_ROWS, d_model), const2),
            pl.BlockSpec((block, block), const2),
            pl.BlockSpec(memory_space=pl.ANY),
        ],
        out_specs=[
            pl.BlockSpec((bb, tl, d_model), lambda b, l: (b, l, 0)),
            pl.BlockSpec((bb, tl, d_model // 2), lambda b, l: (b, l, 0)),
            pl.BlockSpec((1, 1, EXPERTS_PER_GROUP, rows), lambda b, l: (b, l, 0, 0)),
            pl.BlockSpec((1, 1, EXPERTS_PER_GROUP, rows), lambda b, l: (b, l, 0, 0)),
            pl.BlockSpec((bb, RET_HEADS, dh, dh), lambda b, l: (b, 0, 0, 0)),
            pl.BlockSpec((bb, HIST_ROWS, pw), lambda b, l: (b, 0, 0)),
            pl.BlockSpec((N_EXPERTS, block), const2),
        ],
        scratch_shapes=[
            pltpu.VMEM((bb, HIST_ROWS + tl, pw), F32),
            pltpu.VMEM((rows, rw_width), BF16),
            pltpu.VMEM((rows, rw_width), F32),
            pltpu.VMEM((rows, rw_width), BF16),
            pltpu.VMEM((rows, rw_width), F32),
            pltpu.VMEM((rows, d_model), BF16),
        ],
    )
    out_shape = [
        jax.ShapeDtypeStruct((bsz, seq, d_model), F32),
        jax.ShapeDtypeStruct((bsz, seq, d_model // 2), U32),
        jax.ShapeDtypeStruct((bsz // bb, seq // tl, EXPERTS_PER_GROUP, rows), I32),
        jax.ShapeDtypeStruct((bsz // bb, seq // tl, EXPERTS_PER_GROUP, rows), F32),
        jax.ShapeDtypeStruct((bsz, RET_HEADS, dh, dh), F32),
        jax.ShapeDtypeStruct((bsz, HIST_ROWS, pw), F32),
        jax.ShapeDtypeStruct((N_EXPERTS, block), F32),
    ]
    kern = functools.partial(_layer_kernel, bb=bb, tl=tl, chunk=chunk, pos0=pos0)
    operands = (d_c, x, s0, h0, *rope, d_intra, d_q, d_k,
                consts["g1"], consts["w_in"], consts["gret"], consts["w_pool"], consts["pscale"],
                consts["w_out"], consts["g2"], consts["wr"], tri, after)
    n_tok = bsz * seq
    mm_flops_per_token = 2 * (d_model * consts["w_in"].shape[1] + d_model * d_model + d_model * ROUTER_ROWS
                              + pw * pw // 4
                              + rw_width * (2 * chunk + 2 * dh) + N_EXPERTS * block)
    cost = pl.CostEstimate(
        flops=n_tok * mm_flops_per_token, transcendentals=n_tok * (rw_width + 2 * N_EXPERT_GROUPS),
        bytes_accessed=_nbytes(*operands) - _nbytes(x, after) + n_tok * d_model * 4 + _nbytes(*out_shape))
    return pl.pallas_call(
        kern, grid_spec=grid_spec, out_shape=out_shape, name=f"layer_pos{pos0}_b{b0}", cost_estimate=cost,
        compiler_params=pltpu.CompilerParams(
            dimension_semantics=("arbitrary", "arbitrary"), vmem_limit_bytes=VMEM_LIMIT),
    )(*operands)


def _sc_partition(n_units):
    info = plsc.get_sparse_core_info()
    nc, nw = info.num_cores, info.num_cores * info.num_subcores
    upw = -(-n_units // nw)
    upw += upw % 2
    return nc, nw, upw


def _units_by_worker(idx, n_units, upw, nw):
    idx = jnp.pad(idx.reshape(n_units, SC_UNIT), ((0, nw * upw - n_units), (0, 0)))
    return idx.reshape(upw, nw, SC_UNIT).transpose(1, 0, 2)


def _sc_dispatch(srcs, idx0, idx1, n_out_rows, after=None):
    assert 1 <= len(srcs) <= 2
    d = srcs[0].shape[1]
    dtype = srcs[0].dtype
    assert all(src.shape[0] % SC_UNIT == 0 for src in srcs)
    units_a = srcs[0].shape[0] // SC_UNIT
    n_units = sum(src.shape[0] for src in srcs) // SC_UNIT
    nc, nw, upw = _sc_partition(n_units)
    idx0 = _units_by_worker(idx0, n_units, upw, nw)
    idx1 = _units_by_worker(idx1, n_units, upw, nw)
    mesh = plsc.VectorSubcoreMesh(core_axis_name="c", subcore_axis_name="s")
    dma = pltpu.SemaphoreType.DMA
    extra = [] if after is None else [after]

    moved = n_units * SC_UNIT * d * jnp.dtype(dtype).itemsize
    @functools.partial(
        pl.kernel, mesh=mesh,
        cost_estimate=pl.CostEstimate(flops=0, transcendentals=0, bytes_accessed=3 * moved + _nbytes(idx0, idx1)),
        out_type=jax.ShapeDtypeStruct((n_out_rows, d), dtype),
        scratch_types=[
            pltpu.VMEM((upw, SC_UNIT), I32),
            pltpu.VMEM((upw, SC_UNIT), I32),
            pltpu.VMEM((SC_UNIT, d), dtype),
            pltpu.VMEM((SC_UNIT, d), dtype),
            dma, dma, dma, dma, dma, dma,
        ],
    )
    def k(*refs):
        src_hbm = refs[:len(srcs)]
        i0_hbm, i1_hbm, out_hbm, i0_v, i1_v, rows0, rows1, l0, l1, p0, p1, q0, q1 = refs[len(srcs) + len(extra):]
        wid = lax.axis_index("s") * nc + lax.axis_index("c")
        pltpu.sync_copy(i0_hbm.at[wid], i0_v)
        pltpu.sync_copy(i1_hbm.at[wid], i1_v)
        rows, lsem, psem, qsem = (rows0, rows1), (l0, l1), (p0, p1), (q0, q1)

        def live(j):
            return j * nw + wid < n_units

        def load(j, b, op):
            unit = j * nw + wid

            @pl.when(live(j) & (unit < units_a))
            def _():
                op(pltpu.make_async_copy(
                    src_hbm[0].at[pl.ds(pl.multiple_of(unit * SC_UNIT, 8), SC_UNIT)], rows[b], lsem[b]))

            if len(srcs) == 2:
                @pl.when(live(j) & (unit >= units_a))
                def _():
                    op(pltpu.make_async_copy(
                        src_hbm[1].at[pl.ds(pl.multiple_of((unit - units_a) * SC_UNIT, 8), SC_UNIT)],
                        rows[b], lsem[b]))

        def scatter(j, b, op):
            @pl.when(live(j))
            def _():
                op(pltpu.make_async_copy(rows[b], out_hbm.at[i0_v.at[j]], psem[b]))
                op(pltpu.make_async_copy(rows[b], out_hbm.at[i1_v.at[j]], qsem[b]))

        start = lambda c: c.start()
        wait = lambda c: c.wait()
        load(0, 0, start)

        @pl.loop(0, upw, step=2)
        def _(j):
            @pl.when(j > 0)
            def _():
                scatter(j - 1, 1, wait)
            load(j + 1, 1, start)
            load(j, 0, wait)
            scatter(j, 0, start)
            scatter(j, 0, wait)

            @pl.when(j + 2 < upw)
            def _():
                load(j + 2, 0, start)
            load(j + 1, 1, wait)
            scatter(j + 1, 1, start)

        scatter(upw - 1, 1, wait)

    return k(*srcs, *extra, idx0, idx1), idx1


def _sc_gather(table, idx):
    n = idx.shape[0]
    d = table.shape[1]
    assert n % SC_UNIT == 0
    n_units = n // SC_UNIT
    nc, nw, upw = _sc_partition(n_units)
    idx = _units_by_worker(idx, n_units, upw, nw)
    mesh = plsc.VectorSubcoreMesh(core_axis_name="c", subcore_axis_name="s")
    dma = pltpu.SemaphoreType.DMA

    @functools.partial(
        pl.kernel, mesh=mesh,
        cost_estimate=pl.CostEstimate(flops=0, transcendentals=0,
                                      bytes_accessed=2 * n * d * table.dtype.itemsize + _nbytes(idx)),
        out_type=jax.ShapeDtypeStruct((n, d), table.dtype),
        scratch_types=[
            pltpu.VMEM((upw, SC_UNIT), I32),
            pltpu.VMEM((SC_UNIT, d), table.dtype),
            pltpu.VMEM((SC_UNIT, d), table.dtype),
            dma, dma, dma, dma,
        ],
    )
    def k(t_hbm, i_hbm, out_hbm, i_v, rows0, rows1, g0, g1, w0, w1):
        wid = lax.axis_index("s") * nc + lax.axis_index("c")
        pltpu.sync_copy(i_hbm.at[wid], i_v)
        rows, gsem, wsem = (rows0, rows1), (g0, g1), (w0, w1)

        def live(j):
            return j * nw + wid < n_units

        def gather(j, b, op):
            @pl.when(live(j))
            def _():
                op(pltpu.make_async_copy(t_hbm.at[i_v.at[j]], rows[b], gsem[b]))

        def write(j, b, op):
            @pl.when(live(j))
            def _():
                op(pltpu.make_async_copy(
                    rows[b], out_hbm.at[pl.ds(pl.multiple_of((j * nw + wid) * SC_UNIT, 8), SC_UNIT)], wsem[b]))

        start = lambda c: c.start()
        wait = lambda c: c.wait()
        gather(0, 0, start)

        @pl.loop(0, upw, step=2)
        def _(j):
            @pl.when(j > 0)
            def _():
                write(j - 1, 1, wait)
            gather(j + 1, 1, start)
            gather(j, 0, wait)
            write(j, 0, start)
            write(j, 0, wait)

            @pl.when(j + 2 < upw)
            def _():
                gather(j + 2, 0, start)
            gather(j + 1, 1, wait)
            write(j + 1, 1, start)

        write(upw - 1, 1, wait)

    return k(table, idx)


def _moe_kernel(start0_ref, count0_ref, gtot0_ref, start1_ref, count1_ref, gtot1_ref,
                xs0_hbm, xs1_hbm, wg_ref, wu_ref, wd_ref, ys0_hbm, ys1_hbm,
                wgu_s, wd_s, xbuf0, ybuf0, xbuf1, ybuf1, sem_in0, sem_out0, sem_in1, sem_out1):
    e = pl.program_id(0)
    last = pl.num_programs(0) - 1
    hidden = wd_s.shape[0]
    half = xbuf0.shape[-1]
    segments = (
        (xs0_hbm, ys0_hbm, xbuf0, ybuf0, sem_in0, sem_out0, start0_ref[e], count0_ref[e], gtot0_ref[0]),
        (xs1_hbm, ys1_hbm, xbuf1, ybuf1, sem_in1, sem_out1, start1_ref[e], count1_ref[e], gtot1_ref[0]),
    )

    def rows_of(g):
        return pl.ds(pl.multiple_of(g * MOE_TILE, MOE_TILE), MOE_TILE)

    def pipeline(xs_hbm, ys_hbm, xbuf, ybuf, sem_in, sem_out):
        def copy_in(g):
            slot = g % MOE_BUFFERS
            return pltpu.make_async_copy(xs_hbm.at[rows_of(g)], xbuf.at[slot], sem_in.at[slot])

        def copy_out(g):
            slot = g % MOE_BUFFERS
            return pltpu.make_async_copy(ybuf.at[slot], ys_hbm.at[rows_of(g)], sem_out.at[slot])
        return copy_in, copy_out

    @pl.when(e == 0)
    def _():
        for xs_hbm, ys_hbm, xbuf, ybuf, sem_in, sem_out, _, _, g_total in segments:
            copy_in, _ = pipeline(xs_hbm, ys_hbm, xbuf, ybuf, sem_in, sem_out)
            for g in range(MOE_LOOKAHEAD):
                @pl.when(g < g_total)
                def _():
                    copy_in(g).start(priority=TILE_DMA_PRIORITY)

    @pl.when(segments[0][7] + segments[1][7] > 0)
    def _():
        wgu_s[:, 0:hidden] = wg_ref[0].astype(BF16)
        wgu_s[:, hidden:2 * hidden] = wu_ref[0].astype(BF16)
        wd_s[...] = wd_ref[0].astype(BF16)

    def expert_rows(xbuf, ybuf, slot, valid):
        row = lax.broadcasted_iota(I32, (MOE_TILE, half), 0)
        x_lo, x_hi = _unpack_bf16_pair(jnp.where(row < valid, xbuf[slot], jnp.uint32(0)))
        ab = (jnp.dot(x_lo.astype(BF16), wgu_s[0:half, :], preferred_element_type=F32)
              + jnp.dot(x_hi.astype(BF16), wgu_s[half:2 * half, :], preferred_element_type=F32))
        a = ab[:, 0:hidden]
        he = a * _sigmoid(a) * ab[:, hidden:2 * hidden]
        y = jnp.dot(he.astype(BF16), wd_s[...], preferred_element_type=F32)
        ybuf[slot] = _pack_bf16_pair(y[:, 0:half], y[:, half:2 * half])

    for xs_hbm, ys_hbm, xbuf, ybuf, sem_in, sem_out, start, count, g_total in segments:
        copy_in, copy_out = pipeline(xs_hbm, ys_hbm, xbuf, ybuf, sem_in, sem_out)
        g_first = start // MOE_TILE
        n_tiles = (count + MOE_TILE - 1) // MOE_TILE

        def tiles(t, width, copy_in=copy_in, copy_out=copy_out, xbuf=xbuf, ybuf=ybuf,
                  g_first=g_first, count=count, g_total=g_total):
            gs = [g_first + t + i for i in range(width)]
            for g in gs:
                @pl.when(g + MOE_LOOKAHEAD < g_total)
                def _():
                    copy_in(g + MOE_LOOKAHEAD).start(priority=TILE_DMA_PRIORITY)
            for g in gs:
                copy_in(g).wait()

                @pl.when(g >= MOE_BUFFERS)
                def _():
                    copy_out(g - MOE_BUFFERS).wait()
            for i, g in enumerate(gs):
                expert_rows(xbuf, ybuf, g % MOE_BUFFERS, count - (t + i) * MOE_TILE)
            for g in gs:
                copy_out(g).start(priority=TILE_DMA_PRIORITY)

        def pair(p, carry, tiles=tiles):
            tiles(MOE_UNROLL * p, MOE_UNROLL)
            return carry

        lax.fori_loop(0, n_tiles // MOE_UNROLL, pair, 0)

        def single(r, carry, tiles=tiles, n_tiles=n_tiles):
            tiles(n_tiles // MOE_UNROLL * MOE_UNROLL + r, 1)
            return carry

        lax.fori_loop(0, n_tiles % MOE_UNROLL, single, 0)

        @pl.when(e == last)
        def _(copy_out=copy_out, g_total=g_total):
            for j in range(1, MOE_BUFFERS + 1):
                @pl.when(g_total >= j)
                def _():
                    copy_out(g_total - j).wait()


def _moe_call(groups, w_g, w_u, w_d):
    (xs0, starts0, cnt0), (xs1, starts1, cnt1) = groups
    half = xs0.shape[1]
    n_experts, d_model, hidden = w_g.shape

    def tiles_total(starts, cnt):
        return ((starts[-1:] + cnt[-1:] + MOE_TILE - 1) // MOE_TILE).astype(I32)

    wspec = lambda shape: pl.BlockSpec(shape, lambda e, *_: (e, 0, 0))
    tile_bufs = [pltpu.VMEM((MOE_BUFFERS, MOE_TILE, half), U32)] * 4
    grid_spec = pltpu.PrefetchScalarGridSpec(
        num_scalar_prefetch=6,
        grid=(n_experts,),
        in_specs=[
            pl.BlockSpec(memory_space=pl.ANY),
            pl.BlockSpec(memory_space=pl.ANY),
            wspec((1, d_model, hidden)),
            wspec((1, d_model, hidden)),
            wspec((1, hidden, d_model)),
        ],
        out_specs=[pl.BlockSpec(memory_space=pl.ANY), pl.BlockSpec(memory_space=pl.ANY)],
        scratch_shapes=[
            pltpu.VMEM((d_model, 2 * hidden), BF16),
            pltpu.VMEM((hidden, d_model), BF16),
            *tile_bufs,
            *[pltpu.SemaphoreType.DMA((MOE_BUFFERS,))] * 4,
        ],
    )
    n_rows = xs0.shape[0] + xs1.shape[0]
    cost = pl.CostEstimate(flops=n_rows * 6 * d_model * hidden, transcendentals=n_rows * hidden,
                           bytes_accessed=2 * _nbytes(xs0, xs1) + _nbytes(w_g, w_u, w_d))
    return pl.pallas_call(
        _moe_kernel, grid_spec=grid_spec, cost_estimate=cost,
        out_shape=[jax.ShapeDtypeStruct(xs0.shape, U32), jax.ShapeDtypeStruct(xs1.shape, U32)], name="moe_experts",
        compiler_params=pltpu.CompilerParams(
            dimension_semantics=("arbitrary",), vmem_limit_bytes=VMEM_LIMIT),
    )(starts0, cnt0, tiles_total(starts0, cnt0), starts1, cnt1, tiles_total(starts1, cnt1),
      xs0, xs1, w_g, w_u, w_d)


def _combine_kernel(x1_ref, y0_ref, y1_ref, rw_ref, gf_ref, *rest):
    out_ref = rest[-1]
    tr = x1_ref.shape[0]
    w_rows = jnp.concatenate([rw_ref[0], jnp.zeros((LANES - rw_ref.shape[1], tr), F32)], axis=0)
    w_cols = w_rows.T
    w0, w1 = w_cols[:, 0:1], w_cols[:, 1:2]
    a_lo, a_hi = _unpack_bf16_pair(y0_ref[0])
    b_lo, b_hi = _unpack_bf16_pair(y1_ref[0])
    moe = jnp.concatenate([w0 * a_lo + w1 * b_lo, w0 * a_hi + w1 * b_hi], axis=-1)
    out_ref[...] = _rms(x1_ref[...] + moe, gf_ref[...])


def _combine_call(x1, rw, row0, n, yg, gf, out_rows, out_row0, prev_out=None):
    t, d_model = x1.shape
    tr = rw.shape[-1]
    half = yg.shape[-1]
    assert t % tr == 0 and row0 % tr == 0 and n % tr == 0 and rw.shape == (t // tr, EXPERTS_PER_GROUP, tr)
    assert yg.shape == (2, n, half) and out_row0 % tr == 0
    off = row0 // tr
    ooff = out_row0 // tr
    in_specs = [
        pl.BlockSpec((tr, d_model), lambda i: (off + i, 0)),
        pl.BlockSpec((1, tr, half), lambda i: (0, i, 0)),
        pl.BlockSpec((1, tr, half), lambda i: (1, i, 0)),
        pl.BlockSpec((1, EXPERTS_PER_GROUP, tr), lambda i: (off + i, 0, 0)),
        pl.BlockSpec((1, d_model), lambda i: (0, 0)),
    ]
    args = [x1, yg, yg, rw, gf]
    aliases = {}
    if prev_out is not None:
        in_specs.append(pl.BlockSpec(memory_space=pl.ANY))
        args.append(prev_out)
        aliases = {len(args) - 1: 0}
    return pl.pallas_call(
        _combine_kernel,
        grid=(n // tr,),
        in_specs=in_specs,
        out_specs=pl.BlockSpec((tr, d_model), lambda i: (ooff + i, 0)),
        out_shape=jax.ShapeDtypeStruct((out_rows, d_model), F32), name=f"combine_row{out_row0}_of{out_rows}",
        cost_estimate=pl.CostEstimate(flops=8 * n * d_model, transcendentals=n,
                                      bytes_accessed=2 * n * d_model * 4 + _nbytes(yg) + n * 4 * EXPERTS_PER_GROUP),
        input_output_aliases=aliases,
        compiler_params=pltpu.CompilerParams(
            dimension_semantics=("arbitrary",), vmem_limit_bytes=VMEM_LIMIT),
    )(*args)


def _route(streams, after=None):
    tokens = [h2.shape[0] for h2, _, _ in streams]
    counts = [cnt[:, 0].astype(I32) for _, _, cnt in streams]
    total = sum(counts)
    padded = ((total + MOE_TILE - 1) // MOE_TILE) * MOE_TILE
    starts = (jnp.cumsum(padded) - padded).astype(I32)
    experts = jnp.arange(N_EXPERTS, dtype=I32)[None, :, None]
    pos, base = [], starts
    for (_, ri, _), t, cnt in zip(streams, tokens, counts):
        ri = jnp.moveaxis(ri, 2, 0).reshape(ri.shape[2], t)
        first_row = jnp.sum(jnp.where(ri[0:2, None, :] == experts, base[None, :, None], 0), axis=1)
        pos.append(ri[2:4] + first_row)
        base = base + cnt
    pos = jnp.concatenate(pos, axis=1)
    n_rows = ((2 * sum(tokens) + N_EXPERTS * (MOE_TILE - 1)) // MOE_TILE) * MOE_TILE
    xs_sorted, ready = _sc_dispatch([h2 for h2, _, _ in streams], pos[0], pos[1], n_rows, after)
    return (xs_sorted, starts, total), pos, ready


def _gather_tokens(ys_sorted, pos, t0, n):
    return _sc_gather(ys_sorted, pos[:, t0:t0 + n].reshape(2 * n)).reshape(2, n, ys_sorted.shape[-1])


def _one_layer(xp, xs, s_ret, c_pool, norm1_g, w_in, ret_norm_g, w_pool, pool_scale, w_out, norm2_g,
               w_rg, w_re, w_g, w_u, w_d, final_g, past_len):
    bp, seq, d_model = xp.shape
    bs, dseq, _ = xs.shape
    rw_width = ret_norm_g.shape[-1]
    pw = pool_scale.shape[-1]
    dh = rw_width // RET_HEADS
    half = d_model // 2

    w_r = jnp.concatenate(
        [w_re.T, w_rg.T, jnp.zeros((ROUTER_ROWS - N_EXPERTS - N_EXPERT_GROUPS, d_model), F32)], axis=0)
    wr = w_r.astype(BF16)
    consts = dict(
        g1=norm1_g.reshape(1, d_model), w_in=w_in.astype(BF16), gret=ret_norm_g.reshape(1, rw_width),
        w_pool=w_pool.astype(BF16), pscale=pool_scale.reshape(1, pw), w_out=w_out.astype(BF16),
        g2=norm2_g.reshape(1, d_model), wr=wr)

    gf = final_g.reshape(1, d_model)

    ts = bs * dseq
    b_lead = bp - 1
    t_lead, t_rest = b_lead * seq, (bp - b_lead) * seq
    zeros = lambda nb: (jnp.zeros((nb, RET_HEADS, dh, dh), F32), jnp.zeros((nb, HIST_ROWS, pw), F32))
    h0s = jnp.pad(c_pool, ((0, 0), (HIST_ROWS - POOL_HIST, 0), (0, 0)))
    prompt_tile = dict(bb=1, tl=PROMPT_TILE, chunk=min(RET_CHUNK, seq))
    sample_tile = dict(bb=bs, tl=dseq, chunk=min(RET_CHUNK, dseq))
    prompt_tables, sample_tables, zero_state = lax.optimization_barrier(
        (_layer_tables(0, seq, dh, **prompt_tile), _layer_tables(past_len, dseq, dh, **sample_tile),
         zeros(max(b_lead, bp - b_lead))))

    def stream(layer_out, t):
        x1, h2, ri, rw, st, hist, cnt = layer_out
        return dict(x1=x1.reshape(t, d_model), route=(h2.reshape(t, half), ri, cnt),
                    rw=rw.reshape(-1, EXPERTS_PER_GROUP, rw.shape[-1]), st=st, hist=hist)

    pa = stream(_layer_call(xp, 0, b_lead, *zero_state, 0, consts, prompt_tables, gf, **prompt_tile), t_lead)
    group0, pos0, ready0 = _route([pa["route"]])
    pb = stream(_layer_call(xp, b_lead, bp - b_lead, *zero_state, 0, consts, prompt_tables, ready0,
                            **prompt_tile), t_rest)
    sm = stream(_layer_call(xs, 0, bs, s_ret, h0s, past_len, consts, sample_tables, pb["route"][2], **sample_tile),
                ts)
    group1, pos1, _ = _route([pb["route"], sm["route"]], after=group0[0])
    ys0, ys1 = _moe_call([group0, group1], w_g, w_u, w_d)

    tp = bp * seq
    yp = None
    for st, ys, pos, out0, t_stream, sizes in ((pb, ys1, pos1, t_lead, t_rest, REST_CHUNK_ROWS),
                                               (pa, ys0, pos0, 0, t_lead, LEAD_CHUNK_ROWS)):
        row0 = 0
        for size in sizes:
            n = min(size, t_stream - row0)
            if n > 0:
                yp = _combine_call(st["x1"], st["rw"], row0, n, _gather_tokens(ys, pos, row0, n), gf,
                                   tp, out0 + row0, prev_out=yp)
                row0 += n
        assert row0 == t_stream
    ysm = _combine_call(sm["x1"], sm["rw"], 0, ts, _gather_tokens(ys1, pos1, t_rest, ts), gf, ts, 0)
    st_p = jnp.concatenate([pa["st"], pb["st"]], axis=0)
    hist_p = jnp.concatenate([pa["hist"], pb["hist"]], axis=0)
    return (yp.reshape(bp, seq, d_model), ysm.reshape(bs, dseq, d_model),
            st_p, hist_p[:, HIST_ROWS - POOL_HIST:], sm["st"], sm["hist"][:, HIST_ROWS - POOL_HIST:])


def kernel(x_prompt, x_sample, state_ret, cache_pool, norm1_g, w_in, ret_norm_g, w_pool, pool_scale, w_out,
           norm2_g, w_router_group, w_router_expert, w_exp_gate, w_exp_up, w_exp_down, final_norm_g):
    depth = w_in.shape[0]
    assert depth == 1, "the final RMSNorm is fused into the layer's combine kernel"
    assert x_prompt.shape[0] >= 2 and x_prompt.shape[1] % PROMPT_TILE == 0
    yp, ys, s_p, h_p, s_s, h_s = _one_layer(
        x_prompt, x_sample, state_ret[0], cache_pool[0], norm1_g[0], w_in[0], ret_norm_g[0], w_pool[0],
        pool_scale[0], w_out[0], norm2_g[0], w_router_group[0], w_router_expert[0],
        w_exp_gate[0], w_exp_up[0], w_exp_down[0], final_norm_g, PAST_LEN)
    return (yp, ys, s_p[None], h_p[None], s_s[None], h_s[None])
```
